```python
import math
import jax
import jax.numpy as jnp
from jax import lax
import numpy as np

D_MODEL = 1024
BATCH = 8
SEQ = 8192
DEPTH = 2

N_MIXERS = 4
GROUP_WIDTH = D_MODEL // N_MIXERS
HEAD_DIM = 64
GROUP_HEADS = GROUP_WIDTH // HEAD_DIM
D_FF = 2816
SHORT_CONV = 4
CONF_KERNEL = 31
CONF_GROUPS = 4
GDN_CHUNK = 64
Q_BLOCK = 128
N_MEM = 256
MEM_HEADS = 4
MEM_HEAD_DIM = D_MODEL // MEM_HEADS
DN_ALPHA = float((2 * DEPTH) ** 0.25)
DN_INIT = float((8 * DEPTH) ** -0.25)
LN_EPS = 1e-5
RMS_EPS = 1e-6
L2_EPS = 1e-6
NEG_BIG = -1e30
IN_SPLITS = (3 * GROUP_WIDTH,
             GROUP_WIDTH,
             GROUP_HEADS,
             GROUP_HEADS,
             3 * GROUP_WIDTH,
             GROUP_HEADS,
             2 * GROUP_WIDTH,
             3 * GROUP_WIDTH)
IN_WIDTH = sum(IN_SPLITS)
IN_OFFSETS = tuple(int(o) for o in np.cumsum(IN_SPLITS)[:-1])

kernel_name = 'hybrid_headgroup_gdn_fox_conv_stickbreak'


def layer_norm(x, g, b):
    xf = x.astype(jnp.float32)
    mu = jnp.mean(xf, axis=-1, keepdims=True)
    var = jnp.mean(jnp.square(xf - mu), axis=-1, keepdims=True)
    y = (xf - mu) * lax.rsqrt(var + LN_EPS) * g.astype(jnp.float32) + b.astype(jnp.float32)
    return y.astype(x.dtype)


def swiglu(x, w_gate, w_up, w_down):
    return (jax.nn.silu(x @ w_gate) * (x @ w_up)) @ w_down


def causal_depthwise_conv(x, w):
    width, ch = w.shape
    xp = jnp.pad(x, ((0, 0), (width - 1, 0), (0, 0)))
    return lax.conv_general_dilated(xp, w[:, None, :].astype(x.dtype), window_strides=(1,), padding='VALID',
                                    dimension_numbers=('NWC', 'WIO', 'NWC'), feature_group_count=ch)


def to_heads(t):
    b, s, _ = t.shape
    return t.reshape(b, s, GROUP_HEADS, HEAD_DIM).transpose(0, 2, 1, 3)


def from_heads(t):
    b, h, s, d = t.shape
    return t.transpose(0, 2, 1, 3).reshape(b, s, h * d)


def l2_normalize(t):
    tf = t.astype(jnp.float32)
    return tf * lax.rsqrt(jnp.sum(tf * tf, axis=-1, keepdims=True) + L2_EPS)


def query_blocks(t):
    b, h, s = t.shape[:3]
    return jnp.moveaxis(t.reshape(b, h, s // Q_BLOCK, Q_BLOCK, *t.shape[3:]), 2, 0)


def merge_blocks(t):
    nb, b, h, qb, d = t.shape
    return jnp.moveaxis(t, 0, 2).reshape(b, h, nb * qb, d)


def gated_delta_rule_chunked(q, k, v, g, beta):
    f32 = jnp.float32
    b, h, s, dk = q.shape
    dv = v.shape[-1]
    c = GDN_CHUNK
    n = s // c
    q = q.astype(f32).reshape(b, h, n, c, dk) * (dk ** -0.5)
    k = k.astype(f32).reshape(b, h, n, c, dk)
    v = v.astype(f32).reshape(b, h, n, c, dv)
    beta = beta.astype(f32).reshape(b, h, n, c)
    g = jnp.cumsum(g.astype(f32).reshape(b, h, n, c), axis=-1)
    k_beta = k * beta[..., None]
    v_beta = v * beta[..., None]
    lower_incl = jnp.tril(jnp.ones((c, c), dtype=bool))
    strict_lower = jnp.tril(jnp.ones((c, c), dtype=bool), -1)
    decay = jnp.exp(jnp.where(lower_incl, g[..., :, None] - g[..., None, :], -jnp.inf))
    lkk = jnp.where(strict_lower, jnp.einsum('bhncd,bhnsd->bhncs', k_beta, k) * decay, 0.0)
    eye = jnp.eye(c, dtype=f32)
    t_inv = lax.linalg.triangular_solve(eye + lkk, jnp.broadcast_to(eye, lkk.shape), left_side=True, lower=True)
    u = jnp.einsum('bhncs,bhnsv->bhncv', t_inv, v_beta)
    w = jnp.einsum('bhncs,bhnsd->bhncd', t_inv, k_beta * jnp.exp(g)[..., None])
    a_qk = jnp.einsum('bhncd,bhnsd->bhncs', q, k) * decay
    g_last = g[..., -1]
    q_dec = q * jnp.exp(g)[..., None]
    k_dec = k * jnp.exp(g_last[..., None] - g)[..., None]

    def chunk_step(state, inp):
        q_c, k_c, u_c, w_c, a_c, gl_c = inp
        v_new = u_c - jnp.einsum('bhcd,bhdv->bhcv', w_c, state)
        o_c = jnp.einsum('bhcd,bhdv->bhcv', q_c, state) + jnp.einsum('bhcs,bhsv->bhcv', a_c, v_new)
        state = state * jnp.exp(gl_c)[..., None, None] + jnp.einsum('bhcd,bhcv->bhdv', k_c, v_new)
        return state, o_c

    xs = tuple(jnp.moveaxis(t, 2, 0) for t in (q_dec, k_dec, u, w, a_qk, g_last))
    state0 = jnp.zeros((b, h, dk, dv), f32)
    _, o = lax.scan(chunk_step, state0, xs)
    return jnp.moveaxis(o, 0, 2).reshape(b, h, s, dv)


def forgetting_attention(q, k, v, log_f):
    s_len, d = q.shape[2], q.shape[3]
    scale = d ** -0.5
    cum = jnp.cumsum(log_f.astype(jnp.float32), axis=-1)
    key_pos = jnp.arange(s_len)
    starts = jnp.arange(s_len // Q_BLOCK) * Q_BLOCK

    def block(args):
        q_blk, cum_blk, start = args
        logits = jnp.einsum('bhqd,bhkd->bhqk', q_blk, k).astype(jnp.float32) * scale
        logits = logits + cum_blk[..., :, None] - cum[..., None, :]
        causal = key_pos[None, :] <= (start + jnp.arange(Q_BLOCK))[:, None]
        probs = jax.nn.softmax(jnp.where(causal, logits, NEG_BIG), axis=-1)
        return jnp.einsum('bhqk,bhkd->bhqd', probs.astype(v.dtype), v)

    return merge_blocks(lax.map(block, (query_blocks(q), query_blocks(cum), starts)))


def stick_breaking_attention(q, k, v):
    s_len, d = q.shape[2], q.shape[3]
    scale = d ** -0.5
    key_pos = jnp.arange(s_len)
    starts = jnp.arange(s_len // Q_BLOCK) * Q_BLOCK

    def block(args):
        q_blk, start = args
        z = jnp.einsum('bhqd,bhkd->bhqk', q_blk, k).astype(jnp.float32) * scale
        strict = key_pos[None, :] < (start + jnp.arange(Q_BLOCK))[:, None]
        log_keep = jnp.where(strict, jax.nn.log_sigmoid(-z), 0.0)
        log_rest = lax.cumsum(log_keep, axis=3, reverse=True) - log_keep
        weights = jnp.where(strict, jnp.exp(jax.nn.log_sigmoid(z) + log_rest), 0.0)
        return jnp.einsum('bhqk,bhkd->bhqd', weights.astype(v.dtype), v)

    return merge_blocks(lax.map(block, (query_blocks(q), starts)))


def channel_group_norm(h, g, b):
    bsz, s, ch = h.shape
    hf = h.astype(jnp.float32).reshape(bsz, s, CONF_GROUPS, ch // CONF_GROUPS)
    mu = jnp.mean(hf, axis=-1, keepdims=True)
    var = jnp.mean(jnp.square(hf - mu), axis=-1, keepdims=True)
    hn = ((hf - mu) * lax.rsqrt(var + LN_EPS)).reshape(bsz, s, ch)
    return (hn * g.astype(jnp.float32) + b.astype(jnp.float32)).astype(h.dtype)


def parallel_head_group_mixers(x, w_in, gdn_conv_w, gdn_a_log, gdn_dt_bias, gdn_norm_g, fox_b_f,
                               conf_dw_w, conf_dw_b, conf_norm_g, conf_norm_b, w_out):
    f32 = jnp.float32
    proj = x @ w_in
    gdn_qkv, gdn_z, gdn_a, gdn_b, fox_qkv, fox_f, conf_glu, sb_qkv = jnp.split(proj, IN_OFFSETS, axis=-1)

    qkv = jax.nn.silu(causal_depthwise_conv(gdn_qkv, gdn_conv_w))
    q_a, k_a, v_a = (to_heads(t) for t in jnp.split(qkv, 3, axis=-1))
    beta = jax.nn.sigmoid(gdn_b.astype(f32)).transpose(0, 2, 1)
    log_decay = (-jnp.exp(gdn_a_log.astype(f32)) *
                 jax.nn.softplus(gdn_a.astype(f32) + gdn_dt_bias.astype(f32))).transpose(0, 2, 1)
    o_a = gated_delta_rule_chunked(l2_normalize(q_a), l2_normalize(k_a), v_a, log_decay, beta)
    o_a = o_a * lax.rsqrt(jnp.mean(o_a * o_a, axis=-1, keepdims=True) + RMS_EPS) * gdn_norm_g.astype(f32)
    y_a = (from_heads(o_a) * jax.nn.silu(gdn_z.astype(f32))).astype(x.dtype)

    q_b, k_b, v_b = (to_heads(t) for t in jnp.split(fox_qkv, 3, axis=-1))
    log_f = jax.nn.log_sigmoid(fox_f.astype(f32) + fox_b_f.astype(f32)).transpose(0, 2, 1)
    y_b = from_heads(forgetting_attention(q_b, k_b, v_b, log_f)).astype(x.dtype)

    val, gate = jnp.split(conf_glu, 2, axis=-1)
    c = causal_depthwise_conv(val * jax.nn.sigmoid(gate), conf_dw_w) + conf_dw_b
    y_c = jax.nn.silu(channel_group_norm(c, conf_norm_g, conf_norm_b)).astype(x.dtype)

    q_d, k_d, v_d = (to_heads(t) for t in jnp.split(sb_qkv, 3, axis=-1))
    y_d = from_heads(stick_breaking_attention(q_d, k_d, v_d)).astype(x.dtype)

    return jnp.concatenate([y_a, y_b, y_c, y_d], axis=-1) @ w_out


def memory_cross_attention(x, mem, w_q, w_kv, w_o):
    b, s, _ = x.shape
    m = mem.shape[1]
    q = (x @ w_q).reshape(b, s, MEM_HEADS, MEM_HEAD_DIM)
    k, v = jnp.split(mem @ w_kv, 2, axis=-1)
    k = k.reshape(b, m, MEM_HEADS, MEM_HEAD_DIM)
    v = v.reshape(b, m, MEM_HEADS, MEM_HEAD_DIM)
    scores = jnp.einsum('bthd,bmhd->bhtm', q, k).astype(jnp.float32) * (MEM_HEAD_DIM ** -0.5)
    probs = jax.nn.softmax(scores, axis=-1).astype(v.dtype)
    out = jnp.einsum('bhtm,bmhd->bthd', probs, v).reshape(b, s, D_MODEL)
    return out @ w_o


def _fwd_setup_inputs(seed: int = 0) -> dict:
    key = jax.random.key(seed)
    keys = iter(jax.random.split(key, 40))
    f32 = jnp.float32
    L = DEPTH

    def normal(shape, scale):
        return jax.random.normal(next(keys), shape, f32) * scale

    def gain(shape):
        return 1.0 + normal(shape, 0.02)

    dt = jnp.exp(jax.random.uniform(next(keys), (L, GROUP_HEADS), f32, math.log(1e-3), math.log(1e-1)))
    gdn_dt_bias = dt + jnp.log(-jnp.expm1(-dt))
    gdn_a_log = jnp.log(jax.random.uniform(next(keys), (L, GROUP_HEADS), f32, 1.0, 16.0))
    d_in = D_MODEL ** -0.5
    return {
        'x': normal((BATCH, SEQ, D_MODEL), 1.0),
        'mem': normal((BATCH, N_MEM, D_MODEL), 1.0),
        'ffn1_w_gate': normal((L, D_MODEL, D_FF), d_in),
        'ffn1_w_up': normal((L, D_MODEL, D_FF), d_in),
        'ffn1_w_down': normal((L, D_FF, D_MODEL), DN_INIT * D_FF ** -0.5),
        'ln_ffn1_g': gain((L, D_MODEL)),
        'ln_ffn1_b': normal((L, D_MODEL), 0.02),
        'w_in': normal((L, D_MODEL, IN_WIDTH), d_in),
        'gdn_conv_w': normal((L, SHORT_CONV, 3 * GROUP_WIDTH), SHORT_CONV ** -0.5),
        'gdn_a_log': gdn_a_log,
        'gdn_dt_bias': gdn_dt_bias,
        'gdn_norm_g': gain((L, HEAD_DIM)),
        'fox_b_f': 3.0 + normal((L, GROUP_HEADS), 0.1),
        'conf_dw_w': normal((L, CONF_KERNEL, GROUP_WIDTH), CONF_KERNEL ** -0.5),
        'conf_dw_b': normal((L, GROUP_WIDTH), 0.02),
        'conf_norm_g': gain((L, GROUP_WIDTH)),
        'conf_norm_b': normal((L, GROUP_WIDTH), 0.02),
        'w_out': normal((L, D_MODEL, D_MODEL), DN_INIT * d_in),
        'ln_mix_g': gain((L, D_MODEL)),
        'ln_mix_b': normal((L, D_MODEL), 0.02),
        'mem_w_q': normal((L, D_MODEL, D_MODEL), d_in),
        'mem_w_kv': normal((L, D_MODEL, 2 * D_MODEL), d_in),
        'mem_w_o': normal((L, D_MODEL, D_MODEL), DN_INIT * d_in),
        'ln_mem_g': gain((L, D_MODEL)),
        'ln_mem_b': normal((L, D_MODEL), 0.02),
        'ffn2_w_gate': normal((L, D_MODEL, D_FF), d_in),
        'ffn2_w_up': normal((L, D_MODEL, D_FF), d_in),
        'ffn2_w_down': normal((L, D_FF, D_MODEL), DN_INIT * D_FF ** -0.5),
        'ln_ffn2_g': gain((L, D_MODEL)),
        'ln_ffn2_b': normal((L, D_MODEL), 0.02),
    }


def _fwd_reference(x, mem, ffn1_w_gate, ffn1_w_up, ffn1_w_down, ln_ffn1_g, ln_ffn1_b,
              w_in, gdn_conv_w, gdn_a_log, gdn_dt_bias, gdn_norm_g, fox_b_f,
              conf_dw_w, conf_dw_b, conf_norm_g, conf_norm_b, w_out, ln_mix_g, ln_mix_b,
              mem_w_q, mem_w_kv, mem_w_o, ln_mem_g, ln_mem_b,
              ffn2_w_gate, ffn2_w_up, ffn2_w_down, ln_ffn2_g, ln_ffn2_b):
    for i in range(DEPTH):
        x = layer_norm(DN_ALPHA * x + 0.5 * swiglu(x, ffn1_w_gate[i], ffn1_w_up[i], ffn1_w_down[i]),
                       ln_ffn1_g[i], ln_ffn1_b[i])
        mix = parallel_head_group_mixers(x, w_in[i], gdn_conv_w[i], gdn_a_log[i], gdn_dt_bias[i], gdn_norm_g[i],
                                         fox_b_f[i], conf_dw_w[i], conf_dw_b[i], conf_norm_g[i], conf_norm_b[i],
                                         w_out[i])
        x = layer_norm(DN_ALPHA * x + mix, ln_mix_g[i], ln_mix_b[i])
        x = layer_norm(DN_ALPHA * x + memory_cross_attention(x, mem, mem_w_q[i], mem_w_kv[i], mem_w_o[i]),
                       ln_mem_g[i], ln_mem_b[i])
        x = layer_norm(DN_ALPHA * x + 0.5 * swiglu(x, ffn2_w_gate[i], ffn2_w_up[i], ffn2_w_down[i]),
                       ln_ffn2_g[i], ln_ffn2_b[i])
    return x


import jax as _jax
import jax.numpy as _jnp

TWIN_FORMAT = 'train_step'
FWD_PARAMS = ['x', 'mem', 'ffn1_w_gate', 'ffn1_w_up', 'ffn1_w_down', 'ln_ffn1_g', 'ln_ffn1_b', 'w_in', 'gdn_conv_w', 'gdn_a_log', 'gdn_dt_bias', 'gdn_norm_g', 'fox_b_f', 'conf_dw_w', 'conf_dw_b', 'conf_norm_g', 'conf_norm_b', 'w_out', 'ln_mix_g', 'ln_mix_b', 'mem_w_q', 'mem_w_kv', 'mem_w_o', 'ln_mem_g', 'ln_mem_b', 'ffn2_w_gate', 'ffn2_w_up', 'ffn2_w_down', 'ln_ffn2_g', 'ln_ffn2_b']
TWIN_WEIGHTS = ['ffn1_w_gate', 'ffn1_w_up', 'ffn1_w_down', 'ln_ffn1_g', 'ln_ffn1_b', 'w_in', 'gdn_conv_w', 'gdn_a_log', 'gdn_dt_bias', 'gdn_norm_g', 'fox_b_f', 'conf_dw_w', 'conf_dw_b', 'conf_norm_g', 'conf_norm_b', 'w_out', 'ln_mix_g', 'ln_mix_b', 'mem_w_q', 'mem_w_kv', 'mem_w_o', 'ln_mem_g', 'ln_mem_b', 'ffn2_w_gate', 'ffn2_w_up', 'ffn2_w_down', 'ln_ffn2_g', 'ln_ffn2_b']
TWIN_DIFF_INPUT = 'x'
TWIN_INPUTS = ['x', 'mem', 'ffn1_w_gate', 'ffn1_w_up', 'ffn1_w_down', 'ln_ffn1_g', 'ln_ffn1_b', 'w_in', 'gdn_conv_w', 'gdn_a_log', 'gdn_dt_bias', 'gdn_norm_g', 'fox_b_f', 'conf_dw_w', 'conf_dw_b', 'conf_norm_g', 'conf_norm_b', 'w_out', 'ln_mix_g', 'ln_mix_b', 'mem_w_q', 'mem_w_kv', 'mem_w_o', 'ln_mem_g', 'ln_mem_b', 'ffn2_w_gate', 'ffn2_w_up', 'ffn2_w_down', 'ln_ffn2_g', 'ln_ffn2_b', 'loss_target', 'm_ffn1_w_gate', 'm_ffn1_w_up', 'm_ffn1_w_down', 'm_ln_ffn1_g', 'm_ln_ffn1_b', 'm_w_in', 'm_gdn_conv_w', 'm_gdn_a_log', 'm_gdn_dt_bias', 'm_gdn_norm_g', 'm_fox_b_f', 'm_conf_dw_w', 'm_conf_dw_b', 'm_conf_norm_g', 'm_conf_norm_b', 'm_w_out', 'm_ln_mix_g', 'm_ln_mix_b', 'm_mem_w_q', 'm_mem_w_kv', 'm_mem_w_o', 'm_ln_mem_g', 'm_ln_mem_b', 'm_ffn2_w_gate', 'm_ffn2_w_up', 'm_ffn2_w_down', 'm_ln_ffn2_g', 'm_ln_ffn2_b', 'v_ffn1_w_gate', 'v_ffn1_w_up', 'v_ffn1_w_down', 'v_ln_ffn1_g', 'v_ln_ffn1_b', 'v_w_in', 'v_gdn_conv_w', 'v_gdn_a_log', 'v_gdn_dt_bias', 'v_gdn_norm_g', 'v_fox_b_f', 'v_conf_dw_w', 'v_conf_dw_b', 'v_conf_norm_g', 'v_conf_norm_b', 'v_w_out', 'v_ln_mix_g', 'v_ln_mix_b', 'v_mem_w_q', 'v_mem_w_kv', 'v_mem_w_o', 'v_ln_mem_g', 'v_ln_mem_b', 'v_ffn2_w_gate', 'v_ffn2_w_up', 'v_ffn2_w_down', 'v_ln_ffn2_g', 'v_ln_ffn2_b']
TWIN_OUTPUTS = ['loss', 'grad_x', 'grad_ffn1_w_gate', 'grad_ffn1_w_up', 'grad_ffn1_w_down', 'grad_ln_ffn1_g', 'grad_ln_ffn1_b', 'grad_w_in', 'grad_gdn_conv_w', 'grad_gdn_a_log', 'grad_gdn_dt_bias', 'grad_gdn_norm_g', 'grad_fox_b_f', 'grad_conf_dw_w', 'grad_conf_dw_b', 'grad_conf_norm_g', 'grad_conf_norm_b', 'grad_w_out', 'grad_ln_mix_g', 'grad_ln_mix_b', 'grad_mem_w_q', 'grad_mem_w_kv', 'grad_mem_w_o', 'grad_ln_mem_g', 'grad_ln_mem_b', 'grad_ffn2_w_gate', 'grad_ffn2_w_up', 'grad_ffn2_w_down', 'grad_ln_ffn2_g', 'grad_ln_ffn2_b', 'delta_ffn1_w_gate', 'delta_ffn1_w_up', 'delta_ffn1_w_down', 'delta_ln_ffn1_g', 'delta_ln_ffn1_b', 'delta_w_in', 'delta_gdn_conv_w', 'delta_gdn_a_log', 'delta_gdn_dt_bias', 'delta_gdn_norm_g', 'delta_fox_b_f', 'delta_conf_dw_w', 'delta_conf_dw_b', 'delta_conf_norm_g', 'delta_conf_norm_b', 'delta_w_out', 'delta_ln_mix_g', 'delta_ln_mix_b', 'delta_mem_w_q', 'delta_mem_w_kv', 'delta_mem_w_o', 'delta_ln_mem_g', 'delta_ln_mem_b', 'delta_ffn2_w_gate', 'delta_ffn2_w_up', 'delta_ffn2_w_down', 'delta_ln_ffn2_g', 'delta_ln_ffn2_b', 'new_m_ffn1_w_gate', 'new_m_ffn1_w_up', 'new_m_ffn1_w_down', 'new_m_ln_ffn1_g', 'new_m_ln_ffn1_b', 'new_m_w_in', 'new_m_gdn_conv_w', 'new_m_gdn_a_log', 'new_m_gdn_dt_bias', 'new_m_gdn_norm_g', 'new_m_fox_b_f', 'new_m_conf_dw_w', 'new_m_conf_dw_b', 'new_m_conf_norm_g', 'new_m_conf_norm_b', 'new_m_w_out', 'new_m_ln_mix_g', 'new_m_ln_mix_b', 'new_m_mem_w_q', 'new_m_mem_w_kv', 'new_m_mem_w_o', 'new_m_ln_mem_g', 'new_m_ln_mem_b', 'new_m_ffn2_w_gate', 'new_m_ffn2_w_up', 'new_m_ffn2_w_down', 'new_m_ln_ffn2_g', 'new_m_ln_ffn2_b', 'new_v_ffn1_w_gate', 'new_v_ffn1_w_up', 'new_v_ffn1_w_down', 'new_v_ln_ffn1_g', 'new_v_ln_ffn1_b', 'new_v_w_in', 'new_v_gdn_conv_w', 'new_v_gdn_a_log', 'new_v_gdn_dt_bias', 'new_v_gdn_norm_g', 'new_v_fox_b_f', 'new_v_conf_dw_w', 'new_v_conf_dw_b', 'new_v_conf_norm_g', 'new_v_conf_norm_b', 'new_v_w_out', 'new_v_ln_mix_g', 'new_v_ln_mix_b', 'new_v_mem_w_q', 'new_v_mem_w_kv', 'new_v_mem_w_o', 'new_v_ln_mem_g', 'new_v_ln_mem_b', 'new_v_ffn2_w_gate', 'new_v_ffn2_w_up', 'new_v_ffn2_w_down', 'new_v_ln_ffn2_g', 'new_v_ln_ffn2_b']
TWIN_LEAF_KINDS = {'loss': 'loss', 'grad_x': 'grad_x', 'grad_ffn1_w_gate': 'grad_w', 'grad_ffn1_w_up': 'grad_w', 'grad_ffn1_w_down': 'grad_w', 'grad_ln_ffn1_g': 'grad_w', 'grad_ln_ffn1_b': 'grad_w', 'grad_w_in': 'grad_w', 'grad_gdn_conv_w': 'grad_w', 'grad_gdn_a_log': 'grad_w', 'grad_gdn_dt_bias': 'grad_w', 'grad_gdn_norm_g': 'grad_w', 'grad_fox_b_f': 'grad_w', 'grad_conf_dw_w': 'grad_w', 'grad_conf_dw_b': 'grad_w', 'grad_conf_norm_g': 'grad_w', 'grad_conf_norm_b': 'grad_w', 'grad_w_out': 'grad_w', 'grad_ln_mix_g': 'grad_w', 'grad_ln_mix_b': 'grad_w', 'grad_mem_w_q': 'grad_w', 'grad_mem_w_kv': 'grad_w', 'grad_mem_w_o': 'grad_w', 'grad_ln_mem_g': 'grad_w', 'grad_ln_mem_b': 'grad_w', 'grad_ffn2_w_gate': 'grad_w', 'grad_ffn2_w_up': 'grad_w', 'grad_ffn2_w_down': 'grad_w', 'grad_ln_ffn2_g': 'grad_w', 'grad_ln_ffn2_b': 'grad_w', 'delta_ffn1_w_gate': 'delta_w', 'delta_ffn1_w_up': 'delta_w', 'delta_ffn1_w_down': 'delta_w', 'delta_ln_ffn1_g': 'delta_w', 'delta_ln_ffn1_b': 'delta_w', 'delta_w_in': 'delta_w', 'delta_gdn_conv_w': 'delta_w', 'delta_gdn_a_log': 'delta_w', 'delta_gdn_dt_bias': 'delta_w', 'delta_gdn_norm_g': 'delta_w', 'delta_fox_b_f': 'delta_w', 'delta_conf_dw_w': 'delta_w', 'delta_conf_dw_b': 'delta_w', 'delta_conf_norm_g': 'delta_w', 'delta_conf_norm_b': 'delta_w', 'delta_w_out': 'delta_w', 'delta_ln_mix_g': 'delta_w', 'delta_ln_mix_b': 'delta_w', 'delta_mem_w_q': 'delta_w', 'delta_mem_w_kv': 'delta_w', 'delta_mem_w_o': 'delta_w', 'delta_ln_mem_g': 'delta_w', 'delta_ln_mem_b': 'delta_w', 'delta_ffn2_w_gate': 'delta_w', 'delta_ffn2_w_up': 'delta_w', 'delta_ffn2_w_down': 'delta_w', 'delta_ln_ffn2_g': 'delta_w', 'delta_ln_ffn2_b': 'delta_w', 'new_m_ffn1_w_gate': 'new_m', 'new_m_ffn1_w_up': 'new_m', 'new_m_ffn1_w_down': 'new_m', 'new_m_ln_ffn1_g': 'new_m', 'new_m_ln_ffn1_b': 'new_m', 'new_m_w_in': 'new_m', 'new_m_gdn_conv_w': 'new_m', 'new_m_gdn_a_log': 'new_m', 'new_m_gdn_dt_bias': 'new_m', 'new_m_gdn_norm_g': 'new_m', 'new_m_fox_b_f': 'new_m', 'new_m_conf_dw_w': 'new_m', 'new_m_conf_dw_b': 'new_m', 'new_m_conf_norm_g': 'new_m', 'new_m_conf_norm_b': 'new_m', 'new_m_w_out': 'new_m', 'new_m_ln_mix_g': 'new_m', 'new_m_ln_mix_b': 'new_m', 'new_m_mem_w_q': 'new_m', 'new_m_mem_w_kv': 'new_m', 'new_m_mem_w_o': 'new_m', 'new_m_ln_mem_g': 'new_m', 'new_m_ln_mem_b': 'new_m', 'new_m_ffn2_w_gate': 'new_m', 'new_m_ffn2_w_up': 'new_m', 'new_m_ffn2_w_down': 'new_m', 'new_m_ln_ffn2_g': 'new_m', 'new_m_ln_ffn2_b': 'new_m', 'new_v_ffn1_w_gate': 'new_v', 'new_v_ffn1_w_up': 'new_v', 'new_v_ffn1_w_down': 'new_v', 'new_v_ln_ffn1_g': 'new_v', 'new_v_ln_ffn1_b': 'new_v', 'new_v_w_in': 'new_v', 'new_v_gdn_conv_w': 'new_v', 'new_v_gdn_a_log': 'new_v', 'new_v_gdn_dt_bias': 'new_v', 'new_v_gdn_norm_g': 'new_v', 'new_v_fox_b_f': 'new_v', 'new_v_conf_dw_w': 'new_v', 'new_v_conf_dw_b': 'new_v', 'new_v_conf_norm_g': 'new_v', 'new_v_conf_norm_b': 'new_v', 'new_v_w_out': 'new_v', 'new_v_ln_mix_g': 'new_v', 'new_v_ln_mix_b': 'new_v', 'new_v_mem_w_q': 'new_v', 'new_v_mem_w_kv': 'new_v', 'new_v_mem_w_o': 'new_v', 'new_v_ln_mem_g': 'new_v', 'new_v_ln_mem_b': 'new_v', 'new_v_ffn2_w_gate': 'new_v', 'new_v_ffn2_w_up': 'new_v', 'new_v_ffn2_w_down': 'new_v', 'new_v_ln_ffn2_g': 'new_v', 'new_v_ln_ffn2_b': 'new_v'}


def _forward(args):
    return _fwd_reference(*[args[k] for k in FWD_PARAMS])


def _output_shape():
    def fwd():
        inp = _fwd_setup_inputs(0)
        return _fwd_reference(*[inp[k] for k in FWD_PARAMS])
    out = _jax.eval_shape(fwd)
    return out.shape, out.dtype

N_MICROBATCH = 1
ADAM_LR = 0.001
ADAM_B1 = 0.9
ADAM_B2 = 0.999
ADAM_EPS = 1e-08
ADAM_WD = 0.01
ADAM_STEP = 10
PER_EXAMPLE_BATCH_AXIS = {'x': 0, 'mem': 0, 'loss_target': 0}
SHARED_INPUTS = []
_WEIGHT_DTYPES = {'ffn1_w_gate': _jnp.float32, 'ffn1_w_up': _jnp.float32, 'ffn1_w_down': _jnp.float32, 'ln_ffn1_g': _jnp.float32, 'ln_ffn1_b': _jnp.float32, 'w_in': _jnp.float32, 'gdn_conv_w': _jnp.float32, 'gdn_a_log': _jnp.float32, 'gdn_dt_bias': _jnp.float32, 'gdn_norm_g': _jnp.float32, 'fox_b_f': _jnp.float32, 'conf_dw_w': _jnp.float32, 'conf_dw_b': _jnp.float32, 'conf_norm_g': _jnp.float32, 'conf_norm_b': _jnp.float32, 'w_out': _jnp.float32, 'ln_mix_g': _jnp.float32, 'ln_mix_b': _jnp.float32, 'mem_w_q': _jnp.float32, 'mem_w_kv': _jnp.float32, 'mem_w_o': _jnp.float32, 'ln_mem_g': _jnp.float32, 'ln_mem_b': _jnp.float32, 'ffn2_w_gate': _jnp.float32, 'ffn2_w_up': _jnp.float32, 'ffn2_w_down': _jnp.float32, 'ln_ffn2_g': _jnp.float32, 'ln_ffn2_b': _jnp.float32}
MOMENT_SCALE = {'ffn1_w_gate': 1.699479e-02, 'ffn1_w_up': 1.647085e-02, 'ffn1_w_down': 5.469157e-02, 'ln_ffn1_g': 2.165135e+00, 'ln_ffn1_b': 1.093919e+00, 'w_in': 4.061518e-02, 'gdn_conv_w': 5.021419e-02, 'gdn_a_log': 3.322209e-01, 'gdn_dt_bias': 3.222226e-01, 'gdn_norm_g': 1.220584e-01, 'fox_b_f': 1.022855e-01, 'conf_dw_w': 5.656039e-02, 'conf_dw_b': 2.897449e-01, 'conf_norm_g': 1.261159e-01, 'conf_norm_b': 1.866181e-01, 'w_out': 1.152994e-01, 'ln_mix_g': 2.278640e+00, 'ln_mix_b': 1.053757e+00, 'mem_w_q': 9.471621e-03, 'mem_w_kv': 1.021843e-02, 'mem_w_o': 2.180317e-02, 'ln_mem_g': 2.275327e+00, 'ln_mem_b': 1.053608e+00, 'ffn2_w_gate': 1.658826e-02, 'ffn2_w_up': 1.615276e-02, 'ffn2_w_down': 5.351084e-02, 'ln_ffn2_g': 4.542703e+01, 'ln_ffn2_b': 3.719046e+00}


def _to_microbatches(a, axis):
    t = _jnp.moveaxis(a, axis, 0)
    t = t.reshape((N_MICROBATCH, t.shape[0] // N_MICROBATCH) + t.shape[1:])
    return _jnp.moveaxis(t, 1, axis + 1)


def setup_inputs(seed: int = 0) -> dict:
    inp = _fwd_setup_inputs(seed)
    key = _jax.random.fold_in(_jax.random.key(seed), 7919)
    shape, _ = _output_shape()
    out = dict(inp)
    out["loss_target"] = _jax.random.normal(_jax.random.fold_in(key, 0), shape, _jnp.float32)
    for i, name in enumerate(TWIN_WEIGHTS):
        w = inp[name].astype(_jnp.float32)
        if MOMENT_SCALE is None:
            s = _jnp.sqrt(_jnp.mean(_jnp.square(w)) + 1e-30)
        else:
            s = MOMENT_SCALE[name]
        km, kv = _jax.random.split(_jax.random.fold_in(key, i + 1))
        out[name] = w
        out["m_" + name] = s * _jax.random.normal(km, w.shape, _jnp.float32)
        out["v_" + name] = (s * s) * _jax.random.uniform(kv, w.shape, _jnp.float32, 0.5, 1.5)
    if N_MICROBATCH > 1:
        for name, axis in PER_EXAMPLE_BATCH_AXIS.items():
            out[name] = _to_microbatches(out[name], axis)
    return {'x': out['x'], 'mem': out['mem'], 'ffn1_w_gate': out['ffn1_w_gate'], 'ffn1_w_up': out['ffn1_w_up'], 'ffn1_w_down': out['ffn1_w_down'], 'ln_ffn1_g': out['ln_ffn1_g'], 'ln_ffn1_b': out['ln_ffn1_b'], 'w_in': out['w_in'], 'gdn_conv_w': out['gdn_conv_w'], 'gdn_a_log': out['gdn_a_log'], 'gdn_dt_bias': out['gdn_dt_bias'], 'gdn_norm_g': out['gdn_norm_g'], 'fox_b_f': out['fox_b_f'], 'conf_dw_w': out['conf_dw_w'], 'conf_dw_b': out['conf_dw_b'], 'conf_norm_g': out['conf_norm_g'], 'conf_norm_b': out['conf_norm_b'], 'w_out': out['w_out'], 'ln_mix_g': out['ln_mix_g'], 'ln_mix_b': out['ln_mix_b'], 'mem_w_q': out['mem_w_q'], 'mem_w_kv': out['mem_w_kv'], 'mem_w_o': out['mem_w_o'], 'ln_mem_g': out['ln_mem_g'], 'ln_mem_b': out['ln_mem_b'], 'ffn2_w_gate': out['ffn2_w_gate'], 'ffn2_w_up': out['ffn2_w_up'], 'ffn2_w_down': out['ffn2_w_down'], 'ln_ffn2_g': out['ln_ffn2_g'], 'ln_ffn2_b': out['ln_ffn2_b'], 'loss_target': out['loss_target'], 'm_ffn1_w_gate': out['m_ffn1_w_gate'], 'm_ffn1_w_up': out['m_ffn1_w_up'], 'm_ffn1_w_down': out['m_ffn1_w_down'], 'm_ln_ffn1_g': out['m_ln_ffn1_g'], 'm_ln_ffn1_b': out['m_ln_ffn1_b'], 'm_w_in': out['m_w_in'], 'm_gdn_conv_w': out['m_gdn_conv_w'], 'm_gdn_a_log': out['m_gdn_a_log'], 'm_gdn_dt_bias': out['m_gdn_dt_bias'], 'm_gdn_norm_g': out['m_gdn_norm_g'], 'm_fox_b_f': out['m_fox_b_f'], 'm_conf_dw_w': out['m_conf_dw_w'], 'm_conf_dw_b': out['m_conf_dw_b'], 'm_conf_norm_g': out['m_conf_norm_g'], 'm_conf_norm_b': out['m_conf_norm_b'], 'm_w_out': out['m_w_out'], 'm_ln_mix_g': out['m_ln_mix_g'], 'm_ln_mix_b': out['m_ln_mix_b'], 'm_mem_w_q': out['m_mem_w_q'], 'm_mem_w_kv': out['m_mem_w_kv'], 'm_mem_w_o': out['m_mem_w_o'], 'm_ln_mem_g': out['m_ln_mem_g'], 'm_ln_mem_b': out['m_ln_mem_b'], 'm_ffn2_w_gate': out['m_ffn2_w_gate'], 'm_ffn2_w_up': out['m_ffn2_w_up'], 'm_ffn2_w_down': out['m_ffn2_w_down'], 'm_ln_ffn2_g': out['m_ln_ffn2_g'], 'm_ln_ffn2_b': out['m_ln_ffn2_b'], 'v_ffn1_w_gate': out['v_ffn1_w_gate'], 'v_ffn1_w_up': out['v_ffn1_w_up'], 'v_ffn1_w_down': out['v_ffn1_w_down'], 'v_ln_ffn1_g': out['v_ln_ffn1_g'], 'v_ln_ffn1_b': out['v_ln_ffn1_b'], 'v_w_in': out['v_w_in'], 'v_gdn_conv_w': out['v_gdn_conv_w'], 'v_gdn_a_log': out['v_gdn_a_log'], 'v_gdn_dt_bias': out['v_gdn_dt_bias'], 'v_gdn_norm_g': out['v_gdn_norm_g'], 'v_fox_b_f': out['v_fox_b_f'], 'v_conf_dw_w': out['v_conf_dw_w'], 'v_conf_dw_b': out['v_conf_dw_b'], 'v_conf_norm_g': out['v_conf_norm_g'], 'v_conf_norm_b': out['v_conf_norm_b'], 'v_w_out': out['v_w_out'], 'v_ln_mix_g': out['v_ln_mix_g'], 'v_ln_mix_b': out['v_ln_mix_b'], 'v_mem_w_q': out['v_mem_w_q'], 'v_mem_w_kv': out['v_mem_w_kv'], 'v_mem_w_o': out['v_mem_w_o'], 'v_ln_mem_g': out['v_ln_mem_g'], 'v_ln_mem_b': out['v_ln_mem_b'], 'v_ffn2_w_gate': out['v_ffn2_w_gate'], 'v_ffn2_w_up': out['v_ffn2_w_up'], 'v_ffn2_w_down': out['v_ffn2_w_down'], 'v_ln_ffn2_g': out['v_ln_ffn2_g'], 'v_ln_ffn2_b': out['v_ln_ffn2_b']}


def _loss(weights, diff, rest, loss_target):
    with _jax.named_scope("forward"):
        args = {**rest, TWIN_DIFF_INPUT: diff, **{k: w.astype(_WEIGHT_DTYPES[k]) for k, w in weights.items()}}
        y = _forward(args)
    with _jax.named_scope("loss_head"):
        err = _jnp.square(y.astype(_jnp.float32) - loss_target)
        return 0.5 * _jnp.sum(_jnp.mean(err, axis=-1)) if err.ndim else 0.5 * err


def _adamw(w, g, m, v):
    m = ADAM_B1 * m + (1.0 - ADAM_B1) * g
    v = ADAM_B2 * v + (1.0 - ADAM_B2) * _jnp.square(g)
    m_hat = m / (1.0 - ADAM_B1 ** ADAM_STEP)
    v_hat = v / (1.0 - ADAM_B2 ** ADAM_STEP)
    delta = -ADAM_LR * (m_hat / (_jnp.sqrt(v_hat) + ADAM_EPS) + ADAM_WD * w)
    return delta, m, v


def reference(x, mem, ffn1_w_gate, ffn1_w_up, ffn1_w_down, ln_ffn1_g, ln_ffn1_b, w_in, gdn_conv_w, gdn_a_log, gdn_dt_bias, gdn_norm_g, fox_b_f, conf_dw_w, conf_dw_b, conf_norm_g, conf_norm_b, w_out, ln_mix_g, ln_mix_b, mem_w_q, mem_w_kv, mem_w_o, ln_mem_g, ln_mem_b, ffn2_w_gate, ffn2_w_up, ffn2_w_down, ln_ffn2_g, ln_ffn2_b, loss_target, m_ffn1_w_gate, m_ffn1_w_up, m_ffn1_w_down, m_ln_ffn1_g, m_ln_ffn1_b, m_w_in, m_gdn_conv_w, m_gdn_a_log, m_gdn_dt_bias, m_gdn_norm_g, m_fox_b_f, m_conf_dw_w, m_conf_dw_b, m_conf_norm_g, m_conf_norm_b, m_w_out, m_ln_mix_g, m_ln_mix_b, m_mem_w_q, m_mem_w_kv, m_mem_w_o, m_ln_mem_g, m_ln_mem_b, m_ffn2_w_gate, m_ffn2_w_up, m_ffn2_w_down, m_ln_ffn2_g, m_ln_ffn2_b, v_ffn1_w_gate, v_ffn1_w_up, v_ffn1_w_down, v_ln_ffn1_g, v_ln_ffn1_b, v_w_in, v_gdn_conv_w, v_gdn_a_log, v_gdn_dt_bias, v_gdn_norm_g, v_fox_b_f, v_conf_dw_w, v_conf_dw_b, v_conf_norm_g, v_conf_norm_b, v_w_out, v_ln_mix_g, v_ln_mix_b, v_mem_w_q, v_mem_w_kv, v_mem_w_o, v_ln_mem_g, v_ln_mem_b, v_ffn2_w_gate, v_ffn2_w_up, v_ffn2_w_down, v_ln_ffn2_g, v_ln_ffn2_b):
    given = dict(x=x, mem=mem, ffn1_w_gate=ffn1_w_gate, ffn1_w_up=ffn1_w_up, ffn1_w_down=ffn1_w_down, ln_ffn1_g=ln_ffn1_g, ln_ffn1_b=ln_ffn1_b, w_in=w_in, gdn_conv_w=gdn_conv_w, gdn_a_log=gdn_a_log, gdn_dt_bias=gdn_dt_bias, gdn_norm_g=gdn_norm_g, fox_b_f=fox_b_f, conf_dw_w=conf_dw_w, conf_dw_b=conf_dw_b, conf_norm_g=conf_norm_g, conf_norm_b=conf_norm_b, w_out=w_out, ln_mix_g=ln_mix_g, ln_mix_b=ln_mix_b, mem_w_q=mem_w_q, mem_w_kv=mem_w_kv, mem_w_o=mem_w_o, ln_mem_g=ln_mem_g, ln_mem_b=ln_mem_b, ffn2_w_gate=ffn2_w_gate, ffn2_w_up=ffn2_w_up, ffn2_w_down=ffn2_w_down, ln_ffn2_g=ln_ffn2_g, ln_ffn2_b=ln_ffn2_b, loss_target=loss_target, m_ffn1_w_gate=m_ffn1_w_gate, m_ffn1_w_up=m_ffn1_w_up, m_ffn1_w_down=m_ffn1_w_down, m_ln_ffn1_g=m_ln_ffn1_g, m_ln_ffn1_b=m_ln_ffn1_b, m_w_in=m_w_in, m_gdn_conv_w=m_gdn_conv_w, m_gdn_a_log=m_gdn_a_log, m_gdn_dt_bias=m_gdn_dt_bias, m_gdn_norm_g=m_gdn_norm_g, m_fox_b_f=m_fox_b_f, m_conf_dw_w=m_conf_dw_w, m_conf_dw_b=m_conf_dw_b, m_conf_norm_g=m_conf_norm_g, m_conf_norm_b=m_conf_norm_b, m_w_out=m_w_out, m_ln_mix_g=m_ln_mix_g, m_ln_mix_b=m_ln_mix_b, m_mem_w_q=m_mem_w_q, m_mem_w_kv=m_mem_w_kv, m_mem_w_o=m_mem_w_o, m_ln_mem_g=m_ln_mem_g, m_ln_mem_b=m_ln_mem_b, m_ffn2_w_gate=m_ffn2_w_gate, m_ffn2_w_up=m_ffn2_w_up, m_ffn2_w_down=m_ffn2_w_down, m_ln_ffn2_g=m_ln_ffn2_g, m_ln_ffn2_b=m_ln_ffn2_b, v_ffn1_w_gate=v_ffn1_w_gate, v_ffn1_w_up=v_ffn1_w_up, v_ffn1_w_down=v_ffn1_w_down, v_ln_ffn1_g=v_ln_ffn1_g, v_ln_ffn1_b=v_ln_ffn1_b, v_w_in=v_w_in, v_gdn_conv_w=v_gdn_conv_w, v_gdn_a_log=v_gdn_a_log, v_gdn_dt_bias=v_gdn_dt_bias, v_gdn_norm_g=v_gdn_norm_g, v_fox_b_f=v_fox_b_f, v_conf_dw_w=v_conf_dw_w, v_conf_dw_b=v_conf_dw_b, v_conf_norm_g=v_conf_norm_g, v_conf_norm_b=v_conf_norm_b, v_w_out=v_w_out, v_ln_mix_g=v_ln_mix_g, v_ln_mix_b=v_ln_mix_b, v_mem_w_q=v_mem_w_q, v_mem_w_kv=v_mem_w_kv, v_mem_w_o=v_mem_w_o, v_ln_mem_g=v_ln_mem_g, v_ln_mem_b=v_ln_mem_b, v_ffn2_w_gate=v_ffn2_w_gate, v_ffn2_w_up=v_ffn2_w_up, v_ffn2_w_down=v_ffn2_w_down, v_ln_ffn2_g=v_ln_ffn2_g, v_ln_ffn2_b=v_ln_ffn2_b)
    weights = {n: given[n] for n in TWIN_WEIGHTS}
    shared = {n: given[n] for n in SHARED_INPUTS}
    per_example = {n: given[n] for n in ['x', 'mem']}
    grad_fn = _jax.value_and_grad(_loss, argnums=(0, 1))

    def one_microbatch(ex, loss_target):
        ex = dict(ex)
        diff = ex.pop(TWIN_DIFF_INPUT)
        return grad_fn(weights, diff, {**shared, **ex}, loss_target)

    if N_MICROBATCH == 1:
        loss, (grad_w, grad_x) = one_microbatch(per_example, given["loss_target"])
    else:
        def body(carry, xs):
            loss_sum, grad_sum = carry
            l_k, (gw_k, gx_k) = one_microbatch(xs[0], xs[1])
            with _jax.named_scope("update"):
                return (loss_sum + l_k, _jax.tree.map(_jnp.add, grad_sum, gw_k)), gx_k

        init = (_jnp.zeros((), _jnp.float32), _jax.tree.map(_jnp.zeros_like, weights))
        (loss, grad_w), grad_x = _jax.lax.scan(body, init, (per_example, given["loss_target"]))
    with _jax.named_scope("update"):
        delta_w, new_m, new_v = {}, {}, {}
        for n in TWIN_WEIGHTS:
            delta_w[n], new_m[n], new_v[n] = _adamw(weights[n], grad_w[n], given["m_" + n], given["v_" + n])
    return (loss, grad_x, *[grad_w[n] for n in TWIN_WEIGHTS], *[delta_w[n] for n in TWIN_WEIGHTS],
            *[new_m[n] for n in TWIN_WEIGHTS], *[new_v[n] for n in TWIN_WEIGHTS])
```

```python
import functools

import jax
import jax.numpy as jnp
from jax import lax
from jax.experimental import pallas as pl
from jax.experimental.pallas import tpu as pltpu

f32 = jnp.float32
bf16 = jnp.bfloat16

D = 1024
F = 2816
GW = 256
HD = 64
NH = 4
CHUNK = 64
CONF_K = 31
GDN_K = 4
DEPTH = 2
ALPHA = float((2 * DEPTH) ** 0.25)
LN_EPS = 1e-5
RMS_EPS = 1e-6
L2_EPS = 1e-6
NEG = -1e30
PW = 3200
C_GQKV, C_GZ, C_FOX, C_CONF, C_SB, C_SMALL = 0, 768, 1024, 1792, 2304, 3072
ADAM_LR, ADAM_B1, ADAM_B2, ADAM_EPS, ADAM_WD, ADAM_STEP = 0.001, 0.9, 0.999, 1e-08, 0.01, 10
VMEM_LIMIT = 56 * 1024 * 1024
MESH = pl.DeviceIdType.MESH


def _cparams(sem):
    return pltpu.CompilerParams(dimension_semantics=sem, vmem_limit_bytes=VMEM_LIMIT)


def _pick(n, cands):
    for c in cands:
        if n % c == 0:
            return c
    return n


def _sds(shape, dtype=f32):
    return jax.ShapeDtypeStruct(shape, dtype)


def _layer_norm(z, g, b):
    mu = jnp.mean(z, axis=-1, keepdims=True)
    zc = z - mu
    var = jnp.mean(zc * zc, axis=-1, keepdims=True)
    return zc * lax.rsqrt(var + LN_EPS) * g + b


def _softplus(x):
    return jnp.maximum(x, 0.0) + jnp.log(1.0 + jnp.exp(-jnp.abs(x)))


def _dsilu(x):
    s = jax.nn.sigmoid(x)
    return s * (1.0 + x * (1.0 - s))


def _split_hi_lo(x):
    hi = x.astype(bf16)
    lo = (x - hi.astype(f32)).astype(bf16)
    return hi, lo


def _dot(a, b):
    return jnp.dot(a, b, preferred_element_type=f32)


def _dot_nt(a, b):
    return lax.dot_general(a, b, (((1,), (1,)), ((), ())), preferred_element_type=f32)


def _dot_tn(a, b):
    return lax.dot_general(a, b, (((0,), (0,)), ((), ())), preferred_element_type=f32)


def mm_nn(a, w, name, out_dtype=f32, also_bf16=False):
    T, K = a.shape
    N = w.shape[1]
    tm = min(T, 512)
    tn = N if N <= 1024 else _pick(N, (640, 512))

    def body(a_ref, w_ref, *o_refs):
        r = _dot(a_ref[...].astype(bf16), w_ref[...].astype(bf16))
        o_refs[0][...] = r.astype(o_refs[0].dtype)
        if also_bf16:
            o_refs[1][...] = r.astype(bf16)

    out_shape = [_sds((T, N), out_dtype)]
    out_specs = [pl.BlockSpec((tm, tn), lambda i, j: (i, j))]
    if also_bf16:
        out_shape.append(_sds((T, N), bf16))
        out_specs.append(pl.BlockSpec((tm, tn), lambda i, j: (i, j)))
    res = pl.pallas_call(
        body, grid=(T // tm, N // tn),
        in_specs=[pl.BlockSpec((tm, K), lambda i, j: (i, 0)), pl.BlockSpec((K, tn), lambda i, j: (0, j))],
        out_specs=out_specs, out_shape=out_shape,
        compiler_params=_cparams(("parallel", "arbitrary")), name=name)(a, w)
    return res if also_bf16 else res[0]


def mm_nt(g, w, name, add=None, add_scale=1.0):
    T, N = g.shape
    K = w.shape[0]
    tm = min(T, 512)

    def body(*refs):
        if add is None:
            g_ref, w_ref, o_ref = refs
        else:
            g_ref, w_ref, add_ref, o_ref = refs
        r = _dot_nt(g_ref[...].astype(bf16), w_ref[...].astype(bf16))
        if add is not None:
            r = r + add_scale * add_ref[...]
        o_ref[...] = r

    in_specs = [pl.BlockSpec((tm, N), lambda i: (i, 0)), pl.BlockSpec((K, N), lambda i: (0, 0))]
    args = [g, w]
    if add is not None:
        in_specs.append(pl.BlockSpec((tm, K), lambda i: (i, 0)))
        args.append(add)
    return pl.pallas_call(
        body, grid=(T // tm,), in_specs=in_specs,
        out_specs=pl.BlockSpec((tm, K), lambda i: (i, 0)), out_shape=_sds((T, K)),
        compiler_params=_cparams(("parallel",)), name=name)(*args)


def mm_tn(a, g, name):
    T, K = a.shape
    N = g.shape[1]
    tt = min(T, 512)
    if K * N * 4 <= 4 * 1024 * 1024:
        tn = N
    else:
        tn = _pick(N, (1408, 640, 512))

    def body(a_ref, g_ref, o_ref):
        @pl.when(pl.program_id(1) == 0)
        def _():
            o_ref[...] = jnp.zeros_like(o_ref)
        o_ref[...] += _dot_tn(a_ref[...].astype(bf16), g_ref[...].astype(bf16))

    return pl.pallas_call(
        body, grid=(N // tn, T // tt),
        in_specs=[pl.BlockSpec((tt, K), lambda j, t: (t, 0)), pl.BlockSpec((tt, tn), lambda j, t: (t, j))],
        out_specs=pl.BlockSpec((K, tn), lambda j, t: (0, j)), out_shape=_sds((K, N)),
        compiler_params=_cparams(("parallel", "arbitrary")), name=name)(a, g)


FFN_TF = 1408


def ffn_fwd(x, wg, wu, wd, g, b, name):
    T = x.shape[0]
    tm = min(T, 512)
    nf = F // FFN_TF

    def body(x_ref, wg_ref, wu_ref, wd_ref, g_ref, b_ref, z_ref, y_ref, acc):
        j = pl.program_id(1)

        @pl.when(j == 0)
        def _():
            acc[...] = jnp.zeros_like(acc)

        xb = x_ref[...].astype(bf16)
        h = _dot(xb, wg_ref[...])
        u = _dot(xb, wu_ref[...])
        a = (h * jax.nn.sigmoid(h) * u).astype(bf16)
        acc[...] += _dot(a, wd_ref[...])

        @pl.when(j == nf - 1)
        def _():
            z = ALPHA * x_ref[...] + 0.5 * acc[...]
            z_ref[...] = z
            y_ref[...] = _layer_norm(z, g_ref[...], b_ref[...])

    row = pl.BlockSpec((tm, D), lambda i, j: (i, 0))
    vec = pl.BlockSpec((1, D), lambda i, j: (0, 0))
    return pl.pallas_call(
        body, grid=(T // tm, nf),
        in_specs=[row, pl.BlockSpec((D, FFN_TF), lambda i, j: (0, j)), pl.BlockSpec((D, FFN_TF), lambda i, j: (0, j)),
                  pl.BlockSpec((FFN_TF, D), lambda i, j: (j, 0)), vec, vec],
        out_specs=[row, row], out_shape=[_sds((T, D)), _sds((T, D))],
        scratch_shapes=[pltpu.VMEM((tm, D), f32)],
        compiler_params=_cparams(("parallel", "arbitrary")), name=name)(x, wg, wu, wd, g, b)


def ffn_bwd(dz, x, wg, wu, wd, name):
    T = x.shape[0]
    tm = min(T, 512)
    nf = F // FFN_TF

    def body(dz_ref, x_ref, wg_ref, wu_ref, wd_ref, dx_ref, a_ref, dh_ref, du_ref, dzh_ref, acc):
        j = pl.program_id(1)

        @pl.when(j == 0)
        def _():
            acc[...] = jnp.zeros_like(acc)

        dzh = (0.5 * dz_ref[...]).astype(bf16)
        xb = x_ref[...].astype(bf16)
        h = _dot(xb, wg_ref[...])
        u = _dot(xb, wu_ref[...])
        s = jax.nn.sigmoid(h)
        hs = h * s
        da = _dot_nt(dzh, wd_ref[...])
        du = (da * hs).astype(bf16)
        dh = (da * u * (s + hs * (1.0 - s))).astype(bf16)
        a_ref[...] = (hs * u).astype(bf16)
        dh_ref[...] = dh
        du_ref[...] = du
        acc[...] += _dot_nt(dh, wg_ref[...]) + _dot_nt(du, wu_ref[...])

        @pl.when(j == nf - 1)
        def _():
            dx_ref[...] = ALPHA * dz_ref[...] + acc[...]
            dzh_ref[...] = dzh

    row = pl.BlockSpec((tm, D), lambda i, j: (i, 0))
    wide = pl.BlockSpec((tm, FFN_TF), lambda i, j: (i, j))
    return pl.pallas_call(
        body, grid=(T // tm, nf),
        in_specs=[row, row, pl.BlockSpec((D, FFN_TF), lambda i, j: (0, j)), pl.BlockSpec((D, FFN_TF), lambda i, j: (0, j)),
                  pl.BlockSpec((FFN_TF, D), lambda i, j: (j, 0))],
        out_specs=[row, wide, wide, wide, row],
        out_shape=[_sds((T, D)), _sds((T, F), bf16), _sds((T, F), bf16), _sds((T, F), bf16), _sds((T, D), bf16)],
        scratch_shapes=[pltpu.VMEM((tm, D), f32)],
        compiler_params=_cparams(("parallel", "arbitrary")), name=name)(dz, x, wg, wu, wd)


def lin_res_ln(a, w, res, g, b, name):
    T, K = a.shape
    tm = min(T, 512)

    def body(a_ref, w_ref, res_ref, g_ref, b_ref, z_ref, y_ref):
        z = ALPHA * res_ref[...] + _dot(a_ref[...].astype(bf16), w_ref[...])
        z_ref[...] = z
        y_ref[...] = _layer_norm(z, g_ref[...], b_ref[...])

    row = pl.BlockSpec((tm, D), lambda i: (i, 0))
    vec = pl.BlockSpec((1, D), lambda i: (0, 0))
    return pl.pallas_call(
        body, grid=(T // tm,),
        in_specs=[pl.BlockSpec((tm, K), lambda i: (i, 0)), pl.BlockSpec((K, D), lambda i: (0, 0)), row, vec, vec],
        out_specs=[row, row], out_shape=[_sds((T, D)), _sds((T, D))],
        compiler_params=_cparams(("parallel",)), name=name)(a, w, res, g, b)


def ln_bwd(dy, z, g, name):
    T = z.shape[0]
    tm = min(T, 512)

    def body(dy_ref, z_ref, g_ref, dz_ref, dg_ref, db_ref):
        @pl.when(pl.program_id(0) == 0)
        def _():
            dg_ref[...] = jnp.zeros_like(dg_ref)
            db_ref[...] = jnp.zeros_like(db_ref)

        zv = z_ref[...]
        dy = dy_ref[...]
        mu = jnp.mean(zv, axis=-1, keepdims=True)
        zc = zv - mu
        rstd = lax.rsqrt(jnp.mean(zc * zc, axis=-1, keepdims=True) + LN_EPS)
        xh = zc * rstd
        dxh = dy * g_ref[...]
        m1 = jnp.mean(dxh, axis=-1, keepdims=True)
        m2 = jnp.mean(dxh * xh, axis=-1, keepdims=True)
        dz_ref[...] = rstd * (dxh - m1 - xh * m2)
        dg_ref[...] += jnp.sum(dy * xh, axis=0, keepdims=True)
        db_ref[...] += jnp.sum(dy, axis=0, keepdims=True)

    row = pl.BlockSpec((tm, D), lambda i: (i, 0))
    vec = pl.BlockSpec((1, D), lambda i: (0, 0))
    return pl.pallas_call(
        body, grid=(T // tm,), in_specs=[row, row, vec], out_specs=[row, vec, vec],
        out_shape=[_sds((T, D)), _sds((1, D)), _sds((1, D))],
        compiler_params=_cparams(("arbitrary",)), name=name)(dy, z, g)


def loss_and_grad(y, target, name):
    T = y.shape[0]
    tm = min(T, 512)

    def body(y_ref, t_ref, l_ref, dy_ref):
        @pl.when(pl.program_id(0) == 0)
        def _():
            l_ref[...] = jnp.zeros_like(l_ref)
        d = y_ref[...] - t_ref[...]
        dy_ref[...] = d * (1.0 / D)
        l_ref[...] += (0.5 / D) * jnp.sum(jnp.sum(d * d, axis=1, keepdims=True), axis=0, keepdims=True)

    row = pl.BlockSpec((tm, D), lambda i: (i, 0))
    return pl.pallas_call(
        body, grid=(T // tm,), in_specs=[row, row],
        out_specs=[pl.BlockSpec((1, 128), lambda i: (0, 0)), row],
        out_shape=[_sds((1, 128)), _sds((T, D))],
        compiler_params=_cparams(("arbitrary",)), name=name)(y, target)


def _shifted(ext, s):
    return ext if s == 0 else pltpu.roll(ext, s, 0)


def _halo_maps(tm, P, nblk):
    per = tm // P
    prev = lambda i, c: (jnp.maximum(i * per - 1, 0), c)
    nxt = lambda i, c: (jnp.minimum((i + 1) * per, nblk * per - 1), c)
    return prev, nxt


GDN_P = 8
CONF_P = 32


def gdn_conv_fwd(proj, w8, name):
    T = proj.shape[0]
    tm = min(T, 512)
    nblk = T // tm
    C = 3 * GW
    prev, _ = _halo_maps(tm, GDN_P, nblk)

    def body(xc_ref, xp_ref, w_ref, o_ref):
        i = pl.program_id(0)
        xp = jnp.where(i > 0, xp_ref[...], 0.0)
        ext = jnp.concatenate([xp, xc_ref[...]], axis=0)
        acc = jnp.zeros((tm, C), f32)
        for k in range(GDN_K):
            acc = acc + w_ref[k:k + 1, :] * _shifted(ext, GDN_K - 1 - k)[GDN_P:, :]
        o_ref[...] = acc * jax.nn.sigmoid(acc)

    return pl.pallas_call(
        body, grid=(nblk,),
        in_specs=[pl.BlockSpec((tm, C), lambda i: (i, 0)), pl.BlockSpec((GDN_P, C), lambda i: prev(i, 0)),
                  pl.BlockSpec((8, C), lambda i: (0, 0))],
        out_specs=pl.BlockSpec((tm, C), lambda i: (i, 0)), out_shape=_sds((T, C)),
        compiler_params=_cparams(("parallel",)), name=name)(proj, proj, w8)


def gdn_conv_bwd(dy, proj, w8, name):
    T = proj.shape[0]
    tm = min(T, 512)
    nblk = T // tm
    C = 3 * GW
    P = GDN_P
    prev, nxt = _halo_maps(tm, P, nblk)

    def body(dyc_ref, dyn_ref, xc_ref, xp_ref, xn_ref, w_ref, dx_ref, dw_ref):
        i = pl.program_id(0)

        @pl.when(i == 0)
        def _():
            dw_ref[...] = jnp.zeros_like(dw_ref)

        xp = jnp.where(i > 0, xp_ref[...], 0.0)
        last = i == nblk - 1
        xn = jnp.where(last, 0.0, xn_ref[...])
        dyn = jnp.where(last, 0.0, dyn_ref[...])
        ext = jnp.concatenate([xp, xc_ref[...], xn], axis=0)
        sh = [_shifted(ext, GDN_K - 1 - k)[P:, :] for k in range(GDN_K)]
        s = jnp.zeros((tm + P, C), f32)
        for k in range(GDN_K):
            s = s + w_ref[k:k + 1, :] * sh[k]
        ds = jnp.concatenate([dyc_ref[...], dyn], axis=0) * _dsilu(s)
        dx = jnp.zeros((tm, C), f32)
        for k in range(GDN_K):
            d = GDN_K - 1 - k
            moved = ds if d == 0 else pltpu.roll(ds, tm + P - d, 0)
            dx = dx + w_ref[k:k + 1, :] * moved[:tm, :]
            dw_ref[k:k + 1, :] += jnp.sum(ds[:tm, :] * sh[k][:tm, :], axis=0, keepdims=True)
        dx_ref[...] = dx

    col = lambda i: (i, 0)
    return pl.pallas_call(
        body, grid=(nblk,),
        in_specs=[pl.BlockSpec((tm, C), col), pl.BlockSpec((P, C), lambda i: nxt(i, 0)),
                  pl.BlockSpec((tm, C), col), pl.BlockSpec((P, C), lambda i: prev(i, 0)),
                  pl.BlockSpec((P, C), lambda i: nxt(i, 0)), pl.BlockSpec((8, C), lambda i: (0, 0))],
        out_specs=[pl.BlockSpec((tm, C), col), pl.BlockSpec((8, C), lambda i: (0, 0))],
        out_shape=[_sds((T, C)), _sds((8, C))],
        compiler_params=_cparams(("arbitrary",)), name=name)(dy, dy, proj, proj, proj, w8)


def _group_ones():
    r = lax.broadcasted_iota(jnp.int32, (GW, GW), 0) // HD
    c = lax.broadcasted_iota(jnp.int32, (GW, GW), 1) // HD
    return (r == c).astype(bf16)


def _group_mean(x, ones):
    hi, lo = _split_hi_lo(x)
    return (_dot(hi, ones) + _dot(lo, ones)) * (1.0 / HD)


def _conf_norm(c, g, b, ones):
    mu = _group_mean(c, ones)
    cc = c - mu
    rstd = lax.rsqrt(_group_mean(cc * cc, ones) + LN_EPS)
    hn = cc * rstd
    return hn, rstd, hn * g + b


CONF_VAL_BLK = C_CONF // GW
CONF_GATE_BLK = C_CONF // GW + 1


def conf_fwd(proj, w32, bias, ng, nb, name):
    T = proj.shape[0]
    tm = min(T, 512)
    nblk = T // tm
    P = CONF_P
    prev, _ = _halo_maps(tm, P, nblk)

    def body(vc_ref, gc_ref, vp_ref, gp_ref, w_ref, bias_ref, ng_ref, nb_ref, y_ref, c_ref):
        i = pl.program_id(0)
        pc = vc_ref[...] * jax.nn.sigmoid(gc_ref[...])
        pp = jnp.where(i > 0, vp_ref[...] * jax.nn.sigmoid(gp_ref[...]), 0.0)
        ext = jnp.concatenate([pp, pc], axis=0)
        acc = jnp.zeros((tm, GW), f32)
        for k in range(CONF_K):
            acc = acc + w_ref[k:k + 1, :] * _shifted(ext, CONF_K - 1 - k)[P:, :]
        c = acc + bias_ref[...]
        c_ref[...] = c
        _, _, yn = _conf_norm(c, ng_ref[...], nb_ref[...], _group_ones())
        y_ref[...] = yn * jax.nn.sigmoid(yn)

    vec = pl.BlockSpec((1, GW), lambda i: (0, 0))
    return pl.pallas_call(
        body, grid=(nblk,),
        in_specs=[pl.BlockSpec((tm, GW), lambda i: (i, CONF_VAL_BLK)), pl.BlockSpec((tm, GW), lambda i: (i, CONF_GATE_BLK)),
                  pl.BlockSpec((P, GW), lambda i: prev(i, CONF_VAL_BLK)), pl.BlockSpec((P, GW), lambda i: prev(i, CONF_GATE_BLK)),
                  pl.BlockSpec((32, GW), lambda i: (0, 0)), vec, vec, vec],
        out_specs=[pl.BlockSpec((tm, GW), lambda i: (i, 0))] * 2, out_shape=[_sds((T, GW))] * 2,
        compiler_params=_cparams(("parallel",)), name=name)(proj, proj, proj, proj, w32, bias, ng, nb)


def conf_bwd(dy, c, proj, w32, ng, nb, name):
    T = proj.shape[0]
    tm = min(T, 512)
    nblk = T // tm
    P = CONF_P
    prev, nxt = _halo_maps(tm, P, nblk)

    def body(dyc_ref, dyn_ref, cc_ref, cn_ref, vc_ref, gc_ref, vp_ref, gp_ref, w_ref, ng_ref, nb_ref,
             dglu_ref, dw_ref, dbias_ref, dng_ref, dnb_ref):
        i = pl.program_id(0)

        @pl.when(i == 0)
        def _():
            dw_ref[...] = jnp.zeros_like(dw_ref)
            dbias_ref[...] = jnp.zeros_like(dbias_ref)
            dng_ref[...] = jnp.zeros_like(dng_ref)
            dnb_ref[...] = jnp.zeros_like(dnb_ref)

        ones = _group_ones()
        g = ng_ref[...]

        def dc_of(dyv, cv):
            hn, rstd, yn = _conf_norm(cv, g, nb_ref[...], ones)
            dyn_ = dyv * _dsilu(yn)
            dhn = dyn_ * g
            dc = rstd * (dhn - _group_mean(dhn, ones) - hn * _group_mean(dhn * hn, ones))
            return dc, dyn_, hn

        dc_c, dyn_c, hn_c = dc_of(dyc_ref[...], cc_ref[...])
        dc_n, _, _ = dc_of(dyn_ref[...], cn_ref[...])
        dc_n = jnp.where(i == nblk - 1, 0.0, dc_n)
        dng_ref[...] += jnp.sum(dyn_c * hn_c, axis=0, keepdims=True)
        dnb_ref[...] += jnp.sum(dyn_c, axis=0, keepdims=True)
        dbias_ref[...] += jnp.sum(dc_c, axis=0, keepdims=True)

        sig_c = jax.nn.sigmoid(gc_ref[...])
        val_c = vc_ref[...]
        pc = val_c * sig_c
        pp = jnp.where(i > 0, vp_ref[...] * jax.nn.sigmoid(gp_ref[...]), 0.0)
        ext = jnp.concatenate([pp, pc], axis=0)
        dext = jnp.concatenate([dc_c, dc_n], axis=0)
        dp = jnp.zeros((tm, GW), f32)
        for k in range(CONF_K):
            d = CONF_K - 1 - k
            moved = dext if d == 0 else pltpu.roll(dext, tm + P - d, 0)
            dp = dp + w_ref[k:k + 1, :] * moved[:tm, :]
            dw_ref[k:k + 1, :] += jnp.sum(dc_c * _shifted(ext, d)[P:, :], axis=0, keepdims=True)
        dglu_ref[:, 0:GW] = dp * sig_c
        dglu_ref[:, GW:2 * GW] = dp * val_c * sig_c * (1.0 - sig_c)

    vec = pl.BlockSpec((1, GW), lambda i: (0, 0))
    blk = pl.BlockSpec((tm, GW), lambda i: (i, 0))
    return pl.pallas_call(
        body, grid=(nblk,),
        in_specs=[blk, pl.BlockSpec((P, GW), lambda i: nxt(i, 0)), blk, pl.BlockSpec((P, GW), lambda i: nxt(i, 0)),
                  pl.BlockSpec((tm, GW), lambda i: (i, CONF_VAL_BLK)), pl.BlockSpec((tm, GW), lambda i: (i, CONF_GATE_BLK)),
                  pl.BlockSpec((P, GW), lambda i: prev(i, CONF_VAL_BLK)), pl.BlockSpec((P, GW), lambda i: prev(i, CONF_GATE_BLK)),
                  pl.BlockSpec((32, GW), lambda i: (0, 0)), vec, vec],
        out_specs=[pl.BlockSpec((tm, 2 * GW), lambda i: (i, 0)), pl.BlockSpec((32, GW), lambda i: (0, 0)), vec, vec, vec],
        out_shape=[_sds((T, 2 * GW)), _sds((32, GW)), _sds((1, GW)), _sds((1, GW)), _sds((1, GW))],
        compiler_params=_cparams(("arbitrary",)), name=name)(dy, dy, c, c, proj, proj, proj, proj, w32, ng, nb)


def _mm_raw(a, b, ta, tb):
    ca = a.ndim - 2 if ta else a.ndim - 1
    cb = b.ndim - 1 if tb else b.ndim - 2
    batch = ((0,), (0,)) if a.ndim == 3 else ((), ())
    return lax.dot_general(a, b, (((ca,), (cb,)), batch), preferred_element_type=f32)


def _mm_prec(a, b, ta, tb, prec):
    if prec == 1:
        return _mm_raw(a.astype(bf16), b.astype(bf16), ta, tb)
    bh, bl = _split_hi_lo(b)
    if prec == 2:
        ab = a.astype(bf16)
        return _mm_raw(ab, bh, ta, tb) + _mm_raw(ab, bl, ta, tb)
    ah, al = _split_hi_lo(a)
    return _mm_raw(ah, bh, ta, tb) + (_mm_raw(ah, bl, ta, tb) + _mm_raw(al, bh, ta, tb))


@functools.partial(jax.custom_vjp, nondiff_argnums=(2, 3, 4))
def mm(a, b, ta=False, tb=False, prec=1):
    return _mm_prec(a, b, ta, tb, prec)


def _mm_fwd(a, b, ta, tb, prec):
    return _mm_prec(a, b, ta, tb, prec), (a, b)


def _mm_bwd(ta, tb, prec, res, ct):
    a, b = res
    da = _mm_prec(b, ct, tb, True, 1) if ta else _mm_prec(ct, b, False, not tb, 1)
    db = _mm_prec(ct, a, True, ta, 1) if tb else _mm_prec(a, ct, not ta, False, 2 if prec == 2 else 1)
    return da, db


mm.defvjp(_mm_fwd, _mm_bwd)


def _tri_inv_raw(l):
    n = -l
    rr = lax.broadcasted_iota(jnp.int32, l.shape, 1)
    cc = lax.broadcasted_iota(jnp.int32, l.shape, 2)
    p = jnp.where(rr == cc, 1.0, 0.0) + n
    for _ in range(5):
        n = _mm_prec(n, n, False, False, 3)
        p = p + _mm_prec(p, n, False, False, 3)
    return p


@jax.custom_vjp
def tri_inv(l):
    return _tri_inv_raw(l)


def _tri_inv_fwd(l):
    t = _tri_inv_raw(l)
    return t, t


def _tri_inv_bwd(t, ct):
    return (-_mm_prec(_mm_prec(t, ct, True, False, 1), t, False, True, 1),)


tri_inv.defvjp(_tri_inv_fwd, _tri_inv_bwd)


def _gdn_block(S, qs, ks, vs, a, b, z, alog, dtb, ng):
    shp = (NH, CHUNK, CHUNK)
    ii = lax.broadcasted_iota(jnp.int32, shp, 1)
    jj = lax.broadcasted_iota(jnp.int32, shp, 2)
    l_incl = jnp.where(ii >= jj, 1.0, 0.0)
    ys = []
    for c in range(len(qs)):
        q = qs[c] * lax.rsqrt(jnp.sum(qs[c] * qs[c], axis=-1, keepdims=True) + L2_EPS) * (HD ** -0.5)
        k = ks[c] * lax.rsqrt(jnp.sum(ks[c] * ks[c], axis=-1, keepdims=True) + L2_EPS)
        v = vs[c]
        beta = jax.nn.sigmoid(b[c])
        g = -jnp.exp(alog) * _softplus(a[c] + dtb)
        gcb = mm(l_incl, jnp.broadcast_to(g, shp), False, False, 2)
        gcr = jnp.swapaxes(gcb, 1, 2)
        decay = jnp.exp(jnp.where(ii >= jj, gcb - gcr, NEG))
        g_last = jnp.sum(jnp.where(ii == CHUNK - 1, gcb, 0.0), axis=1, keepdims=True)
        eg = jnp.exp(gcb)
        kb = k * beta
        lkk = jnp.where(ii > jj, mm(kb, k, False, True) * decay, 0.0)
        t_inv = tri_inv(lkk)
        u = mm(t_inv, v * beta)
        w = mm(t_inv, kb * eg)
        a_qk = jnp.where(ii >= jj, mm(q, k, False, True) * decay, 0.0)
        q_dec = q * eg
        k_dec = k * jnp.exp(g_last - gcb)
        v_new = u - mm(w, S)
        o = mm(q_dec, S) + mm(a_qk, v_new)
        S = S * jnp.exp(g_last) + mm(k_dec, v_new, True, False)
        y = o * lax.rsqrt(jnp.mean(o * o, axis=-1, keepdims=True) + RMS_EPS) * ng
        ys.append(y * (z[c] * jax.nn.sigmoid(z[c])))
    return S, ys


GDN_CB = 256


def _gdn_load(refs, nc):
    q_ref, k_ref, v_ref, z_ref, a_ref, b_ref = refs
    sl = lambda r, c: r[:, c * CHUNK:(c + 1) * CHUNK, :]
    return tuple([sl(r, c) for c in range(nc)] for r in (q_ref, k_ref, v_ref, z_ref, a_ref, b_ref))


def gdn_fwd(qkv_hm, z_hm, ab_hm, alog, dtb, ng, name):
    T = z_hm.shape[1]
    cb = min(T, GDN_CB)
    nc = cb // CHUNK
    nb = T // cb

    def body(q_ref, k_ref, v_ref, z_ref, a_ref, b_ref, alog_ref, dtb_ref, ng_ref, y_ref, s_ref, S):
        @pl.when(pl.program_id(0) == 0)
        def _():
            S[...] = jnp.zeros_like(S)
        s_ref[...] = S[...]
        qs, ks, vs, zs, as_, bs = _gdn_load((q_ref, k_ref, v_ref, z_ref, a_ref, b_ref), nc)
        s_out, ys = _gdn_block(S[...], qs, ks, vs, as_, bs, zs, alog_ref[...], dtb_ref[...], ng_ref[...])
        S[...] = s_out
        for c in range(nc):
            y_ref[:, c * CHUNK:(c + 1) * CHUNK, :] = ys[c]

    hm = lambda h0: pl.BlockSpec((NH, cb, HD), lambda i: (h0, i, 0))
    col = lambda h0: pl.BlockSpec((NH, cb, 1), lambda i: (h0, i, 0))
    par = pl.BlockSpec((NH, 1, 1), lambda i: (0, 0, 0))
    return pl.pallas_call(
        body, grid=(nb,),
        in_specs=[hm(0), hm(1), hm(2), hm(0), col(0), col(1), par, par, pl.BlockSpec((1, 1, HD), lambda i: (0, 0, 0))],
        out_specs=[hm(0), pl.BlockSpec((None, NH, HD, HD), lambda i: (i, 0, 0, 0))],
        out_shape=[_sds((NH, T, HD)), _sds((nb, NH, HD, HD))],
        scratch_shapes=[pltpu.VMEM((NH, HD, HD), f32)],
        compiler_params=_cparams(("arbitrary",)), name=name)(qkv_hm, qkv_hm, qkv_hm, z_hm, ab_hm, ab_hm, alog, dtb, ng)


def gdn_bwd(dy_hm, states, qkv_hm, z_hm, ab_hm, alog, dtb, ng, name):
    T = z_hm.shape[1]
    cb = min(T, GDN_CB)
    nc = cb // CHUNK
    nb = T // cb

    def body(dy_ref, s_ref, q_ref, k_ref, v_ref, z_ref, a_ref, b_ref, alog_ref, dtb_ref, ng_ref,
             dq_ref, dk_ref, dv_ref, dz_ref, da_ref, db_ref, dalog_ref, ddtb_ref, dng_ref, dS):
        @pl.when(pl.program_id(0) == 0)
        def _():
            dS[...] = jnp.zeros_like(dS)
            dalog_ref[...] = jnp.zeros_like(dalog_ref)
            ddtb_ref[...] = jnp.zeros_like(ddtb_ref)
            dng_ref[...] = jnp.zeros_like(dng_ref)

        qs, ks, vs, zs, as_, bs = _gdn_load((q_ref, k_ref, v_ref, z_ref, a_ref, b_ref), nc)
        _, vjp = jax.vjp(_gdn_block, s_ref[...], qs, ks, vs, as_, bs, zs, alog_ref[...], dtb_ref[...], ng_ref[...])
        dys = [dy_ref[:, c * CHUNK:(c + 1) * CHUNK, :] for c in range(nc)]
        d_s, dqs, dks, dvs, das, dbs, dzs, d_alog, d_dtb, d_ng = vjp((dS[...], dys))
        dS[...] = d_s
        dalog_ref[...] += d_alog
        ddtb_ref[...] += d_dtb
        dng_ref[...] += d_ng
        for c in range(nc):
            sl = slice(c * CHUNK, (c + 1) * CHUNK)
            dq_ref[:, sl, :] = dqs[c]
            dk_ref[:, sl, :] = dks[c]
            dv_ref[:, sl, :] = dvs[c]
            dz_ref[:, sl, :] = dzs[c]
            da_ref[:, sl, :] = das[c]
            db_ref[:, sl, :] = dbs[c]

    rev = lambda i: nb - 1 - i
    hm = lambda h0: pl.BlockSpec((NH, cb, HD), lambda i: (h0, rev(i), 0))
    col = lambda h0: pl.BlockSpec((NH, cb, 1), lambda i: (h0, rev(i), 0))
    par = pl.BlockSpec((NH, 1, 1), lambda i: (0, 0, 0))
    ngs = pl.BlockSpec((1, 1, HD), lambda i: (0, 0, 0))
    res = pl.pallas_call(
        body, grid=(nb,),
        in_specs=[hm(0), pl.BlockSpec((None, NH, HD, HD), lambda i: (rev(i), 0, 0, 0)),
                  hm(0), hm(1), hm(2), hm(0), col(0), col(1), par, par, ngs],
        out_specs=[hm(0), hm(0), hm(0), hm(0), col(0), col(0), par, par, ngs],
        out_shape=[_sds((NH, T, HD))] * 4 + [_sds((NH, T, 1))] * 2 + [_sds((NH, 1, 1))] * 2 + [_sds((1, 1, HD))],
        scratch_shapes=[pltpu.VMEM((NH, HD, HD), f32)],
        compiler_params=_cparams(("arbitrary",)), name=name)(dy_hm, states, qkv_hm, qkv_hm, qkv_hm, z_hm, ab_hm, ab_hm, alog, dtb, ng)
    dq, dk, dv, dz, da, db, dalog, ddtb, dng = res
    return jnp.concatenate([dq, dk, dv], axis=0), dz, jnp.concatenate([da, db], axis=0), dalog, ddtb, dng


F_LANE = 8
SCAN_TB = 256


def fox_gate_fwd(proj, bfv, name):
    T = proj.shape[0]
    tb = min(T, SCAN_TB)

    def body(x_ref, b_ref, o_ref, carry):
        @pl.when(pl.program_id(0) == 0)
        def _():
            carry[...] = jnp.zeros_like(carry)
        logf = -_softplus(-(x_ref[...] + b_ref[...]))
        r = lax.broadcasted_iota(jnp.int32, (tb, tb), 0)
        c = lax.broadcasted_iota(jnp.int32, (tb, tb), 1)
        tri = (r >= c).astype(bf16)
        hi, lo = _split_hi_lo(logf)
        cum = _dot(tri, hi) + _dot(tri, lo) + carry[0:1, :]
        o_ref[...] = cum
        carry[0:1, :] = cum[tb - 1:tb, :]

    return pl.pallas_call(
        body, grid=(T // tb,),
        in_specs=[pl.BlockSpec((tb, 128), lambda i: (i, C_SMALL // 128)), pl.BlockSpec((1, 128), lambda i: (0, 0))],
        out_specs=pl.BlockSpec((tb, 128), lambda i: (i, 0)), out_shape=_sds((T, 128)),
        scratch_shapes=[pltpu.VMEM((8, 128), f32)],
        compiler_params=_cparams(("arbitrary",)), name=name)(proj, bfv)


def fox_gate_bwd(dcum, proj, bfv, name):
    T = proj.shape[0]
    tb = min(T, SCAN_TB)
    nb = T // tb

    def body(d_ref, x_ref, b_ref, o_ref, db_ref, carry):
        @pl.when(pl.program_id(0) == 0)
        def _():
            carry[...] = jnp.zeros_like(carry)
            db_ref[...] = jnp.zeros_like(db_ref)
        r = lax.broadcasted_iota(jnp.int32, (tb, tb), 0)
        c = lax.broadcasted_iota(jnp.int32, (tb, tb), 1)
        tri = (c >= r).astype(bf16)
        hi, lo = _split_hi_lo(d_ref[...])
        dlogf = _dot(tri, hi) + _dot(tri, lo) + carry[0:1, :]
        carry[0:1, :] = dlogf[0:1, :]
        lane = lax.broadcasted_iota(jnp.int32, (tb, 128), 1)
        keep = (lane >= F_LANE) & (lane < F_LANE + NH)
        dx = jnp.where(keep, dlogf * jax.nn.sigmoid(-(x_ref[...] + b_ref[...])), 0.0)
        o_ref[...] = dx
        db_ref[...] += jnp.sum(dx, axis=0, keepdims=True)

    rev = lambda i: nb - 1 - i
    return pl.pallas_call(
        body, grid=(nb,),
        in_specs=[pl.BlockSpec((tb, 128), lambda i: (rev(i), 0)), pl.BlockSpec((tb, 128), lambda i: (rev(i), C_SMALL // 128)),
                  pl.BlockSpec((1, 128), lambda i: (0, 0))],
        out_specs=[pl.BlockSpec((tb, 128), lambda i: (rev(i), 0)), pl.BlockSpec((1, 128), lambda i: (0, 0))],
        out_shape=[_sds((T, 128)), _sds((1, 128))],
        scratch_shapes=[pltpu.VMEM((8, 128), f32)],
        compiler_params=_cparams(("arbitrary",)), name=name)(dcum, proj, bfv)


ATT_TQ = 256


def _lane_col(tile, lane):
    li = lax.broadcasted_iota(jnp.int32, tile.shape, 1)
    return jnp.sum(jnp.where(li == lane, tile, 0.0), axis=1, keepdims=True)


def _pack_cols(cols):
    rows = cols[0].shape[0]
    li = lax.broadcasted_iota(jnp.int32, (rows, 128), 1)
    out = jnp.zeros((rows, 128), f32)
    for h, cv in enumerate(cols):
        out = jnp.where(li == h, cv, out)
    return out


def _head_masks():
    li = lax.broadcasted_iota(jnp.int32, (1, 128), 1)
    return [li < HD, li >= HD]


def _qkv_specs(T, tq, base_blk):
    q = pl.BlockSpec((tq, 128), lambda p, i: (i, base_blk + p))
    k = pl.BlockSpec((T, 128), lambda p, i: (0, base_blk + 2 + p))
    v = pl.BlockSpec((T, 128), lambda p, i: (0, base_blk + 4 + p))
    return q, k, v


def fox_fwd(pbf, cumcol, cumrow, name):
    T = pbf.shape[0]
    tq = min(T, ATT_TQ)
    nq = T // tq
    scale = HD ** -0.5

    def body(q_ref, k_ref, v_ref, cc_ref, cr_ref, o_ref, lse_ref):
        i = pl.program_id(1)
        masks = _head_masks()
        qv = q_ref[...]
        qh = [jnp.where(m, qv, jnp.zeros_like(qv)) for m in masks]
        cct = cc_ref[...]
        ccol = [_lane_col(cct, h) for h in range(2)]
        rows = i * tq + lax.broadcasted_iota(jnp.int32, (tq, 1), 0)

        def kstep(kb, carry):
            ms, ls, acc = carry
            off = pl.multiple_of(kb * tq, tq)
            kblk = k_ref[pl.ds(off, tq), :]
            vblk = v_ref[pl.ds(off, tq), :]
            crow = cr_ref[kb]
            cols = kb * tq + lax.broadcasted_iota(jnp.int32, (1, tq), 1)
            causal = cols <= rows
            new_m, new_l = [], []
            for h in range(2):
                s = _dot_nt(qh[h], kblk) * scale + ccol[h] - crow[h:h + 1, :]
                s = jnp.where(causal, s, NEG)
                m_new = jnp.maximum(ms[h], jnp.max(s, axis=1, keepdims=True))
                corr = jnp.exp(ms[h] - m_new)
                p = jnp.exp(s - m_new)
                new_l.append(ls[h] * corr + jnp.sum(p, axis=1, keepdims=True))
                new_m.append(m_new)
                vh = jnp.where(masks[h], vblk, jnp.zeros_like(vblk))
                acc = jnp.where(masks[h], acc * corr, acc) + _dot(p.astype(bf16), vh)
            return tuple(new_m), tuple(new_l), acc

        init = ((jnp.full((tq, 1), NEG, f32),) * 2, (jnp.zeros((tq, 1), f32),) * 2, jnp.zeros((tq, 128), f32))
        ms, ls, acc = lax.fori_loop(0, i + 1, kstep, init)
        o_ref[...] = acc * jnp.where(masks[0], 1.0 / ls[0], 1.0 / ls[1])
        lse_ref[...] = _pack_cols([ms[h] + jnp.log(ls[h]) for h in range(2)])

    qs, ks, vs = _qkv_specs(T, tq, C_FOX // 128)
    return pl.pallas_call(
        body, grid=(2, nq),
        in_specs=[qs, ks, vs, pl.BlockSpec((None, tq, 128), lambda p, i: (p, i, 0)),
                  pl.BlockSpec((None, nq, 8, tq), lambda p, i: (p, 0, 0, 0))],
        out_specs=[pl.BlockSpec((tq, 128), lambda p, i: (i, p)), pl.BlockSpec((None, tq, 128), lambda p, i: (p, i, 0))],
        out_shape=[_sds((T, GW)), _sds((2, T, 128))],
        compiler_params=_cparams(("parallel", "parallel")), name=name)(pbf, pbf, pbf, cumcol, cumrow)


def fox_bwd(do, o, lse, pbf, cumcol, cumrow, name):
    T = pbf.shape[0]
    tq = min(T, ATT_TQ)
    nq = T // tq
    scale = HD ** -0.5

    def body(do_ref, o_ref, lse_ref, q_ref, k_ref, v_ref, cc_ref, cr_ref, dq_ref, dk_ref, dv_ref, dc_ref, dcq_ref):
        i = pl.program_id(1)

        @pl.when(i == 0)
        def _():
            dk_ref[...] = jnp.zeros_like(dk_ref)
            dv_ref[...] = jnp.zeros_like(dv_ref)
            dc_ref[...] = jnp.zeros_like(dc_ref)

        masks = _head_masks()
        qv = q_ref[...]
        dov = do_ref[...]
        dob = dov.astype(bf16)
        qh = [jnp.where(m, qv, jnp.zeros_like(qv)) for m in masks]
        doh = [jnp.where(m, dob, jnp.zeros_like(dob)) for m in masks]
        prod = dov * o_ref[...]
        delta = [jnp.sum(jnp.where(m, prod, 0.0), axis=1, keepdims=True) for m in masks]
        cct = cc_ref[...]
        lset = lse_ref[...]
        ccol = [_lane_col(cct, h) for h in range(2)]
        lse_h = [_lane_col(lset, h) for h in range(2)]
        rows = i * tq + lax.broadcasted_iota(jnp.int32, (tq, 1), 0)

        def kstep(kb, carry):
            dq, rsum = carry
            off = pl.multiple_of(kb * tq, tq)
            kblk = k_ref[pl.ds(off, tq), :]
            vblk = v_ref[pl.ds(off, tq), :]
            crow = cr_ref[kb]
            new_rsum = []
            cols = kb * tq + lax.broadcasted_iota(jnp.int32, (1, tq), 1)
            causal = cols <= rows
            dk_add = jnp.zeros((tq, 128), f32)
            dv_add = jnp.zeros((tq, 128), f32)
            for h in range(2):
                s = _dot_nt(qh[h], kblk) * scale + ccol[h] - crow[h:h + 1, :]
                p = jnp.where(causal, jnp.exp(s - lse_h[h]), 0.0)
                dp = _dot_nt(doh[h], vblk)
                ds = p * (dp - delta[h])
                dsb = ds.astype(bf16)
                dq = dq + jnp.where(masks[h], _dot(dsb, kblk), 0.0)
                dk_add = dk_add + jnp.where(masks[h], _dot_tn(dsb, qv), 0.0)
                dv_add = dv_add + jnp.where(masks[h], _dot_tn(p.astype(bf16), dob), 0.0)
                dc_ref[kb, h:h + 1, :] += -jnp.sum(ds, axis=0, keepdims=True)
                new_rsum.append(rsum[h] + jnp.sum(ds, axis=1, keepdims=True))
            dk_ref[pl.ds(off, tq), :] += dk_add * scale
            dv_ref[pl.ds(off, tq), :] += dv_add
            return dq, tuple(new_rsum)

        init = (jnp.zeros((tq, 128), f32), (jnp.zeros((tq, 1), f32),) * 2)
        dq, rsum = lax.fori_loop(0, i + 1, kstep, init)
        dq_ref[...] = dq * scale
        dcq_ref[...] = _pack_cols(list(rsum))

    qs, ks, vs = _qkv_specs(T, tq, C_FOX // 128)
    tile = pl.BlockSpec((tq, 128), lambda p, i: (i, p))
    pair = pl.BlockSpec((None, tq, 128), lambda p, i: (p, i, 0))
    rowsp = pl.BlockSpec((None, nq, 8, tq), lambda p, i: (p, 0, 0, 0))
    full = pl.BlockSpec((T, 128), lambda p, i: (0, p))
    return pl.pallas_call(
        body, grid=(2, nq),
        in_specs=[tile, tile, pair, qs, ks, vs, pair, rowsp],
        out_specs=[tile, full, full, rowsp, pair],
        out_shape=[_sds((T, GW)), _sds((T, GW)), _sds((T, GW)), _sds((2, nq, 8, tq)), _sds((2, T, 128))],
        compiler_params=_cparams(("parallel", "arbitrary")), name=name)(do, o, lse, pbf, pbf, pbf, cumcol, cumrow)


def _tri(tq, pred):
    r = lax.broadcasted_iota(jnp.int32, (tq, tq), 0)
    c = lax.broadcasted_iota(jnp.int32, (tq, tq), 1)
    return pred(r, c).astype(bf16)


def _dot_hilo(x, tri):
    hi, lo = _split_hi_lo(x)
    return _dot(hi, tri) + _dot(lo, tri)


def sb_fwd(pbf, name):
    T = pbf.shape[0]
    tq = min(T, ATT_TQ)
    nq = T // tq
    scale = HD ** -0.5

    def body(q_ref, k_ref, v_ref, o_ref, tot_ref):
        i = pl.program_id(1)
        masks = _head_masks()
        qv = q_ref[...]
        qh = [jnp.where(m, qv, jnp.zeros_like(qv)) for m in masks]
        rows = i * tq + lax.broadcasted_iota(jnp.int32, (tq, 1), 0)
        after = _tri(tq, lambda r, c: r > c)

        def kstep(n, carry):
            rs, acc = carry
            kb = i - n
            off = pl.multiple_of(kb * tq, tq)
            kblk = k_ref[pl.ds(off, tq), :]
            vblk = v_ref[pl.ds(off, tq), :]
            cols = kb * tq + lax.broadcasted_iota(jnp.int32, (1, tq), 1)
            strict = cols < rows
            new_r = []
            for h in range(2):
                z = _dot_nt(qh[h], kblk) * scale
                lk = jnp.where(strict, -_softplus(z), 0.0)
                rest = _dot_hilo(lk, after) + rs[h]
                w = jnp.where(strict, jnp.exp(z + lk + rest), 0.0)
                vh = jnp.where(masks[h], vblk, jnp.zeros_like(vblk))
                acc = acc + _dot(w.astype(bf16), vh)
                new_r.append(rs[h] + jnp.sum(lk, axis=1, keepdims=True))
            return tuple(new_r), acc

        rs, acc = lax.fori_loop(0, i + 1, kstep, ((jnp.zeros((tq, 1), f32),) * 2, jnp.zeros((tq, 128), f32)))
        o_ref[...] = acc
        tot_ref[...] = _pack_cols(list(rs))

    qs, ks, vs = _qkv_specs(T, tq, C_SB // 128)
    return pl.pallas_call(
        body, grid=(2, nq), in_specs=[qs, ks, vs],
        out_specs=[pl.BlockSpec((tq, 128), lambda p, i: (i, p)), pl.BlockSpec((None, tq, 128), lambda p, i: (p, i, 0))],
        out_shape=[_sds((T, GW)), _sds((2, T, 128))],
        compiler_params=_cparams(("parallel", "parallel")), name=name)(pbf, pbf, pbf)


def sb_bwd(do, tot, pbf, name):
    T = pbf.shape[0]
    tq = min(T, ATT_TQ)
    nq = T // tq
    scale = HD ** -0.5

    def body(do_ref, tot_ref, q_ref, k_ref, v_ref, dq_ref, dk_ref, dv_ref):
        i = pl.program_id(1)

        @pl.when(i == 0)
        def _():
            dk_ref[...] = jnp.zeros_like(dk_ref)
            dv_ref[...] = jnp.zeros_like(dv_ref)

        masks = _head_masks()
        qv = q_ref[...]
        dob = do_ref[...].astype(bf16)
        qh = [jnp.where(m, qv, jnp.zeros_like(qv)) for m in masks]
        doh = [jnp.where(m, dob, jnp.zeros_like(dob)) for m in masks]
        tott = tot_ref[...]
        tot_h = [_lane_col(tott, h) for h in range(2)]
        rows = i * tq + lax.broadcasted_iota(jnp.int32, (tq, 1), 0)
        upto = _tri(tq, lambda r, c: r <= c)
        before = _tri(tq, lambda r, c: r < c)

        def kstep(kb, carry):
            pre, cg, dq = carry
            off = pl.multiple_of(kb * tq, tq)
            kblk = k_ref[pl.ds(off, tq), :]
            vblk = v_ref[pl.ds(off, tq), :]
            cols = kb * tq + lax.broadcasted_iota(jnp.int32, (1, tq), 1)
            strict = cols < rows
            dk_add = jnp.zeros((tq, 128), f32)
            dv_add = jnp.zeros((tq, 128), f32)
            new_pre, new_cg = [], []
            for h in range(2):
                z = _dot_nt(qh[h], kblk) * scale
                lk = jnp.where(strict, -_softplus(z), 0.0)
                rest = tot_h[h] - (pre[h] + _dot_hilo(lk, upto))
                w = jnp.where(strict, jnp.exp(z + lk + rest), 0.0)
                gmat = w * _dot_nt(doh[h], vblk)
                cmat = cg[h] + _dot_hilo(gmat, before)
                sg = jax.nn.sigmoid(z)
                dz = jnp.where(strict, gmat * (1.0 - sg) - cmat * sg, 0.0)
                dzb = dz.astype(bf16)
                dq = dq + jnp.where(masks[h], _dot(dzb, kblk), 0.0)
                dk_add = dk_add + jnp.where(masks[h], _dot_tn(dzb, qv), 0.0)
                dv_add = dv_add + jnp.where(masks[h], _dot_tn(w.astype(bf16), dob), 0.0)
                new_pre.append(pre[h] + jnp.sum(lk, axis=1, keepdims=True))
                new_cg.append(cg[h] + jnp.sum(gmat, axis=1, keepdims=True))
            dk_ref[pl.ds(off, tq), :] += dk_add * scale
            dv_ref[pl.ds(off, tq), :] += dv_add
            return tuple(new_pre), tuple(new_cg), dq

        zc = (jnp.zeros((tq, 1), f32),) * 2
        _, _, dq = lax.fori_loop(0, i + 1, kstep, (zc, zc, jnp.zeros((tq, 128), f32)))
        dq_ref[...] = dq * scale

    qs, ks, vs = _qkv_specs(T, tq, C_SB // 128)
    tile = pl.BlockSpec((tq, 128), lambda p, i: (i, p))
    pair = pl.BlockSpec((None, tq, 128), lambda p, i: (p, i, 0))
    full = pl.BlockSpec((T, 128), lambda p, i: (0, p))
    return pl.pallas_call(
        body, grid=(2, nq), in_specs=[tile, pair, qs, ks, vs],
        out_specs=[tile, full, full], out_shape=[_sds((T, GW))] * 3,
        compiler_params=_cparams(("parallel", "arbitrary")), name=name)(do, tot, pbf, pbf, pbf)


MEM_HD = D // 4


def mem_fwd(q, kv, name):
    T = q.shape[0]
    M = kv.shape[0]
    tm = min(T, 512)
    scale = MEM_HD ** -0.5

    def body(q_ref, kv_ref, o_ref):
        for h in range(4):
            sl = slice(h * MEM_HD, (h + 1) * MEM_HD)
            kh = kv_ref[:, sl].astype(bf16)
            vh = kv_ref[:, D + h * MEM_HD:D + (h + 1) * MEM_HD].astype(bf16)
            s = _dot_nt(q_ref[:, sl], kh) * scale
            e = jnp.exp(s - jnp.max(s, axis=1, keepdims=True))
            p = e / jnp.sum(e, axis=1, keepdims=True)
            o_ref[:, sl] = _dot(p.astype(bf16), vh).astype(bf16)

    return pl.pallas_call(
        body, grid=(T // tm,),
        in_specs=[pl.BlockSpec((tm, D), lambda i: (i, 0)), pl.BlockSpec((M, 2 * D), lambda i: (0, 0))],
        out_specs=pl.BlockSpec((tm, D), lambda i: (i, 0)), out_shape=_sds((T, D), bf16),
        compiler_params=_cparams(("parallel",)), name=name)(q, kv)


def mem_bwd(do, q, kv, name):
    T = q.shape[0]
    M = kv.shape[0]
    tm = min(T, 512)
    scale = MEM_HD ** -0.5

    def body(do_ref, q_ref, kv_ref, dq_ref, dkv_ref):
        @pl.when(pl.program_id(0) == 0)
        def _():
            dkv_ref[...] = jnp.zeros_like(dkv_ref)
        for h in range(4):
            sl = slice(h * MEM_HD, (h + 1) * MEM_HD)
            vsl = slice(D + h * MEM_HD, D + (h + 1) * MEM_HD)
            qh = q_ref[:, sl]
            kh = kv_ref[:, sl].astype(bf16)
            vh = kv_ref[:, vsl].astype(bf16)
            doh = do_ref[:, sl].astype(bf16)
            s = _dot_nt(qh, kh) * scale
            e = jnp.exp(s - jnp.max(s, axis=1, keepdims=True))
            p = e / jnp.sum(e, axis=1, keepdims=True)
            dp = _dot_nt(doh, vh)
            ds = p * (dp - jnp.sum(dp * p, axis=1, keepdims=True))
            dsb = ds.astype(bf16)
            dq_ref[:, sl] = _dot(dsb, kh) * scale
            dkv_ref[:, sl] += _dot_tn(dsb, qh) * scale
            dkv_ref[:, vsl] += _dot_tn(p.astype(bf16), doh)

    row = pl.BlockSpec((tm, D), lambda i: (i, 0))
    whole = pl.BlockSpec((M, 2 * D), lambda i: (0, 0))
    return pl.pallas_call(
        body, grid=(T // tm,), in_specs=[row, row, whole], out_specs=[row, whole],
        out_shape=[_sds((T, D)), _sds((M, 2 * D))],
        compiler_params=_cparams(("arbitrary",)), name=name)(do, q, kv)


def _chip_peers():
    x, y, c = lax.axis_index("x"), lax.axis_index("y"), lax.axis_index("c")
    me = 2 * x + y
    peers = [((1 - x, y, c), 2 * (1 - x) + y), ((x, 1 - y, c), 2 * x + (1 - y)), ((1 - x, 1 - y, c), 2 * (1 - x) + (1 - y))]
    return me, peers


def gather_chips(arrs, name):
    n = len(arrs)

    def body(*refs):
        ins, outs = refs[:n], refs[n:2 * n]
        send_sems, recv_sems, loc_sems = refs[2 * n:]
        me, peers = _chip_peers()
        started = []
        for i in range(n):
            loc = pltpu.make_async_copy(ins[i], outs[i].at[me], loc_sems.at[i])
            loc.start()
            started.append(loc)
            for k, (dev, _) in enumerate(peers):
                cp = pltpu.make_async_remote_copy(src_ref=ins[i], dst_ref=outs[i].at[me], send_sem=send_sems.at[i, k],
                                                  recv_sem=recv_sems.at[i, k], device_id=dev, device_id_type=MESH)
                cp.start()
                started.append(cp)
        for cp in started:
            cp.wait()

    anyspec = pl.BlockSpec(memory_space=pl.ANY)
    return pl.pallas_call(
        body, in_specs=[anyspec] * n, out_specs=[anyspec] * n,
        out_shape=[_sds((4,) + a.shape, a.dtype) for a in arrs],
        scratch_shapes=[pltpu.SemaphoreType.DMA((n, 3)), pltpu.SemaphoreType.DMA((n, 3)), pltpu.SemaphoreType.DMA((n,))],
        name=name)(*arrs)


def exchange_chips(stacks, name):
    n = len(stacks)

    def body(*refs):
        ins, outs = refs[:n], refs[n:2 * n]
        send_sems, recv_sems, loc_sems = refs[2 * n:]
        me, peers = _chip_peers()
        started = []
        for i in range(n):
            loc = pltpu.make_async_copy(ins[i].at[me], outs[i].at[me], loc_sems.at[i])
            loc.start()
            started.append(loc)
            for k, (dev, pj) in enumerate(peers):
                cp = pltpu.make_async_remote_copy(src_ref=ins[i].at[pj], dst_ref=outs[i].at[me], send_sem=send_sems.at[i, k],
                                                  recv_sem=recv_sems.at[i, k], device_id=dev, device_id_type=MESH)
                cp.start()
                started.append(cp)
        for cp in started:
            cp.wait()

    anyspec = pl.BlockSpec(memory_space=pl.ANY)
    return pl.pallas_call(
        body, in_specs=[anyspec] * n, out_specs=[anyspec] * n,
        out_shape=[_sds(a.shape, a.dtype) for a in stacks],
        scratch_shapes=[pltpu.SemaphoreType.DMA((n, 3)), pltpu.SemaphoreType.DMA((n, 3)), pltpu.SemaphoreType.DMA((n,))],
        name=name)(*stacks)


def swap_sibling(arrs, name):
    n = len(arrs)

    def body(*refs):
        ins, outs = refs[:n], refs[n:2 * n]
        send_sems, recv_sems = refs[2 * n:]
        x, y, c = lax.axis_index("x"), lax.axis_index("y"), lax.axis_index("c")
        started = []
        for i in range(n):
            cp = pltpu.make_async_remote_copy(src_ref=ins[i], dst_ref=outs[i], send_sem=send_sems.at[i],
                                              recv_sem=recv_sems.at[i], device_id=(x, y, 1 - c), device_id_type=MESH)
            cp.start()
            started.append(cp)
        for cp in started:
            cp.wait()

    anyspec = pl.BlockSpec(memory_space=pl.ANY)
    return pl.pallas_call(
        body, in_specs=[anyspec] * n, out_specs=[anyspec] * n,
        out_shape=[_sds(a.shape, a.dtype) for a in arrs],
        scratch_shapes=[pltpu.SemaphoreType.DMA((n,)), pltpu.SemaphoreType.DMA((n,))],
        name=name)(*arrs)


def gather_all(a, name):
    def body(a_ref, o_ref, send_sems, recv_sems, loc_sem):
        x, y, c = lax.axis_index("x"), lax.axis_index("y"), lax.axis_index("c")
        me = 4 * x + 2 * y + c
        loc = pltpu.make_async_copy(a_ref, o_ref.at[me], loc_sem)
        loc.start()
        started = [loc]
        for k in range(1, 8):
            dev = (x ^ (k >> 2), y ^ ((k >> 1) & 1), c ^ (k & 1))
            cp = pltpu.make_async_remote_copy(src_ref=a_ref, dst_ref=o_ref.at[me], send_sem=send_sems.at[k - 1],
                                              recv_sem=recv_sems.at[k - 1], device_id=dev, device_id_type=MESH)
            cp.start()
            started.append(cp)
        for cp in started:
            cp.wait()

    anyspec = pl.BlockSpec(memory_space=pl.ANY)
    return pl.pallas_call(
        body, in_specs=[anyspec], out_specs=anyspec, out_shape=_sds((8,) + a.shape, a.dtype),
        scratch_shapes=[pltpu.SemaphoreType.DMA((7,)), pltpu.SemaphoreType.DMA((7,)), pltpu.SemaphoreType.DMA(())],
        name=name)(a)


def sum_slots(stack, name):
    n, R, C = stack.shape
    tr = R if R <= 512 else _pick(R, (512, 352, 256))

    def body(s_ref, o_ref):
        acc = s_ref[0]
        for j in range(1, n):
            acc = acc + s_ref[j]
        o_ref[...] = acc

    return pl.pallas_call(
        body, grid=(R // tr,), in_specs=[pl.BlockSpec((n, tr, C), lambda i: (0, i, 0))],
        out_specs=pl.BlockSpec((tr, C), lambda i: (i, 0)), out_shape=_sds((R, C)),
        compiler_params=_cparams(("parallel",)), name=name)(stack)


def adamw(w, g1, g2, m, v, name):
    R, C = w.shape
    tr = R if R <= 512 else _pick(R, (512, 352, 256))
    c1 = 1.0 - ADAM_B1 ** ADAM_STEP
    c2 = 1.0 - ADAM_B2 ** ADAM_STEP

    def body(*refs):
        if g2 is None:
            w_ref, g1_ref, m_ref, v_ref, g_out, d_out, m_out, v_out = refs
            g = g1_ref[...]
        else:
            w_ref, g1_ref, g2_ref, m_ref, v_ref, g_out, d_out, m_out, v_out = refs
            g = g1_ref[...] + g2_ref[...]
        mn = ADAM_B1 * m_ref[...] + (1.0 - ADAM_B1) * g
        vn = ADAM_B2 * v_ref[...] + (1.0 - ADAM_B2) * (g * g)
        g_out[...] = g
        m_out[...] = mn
        v_out[...] = vn
        d_out[...] = -ADAM_LR * ((mn / c1) / (jnp.sqrt(vn / c2) + ADAM_EPS) + ADAM_WD * w_ref[...])

    blk = pl.BlockSpec((tr, C), lambda i: (i, 0))
    args = [w, g1] + ([] if g2 is None else [g2]) + [m, v]
    return pl.pallas_call(
        body, grid=(R // tr,), in_specs=[blk] * len(args), out_specs=[blk] * 4, out_shape=[_sds((R, C))] * 4,
        compiler_params=_cparams(("parallel",)), name=name)(*args)


IN_SPLITS = (768, 256, 4, 4, 768, 4, 512, 768)
IN_OFF = (0, 768, 1024, 1028, 1032, 1800, 1804, 2316, 3084)


def regroup_w_in(w):
    seg = lambda i: w[:, IN_OFF[i]:IN_OFF[i + 1]]
    pad = jnp.zeros((w.shape[0], PW - C_SMALL - 12), w.dtype)
    return jnp.concatenate([seg(0), seg(1), seg(4), seg(6), seg(7), seg(2), seg(3), seg(5), pad], axis=1)


def ungroup_w_in(g):
    s = C_SMALL
    return jnp.concatenate([g[:, 0:1024], g[:, s:s + 8], g[:, 1024:1792], g[:, s + 8:s + 12], g[:, 1792:3072]], axis=1)


def to_hm(t, nh):
    T = t.shape[0]
    return t.reshape(T, nh, HD).transpose(1, 0, 2)


def from_hm(t):
    nh, T, _ = t.shape
    return t.transpose(1, 0, 2).reshape(T, nh * HD)


def col_shards(w):
    c = w.shape[-1] // 4
    return jnp.moveaxis(w.reshape(w.shape[:-1] + (4, c)), -2, 0)


def row_shards(w):
    L, r4, c = w.shape
    return w.reshape(L, 4, r4 // 4, c).transpose(1, 0, 2, 3)


def join_cols(g):
    return jnp.moveaxis(g, 0, -2).reshape(g.shape[1:-1] + (4 * g.shape[-1],))


def join_rows(g):
    _, L, r, c = g.shape
    return g.transpose(1, 0, 2, 3).reshape(L, 4 * r, c)


COL_SHARDED = ("ffn1_w_gate", "ffn1_w_up", "w_in", "gdn_conv_w", "conf_dw_w", "mem_w_kv", "ffn2_w_gate", "ffn2_w_up")
CONV_WEIGHTS = ("gdn_conv_w", "conf_dw_w")
ROW_SHARDED = ("ffn1_w_down", "w_out", "mem_w_q", "mem_w_o", "ffn2_w_down")
REPLICATED = ("ln_ffn1_g", "ln_ffn1_b", "gdn_a_log", "gdn_dt_bias", "gdn_norm_g", "fox_b_f", "conf_dw_b", "conf_norm_g",
              "conf_norm_b", "ln_mix_g", "ln_mix_b", "ln_mem_g", "ln_mem_b", "ln_ffn2_g", "ln_ffn2_b")
WEIGHTS = ("ffn1_w_gate", "ffn1_w_up", "ffn1_w_down", "ln_ffn1_g", "ln_ffn1_b", "w_in", "gdn_conv_w", "gdn_a_log",
           "gdn_dt_bias", "gdn_norm_g", "fox_b_f", "conf_dw_w", "conf_dw_b", "conf_norm_g", "conf_norm_b", "w_out",
           "ln_mix_g", "ln_mix_b", "mem_w_q", "mem_w_kv", "mem_w_o", "ln_mem_g", "ln_mem_b", "ffn2_w_gate",
           "ffn2_w_up", "ffn2_w_down", "ln_ffn2_g", "ln_ffn2_b")


def pack_small(d):
    flat = jnp.concatenate([d[n].reshape(-1) for n in REPLICATED])
    rows = -(-flat.shape[0] // 1024) * 8
    return jnp.pad(flat, (0, rows * 128 - flat.shape[0])).reshape(rows, 128)


def unpack_small(p, like):
    flat = p.reshape(-1)
    out, o = {}, 0
    for n in REPLICATED:
        sz = like[n].size
        out[n] = flat[o:o + sz].reshape(like[n].shape)
        o += sz
    return out


def _vec(v):
    return v.reshape(1, -1)


def _pad_rows(w, rows):
    return jnp.pad(w, ((0, rows - w.shape[0]), (0, 0)))


def _small_lane_vec(v4, lane0):
    return jnp.pad(v4.reshape(1, -1), ((0, 0), (lane0, 128 - lane0 - v4.shape[0])))


def layer_fwd(x0, mem, W, li):
    T = x0.shape[0]
    tq = min(T, ATT_TQ)
    nq = T // tq
    n = lambda s: f"l{li}_{s}"
    R = {"x0": x0}
    R["z1"], x1 = ffn_fwd(x0, W["ffn1_w_gate"], W["ffn1_w_up"], W["ffn1_w_down"], _vec(W["ln_ffn1_g"]), _vec(W["ln_ffn1_b"]), n("ffn1_fwd"))
    R["x1"] = x1
    proj, pbf = mm_nn(x1, W["w_in_r"], n("proj"), also_bf16=True)
    R["proj"], R["pbf"] = proj, pbf

    w8 = _pad_rows(W["gdn_conv_w"], 8)
    qkv_s = gdn_conv_fwd(proj, w8, n("gdn_conv_fwd"))
    qkv_hm = to_hm(qkv_s, 12)
    z_hm = to_hm(proj[:, C_GZ:C_GZ + GW], 4)
    ab_hm = proj[:, C_SMALL:C_SMALL + 8].T.reshape(8, T, 1)
    alog = W["gdn_a_log"].reshape(NH, 1, 1)
    dtb = W["gdn_dt_bias"].reshape(NH, 1, 1)
    ng = W["gdn_norm_g"].reshape(1, 1, HD)
    ya_hm, states = gdn_fwd(qkv_hm, z_hm, ab_hm, alog, dtb, ng, n("gdn_fwd"))
    R.update(qkv_hm=qkv_hm, z_hm=z_hm, ab_hm=ab_hm, states=states)

    bfv = _small_lane_vec(W["fox_b_f"], F_LANE)
    cum = fox_gate_fwd(proj, bfv, n("fox_gate_fwd"))
    cum4 = cum[:, F_LANE:F_LANE + NH]
    cumcol = jnp.pad(cum4.reshape(T, 2, 2).transpose(1, 0, 2), ((0, 0), (0, 0), (0, 126)))
    cumrow = jnp.pad(cum4.T.reshape(2, 2, nq, tq).transpose(0, 2, 1, 3), ((0, 0), (0, 0), (0, 6), (0, 0)))
    yb, lse = fox_fwd(pbf, cumcol, cumrow, n("fox_fwd"))
    R.update(cumcol=cumcol, cumrow=cumrow, yb=yb, lse=lse)

    w32 = _pad_rows(W["conf_dw_w"], 32)
    yc, cc = conf_fwd(proj, w32, _vec(W["conf_dw_b"]), _vec(W["conf_norm_g"]), _vec(W["conf_norm_b"]), n("conf_fwd"))
    R["cc"] = cc

    yd, tot = sb_fwd(pbf, n("sb_fwd"))
    R["tot"] = tot

    ycat = jnp.concatenate([from_hm(ya_hm), yb, yc, yd], axis=1).astype(bf16)
    R["ycat"] = ycat
    R["z2"], x2 = lin_res_ln(ycat, W["w_out"], x1, _vec(W["ln_mix_g"]), _vec(W["ln_mix_b"]), n("mix_out"))
    R["x2"] = x2

    qm = mm_nn(x2, W["mem_w_q"], n("mem_q"), out_dtype=bf16)
    kv = mm_nn(mem, W["mem_w_kv"], n("mem_kv"))
    om = mem_fwd(qm, kv, n("mem_fwd"))
    R.update(qm=qm, kv=kv, om=om)
    R["z3"], x3 = lin_res_ln(om, W["mem_w_o"], x2, _vec(W["ln_mem_g"]), _vec(W["ln_mem_b"]), n("mem_out"))
    R["x3"] = x3
    R["z4"], x4 = ffn_fwd(x3, W["ffn2_w_gate"], W["ffn2_w_up"], W["ffn2_w_down"], _vec(W["ln_ffn2_g"]), _vec(W["ln_ffn2_b"]), n("ffn2_fwd"))
    return x4, R


def layer_bwd(dx4, mem, W, R, li):
    T = dx4.shape[0]
    n = lambda s: f"l{li}_{s}"
    G = {}

    def ffn_back(dy, z, x, pre, tag):
        dz, dg, db = ln_bwd(dy, z, _vec(W[f"ln_{pre}_g"]), n(f"{tag}_ln_bwd"))
        dx, a, dh, du, dzh = ffn_bwd(dz, x, W[f"{pre}_w_gate"], W[f"{pre}_w_up"], W[f"{pre}_w_down"], n(f"{tag}_bwd"))
        G[f"{pre}_w_gate"] = mm_tn(x, dh, n(f"{tag}_dwg"))
        G[f"{pre}_w_up"] = mm_tn(x, du, n(f"{tag}_dwu"))
        G[f"{pre}_w_down"] = mm_tn(a, dzh, n(f"{tag}_dwd"))
        G[f"ln_{pre}_g"], G[f"ln_{pre}_b"] = dg.reshape(-1), db.reshape(-1)
        return dx

    dx3 = ffn_back(dx4, R["z4"], R["x3"], "ffn2", "ffn2")

    dz3, dg, db = ln_bwd(dx3, R["z3"], _vec(W["ln_mem_g"]), n("mem_ln_bwd"))
    G["ln_mem_g"], G["ln_mem_b"] = dg.reshape(-1), db.reshape(-1)
    dom = mm_nt(dz3, W["mem_w_o"], n("mem_dom"))
    G["mem_w_o"] = mm_tn(R["om"], dz3, n("mem_dwo"))
    dqm, dkv = mem_bwd(dom, R["qm"], R["kv"], n("mem_bwd"))
    G["mem_w_q"] = mm_tn(R["x2"], dqm, n("mem_dwq"))
    G["mem_w_kv"] = mm_tn(mem, dkv, n("mem_dwkv"))
    dx2 = mm_nt(dqm, W["mem_w_q"], n("mem_dx"), add=dz3, add_scale=ALPHA)

    dz2, dg, db = ln_bwd(dx2, R["z2"], _vec(W["ln_mix_g"]), n("mix_ln_bwd"))
    G["ln_mix_g"], G["ln_mix_b"] = dg.reshape(-1), db.reshape(-1)
    dycat = mm_nt(dz2, W["w_out"], n("mix_dycat"))
    G["w_out"] = mm_tn(R["ycat"], dz2, n("mix_dwout"))
    dya, dyb, dyc, dyd = (dycat[:, i * GW:(i + 1) * GW] for i in range(4))
    proj, pbf = R["proj"], R["pbf"]

    alog = W["gdn_a_log"].reshape(NH, 1, 1)
    dtb = W["gdn_dt_bias"].reshape(NH, 1, 1)
    ng = W["gdn_norm_g"].reshape(1, 1, HD)
    dqkv_hm, dz_hm, dab_hm, dalog, ddtb, dng = gdn_bwd(to_hm(dya, 4), R["states"], R["qkv_hm"], R["z_hm"], R["ab_hm"],
                                                      alog, dtb, ng, n("gdn_bwd"))
    G["gdn_a_log"], G["gdn_dt_bias"], G["gdn_norm_g"] = dalog.reshape(-1), ddtb.reshape(-1), dng.reshape(-1)
    w8 = _pad_rows(W["gdn_conv_w"], 8)
    dgqkv, dw8 = gdn_conv_bwd(from_hm(dqkv_hm), proj, w8, n("gdn_conv_bwd"))
    G["gdn_conv_w"] = dw8[:GDN_K]

    dfq, dfk, dfv, dcumrow, dcumq = fox_bwd(dyb, R["yb"], R["lse"], pbf, R["cumcol"], R["cumrow"], n("fox_bwd"))
    dcum4 = dcumrow[:, :, 0:2, :].transpose(0, 2, 1, 3).reshape(4, T).T
    dcum4 = dcum4 + dcumq[:, :, 0:2].transpose(1, 0, 2).reshape(T, 4)
    dcum = jnp.pad(dcum4, ((0, 0), (F_LANE, 128 - F_LANE - NH)))
    bfv = _small_lane_vec(W["fox_b_f"], F_LANE)
    dsmall_f, dbf = fox_gate_bwd(dcum, proj, bfv, n("fox_gate_bwd"))
    G["fox_b_f"] = dbf[0, F_LANE:F_LANE + NH]

    w32 = _pad_rows(W["conf_dw_w"], 32)
    dglu, dw32, dcb, dcg, dcbeta = conf_bwd(dyc, R["cc"], proj, w32, _vec(W["conf_norm_g"]), _vec(W["conf_norm_b"]), n("conf_bwd"))
    G["conf_dw_w"], G["conf_dw_b"] = dw32[:CONF_K], dcb.reshape(-1)
    G["conf_norm_g"], G["conf_norm_b"] = dcg.reshape(-1), dcbeta.reshape(-1)

    dsq, dsk, dsv = sb_bwd(dyd, R["tot"], pbf, n("sb_bwd"))

    dsmall = jnp.concatenate([dab_hm.reshape(8, T).T, dsmall_f[:, F_LANE:F_LANE + NH], jnp.zeros((T, PW - C_SMALL - 12), f32)], axis=1)
    dproj = jnp.concatenate([dgqkv, from_hm(dz_hm), dfq, dfk, dfv, dglu, dsq, dsk, dsv, dsmall], axis=1).astype(bf16)
    G["w_in_r"] = mm_tn(R["x1"], dproj, n("proj_dw"))
    dx1 = mm_nt(dproj, W["w_in_r"], n("proj_dx"), add=dz2, add_scale=ALPHA)

    dx0 = ffn_back(dx1, R["z1"], R["x0"], "ffn1", "ffn1")
    return dx0, G


def _step(P, M, V, x, mem, loss_target):
    xs, mems, tgt = x[0], mem[0], loss_target[0]

    big = COL_SHARDED + ROW_SHARDED
    gathered = gather_chips([P[k] if k in CONV_WEIGHTS else P[k].astype(bf16) for k in big], "gather_weights")
    full = {}
    for k, g in zip(big, gathered):
        full[k] = join_cols(g) if k in COL_SHARDED else join_rows(g)

    layers = []
    for li in range(DEPTH):
        W = {k: full[k][li] for k in big}
        W.update({k: P[k][li] for k in REPLICATED})
        W["w_in_r"] = regroup_w_in(W["w_in"])
        layers.append(W)

    h = xs
    saved = []
    for li in range(DEPTH):
        h, R = layer_fwd(h, mems, layers[li], li)
        saved.append(R)
    loss_row, dy = loss_and_grad(h, tgt, "loss")
    loss = lax.psum(loss_row[0, 0], ("x", "y", "c"))

    grads = [None] * DEPTH
    for li in reversed(range(DEPTH)):
        dy, G = layer_bwd(dy, mems, layers[li], saved[li], li)
        G["w_in"] = ungroup_w_in(G.pop("w_in_r"))
        grads[li] = G
    grad_x = dy[None]

    stacked = {k: jnp.stack([grads[li][k] for li in range(DEPTH)]) for k in WEIGHTS}
    slots = [col_shards(stacked[k]) if k in COL_SHARDED else row_shards(stacked[k]) for k in big]
    received = exchange_chips(slots, "exchange_grads")
    partial_sums = []
    for k, r in zip(big, received):
        shp = r.shape
        partial_sums.append(sum_slots(r.reshape(4, -1, shp[-1]), f"sum_{k}"))
    from_sibling = swap_sibling(partial_sums, "swap_partials")

    out_g, out_d, out_m, out_v = {}, {}, {}, {}
    for k, mine, theirs in zip(big, partial_sums, from_sibling):
        shp = P[k].shape
        flat = lambda t: t.reshape(-1, shp[-1])
        g, d, mn, vn = adamw(flat(P[k]), mine, theirs, flat(M[k]), flat(V[k]), f"adamw_{k}")
        out_g[k], out_d[k], out_m[k], out_v[k] = (t.reshape(shp) for t in (g, d, mn, vn))

    gsmall = sum_slots(gather_all(pack_small(stacked), "gather_small"), "sum_small")
    g, d, mn, vn = adamw(pack_small(P), gsmall, None, pack_small(M), pack_small(V), "adamw_small")
    for dst, packed in ((out_g, g), (out_d, d), (out_m, mn), (out_v, vn)):
        dst.update(unpack_small(packed, P))

    return (loss, grad_x, *[out_g[k] for k in WEIGHTS], *[out_d[k] for k in WEIGHTS],
            *[out_m[k] for k in WEIGHTS], *[out_v[k] for k in WEIGHTS])


def kernel(x, mem, ffn1_w_gate, ffn1_w_up, ffn1_w_down, ln_ffn1_g, ln_ffn1_b, w_in, gdn_conv_w, gdn_a_log, gdn_dt_bias, gdn_norm_g, fox_b_f, conf_dw_w, conf_dw_b, conf_norm_g, conf_norm_b, w_out, ln_mix_g, ln_mix_b, mem_w_q, mem_w_kv, mem_w_o, ln_mem_g, ln_mem_b, ffn2_w_gate, ffn2_w_up, ffn2_w_down, ln_ffn2_g, ln_ffn2_b, loss_target, m_ffn1_w_gate, m_ffn1_w_up, m_ffn1_w_down, m_ln_ffn1_g, m_ln_ffn1_b, m_w_in, m_gdn_conv_w, m_gdn_a_log, m_gdn_dt_bias, m_gdn_norm_g, m_fox_b_f, m_conf_dw_w, m_conf_dw_b, m_conf_norm_g, m_conf_norm_b, m_w_out, m_ln_mix_g, m_ln_mix_b, m_mem_w_q, m_mem_w_kv, m_mem_w_o, m_ln_mem_g, m_ln_mem_b, m_ffn2_w_gate, m_ffn2_w_up, m_ffn2_w_down, m_ln_ffn2_g, m_ln_ffn2_b, v_ffn1_w_gate, v_ffn1_w_up, v_ffn1_w_down, v_ln_ffn1_g, v_ln_ffn1_b, v_w_in, v_gdn_conv_w, v_gdn_a_log, v_gdn_dt_bias, v_gdn_norm_g, v_fox_b_f, v_conf_dw_w, v_conf_dw_b, v_conf_norm_g, v_conf_norm_b, v_w_out, v_ln_mix_g, v_ln_mix_b, v_mem_w_q, v_mem_w_kv, v_mem_w_o, v_ln_mem_g, v_ln_mem_b, v_ffn2_w_gate, v_ffn2_w_up, v_ffn2_w_down, v_ln_ffn2_g, v_ln_ffn2_b):
    a = locals()
    P = {k: a[k] for k in WEIGHTS}
    M = {k: a["m_" + k] for k in WEIGHTS}
    V = {k: a["v_" + k] for k in WEIGHTS}
    return _step(P, M, V, x, mem, loss_target)
```

```python
import functools

import jax
import jax.numpy as jnp
from jax import lax
from jax.experimental import pallas as pl
from jax.experimental.pallas import tpu as pltpu

f32 = jnp.float32
bf16 = jnp.bfloat16

D = 1024
F = 2816
GW = 256
HD = 64
NH = 4
CHUNK = 64
CONF_K = 31
GDN_K = 4
DEPTH = 2
ALPHA = float((2 * DEPTH) ** 0.25)
LN_EPS = 1e-5
RMS_EPS = 1e-6
L2_EPS = 1e-6
NEG = -1e30
PW = 3200
C_GQKV, C_GZ, C_FOX, C_CONF, C_SB, C_SMALL = 0, 768, 1024, 1792, 2304, 3072
ADAM_LR, ADAM_B1, ADAM_B2, ADAM_EPS, ADAM_WD, ADAM_STEP = 0.001, 0.9, 0.999, 1e-08, 0.01, 10
VMEM_LIMIT = 56 * 1024 * 1024
MESH = pl.DeviceIdType.MESH


def _cparams(sem):
    return pltpu.CompilerParams(dimension_semantics=sem, vmem_limit_bytes=VMEM_LIMIT)


def _pick(n, cands):
    for c in cands:
        if n % c == 0:
            return c
    return n


def _sds(shape, dtype=f32):
    return jax.ShapeDtypeStruct(shape, dtype)


def _layer_norm(z, g, b):
    mu = jnp.mean(z, axis=-1, keepdims=True)
    zc = z - mu
    var = jnp.mean(zc * zc, axis=-1, keepdims=True)
    return zc * lax.rsqrt(var + LN_EPS) * g + b


def _softplus(x):
    return jnp.maximum(x, 0.0) + jnp.log(1.0 + jnp.exp(-jnp.abs(x)))


def _neg_softplus(z):
    nz = -z
    return jnp.minimum(nz, 0.0) - jnp.log(1.0 + jnp.exp(jnp.minimum(z, nz)))


def _dsilu(x):
    s = jax.nn.sigmoid(x)
    return s * (1.0 + x * (1.0 - s))


def _split_hi_lo(x):
    hi = x.astype(bf16)
    lo = (x - hi.astype(f32)).astype(bf16)
    return hi, lo


def _dot(a, b):
    return jnp.dot(a, b, preferred_element_type=f32)


def _dot_nt(a, b):
    return lax.dot_general(a, b, (((1,), (1,)), ((), ())), preferred_element_type=f32)


def _dot_tn(a, b):
    return lax.dot_general(a, b, (((0,), (0,)), ((), ())), preferred_element_type=f32)


def mm_nn(a, w, name, out_dtype=f32, also_bf16=False):
    T, K = a.shape
    N = w.shape[1]
    tm = min(T, 512)
    tn = N if N <= 1024 else _pick(N, (640, 512))

    def body(a_ref, w_ref, *o_refs):
        r = _dot(a_ref[...].astype(bf16), w_ref[...].astype(bf16))
        o_refs[0][...] = r.astype(o_refs[0].dtype)
        if also_bf16:
            o_refs[1][...] = r.astype(bf16)

    out_shape = [_sds((T, N), out_dtype)]
    out_specs = [pl.BlockSpec((tm, tn), lambda i, j: (i, j))]
    if also_bf16:
        out_shape.append(_sds((T, N), bf16))
        out_specs.append(pl.BlockSpec((tm, tn), lambda i, j: (i, j)))
    res = pl.pallas_call(
        body, grid=(T // tm, N // tn),
        in_specs=[pl.BlockSpec((tm, K), lambda i, j: (i, 0)), pl.BlockSpec((K, tn), lambda i, j: (0, j))],
        out_specs=out_specs, out_shape=out_shape,
        compiler_params=_cparams(("parallel", "arbitrary")), name=name)(a, w)
    return res if also_bf16 else res[0]


def mm_nt(g, w, name, add=None, add_scale=1.0):
    T, N = g.shape
    K = w.shape[0]
    tm = min(T, 512)

    def body(*refs):
        if add is None:
            g_ref, w_ref, o_ref = refs
        else:
            g_ref, w_ref, add_ref, o_ref = refs
        r = _dot_nt(g_ref[...].astype(bf16), w_ref[...].astype(bf16))
        if add is not None:
            r = r + add_scale * add_ref[...]
        o_ref[...] = r

    in_specs = [pl.BlockSpec((tm, N), lambda i: (i, 0)), pl.BlockSpec((K, N), lambda i: (0, 0))]
    args = [g, w]
    if add is not None:
        in_specs.append(pl.BlockSpec((tm, K), lambda i: (i, 0)))
        args.append(add)
    return pl.pallas_call(
        body, grid=(T // tm,), in_specs=in_specs,
        out_specs=pl.BlockSpec((tm, K), lambda i: (i, 0)), out_shape=_sds((T, K)),
        compiler_params=_cparams(("parallel",)), name=name)(*args)


def mm_tn(a, g, name):
    T, K = a.shape
    N = g.shape[1]
    tt = min(T, 512)
    if K * N * 4 <= 4 * 1024 * 1024:
        tn = N
    else:
        tn = _pick(N, (1408, 640, 512))

    def body(a_ref, g_ref, o_ref):
        @pl.when(pl.program_id(1) == 0)
        def _():
            o_ref[...] = jnp.zeros_like(o_ref)
        o_ref[...] += _dot_tn(a_ref[...].astype(bf16), g_ref[...].astype(bf16))

    return pl.pallas_call(
        body, grid=(N // tn, T // tt),
        in_specs=[pl.BlockSpec((tt, K), lambda j, t: (t, 0)), pl.BlockSpec((tt, tn), lambda j, t: (t, j))],
        out_specs=pl.BlockSpec((K, tn), lambda j, t: (0, j)), out_shape=_sds((K, N)),
        compiler_params=_cparams(("parallel", "arbitrary")), name=name)(a, g)


FFN_TF = 1408


def ffn_fwd(x, wg, wu, wd, g, b, name):
    T = x.shape[0]
    tm = min(T, 512)
    nf = F // FFN_TF

    def body(x_ref, wg_ref, wu_ref, wd_ref, g_ref, b_ref, z_ref, y_ref, acc):
        j = pl.program_id(1)

        @pl.when(j == 0)
        def _():
            acc[...] = jnp.zeros_like(acc)

        xb = x_ref[...].astype(bf16)
        h = _dot(xb, wg_ref[...])
        u = _dot(xb, wu_ref[...])
        a = (h * jax.nn.sigmoid(h) * u).astype(bf16)
        acc[...] += _dot(a, wd_ref[...])

        @pl.when(j == nf - 1)
        def _():
            z = ALPHA * x_ref[...] + 0.5 * acc[...]
            z_ref[...] = z
            y_ref[...] = _layer_norm(z, g_ref[...], b_ref[...])

    row = pl.BlockSpec((tm, D), lambda i, j: (i, 0))
    vec = pl.BlockSpec((1, D), lambda i, j: (0, 0))
    return pl.pallas_call(
        body, grid=(T // tm, nf),
        in_specs=[row, pl.BlockSpec((D, FFN_TF), lambda i, j: (0, j)), pl.BlockSpec((D, FFN_TF), lambda i, j: (0, j)),
                  pl.BlockSpec((FFN_TF, D), lambda i, j: (j, 0)), vec, vec],
        out_specs=[row, row], out_shape=[_sds((T, D)), _sds((T, D))],
        scratch_shapes=[pltpu.VMEM((tm, D), f32)],
        compiler_params=_cparams(("parallel", "arbitrary")), name=name)(x, wg, wu, wd, g, b)


def ffn_bwd(dz, x, wg, wu, wd, name):
    T = x.shape[0]
    tm = min(T, 512)
    nf = F // FFN_TF

    def body(dz_ref, x_ref, wg_ref, wu_ref, wd_ref, dx_ref, a_ref, dh_ref, du_ref, dzh_ref, acc):
        j = pl.program_id(1)

        @pl.when(j == 0)
        def _():
            acc[...] = jnp.zeros_like(acc)

        dzh = (0.5 * dz_ref[...]).astype(bf16)
        xb = x_ref[...].astype(bf16)
        h = _dot(xb, wg_ref[...])
        u = _dot(xb, wu_ref[...])
        s = jax.nn.sigmoid(h)
        hs = h * s
        da = _dot_nt(dzh, wd_ref[...])
        du = (da * hs).astype(bf16)
        dh = (da * u * (s + hs * (1.0 - s))).astype(bf16)
        a_ref[...] = (hs * u).astype(bf16)
        dh_ref[...] = dh
        du_ref[...] = du
        acc[...] += _dot_nt(dh, wg_ref[...]) + _dot_nt(du, wu_ref[...])

        @pl.when(j == nf - 1)
        def _():
            dx_ref[...] = ALPHA * dz_ref[...] + acc[...]
            dzh_ref[...] = dzh

    row = pl.BlockSpec((tm, D), lambda i, j: (i, 0))
    wide = pl.BlockSpec((tm, FFN_TF), lambda i, j: (i, j))
    return pl.pallas_call(
        body, grid=(T // tm, nf),
        in_specs=[row, row, pl.BlockSpec((D, FFN_TF), lambda i, j: (0, j)), pl.BlockSpec((D, FFN_TF), lambda i, j: (0, j)),
                  pl.BlockSpec((FFN_TF, D), lambda i, j: (j, 0))],
        out_specs=[row, wide, wide, wide, row],
        out_shape=[_sds((T, D)), _sds((T, F), bf16), _sds((T, F), bf16), _sds((T, F), bf16), _sds((T, D), bf16)],
        scratch_shapes=[pltpu.VMEM((tm, D), f32)],
        compiler_params=_cparams(("parallel", "arbitrary")), name=name)(dz, x, wg, wu, wd)


def lin_res_ln(a, w, res, g, b, name):
    T, K = a.shape
    tm = min(T, 512)

    def body(a_ref, w_ref, res_ref, g_ref, b_ref, z_ref, y_ref):
        z = ALPHA * res_ref[...] + _dot(a_ref[...].astype(bf16), w_ref[...])
        z_ref[...] = z
        y_ref[...] = _layer_norm(z, g_ref[...], b_ref[...])

    row = pl.BlockSpec((tm, D), lambda i: (i, 0))
    vec = pl.BlockSpec((1, D), lambda i: (0, 0))
    return pl.pallas_call(
        body, grid=(T // tm,),
        in_specs=[pl.BlockSpec((tm, K), lambda i: (i, 0)), pl.BlockSpec((K, D), lambda i: (0, 0)), row, vec, vec],
        out_specs=[row, row], out_shape=[_sds((T, D)), _sds((T, D))],
        compiler_params=_cparams(("parallel",)), name=name)(a, w, res, g, b)


def ln_bwd(dy, z, g, name):
    T = z.shape[0]
    tm = min(T, 512)

    def body(dy_ref, z_ref, g_ref, dz_ref, dg_ref, db_ref):
        @pl.when(pl.program_id(0) == 0)
        def _():
            dg_ref[...] = jnp.zeros_like(dg_ref)
            db_ref[...] = jnp.zeros_like(db_ref)

        zv = z_ref[...]
        dy = dy_ref[...]
        mu = jnp.mean(zv, axis=-1, keepdims=True)
        zc = zv - mu
        rstd = lax.rsqrt(jnp.mean(zc * zc, axis=-1, keepdims=True) + LN_EPS)
        xh = zc * rstd
        dxh = dy * g_ref[...]
        m1 = jnp.mean(dxh, axis=-1, keepdims=True)
        m2 = jnp.mean(dxh * xh, axis=-1, keepdims=True)
        dz_ref[...] = rstd * (dxh - m1 - xh * m2)
        dg_ref[...] += jnp.sum(dy * xh, axis=0, keepdims=True)
        db_ref[...] += jnp.sum(dy, axis=0, keepdims=True)

    row = pl.BlockSpec((tm, D), lambda i: (i, 0))
    vec = pl.BlockSpec((1, D), lambda i: (0, 0))
    return pl.pallas_call(
        body, grid=(T // tm,), in_specs=[row, row, vec], out_specs=[row, vec, vec],
        out_shape=[_sds((T, D)), _sds((1, D)), _sds((1, D))],
        compiler_params=_cparams(("arbitrary",)), name=name)(dy, z, g)


def loss_and_grad(y, target, name):
    T = y.shape[0]
    tm = min(T, 512)

    def body(y_ref, t_ref, l_ref, dy_ref):
        @pl.when(pl.program_id(0) == 0)
        def _():
            l_ref[...] = jnp.zeros_like(l_ref)
        d = y_ref[...] - t_ref[...]
        dy_ref[...] = d * (1.0 / D)
        l_ref[...] += (0.5 / D) * jnp.sum(jnp.sum(d * d, axis=1, keepdims=True), axis=0, keepdims=True)

    row = pl.BlockSpec((tm, D), lambda i: (i, 0))
    return pl.pallas_call(
        body, grid=(T // tm,), in_specs=[row, row],
        out_specs=[pl.BlockSpec((1, 128), lambda i: (0, 0)), row],
        out_shape=[_sds((1, 128)), _sds((T, D))],
        compiler_params=_cparams(("arbitrary",)), name=name)(y, target)


def _shifted(ext, s):
    return ext if s == 0 else pltpu.roll(ext, s, 0)


def _halo_maps(tm, P, nblk):
    per = tm // P
    prev = lambda i, c: (jnp.maximum(i * per - 1, 0), c)
    nxt = lambda i, c: (jnp.minimum((i + 1) * per, nblk * per - 1), c)
    return prev, nxt


GDN_P = 8
CONF_P = 32


def gdn_conv_fwd(proj, w8, name):
    T = proj.shape[0]
    tm = min(T, 512)
    nblk = T // tm
    C = 3 * GW
    prev, _ = _halo_maps(tm, GDN_P, nblk)

    def body(xc_ref, xp_ref, w_ref, o_ref):
        i = pl.program_id(0)
        xp = jnp.where(i > 0, xp_ref[...], 0.0)
        ext = jnp.concatenate([xp, xc_ref[...]], axis=0)
        acc = jnp.zeros((tm, C), f32)
        for k in range(GDN_K):
            acc = acc + w_ref[k:k + 1, :] * _shifted(ext, GDN_K - 1 - k)[GDN_P:, :]
        o_ref[...] = acc * jax.nn.sigmoid(acc)

    return pl.pallas_call(
        body, grid=(nblk,),
        in_specs=[pl.BlockSpec((tm, C), lambda i: (i, 0)), pl.BlockSpec((GDN_P, C), lambda i: prev(i, 0)),
                  pl.BlockSpec((8, C), lambda i: (0, 0))],
        out_specs=pl.BlockSpec((tm, C), lambda i: (i, 0)), out_shape=_sds((T, C)),
        compiler_params=_cparams(("parallel",)), name=name)(proj, proj, w8)


def gdn_conv_bwd(dy, proj, w8, name):
    T = proj.shape[0]
    tm = min(T, 512)
    nblk = T // tm
    C = 3 * GW
    P = GDN_P
    prev, nxt = _halo_maps(tm, P, nblk)

    def body(dyc_ref, dyn_ref, xc_ref, xp_ref, xn_ref, w_ref, dx_ref, dw_ref):
        i = pl.program_id(0)

        @pl.when(i == 0)
        def _():
            dw_ref[...] = jnp.zeros_like(dw_ref)

        xp = jnp.where(i > 0, xp_ref[...], 0.0)
        last = i == nblk - 1
        xn = jnp.where(last, 0.0, xn_ref[...])
        dyn = jnp.where(last, 0.0, dyn_ref[...])
        ext = jnp.concatenate([xp, xc_ref[...], xn], axis=0)
        sh = [_shifted(ext, GDN_K - 1 - k)[P:, :] for k in range(GDN_K)]
        s = jnp.zeros((tm + P, C), f32)
        for k in range(GDN_K):
            s = s + w_ref[k:k + 1, :] * sh[k]
        ds = jnp.concatenate([dyc_ref[...], dyn], axis=0) * _dsilu(s)
        dx = jnp.zeros((tm, C), f32)
        for k in range(GDN_K):
            d = GDN_K - 1 - k
            moved = ds if d == 0 else pltpu.roll(ds, tm + P - d, 0)
            dx = dx + w_ref[k:k + 1, :] * moved[:tm, :]
            dw_ref[k:k + 1, :] += jnp.sum(ds[:tm, :] * sh[k][:tm, :], axis=0, keepdims=True)
        dx_ref[...] = dx

    col = lambda i: (i, 0)
    return pl.pallas_call(
        body, grid=(nblk,),
        in_specs=[pl.BlockSpec((tm, C), col), pl.BlockSpec((P, C), lambda i: nxt(i, 0)),
                  pl.BlockSpec((tm, C), col), pl.BlockSpec((P, C), lambda i: prev(i, 0)),
                  pl.BlockSpec((P, C), lambda i: nxt(i, 0)), pl.BlockSpec((8, C), lambda i: (0, 0))],
        out_specs=[pl.BlockSpec((tm, C), col), pl.BlockSpec((8, C), lambda i: (0, 0))],
        out_shape=[_sds((T, C)), _sds((8, C))],
        compiler_params=_cparams(("arbitrary",)), name=name)(dy, dy, proj, proj, proj, w8)


def _group_ones():
    r = lax.broadcasted_iota(jnp.int32, (GW, GW), 0) // HD
    c = lax.broadcasted_iota(jnp.int32, (GW, GW), 1) // HD
    return (r == c).astype(bf16)


def _group_mean(x, ones):
    hi, lo = _split_hi_lo(x)
    return (_dot(hi, ones) + _dot(lo, ones)) * (1.0 / HD)


def _conf_norm(c, g, b, ones):
    mu = _group_mean(c, ones)
    cc = c - mu
    rstd = lax.rsqrt(_group_mean(cc * cc, ones) + LN_EPS)
    hn = cc * rstd
    return hn, rstd, hn * g + b


CONF_VAL_BLK = C_CONF // GW
CONF_GATE_BLK = C_CONF // GW + 1


def conf_fwd(proj, w32, bias, ng, nb, name):
    T = proj.shape[0]
    tm = min(T, 512)
    nblk = T // tm
    P = CONF_P
    prev, _ = _halo_maps(tm, P, nblk)

    def body(vc_ref, gc_ref, vp_ref, gp_ref, w_ref, bias_ref, ng_ref, nb_ref, y_ref, c_ref):
        i = pl.program_id(0)
        pc = vc_ref[...] * jax.nn.sigmoid(gc_ref[...])
        pp = jnp.where(i > 0, vp_ref[...] * jax.nn.sigmoid(gp_ref[...]), 0.0)
        ext = jnp.concatenate([pp, pc], axis=0)
        acc = jnp.zeros((tm, GW), f32)
        for k in range(CONF_K):
            acc = acc + w_ref[k:k + 1, :] * _shifted(ext, CONF_K - 1 - k)[P:, :]
        c = acc + bias_ref[...]
        c_ref[...] = c
        _, _, yn = _conf_norm(c, ng_ref[...], nb_ref[...], _group_ones())
        y_ref[...] = yn * jax.nn.sigmoid(yn)

    vec = pl.BlockSpec((1, GW), lambda i: (0, 0))
    return pl.pallas_call(
        body, grid=(nblk,),
        in_specs=[pl.BlockSpec((tm, GW), lambda i: (i, CONF_VAL_BLK)), pl.BlockSpec((tm, GW), lambda i: (i, CONF_GATE_BLK)),
                  pl.BlockSpec((P, GW), lambda i: prev(i, CONF_VAL_BLK)), pl.BlockSpec((P, GW), lambda i: prev(i, CONF_GATE_BLK)),
                  pl.BlockSpec((32, GW), lambda i: (0, 0)), vec, vec, vec],
        out_specs=[pl.BlockSpec((tm, GW), lambda i: (i, 0))] * 2, out_shape=[_sds((T, GW))] * 2,
        compiler_params=_cparams(("parallel",)), name=name)(proj, proj, proj, proj, w32, bias, ng, nb)


def conf_bwd(dy, c, proj, w32, ng, nb, name):
    T = proj.shape[0]
    tm = min(T, 512)
    nblk = T // tm
    P = CONF_P
    prev, nxt = _halo_maps(tm, P, nblk)

    def body(dyc_ref, dyn_ref, cc_ref, cn_ref, vc_ref, gc_ref, vp_ref, gp_ref, w_ref, ng_ref, nb_ref,
             dglu_ref, dw_ref, dbias_ref, dng_ref, dnb_ref):
        i = pl.program_id(0)

        @pl.when(i == 0)
        def _():
            dw_ref[...] = jnp.zeros_like(dw_ref)
            dbias_ref[...] = jnp.zeros_like(dbias_ref)
            dng_ref[...] = jnp.zeros_like(dng_ref)
            dnb_ref[...] = jnp.zeros_like(dnb_ref)

        ones = _group_ones()
        g = ng_ref[...]

        def dc_of(dyv, cv):
            hn, rstd, yn = _conf_norm(cv, g, nb_ref[...], ones)
            dyn_ = dyv * _dsilu(yn)
            dhn = dyn_ * g
            dc = rstd * (dhn - _group_mean(dhn, ones) - hn * _group_mean(dhn * hn, ones))
            return dc, dyn_, hn

        dc_c, dyn_c, hn_c = dc_of(dyc_ref[...], cc_ref[...])
        dc_n, _, _ = dc_of(dyn_ref[...], cn_ref[...])
        dc_n = jnp.where(i == nblk - 1, 0.0, dc_n)
        dng_ref[...] += jnp.sum(dyn_c * hn_c, axis=0, keepdims=True)
        dnb_ref[...] += jnp.sum(dyn_c, axis=0, keepdims=True)
        dbias_ref[...] += jnp.sum(dc_c, axis=0, keepdims=True)

        sig_c = jax.nn.sigmoid(gc_ref[...])
        val_c = vc_ref[...]
        pc = val_c * sig_c
        pp = jnp.where(i > 0, vp_ref[...] * jax.nn.sigmoid(gp_ref[...]), 0.0)
        ext = jnp.concatenate([pp, pc], axis=0)
        dext = jnp.concatenate([dc_c, dc_n], axis=0)
        dp = jnp.zeros((tm, GW), f32)
        for k in range(CONF_K):
            d = CONF_K - 1 - k
            moved = dext if d == 0 else pltpu.roll(dext, tm + P - d, 0)
            dp = dp + w_ref[k:k + 1, :] * moved[:tm, :]
            dw_ref[k:k + 1, :] += jnp.sum(dc_c * _shifted(ext, d)[P:, :], axis=0, keepdims=True)
        dglu_ref[:, 0:GW] = dp * sig_c
        dglu_ref[:, GW:2 * GW] = dp * val_c * sig_c * (1.0 - sig_c)

    vec = pl.BlockSpec((1, GW), lambda i: (0, 0))
    blk = pl.BlockSpec((tm, GW), lambda i: (i, 0))
    return pl.pallas_call(
        body, grid=(nblk,),
        in_specs=[blk, pl.BlockSpec((P, GW), lambda i: nxt(i, 0)), blk, pl.BlockSpec((P, GW), lambda i: nxt(i, 0)),
                  pl.BlockSpec((tm, GW), lambda i: (i, CONF_VAL_BLK)), pl.BlockSpec((tm, GW), lambda i: (i, CONF_GATE_BLK)),
                  pl.BlockSpec((P, GW), lambda i: prev(i, CONF_VAL_BLK)), pl.BlockSpec((P, GW), lambda i: prev(i, CONF_GATE_BLK)),
                  pl.BlockSpec((32, GW), lambda i: (0, 0)), vec, vec],
        out_specs=[pl.BlockSpec((tm, 2 * GW), lambda i: (i, 0)), pl.BlockSpec((32, GW), lambda i: (0, 0)), vec, vec, vec],
        out_shape=[_sds((T, 2 * GW)), _sds((32, GW)), _sds((1, GW)), _sds((1, GW)), _sds((1, GW))],
        compiler_params=_cparams(("arbitrary",)), name=name)(dy, dy, c, c, proj, proj, proj, proj, w32, ng, nb)


def _mm_raw(a, b, ta, tb):
    ca = a.ndim - 2 if ta else a.ndim - 1
    cb = b.ndim - 1 if tb else b.ndim - 2
    batch = ((0,), (0,)) if a.ndim == 3 else ((), ())
    return lax.dot_general(a, b, (((ca,), (cb,)), batch), preferred_element_type=f32)


def _mm_prec(a, b, ta, tb, prec):
    if prec == 1:
        return _mm_raw(a.astype(bf16), b.astype(bf16), ta, tb)
    bh, bl = _split_hi_lo(b)
    if prec == 2:
        ab = a.astype(bf16)
        return _mm_raw(ab, bh, ta, tb) + _mm_raw(ab, bl, ta, tb)
    ah, al = _split_hi_lo(a)
    return _mm_raw(ah, bh, ta, tb) + (_mm_raw(ah, bl, ta, tb) + _mm_raw(al, bh, ta, tb))


@functools.partial(jax.custom_vjp, nondiff_argnums=(2, 3, 4))
def mm(a, b, ta=False, tb=False, prec=1):
    return _mm_prec(a, b, ta, tb, prec)


def _mm_fwd(a, b, ta, tb, prec):
    return _mm_prec(a, b, ta, tb, prec), (a, b)


def _mm_bwd(ta, tb, prec, res, ct):
    a, b = res
    da = _mm_prec(b, ct, tb, True, 1) if ta else _mm_prec(ct, b, False, not tb, 1)
    db = _mm_prec(ct, a, True, ta, 1) if tb else _mm_prec(a, ct, not ta, False, 2 if prec == 2 else 1)
    return da, db


mm.defvjp(_mm_fwd, _mm_bwd)


def _tri_inv_raw(l):
    n = -l
    rr = lax.broadcasted_iota(jnp.int32, l.shape, 1)
    cc = lax.broadcasted_iota(jnp.int32, l.shape, 2)
    p = jnp.where(rr == cc, 1.0, 0.0) + n
    for _ in range(5):
        n = _mm_prec(n, n, False, False, 3)
        p = p + _mm_prec(p, n, False, False, 3)
    return p


@jax.custom_vjp
def tri_inv(l):
    return _tri_inv_raw(l)


def _tri_inv_fwd(l):
    t = _tri_inv_raw(l)
    return t, t


def _tri_inv_bwd(t, ct):
    return (-_mm_prec(_mm_prec(t, ct, True, False, 1), t, False, True, 1),)


tri_inv.defvjp(_tri_inv_fwd, _tri_inv_bwd)


def _gdn_block(S, qs, ks, vs, a, b, z, alog, dtb, ng):
    shp = (NH, CHUNK, CHUNK)
    ii = lax.broadcasted_iota(jnp.int32, shp, 1)
    jj = lax.broadcasted_iota(jnp.int32, shp, 2)
    l_incl = jnp.where(ii >= jj, 1.0, 0.0)
    ys = []
    for c in range(len(qs)):
        q = qs[c] * lax.rsqrt(jnp.sum(qs[c] * qs[c], axis=-1, keepdims=True) + L2_EPS) * (HD ** -0.5)
        k = ks[c] * lax.rsqrt(jnp.sum(ks[c] * ks[c], axis=-1, keepdims=True) + L2_EPS)
        v = vs[c]
        beta = jax.nn.sigmoid(b[c])
        g = -jnp.exp(alog) * _softplus(a[c] + dtb)
        gcb = mm(l_incl, jnp.broadcast_to(g, shp), False, False, 2)
        gcr = jnp.swapaxes(gcb, 1, 2)
        decay = jnp.exp(jnp.where(ii >= jj, gcb - gcr, NEG))
        g_last = jnp.sum(jnp.where(ii == CHUNK - 1, gcb, 0.0), axis=1, keepdims=True)
        eg = jnp.exp(gcb)
        kb = k * beta
        lkk = jnp.where(ii > jj, mm(kb, k, False, True) * decay, 0.0)
        t_inv = tri_inv(lkk)
        u = mm(t_inv, v * beta)
        w = mm(t_inv, kb * eg)
        a_qk = jnp.where(ii >= jj, mm(q, k, False, True) * decay, 0.0)
        q_dec = q * eg
        k_dec = k * jnp.exp(g_last - gcb)
        v_new = u - mm(w, S)
        o = mm(q_dec, S) + mm(a_qk, v_new)
        S = S * jnp.exp(g_last) + mm(k_dec, v_new, True, False)
        y = o * lax.rsqrt(jnp.mean(o * o, axis=-1, keepdims=True) + RMS_EPS) * ng
        ys.append(y * (z[c] * jax.nn.sigmoid(z[c])))
    return S, ys


GDN_CB = 256


def _gdn_load(refs, nc):
    q_ref, k_ref, v_ref, z_ref, a_ref, b_ref = refs
    sl = lambda r, c: r[:, c * CHUNK:(c + 1) * CHUNK, :]
    return tuple([sl(r, c) for c in range(nc)] for r in (q_ref, k_ref, v_ref, z_ref, a_ref, b_ref))


def gdn_fwd(qkv_hm, z_hm, ab_hm, alog, dtb, ng, name):
    T = z_hm.shape[1]
    cb = min(T, GDN_CB)
    nc = cb // CHUNK
    nb = T // cb

    def body(q_ref, k_ref, v_ref, z_ref, a_ref, b_ref, alog_ref, dtb_ref, ng_ref, y_ref, s_ref, S):
        @pl.when(pl.program_id(0) == 0)
        def _():
            S[...] = jnp.zeros_like(S)
        s_ref[...] = S[...]
        qs, ks, vs, zs, as_, bs = _gdn_load((q_ref, k_ref, v_ref, z_ref, a_ref, b_ref), nc)
        s_out, ys = _gdn_block(S[...], qs, ks, vs, as_, bs, zs, alog_ref[...], dtb_ref[...], ng_ref[...])
        S[...] = s_out
        for c in range(nc):
            y_ref[:, c * CHUNK:(c + 1) * CHUNK, :] = ys[c]

    hm = lambda h0: pl.BlockSpec((NH, cb, HD), lambda i: (h0, i, 0))
    col = lambda h0: pl.BlockSpec((NH, cb, 1), lambda i: (h0, i, 0))
    par = pl.BlockSpec((NH, 1, 1), lambda i: (0, 0, 0))
    return pl.pallas_call(
        body, grid=(nb,),
        in_specs=[hm(0), hm(1), hm(2), hm(0), col(0), col(1), par, par, pl.BlockSpec((1, 1, HD), lambda i: (0, 0, 0))],
        out_specs=[hm(0), pl.BlockSpec((None, NH, HD, HD), lambda i: (i, 0, 0, 0))],
        out_shape=[_sds((NH, T, HD)), _sds((nb, NH, HD, HD))],
        scratch_shapes=[pltpu.VMEM((NH, HD, HD), f32)],
        compiler_params=_cparams(("arbitrary",)), name=name)(qkv_hm, qkv_hm, qkv_hm, z_hm, ab_hm, ab_hm, alog, dtb, ng)


def gdn_bwd(dy_hm, states, qkv_hm, z_hm, ab_hm, alog, dtb, ng, name):
    T = z_hm.shape[1]
    cb = min(T, GDN_CB)
    nc = cb // CHUNK
    nb = T // cb

    def body(dy_ref, s_ref, q_ref, k_ref, v_ref, z_ref, a_ref, b_ref, alog_ref, dtb_ref, ng_ref,
             dq_ref, dk_ref, dv_ref, dz_ref, da_ref, db_ref, dalog_ref, ddtb_ref, dng_ref, dS):
        @pl.when(pl.program_id(0) == 0)
        def _():
            dS[...] = jnp.zeros_like(dS)
            dalog_ref[...] = jnp.zeros_like(dalog_ref)
            ddtb_ref[...] = jnp.zeros_like(ddtb_ref)
            dng_ref[...] = jnp.zeros_like(dng_ref)

        qs, ks, vs, zs, as_, bs = _gdn_load((q_ref, k_ref, v_ref, z_ref, a_ref, b_ref), nc)
        _, vjp = jax.vjp(_gdn_block, s_ref[...], qs, ks, vs, as_, bs, zs, alog_ref[...], dtb_ref[...], ng_ref[...])
        dys = [dy_ref[:, c * CHUNK:(c + 1) * CHUNK, :] for c in range(nc)]
        d_s, dqs, dks, dvs, das, dbs, dzs, d_alog, d_dtb, d_ng = vjp((dS[...], dys))
        dS[...] = d_s
        dalog_ref[...] += d_alog
        ddtb_ref[...] += d_dtb
        dng_ref[...] += d_ng
        for c in range(nc):
            sl = slice(c * CHUNK, (c + 1) * CHUNK)
            dq_ref[:, sl, :] = dqs[c]
            dk_ref[:, sl, :] = dks[c]
            dv_ref[:, sl, :] = dvs[c]
            dz_ref[:, sl, :] = dzs[c]
            da_ref[:, sl, :] = das[c]
            db_ref[:, sl, :] = dbs[c]

    rev = lambda i: nb - 1 - i
    hm = lambda h0: pl.BlockSpec((NH, cb, HD), lambda i: (h0, rev(i), 0))
    col = lambda h0: pl.BlockSpec((NH, cb, 1), lambda i: (h0, rev(i), 0))
    par = pl.BlockSpec((NH, 1, 1), lambda i: (0, 0, 0))
    ngs = pl.BlockSpec((1, 1, HD), lambda i: (0, 0, 0))
    res = pl.pallas_call(
        body, grid=(nb,),
        in_specs=[hm(0), pl.BlockSpec((None, NH, HD, HD), lambda i: (rev(i), 0, 0, 0)),
                  hm(0), hm(1), hm(2), hm(0), col(0), col(1), par, par, ngs],
        out_specs=[hm(0), hm(0), hm(0), hm(0), col(0), col(0), par, par, ngs],
        out_shape=[_sds((NH, T, HD))] * 4 + [_sds((NH, T, 1))] * 2 + [_sds((NH, 1, 1))] * 2 + [_sds((1, 1, HD))],
        scratch_shapes=[pltpu.VMEM((NH, HD, HD), f32)],
        compiler_params=_cparams(("arbitrary",)), name=name)(dy_hm, states, qkv_hm, qkv_hm, qkv_hm, z_hm, ab_hm, ab_hm, alog, dtb, ng)
    dq, dk, dv, dz, da, db, dalog, ddtb, dng = res
    return jnp.concatenate([dq, dk, dv], axis=0), dz, jnp.concatenate([da, db], axis=0), dalog, ddtb, dng


F_LANE = 8
SCAN_TB = 256


def fox_gate_fwd(proj, bfv, name):
    T = proj.shape[0]
    tb = min(T, SCAN_TB)

    def body(x_ref, b_ref, o_ref, carry):
        @pl.when(pl.program_id(0) == 0)
        def _():
            carry[...] = jnp.zeros_like(carry)
        logf = -_softplus(-(x_ref[...] + b_ref[...]))
        r = lax.broadcasted_iota(jnp.int32, (tb, tb), 0)
        c = lax.broadcasted_iota(jnp.int32, (tb, tb), 1)
        tri = (r >= c).astype(bf16)
        hi, lo = _split_hi_lo(logf)
        cum = _dot(tri, hi) + _dot(tri, lo) + carry[0:1, :]
        o_ref[...] = cum
        carry[0:1, :] = cum[tb - 1:tb, :]

    return pl.pallas_call(
        body, grid=(T // tb,),
        in_specs=[pl.BlockSpec((tb, 128), lambda i: (i, C_SMALL // 128)), pl.BlockSpec((1, 128), lambda i: (0, 0))],
        out_specs=pl.BlockSpec((tb, 128), lambda i: (i, 0)), out_shape=_sds((T, 128)),
        scratch_shapes=[pltpu.VMEM((8, 128), f32)],
        compiler_params=_cparams(("arbitrary",)), name=name)(proj, bfv)


def fox_gate_bwd(dcum, proj, bfv, name):
    T = proj.shape[0]
    tb = min(T, SCAN_TB)
    nb = T // tb

    def body(d_ref, x_ref, b_ref, o_ref, db_ref, carry):
        @pl.when(pl.program_id(0) == 0)
        def _():
            carry[...] = jnp.zeros_like(carry)
            db_ref[...] = jnp.zeros_like(db_ref)
        r = lax.broadcasted_iota(jnp.int32, (tb, tb), 0)
        c = lax.broadcasted_iota(jnp.int32, (tb, tb), 1)
        tri = (c >= r).astype(bf16)
        hi, lo = _split_hi_lo(d_ref[...])
        dlogf = _dot(tri, hi) + _dot(tri, lo) + carry[0:1, :]
        carry[0:1, :] = dlogf[0:1, :]
        lane = lax.broadcasted_iota(jnp.int32, (tb, 128), 1)
        keep = (lane >= F_LANE) & (lane < F_LANE + NH)
        dx = jnp.where(keep, dlogf * jax.nn.sigmoid(-(x_ref[...] + b_ref[...])), 0.0)
        o_ref[...] = dx
        db_ref[...] += jnp.sum(dx, axis=0, keepdims=True)

    rev = lambda i: nb - 1 - i
    return pl.pallas_call(
        body, grid=(nb,),
        in_specs=[pl.BlockSpec((tb, 128), lambda i: (rev(i), 0)), pl.BlockSpec((tb, 128), lambda i: (rev(i), C_SMALL // 128)),
                  pl.BlockSpec((1, 128), lambda i: (0, 0))],
        out_specs=[pl.BlockSpec((tb, 128), lambda i: (rev(i), 0)), pl.BlockSpec((1, 128), lambda i: (0, 0))],
        out_shape=[_sds((T, 128)), _sds((1, 128))],
        scratch_shapes=[pltpu.VMEM((8, 128), f32)],
        compiler_params=_cparams(("arbitrary",)), name=name)(dcum, proj, bfv)


ATT_TQ = 256


def _lane_col(tile, lane):
    li = lax.broadcasted_iota(jnp.int32, tile.shape, 1)
    return jnp.sum(jnp.where(li == lane, tile, 0.0), axis=1, keepdims=True)


def _pack_cols(cols):
    rows = cols[0].shape[0]
    li = lax.broadcasted_iota(jnp.int32, (rows, 128), 1)
    out = jnp.zeros((rows, 128), f32)
    for h, cv in enumerate(cols):
        out = jnp.where(li == h, cv, out)
    return out


def _head_masks():
    li = lax.broadcasted_iota(jnp.int32, (1, 128), 1)
    return [li < HD, li >= HD]


def _qkv_specs(T, tq, base_blk):
    q = pl.BlockSpec((tq, 128), lambda p, i: (i, base_blk + p))
    k = pl.BlockSpec((T, 128), lambda p, i: (0, base_blk + 2 + p))
    v = pl.BlockSpec((T, 128), lambda p, i: (0, base_blk + 4 + p))
    return q, k, v


def _stack_heads(x, masks):
    return jnp.concatenate([jnp.where(m, x, jnp.zeros_like(x)) for m in masks], axis=0)


def _side_by_side(x, tq):
    return jnp.concatenate([x[:tq], x[tq:]], axis=1)


def _stacked_mask(tq, strict):
    r = lax.broadcasted_iota(jnp.int32, (2 * tq, tq), 0)
    r = jnp.where(r >= tq, r - tq, r)
    c = lax.broadcasted_iota(jnp.int32, (2 * tq, tq), 1)
    return c < r if strict else c <= r


def _sub_head_rows(s, rows2, tq):
    return jnp.concatenate([s[:tq] - rows2[0:1, :], s[tq:] - rows2[1:2, :]], axis=0)


def _lane_cols2(tile):
    return jnp.concatenate([_lane_col(tile, 0), _lane_col(tile, 1)], axis=0)


def _pack_cols2(col, tq):
    return _pack_cols([col[:tq], col[tq:]])


def _dot_hilo2(x, tri):
    n = x.shape[0]
    hi, lo = _split_hi_lo(x)
    r = _dot(jnp.concatenate([hi, lo], axis=0), tri)
    return r[:n] + r[n:]


def fox_fwd(pbf, cumrow, name):
    T = pbf.shape[0]
    tq = min(T, ATT_TQ)
    nq = T // tq
    scale = HD ** -0.5

    def body(q_ref, k_ref, v_ref, cr_ref, o_ref, lse_ref):
        i = pl.program_id(1)
        masks = _head_masks()
        qs2 = _stack_heads(q_ref[...] * scale, masks)

        def tile(kb, carry, diag):
            m, l, acc = carry
            off = pl.multiple_of(kb * tq, tq)
            v2 = _stack_heads(v_ref[pl.ds(off, tq), :], masks)
            s = _sub_head_rows(_dot_nt(qs2, k_ref[pl.ds(off, tq), :]), cr_ref[kb], tq)
            if diag:
                s = jnp.where(_stacked_mask(tq, False), s, NEG)
            m_new = jnp.maximum(m, jnp.max(s, axis=1, keepdims=True))
            corr = jnp.exp(m - m_new)
            p = jnp.exp(s - m_new)
            l = l * corr + jnp.sum(p, axis=1, keepdims=True)
            acc = acc * jnp.where(masks[0], corr[:tq], corr[tq:]) + _dot(_side_by_side(p.astype(bf16), tq), v2)
            return m_new, l, acc

        init = (jnp.full((2 * tq, 1), NEG, f32), jnp.zeros((2 * tq, 1), f32), jnp.zeros((tq, 128), f32))
        carry = lax.fori_loop(0, i, lambda kb, c: tile(kb, c, False), init)
        m, l, acc = tile(i, carry, True)
        o_ref[...] = acc * jnp.where(masks[0], 1.0 / l[:tq], 1.0 / l[tq:])
        lse_ref[...] = _pack_cols2(m + jnp.log(l), tq)

    qs, ks, vs = _qkv_specs(T, tq, C_FOX // 128)
    return pl.pallas_call(
        body, grid=(2, nq),
        in_specs=[qs, ks, vs, pl.BlockSpec((None, nq, 8, tq), lambda p, i: (p, 0, 0, 0))],
        out_specs=[pl.BlockSpec((tq, 128), lambda p, i: (i, p)), pl.BlockSpec((None, tq, 128), lambda p, i: (p, i, 0))],
        out_shape=[_sds((T, GW)), _sds((2, T, 128))],
        compiler_params=_cparams(("parallel", "parallel")), name=name)(pbf, pbf, pbf, cumrow)


def fox_bwd(do, o, lse, pbf, cumrow, name):
    T = pbf.shape[0]
    tq = min(T, ATT_TQ)
    nq = T // tq
    scale = HD ** -0.5

    def body(do_ref, o_ref, lse_ref, q_ref, k_ref, v_ref, cr_ref, dq_ref, dk_ref, dv_ref, dc_ref, dcq_ref):
        i = pl.program_id(1)

        @pl.when(i == 0)
        def _():
            dk_ref[...] = jnp.zeros_like(dk_ref)
            dv_ref[...] = jnp.zeros_like(dv_ref)
            dc_ref[...] = jnp.zeros_like(dc_ref)

        masks = _head_masks()
        dov = do_ref[...]
        qs2 = _stack_heads(q_ref[...] * scale, masks)
        do2 = _stack_heads(dov.astype(bf16), masks)
        prod = dov * o_ref[...]
        delta = jnp.concatenate([jnp.sum(jnp.where(m, prod, 0.0), axis=1, keepdims=True) for m in masks], axis=0)
        lse2 = _lane_cols2(lse_ref[...])

        def tile(kb, carry, diag):
            dq, rsum = carry
            off = pl.multiple_of(kb * tq, tq)
            kblk = k_ref[pl.ds(off, tq), :]
            p = jnp.exp(_sub_head_rows(_dot_nt(qs2, kblk), cr_ref[kb], tq) - lse2)
            if diag:
                p = jnp.where(_stacked_mask(tq, False), p, 0.0)
            dp = _dot_nt(do2, v_ref[pl.ds(off, tq), :])
            ds = p * (dp - delta)
            dsb = ds.astype(bf16)
            dq = dq + _dot(_side_by_side(dsb, tq), _stack_heads(kblk, masks))
            dk_ref[pl.ds(off, tq), :] += _dot_tn(dsb, qs2)
            dv_ref[pl.ds(off, tq), :] += _dot_tn(p.astype(bf16), do2)
            dc_ref[kb, 0:1, :] += -jnp.sum(ds[:tq], axis=0, keepdims=True)
            dc_ref[kb, 1:2, :] += -jnp.sum(ds[tq:], axis=0, keepdims=True)
            return dq, rsum + jnp.sum(ds, axis=1, keepdims=True)

        init = (jnp.zeros((tq, 128), f32), jnp.zeros((2 * tq, 1), f32))
        carry = lax.fori_loop(0, i, lambda kb, c: tile(kb, c, False), init)
        dq, rsum = tile(i, carry, True)
        dq_ref[...] = dq * scale
        dcq_ref[...] = _pack_cols2(rsum, tq)

    qs, ks, vs = _qkv_specs(T, tq, C_FOX // 128)
    tile_spec = pl.BlockSpec((tq, 128), lambda p, i: (i, p))
    pair = pl.BlockSpec((None, tq, 128), lambda p, i: (p, i, 0))
    rowsp = pl.BlockSpec((None, nq, 8, tq), lambda p, i: (p, 0, 0, 0))
    full = pl.BlockSpec((T, 128), lambda p, i: (0, p))
    return pl.pallas_call(
        body, grid=(2, nq),
        in_specs=[tile_spec, tile_spec, pair, qs, ks, vs, rowsp],
        out_specs=[tile_spec, full, full, rowsp, pair],
        out_shape=[_sds((T, GW)), _sds((T, GW)), _sds((T, GW)), _sds((2, nq, 8, tq)), _sds((2, T, 128))],
        compiler_params=_cparams(("parallel", "arbitrary")), name=name)(do, o, lse, pbf, pbf, pbf, cumrow)


def _tri(tq, pred):
    r = lax.broadcasted_iota(jnp.int32, (tq, tq), 0)
    c = lax.broadcasted_iota(jnp.int32, (tq, tq), 1)
    return pred(r, c).astype(bf16)


def sb_fwd(pbf, name):
    T = pbf.shape[0]
    tq = min(T, ATT_TQ)
    nq = T // tq
    scale = HD ** -0.5

    def body(q_ref, k_ref, v_ref, o_ref, tot_ref):
        i = pl.program_id(1)
        masks = _head_masks()
        qs2 = _stack_heads(q_ref[...] * scale, masks)
        after = _tri(tq, lambda r, c: r > c)

        def tile(kb, carry, diag):
            rs, acc = carry
            off = pl.multiple_of(kb * tq, tq)
            z = _dot_nt(qs2, k_ref[pl.ds(off, tq), :])
            lk = _neg_softplus(z)
            if diag:
                lk = jnp.where(_stacked_mask(tq, True), lk, 0.0)
            w = jnp.exp(z + lk + (_dot_hilo2(lk, after) + rs))
            if diag:
                w = jnp.where(_stacked_mask(tq, True), w, 0.0)
            acc = acc + _dot(_side_by_side(w.astype(bf16), tq), _stack_heads(v_ref[pl.ds(off, tq), :], masks))
            return rs + jnp.sum(lk, axis=1, keepdims=True), acc

        carry = tile(i, (jnp.zeros((2 * tq, 1), f32), jnp.zeros((tq, 128), f32)), True)
        rs, acc = lax.fori_loop(0, i, lambda n, c: tile(i - 1 - n, c, False), carry)
        o_ref[...] = acc
        tot_ref[...] = _pack_cols2(rs, tq)

    qs, ks, vs = _qkv_specs(T, tq, C_SB // 128)
    return pl.pallas_call(
        body, grid=(2, nq), in_specs=[qs, ks, vs],
        out_specs=[pl.BlockSpec((tq, 128), lambda p, i: (i, p)), pl.BlockSpec((None, tq, 128), lambda p, i: (p, i, 0))],
        out_shape=[_sds((T, GW)), _sds((2, T, 128))],
        compiler_params=_cparams(("parallel", "parallel")), name=name)(pbf, pbf, pbf)


def sb_bwd(do, tot, pbf, name):
    T = pbf.shape[0]
    tq = min(T, ATT_TQ)
    nq = T // tq
    scale = HD ** -0.5

    def body(do_ref, tot_ref, q_ref, k_ref, v_ref, dq_ref, dk_ref, dv_ref):
        i = pl.program_id(1)

        @pl.when(i == 0)
        def _():
            dk_ref[...] = jnp.zeros_like(dk_ref)
            dv_ref[...] = jnp.zeros_like(dv_ref)

        masks = _head_masks()
        qs2 = _stack_heads(q_ref[...] * scale, masks)
        do2 = _stack_heads(do_ref[...].astype(bf16), masks)
        tot2 = _lane_cols2(tot_ref[...])
        upto = _tri(tq, lambda r, c: r <= c)
        before = _tri(tq, lambda r, c: r < c)

        def tile(kb, carry, diag):
            pre, cg, dq = carry
            off = pl.multiple_of(kb * tq, tq)
            kblk = k_ref[pl.ds(off, tq), :]
            z = _dot_nt(qs2, kblk)
            lk = _neg_softplus(z)
            keep = jnp.exp(lk)
            if diag:
                lk = jnp.where(_stacked_mask(tq, True), lk, 0.0)
            w = jnp.exp(z + lk + (tot2 - (pre + _dot_hilo2(lk, upto))))
            if diag:
                w = jnp.where(_stacked_mask(tq, True), w, 0.0)
            gmat = w * _dot_nt(do2, v_ref[pl.ds(off, tq), :])
            cmat = cg + _dot_hilo2(gmat, before)
            dz = gmat * keep - cmat * (1.0 - keep)
            if diag:
                dz = jnp.where(_stacked_mask(tq, True), dz, 0.0)
            dzb = dz.astype(bf16)
            dq = dq + _dot(_side_by_side(dzb, tq), _stack_heads(kblk, masks))
            dk_ref[pl.ds(off, tq), :] += _dot_tn(dzb, qs2)
            dv_ref[pl.ds(off, tq), :] += _dot_tn(w.astype(bf16), do2)
            return pre + jnp.sum(lk, axis=1, keepdims=True), cg + jnp.sum(gmat, axis=1, keepdims=True), dq

        zc = jnp.zeros((2 * tq, 1), f32)
        carry = lax.fori_loop(0, i, lambda kb, c: tile(kb, c, False), (zc, zc, jnp.zeros((tq, 128), f32)))
        _, _, dq = tile(i, carry, True)
        dq_ref[...] = dq * scale

    qs, ks, vs = _qkv_specs(T, tq, C_SB // 128)
    tile_spec = pl.BlockSpec((tq, 128), lambda p, i: (i, p))
    pair = pl.BlockSpec((None, tq, 128), lambda p, i: (p, i, 0))
    full = pl.BlockSpec((T, 128), lambda p, i: (0, p))
    return pl.pallas_call(
        body, grid=(2, nq), in_specs=[tile_spec, pair, qs, ks, vs],
        out_specs=[tile_spec, full, full], out_shape=[_sds((T, GW))] * 3,
        compiler_params=_cparams(("parallel", "arbitrary")), name=name)(do, tot, pbf, pbf, pbf)


MEM_HD = D // 4


def mem_fwd(q, kv, name):
    T = q.shape[0]
    M = kv.shape[0]
    tm = min(T, 512)
    scale = MEM_HD ** -0.5

    def body(q_ref, kv_ref, o_ref):
        for h in range(4):
            sl = slice(h * MEM_HD, (h + 1) * MEM_HD)
            kh = kv_ref[:, sl].astype(bf16)
            vh = kv_ref[:, D + h * MEM_HD:D + (h + 1) * MEM_HD].astype(bf16)
            s = _dot_nt(q_ref[:, sl], kh) * scale
            e = jnp.exp(s - jnp.max(s, axis=1, keepdims=True))
            p = e / jnp.sum(e, axis=1, keepdims=True)
            o_ref[:, sl] = _dot(p.astype(bf16), vh).astype(bf16)

    return pl.pallas_call(
        body, grid=(T // tm,),
        in_specs=[pl.BlockSpec((tm, D), lambda i: (i, 0)), pl.BlockSpec((M, 2 * D), lambda i: (0, 0))],
        out_specs=pl.BlockSpec((tm, D), lambda i: (i, 0)), out_shape=_sds((T, D), bf16),
        compiler_params=_cparams(("parallel",)), name=name)(q, kv)


def mem_bwd(do, q, kv, name):
    T = q.shape[0]
    M = kv.shape[0]
    tm = min(T, 512)
    scale = MEM_HD ** -0.5

    def body(do_ref, q_ref, kv_ref, dq_ref, dkv_ref):
        @pl.when(pl.program_id(0) == 0)
        def _():
            dkv_ref[...] = jnp.zeros_like(dkv_ref)
        for h in range(4):
            sl = slice(h * MEM_HD, (h + 1) * MEM_HD)
            vsl = slice(D + h * MEM_HD, D + (h + 1) * MEM_HD)
            qh = q_ref[:, sl]
            kh = kv_ref[:, sl].astype(bf16)
            vh = kv_ref[:, vsl].astype(bf16)
            doh = do_ref[:, sl].astype(bf16)
            s = _dot_nt(qh, kh) * scale
            e = jnp.exp(s - jnp.max(s, axis=1, keepdims=True))
            p = e / jnp.sum(e, axis=1, keepdims=True)
            dp = _dot_nt(doh, vh)
            ds = p * (dp - jnp.sum(dp * p, axis=1, keepdims=True))
            dsb = ds.astype(bf16)
            dq_ref[:, sl] = _dot(dsb, kh) * scale
            dkv_ref[:, sl] += _dot_tn(dsb, qh) * scale
            dkv_ref[:, vsl] += _dot_tn(p.astype(bf16), doh)

    row = pl.BlockSpec((tm, D), lambda i: (i, 0))
    whole = pl.BlockSpec((M, 2 * D), lambda i: (0, 0))
    return pl.pallas_call(
        body, grid=(T // tm,), in_specs=[row, row, whole], out_specs=[row, whole],
        out_shape=[_sds((T, D)), _sds((M, 2 * D))],
        compiler_params=_cparams(("arbitrary",)), name=name)(do, q, kv)


def _chip_peers():
    x, y, c = lax.axis_index("x"), lax.axis_index("y"), lax.axis_index("c")
    me = 2 * x + y
    peers = [((1 - x, y, c), 2 * (1 - x) + y), ((x, 1 - y, c), 2 * x + (1 - y)), ((1 - x, 1 - y, c), 2 * (1 - x) + (1 - y))]
    return me, peers


def gather_chips(arrs, name):
    n = len(arrs)

    def body(*refs):
        ins, outs = refs[:n], refs[n:2 * n]
        send_sems, recv_sems, loc_sems = refs[2 * n:]
        me, peers = _chip_peers()
        started = []
        for i in range(n):
            loc = pltpu.make_async_copy(ins[i], outs[i].at[me], loc_sems.at[i])
            loc.start()
            started.append(loc)
            for k, (dev, _) in enumerate(peers):
                cp = pltpu.make_async_remote_copy(src_ref=ins[i], dst_ref=outs[i].at[me], send_sem=send_sems.at[i, k],
                                                  recv_sem=recv_sems.at[i, k], device_id=dev, device_id_type=MESH)
                cp.start()
                started.append(cp)
        for cp in started:
            cp.wait()

    anyspec = pl.BlockSpec(memory_space=pl.ANY)
    return pl.pallas_call(
        body, in_specs=[anyspec] * n, out_specs=[anyspec] * n,
        out_shape=[_sds((4,) + a.shape, a.dtype) for a in arrs],
        scratch_shapes=[pltpu.SemaphoreType.DMA((n, 3)), pltpu.SemaphoreType.DMA((n, 3)), pltpu.SemaphoreType.DMA((n,))],
        name=name)(*arrs)


def exchange_chips(stacks, name):
    n = len(stacks)

    def body(*refs):
        ins, outs = refs[:n], refs[n:2 * n]
        send_sems, recv_sems, loc_sems = refs[2 * n:]
        me, peers = _chip_peers()
        started = []
        for i in range(n):
            loc = pltpu.make_async_copy(ins[i].at[me], outs[i].at[me], loc_sems.at[i])
            loc.start()
            started.append(loc)
            for k, (dev, pj) in enumerate(peers):
                cp = pltpu.make_async_remote_copy(src_ref=ins[i].at[pj], dst_ref=outs[i].at[me], send_sem=send_sems.at[i, k],
                                                  recv_sem=recv_sems.at[i, k], device_id=dev, device_id_type=MESH)
                cp.start()
                started.append(cp)
        for cp in started:
            cp.wait()

    anyspec = pl.BlockSpec(memory_space=pl.ANY)
    return pl.pallas_call(
        body, in_specs=[anyspec] * n, out_specs=[anyspec] * n,
        out_shape=[_sds(a.shape, a.dtype) for a in stacks],
        scratch_shapes=[pltpu.SemaphoreType.DMA((n, 3)), pltpu.SemaphoreType.DMA((n, 3)), pltpu.SemaphoreType.DMA((n,))],
        name=name)(*stacks)


def swap_sibling(arrs, name):
    n = len(arrs)

    def body(*refs):
        ins, outs = refs[:n], refs[n:2 * n]
        send_sems, recv_sems = refs[2 * n:]
        x, y, c = lax.axis_index("x"), lax.axis_index("y"), lax.axis_index("c")
        started = []
        for i in range(n):
            cp = pltpu.make_async_remote_copy(src_ref=ins[i], dst_ref=outs[i], send_sem=send_sems.at[i],
                                              recv_sem=recv_sems.at[i], device_id=(x, y, 1 - c), device_id_type=MESH)
            cp.start()
            started.append(cp)
        for cp in started:
            cp.wait()

    anyspec = pl.BlockSpec(memory_space=pl.ANY)
    return pl.pallas_call(
        body, in_specs=[anyspec] * n, out_specs=[anyspec] * n,
        out_shape=[_sds(a.shape, a.dtype) for a in arrs],
        scratch_shapes=[pltpu.SemaphoreType.DMA((n,)), pltpu.SemaphoreType.DMA((n,))],
        name=name)(*arrs)


def gather_all(a, name):
    def body(a_ref, o_ref, send_sems, recv_sems, loc_sem):
        x, y, c = lax.axis_index("x"), lax.axis_index("y"), lax.axis_index("c")
        me = 4 * x + 2 * y + c
        loc = pltpu.make_async_copy(a_ref, o_ref.at[me], loc_sem)
        loc.start()
        started = [loc]
        for k in range(1, 8):
            dev = (x ^ (k >> 2), y ^ ((k >> 1) & 1), c ^ (k & 1))
            cp = pltpu.make_async_remote_copy(src_ref=a_ref, dst_ref=o_ref.at[me], send_sem=send_sems.at[k - 1],
                                              recv_sem=recv_sems.at[k - 1], device_id=dev, device_id_type=MESH)
            cp.start()
            started.append(cp)
        for cp in started:
            cp.wait()

    anyspec = pl.BlockSpec(memory_space=pl.ANY)
    return pl.pallas_call(
        body, in_specs=[anyspec], out_specs=anyspec, out_shape=_sds((8,) + a.shape, a.dtype),
        scratch_shapes=[pltpu.SemaphoreType.DMA((7,)), pltpu.SemaphoreType.DMA((7,)), pltpu.SemaphoreType.DMA(())],
        name=name)(a)


def sum_slots(stack, name):
    n, R, C = stack.shape
    tr = R if R <= 512 else _pick(R, (512, 352, 256))

    def body(s_ref, o_ref):
        acc = s_ref[0].astype(f32)
        for j in range(1, n):
            acc = acc + s_ref[j].astype(f32)
        o_ref[...] = acc

    return pl.pallas_call(
        body, grid=(R // tr,), in_specs=[pl.BlockSpec((n, tr, C), lambda i: (0, i, 0))],
        out_specs=pl.BlockSpec((tr, C), lambda i: (i, 0)), out_shape=_sds((R, C)),
        compiler_params=_cparams(("parallel",)), name=name)(stack)


def adamw(w, g1, g2, m, v, name):
    R, C = w.shape
    tr = R if R <= 512 else _pick(R, (512, 352, 256))
    c1 = 1.0 - ADAM_B1 ** ADAM_STEP
    c2 = 1.0 - ADAM_B2 ** ADAM_STEP

    def body(*refs):
        if g2 is None:
            w_ref, g1_ref, m_ref, v_ref, g_out, d_out, m_out, v_out = refs
            g = g1_ref[...]
        else:
            w_ref, g1_ref, g2_ref, m_ref, v_ref, g_out, d_out, m_out, v_out = refs
            g = g1_ref[...] + g2_ref[...]
        mn = ADAM_B1 * m_ref[...] + (1.0 - ADAM_B1) * g
        vn = ADAM_B2 * v_ref[...] + (1.0 - ADAM_B2) * (g * g)
        g_out[...] = g
        m_out[...] = mn
        v_out[...] = vn
        d_out[...] = -ADAM_LR * ((mn / c1) / (jnp.sqrt(vn / c2) + ADAM_EPS) + ADAM_WD * w_ref[...])

    blk = pl.BlockSpec((tr, C), lambda i: (i, 0))
    args = [w, g1] + ([] if g2 is None else [g2]) + [m, v]
    return pl.pallas_call(
        body, grid=(R // tr,), in_specs=[blk] * len(args), out_specs=[blk] * 4, out_shape=[_sds((R, C))] * 4,
        compiler_params=_cparams(("parallel",)), name=name)(*args)


IN_SPLITS = (768, 256, 4, 4, 768, 4, 512, 768)
IN_OFF = (0, 768, 1024, 1028, 1032, 1800, 1804, 2316, 3084)


def regroup_w_in(w):
    seg = lambda i: w[:, IN_OFF[i]:IN_OFF[i + 1]]
    pad = jnp.zeros((w.shape[0], PW - C_SMALL - 12), w.dtype)
    return jnp.concatenate([seg(0), seg(1), seg(4), seg(6), seg(7), seg(2), seg(3), seg(5), pad], axis=1)


def ungroup_w_in(g):
    s = C_SMALL
    return jnp.concatenate([g[:, 0:1024], g[:, s:s + 8], g[:, 1024:1792], g[:, s + 8:s + 12], g[:, 1792:3072]], axis=1)


def to_hm(t, nh):
    T = t.shape[0]
    return t.reshape(T, nh, HD).transpose(1, 0, 2)


def from_hm(t):
    nh, T, _ = t.shape
    return t.transpose(1, 0, 2).reshape(T, nh * HD)


def col_shards(w):
    c = w.shape[-1] // 4
    return jnp.moveaxis(w.reshape(w.shape[:-1] + (4, c)), -2, 0)


def row_shards(w):
    L, r4, c = w.shape
    return w.reshape(L, 4, r4 // 4, c).transpose(1, 0, 2, 3)


def join_cols(g):
    return jnp.moveaxis(g, 0, -2).reshape(g.shape[1:-1] + (4 * g.shape[-1],))


def join_rows(g):
    _, L, r, c = g.shape
    return g.transpose(1, 0, 2, 3).reshape(L, 4 * r, c)


COL_SHARDED = ("ffn1_w_gate", "ffn1_w_up", "w_in", "gdn_conv_w", "conf_dw_w", "mem_w_kv", "ffn2_w_gate", "ffn2_w_up")
CONV_WEIGHTS = ("gdn_conv_w", "conf_dw_w")
ROW_SHARDED = ("ffn1_w_down", "w_out", "mem_w_q", "mem_w_o", "ffn2_w_down")
REPLICATED = ("ln_ffn1_g", "ln_ffn1_b", "gdn_a_log", "gdn_dt_bias", "gdn_norm_g", "fox_b_f", "conf_dw_b", "conf_norm_g",
              "conf_norm_b", "ln_mix_g", "ln_mix_b", "ln_mem_g", "ln_mem_b", "ln_ffn2_g", "ln_ffn2_b")
WEIGHTS = ("ffn1_w_gate", "ffn1_w_up", "ffn1_w_down", "ln_ffn1_g", "ln_ffn1_b", "w_in", "gdn_conv_w", "gdn_a_log",
           "gdn_dt_bias", "gdn_norm_g", "fox_b_f", "conf_dw_w", "conf_dw_b", "conf_norm_g", "conf_norm_b", "w_out",
           "ln_mix_g", "ln_mix_b", "mem_w_q", "mem_w_kv", "mem_w_o", "ln_mem_g", "ln_mem_b", "ffn2_w_gate",
           "ffn2_w_up", "ffn2_w_down", "ln_ffn2_g", "ln_ffn2_b")


def pack_small(d):
    flat = jnp.concatenate([d[n].reshape(-1) for n in REPLICATED])
    rows = -(-flat.shape[0] // 1024) * 8
    return jnp.pad(flat, (0, rows * 128 - flat.shape[0])).reshape(rows, 128)


def unpack_small(p, like):
    flat = p.reshape(-1)
    out, o = {}, 0
    for n in REPLICATED:
        sz = like[n].size
        out[n] = flat[o:o + sz].reshape(like[n].shape)
        o += sz
    return out


def _vec(v):
    return v.reshape(1, -1)


def _pad_rows(w, rows):
    return jnp.pad(w, ((0, rows - w.shape[0]), (0, 0)))


def _small_lane_vec(v4, lane0):
    return jnp.pad(v4.reshape(1, -1), ((0, 0), (lane0, 128 - lane0 - v4.shape[0])))


def layer_fwd(x0, mem, W, li):
    T = x0.shape[0]
    tq = min(T, ATT_TQ)
    nq = T // tq
    n = lambda s: f"l{li}_{s}"
    R = {"x0": x0}
    R["z1"], x1 = ffn_fwd(x0, W["ffn1_w_gate"], W["ffn1_w_up"], W["ffn1_w_down"], _vec(W["ln_ffn1_g"]), _vec(W["ln_ffn1_b"]), n("ffn1_fwd"))
    R["x1"] = x1
    proj, pbf = mm_nn(x1, W["w_in_r"], n("proj"), also_bf16=True)
    R["proj"], R["pbf"] = proj, pbf

    w8 = _pad_rows(W["gdn_conv_w"], 8)
    qkv_s = gdn_conv_fwd(proj, w8, n("gdn_conv_fwd"))
    qkv_hm = to_hm(qkv_s, 12)
    z_hm = to_hm(proj[:, C_GZ:C_GZ + GW], 4)
    ab_hm = proj[:, C_SMALL:C_SMALL + 8].T.reshape(8, T, 1)
    alog = W["gdn_a_log"].reshape(NH, 1, 1)
    dtb = W["gdn_dt_bias"].reshape(NH, 1, 1)
    ng = W["gdn_norm_g"].reshape(1, 1, HD)
    ya_hm, states = gdn_fwd(qkv_hm, z_hm, ab_hm, alog, dtb, ng, n("gdn_fwd"))
    R.update(qkv_hm=qkv_hm, z_hm=z_hm, ab_hm=ab_hm, states=states)

    bfv = _small_lane_vec(W["fox_b_f"], F_LANE)
    cum = fox_gate_fwd(proj, bfv, n("fox_gate_fwd"))
    cum4 = cum[:, F_LANE:F_LANE + NH]
    cumrow = jnp.pad(cum4.T.reshape(2, 2, nq, tq).transpose(0, 2, 1, 3), ((0, 0), (0, 0), (0, 6), (0, 0)))
    yb, lse = fox_fwd(pbf, cumrow, n("fox_fwd"))
    R.update(cumrow=cumrow, yb=yb, lse=lse)

    w32 = _pad_rows(W["conf_dw_w"], 32)
    yc, cc = conf_fwd(proj, w32, _vec(W["conf_dw_b"]), _vec(W["conf_norm_g"]), _vec(W["conf_norm_b"]), n("conf_fwd"))
    R["cc"] = cc

    yd, tot = sb_fwd(pbf, n("sb_fwd"))
    R["tot"] = tot

    ycat = jnp.concatenate([from_hm(ya_hm), yb, yc, yd], axis=1).astype(bf16)
    R["ycat"] = ycat
    R["z2"], x2 = lin_res_ln(ycat, W["w_out"], x1, _vec(W["ln_mix_g"]), _vec(W["ln_mix_b"]), n("mix_out"))
    R["x2"] = x2

    qm = mm_nn(x2, W["mem_w_q"], n("mem_q"), out_dtype=bf16)
    kv = mm_nn(mem, W["mem_w_kv"], n("mem_kv"))
    om = mem_fwd(qm, kv, n("mem_fwd"))
    R.update(qm=qm, kv=kv, om=om)
    R["z3"], x3 = lin_res_ln(om, W["mem_w_o"], x2, _vec(W["ln_mem_g"]), _vec(W["ln_mem_b"]), n("mem_out"))
    R["x3"] = x3
    R["z4"], x4 = ffn_fwd(x3, W["ffn2_w_gate"], W["ffn2_w_up"], W["ffn2_w_down"], _vec(W["ln_ffn2_g"]), _vec(W["ln_ffn2_b"]), n("ffn2_fwd"))
    return x4, R


def layer_bwd(dx4, mem, W, R, li):
    T = dx4.shape[0]
    n = lambda s: f"l{li}_{s}"
    G = {}

    def ffn_back(dy, z, x, pre, tag):
        dz, dg, db = ln_bwd(dy, z, _vec(W[f"ln_{pre}_g"]), n(f"{tag}_ln_bwd"))
        dx, a, dh, du, dzh = ffn_bwd(dz, x, W[f"{pre}_w_gate"], W[f"{pre}_w_up"], W[f"{pre}_w_down"], n(f"{tag}_bwd"))
        G[f"{pre}_w_gate"] = mm_tn(x, dh, n(f"{tag}_dwg"))
        G[f"{pre}_w_up"] = mm_tn(x, du, n(f"{tag}_dwu"))
        G[f"{pre}_w_down"] = mm_tn(a, dzh, n(f"{tag}_dwd"))
        G[f"ln_{pre}_g"], G[f"ln_{pre}_b"] = dg.reshape(-1), db.reshape(-1)
        return dx

    dx3 = ffn_back(dx4, R["z4"], R["x3"], "ffn2", "ffn2")

    dz3, dg, db = ln_bwd(dx3, R["z3"], _vec(W["ln_mem_g"]), n("mem_ln_bwd"))
    G["ln_mem_g"], G["ln_mem_b"] = dg.reshape(-1), db.reshape(-1)
    dom = mm_nt(dz3, W["mem_w_o"], n("mem_dom"))
    G["mem_w_o"] = mm_tn(R["om"], dz3, n("mem_dwo"))
    dqm, dkv = mem_bwd(dom, R["qm"], R["kv"], n("mem_bwd"))
    G["mem_w_q"] = mm_tn(R["x2"], dqm, n("mem_dwq"))
    G["mem_w_kv"] = mm_tn(mem, dkv, n("mem_dwkv"))
    dx2 = mm_nt(dqm, W["mem_w_q"], n("mem_dx"), add=dz3, add_scale=ALPHA)

    dz2, dg, db = ln_bwd(dx2, R["z2"], _vec(W["ln_mix_g"]), n("mix_ln_bwd"))
    G["ln_mix_g"], G["ln_mix_b"] = dg.reshape(-1), db.reshape(-1)
    dycat = mm_nt(dz2, W["w_out"], n("mix_dycat"))
    G["w_out"] = mm_tn(R["ycat"], dz2, n("mix_dwout"))
    dya, dyb, dyc, dyd = (dycat[:, i * GW:(i + 1) * GW] for i in range(4))
    proj, pbf = R["proj"], R["pbf"]

    alog = W["gdn_a_log"].reshape(NH, 1, 1)
    dtb = W["gdn_dt_bias"].reshape(NH, 1, 1)
    ng = W["gdn_norm_g"].reshape(1, 1, HD)
    dqkv_hm, dz_hm, dab_hm, dalog, ddtb, dng = gdn_bwd(to_hm(dya, 4), R["states"], R["qkv_hm"], R["z_hm"], R["ab_hm"],
                                                      alog, dtb, ng, n("gdn_bwd"))
    G["gdn_a_log"], G["gdn_dt_bias"], G["gdn_norm_g"] = dalog.reshape(-1), ddtb.reshape(-1), dng.reshape(-1)
    w8 = _pad_rows(W["gdn_conv_w"], 8)
    dgqkv, dw8 = gdn_conv_bwd(from_hm(dqkv_hm), proj, w8, n("gdn_conv_bwd"))
    G["gdn_conv_w"] = dw8[:GDN_K]

    dfq, dfk, dfv, dcumrow, dcumq = fox_bwd(dyb, R["yb"], R["lse"], pbf, R["cumrow"], n("fox_bwd"))
    dcum4 = dcumrow[:, :, 0:2, :].transpose(0, 2, 1, 3).reshape(4, T).T
    dcum4 = dcum4 + dcumq[:, :, 0:2].transpose(1, 0, 2).reshape(T, 4)
    dcum = jnp.pad(dcum4, ((0, 0), (F_LANE, 128 - F_LANE - NH)))
    bfv = _small_lane_vec(W["fox_b_f"], F_LANE)
    dsmall_f, dbf = fox_gate_bwd(dcum, proj, bfv, n("fox_gate_bwd"))
    G["fox_b_f"] = dbf[0, F_LANE:F_LANE + NH]

    w32 = _pad_rows(W["conf_dw_w"], 32)
    dglu, dw32, dcb, dcg, dcbeta = conf_bwd(dyc, R["cc"], proj, w32, _vec(W["conf_norm_g"]), _vec(W["conf_norm_b"]), n("conf_bwd"))
    G["conf_dw_w"], G["conf_dw_b"] = dw32[:CONF_K], dcb.reshape(-1)
    G["conf_norm_g"], G["conf_norm_b"] = dcg.reshape(-1), dcbeta.reshape(-1)

    dsq, dsk, dsv = sb_bwd(dyd, R["tot"], pbf, n("sb_bwd"))

    dsmall = jnp.concatenate([dab_hm.reshape(8, T).T, dsmall_f[:, F_LANE:F_LANE + NH], jnp.zeros((T, PW - C_SMALL - 12), f32)], axis=1)
    dproj = jnp.concatenate([dgqkv, from_hm(dz_hm), dfq, dfk, dfv, dglu, dsq, dsk, dsv, dsmall], axis=1).astype(bf16)
    G["w_in_r"] = mm_tn(R["x1"], dproj, n("proj_dw"))
    dx1 = mm_nt(dproj, W["w_in_r"], n("proj_dx"), add=dz2, add_scale=ALPHA)

    dx0 = ffn_back(dx1, R["z1"], R["x0"], "ffn1", "ffn1")
    return dx0, G


def _step(P, M, V, x, mem, loss_target):
    xs, mems, tgt = x[0], mem[0], loss_target[0]

    big = COL_SHARDED + ROW_SHARDED
    gathered = gather_chips([P[k] if k in CONV_WEIGHTS else P[k].astype(bf16) for k in big], "gather_weights")
    full = {}
    for k, g in zip(big, gathered):
        full[k] = join_cols(g) if k in COL_SHARDED else join_rows(g)

    layers = []
    for li in range(DEPTH):
        W = {k: full[k][li] for k in big}
        W.update({k: P[k][li] for k in REPLICATED})
        W["w_in_r"] = regroup_w_in(W["w_in"])
        layers.append(W)

    h = xs
    saved = []
    for li in range(DEPTH):
        h, R = layer_fwd(h, mems, layers[li], li)
        saved.append(R)
    loss_row, dy = loss_and_grad(h, tgt, "loss")
    loss = lax.psum(loss_row[0, 0], ("x", "y", "c"))

    grads = [None] * DEPTH
    for li in reversed(range(DEPTH)):
        dy, G = layer_bwd(dy, mems, layers[li], saved[li], li)
        G["w_in"] = ungroup_w_in(G.pop("w_in_r"))
        grads[li] = G
    grad_x = dy[None]

    stacked = {k: jnp.stack([grads[li][k] for li in range(DEPTH)]) for k in WEIGHTS}
    slots = [col_shards(stacked[k]) if k in COL_SHARDED else row_shards(stacked[k]) for k in big]
    slots = [s if k in CONV_WEIGHTS else s.astype(bf16) for k, s in zip(big, slots)]
    received = exchange_chips(slots, "exchange_grads")
    partial_sums = []
    for k, r in zip(big, received):
        shp = r.shape
        partial_sums.append(sum_slots(r.reshape(4, -1, shp[-1]), f"sum_{k}"))
    from_sibling = swap_sibling(partial_sums, "swap_partials")

    out_g, out_d, out_m, out_v = {}, {}, {}, {}
    for k, mine, theirs in zip(big, partial_sums, from_sibling):
        shp = P[k].shape
        flat = lambda t: t.reshape(-1, shp[-1])
        g, d, mn, vn = adamw(flat(P[k]), mine, theirs, flat(M[k]), flat(V[k]), f"adamw_{k}")
        out_g[k], out_d[k], out_m[k], out_v[k] = (t.reshape(shp) for t in (g, d, mn, vn))

    gsmall = sum_slots(gather_all(pack_small(stacked), "gather_small"), "sum_small")
    g, d, mn, vn = adamw(pack_small(P), gsmall, None, pack_small(M), pack_small(V), "adamw_small")
    for dst, packed in ((out_g, g), (out_d, d), (out_m, mn), (out_v, vn)):
        dst.update(unpack_small(packed, P))

    return (loss, grad_x, *[out_g[k] for k in WEIGHTS], *[out_d[k] for k in WEIGHTS],
            *[out_m[k] for k in WEIGHTS], *[out_v[k] for k in WEIGHTS])


def kernel(x, mem, ffn1_w_gate, ffn1_w_up, ffn1_w_down, ln_ffn1_g, ln_ffn1_b, w_in, gdn_conv_w, gdn_a_log, gdn_dt_bias, gdn_norm_g, fox_b_f, conf_dw_w, conf_dw_b, conf_norm_g, conf_norm_b, w_out, ln_mix_g, ln_mix_b, mem_w_q, mem_w_kv, mem_w_o, ln_mem_g, ln_mem_b, ffn2_w_gate, ffn2_w_up, ffn2_w_down, ln_ffn2_g, ln_ffn2_b, loss_target, m_ffn1_w_gate, m_ffn1_w_up, m_ffn1_w_down, m_ln_ffn1_g, m_ln_ffn1_b, m_w_in, m_gdn_conv_w, m_gdn_a_log, m_gdn_dt_bias, m_gdn_norm_g, m_fox_b_f, m_conf_dw_w, m_conf_dw_b, m_conf_norm_g, m_conf_norm_b, m_w_out, m_ln_mix_g, m_ln_mix_b, m_mem_w_q, m_mem_w_kv, m_mem_w_o, m_ln_mem_g, m_ln_mem_b, m_ffn2_w_gate, m_ffn2_w_up, m_ffn2_w_down, m_ln_ffn2_g, m_ln_ffn2_b, v_ffn1_w_gate, v_ffn1_w_up, v_ffn1_w_down, v_ln_ffn1_g, v_ln_ffn1_b, v_w_in, v_gdn_conv_w, v_gdn_a_log, v_gdn_dt_bias, v_gdn_norm_g, v_fox_b_f, v_conf_dw_w, v_conf_dw_b, v_conf_norm_g, v_conf_norm_b, v_w_out, v_ln_mix_g, v_ln_mix_b, v_mem_w_q, v_mem_w_kv, v_mem_w_o, v_ln_mem_g, v_ln_mem_b, v_ffn2_w_gate, v_ffn2_w_up, v_ffn2_w_down, v_ln_ffn2_g, v_ln_ffn2_b):
    a = locals()
    P = {k: a[k] for k in WEIGHTS}
    M = {k: a["m_" + k] for k in WEIGHTS}
    V = {k: a["v_" + k] for k in WEIGHTS}
    return _step(P, M, V, x, mem, loss_target)
```

```python
import functools

import jax
import jax.numpy as jnp
from jax import lax
from jax.experimental import pallas as pl
from jax.experimental.pallas import tpu as pltpu

f32 = jnp.float32
bf16 = jnp.bfloat16

D = 1024
F = 2816
GW = 256
HD = 64
NH = 4
CHUNK = 64
CONF_K = 31
GDN_K = 4
DEPTH = 2
ALPHA = float((2 * DEPTH) ** 0.25)
LN_EPS = 1e-5
RMS_EPS = 1e-6
L2_EPS = 1e-6
NEG = -1e30
PW = 3200
C_GQKV, C_GZ, C_FOX, C_CONF, C_SB, C_SMALL = 0, 768, 1024, 1792, 2304, 3072
ADAM_LR, ADAM_B1, ADAM_B2, ADAM_EPS, ADAM_WD, ADAM_STEP = 0.001, 0.9, 0.999, 1e-08, 0.01, 10
VMEM_LIMIT = 56 * 1024 * 1024
MESH = pl.DeviceIdType.MESH


def _cparams(sem):
    return pltpu.CompilerParams(dimension_semantics=sem, vmem_limit_bytes=VMEM_LIMIT)


def _pick(n, cands):
    for c in cands:
        if n % c == 0:
            return c
    return n


def _sds(shape, dtype=f32):
    return jax.ShapeDtypeStruct(shape, dtype)


def _layer_norm(z, g, b):
    mu = jnp.mean(z, axis=-1, keepdims=True)
    zc = z - mu
    var = jnp.mean(zc * zc, axis=-1, keepdims=True)
    return zc * lax.rsqrt(var + LN_EPS) * g + b


def _softplus(x):
    return jnp.maximum(x, 0.0) + jnp.log(1.0 + jnp.exp(-jnp.abs(x)))


def _neg_softplus(z):
    nz = -z
    return jnp.minimum(nz, 0.0) - jnp.log(1.0 + jnp.exp(jnp.minimum(z, nz)))


def _dsilu(x):
    s = jax.nn.sigmoid(x)
    return s * (1.0 + x * (1.0 - s))


def _split_hi_lo(x):
    hi = x.astype(bf16)
    lo = (x - hi.astype(f32)).astype(bf16)
    return hi, lo


def _dot(a, b):
    return jnp.dot(a, b, preferred_element_type=f32)


def _dot_nt(a, b):
    return lax.dot_general(a, b, (((1,), (1,)), ((), ())), preferred_element_type=f32)


def _dot_tn(a, b):
    return lax.dot_general(a, b, (((0,), (0,)), ((), ())), preferred_element_type=f32)


def mm_nn(a, w, name, out_dtype=f32, also_bf16=False):
    T, K = a.shape
    N = w.shape[1]
    tm = min(T, 512)
    tn = N if N <= 1024 else _pick(N, (640, 512))

    def body(a_ref, w_ref, *o_refs):
        r = _dot(a_ref[...].astype(bf16), w_ref[...].astype(bf16))
        o_refs[0][...] = r.astype(o_refs[0].dtype)
        if also_bf16:
            o_refs[1][...] = r.astype(bf16)

    out_shape = [_sds((T, N), out_dtype)]
    out_specs = [pl.BlockSpec((tm, tn), lambda i, j: (i, j))]
    if also_bf16:
        out_shape.append(_sds((T, N), bf16))
        out_specs.append(pl.BlockSpec((tm, tn), lambda i, j: (i, j)))
    res = pl.pallas_call(
        body, grid=(T // tm, N // tn),
        in_specs=[pl.BlockSpec((tm, K), lambda i, j: (i, 0)), pl.BlockSpec((K, tn), lambda i, j: (0, j))],
        out_specs=out_specs, out_shape=out_shape,
        compiler_params=_cparams(("parallel", "arbitrary")), name=name)(a, w)
    return res if also_bf16 else res[0]


def mm_nt(g, w, name, add=None, add_scale=1.0):
    T, N = g.shape
    K = w.shape[0]
    tm = min(T, 512)

    def body(*refs):
        if add is None:
            g_ref, w_ref, o_ref = refs
        else:
            g_ref, w_ref, add_ref, o_ref = refs
        r = _dot_nt(g_ref[...].astype(bf16), w_ref[...].astype(bf16))
        if add is not None:
            r = r + add_scale * add_ref[...]
        o_ref[...] = r

    in_specs = [pl.BlockSpec((tm, N), lambda i: (i, 0)), pl.BlockSpec((K, N), lambda i: (0, 0))]
    args = [g, w]
    if add is not None:
        in_specs.append(pl.BlockSpec((tm, K), lambda i: (i, 0)))
        args.append(add)
    return pl.pallas_call(
        body, grid=(T // tm,), in_specs=in_specs,
        out_specs=pl.BlockSpec((tm, K), lambda i: (i, 0)), out_shape=_sds((T, K)),
        compiler_params=_cparams(("parallel",)), name=name)(*args)


def mm_tn(a, g, name):
    T, K = a.shape
    N = g.shape[1]
    tt = min(T, 512)
    tk = K if K * N * 4 <= 8 * 1024 * 1024 else _pick(K, (512, 1408))

    def body(a_ref, g_ref, o_ref):
        @pl.when(pl.program_id(1) == 0)
        def _():
            o_ref[...] = jnp.zeros_like(o_ref)
        o_ref[...] += _dot_tn(a_ref[...].astype(bf16), g_ref[...].astype(bf16))

    return pl.pallas_call(
        body, grid=(K // tk, T // tt),
        in_specs=[pl.BlockSpec((tt, tk), lambda j, t: (t, j)), pl.BlockSpec((tt, N), lambda j, t: (t, 0))],
        out_specs=pl.BlockSpec((tk, N), lambda j, t: (j, 0)), out_shape=_sds((K, N)),
        compiler_params=_cparams(("parallel", "arbitrary")), name=name)(a, g)


FFN_TF = 1408


def ffn_fwd(x, wg, wu, wd, g, b, name):
    T = x.shape[0]
    tm = min(T, 512)
    nf = F // FFN_TF

    def body(x_ref, wg_ref, wu_ref, wd_ref, g_ref, b_ref, z_ref, y_ref, acc):
        j = pl.program_id(1)

        @pl.when(j == 0)
        def _():
            acc[...] = jnp.zeros_like(acc)

        xb = x_ref[...].astype(bf16)
        h = _dot(xb, wg_ref[...])
        u = _dot(xb, wu_ref[...])
        a = (h * jax.nn.sigmoid(h) * u).astype(bf16)
        acc[...] += _dot(a, wd_ref[...])

        @pl.when(j == nf - 1)
        def _():
            z = ALPHA * x_ref[...] + 0.5 * acc[...]
            z_ref[...] = z
            y_ref[...] = _layer_norm(z, g_ref[...], b_ref[...])

    row = pl.BlockSpec((tm, D), lambda i, j: (i, 0))
    vec = pl.BlockSpec((1, D), lambda i, j: (0, 0))
    return pl.pallas_call(
        body, grid=(T // tm, nf),
        in_specs=[row, pl.BlockSpec((D, FFN_TF), lambda i, j: (0, j)), pl.BlockSpec((D, FFN_TF), lambda i, j: (0, j)),
                  pl.BlockSpec((FFN_TF, D), lambda i, j: (j, 0)), vec, vec],
        out_specs=[row, row], out_shape=[_sds((T, D)), _sds((T, D))],
        scratch_shapes=[pltpu.VMEM((tm, D), f32)],
        compiler_params=_cparams(("parallel", "arbitrary")), name=name)(x, wg, wu, wd, g, b)


def ffn_bwd(dz, x, wg, wu, wd, name):
    T = x.shape[0]
    tm = min(T, 512)
    nf = F // FFN_TF

    def body(dz_ref, x_ref, wg_ref, wu_ref, wd_ref, dx_ref, a_ref, dh_ref, du_ref, dzh_ref, acc):
        j = pl.program_id(1)

        @pl.when(j == 0)
        def _():
            acc[...] = jnp.zeros_like(acc)

        dzh = (0.5 * dz_ref[...]).astype(bf16)
        xb = x_ref[...].astype(bf16)
        h = _dot(xb, wg_ref[...])
        u = _dot(xb, wu_ref[...])
        s = jax.nn.sigmoid(h)
        hs = h * s
        da = _dot_nt(dzh, wd_ref[...])
        du = (da * hs).astype(bf16)
        dh = (da * u * (s + hs * (1.0 - s))).astype(bf16)
        a_ref[...] = (hs * u).astype(bf16)
        dh_ref[...] = dh
        du_ref[...] = du
        acc[...] += _dot_nt(dh, wg_ref[...]) + _dot_nt(du, wu_ref[...])

        @pl.when(j == nf - 1)
        def _():
            dx_ref[...] = ALPHA * dz_ref[...] + acc[...]
            dzh_ref[...] = dzh

    row = pl.BlockSpec((tm, D), lambda i, j: (i, 0))
    wide = pl.BlockSpec((tm, FFN_TF), lambda i, j: (i, j))
    return pl.pallas_call(
        body, grid=(T // tm, nf),
        in_specs=[row, row, pl.BlockSpec((D, FFN_TF), lambda i, j: (0, j)), pl.BlockSpec((D, FFN_TF), lambda i, j: (0, j)),
                  pl.BlockSpec((FFN_TF, D), lambda i, j: (j, 0))],
        out_specs=[row, wide, wide, wide, row],
        out_shape=[_sds((T, D)), _sds((T, F), bf16), _sds((T, F), bf16), _sds((T, F), bf16), _sds((T, D), bf16)],
        scratch_shapes=[pltpu.VMEM((tm, D), f32)],
        compiler_params=_cparams(("parallel", "arbitrary")), name=name)(dz, x, wg, wu, wd)


def lin_res_ln(a, w, res, g, b, name):
    T, K = a.shape
    tm = min(T, 512)

    def body(a_ref, w_ref, res_ref, g_ref, b_ref, z_ref, y_ref):
        z = ALPHA * res_ref[...] + _dot(a_ref[...].astype(bf16), w_ref[...])
        z_ref[...] = z
        y_ref[...] = _layer_norm(z, g_ref[...], b_ref[...])

    row = pl.BlockSpec((tm, D), lambda i: (i, 0))
    vec = pl.BlockSpec((1, D), lambda i: (0, 0))
    return pl.pallas_call(
        body, grid=(T // tm,),
        in_specs=[pl.BlockSpec((tm, K), lambda i: (i, 0)), pl.BlockSpec((K, D), lambda i: (0, 0)), row, vec, vec],
        out_specs=[row, row], out_shape=[_sds((T, D)), _sds((T, D))],
        compiler_params=_cparams(("parallel",)), name=name)(a, w, res, g, b)


def ln_bwd(dy, z, g, name):
    T = z.shape[0]
    tm = min(T, 512)

    def body(dy_ref, z_ref, g_ref, dz_ref, dg_ref, db_ref):
        @pl.when(pl.program_id(0) == 0)
        def _():
            dg_ref[...] = jnp.zeros_like(dg_ref)
            db_ref[...] = jnp.zeros_like(db_ref)

        zv = z_ref[...]
        dy = dy_ref[...]
        mu = jnp.mean(zv, axis=-1, keepdims=True)
        zc = zv - mu
        rstd = lax.rsqrt(jnp.mean(zc * zc, axis=-1, keepdims=True) + LN_EPS)
        xh = zc * rstd
        dxh = dy * g_ref[...]
        m1 = jnp.mean(dxh, axis=-1, keepdims=True)
        m2 = jnp.mean(dxh * xh, axis=-1, keepdims=True)
        dz_ref[...] = rstd * (dxh - m1 - xh * m2)
        dg_ref[...] += jnp.sum(dy * xh, axis=0, keepdims=True)
        db_ref[...] += jnp.sum(dy, axis=0, keepdims=True)

    row = pl.BlockSpec((tm, D), lambda i: (i, 0))
    vec = pl.BlockSpec((1, D), lambda i: (0, 0))
    return pl.pallas_call(
        body, grid=(T // tm,), in_specs=[row, row, vec], out_specs=[row, vec, vec],
        out_shape=[_sds((T, D)), _sds((1, D)), _sds((1, D))],
        compiler_params=_cparams(("arbitrary",)), name=name)(dy, z, g)


def loss_and_grad(y, target, name):
    T = y.shape[0]
    tm = min(T, 512)

    def body(y_ref, t_ref, l_ref, dy_ref):
        @pl.when(pl.program_id(0) == 0)
        def _():
            l_ref[...] = jnp.zeros_like(l_ref)
        d = y_ref[...] - t_ref[...]
        dy_ref[...] = d * (1.0 / D)
        l_ref[...] += (0.5 / D) * jnp.sum(jnp.sum(d * d, axis=1, keepdims=True), axis=0, keepdims=True)

    row = pl.BlockSpec((tm, D), lambda i: (i, 0))
    return pl.pallas_call(
        body, grid=(T // tm,), in_specs=[row, row],
        out_specs=[pl.BlockSpec((1, 128), lambda i: (0, 0)), row],
        out_shape=[_sds((1, 128)), _sds((T, D))],
        compiler_params=_cparams(("arbitrary",)), name=name)(y, target)


def _shifted(ext, s):
    return ext if s == 0 else pltpu.roll(ext, s, 0)


def _halo_maps(tm, P, nblk):
    per = tm // P
    prev = lambda i, c: (jnp.maximum(i * per - 1, 0), c)
    nxt = lambda i, c: (jnp.minimum((i + 1) * per, nblk * per - 1), c)
    return prev, nxt


GDN_P = 8
CONF_P = 32


def gdn_conv_fwd(proj, w8, name):
    T = proj.shape[0]
    tm = min(T, 512)
    nblk = T // tm
    C = 3 * GW
    prev, _ = _halo_maps(tm, GDN_P, nblk)

    def body(xc_ref, xp_ref, w_ref, o_ref):
        i = pl.program_id(0)
        xp = jnp.where(i > 0, xp_ref[...], 0.0)
        ext = jnp.concatenate([xp, xc_ref[...]], axis=0)
        acc = jnp.zeros((tm, C), f32)
        for k in range(GDN_K):
            acc = acc + w_ref[k:k + 1, :] * _shifted(ext, GDN_K - 1 - k)[GDN_P:, :]
        o_ref[...] = acc * jax.nn.sigmoid(acc)

    return pl.pallas_call(
        body, grid=(nblk,),
        in_specs=[pl.BlockSpec((tm, C), lambda i: (i, 0)), pl.BlockSpec((GDN_P, C), lambda i: prev(i, 0)),
                  pl.BlockSpec((8, C), lambda i: (0, 0))],
        out_specs=pl.BlockSpec((tm, C), lambda i: (i, 0)), out_shape=_sds((T, C)),
        compiler_params=_cparams(("parallel",)), name=name)(proj, proj, w8)


def gdn_conv_bwd(dy, proj, w8, name):
    T = proj.shape[0]
    tm = min(T, 512)
    nblk = T // tm
    C = 3 * GW
    P = GDN_P
    prev, nxt = _halo_maps(tm, P, nblk)

    def body(dyc_ref, dyn_ref, xc_ref, xp_ref, xn_ref, w_ref, dx_ref, dw_ref):
        i = pl.program_id(0)

        @pl.when(i == 0)
        def _():
            dw_ref[...] = jnp.zeros_like(dw_ref)

        xp = jnp.where(i > 0, xp_ref[...], 0.0)
        last = i == nblk - 1
        xn = jnp.where(last, 0.0, xn_ref[...])
        dyn = jnp.where(last, 0.0, dyn_ref[...])
        ext = jnp.concatenate([xp, xc_ref[...], xn], axis=0)
        sh = [_shifted(ext, GDN_K - 1 - k)[P:, :] for k in range(GDN_K)]
        s = jnp.zeros((tm + P, C), f32)
        for k in range(GDN_K):
            s = s + w_ref[k:k + 1, :] * sh[k]
        ds = jnp.concatenate([dyc_ref[...], dyn], axis=0) * _dsilu(s)
        dx = jnp.zeros((tm, C), f32)
        for k in range(GDN_K):
            d = GDN_K - 1 - k
            moved = ds if d == 0 else pltpu.roll(ds, tm + P - d, 0)
            dx = dx + w_ref[k:k + 1, :] * moved[:tm, :]
            dw_ref[k:k + 1, :] += jnp.sum(ds[:tm, :] * sh[k][:tm, :], axis=0, keepdims=True)
        dx_ref[...] = dx

    col = lambda i: (i, 0)
    return pl.pallas_call(
        body, grid=(nblk,),
        in_specs=[pl.BlockSpec((tm, C), col), pl.BlockSpec((P, C), lambda i: nxt(i, 0)),
                  pl.BlockSpec((tm, C), col), pl.BlockSpec((P, C), lambda i: prev(i, 0)),
                  pl.BlockSpec((P, C), lambda i: nxt(i, 0)), pl.BlockSpec((8, C), lambda i: (0, 0))],
        out_specs=[pl.BlockSpec((tm, C), col), pl.BlockSpec((8, C), lambda i: (0, 0))],
        out_shape=[_sds((T, C)), _sds((8, C))],
        compiler_params=_cparams(("arbitrary",)), name=name)(dy, dy, proj, proj, proj, w8)


def _group_ones():
    r = lax.broadcasted_iota(jnp.int32, (GW, GW), 0) // HD
    c = lax.broadcasted_iota(jnp.int32, (GW, GW), 1) // HD
    return (r == c).astype(bf16)


def _group_mean(x, ones):
    hi, lo = _split_hi_lo(x)
    return (_dot(hi, ones) + _dot(lo, ones)) * (1.0 / HD)


def _conf_norm(c, g, b, ones):
    mu = _group_mean(c, ones)
    cc = c - mu
    rstd = lax.rsqrt(_group_mean(cc * cc, ones) + LN_EPS)
    hn = cc * rstd
    return hn, rstd, hn * g + b


CONF_VAL_BLK = C_CONF // GW
CONF_GATE_BLK = C_CONF // GW + 1


def conf_fwd(proj, w32, bias, ng, nb, name):
    T = proj.shape[0]
    tm = min(T, 512)
    nblk = T // tm
    P = CONF_P
    prev, _ = _halo_maps(tm, P, nblk)

    def body(vc_ref, gc_ref, vp_ref, gp_ref, w_ref, bias_ref, ng_ref, nb_ref, y_ref, c_ref):
        i = pl.program_id(0)
        pc = vc_ref[...] * jax.nn.sigmoid(gc_ref[...])
        pp = jnp.where(i > 0, vp_ref[...] * jax.nn.sigmoid(gp_ref[...]), 0.0)
        ext = jnp.concatenate([pp, pc], axis=0)
        acc = jnp.zeros((tm, GW), f32)
        for k in range(CONF_K):
            acc = acc + w_ref[k:k + 1, :] * _shifted(ext, CONF_K - 1 - k)[P:, :]
        c = acc + bias_ref[...]
        c_ref[...] = c
        _, _, yn = _conf_norm(c, ng_ref[...], nb_ref[...], _group_ones())
        y_ref[...] = yn * jax.nn.sigmoid(yn)

    vec = pl.BlockSpec((1, GW), lambda i: (0, 0))
    return pl.pallas_call(
        body, grid=(nblk,),
        in_specs=[pl.BlockSpec((tm, GW), lambda i: (i, CONF_VAL_BLK)), pl.BlockSpec((tm, GW), lambda i: (i, CONF_GATE_BLK)),
                  pl.BlockSpec((P, GW), lambda i: prev(i, CONF_VAL_BLK)), pl.BlockSpec((P, GW), lambda i: prev(i, CONF_GATE_BLK)),
                  pl.BlockSpec((32, GW), lambda i: (0, 0)), vec, vec, vec],
        out_specs=[pl.BlockSpec((tm, GW), lambda i: (i, 0))] * 2, out_shape=[_sds((T, GW))] * 2,
        compiler_params=_cparams(("parallel",)), name=name)(proj, proj, proj, proj, w32, bias, ng, nb)


def conf_bwd(dy, c, proj, w32, ng, nb, name):
    T = proj.shape[0]
    tm = min(T, 512)
    nblk = T // tm
    P = CONF_P
    prev, nxt = _halo_maps(tm, P, nblk)

    def body(dyc_ref, dyn_ref, cc_ref, cn_ref, vc_ref, gc_ref, vp_ref, gp_ref, w_ref, ng_ref, nb_ref,
             dglu_ref, dw_ref, dbias_ref, dng_ref, dnb_ref):
        i = pl.program_id(0)

        @pl.when(i == 0)
        def _():
            dw_ref[...] = jnp.zeros_like(dw_ref)
            dbias_ref[...] = jnp.zeros_like(dbias_ref)
            dng_ref[...] = jnp.zeros_like(dng_ref)
            dnb_ref[...] = jnp.zeros_like(dnb_ref)

        ones = _group_ones()
        g = ng_ref[...]

        def dc_of(dyv, cv):
            hn, rstd, yn = _conf_norm(cv, g, nb_ref[...], ones)
            dyn_ = dyv * _dsilu(yn)
            dhn = dyn_ * g
            dc = rstd * (dhn - _group_mean(dhn, ones) - hn * _group_mean(dhn * hn, ones))
            return dc, dyn_, hn

        dc_c, dyn_c, hn_c = dc_of(dyc_ref[...], cc_ref[...])
        dc_n, _, _ = dc_of(dyn_ref[...], cn_ref[...])
        dc_n = jnp.where(i == nblk - 1, 0.0, dc_n)
        dng_ref[...] += jnp.sum(dyn_c * hn_c, axis=0, keepdims=True)
        dnb_ref[...] += jnp.sum(dyn_c, axis=0, keepdims=True)
        dbias_ref[...] += jnp.sum(dc_c, axis=0, keepdims=True)

        sig_c = jax.nn.sigmoid(gc_ref[...])
        val_c = vc_ref[...]
        pc = val_c * sig_c
        pp = jnp.where(i > 0, vp_ref[...] * jax.nn.sigmoid(gp_ref[...]), 0.0)
        ext = jnp.concatenate([pp, pc], axis=0)
        dext = jnp.concatenate([dc_c, dc_n], axis=0)
        dp = jnp.zeros((tm, GW), f32)
        for k in range(CONF_K):
            d = CONF_K - 1 - k
            moved = dext if d == 0 else pltpu.roll(dext, tm + P - d, 0)
            dp = dp + w_ref[k:k + 1, :] * moved[:tm, :]
            dw_ref[k:k + 1, :] += jnp.sum(dc_c * _shifted(ext, d)[P:, :], axis=0, keepdims=True)
        dglu_ref[:, 0:GW] = dp * sig_c
        dglu_ref[:, GW:2 * GW] = dp * val_c * sig_c * (1.0 - sig_c)

    vec = pl.BlockSpec((1, GW), lambda i: (0, 0))
    blk = pl.BlockSpec((tm, GW), lambda i: (i, 0))
    return pl.pallas_call(
        body, grid=(nblk,),
        in_specs=[blk, pl.BlockSpec((P, GW), lambda i: nxt(i, 0)), blk, pl.BlockSpec((P, GW), lambda i: nxt(i, 0)),
                  pl.BlockSpec((tm, GW), lambda i: (i, CONF_VAL_BLK)), pl.BlockSpec((tm, GW), lambda i: (i, CONF_GATE_BLK)),
                  pl.BlockSpec((P, GW), lambda i: prev(i, CONF_VAL_BLK)), pl.BlockSpec((P, GW), lambda i: prev(i, CONF_GATE_BLK)),
                  pl.BlockSpec((32, GW), lambda i: (0, 0)), vec, vec],
        out_specs=[pl.BlockSpec((tm, 2 * GW), lambda i: (i, 0)), pl.BlockSpec((32, GW), lambda i: (0, 0)), vec, vec, vec],
        out_shape=[_sds((T, 2 * GW)), _sds((32, GW)), _sds((1, GW)), _sds((1, GW)), _sds((1, GW))],
        compiler_params=_cparams(("arbitrary",)), name=name)(dy, dy, c, c, proj, proj, proj, proj, w32, ng, nb)


def _mm_raw(a, b, ta, tb):
    ca = a.ndim - 2 if ta else a.ndim - 1
    cb = b.ndim - 1 if tb else b.ndim - 2
    batch = ((0,), (0,)) if a.ndim == 3 else ((), ())
    return lax.dot_general(a, b, (((ca,), (cb,)), batch), preferred_element_type=f32)


def _mm_prec(a, b, ta, tb, prec):
    if prec == 1:
        return _mm_raw(a.astype(bf16), b.astype(bf16), ta, tb)
    bh, bl = _split_hi_lo(b)
    if prec == 2:
        ab = a.astype(bf16)
        return _mm_raw(ab, bh, ta, tb) + _mm_raw(ab, bl, ta, tb)
    ah, al = _split_hi_lo(a)
    return _mm_raw(ah, bh, ta, tb) + (_mm_raw(ah, bl, ta, tb) + _mm_raw(al, bh, ta, tb))


@functools.partial(jax.custom_vjp, nondiff_argnums=(2, 3, 4))
def mm(a, b, ta=False, tb=False, prec=1):
    return _mm_prec(a, b, ta, tb, prec)


def _mm_fwd(a, b, ta, tb, prec):
    return _mm_prec(a, b, ta, tb, prec), (a, b)


def _mm_bwd(ta, tb, prec, res, ct):
    a, b = res
    da = _mm_prec(b, ct, tb, True, 1) if ta else _mm_prec(ct, b, False, not tb, 1)
    db = _mm_prec(ct, a, True, ta, 1) if tb else _mm_prec(a, ct, not ta, False, 2 if prec == 2 else 1)
    return da, db


mm.defvjp(_mm_fwd, _mm_bwd)


def _tri_inv_raw(l):
    n = -l
    rr = lax.broadcasted_iota(jnp.int32, l.shape, 1)
    cc = lax.broadcasted_iota(jnp.int32, l.shape, 2)
    p = jnp.where(rr == cc, 1.0, 0.0) + n
    for _ in range(5):
        n = _mm_prec(n, n, False, False, 3)
        p = p + _mm_prec(p, n, False, False, 3)
    return p


@jax.custom_vjp
def tri_inv(l):
    return _tri_inv_raw(l)


def _tri_inv_fwd(l):
    t = _tri_inv_raw(l)
    return t, t


def _tri_inv_bwd(t, ct):
    return (-_mm_prec(_mm_prec(t, ct, True, False, 1), t, False, True, 1),)


tri_inv.defvjp(_tri_inv_fwd, _tri_inv_bwd)


def _gdn_block(S, qs, ks, vs, a, b, z, alog, dtb, ng):
    shp = (NH, CHUNK, CHUNK)
    ii = lax.broadcasted_iota(jnp.int32, shp, 1)
    jj = lax.broadcasted_iota(jnp.int32, shp, 2)
    l_incl = jnp.where(ii >= jj, 1.0, 0.0)
    ys = []
    for c in range(len(qs)):
        q = qs[c] * lax.rsqrt(jnp.sum(qs[c] * qs[c], axis=-1, keepdims=True) + L2_EPS) * (HD ** -0.5)
        k = ks[c] * lax.rsqrt(jnp.sum(ks[c] * ks[c], axis=-1, keepdims=True) + L2_EPS)
        v = vs[c]
        beta = jax.nn.sigmoid(b[c])
        g = -jnp.exp(alog) * _softplus(a[c] + dtb)
        gcb = mm(l_incl, jnp.broadcast_to(g, shp), False, False, 2)
        gcr = jnp.swapaxes(gcb, 1, 2)
        decay = jnp.exp(jnp.where(ii >= jj, gcb - gcr, NEG))
        g_last = jnp.sum(jnp.where(ii == CHUNK - 1, gcb, 0.0), axis=1, keepdims=True)
        eg = jnp.exp(gcb)
        kb = k * beta
        lkk = jnp.where(ii > jj, mm(kb, k, False, True) * decay, 0.0)
        t_inv = tri_inv(lkk)
        u = mm(t_inv, v * beta)
        w = mm(t_inv, kb * eg)
        a_qk = jnp.where(ii >= jj, mm(q, k, False, True) * decay, 0.0)
        q_dec = q * eg
        k_dec = k * jnp.exp(g_last - gcb)
        v_new = u - mm(w, S)
        o = mm(q_dec, S) + mm(a_qk, v_new)
        S = S * jnp.exp(g_last) + mm(k_dec, v_new, True, False)
        y = o * lax.rsqrt(jnp.mean(o * o, axis=-1, keepdims=True) + RMS_EPS) * ng
        ys.append(y * (z[c] * jax.nn.sigmoid(z[c])))
    return S, ys


GDN_CB = 256


def _gdn_load(refs, nc):
    q_ref, k_ref, v_ref, z_ref, a_ref, b_ref = refs
    sl = lambda r, c: r[:, c * CHUNK:(c + 1) * CHUNK, :]
    return tuple([sl(r, c) for c in range(nc)] for r in (q_ref, k_ref, v_ref, z_ref, a_ref, b_ref))


def gdn_fwd(qkv_hm, z_hm, ab_hm, alog, dtb, ng, name):
    T = z_hm.shape[1]
    cb = min(T, GDN_CB)
    nc = cb // CHUNK
    nb = T // cb

    def body(q_ref, k_ref, v_ref, z_ref, a_ref, b_ref, alog_ref, dtb_ref, ng_ref, y_ref, s_ref, S):
        @pl.when(pl.program_id(0) == 0)
        def _():
            S[...] = jnp.zeros_like(S)
        s_ref[...] = S[...]
        qs, ks, vs, zs, as_, bs = _gdn_load((q_ref, k_ref, v_ref, z_ref, a_ref, b_ref), nc)
        s_out, ys = _gdn_block(S[...], qs, ks, vs, as_, bs, zs, alog_ref[...], dtb_ref[...], ng_ref[...])
        S[...] = s_out
        for c in range(nc):
            y_ref[:, c * CHUNK:(c + 1) * CHUNK, :] = ys[c]

    hm = lambda h0: pl.BlockSpec((NH, cb, HD), lambda i: (h0, i, 0))
    col = lambda h0: pl.BlockSpec((NH, cb, 1), lambda i: (h0, i, 0))
    par = pl.BlockSpec((NH, 1, 1), lambda i: (0, 0, 0))
    return pl.pallas_call(
        body, grid=(nb,),
        in_specs=[hm(0), hm(1), hm(2), hm(0), col(0), col(1), par, par, pl.BlockSpec((1, 1, HD), lambda i: (0, 0, 0))],
        out_specs=[hm(0), pl.BlockSpec((None, NH, HD, HD), lambda i: (i, 0, 0, 0))],
        out_shape=[_sds((NH, T, HD)), _sds((nb, NH, HD, HD))],
        scratch_shapes=[pltpu.VMEM((NH, HD, HD), f32)],
        compiler_params=_cparams(("arbitrary",)), name=name)(qkv_hm, qkv_hm, qkv_hm, z_hm, ab_hm, ab_hm, alog, dtb, ng)


def gdn_bwd(dy_hm, states, qkv_hm, z_hm, ab_hm, alog, dtb, ng, name):
    T = z_hm.shape[1]
    cb = min(T, GDN_CB)
    nc = cb // CHUNK
    nb = T // cb

    def body(dy_ref, s_ref, q_ref, k_ref, v_ref, z_ref, a_ref, b_ref, alog_ref, dtb_ref, ng_ref,
             dq_ref, dk_ref, dv_ref, dz_ref, da_ref, db_ref, dalog_ref, ddtb_ref, dng_ref, dS):
        @pl.when(pl.program_id(0) == 0)
        def _():
            dS[...] = jnp.zeros_like(dS)
            dalog_ref[...] = jnp.zeros_like(dalog_ref)
            ddtb_ref[...] = jnp.zeros_like(ddtb_ref)
            dng_ref[...] = jnp.zeros_like(dng_ref)

        qs, ks, vs, zs, as_, bs = _gdn_load((q_ref, k_ref, v_ref, z_ref, a_ref, b_ref), nc)
        _, vjp = jax.vjp(_gdn_block, s_ref[...], qs, ks, vs, as_, bs, zs, alog_ref[...], dtb_ref[...], ng_ref[...])
        dys = [dy_ref[:, c * CHUNK:(c + 1) * CHUNK, :] for c in range(nc)]
        d_s, dqs, dks, dvs, das, dbs, dzs, d_alog, d_dtb, d_ng = vjp((dS[...], dys))
        dS[...] = d_s
        dalog_ref[...] += d_alog
        ddtb_ref[...] += d_dtb
        dng_ref[...] += d_ng
        for c in range(nc):
            sl = slice(c * CHUNK, (c + 1) * CHUNK)
            dq_ref[:, sl, :] = dqs[c]
            dk_ref[:, sl, :] = dks[c]
            dv_ref[:, sl, :] = dvs[c]
            dz_ref[:, sl, :] = dzs[c]
            da_ref[:, sl, :] = das[c]
            db_ref[:, sl, :] = dbs[c]

    rev = lambda i: nb - 1 - i
    hm = lambda h0: pl.BlockSpec((NH, cb, HD), lambda i: (h0, rev(i), 0))
    col = lambda h0: pl.BlockSpec((NH, cb, 1), lambda i: (h0, rev(i), 0))
    par = pl.BlockSpec((NH, 1, 1), lambda i: (0, 0, 0))
    ngs = pl.BlockSpec((1, 1, HD), lambda i: (0, 0, 0))
    res = pl.pallas_call(
        body, grid=(nb,),
        in_specs=[hm(0), pl.BlockSpec((None, NH, HD, HD), lambda i: (rev(i), 0, 0, 0)),
                  hm(0), hm(1), hm(2), hm(0), col(0), col(1), par, par, ngs],
        out_specs=[hm(0), hm(0), hm(0), hm(0), col(0), col(0), par, par, ngs],
        out_shape=[_sds((NH, T, HD))] * 4 + [_sds((NH, T, 1))] * 2 + [_sds((NH, 1, 1))] * 2 + [_sds((1, 1, HD))],
        scratch_shapes=[pltpu.VMEM((NH, HD, HD), f32)],
        compiler_params=_cparams(("arbitrary",)), name=name)(dy_hm, states, qkv_hm, qkv_hm, qkv_hm, z_hm, ab_hm, ab_hm, alog, dtb, ng)
    dq, dk, dv, dz, da, db, dalog, ddtb, dng = res
    return jnp.concatenate([dq, dk, dv], axis=0), dz, jnp.concatenate([da, db], axis=0), dalog, ddtb, dng


F_LANE = 8
SCAN_TB = 256


def fox_gate_fwd(proj, bfv, name):
    T = proj.shape[0]
    tb = min(T, SCAN_TB)

    def body(x_ref, b_ref, o_ref, carry):
        @pl.when(pl.program_id(0) == 0)
        def _():
            carry[...] = jnp.zeros_like(carry)
        logf = -_softplus(-(x_ref[...] + b_ref[...]))
        r = lax.broadcasted_iota(jnp.int32, (tb, tb), 0)
        c = lax.broadcasted_iota(jnp.int32, (tb, tb), 1)
        tri = (r >= c).astype(bf16)
        hi, lo = _split_hi_lo(logf)
        cum = _dot(tri, hi) + _dot(tri, lo) + carry[0:1, :]
        o_ref[...] = cum
        carry[0:1, :] = cum[tb - 1:tb, :]

    return pl.pallas_call(
        body, grid=(T // tb,),
        in_specs=[pl.BlockSpec((tb, 128), lambda i: (i, C_SMALL // 128)), pl.BlockSpec((1, 128), lambda i: (0, 0))],
        out_specs=pl.BlockSpec((tb, 128), lambda i: (i, 0)), out_shape=_sds((T, 128)),
        scratch_shapes=[pltpu.VMEM((8, 128), f32)],
        compiler_params=_cparams(("arbitrary",)), name=name)(proj, bfv)


def fox_gate_bwd(dcum, proj, bfv, name):
    T = proj.shape[0]
    tb = min(T, SCAN_TB)
    nb = T // tb

    def body(d_ref, x_ref, b_ref, o_ref, db_ref, carry):
        @pl.when(pl.program_id(0) == 0)
        def _():
            carry[...] = jnp.zeros_like(carry)
            db_ref[...] = jnp.zeros_like(db_ref)
        r = lax.broadcasted_iota(jnp.int32, (tb, tb), 0)
        c = lax.broadcasted_iota(jnp.int32, (tb, tb), 1)
        tri = (c >= r).astype(bf16)
        hi, lo = _split_hi_lo(d_ref[...])
        dlogf = _dot(tri, hi) + _dot(tri, lo) + carry[0:1, :]
        carry[0:1, :] = dlogf[0:1, :]
        lane = lax.broadcasted_iota(jnp.int32, (tb, 128), 1)
        keep = (lane >= F_LANE) & (lane < F_LANE + NH)
        dx = jnp.where(keep, dlogf * jax.nn.sigmoid(-(x_ref[...] + b_ref[...])), 0.0)
        o_ref[...] = dx
        db_ref[...] += jnp.sum(dx, axis=0, keepdims=True)

    rev = lambda i: nb - 1 - i
    return pl.pallas_call(
        body, grid=(nb,),
        in_specs=[pl.BlockSpec((tb, 128), lambda i: (rev(i), 0)), pl.BlockSpec((tb, 128), lambda i: (rev(i), C_SMALL // 128)),
                  pl.BlockSpec((1, 128), lambda i: (0, 0))],
        out_specs=[pl.BlockSpec((tb, 128), lambda i: (rev(i), 0)), pl.BlockSpec((1, 128), lambda i: (0, 0))],
        out_shape=[_sds((T, 128)), _sds((1, 128))],
        scratch_shapes=[pltpu.VMEM((8, 128), f32)],
        compiler_params=_cparams(("arbitrary",)), name=name)(dcum, proj, bfv)


ATT_TQ = 256


def _lane_col(tile, lane):
    li = lax.broadcasted_iota(jnp.int32, tile.shape, 1)
    return jnp.sum(jnp.where(li == lane, tile, 0.0), axis=1, keepdims=True)


def _pack_cols(cols):
    rows = cols[0].shape[0]
    li = lax.broadcasted_iota(jnp.int32, (rows, 128), 1)
    out = jnp.zeros((rows, 128), f32)
    for h, cv in enumerate(cols):
        out = jnp.where(li == h, cv, out)
    return out


def _head_masks():
    li = lax.broadcasted_iota(jnp.int32, (1, 128), 1)
    return [li < HD, li >= HD]


def _qkv_specs(T, tq, base_blk):
    q = pl.BlockSpec((tq, 128), lambda p, i: (i, base_blk + p))
    k = pl.BlockSpec((T, 128), lambda p, i: (0, base_blk + 2 + p))
    v = pl.BlockSpec((T, 128), lambda p, i: (0, base_blk + 4 + p))
    return q, k, v


def _stack_heads(x, masks):
    return jnp.concatenate([jnp.where(m, x, jnp.zeros_like(x)) for m in masks], axis=0)


def _side_by_side(x, tq):
    return jnp.concatenate([x[:tq], x[tq:]], axis=1)


def _stacked_mask(tq, strict):
    r = lax.broadcasted_iota(jnp.int32, (2 * tq, tq), 0)
    r = jnp.where(r >= tq, r - tq, r)
    c = lax.broadcasted_iota(jnp.int32, (2 * tq, tq), 1)
    return c < r if strict else c <= r


def _sub_head_rows(s, rows2, tq):
    return jnp.concatenate([s[:tq] - rows2[0:1, :], s[tq:] - rows2[1:2, :]], axis=0)


def _lane_cols2(tile):
    return jnp.concatenate([_lane_col(tile, 0), _lane_col(tile, 1)], axis=0)


def _pack_cols2(col, tq):
    return _pack_cols([col[:tq], col[tq:]])


def _dot_hilo2(x, tri):
    n = x.shape[0]
    hi, lo = _split_hi_lo(x)
    r = _dot(jnp.concatenate([hi, lo], axis=0), tri)
    return r[:n] + r[n:]


def fox_fwd(pbf, cumrow, name):
    T = pbf.shape[0]
    tq = min(T, ATT_TQ)
    nq = T // tq
    scale = HD ** -0.5

    def body(q_ref, k_ref, v_ref, cr_ref, o_ref, lse_ref):
        i = pl.program_id(1)
        masks = _head_masks()
        qs2 = _stack_heads(q_ref[...] * scale, masks)

        def tile(kb, carry, diag):
            m, l, acc = carry
            off = pl.multiple_of(kb * tq, tq)
            v2 = _stack_heads(v_ref[pl.ds(off, tq), :], masks)
            s = _sub_head_rows(_dot_nt(qs2, k_ref[pl.ds(off, tq), :]), cr_ref[kb], tq)
            if diag:
                s = jnp.where(_stacked_mask(tq, False), s, NEG)
            m_new = jnp.maximum(m, jnp.max(s, axis=1, keepdims=True))
            corr = jnp.exp(m - m_new)
            p = jnp.exp(s - m_new)
            l = l * corr + jnp.sum(p, axis=1, keepdims=True)
            acc = acc * jnp.where(masks[0], corr[:tq], corr[tq:]) + _dot(_side_by_side(p.astype(bf16), tq), v2)
            return m_new, l, acc

        init = (jnp.full((2 * tq, 1), NEG, f32), jnp.zeros((2 * tq, 1), f32), jnp.zeros((tq, 128), f32))
        carry = lax.fori_loop(0, i, lambda kb, c: tile(kb, c, False), init)
        m, l, acc = tile(i, carry, True)
        o_ref[...] = acc * jnp.where(masks[0], 1.0 / l[:tq], 1.0 / l[tq:])
        lse_ref[...] = _pack_cols2(m + jnp.log(l), tq)

    qs, ks, vs = _qkv_specs(T, tq, C_FOX // 128)
    return pl.pallas_call(
        body, grid=(2, nq),
        in_specs=[qs, ks, vs, pl.BlockSpec((None, nq, 8, tq), lambda p, i: (p, 0, 0, 0))],
        out_specs=[pl.BlockSpec((tq, 128), lambda p, i: (i, p)), pl.BlockSpec((None, tq, 128), lambda p, i: (p, i, 0))],
        out_shape=[_sds((T, GW)), _sds((2, T, 128))],
        compiler_params=_cparams(("parallel", "parallel")), name=name)(pbf, pbf, pbf, cumrow)


def fox_bwd(do, o, lse, pbf, cumrow, name):
    T = pbf.shape[0]
    tq = min(T, ATT_TQ)
    nq = T // tq
    scale = HD ** -0.5

    def body(do_ref, o_ref, lse_ref, q_ref, k_ref, v_ref, cr_ref, dq_ref, dk_ref, dv_ref, dc_ref, dcq_ref):
        i = pl.program_id(1)

        @pl.when(i == 0)
        def _():
            dk_ref[...] = jnp.zeros_like(dk_ref)
            dv_ref[...] = jnp.zeros_like(dv_ref)
            dc_ref[...] = jnp.zeros_like(dc_ref)

        masks = _head_masks()
        dov = do_ref[...]
        qs2 = _stack_heads(q_ref[...] * scale, masks)
        do2 = _stack_heads(dov.astype(bf16), masks)
        prod = dov * o_ref[...]
        delta = jnp.concatenate([jnp.sum(jnp.where(m, prod, 0.0), axis=1, keepdims=True) for m in masks], axis=0)
        lse2 = _lane_cols2(lse_ref[...])

        def tile(kb, carry, diag):
            dq, rsum = carry
            off = pl.multiple_of(kb * tq, tq)
            kblk = k_ref[pl.ds(off, tq), :]
            p = jnp.exp(_sub_head_rows(_dot_nt(qs2, kblk), cr_ref[kb], tq) - lse2)
            if diag:
                p = jnp.where(_stacked_mask(tq, False), p, 0.0)
            dp = _dot_nt(do2, v_ref[pl.ds(off, tq), :])
            ds = p * (dp - delta)
            dsb = ds.astype(bf16)
            dq = dq + _dot(_side_by_side(dsb, tq), _stack_heads(kblk, masks))
            dk_ref[pl.ds(off, tq), :] += _dot_tn(dsb, qs2)
            dv_ref[pl.ds(off, tq), :] += _dot_tn(p.astype(bf16), do2)
            dc_ref[kb, 0:1, :] += -jnp.sum(ds[:tq], axis=0, keepdims=True)
            dc_ref[kb, 1:2, :] += -jnp.sum(ds[tq:], axis=0, keepdims=True)
            return dq, rsum + jnp.sum(ds, axis=1, keepdims=True)

        init = (jnp.zeros((tq, 128), f32), jnp.zeros((2 * tq, 1), f32))
        carry = lax.fori_loop(0, i, lambda kb, c: tile(kb, c, False), init)
        dq, rsum = tile(i, carry, True)
        dq_ref[...] = dq * scale
        dcq_ref[...] = _pack_cols2(rsum, tq)

    qs, ks, vs = _qkv_specs(T, tq, C_FOX // 128)
    tile_spec = pl.BlockSpec((tq, 128), lambda p, i: (i, p))
    pair = pl.BlockSpec((None, tq, 128), lambda p, i: (p, i, 0))
    rowsp = pl.BlockSpec((None, nq, 8, tq), lambda p, i: (p, 0, 0, 0))
    full = pl.BlockSpec((T, 128), lambda p, i: (0, p))
    return pl.pallas_call(
        body, grid=(2, nq),
        in_specs=[tile_spec, tile_spec, pair, qs, ks, vs, rowsp],
        out_specs=[tile_spec, full, full, rowsp, pair],
        out_shape=[_sds((T, GW)), _sds((T, GW)), _sds((T, GW)), _sds((2, nq, 8, tq)), _sds((2, T, 128))],
        compiler_params=_cparams(("parallel", "arbitrary")), name=name)(do, o, lse, pbf, pbf, pbf, cumrow)


def _tri(tq, pred):
    r = lax.broadcasted_iota(jnp.int32, (tq, tq), 0)
    c = lax.broadcasted_iota(jnp.int32, (tq, tq), 1)
    return pred(r, c).astype(bf16)


def sb_fwd(pbf, name):
    T = pbf.shape[0]
    tq = min(T, ATT_TQ)
    nq = T // tq
    scale = HD ** -0.5

    def body(q_ref, k_ref, v_ref, o_ref, tot_ref):
        i = pl.program_id(1)
        masks = _head_masks()
        qs2 = _stack_heads(q_ref[...] * scale, masks)
        after = _tri(tq, lambda r, c: r > c)

        def tile(kb, carry, diag):
            rs, acc = carry
            off = pl.multiple_of(kb * tq, tq)
            z = _dot_nt(qs2, k_ref[pl.ds(off, tq), :])
            lk = _neg_softplus(z)
            if diag:
                lk = jnp.where(_stacked_mask(tq, True), lk, 0.0)
            w = jnp.exp(z + lk + (_dot_hilo2(lk, after) + rs))
            if diag:
                w = jnp.where(_stacked_mask(tq, True), w, 0.0)
            acc = acc + _dot(_side_by_side(w.astype(bf16), tq), _stack_heads(v_ref[pl.ds(off, tq), :], masks))
            return rs + jnp.sum(lk, axis=1, keepdims=True), acc

        carry = tile(i, (jnp.zeros((2 * tq, 1), f32), jnp.zeros((tq, 128), f32)), True)
        rs, acc = lax.fori_loop(0, i, lambda n, c: tile(i - 1 - n, c, False), carry)
        o_ref[...] = acc
        tot_ref[...] = _pack_cols2(rs, tq)

    qs, ks, vs = _qkv_specs(T, tq, C_SB // 128)
    return pl.pallas_call(
        body, grid=(2, nq), in_specs=[qs, ks, vs],
        out_specs=[pl.BlockSpec((tq, 128), lambda p, i: (i, p)), pl.BlockSpec((None, tq, 128), lambda p, i: (p, i, 0))],
        out_shape=[_sds((T, GW)), _sds((2, T, 128))],
        compiler_params=_cparams(("parallel", "parallel")), name=name)(pbf, pbf, pbf)


def sb_bwd(do, tot, pbf, name):
    T = pbf.shape[0]
    tq = min(T, ATT_TQ)
    nq = T // tq
    scale = HD ** -0.5

    def body(do_ref, tot_ref, q_ref, k_ref, v_ref, dq_ref, dk_ref, dv_ref):
        i = pl.program_id(1)

        @pl.when(i == 0)
        def _():
            dk_ref[...] = jnp.zeros_like(dk_ref)
            dv_ref[...] = jnp.zeros_like(dv_ref)

        masks = _head_masks()
        qs2 = _stack_heads(q_ref[...] * scale, masks)
        do2 = _stack_heads(do_ref[...].astype(bf16), masks)
        tot2 = _lane_cols2(tot_ref[...])
        upto = _tri(tq, lambda r, c: r <= c)
        before = _tri(tq, lambda r, c: r < c)

        def tile(kb, carry, diag):
            pre, cg, dq = carry
            off = pl.multiple_of(kb * tq, tq)
            kblk = k_ref[pl.ds(off, tq), :]
            z = _dot_nt(qs2, kblk)
            lk = _neg_softplus(z)
            keep = jnp.exp(lk)
            if diag:
                lk = jnp.where(_stacked_mask(tq, True), lk, 0.0)
            w = jnp.exp(z + lk + (tot2 - (pre + _dot_hilo2(lk, upto))))
            if diag:
                w = jnp.where(_stacked_mask(tq, True), w, 0.0)
            gmat = w * _dot_nt(do2, v_ref[pl.ds(off, tq), :])
            cmat = cg + _dot(gmat.astype(bf16), before)
            dz = gmat * keep - cmat * (1.0 - keep)
            if diag:
                dz = jnp.where(_stacked_mask(tq, True), dz, 0.0)
            dzb = dz.astype(bf16)
            dq = dq + _dot(_side_by_side(dzb, tq), _stack_heads(kblk, masks))
            dk_ref[pl.ds(off, tq), :] += _dot_tn(dzb, qs2)
            dv_ref[pl.ds(off, tq), :] += _dot_tn(w.astype(bf16), do2)
            return pre + jnp.sum(lk, axis=1, keepdims=True), cg + jnp.sum(gmat, axis=1, keepdims=True), dq

        zc = jnp.zeros((2 * tq, 1), f32)
        carry = lax.fori_loop(0, i, lambda kb, c: tile(kb, c, False), (zc, zc, jnp.zeros((tq, 128), f32)))
        _, _, dq = tile(i, carry, True)
        dq_ref[...] = dq * scale

    qs, ks, vs = _qkv_specs(T, tq, C_SB // 128)
    tile_spec = pl.BlockSpec((tq, 128), lambda p, i: (i, p))
    pair = pl.BlockSpec((None, tq, 128), lambda p, i: (p, i, 0))
    full = pl.BlockSpec((T, 128), lambda p, i: (0, p))
    return pl.pallas_call(
        body, grid=(2, nq), in_specs=[tile_spec, pair, qs, ks, vs],
        out_specs=[tile_spec, full, full], out_shape=[_sds((T, GW))] * 3,
        compiler_params=_cparams(("parallel", "arbitrary")), name=name)(do, tot, pbf, pbf, pbf)


MEM_HD = D // 4


def mem_fwd(q, kv, name):
    T = q.shape[0]
    M = kv.shape[0]
    tm = min(T, 512)
    scale = MEM_HD ** -0.5

    def body(q_ref, kv_ref, o_ref):
        for h in range(4):
            sl = slice(h * MEM_HD, (h + 1) * MEM_HD)
            kh = kv_ref[:, sl].astype(bf16)
            vh = kv_ref[:, D + h * MEM_HD:D + (h + 1) * MEM_HD].astype(bf16)
            s = _dot_nt(q_ref[:, sl], kh) * scale
            e = jnp.exp(s - jnp.max(s, axis=1, keepdims=True))
            p = e / jnp.sum(e, axis=1, keepdims=True)
            o_ref[:, sl] = _dot(p.astype(bf16), vh).astype(bf16)

    return pl.pallas_call(
        body, grid=(T // tm,),
        in_specs=[pl.BlockSpec((tm, D), lambda i: (i, 0)), pl.BlockSpec((M, 2 * D), lambda i: (0, 0))],
        out_specs=pl.BlockSpec((tm, D), lambda i: (i, 0)), out_shape=_sds((T, D), bf16),
        compiler_params=_cparams(("parallel",)), name=name)(q, kv)


def mem_bwd(do, q, kv, name):
    T = q.shape[0]
    M = kv.shape[0]
    tm = min(T, 512)
    scale = MEM_HD ** -0.5

    def body(do_ref, q_ref, kv_ref, dq_ref, dkv_ref):
        @pl.when(pl.program_id(0) == 0)
        def _():
            dkv_ref[...] = jnp.zeros_like(dkv_ref)
        for h in range(4):
            sl = slice(h * MEM_HD, (h + 1) * MEM_HD)
            vsl = slice(D + h * MEM_HD, D + (h + 1) * MEM_HD)
            qh = q_ref[:, sl]
            kh = kv_ref[:, sl].astype(bf16)
            vh = kv_ref[:, vsl].astype(bf16)
            doh = do_ref[:, sl].astype(bf16)
            s = _dot_nt(qh, kh) * scale
            e = jnp.exp(s - jnp.max(s, axis=1, keepdims=True))
            p = e / jnp.sum(e, axis=1, keepdims=True)
            dp = _dot_nt(doh, vh)
            ds = p * (dp - jnp.sum(dp * p, axis=1, keepdims=True))
            dsb = ds.astype(bf16)
            dq_ref[:, sl] = _dot(dsb, kh) * scale
            dkv_ref[:, sl] += _dot_tn(dsb, qh) * scale
            dkv_ref[:, vsl] += _dot_tn(p.astype(bf16), doh)

    row = pl.BlockSpec((tm, D), lambda i: (i, 0))
    whole = pl.BlockSpec((M, 2 * D), lambda i: (0, 0))
    return pl.pallas_call(
        body, grid=(T // tm,), in_specs=[row, row, whole], out_specs=[row, whole],
        out_shape=[_sds((T, D)), _sds((M, 2 * D))],
        compiler_params=_cparams(("arbitrary",)), name=name)(do, q, kv)


def _chip_peers():
    x, y, c = lax.axis_index("x"), lax.axis_index("y"), lax.axis_index("c")
    me = 2 * x + y
    peers = [((1 - x, y, c), 2 * (1 - x) + y), ((x, 1 - y, c), 2 * x + (1 - y)), ((1 - x, 1 - y, c), 2 * (1 - x) + (1 - y))]
    return me, peers


def gather_chips(arrs, split, name):
    n = len(arrs)

    def body(*refs):
        ins, outs = refs[:n], refs[n:2 * n]
        send_sems, recv_sems, pass_send, pass_recv, loc_sems = refs[2 * n:]
        x, y, c = lax.axis_index("x"), lax.axis_index("y"), lax.axis_index("c")
        me, peers = _chip_peers()

        def mine(ref, i):
            if not split[i]:
                return ref
            r = arrs[i].shape[1] // 2
            return ref.at[:, pl.ds(c * r, r), :]

        local, fetch, passed = [], [], []
        for i in range(n):
            loc = pltpu.make_async_copy(ins[i], outs[i].at[me], loc_sems.at[i])
            loc.start()
            local.append(loc)
            for k, (dev, _) in enumerate(peers):
                cp = pltpu.make_async_remote_copy(src_ref=mine(ins[i], i), dst_ref=mine(outs[i].at[me], i),
                                                  send_sem=send_sems.at[i, k], recv_sem=recv_sems.at[i, k],
                                                  device_id=dev, device_id_type=MESH)
                cp.start()
                fetch.append((i, k, cp))
        for i, k, cp in fetch:
            cp.wait_recv()
            if split[i]:
                rows = mine(outs[i].at[peers[k][1]], i)
                fw = pltpu.make_async_remote_copy(src_ref=rows, dst_ref=rows, send_sem=pass_send.at[i, k],
                                                  recv_sem=pass_recv.at[i, k], device_id=(x, y, 1 - c), device_id_type=MESH)
                fw.start()
                passed.append(fw)
        for _, _, cp in fetch:
            cp.wait_send()
        for cp in passed + local:
            cp.wait()

    anyspec = pl.BlockSpec(memory_space=pl.ANY)
    sem = pltpu.SemaphoreType.DMA((n, 3))
    return pl.pallas_call(
        body, in_specs=[anyspec] * n, out_specs=[anyspec] * n,
        out_shape=[_sds((4,) + a.shape, a.dtype) for a in arrs],
        scratch_shapes=[sem, sem, sem, sem, pltpu.SemaphoreType.DMA((n,))],
        name=name)(*arrs)


def exchange_chips(stacks, name):
    n = len(stacks)

    def body(*refs):
        ins, outs = refs[:n], refs[n:2 * n]
        send_sems, recv_sems, loc_sems = refs[2 * n:]
        me, peers = _chip_peers()
        started = []
        for i in range(n):
            loc = pltpu.make_async_copy(ins[i].at[me], outs[i].at[me], loc_sems.at[i])
            loc.start()
            started.append(loc)
            for k, (dev, pj) in enumerate(peers):
                cp = pltpu.make_async_remote_copy(src_ref=ins[i].at[pj], dst_ref=outs[i].at[me], send_sem=send_sems.at[i, k],
                                                  recv_sem=recv_sems.at[i, k], device_id=dev, device_id_type=MESH)
                cp.start()
                started.append(cp)
        for cp in started:
            cp.wait()

    anyspec = pl.BlockSpec(memory_space=pl.ANY)
    return pl.pallas_call(
        body, in_specs=[anyspec] * n, out_specs=[anyspec] * n,
        out_shape=[_sds(a.shape, a.dtype) for a in stacks],
        scratch_shapes=[pltpu.SemaphoreType.DMA((n, 3)), pltpu.SemaphoreType.DMA((n, 3)), pltpu.SemaphoreType.DMA((n,))],
        name=name)(*stacks)


def swap_sibling(arrs, name):
    n = len(arrs)

    def body(*refs):
        ins, outs = refs[:n], refs[n:2 * n]
        send_sems, recv_sems = refs[2 * n:]
        x, y, c = lax.axis_index("x"), lax.axis_index("y"), lax.axis_index("c")
        started = []
        for i in range(n):
            cp = pltpu.make_async_remote_copy(src_ref=ins[i], dst_ref=outs[i], send_sem=send_sems.at[i],
                                              recv_sem=recv_sems.at[i], device_id=(x, y, 1 - c), device_id_type=MESH)
            cp.start()
            started.append(cp)
        for cp in started:
            cp.wait()

    anyspec = pl.BlockSpec(memory_space=pl.ANY)
    return pl.pallas_call(
        body, in_specs=[anyspec] * n, out_specs=[anyspec] * n,
        out_shape=[_sds(a.shape, a.dtype) for a in arrs],
        scratch_shapes=[pltpu.SemaphoreType.DMA((n,)), pltpu.SemaphoreType.DMA((n,))],
        name=name)(*arrs)


def gather_all(a, name):
    def body(a_ref, o_ref, send_sems, recv_sems, loc_sem):
        x, y, c = lax.axis_index("x"), lax.axis_index("y"), lax.axis_index("c")
        me = 4 * x + 2 * y + c
        loc = pltpu.make_async_copy(a_ref, o_ref.at[me], loc_sem)
        loc.start()
        started = [loc]
        for k in range(1, 8):
            dev = (x ^ (k >> 2), y ^ ((k >> 1) & 1), c ^ (k & 1))
            cp = pltpu.make_async_remote_copy(src_ref=a_ref, dst_ref=o_ref.at[me], send_sem=send_sems.at[k - 1],
                                              recv_sem=recv_sems.at[k - 1], device_id=dev, device_id_type=MESH)
            cp.start()
            started.append(cp)
        for cp in started:
            cp.wait()

    anyspec = pl.BlockSpec(memory_space=pl.ANY)
    return pl.pallas_call(
        body, in_specs=[anyspec], out_specs=anyspec, out_shape=_sds((8,) + a.shape, a.dtype),
        scratch_shapes=[pltpu.SemaphoreType.DMA((7,)), pltpu.SemaphoreType.DMA((7,)), pltpu.SemaphoreType.DMA(())],
        name=name)(a)


def sum_slots(stack, name):
    n, R, C = stack.shape
    tr = R if R <= 512 else _pick(R, (512, 352, 256))

    def body(s_ref, o_ref):
        acc = s_ref[0].astype(f32)
        for j in range(1, n):
            acc = acc + s_ref[j].astype(f32)
        o_ref[...] = acc

    return pl.pallas_call(
        body, grid=(R // tr,), in_specs=[pl.BlockSpec((n, tr, C), lambda i: (0, i, 0))],
        out_specs=pl.BlockSpec((tr, C), lambda i: (i, 0)), out_shape=_sds((R, C)),
        compiler_params=_cparams(("parallel",)), name=name)(stack)


def adamw(w, g1, g2, m, v, name):
    R, C = w.shape
    tr = R if R <= 512 else _pick(R, (512, 352, 256))
    c1 = 1.0 - ADAM_B1 ** ADAM_STEP
    c2 = 1.0 - ADAM_B2 ** ADAM_STEP

    def body(*refs):
        if g2 is None:
            w_ref, g1_ref, m_ref, v_ref, g_out, d_out, m_out, v_out = refs
            g = g1_ref[...]
        else:
            w_ref, g1_ref, g2_ref, m_ref, v_ref, g_out, d_out, m_out, v_out = refs
            g = g1_ref[...] + g2_ref[...]
        mn = ADAM_B1 * m_ref[...] + (1.0 - ADAM_B1) * g
        vn = ADAM_B2 * v_ref[...] + (1.0 - ADAM_B2) * (g * g)
        g_out[...] = g
        m_out[...] = mn
        v_out[...] = vn
        d_out[...] = -ADAM_LR * ((mn / c1) / (jnp.sqrt(vn / c2) + ADAM_EPS) + ADAM_WD * w_ref[...])

    blk = pl.BlockSpec((tr, C), lambda i: (i, 0))
    args = [w, g1] + ([] if g2 is None else [g2]) + [m, v]
    return pl.pallas_call(
        body, grid=(R // tr,), in_specs=[blk] * len(args), out_specs=[blk] * 4, out_shape=[_sds((R, C))] * 4,
        compiler_params=_cparams(("parallel",)), name=name)(*args)


IN_SPLITS = (768, 256, 4, 4, 768, 4, 512, 768)
IN_OFF = (0, 768, 1024, 1028, 1032, 1800, 1804, 2316, 3084)


def regroup_w_in(w):
    seg = lambda i: w[:, IN_OFF[i]:IN_OFF[i + 1]]
    pad = jnp.zeros((w.shape[0], PW - C_SMALL - 12), w.dtype)
    return jnp.concatenate([seg(0), seg(1), seg(4), seg(6), seg(7), seg(2), seg(3), seg(5), pad], axis=1)


def ungroup_w_in(g):
    s = C_SMALL
    return jnp.concatenate([g[:, 0:1024], g[:, s:s + 8], g[:, 1024:1792], g[:, s + 8:s + 12], g[:, 1792:3072]], axis=1)


def to_hm(t, nh):
    T = t.shape[0]
    return t.reshape(T, nh, HD).transpose(1, 0, 2)


def from_hm(t):
    nh, T, _ = t.shape
    return t.transpose(1, 0, 2).reshape(T, nh * HD)


def col_shards(w):
    c = w.shape[-1] // 4
    return jnp.moveaxis(w.reshape(w.shape[:-1] + (4, c)), -2, 0)


def row_shards(w):
    L, r4, c = w.shape
    return w.reshape(L, 4, r4 // 4, c).transpose(1, 0, 2, 3)


def join_cols(g):
    return jnp.moveaxis(g, 0, -2).reshape(g.shape[1:-1] + (4 * g.shape[-1],))


def join_rows(g):
    _, L, r, c = g.shape
    return g.transpose(1, 0, 2, 3).reshape(L, 4 * r, c)


COL_SHARDED = ("ffn1_w_gate", "ffn1_w_up", "w_in", "gdn_conv_w", "conf_dw_w", "mem_w_kv", "ffn2_w_gate", "ffn2_w_up")
CONV_WEIGHTS = ("gdn_conv_w", "conf_dw_w")
ROW_SHARDED = ("ffn1_w_down", "w_out", "mem_w_q", "mem_w_o", "ffn2_w_down")
REPLICATED = ("ln_ffn1_g", "ln_ffn1_b", "gdn_a_log", "gdn_dt_bias", "gdn_norm_g", "fox_b_f", "conf_dw_b", "conf_norm_g",
              "conf_norm_b", "ln_mix_g", "ln_mix_b", "ln_mem_g", "ln_mem_b", "ln_ffn2_g", "ln_ffn2_b")
WEIGHTS = ("ffn1_w_gate", "ffn1_w_up", "ffn1_w_down", "ln_ffn1_g", "ln_ffn1_b", "w_in", "gdn_conv_w", "gdn_a_log",
           "gdn_dt_bias", "gdn_norm_g", "fox_b_f", "conf_dw_w", "conf_dw_b", "conf_norm_g", "conf_norm_b", "w_out",
           "ln_mix_g", "ln_mix_b", "mem_w_q", "mem_w_kv", "mem_w_o", "ln_mem_g", "ln_mem_b", "ffn2_w_gate",
           "ffn2_w_up", "ffn2_w_down", "ln_ffn2_g", "ln_ffn2_b")


def pack_small(d):
    flat = jnp.concatenate([d[n].reshape(-1) for n in REPLICATED])
    rows = -(-flat.shape[0] // 1024) * 8
    return jnp.pad(flat, (0, rows * 128 - flat.shape[0])).reshape(rows, 128)


def unpack_small(p, like):
    flat = p.reshape(-1)
    out, o = {}, 0
    for n in REPLICATED:
        sz = like[n].size
        out[n] = flat[o:o + sz].reshape(like[n].shape)
        o += sz
    return out


def _vec(v):
    return v.reshape(1, -1)


def _pad_rows(w, rows):
    return jnp.pad(w, ((0, rows - w.shape[0]), (0, 0)))


def _small_lane_vec(v4, lane0):
    return jnp.pad(v4.reshape(1, -1), ((0, 0), (lane0, 128 - lane0 - v4.shape[0])))


def layer_fwd(x0, mem, W, li):
    T = x0.shape[0]
    tq = min(T, ATT_TQ)
    nq = T // tq
    n = lambda s: f"l{li}_{s}"
    R = {"x0": x0}
    R["z1"], x1 = ffn_fwd(x0, W["ffn1_w_gate"], W["ffn1_w_up"], W["ffn1_w_down"], _vec(W["ln_ffn1_g"]), _vec(W["ln_ffn1_b"]), n("ffn1_fwd"))
    R["x1"] = x1
    proj, pbf = mm_nn(x1, W["w_in_r"], n("proj"), also_bf16=True)
    R["proj"], R["pbf"] = proj, pbf

    w8 = _pad_rows(W["gdn_conv_w"], 8)
    qkv_s = gdn_conv_fwd(proj, w8, n("gdn_conv_fwd"))
    qkv_hm = to_hm(qkv_s, 12)
    z_hm = to_hm(proj[:, C_GZ:C_GZ + GW], 4)
    ab_hm = proj[:, C_SMALL:C_SMALL + 8].T.reshape(8, T, 1)
    alog = W["gdn_a_log"].reshape(NH, 1, 1)
    dtb = W["gdn_dt_bias"].reshape(NH, 1, 1)
    ng = W["gdn_norm_g"].reshape(1, 1, HD)
    ya_hm, states = gdn_fwd(qkv_hm, z_hm, ab_hm, alog, dtb, ng, n("gdn_fwd"))
    R.update(qkv_hm=qkv_hm, z_hm=z_hm, ab_hm=ab_hm, states=states)

    bfv = _small_lane_vec(W["fox_b_f"], F_LANE)
    cum = fox_gate_fwd(proj, bfv, n("fox_gate_fwd"))
    cum4 = cum[:, F_LANE:F_LANE + NH]
    cumrow = jnp.pad(cum4.T.reshape(2, 2, nq, tq).transpose(0, 2, 1, 3), ((0, 0), (0, 0), (0, 6), (0, 0)))
    yb, lse = fox_fwd(pbf, cumrow, n("fox_fwd"))
    R.update(cumrow=cumrow, yb=yb, lse=lse)

    w32 = _pad_rows(W["conf_dw_w"], 32)
    yc, cc = conf_fwd(proj, w32, _vec(W["conf_dw_b"]), _vec(W["conf_norm_g"]), _vec(W["conf_norm_b"]), n("conf_fwd"))
    R["cc"] = cc

    yd, tot = sb_fwd(pbf, n("sb_fwd"))
    R["tot"] = tot

    ycat = jnp.concatenate([from_hm(ya_hm), yb, yc, yd], axis=1).astype(bf16)
    R["ycat"] = ycat
    R["z2"], x2 = lin_res_ln(ycat, W["w_out"], x1, _vec(W["ln_mix_g"]), _vec(W["ln_mix_b"]), n("mix_out"))
    R["x2"] = x2

    qm = mm_nn(x2, W["mem_w_q"], n("mem_q"), out_dtype=bf16)
    kv = mm_nn(mem, W["mem_w_kv"], n("mem_kv"))
    om = mem_fwd(qm, kv, n("mem_fwd"))
    R.update(qm=qm, kv=kv, om=om)
    R["z3"], x3 = lin_res_ln(om, W["mem_w_o"], x2, _vec(W["ln_mem_g"]), _vec(W["ln_mem_b"]), n("mem_out"))
    R["x3"] = x3
    R["z4"], x4 = ffn_fwd(x3, W["ffn2_w_gate"], W["ffn2_w_up"], W["ffn2_w_down"], _vec(W["ln_ffn2_g"]), _vec(W["ln_ffn2_b"]), n("ffn2_fwd"))
    return x4, R


def layer_bwd(dx4, mem, W, R, li):
    T = dx4.shape[0]
    n = lambda s: f"l{li}_{s}"
    G = {}

    def ffn_back(dy, z, x, pre, tag):
        dz, dg, db = ln_bwd(dy, z, _vec(W[f"ln_{pre}_g"]), n(f"{tag}_ln_bwd"))
        dx, a, dh, du, dzh = ffn_bwd(dz, x, W[f"{pre}_w_gate"], W[f"{pre}_w_up"], W[f"{pre}_w_down"], n(f"{tag}_bwd"))
        G[f"{pre}_w_gate"] = mm_tn(x, dh, n(f"{tag}_dwg"))
        G[f"{pre}_w_up"] = mm_tn(x, du, n(f"{tag}_dwu"))
        G[f"{pre}_w_down"] = mm_tn(a, dzh, n(f"{tag}_dwd"))
        G[f"ln_{pre}_g"], G[f"ln_{pre}_b"] = dg.reshape(-1), db.reshape(-1)
        return dx

    dx3 = ffn_back(dx4, R["z4"], R["x3"], "ffn2", "ffn2")

    dz3, dg, db = ln_bwd(dx3, R["z3"], _vec(W["ln_mem_g"]), n("mem_ln_bwd"))
    G["ln_mem_g"], G["ln_mem_b"] = dg.reshape(-1), db.reshape(-1)
    dom = mm_nt(dz3, W["mem_w_o"], n("mem_dom"))
    G["mem_w_o"] = mm_tn(R["om"], dz3, n("mem_dwo"))
    dqm, dkv = mem_bwd(dom, R["qm"], R["kv"], n("mem_bwd"))
    G["mem_w_q"] = mm_tn(R["x2"], dqm, n("mem_dwq"))
    G["mem_w_kv"] = mm_tn(mem, dkv, n("mem_dwkv"))
    dx2 = mm_nt(dqm, W["mem_w_q"], n("mem_dx"), add=dz3, add_scale=ALPHA)

    dz2, dg, db = ln_bwd(dx2, R["z2"], _vec(W["ln_mix_g"]), n("mix_ln_bwd"))
    G["ln_mix_g"], G["ln_mix_b"] = dg.reshape(-1), db.reshape(-1)
    dycat = mm_nt(dz2, W["w_out"], n("mix_dycat"))
    G["w_out"] = mm_tn(R["ycat"], dz2, n("mix_dwout"))
    dya, dyb, dyc, dyd = (dycat[:, i * GW:(i + 1) * GW] for i in range(4))
    proj, pbf = R["proj"], R["pbf"]

    alog = W["gdn_a_log"].reshape(NH, 1, 1)
    dtb = W["gdn_dt_bias"].reshape(NH, 1, 1)
    ng = W["gdn_norm_g"].reshape(1, 1, HD)
    dqkv_hm, dz_hm, dab_hm, dalog, ddtb, dng = gdn_bwd(to_hm(dya, 4), R["states"], R["qkv_hm"], R["z_hm"], R["ab_hm"],
                                                      alog, dtb, ng, n("gdn_bwd"))
    G["gdn_a_log"], G["gdn_dt_bias"], G["gdn_norm_g"] = dalog.reshape(-1), ddtb.reshape(-1), dng.reshape(-1)
    w8 = _pad_rows(W["gdn_conv_w"], 8)
    dgqkv, dw8 = gdn_conv_bwd(from_hm(dqkv_hm), proj, w8, n("gdn_conv_bwd"))
    G["gdn_conv_w"] = dw8[:GDN_K]

    dfq, dfk, dfv, dcumrow, dcumq = fox_bwd(dyb, R["yb"], R["lse"], pbf, R["cumrow"], n("fox_bwd"))
    dcum4 = dcumrow[:, :, 0:2, :].transpose(0, 2, 1, 3).reshape(4, T).T
    dcum4 = dcum4 + dcumq[:, :, 0:2].transpose(1, 0, 2).reshape(T, 4)
    dcum = jnp.pad(dcum4, ((0, 0), (F_LANE, 128 - F_LANE - NH)))
    bfv = _small_lane_vec(W["fox_b_f"], F_LANE)
    dsmall_f, dbf = fox_gate_bwd(dcum, proj, bfv, n("fox_gate_bwd"))
    G["fox_b_f"] = dbf[0, F_LANE:F_LANE + NH]

    w32 = _pad_rows(W["conf_dw_w"], 32)
    dglu, dw32, dcb, dcg, dcbeta = conf_bwd(dyc, R["cc"], proj, w32, _vec(W["conf_norm_g"]), _vec(W["conf_norm_b"]), n("conf_bwd"))
    G["conf_dw_w"], G["conf_dw_b"] = dw32[:CONF_K], dcb.reshape(-1)
    G["conf_norm_g"], G["conf_norm_b"] = dcg.reshape(-1), dcbeta.reshape(-1)

    dsq, dsk, dsv = sb_bwd(dyd, R["tot"], pbf, n("sb_bwd"))

    dsmall = jnp.concatenate([dab_hm.reshape(8, T).T, dsmall_f[:, F_LANE:F_LANE + NH], jnp.zeros((T, PW - C_SMALL - 12), f32)], axis=1)
    dproj = jnp.concatenate([dgqkv, from_hm(dz_hm), dfq, dfk, dfv, dglu, dsq, dsk, dsv, dsmall], axis=1).astype(bf16)
    G["w_in_r"] = mm_tn(R["x1"], dproj, n("proj_dw"))
    dx1 = mm_nt(dproj, W["w_in_r"], n("proj_dx"), add=dz2, add_scale=ALPHA)

    dx0 = ffn_back(dx1, R["z1"], R["x0"], "ffn1", "ffn1")
    return dx0, G


def _step(P, M, V, x, mem, loss_target):
    xs, mems, tgt = x[0], mem[0], loss_target[0]

    big = COL_SHARDED + ROW_SHARDED
    gathered = gather_chips([P[k] if k in CONV_WEIGHTS else P[k].astype(bf16) for k in big],
                            [k not in CONV_WEIGHTS for k in big], "gather_weights")
    full = {}
    for k, g in zip(big, gathered):
        full[k] = join_cols(g) if k in COL_SHARDED else join_rows(g)

    layers = []
    for li in range(DEPTH):
        W = {k: full[k][li] for k in big}
        W.update({k: P[k][li] for k in REPLICATED})
        W["w_in_r"] = regroup_w_in(W["w_in"])
        layers.append(W)

    h = xs
    saved = []
    for li in range(DEPTH):
        h, R = layer_fwd(h, mems, layers[li], li)
        saved.append(R)
    loss_row, dy = loss_and_grad(h, tgt, "loss")
    loss = lax.psum(loss_row[0, 0], ("x", "y", "c"))

    grads = [None] * DEPTH
    for li in reversed(range(DEPTH)):
        dy, G = layer_bwd(dy, mems, layers[li], saved[li], li)
        G["w_in"] = ungroup_w_in(G.pop("w_in_r"))
        grads[li] = G
    grad_x = dy[None]

    stacked = {k: jnp.stack([grads[li][k] for li in range(DEPTH)]) for k in WEIGHTS}
    slots = [col_shards(stacked[k]) if k in COL_SHARDED else row_shards(stacked[k]) for k in big]
    slots = [s if k in CONV_WEIGHTS else s.astype(bf16) for k, s in zip(big, slots)]
    received = exchange_chips(slots, "exchange_grads")
    partial_sums = []
    for k, r in zip(big, received):
        shp = r.shape
        partial_sums.append(sum_slots(r.reshape(4, -1, shp[-1]), f"sum_{k}"))
    from_sibling = swap_sibling(partial_sums, "swap_partials")

    out_g, out_d, out_m, out_v = {}, {}, {}, {}
    for k, mine, theirs in zip(big, partial_sums, from_sibling):
        shp = P[k].shape
        flat = lambda t: t.reshape(-1, shp[-1])
        g, d, mn, vn = adamw(flat(P[k]), mine, theirs, flat(M[k]), flat(V[k]), f"adamw_{k}")
        out_g[k], out_d[k], out_m[k], out_v[k] = (t.reshape(shp) for t in (g, d, mn, vn))

    gsmall = sum_slots(gather_all(pack_small(stacked), "gather_small"), "sum_small")
    g, d, mn, vn = adamw(pack_small(P), gsmall, None, pack_small(M), pack_small(V), "adamw_small")
    for dst, packed in ((out_g, g), (out_d, d), (out_m, mn), (out_v, vn)):
        dst.update(unpack_small(packed, P))

    return (loss, grad_x, *[out_g[k] for k in WEIGHTS], *[out_d[k] for k in WEIGHTS],
            *[out_m[k] for k in WEIGHTS], *[out_v[k] for k in WEIGHTS])


def kernel(x, mem, ffn1_w_gate, ffn1_w_up, ffn1_w_down, ln_ffn1_g, ln_ffn1_b, w_in, gdn_conv_w, gdn_a_log, gdn_dt_bias, gdn_norm_g, fox_b_f, conf_dw_w, conf_dw_b, conf_norm_g, conf_norm_b, w_out, ln_mix_g, ln_mix_b, mem_w_q, mem_w_kv, mem_w_o, ln_mem_g, ln_mem_b, ffn2_w_gate, ffn2_w_up, ffn2_w_down, ln_ffn2_g, ln_ffn2_b, loss_target, m_ffn1_w_gate, m_ffn1_w_up, m_ffn1_w_down, m_ln_ffn1_g, m_ln_ffn1_b, m_w_in, m_gdn_conv_w, m_gdn_a_log, m_gdn_dt_bias, m_gdn_norm_g, m_fox_b_f, m_conf_dw_w, m_conf_dw_b, m_conf_norm_g, m_conf_norm_b, m_w_out, m_ln_mix_g, m_ln_mix_b, m_mem_w_q, m_mem_w_kv, m_mem_w_o, m_ln_mem_g, m_ln_mem_b, m_ffn2_w_gate, m_ffn2_w_up, m_ffn2_w_down, m_ln_ffn2_g, m_ln_ffn2_b, v_ffn1_w_gate, v_ffn1_w_up, v_ffn1_w_down, v_ln_ffn1_g, v_ln_ffn1_b, v_w_in, v_gdn_conv_w, v_gdn_a_log, v_gdn_dt_bias, v_gdn_norm_g, v_fox_b_f, v_conf_dw_w, v_conf_dw_b, v_conf_norm_g, v_conf_norm_b, v_w_out, v_ln_mix_g, v_ln_mix_b, v_mem_w_q, v_mem_w_kv, v_mem_w_o, v_ln_mem_g, v_ln_mem_b, v_ffn2_w_gate, v_ffn2_w_up, v_ffn2_w_down, v_ln_ffn2_g, v_ln_ffn2_b):
    a = locals()
    P = {k: a[k] for k in WEIGHTS}
    M = {k: a["m_" + k] for k in WEIGHTS}
    V = {k: a["v_" + k] for k in WEIGHTS}
    return _step(P, M, V, x, mem, loss_target)
```

```python
import functools

import jax
import jax.numpy as jnp
from jax import lax
from jax.experimental import pallas as pl
from jax.experimental.pallas import tpu as pltpu

f32 = jnp.float32
bf16 = jnp.bfloat16

D = 1024
F = 2816
GW = 256
HD = 64
NH = 4
CHUNK = 64
CONF_K = 31
GDN_K = 4
DEPTH = 2
ALPHA = float((2 * DEPTH) ** 0.25)
LN_EPS = 1e-5
RMS_EPS = 1e-6
L2_EPS = 1e-6
NEG = -1e30
PW = 3200
C_GQKV, C_GZ, C_FOX, C_CONF, C_SB, C_SMALL = 0, 768, 1024, 1792, 2304, 3072
ADAM_LR, ADAM_B1, ADAM_B2, ADAM_EPS, ADAM_WD, ADAM_STEP = 0.001, 0.9, 0.999, 1e-08, 0.01, 10
VMEM_LIMIT = 56 * 1024 * 1024
MESH = pl.DeviceIdType.MESH


def _cparams(sem):
    return pltpu.CompilerParams(dimension_semantics=sem, vmem_limit_bytes=VMEM_LIMIT)


def _pick(n, cands):
    for c in cands:
        if n % c == 0:
            return c
    return n


def _sds(shape, dtype=f32):
    return jax.ShapeDtypeStruct(shape, dtype)


def _layer_norm(z, g, b):
    mu = jnp.mean(z, axis=-1, keepdims=True)
    zc = z - mu
    var = jnp.mean(zc * zc, axis=-1, keepdims=True)
    return zc * lax.rsqrt(var + LN_EPS) * g + b


def _softplus(x):
    return jnp.maximum(x, 0.0) + jnp.log(1.0 + jnp.exp(-jnp.abs(x)))


def _neg_softplus(z):
    nz = -z
    return jnp.minimum(nz, 0.0) - jnp.log(1.0 + jnp.exp(jnp.minimum(z, nz)))


def _dsilu(x):
    s = jax.nn.sigmoid(x)
    return s * (1.0 + x * (1.0 - s))


def _split_hi_lo(x):
    hi = x.astype(bf16)
    lo = (x - hi.astype(f32)).astype(bf16)
    return hi, lo


def _dot(a, b):
    return jnp.dot(a, b, preferred_element_type=f32)


def _dot_nt(a, b):
    return lax.dot_general(a, b, (((1,), (1,)), ((), ())), preferred_element_type=f32)


def _dot_tn(a, b):
    return lax.dot_general(a, b, (((0,), (0,)), ((), ())), preferred_element_type=f32)


def mm_nn(a, w, name, out_dtype=f32, also_bf16=False):
    T, K = a.shape
    N = w.shape[1]
    tm = min(T, 512)
    tn = N if N <= 1024 else _pick(N, (640, 512))

    def body(a_ref, w_ref, *o_refs):
        r = _dot(a_ref[...].astype(bf16), w_ref[...].astype(bf16))
        o_refs[0][...] = r.astype(o_refs[0].dtype)
        if also_bf16:
            o_refs[1][...] = r.astype(bf16)

    out_shape = [_sds((T, N), out_dtype)]
    out_specs = [pl.BlockSpec((tm, tn), lambda i, j: (i, j))]
    if also_bf16:
        out_shape.append(_sds((T, N), bf16))
        out_specs.append(pl.BlockSpec((tm, tn), lambda i, j: (i, j)))
    res = pl.pallas_call(
        body, grid=(T // tm, N // tn),
        in_specs=[pl.BlockSpec((tm, K), lambda i, j: (i, 0)), pl.BlockSpec((K, tn), lambda i, j: (0, j))],
        out_specs=out_specs, out_shape=out_shape,
        compiler_params=_cparams(("parallel", "arbitrary")), name=name)(a, w)
    return res if also_bf16 else res[0]


def mm_nt(g, w, name, add=None, add_scale=1.0):
    T, N = g.shape
    K = w.shape[0]
    tm = min(T, 512)

    def body(*refs):
        if add is None:
            g_ref, w_ref, o_ref = refs
        else:
            g_ref, w_ref, add_ref, o_ref = refs
        r = _dot_nt(g_ref[...].astype(bf16), w_ref[...].astype(bf16))
        if add is not None:
            r = r + add_scale * add_ref[...]
        o_ref[...] = r

    in_specs = [pl.BlockSpec((tm, N), lambda i: (i, 0)), pl.BlockSpec((K, N), lambda i: (0, 0))]
    args = [g, w]
    if add is not None:
        in_specs.append(pl.BlockSpec((tm, K), lambda i: (i, 0)))
        args.append(add)
    return pl.pallas_call(
        body, grid=(T // tm,), in_specs=in_specs,
        out_specs=pl.BlockSpec((tm, K), lambda i: (i, 0)), out_shape=_sds((T, K)),
        compiler_params=_cparams(("parallel",)), name=name)(*args)


def mm_tn(a, g, name):
    T, K = a.shape
    N = g.shape[1]
    tt = min(T, 512)
    tk = K if K * N * 4 <= 14 * 1024 * 1024 else _pick(K, (512, 1408))

    def body(a_ref, g_ref, o_ref):
        @pl.when(pl.program_id(1) == 0)
        def _():
            o_ref[...] = jnp.zeros_like(o_ref)
        o_ref[...] += _dot_tn(a_ref[...].astype(bf16), g_ref[...].astype(bf16))

    return pl.pallas_call(
        body, grid=(K // tk, T // tt),
        in_specs=[pl.BlockSpec((tt, tk), lambda j, t: (t, j)), pl.BlockSpec((tt, N), lambda j, t: (t, 0))],
        out_specs=pl.BlockSpec((tk, N), lambda j, t: (j, 0)), out_shape=_sds((K, N)),
        compiler_params=_cparams(("parallel", "arbitrary")), name=name)(a, g)


FFN_TF = 1408


def ffn_fwd(x, wg, wu, wd, g, b, name):
    T = x.shape[0]
    tm = min(T, 512)
    nf = F // FFN_TF

    def body(x_ref, wg_ref, wu_ref, wd_ref, g_ref, b_ref, z_ref, y_ref, acc):
        j = pl.program_id(1)

        @pl.when(j == 0)
        def _():
            acc[...] = jnp.zeros_like(acc)

        xb = x_ref[...].astype(bf16)
        h = _dot(xb, wg_ref[...])
        u = _dot(xb, wu_ref[...])
        a = (h * jax.nn.sigmoid(h) * u).astype(bf16)
        acc[...] += _dot(a, wd_ref[...])

        @pl.when(j == nf - 1)
        def _():
            z = ALPHA * x_ref[...] + 0.5 * acc[...]
            z_ref[...] = z
            y_ref[...] = _layer_norm(z, g_ref[...], b_ref[...])

    row = pl.BlockSpec((tm, D), lambda i, j: (i, 0))
    vec = pl.BlockSpec((1, D), lambda i, j: (0, 0))
    return pl.pallas_call(
        body, grid=(T // tm, nf),
        in_specs=[row, pl.BlockSpec((D, FFN_TF), lambda i, j: (0, j)), pl.BlockSpec((D, FFN_TF), lambda i, j: (0, j)),
                  pl.BlockSpec((FFN_TF, D), lambda i, j: (j, 0)), vec, vec],
        out_specs=[row, row], out_shape=[_sds((T, D)), _sds((T, D))],
        scratch_shapes=[pltpu.VMEM((tm, D), f32)],
        compiler_params=_cparams(("parallel", "arbitrary")), name=name)(x, wg, wu, wd, g, b)


def ffn_bwd(dz, x, wg, wu, wd, name):
    T = x.shape[0]
    tm = min(T, 512)
    nf = F // FFN_TF

    def body(dz_ref, x_ref, wg_ref, wu_ref, wd_ref, dx_ref, a_ref, dh_ref, du_ref, dzh_ref, acc):
        j = pl.program_id(1)

        @pl.when(j == 0)
        def _():
            acc[...] = jnp.zeros_like(acc)

        dzh = (0.5 * dz_ref[...]).astype(bf16)
        xb = x_ref[...].astype(bf16)
        h = _dot(xb, wg_ref[...])
        u = _dot(xb, wu_ref[...])
        s = jax.nn.sigmoid(h)
        hs = h * s
        da = _dot_nt(dzh, wd_ref[...])
        du = (da * hs).astype(bf16)
        dh = (da * u * (s + hs * (1.0 - s))).astype(bf16)
        a_ref[...] = (hs * u).astype(bf16)
        dh_ref[...] = dh
        du_ref[...] = du
        acc[...] += _dot_nt(dh, wg_ref[...]) + _dot_nt(du, wu_ref[...])

        @pl.when(j == nf - 1)
        def _():
            dx_ref[...] = ALPHA * dz_ref[...] + acc[...]
            dzh_ref[...] = dzh

    row = pl.BlockSpec((tm, D), lambda i, j: (i, 0))
    wide = pl.BlockSpec((tm, FFN_TF), lambda i, j: (i, j))
    return pl.pallas_call(
        body, grid=(T // tm, nf),
        in_specs=[row, row, pl.BlockSpec((D, FFN_TF), lambda i, j: (0, j)), pl.BlockSpec((D, FFN_TF), lambda i, j: (0, j)),
                  pl.BlockSpec((FFN_TF, D), lambda i, j: (j, 0))],
        out_specs=[row, wide, wide, wide, row],
        out_shape=[_sds((T, D)), _sds((T, F), bf16), _sds((T, F), bf16), _sds((T, F), bf16), _sds((T, D), bf16)],
        scratch_shapes=[pltpu.VMEM((tm, D), f32)],
        compiler_params=_cparams(("parallel", "arbitrary")), name=name)(dz, x, wg, wu, wd)


def lin_res_ln(a, w, res, g, b, name):
    T, K = a.shape
    tm = min(T, 512)

    def body(a_ref, w_ref, res_ref, g_ref, b_ref, z_ref, y_ref):
        z = ALPHA * res_ref[...] + _dot(a_ref[...].astype(bf16), w_ref[...])
        z_ref[...] = z
        y_ref[...] = _layer_norm(z, g_ref[...], b_ref[...])

    row = pl.BlockSpec((tm, D), lambda i: (i, 0))
    vec = pl.BlockSpec((1, D), lambda i: (0, 0))
    return pl.pallas_call(
        body, grid=(T // tm,),
        in_specs=[pl.BlockSpec((tm, K), lambda i: (i, 0)), pl.BlockSpec((K, D), lambda i: (0, 0)), row, vec, vec],
        out_specs=[row, row], out_shape=[_sds((T, D)), _sds((T, D))],
        compiler_params=_cparams(("parallel",)), name=name)(a, w, res, g, b)


def ln_bwd(dy, z, g, name):
    T = z.shape[0]
    tm = min(T, 512)

    def body(dy_ref, z_ref, g_ref, dz_ref, dg_ref, db_ref):
        @pl.when(pl.program_id(0) == 0)
        def _():
            dg_ref[...] = jnp.zeros_like(dg_ref)
            db_ref[...] = jnp.zeros_like(db_ref)

        zv = z_ref[...]
        dy = dy_ref[...]
        mu = jnp.mean(zv, axis=-1, keepdims=True)
        zc = zv - mu
        rstd = lax.rsqrt(jnp.mean(zc * zc, axis=-1, keepdims=True) + LN_EPS)
        xh = zc * rstd
        dxh = dy * g_ref[...]
        m1 = jnp.mean(dxh, axis=-1, keepdims=True)
        m2 = jnp.mean(dxh * xh, axis=-1, keepdims=True)
        dz_ref[...] = rstd * (dxh - m1 - xh * m2)
        dg_ref[...] += jnp.sum(dy * xh, axis=0, keepdims=True)
        db_ref[...] += jnp.sum(dy, axis=0, keepdims=True)

    row = pl.BlockSpec((tm, D), lambda i: (i, 0))
    vec = pl.BlockSpec((1, D), lambda i: (0, 0))
    return pl.pallas_call(
        body, grid=(T // tm,), in_specs=[row, row, vec], out_specs=[row, vec, vec],
        out_shape=[_sds((T, D)), _sds((1, D)), _sds((1, D))],
        compiler_params=_cparams(("arbitrary",)), name=name)(dy, z, g)


def loss_and_grad(y, target, name):
    T = y.shape[0]
    tm = min(T, 512)

    def body(y_ref, t_ref, l_ref, dy_ref):
        @pl.when(pl.program_id(0) == 0)
        def _():
            l_ref[...] = jnp.zeros_like(l_ref)
        d = y_ref[...] - t_ref[...]
        dy_ref[...] = d * (1.0 / D)
        l_ref[...] += (0.5 / D) * jnp.sum(jnp.sum(d * d, axis=1, keepdims=True), axis=0, keepdims=True)

    row = pl.BlockSpec((tm, D), lambda i: (i, 0))
    return pl.pallas_call(
        body, grid=(T // tm,), in_specs=[row, row],
        out_specs=[pl.BlockSpec((1, 128), lambda i: (0, 0)), row],
        out_shape=[_sds((1, 128)), _sds((T, D))],
        compiler_params=_cparams(("arbitrary",)), name=name)(y, target)


def _shifted(ext, s):
    return ext if s == 0 else pltpu.roll(ext, s, 0)


def _halo_maps(tm, P, nblk):
    per = tm // P
    prev = lambda i, c: (jnp.maximum(i * per - 1, 0), c)
    nxt = lambda i, c: (jnp.minimum((i + 1) * per, nblk * per - 1), c)
    return prev, nxt


GDN_P = 8
CONF_P = 32


def gdn_conv_fwd(proj, w8, name):
    T = proj.shape[0]
    tm = min(T, 512)
    nblk = T // tm
    C = 3 * GW
    prev, _ = _halo_maps(tm, GDN_P, nblk)

    def body(xc_ref, xp_ref, w_ref, o_ref):
        i = pl.program_id(0)
        xp = jnp.where(i > 0, xp_ref[...], 0.0)
        ext = jnp.concatenate([xp, xc_ref[...]], axis=0)
        acc = jnp.zeros((tm, C), f32)
        for k in range(GDN_K):
            acc = acc + w_ref[k:k + 1, :] * _shifted(ext, GDN_K - 1 - k)[GDN_P:, :]
        o_ref[...] = acc * jax.nn.sigmoid(acc)

    return pl.pallas_call(
        body, grid=(nblk,),
        in_specs=[pl.BlockSpec((tm, C), lambda i: (i, 0)), pl.BlockSpec((GDN_P, C), lambda i: prev(i, 0)),
                  pl.BlockSpec((8, C), lambda i: (0, 0))],
        out_specs=pl.BlockSpec((tm, C), lambda i: (i, 0)), out_shape=_sds((T, C)),
        compiler_params=_cparams(("parallel",)), name=name)(proj, proj, w8)


def gdn_conv_bwd(dy, proj, w8, name):
    T = proj.shape[0]
    tm = min(T, 512)
    nblk = T // tm
    C = 3 * GW
    P = GDN_P
    prev, nxt = _halo_maps(tm, P, nblk)

    def body(dyc_ref, dyn_ref, xc_ref, xp_ref, xn_ref, w_ref, dx_ref, dw_ref):
        i = pl.program_id(0)

        @pl.when(i == 0)
        def _():
            dw_ref[...] = jnp.zeros_like(dw_ref)

        xp = jnp.where(i > 0, xp_ref[...], 0.0)
        last = i == nblk - 1
        xn = jnp.where(last, 0.0, xn_ref[...])
        dyn = jnp.where(last, 0.0, dyn_ref[...])
        ext = jnp.concatenate([xp, xc_ref[...], xn], axis=0)
        sh = [_shifted(ext, GDN_K - 1 - k)[P:, :] for k in range(GDN_K)]
        s = jnp.zeros((tm + P, C), f32)
        for k in range(GDN_K):
            s = s + w_ref[k:k + 1, :] * sh[k]
        ds = jnp.concatenate([dyc_ref[...], dyn], axis=0) * _dsilu(s)
        dx = jnp.zeros((tm, C), f32)
        for k in range(GDN_K):
            d = GDN_K - 1 - k
            moved = ds if d == 0 else pltpu.roll(ds, tm + P - d, 0)
            dx = dx + w_ref[k:k + 1, :] * moved[:tm, :]
            dw_ref[k:k + 1, :] += jnp.sum(ds[:tm, :] * sh[k][:tm, :], axis=0, keepdims=True)
        dx_ref[...] = dx

    col = lambda i: (i, 0)
    return pl.pallas_call(
        body, grid=(nblk,),
        in_specs=[pl.BlockSpec((tm, C), col), pl.BlockSpec((P, C), lambda i: nxt(i, 0)),
                  pl.BlockSpec((tm, C), col), pl.BlockSpec((P, C), lambda i: prev(i, 0)),
                  pl.BlockSpec((P, C), lambda i: nxt(i, 0)), pl.BlockSpec((8, C), lambda i: (0, 0))],
        out_specs=[pl.BlockSpec((tm, C), col), pl.BlockSpec((8, C), lambda i: (0, 0))],
        out_shape=[_sds((T, C)), _sds((8, C))],
        compiler_params=_cparams(("arbitrary",)), name=name)(dy, dy, proj, proj, proj, w8)


def _group_ones():
    r = lax.broadcasted_iota(jnp.int32, (GW, GW), 0) // HD
    c = lax.broadcasted_iota(jnp.int32, (GW, GW), 1) // HD
    return (r == c).astype(bf16)


def _group_mean(x, ones):
    hi, lo = _split_hi_lo(x)
    return (_dot(hi, ones) + _dot(lo, ones)) * (1.0 / HD)


def _conf_norm(c, g, b, ones):
    mu = _group_mean(c, ones)
    cc = c - mu
    rstd = lax.rsqrt(_group_mean(cc * cc, ones) + LN_EPS)
    hn = cc * rstd
    return hn, rstd, hn * g + b


CONF_VAL_BLK = C_CONF // GW
CONF_GATE_BLK = C_CONF // GW + 1


def conf_fwd(proj, w32, bias, ng, nb, name):
    T = proj.shape[0]
    tm = min(T, 512)
    nblk = T // tm
    P = CONF_P
    prev, _ = _halo_maps(tm, P, nblk)

    def body(vc_ref, gc_ref, vp_ref, gp_ref, w_ref, bias_ref, ng_ref, nb_ref, y_ref, c_ref):
        i = pl.program_id(0)
        pc = vc_ref[...] * jax.nn.sigmoid(gc_ref[...])
        pp = jnp.where(i > 0, vp_ref[...] * jax.nn.sigmoid(gp_ref[...]), 0.0)
        ext = jnp.concatenate([pp, pc], axis=0)
        acc = jnp.zeros((tm, GW), f32)
        for k in range(CONF_K):
            acc = acc + w_ref[k:k + 1, :] * _shifted(ext, CONF_K - 1 - k)[P:, :]
        c = acc + bias_ref[...]
        c_ref[...] = c
        _, _, yn = _conf_norm(c, ng_ref[...], nb_ref[...], _group_ones())
        y_ref[...] = yn * jax.nn.sigmoid(yn)

    vec = pl.BlockSpec((1, GW), lambda i: (0, 0))
    return pl.pallas_call(
        body, grid=(nblk,),
        in_specs=[pl.BlockSpec((tm, GW), lambda i: (i, CONF_VAL_BLK)), pl.BlockSpec((tm, GW), lambda i: (i, CONF_GATE_BLK)),
                  pl.BlockSpec((P, GW), lambda i: prev(i, CONF_VAL_BLK)), pl.BlockSpec((P, GW), lambda i: prev(i, CONF_GATE_BLK)),
                  pl.BlockSpec((32, GW), lambda i: (0, 0)), vec, vec, vec],
        out_specs=[pl.BlockSpec((tm, GW), lambda i: (i, 0))] * 2, out_shape=[_sds((T, GW))] * 2,
        compiler_params=_cparams(("parallel",)), name=name)(proj, proj, proj, proj, w32, bias, ng, nb)


def conf_bwd(dy, c, proj, w32, ng, nb, name):
    T = proj.shape[0]
    tm = min(T, 512)
    nblk = T // tm
    P = CONF_P
    prev, nxt = _halo_maps(tm, P, nblk)

    def body(dyc_ref, dyn_ref, cc_ref, cn_ref, vc_ref, gc_ref, vp_ref, gp_ref, w_ref, ng_ref, nb_ref,
             dglu_ref, dw_ref, dbias_ref, dng_ref, dnb_ref):
        i = pl.program_id(0)

        @pl.when(i == 0)
        def _():
            dw_ref[...] = jnp.zeros_like(dw_ref)
            dbias_ref[...] = jnp.zeros_like(dbias_ref)
            dng_ref[...] = jnp.zeros_like(dng_ref)
            dnb_ref[...] = jnp.zeros_like(dnb_ref)

        ones = _group_ones()
        g = ng_ref[...]

        def dc_of(dyv, cv):
            hn, rstd, yn = _conf_norm(cv, g, nb_ref[...], ones)
            dyn_ = dyv * _dsilu(yn)
            dhn = dyn_ * g
            dc = rstd * (dhn - _group_mean(dhn, ones) - hn * _group_mean(dhn * hn, ones))
            return dc, dyn_, hn

        dc_c, dyn_c, hn_c = dc_of(dyc_ref[...], cc_ref[...])
        dc_n, _, _ = dc_of(dyn_ref[...], cn_ref[...])
        dc_n = jnp.where(i == nblk - 1, 0.0, dc_n)
        dng_ref[...] += jnp.sum(dyn_c * hn_c, axis=0, keepdims=True)
        dnb_ref[...] += jnp.sum(dyn_c, axis=0, keepdims=True)
        dbias_ref[...] += jnp.sum(dc_c, axis=0, keepdims=True)

        sig_c = jax.nn.sigmoid(gc_ref[...])
        val_c = vc_ref[...]
        pc = val_c * sig_c
        pp = jnp.where(i > 0, vp_ref[...] * jax.nn.sigmoid(gp_ref[...]), 0.0)
        ext = jnp.concatenate([pp, pc], axis=0)
        dext = jnp.concatenate([dc_c, dc_n], axis=0)
        dp = jnp.zeros((tm, GW), f32)
        for k in range(CONF_K):
            d = CONF_K - 1 - k
            moved = dext if d == 0 else pltpu.roll(dext, tm + P - d, 0)
            dp = dp + w_ref[k:k + 1, :] * moved[:tm, :]
            dw_ref[k:k + 1, :] += jnp.sum(dc_c * _shifted(ext, d)[P:, :], axis=0, keepdims=True)
        dglu_ref[:, 0:GW] = dp * sig_c
        dglu_ref[:, GW:2 * GW] = dp * val_c * sig_c * (1.0 - sig_c)

    vec = pl.BlockSpec((1, GW), lambda i: (0, 0))
    blk = pl.BlockSpec((tm, GW), lambda i: (i, 0))
    return pl.pallas_call(
        body, grid=(nblk,),
        in_specs=[blk, pl.BlockSpec((P, GW), lambda i: nxt(i, 0)), blk, pl.BlockSpec((P, GW), lambda i: nxt(i, 0)),
                  pl.BlockSpec((tm, GW), lambda i: (i, CONF_VAL_BLK)), pl.BlockSpec((tm, GW), lambda i: (i, CONF_GATE_BLK)),
                  pl.BlockSpec((P, GW), lambda i: prev(i, CONF_VAL_BLK)), pl.BlockSpec((P, GW), lambda i: prev(i, CONF_GATE_BLK)),
                  pl.BlockSpec((32, GW), lambda i: (0, 0)), vec, vec],
        out_specs=[pl.BlockSpec((tm, 2 * GW), lambda i: (i, 0)), pl.BlockSpec((32, GW), lambda i: (0, 0)), vec, vec, vec],
        out_shape=[_sds((T, 2 * GW)), _sds((32, GW)), _sds((1, GW)), _sds((1, GW)), _sds((1, GW))],
        compiler_params=_cparams(("arbitrary",)), name=name)(dy, dy, c, c, proj, proj, proj, proj, w32, ng, nb)


def _mm_raw(a, b, ta, tb):
    ca = a.ndim - 2 if ta else a.ndim - 1
    cb = b.ndim - 1 if tb else b.ndim - 2
    batch = ((0,), (0,)) if a.ndim == 3 else ((), ())
    return lax.dot_general(a, b, (((ca,), (cb,)), batch), preferred_element_type=f32)


def _mm_prec(a, b, ta, tb, prec):
    if prec == 1:
        return _mm_raw(a.astype(bf16), b.astype(bf16), ta, tb)
    bh, bl = _split_hi_lo(b)
    if prec == 2:
        ab = a.astype(bf16)
        return _mm_raw(ab, bh, ta, tb) + _mm_raw(ab, bl, ta, tb)
    ah, al = _split_hi_lo(a)
    return _mm_raw(ah, bh, ta, tb) + (_mm_raw(ah, bl, ta, tb) + _mm_raw(al, bh, ta, tb))


@functools.partial(jax.custom_vjp, nondiff_argnums=(2, 3, 4))
def mm(a, b, ta=False, tb=False, prec=1):
    return _mm_prec(a, b, ta, tb, prec)


def _mm_fwd(a, b, ta, tb, prec):
    return _mm_prec(a, b, ta, tb, prec), (a, b)


def _mm_bwd(ta, tb, prec, res, ct):
    a, b = res
    da = _mm_prec(b, ct, tb, True, 1) if ta else _mm_prec(ct, b, False, not tb, 1)
    db = _mm_prec(ct, a, True, ta, 1) if tb else _mm_prec(a, ct, not ta, False, 2 if prec == 2 else 1)
    return da, db


mm.defvjp(_mm_fwd, _mm_bwd)


def _tri_inv_raw(l):
    n = -l
    rr = lax.broadcasted_iota(jnp.int32, l.shape, 1)
    cc = lax.broadcasted_iota(jnp.int32, l.shape, 2)
    p = jnp.where(rr == cc, 1.0, 0.0) + n
    for _ in range(5):
        n = _mm_prec(n, n, False, False, 3)
        p = p + _mm_prec(p, n, False, False, 3)
    return p


@jax.custom_vjp
def tri_inv(l):
    return _tri_inv_raw(l)


def _tri_inv_fwd(l):
    t = _tri_inv_raw(l)
    return t, t


def _tri_inv_bwd(t, ct):
    return (-_mm_prec(_mm_prec(t, ct, True, False, 1), t, False, True, 1),)


tri_inv.defvjp(_tri_inv_fwd, _tri_inv_bwd)


def _gdn_block(S, qs, ks, vs, a, b, z, alog, dtb, ng):
    shp = (NH, CHUNK, CHUNK)
    ii = lax.broadcasted_iota(jnp.int32, shp, 1)
    jj = lax.broadcasted_iota(jnp.int32, shp, 2)
    l_incl = jnp.where(ii >= jj, 1.0, 0.0)
    ys = []
    for c in range(len(qs)):
        q = qs[c] * lax.rsqrt(jnp.sum(qs[c] * qs[c], axis=-1, keepdims=True) + L2_EPS) * (HD ** -0.5)
        k = ks[c] * lax.rsqrt(jnp.sum(ks[c] * ks[c], axis=-1, keepdims=True) + L2_EPS)
        v = vs[c]
        beta = jax.nn.sigmoid(b[c])
        g = -jnp.exp(alog) * _softplus(a[c] + dtb)
        gcb = mm(l_incl, jnp.broadcast_to(g, shp), False, False, 2)
        gcr = jnp.swapaxes(gcb, 1, 2)
        decay = jnp.exp(jnp.where(ii >= jj, gcb - gcr, NEG))
        g_last = jnp.sum(jnp.where(ii == CHUNK - 1, gcb, 0.0), axis=1, keepdims=True)
        eg = jnp.exp(gcb)
        kb = k * beta
        lkk = jnp.where(ii > jj, mm(kb, k, False, True) * decay, 0.0)
        t_inv = tri_inv(lkk)
        u = mm(t_inv, v * beta)
        w = mm(t_inv, kb * eg)
        a_qk = jnp.where(ii >= jj, mm(q, k, False, True) * decay, 0.0)
        q_dec = q * eg
        k_dec = k * jnp.exp(g_last - gcb)
        v_new = u - mm(w, S)
        o = mm(q_dec, S) + mm(a_qk, v_new)
        S = S * jnp.exp(g_last) + mm(k_dec, v_new, True, False)
        y = o * lax.rsqrt(jnp.mean(o * o, axis=-1, keepdims=True) + RMS_EPS) * ng
        ys.append(y * (z[c] * jax.nn.sigmoid(z[c])))
    return S, ys


GDN_CB = 256


def _gdn_load(refs, nc):
    q_ref, k_ref, v_ref, z_ref, a_ref, b_ref = refs
    sl = lambda r, c: r[:, c * CHUNK:(c + 1) * CHUNK, :]
    return tuple([sl(r, c) for c in range(nc)] for r in (q_ref, k_ref, v_ref, z_ref, a_ref, b_ref))


def gdn_fwd(qkv_hm, z_hm, ab_hm, alog, dtb, ng, name):
    T = z_hm.shape[1]
    cb = min(T, GDN_CB)
    nc = cb // CHUNK
    nb = T // cb

    def body(q_ref, k_ref, v_ref, z_ref, a_ref, b_ref, alog_ref, dtb_ref, ng_ref, y_ref, s_ref, S):
        @pl.when(pl.program_id(0) == 0)
        def _():
            S[...] = jnp.zeros_like(S)
        s_ref[...] = S[...]
        qs, ks, vs, zs, as_, bs = _gdn_load((q_ref, k_ref, v_ref, z_ref, a_ref, b_ref), nc)
        s_out, ys = _gdn_block(S[...], qs, ks, vs, as_, bs, zs, alog_ref[...], dtb_ref[...], ng_ref[...])
        S[...] = s_out
        for c in range(nc):
            y_ref[:, c * CHUNK:(c + 1) * CHUNK, :] = ys[c]

    hm = lambda h0: pl.BlockSpec((NH, cb, HD), lambda i: (h0, i, 0))
    col = lambda h0: pl.BlockSpec((NH, cb, 1), lambda i: (h0, i, 0))
    par = pl.BlockSpec((NH, 1, 1), lambda i: (0, 0, 0))
    return pl.pallas_call(
        body, grid=(nb,),
        in_specs=[hm(0), hm(1), hm(2), hm(0), col(0), col(1), par, par, pl.BlockSpec((1, 1, HD), lambda i: (0, 0, 0))],
        out_specs=[hm(0), pl.BlockSpec((None, NH, HD, HD), lambda i: (i, 0, 0, 0))],
        out_shape=[_sds((NH, T, HD)), _sds((nb, NH, HD, HD))],
        scratch_shapes=[pltpu.VMEM((NH, HD, HD), f32)],
        compiler_params=_cparams(("arbitrary",)), name=name)(qkv_hm, qkv_hm, qkv_hm, z_hm, ab_hm, ab_hm, alog, dtb, ng)


def gdn_bwd(dy_hm, states, qkv_hm, z_hm, ab_hm, alog, dtb, ng, name):
    T = z_hm.shape[1]
    cb = min(T, GDN_CB)
    nc = cb // CHUNK
    nb = T // cb

    def body(dy_ref, s_ref, q_ref, k_ref, v_ref, z_ref, a_ref, b_ref, alog_ref, dtb_ref, ng_ref,
             dq_ref, dk_ref, dv_ref, dz_ref, da_ref, db_ref, dalog_ref, ddtb_ref, dng_ref, dS):
        @pl.when(pl.program_id(0) == 0)
        def _():
            dS[...] = jnp.zeros_like(dS)
            dalog_ref[...] = jnp.zeros_like(dalog_ref)
            ddtb_ref[...] = jnp.zeros_like(ddtb_ref)
            dng_ref[...] = jnp.zeros_like(dng_ref)

        qs, ks, vs, zs, as_, bs = _gdn_load((q_ref, k_ref, v_ref, z_ref, a_ref, b_ref), nc)
        _, vjp = jax.vjp(_gdn_block, s_ref[...], qs, ks, vs, as_, bs, zs, alog_ref[...], dtb_ref[...], ng_ref[...])
        dys = [dy_ref[:, c * CHUNK:(c + 1) * CHUNK, :] for c in range(nc)]
        d_s, dqs, dks, dvs, das, dbs, dzs, d_alog, d_dtb, d_ng = vjp((dS[...], dys))
        dS[...] = d_s
        dalog_ref[...] += d_alog
        ddtb_ref[...] += d_dtb
        dng_ref[...] += d_ng
        for c in range(nc):
            sl = slice(c * CHUNK, (c + 1) * CHUNK)
            dq_ref[:, sl, :] = dqs[c]
            dk_ref[:, sl, :] = dks[c]
            dv_ref[:, sl, :] = dvs[c]
            dz_ref[:, sl, :] = dzs[c]
            da_ref[:, sl, :] = das[c]
            db_ref[:, sl, :] = dbs[c]

    rev = lambda i: nb - 1 - i
    hm = lambda h0: pl.BlockSpec((NH, cb, HD), lambda i: (h0, rev(i), 0))
    col = lambda h0: pl.BlockSpec((NH, cb, 1), lambda i: (h0, rev(i), 0))
    par = pl.BlockSpec((NH, 1, 1), lambda i: (0, 0, 0))
    ngs = pl.BlockSpec((1, 1, HD), lambda i: (0, 0, 0))
    res = pl.pallas_call(
        body, grid=(nb,),
        in_specs=[hm(0), pl.BlockSpec((None, NH, HD, HD), lambda i: (rev(i), 0, 0, 0)),
                  hm(0), hm(1), hm(2), hm(0), col(0), col(1), par, par, ngs],
        out_specs=[hm(0), hm(0), hm(0), hm(0), col(0), col(0), par, par, ngs],
        out_shape=[_sds((NH, T, HD))] * 4 + [_sds((NH, T, 1))] * 2 + [_sds((NH, 1, 1))] * 2 + [_sds((1, 1, HD))],
        scratch_shapes=[pltpu.VMEM((NH, HD, HD), f32)],
        compiler_params=_cparams(("arbitrary",)), name=name)(dy_hm, states, qkv_hm, qkv_hm, qkv_hm, z_hm, ab_hm, ab_hm, alog, dtb, ng)
    dq, dk, dv, dz, da, db, dalog, ddtb, dng = res
    return jnp.concatenate([dq, dk, dv], axis=0), dz, jnp.concatenate([da, db], axis=0), dalog, ddtb, dng


F_LANE = 8
SCAN_TB = 256


def fox_gate_fwd(proj, bfv, name):
    T = proj.shape[0]
    tb = min(T, SCAN_TB)

    def body(x_ref, b_ref, o_ref, carry):
        @pl.when(pl.program_id(0) == 0)
        def _():
            carry[...] = jnp.zeros_like(carry)
        logf = -_softplus(-(x_ref[...] + b_ref[...]))
        r = lax.broadcasted_iota(jnp.int32, (tb, tb), 0)
        c = lax.broadcasted_iota(jnp.int32, (tb, tb), 1)
        tri = (r >= c).astype(bf16)
        hi, lo = _split_hi_lo(logf)
        cum = _dot(tri, hi) + _dot(tri, lo) + carry[0:1, :]
        o_ref[...] = cum
        carry[0:1, :] = cum[tb - 1:tb, :]

    return pl.pallas_call(
        body, grid=(T // tb,),
        in_specs=[pl.BlockSpec((tb, 128), lambda i: (i, C_SMALL // 128)), pl.BlockSpec((1, 128), lambda i: (0, 0))],
        out_specs=pl.BlockSpec((tb, 128), lambda i: (i, 0)), out_shape=_sds((T, 128)),
        scratch_shapes=[pltpu.VMEM((8, 128), f32)],
        compiler_params=_cparams(("arbitrary",)), name=name)(proj, bfv)


def fox_gate_bwd(dcum, proj, bfv, name):
    T = proj.shape[0]
    tb = min(T, SCAN_TB)
    nb = T // tb

    def body(d_ref, x_ref, b_ref, o_ref, db_ref, carry):
        @pl.when(pl.program_id(0) == 0)
        def _():
            carry[...] = jnp.zeros_like(carry)
            db_ref[...] = jnp.zeros_like(db_ref)
        r = lax.broadcasted_iota(jnp.int32, (tb, tb), 0)
        c = lax.broadcasted_iota(jnp.int32, (tb, tb), 1)
        tri = (c >= r).astype(bf16)
        hi, lo = _split_hi_lo(d_ref[...])
        dlogf = _dot(tri, hi) + _dot(tri, lo) + carry[0:1, :]
        carry[0:1, :] = dlogf[0:1, :]
        lane = lax.broadcasted_iota(jnp.int32, (tb, 128), 1)
        keep = (lane >= F_LANE) & (lane < F_LANE + NH)
        dx = jnp.where(keep, dlogf * jax.nn.sigmoid(-(x_ref[...] + b_ref[...])), 0.0)
        o_ref[...] = dx
        db_ref[...] += jnp.sum(dx, axis=0, keepdims=True)

    rev = lambda i: nb - 1 - i
    return pl.pallas_call(
        body, grid=(nb,),
        in_specs=[pl.BlockSpec((tb, 128), lambda i: (rev(i), 0)), pl.BlockSpec((tb, 128), lambda i: (rev(i), C_SMALL // 128)),
                  pl.BlockSpec((1, 128), lambda i: (0, 0))],
        out_specs=[pl.BlockSpec((tb, 128), lambda i: (rev(i), 0)), pl.BlockSpec((1, 128), lambda i: (0, 0))],
        out_shape=[_sds((T, 128)), _sds((1, 128))],
        scratch_shapes=[pltpu.VMEM((8, 128), f32)],
        compiler_params=_cparams(("arbitrary",)), name=name)(dcum, proj, bfv)


ATT_TQ = 256


def _lane_col(tile, lane):
    li = lax.broadcasted_iota(jnp.int32, tile.shape, 1)
    return jnp.sum(jnp.where(li == lane, tile, 0.0), axis=1, keepdims=True)


def _pack_cols(cols):
    rows = cols[0].shape[0]
    li = lax.broadcasted_iota(jnp.int32, (rows, 128), 1)
    out = jnp.zeros((rows, 128), f32)
    for h, cv in enumerate(cols):
        out = jnp.where(li == h, cv, out)
    return out


def _head_masks():
    li = lax.broadcasted_iota(jnp.int32, (1, 128), 1)
    return [li < HD, li >= HD]


def _qkv_specs(T, tq, base_blk):
    q = pl.BlockSpec((tq, 128), lambda p, i: (i, base_blk + p))
    k = pl.BlockSpec((T, 128), lambda p, i: (0, base_blk + 2 + p))
    v = pl.BlockSpec((T, 128), lambda p, i: (0, base_blk + 4 + p))
    return q, k, v


def _stack_heads(x, masks):
    return jnp.concatenate([jnp.where(m, x, jnp.zeros_like(x)) for m in masks], axis=0)


def _side_by_side(x, tq):
    return jnp.concatenate([x[:tq], x[tq:]], axis=1)


def _stacked_mask(tq, strict):
    r = lax.broadcasted_iota(jnp.int32, (2 * tq, tq), 0)
    r = jnp.where(r >= tq, r - tq, r)
    c = lax.broadcasted_iota(jnp.int32, (2 * tq, tq), 1)
    return c < r if strict else c <= r


def _sub_head_rows(s, rows2, tq):
    return jnp.concatenate([s[:tq] - rows2[0:1, :], s[tq:] - rows2[1:2, :]], axis=0)


def _lane_cols2(tile):
    return jnp.concatenate([_lane_col(tile, 0), _lane_col(tile, 1)], axis=0)


def _pack_cols2(col, tq):
    return _pack_cols([col[:tq], col[tq:]])


def _dot_hilo2(x, tri):
    n = x.shape[0]
    hi, lo = _split_hi_lo(x)
    r = _dot(jnp.concatenate([hi, lo], axis=0), tri)
    return r[:n] + r[n:]


def fox_fwd(pbf, cumrow, name):
    T = pbf.shape[0]
    tq = min(T, ATT_TQ)
    nq = T // tq
    scale = HD ** -0.5

    def body(q_ref, k_ref, v_ref, cr_ref, o_ref, lse_ref):
        i = pl.program_id(1)
        masks = _head_masks()
        qs2 = _stack_heads(q_ref[...] * scale, masks)

        def tile(kb, carry, diag):
            m, l, acc = carry
            off = pl.multiple_of(kb * tq, tq)
            v2 = _stack_heads(v_ref[pl.ds(off, tq), :], masks)
            s = _sub_head_rows(_dot_nt(qs2, k_ref[pl.ds(off, tq), :]), cr_ref[kb], tq)
            if diag:
                s = jnp.where(_stacked_mask(tq, False), s, NEG)
            m_new = jnp.maximum(m, jnp.max(s, axis=1, keepdims=True))
            corr = jnp.exp(m - m_new)
            p = jnp.exp(s - m_new)
            l = l * corr + jnp.sum(p, axis=1, keepdims=True)
            acc = acc * jnp.where(masks[0], corr[:tq], corr[tq:]) + _dot(_side_by_side(p.astype(bf16), tq), v2)
            return m_new, l, acc

        init = (jnp.full((2 * tq, 1), NEG, f32), jnp.zeros((2 * tq, 1), f32), jnp.zeros((tq, 128), f32))
        carry = lax.fori_loop(0, i, lambda kb, c: tile(kb, c, False), init)
        m, l, acc = tile(i, carry, True)
        o_ref[...] = acc * jnp.where(masks[0], 1.0 / l[:tq], 1.0 / l[tq:])
        lse_ref[...] = _pack_cols2(m + jnp.log(l), tq)

    qs, ks, vs = _qkv_specs(T, tq, C_FOX // 128)
    return pl.pallas_call(
        body, grid=(2, nq),
        in_specs=[qs, ks, vs, pl.BlockSpec((None, nq, 8, tq), lambda p, i: (p, 0, 0, 0))],
        out_specs=[pl.BlockSpec((tq, 128), lambda p, i: (i, p)), pl.BlockSpec((None, tq, 128), lambda p, i: (p, i, 0))],
        out_shape=[_sds((T, GW)), _sds((2, T, 128))],
        compiler_params=_cparams(("parallel", "parallel")), name=name)(pbf, pbf, pbf, cumrow)


def fox_bwd(do, o, lse, pbf, cumrow, name):
    T = pbf.shape[0]
    tq = min(T, ATT_TQ)
    nq = T // tq
    scale = HD ** -0.5

    def body(do_ref, o_ref, lse_ref, q_ref, k_ref, v_ref, cr_ref, dq_ref, dk_ref, dv_ref, dc_ref, dcq_ref):
        i = pl.program_id(1)

        @pl.when(i == 0)
        def _():
            dk_ref[...] = jnp.zeros_like(dk_ref)
            dv_ref[...] = jnp.zeros_like(dv_ref)
            dc_ref[...] = jnp.zeros_like(dc_ref)

        masks = _head_masks()
        dov = do_ref[...]
        qs2 = _stack_heads(q_ref[...] * scale, masks)
        do2 = _stack_heads(dov.astype(bf16), masks)
        prod = dov * o_ref[...]
        delta = jnp.concatenate([jnp.sum(jnp.where(m, prod, 0.0), axis=1, keepdims=True) for m in masks], axis=0)
        lse2 = _lane_cols2(lse_ref[...])

        def tile(kb, carry, diag):
            dq, rsum = carry
            off = pl.multiple_of(kb * tq, tq)
            kblk = k_ref[pl.ds(off, tq), :]
            p = jnp.exp(_sub_head_rows(_dot_nt(qs2, kblk), cr_ref[kb], tq) - lse2)
            if diag:
                p = jnp.where(_stacked_mask(tq, False), p, 0.0)
            dp = _dot_nt(do2, v_ref[pl.ds(off, tq), :])
            ds = p * (dp - delta)
            dsb = ds.astype(bf16)
            dq = dq + _dot(_side_by_side(dsb, tq), _stack_heads(kblk, masks))
            dk_ref[pl.ds(off, tq), :] += _dot_tn(dsb, qs2)
            dv_ref[pl.ds(off, tq), :] += _dot_tn(p.astype(bf16), do2)
            dc_ref[kb, 0:1, :] += -jnp.sum(ds[:tq], axis=0, keepdims=True)
            dc_ref[kb, 1:2, :] += -jnp.sum(ds[tq:], axis=0, keepdims=True)
            return dq, rsum + jnp.sum(ds, axis=1, keepdims=True)

        init = (jnp.zeros((tq, 128), f32), jnp.zeros((2 * tq, 1), f32))
        carry = lax.fori_loop(0, i, lambda kb, c: tile(kb, c, False), init)
        dq, rsum = tile(i, carry, True)
        dq_ref[...] = dq * scale
        dcq_ref[...] = _pack_cols2(rsum, tq)

    qs, ks, vs = _qkv_specs(T, tq, C_FOX // 128)
    tile_spec = pl.BlockSpec((tq, 128), lambda p, i: (i, p))
    pair = pl.BlockSpec((None, tq, 128), lambda p, i: (p, i, 0))
    rowsp = pl.BlockSpec((None, nq, 8, tq), lambda p, i: (p, 0, 0, 0))
    full = pl.BlockSpec((T, 128), lambda p, i: (0, p))
    return pl.pallas_call(
        body, grid=(2, nq),
        in_specs=[tile_spec, tile_spec, pair, qs, ks, vs, rowsp],
        out_specs=[tile_spec, full, full, rowsp, pair],
        out_shape=[_sds((T, GW)), _sds((T, GW)), _sds((T, GW)), _sds((2, nq, 8, tq)), _sds((2, T, 128))],
        compiler_params=_cparams(("parallel", "arbitrary")), name=name)(do, o, lse, pbf, pbf, pbf, cumrow)


def _tri(tq, pred):
    r = lax.broadcasted_iota(jnp.int32, (tq, tq), 0)
    c = lax.broadcasted_iota(jnp.int32, (tq, tq), 1)
    return pred(r, c).astype(bf16)


def _ride(body, n_in, n_out, rider, grid):
    if rider is None:
        return body, [], [], [], [], []
    nr = rider.n

    def wrapped(*refs):
        ins, rin = refs[:n_in], refs[n_in:n_in + nr]
        outs = refs[n_in + nr:n_in + nr + n_out]
        rout, sems = refs[n_in + nr + n_out:n_in + 2 * nr + n_out], refs[n_in + 2 * nr + n_out:]
        ids = [pl.program_id(a) for a in range(len(grid))]
        first = functools.reduce(jnp.logical_and, [i == 0 for i in ids])
        last = functools.reduce(jnp.logical_and, [i == g - 1 for i, g in zip(ids, grid)])

        @pl.when(first)
        def _():
            rider.start(rin, rout, sems)

        body(*ins, *outs)

        @pl.when(last)
        def _():
            rider.finish(rin, rout, sems)

    anyspec = pl.BlockSpec(memory_space=pl.ANY)
    return wrapped, [anyspec] * nr, list(rider.arrays), [anyspec] * nr, list(rider.out_shape), list(rider.scratch)


def sb_fwd(pbf, name, rider=None):
    T = pbf.shape[0]
    tq = min(T, ATT_TQ)
    nq = T // tq
    scale = HD ** -0.5

    def body(q_ref, k_ref, v_ref, o_ref, tot_ref):
        i = pl.program_id(1)
        masks = _head_masks()
        qs2 = _stack_heads(q_ref[...] * scale, masks)
        after = _tri(tq, lambda r, c: r > c)

        def tile(kb, carry, diag):
            rs, acc = carry
            off = pl.multiple_of(kb * tq, tq)
            z = _dot_nt(qs2, k_ref[pl.ds(off, tq), :])
            lk = _neg_softplus(z)
            if diag:
                lk = jnp.where(_stacked_mask(tq, True), lk, 0.0)
            w = jnp.exp(z + lk + (_dot_hilo2(lk, after) + rs))
            if diag:
                w = jnp.where(_stacked_mask(tq, True), w, 0.0)
            acc = acc + _dot(_side_by_side(w.astype(bf16), tq), _stack_heads(v_ref[pl.ds(off, tq), :], masks))
            return rs + jnp.sum(lk, axis=1, keepdims=True), acc

        carry = tile(i, (jnp.zeros((2 * tq, 1), f32), jnp.zeros((tq, 128), f32)), True)
        rs, acc = lax.fori_loop(0, i, lambda n, c: tile(i - 1 - n, c, False), carry)
        o_ref[...] = acc
        tot_ref[...] = _pack_cols2(rs, tq)

    qs, ks, vs = _qkv_specs(T, tq, C_SB // 128)
    body, r_in, r_args, r_out, r_shape, r_scratch = _ride(body, 3, 2, rider, (2, nq))
    res = pl.pallas_call(
        body, grid=(2, nq), in_specs=[qs, ks, vs] + r_in,
        out_specs=[pl.BlockSpec((tq, 128), lambda p, i: (i, p)), pl.BlockSpec((None, tq, 128), lambda p, i: (p, i, 0))] + r_out,
        out_shape=[_sds((T, GW)), _sds((2, T, 128))] + r_shape, scratch_shapes=r_scratch,
        compiler_params=_cparams(("arbitrary", "arbitrary")), name=name)(pbf, pbf, pbf, *r_args)
    return res[0], res[1], res[2:]


def sb_bwd(do, tot, pbf, name, rider=None):
    T = pbf.shape[0]
    tq = min(T, ATT_TQ)
    nq = T // tq
    scale = HD ** -0.5

    def body(do_ref, tot_ref, q_ref, k_ref, v_ref, dq_ref, dk_ref, dv_ref):
        i = pl.program_id(1)

        @pl.when(i == 0)
        def _():
            dk_ref[...] = jnp.zeros_like(dk_ref)
            dv_ref[...] = jnp.zeros_like(dv_ref)

        masks = _head_masks()
        qs2 = _stack_heads(q_ref[...] * scale, masks)
        do2 = _stack_heads(do_ref[...].astype(bf16), masks)
        tot2 = _lane_cols2(tot_ref[...])
        upto = _tri(tq, lambda r, c: r <= c)
        before = _tri(tq, lambda r, c: r < c)

        def tile(kb, carry, diag):
            pre, cg, dq = carry
            off = pl.multiple_of(kb * tq, tq)
            kblk = k_ref[pl.ds(off, tq), :]
            z = _dot_nt(qs2, kblk)
            lk = _neg_softplus(z)
            keep = jnp.exp(lk)
            if diag:
                lk = jnp.where(_stacked_mask(tq, True), lk, 0.0)
            w = jnp.exp(z + lk + (tot2 - (pre + _dot_hilo2(lk, upto))))
            if diag:
                w = jnp.where(_stacked_mask(tq, True), w, 0.0)
            gmat = w * _dot_nt(do2, v_ref[pl.ds(off, tq), :])
            cmat = cg + _dot(gmat.astype(bf16), before)
            dz = gmat * keep - cmat * (1.0 - keep)
            if diag:
                dz = jnp.where(_stacked_mask(tq, True), dz, 0.0)
            dzb = dz.astype(bf16)
            dq = dq + _dot(_side_by_side(dzb, tq), _stack_heads(kblk, masks))
            dk_ref[pl.ds(off, tq), :] += _dot_tn(dzb, qs2)
            dv_ref[pl.ds(off, tq), :] += _dot_tn(w.astype(bf16), do2)
            return pre + jnp.sum(lk, axis=1, keepdims=True), cg + jnp.sum(gmat, axis=1, keepdims=True), dq

        zc = jnp.zeros((2 * tq, 1), f32)
        carry = lax.fori_loop(0, i, lambda kb, c: tile(kb, c, False), (zc, zc, jnp.zeros((tq, 128), f32)))
        _, _, dq = tile(i, carry, True)
        dq_ref[...] = dq * scale

    qs, ks, vs = _qkv_specs(T, tq, C_SB // 128)
    tile_spec = pl.BlockSpec((tq, 128), lambda p, i: (i, p))
    pair = pl.BlockSpec((None, tq, 128), lambda p, i: (p, i, 0))
    full = pl.BlockSpec((T, 128), lambda p, i: (0, p))
    body, r_in, r_args, r_out, r_shape, r_scratch = _ride(body, 5, 3, rider, (2, nq))
    res = pl.pallas_call(
        body, grid=(2, nq), in_specs=[tile_spec, pair, qs, ks, vs] + r_in,
        out_specs=[tile_spec, full, full] + r_out, out_shape=[_sds((T, GW))] * 3 + r_shape, scratch_shapes=r_scratch,
        compiler_params=_cparams(("arbitrary", "arbitrary")), name=name)(do, tot, pbf, pbf, pbf, *r_args)
    return res[0], res[1], res[2], res[3:]


MEM_HD = D // 4


def mem_fwd(q, kv, name):
    T = q.shape[0]
    M = kv.shape[0]
    tm = min(T, 512)
    scale = MEM_HD ** -0.5

    def body(q_ref, kv_ref, o_ref):
        for h in range(4):
            sl = slice(h * MEM_HD, (h + 1) * MEM_HD)
            kh = kv_ref[:, sl].astype(bf16)
            vh = kv_ref[:, D + h * MEM_HD:D + (h + 1) * MEM_HD].astype(bf16)
            s = _dot_nt(q_ref[:, sl], kh) * scale
            e = jnp.exp(s - jnp.max(s, axis=1, keepdims=True))
            p = e / jnp.sum(e, axis=1, keepdims=True)
            o_ref[:, sl] = _dot(p.astype(bf16), vh).astype(bf16)

    return pl.pallas_call(
        body, grid=(T // tm,),
        in_specs=[pl.BlockSpec((tm, D), lambda i: (i, 0)), pl.BlockSpec((M, 2 * D), lambda i: (0, 0))],
        out_specs=pl.BlockSpec((tm, D), lambda i: (i, 0)), out_shape=_sds((T, D), bf16),
        compiler_params=_cparams(("parallel",)), name=name)(q, kv)


def mem_bwd(do, q, kv, name):
    T = q.shape[0]
    M = kv.shape[0]
    tm = min(T, 512)
    scale = MEM_HD ** -0.5

    def body(do_ref, q_ref, kv_ref, dq_ref, dkv_ref):
        @pl.when(pl.program_id(0) == 0)
        def _():
            dkv_ref[...] = jnp.zeros_like(dkv_ref)
        for h in range(4):
            sl = slice(h * MEM_HD, (h + 1) * MEM_HD)
            vsl = slice(D + h * MEM_HD, D + (h + 1) * MEM_HD)
            qh = q_ref[:, sl]
            kh = kv_ref[:, sl].astype(bf16)
            vh = kv_ref[:, vsl].astype(bf16)
            doh = do_ref[:, sl].astype(bf16)
            s = _dot_nt(qh, kh) * scale
            e = jnp.exp(s - jnp.max(s, axis=1, keepdims=True))
            p = e / jnp.sum(e, axis=1, keepdims=True)
            dp = _dot_nt(doh, vh)
            ds = p * (dp - jnp.sum(dp * p, axis=1, keepdims=True))
            dsb = ds.astype(bf16)
            dq_ref[:, sl] = _dot(dsb, kh) * scale
            dkv_ref[:, sl] += _dot_tn(dsb, qh) * scale
            dkv_ref[:, vsl] += _dot_tn(p.astype(bf16), doh)

    row = pl.BlockSpec((tm, D), lambda i: (i, 0))
    whole = pl.BlockSpec((M, 2 * D), lambda i: (0, 0))
    return pl.pallas_call(
        body, grid=(T // tm,), in_specs=[row, row, whole], out_specs=[row, whole],
        out_shape=[_sds((T, D)), _sds((M, 2 * D))],
        compiler_params=_cparams(("arbitrary",)), name=name)(do, q, kv)


def _chip_peers():
    x, y, c = lax.axis_index("x"), lax.axis_index("y"), lax.axis_index("c")
    me = 2 * x + y
    peers = [((1 - x, y, c), 2 * (1 - x) + y), ((x, 1 - y, c), 2 * x + (1 - y)), ((1 - x, 1 - y, c), 2 * (1 - x) + (1 - y))]
    return me, peers


class Rider:
    def __init__(self, arrays, out_shape, scratch, start, finish):
        self.arrays, self.out_shape, self.scratch, self.start, self.finish = arrays, out_shape, scratch, start, finish
        self.n = len(arrays)

    def split(self, refs):
        return refs[:self.n], refs[self.n:2 * self.n], refs[2 * self.n:]


def run_rider(rider, name):
    def body(*refs):
        parts = rider.split(refs)
        rider.start(*parts)
        rider.finish(*parts)

    anyspec = pl.BlockSpec(memory_space=pl.ANY)
    return pl.pallas_call(
        body, in_specs=[anyspec] * rider.n, out_specs=[anyspec] * rider.n, out_shape=rider.out_shape,
        scratch_shapes=rider.scratch, name=name)(*rider.arrays)


def gather_rider(arrs, split):
    n = len(arrs)

    def copies(ins, outs, sems):
        send_sems, recv_sems, pass_send, pass_recv, loc_sems = sems
        x, y, c = lax.axis_index("x"), lax.axis_index("y"), lax.axis_index("c")
        me, peers = _chip_peers()

        def mine(ref, i):
            if not split[i]:
                return ref
            r = arrs[i].shape[1] // 2
            return ref.at[:, pl.ds(c * r, r), :]

        local, fetch, passed = [], [], []
        for i in range(n):
            local.append(pltpu.make_async_copy(ins[i], outs[i].at[me], loc_sems.at[i]))
            for k, (dev, pj) in enumerate(peers):
                fetch.append(pltpu.make_async_remote_copy(src_ref=mine(ins[i], i), dst_ref=mine(outs[i].at[me], i),
                                                          send_sem=send_sems.at[i, k], recv_sem=recv_sems.at[i, k],
                                                          device_id=dev, device_id_type=MESH))
                rows = mine(outs[i].at[pj], i)
                passed.append(pltpu.make_async_remote_copy(src_ref=rows, dst_ref=rows, send_sem=pass_send.at[i, k],
                                                           recv_sem=pass_recv.at[i, k], device_id=(x, y, 1 - c),
                                                           device_id_type=MESH) if split[i] else None)
        return local, fetch, passed

    def start(ins, outs, sems):
        local, fetch, _ = copies(ins, outs, sems)
        for cp in local + fetch:
            cp.start()

    def finish(ins, outs, sems):
        local, fetch, passed = copies(ins, outs, sems)
        for cp, fw in zip(fetch, passed):
            cp.wait_recv()
            if fw is not None:
                fw.start()
        for cp in fetch:
            cp.wait_send()
        for cp in [fw for fw in passed if fw is not None] + local:
            cp.wait()

    sem = pltpu.SemaphoreType.DMA((n, 3))
    return Rider(list(arrs), [_sds((4,) + a.shape, a.dtype) for a in arrs],
                 [sem, sem, sem, sem, pltpu.SemaphoreType.DMA((n,))], start, finish)


def exchange_rider(stacks):
    n = len(stacks)

    def copies(ins, outs, sems):
        send_sems, recv_sems, loc_sems = sems
        me, peers = _chip_peers()
        out = []
        for i in range(n):
            out.append(pltpu.make_async_copy(ins[i].at[me], outs[i].at[me], loc_sems.at[i]))
            for k, (dev, pj) in enumerate(peers):
                out.append(pltpu.make_async_remote_copy(src_ref=ins[i].at[pj], dst_ref=outs[i].at[me], send_sem=send_sems.at[i, k],
                                                        recv_sem=recv_sems.at[i, k], device_id=dev, device_id_type=MESH))
        return out

    def start(ins, outs, sems):
        for cp in copies(ins, outs, sems):
            cp.start()

    def finish(ins, outs, sems):
        for cp in copies(ins, outs, sems):
            cp.wait()

    return Rider(list(stacks), [_sds(a.shape, a.dtype) for a in stacks],
                 [pltpu.SemaphoreType.DMA((n, 3)), pltpu.SemaphoreType.DMA((n, 3)), pltpu.SemaphoreType.DMA((n,))], start, finish)


def swap_sibling(arrs, name):
    n = len(arrs)

    def body(*refs):
        ins, outs = refs[:n], refs[n:2 * n]
        send_sems, recv_sems = refs[2 * n:]
        x, y, c = lax.axis_index("x"), lax.axis_index("y"), lax.axis_index("c")
        started = []
        for i in range(n):
            cp = pltpu.make_async_remote_copy(src_ref=ins[i], dst_ref=outs[i], send_sem=send_sems.at[i],
                                              recv_sem=recv_sems.at[i], device_id=(x, y, 1 - c), device_id_type=MESH)
            cp.start()
            started.append(cp)
        for cp in started:
            cp.wait()

    anyspec = pl.BlockSpec(memory_space=pl.ANY)
    return pl.pallas_call(
        body, in_specs=[anyspec] * n, out_specs=[anyspec] * n,
        out_shape=[_sds(a.shape, a.dtype) for a in arrs],
        scratch_shapes=[pltpu.SemaphoreType.DMA((n,)), pltpu.SemaphoreType.DMA((n,))],
        name=name)(*arrs)


def gather_all(a, name):
    def body(a_ref, o_ref, send_sems, recv_sems, loc_sem):
        x, y, c = lax.axis_index("x"), lax.axis_index("y"), lax.axis_index("c")
        me = 4 * x + 2 * y + c
        loc = pltpu.make_async_copy(a_ref, o_ref.at[me], loc_sem)
        loc.start()
        started = [loc]
        for k in range(1, 8):
            dev = (x ^ (k >> 2), y ^ ((k >> 1) & 1), c ^ (k & 1))
            cp = pltpu.make_async_remote_copy(src_ref=a_ref, dst_ref=o_ref.at[me], send_sem=send_sems.at[k - 1],
                                              recv_sem=recv_sems.at[k - 1], device_id=dev, device_id_type=MESH)
            cp.start()
            started.append(cp)
        for cp in started:
            cp.wait()

    anyspec = pl.BlockSpec(memory_space=pl.ANY)
    return pl.pallas_call(
        body, in_specs=[anyspec], out_specs=anyspec, out_shape=_sds((8,) + a.shape, a.dtype),
        scratch_shapes=[pltpu.SemaphoreType.DMA((7,)), pltpu.SemaphoreType.DMA((7,)), pltpu.SemaphoreType.DMA(())],
        name=name)(a)


def sum_slots(stack, name):
    n, R, C = stack.shape
    tr = R if R <= 512 else _pick(R, (512, 352, 256))

    def body(s_ref, o_ref):
        acc = s_ref[0].astype(f32)
        for j in range(1, n):
            acc = acc + s_ref[j].astype(f32)
        o_ref[...] = acc

    return pl.pallas_call(
        body, grid=(R // tr,), in_specs=[pl.BlockSpec((n, tr, C), lambda i: (0, i, 0))],
        out_specs=pl.BlockSpec((tr, C), lambda i: (i, 0)), out_shape=_sds((R, C)),
        compiler_params=_cparams(("parallel",)), name=name)(stack)


def adamw(w, g1, g2, m, v, name):
    R, C = w.shape
    tr = R if R <= 512 else _pick(R, (512, 352, 256))
    c1 = 1.0 - ADAM_B1 ** ADAM_STEP
    c2 = 1.0 - ADAM_B2 ** ADAM_STEP

    def body(*refs):
        if g2 is None:
            w_ref, g1_ref, m_ref, v_ref, g_out, d_out, m_out, v_out = refs
            g = g1_ref[...]
        else:
            w_ref, g1_ref, g2_ref, m_ref, v_ref, g_out, d_out, m_out, v_out = refs
            g = g1_ref[...] + g2_ref[...]
        mn = ADAM_B1 * m_ref[...] + (1.0 - ADAM_B1) * g
        vn = ADAM_B2 * v_ref[...] + (1.0 - ADAM_B2) * (g * g)
        g_out[...] = g
        m_out[...] = mn
        v_out[...] = vn
        d_out[...] = -ADAM_LR * ((mn / c1) / (jnp.sqrt(vn / c2) + ADAM_EPS) + ADAM_WD * w_ref[...])

    blk = pl.BlockSpec((tr, C), lambda i: (i, 0))
    args = [w, g1] + ([] if g2 is None else [g2]) + [m, v]
    return pl.pallas_call(
        body, grid=(R // tr,), in_specs=[blk] * len(args), out_specs=[blk] * 4, out_shape=[_sds((R, C))] * 4,
        compiler_params=_cparams(("parallel",)), name=name)(*args)


IN_SPLITS = (768, 256, 4, 4, 768, 4, 512, 768)
IN_OFF = (0, 768, 1024, 1028, 1032, 1800, 1804, 2316, 3084)


def regroup_w_in(w):
    seg = lambda i: w[:, IN_OFF[i]:IN_OFF[i + 1]]
    pad = jnp.zeros((w.shape[0], PW - C_SMALL - 12), w.dtype)
    return jnp.concatenate([seg(0), seg(1), seg(4), seg(6), seg(7), seg(2), seg(3), seg(5), pad], axis=1)


def ungroup_w_in(g):
    s = C_SMALL
    return jnp.concatenate([g[:, 0:1024], g[:, s:s + 8], g[:, 1024:1792], g[:, s + 8:s + 12], g[:, 1792:3072]], axis=1)


def to_hm(t, nh):
    T = t.shape[0]
    return t.reshape(T, nh, HD).transpose(1, 0, 2)


def from_hm(t):
    nh, T, _ = t.shape
    return t.transpose(1, 0, 2).reshape(T, nh * HD)


def col_shards(w):
    c = w.shape[-1] // 4
    return jnp.moveaxis(w.reshape(w.shape[:-1] + (4, c)), -2, 0)


def row_shards(w):
    L, r4, c = w.shape
    return w.reshape(L, 4, r4 // 4, c).transpose(1, 0, 2, 3)


def join_cols(g):
    return jnp.moveaxis(g, 0, -2).reshape(g.shape[1:-1] + (4 * g.shape[-1],))


def join_rows(g):
    _, L, r, c = g.shape
    return g.transpose(1, 0, 2, 3).reshape(L, 4 * r, c)


COL_SHARDED = ("ffn1_w_gate", "ffn1_w_up", "w_in", "gdn_conv_w", "conf_dw_w", "mem_w_kv", "ffn2_w_gate", "ffn2_w_up")
CONV_WEIGHTS = ("gdn_conv_w", "conf_dw_w")
ROW_SHARDED = ("ffn1_w_down", "w_out", "mem_w_q", "mem_w_o", "ffn2_w_down")
REPLICATED = ("ln_ffn1_g", "ln_ffn1_b", "gdn_a_log", "gdn_dt_bias", "gdn_norm_g", "fox_b_f", "conf_dw_b", "conf_norm_g",
              "conf_norm_b", "ln_mix_g", "ln_mix_b", "ln_mem_g", "ln_mem_b", "ln_ffn2_g", "ln_ffn2_b")
WEIGHTS = ("ffn1_w_gate", "ffn1_w_up", "ffn1_w_down", "ln_ffn1_g", "ln_ffn1_b", "w_in", "gdn_conv_w", "gdn_a_log",
           "gdn_dt_bias", "gdn_norm_g", "fox_b_f", "conf_dw_w", "conf_dw_b", "conf_norm_g", "conf_norm_b", "w_out",
           "ln_mix_g", "ln_mix_b", "mem_w_q", "mem_w_kv", "mem_w_o", "ln_mem_g", "ln_mem_b", "ffn2_w_gate",
           "ffn2_w_up", "ffn2_w_down", "ln_ffn2_g", "ln_ffn2_b")


def pack_small(d):
    flat = jnp.concatenate([d[n].reshape(-1) for n in REPLICATED])
    rows = -(-flat.shape[0] // 1024) * 8
    return jnp.pad(flat, (0, rows * 128 - flat.shape[0])).reshape(rows, 128)


def unpack_small(p, like):
    flat = p.reshape(-1)
    out, o = {}, 0
    for n in REPLICATED:
        sz = like[n].size
        out[n] = flat[o:o + sz].reshape(like[n].shape)
        o += sz
    return out


def _vec(v):
    return v.reshape(1, -1)


def _pad_rows(w, rows):
    return jnp.pad(w, ((0, rows - w.shape[0]), (0, 0)))


def _small_lane_vec(v4, lane0):
    return jnp.pad(v4.reshape(1, -1), ((0, 0), (lane0, 128 - lane0 - v4.shape[0])))


def layer_fwd(x0, mem, W, li, rider=None):
    T = x0.shape[0]
    tq = min(T, ATT_TQ)
    nq = T // tq
    n = lambda s: f"l{li}_{s}"
    R = {"x0": x0}
    R["z1"], x1 = ffn_fwd(x0, W["ffn1_w_gate"], W["ffn1_w_up"], W["ffn1_w_down"], _vec(W["ln_ffn1_g"]), _vec(W["ln_ffn1_b"]), n("ffn1_fwd"))
    R["x1"] = x1
    proj, pbf = mm_nn(x1, W["w_in_r"], n("proj"), also_bf16=True)
    R["proj"], R["pbf"] = proj, pbf

    w8 = _pad_rows(W["gdn_conv_w"], 8)
    qkv_s = gdn_conv_fwd(proj, w8, n("gdn_conv_fwd"))
    qkv_hm = to_hm(qkv_s, 12)
    z_hm = to_hm(proj[:, C_GZ:C_GZ + GW], 4)
    ab_hm = proj[:, C_SMALL:C_SMALL + 8].T.reshape(8, T, 1)
    alog = W["gdn_a_log"].reshape(NH, 1, 1)
    dtb = W["gdn_dt_bias"].reshape(NH, 1, 1)
    ng = W["gdn_norm_g"].reshape(1, 1, HD)
    ya_hm, states = gdn_fwd(qkv_hm, z_hm, ab_hm, alog, dtb, ng, n("gdn_fwd"))
    R.update(qkv_hm=qkv_hm, z_hm=z_hm, ab_hm=ab_hm, states=states)

    bfv = _small_lane_vec(W["fox_b_f"], F_LANE)
    cum = fox_gate_fwd(proj, bfv, n("fox_gate_fwd"))
    cum4 = cum[:, F_LANE:F_LANE + NH]
    cumrow = jnp.pad(cum4.T.reshape(2, 2, nq, tq).transpose(0, 2, 1, 3), ((0, 0), (0, 0), (0, 6), (0, 0)))
    yb, lse = fox_fwd(pbf, cumrow, n("fox_fwd"))
    R.update(cumrow=cumrow, yb=yb, lse=lse)

    w32 = _pad_rows(W["conf_dw_w"], 32)
    yc, cc = conf_fwd(proj, w32, _vec(W["conf_dw_b"]), _vec(W["conf_norm_g"]), _vec(W["conf_norm_b"]), n("conf_fwd"))
    R["cc"] = cc

    yd, tot, rider_out = sb_fwd(pbf, n("sb_fwd"), rider)
    R["tot"] = tot

    ycat = jnp.concatenate([from_hm(ya_hm), yb, yc, yd], axis=1).astype(bf16)
    R["ycat"] = ycat
    R["z2"], x2 = lin_res_ln(ycat, W["w_out"], x1, _vec(W["ln_mix_g"]), _vec(W["ln_mix_b"]), n("mix_out"))
    R["x2"] = x2

    qm = mm_nn(x2, W["mem_w_q"], n("mem_q"), out_dtype=bf16)
    kv = mm_nn(mem, W["mem_w_kv"], n("mem_kv"))
    om = mem_fwd(qm, kv, n("mem_fwd"))
    R.update(qm=qm, kv=kv, om=om)
    R["z3"], x3 = lin_res_ln(om, W["mem_w_o"], x2, _vec(W["ln_mem_g"]), _vec(W["ln_mem_b"]), n("mem_out"))
    R["x3"] = x3
    R["z4"], x4 = ffn_fwd(x3, W["ffn2_w_gate"], W["ffn2_w_up"], W["ffn2_w_down"], _vec(W["ln_ffn2_g"]), _vec(W["ln_ffn2_b"]), n("ffn2_fwd"))
    return x4, R, rider_out


def layer_bwd(dx4, mem, W, R, li, rider=None):
    T = dx4.shape[0]
    n = lambda s: f"l{li}_{s}"
    G = {}

    def ffn_back(dy, z, x, pre, tag):
        dz, dg, db = ln_bwd(dy, z, _vec(W[f"ln_{pre}_g"]), n(f"{tag}_ln_bwd"))
        dx, a, dh, du, dzh = ffn_bwd(dz, x, W[f"{pre}_w_gate"], W[f"{pre}_w_up"], W[f"{pre}_w_down"], n(f"{tag}_bwd"))
        G[f"{pre}_w_gate"] = mm_tn(x, dh, n(f"{tag}_dwg"))
        G[f"{pre}_w_up"] = mm_tn(x, du, n(f"{tag}_dwu"))
        G[f"{pre}_w_down"] = mm_tn(a, dzh, n(f"{tag}_dwd"))
        G[f"ln_{pre}_g"], G[f"ln_{pre}_b"] = dg.reshape(-1), db.reshape(-1)
        return dx

    dx3 = ffn_back(dx4, R["z4"], R["x3"], "ffn2", "ffn2")

    dz3, dg, db = ln_bwd(dx3, R["z3"], _vec(W["ln_mem_g"]), n("mem_ln_bwd"))
    G["ln_mem_g"], G["ln_mem_b"] = dg.reshape(-1), db.reshape(-1)
    dom = mm_nt(dz3, W["mem_w_o"], n("mem_dom"))
    G["mem_w_o"] = mm_tn(R["om"], dz3, n("mem_dwo"))
    dqm, dkv = mem_bwd(dom, R["qm"], R["kv"], n("mem_bwd"))
    G["mem_w_q"] = mm_tn(R["x2"], dqm, n("mem_dwq"))
    G["mem_w_kv"] = mm_tn(mem, dkv, n("mem_dwkv"))
    dx2 = mm_nt(dqm, W["mem_w_q"], n("mem_dx"), add=dz3, add_scale=ALPHA)

    dz2, dg, db = ln_bwd(dx2, R["z2"], _vec(W["ln_mix_g"]), n("mix_ln_bwd"))
    G["ln_mix_g"], G["ln_mix_b"] = dg.reshape(-1), db.reshape(-1)
    dycat = mm_nt(dz2, W["w_out"], n("mix_dycat"))
    G["w_out"] = mm_tn(R["ycat"], dz2, n("mix_dwout"))
    dya, dyb, dyc, dyd = (dycat[:, i * GW:(i + 1) * GW] for i in range(4))
    proj, pbf = R["proj"], R["pbf"]

    alog = W["gdn_a_log"].reshape(NH, 1, 1)
    dtb = W["gdn_dt_bias"].reshape(NH, 1, 1)
    ng = W["gdn_norm_g"].reshape(1, 1, HD)
    dqkv_hm, dz_hm, dab_hm, dalog, ddtb, dng = gdn_bwd(to_hm(dya, 4), R["states"], R["qkv_hm"], R["z_hm"], R["ab_hm"],
                                                      alog, dtb, ng, n("gdn_bwd"))
    G["gdn_a_log"], G["gdn_dt_bias"], G["gdn_norm_g"] = dalog.reshape(-1), ddtb.reshape(-1), dng.reshape(-1)
    w8 = _pad_rows(W["gdn_conv_w"], 8)
    dgqkv, dw8 = gdn_conv_bwd(from_hm(dqkv_hm), proj, w8, n("gdn_conv_bwd"))
    G["gdn_conv_w"] = dw8[:GDN_K]

    dfq, dfk, dfv, dcumrow, dcumq = fox_bwd(dyb, R["yb"], R["lse"], pbf, R["cumrow"], n("fox_bwd"))
    dcum4 = dcumrow[:, :, 0:2, :].transpose(0, 2, 1, 3).reshape(4, T).T
    dcum4 = dcum4 + dcumq[:, :, 0:2].transpose(1, 0, 2).reshape(T, 4)
    dcum = jnp.pad(dcum4, ((0, 0), (F_LANE, 128 - F_LANE - NH)))
    bfv = _small_lane_vec(W["fox_b_f"], F_LANE)
    dsmall_f, dbf = fox_gate_bwd(dcum, proj, bfv, n("fox_gate_bwd"))
    G["fox_b_f"] = dbf[0, F_LANE:F_LANE + NH]

    w32 = _pad_rows(W["conf_dw_w"], 32)
    dglu, dw32, dcb, dcg, dcbeta = conf_bwd(dyc, R["cc"], proj, w32, _vec(W["conf_norm_g"]), _vec(W["conf_norm_b"]), n("conf_bwd"))
    G["conf_dw_w"], G["conf_dw_b"] = dw32[:CONF_K], dcb.reshape(-1)
    G["conf_norm_g"], G["conf_norm_b"] = dcg.reshape(-1), dcbeta.reshape(-1)

    dsq, dsk, dsv, rider_out = sb_bwd(dyd, R["tot"], pbf, n("sb_bwd"), rider)

    dsmall = jnp.concatenate([dab_hm.reshape(8, T).T, dsmall_f[:, F_LANE:F_LANE + NH], jnp.zeros((T, PW - C_SMALL - 12), f32)], axis=1)
    dproj = jnp.concatenate([dgqkv, from_hm(dz_hm), dfq, dfk, dfv, dglu, dsq, dsk, dsv, dsmall], axis=1).astype(bf16)
    G["w_in_r"] = mm_tn(R["x1"], dproj, n("proj_dw"))
    dx1 = mm_nt(dproj, W["w_in_r"], n("proj_dx"), add=dz2, add_scale=ALPHA)

    dx0 = ffn_back(dx1, R["z1"], R["x0"], "ffn1", "ffn1")
    return dx0, G, rider_out


def _step(P, M, V, x, mem, loss_target):
    xs, mems, tgt = x[0], mem[0], loss_target[0]

    big = COL_SHARDED + ROW_SHARDED
    split = [k not in CONV_WEIGHTS for k in big]

    def weight_gather(li):
        return gather_rider([P[k][li:li + 1] if k in CONV_WEIGHTS else P[k][li:li + 1].astype(bf16) for k in big], split)

    def layer_weights(li, gathered):
        W = {k: (join_cols(g) if k in COL_SHARDED else join_rows(g))[0] for k, g in zip(big, gathered)}
        W.update({k: P[k][li] for k in REPLICATED})
        W["w_in_r"] = regroup_w_in(W["w_in"])
        return W

    layers = [layer_weights(0, run_rider(weight_gather(0), "gather_weights_l0"))]
    h = xs
    saved = []
    for li in range(DEPTH):
        h, R, gathered = layer_fwd(h, mems, layers[li], li, weight_gather(li + 1) if li + 1 < DEPTH else None)
        saved.append(R)
        if li + 1 < DEPTH:
            layers.append(layer_weights(li + 1, gathered))
    loss_row, dy = loss_and_grad(h, tgt, "loss")
    loss = lax.psum(loss_row[0, 0], ("x", "y", "c"))

    def grad_slots(G):
        out = []
        for k in big:
            s = col_shards(G[k][None]) if k in COL_SHARDED else row_shards(G[k][None])
            out.append(s if k in CONV_WEIGHTS else s.astype(bf16))
        return out

    grads, received_by_layer = [None] * DEPTH, [None] * DEPTH
    rider = None
    for li in reversed(range(DEPTH)):
        dy, G, got = layer_bwd(dy, mems, layers[li], saved[li], li, rider)
        if rider is not None:
            received_by_layer[li + 1] = got
        G["w_in"] = ungroup_w_in(G.pop("w_in_r"))
        grads[li] = G
        rider = exchange_rider(grad_slots(G))
    received_by_layer[0] = run_rider(rider, "exchange_grads_l0")
    grad_x = dy[None]

    stacked = {k: jnp.stack([grads[li][k] for li in range(DEPTH)]) for k in WEIGHTS}
    received = [jnp.concatenate(per_layer, axis=1) for per_layer in zip(*received_by_layer)]
    partial_sums = []
    for k, r in zip(big, received):
        shp = r.shape
        partial_sums.append(sum_slots(r.reshape(4, -1, shp[-1]), f"sum_{k}"))
    from_sibling = swap_sibling(partial_sums, "swap_partials")

    out_g, out_d, out_m, out_v = {}, {}, {}, {}
    for k, mine, theirs in zip(big, partial_sums, from_sibling):
        shp = P[k].shape
        flat = lambda t: t.reshape(-1, shp[-1])
        g, d, mn, vn = adamw(flat(P[k]), mine, theirs, flat(M[k]), flat(V[k]), f"adamw_{k}")
        out_g[k], out_d[k], out_m[k], out_v[k] = (t.reshape(shp) for t in (g, d, mn, vn))

    gsmall = sum_slots(gather_all(pack_small(stacked), "gather_small"), "sum_small")
    g, d, mn, vn = adamw(pack_small(P), gsmall, None, pack_small(M), pack_small(V), "adamw_small")
    for dst, packed in ((out_g, g), (out_d, d), (out_m, mn), (out_v, vn)):
        dst.update(unpack_small(packed, P))

    return (loss, grad_x, *[out_g[k] for k in WEIGHTS], *[out_d[k] for k in WEIGHTS],
            *[out_m[k] for k in WEIGHTS], *[out_v[k] for k in WEIGHTS])


def kernel(x, mem, ffn1_w_gate, ffn1_w_up, ffn1_w_down, ln_ffn1_g, ln_ffn1_b, w_in, gdn_conv_w, gdn_a_log, gdn_dt_bias, gdn_norm_g, fox_b_f, conf_dw_w, conf_dw_b, conf_norm_g, conf_norm_b, w_out, ln_mix_g, ln_mix_b, mem_w_q, mem_w_kv, mem_w_o, ln_mem_g, ln_mem_b, ffn2_w_gate, ffn2_w_up, ffn2_w_down, ln_ffn2_g, ln_ffn2_b, loss_target, m_ffn1_w_gate, m_ffn1_w_up, m_ffn1_w_down, m_ln_ffn1_g, m_ln_ffn1_b, m_w_in, m_gdn_conv_w, m_gdn_a_log, m_gdn_dt_bias, m_gdn_norm_g, m_fox_b_f, m_conf_dw_w, m_conf_dw_b, m_conf_norm_g, m_conf_norm_b, m_w_out, m_ln_mix_g, m_ln_mix_b, m_mem_w_q, m_mem_w_kv, m_mem_w_o, m_ln_mem_g, m_ln_mem_b, m_ffn2_w_gate, m_ffn2_w_up, m_ffn2_w_down, m_ln_ffn2_g, m_ln_ffn2_b, v_ffn1_w_gate, v_ffn1_w_up, v_ffn1_w_down, v_ln_ffn1_g, v_ln_ffn1_b, v_w_in, v_gdn_conv_w, v_gdn_a_log, v_gdn_dt_bias, v_gdn_norm_g, v_fox_b_f, v_conf_dw_w, v_conf_dw_b, v_conf_norm_g, v_conf_norm_b, v_w_out, v_ln_mix_g, v_ln_mix_b, v_mem_w_q, v_mem_w_kv, v_mem_w_o, v_ln_mem_g, v_ln_mem_b, v_ffn2_w_gate, v_ffn2_w_up, v_ffn2_w_down, v_ln_ffn2_g, v_ln_ffn2_b):
    a = locals()
    P = {k: a[k] for k in WEIGHTS}
    M = {k: a["m_" + k] for k in WEIGHTS}
    V = {k: a["v_" + k] for k in WEIGHTS}
    return _step(P, M, V, x, mem, loss_target)
```

```python
import functools

import jax
import jax.numpy as jnp
from jax import lax
from jax.experimental import pallas as pl
from jax.experimental.pallas import tpu as pltpu

f32 = jnp.float32
bf16 = jnp.bfloat16

D = 1024
F = 2816
GW = 256
HD = 64
NH = 4
CHUNK = 64
CONF_K = 31
GDN_K = 4
DEPTH = 2
ALPHA = float((2 * DEPTH) ** 0.25)
LN_EPS = 1e-5
RMS_EPS = 1e-6
L2_EPS = 1e-6
NEG = -1e30
PW = 3200
C_GQKV, C_GZ, C_FOX, C_CONF, C_SB, C_SMALL = 0, 768, 1024, 1792, 2304, 3072
ADAM_LR, ADAM_B1, ADAM_B2, ADAM_EPS, ADAM_WD, ADAM_STEP = 0.001, 0.9, 0.999, 1e-08, 0.01, 10
VMEM_LIMIT = 56 * 1024 * 1024
MESH = pl.DeviceIdType.MESH


def _cparams(sem):
    return pltpu.CompilerParams(dimension_semantics=sem, vmem_limit_bytes=VMEM_LIMIT)


def _pick(n, cands):
    for c in cands:
        if n % c == 0:
            return c
    return n


def _sds(shape, dtype=f32):
    return jax.ShapeDtypeStruct(shape, dtype)


def _layer_norm(z, g, b):
    mu = jnp.mean(z, axis=-1, keepdims=True)
    zc = z - mu
    var = jnp.mean(zc * zc, axis=-1, keepdims=True)
    return zc * lax.rsqrt(var + LN_EPS) * g + b


def _softplus(x):
    return jnp.maximum(x, 0.0) + jnp.log(1.0 + jnp.exp(-jnp.abs(x)))


def _neg_softplus(z):
    nz = -z
    return jnp.minimum(nz, 0.0) - jnp.log(1.0 + jnp.exp(jnp.minimum(z, nz)))


def _dsilu(x):
    s = jax.nn.sigmoid(x)
    return s * (1.0 + x * (1.0 - s))


def _split_hi_lo(x):
    hi = x.astype(bf16)
    lo = (x - hi.astype(f32)).astype(bf16)
    return hi, lo


def _dot(a, b):
    return jnp.dot(a, b, preferred_element_type=f32)


def _dot_nt(a, b):
    return lax.dot_general(a, b, (((1,), (1,)), ((), ())), preferred_element_type=f32)


def _dot_tn(a, b):
    return lax.dot_general(a, b, (((0,), (0,)), ((), ())), preferred_element_type=f32)


def mm_nn(a, w, name, out_dtype=f32, also_bf16=False):
    T, K = a.shape
    N = w.shape[1]
    tm = min(T, 512)
    tn = N if N <= 1024 else _pick(N, (640, 512))

    def body(a_ref, w_ref, *o_refs):
        r = _dot(a_ref[...].astype(bf16), w_ref[...].astype(bf16))
        o_refs[0][...] = r.astype(o_refs[0].dtype)
        if also_bf16:
            o_refs[1][...] = r.astype(bf16)

    out_shape = [_sds((T, N), out_dtype)]
    out_specs = [pl.BlockSpec((tm, tn), lambda i, j: (i, j))]
    if also_bf16:
        out_shape.append(_sds((T, N), bf16))
        out_specs.append(pl.BlockSpec((tm, tn), lambda i, j: (i, j)))
    res = pl.pallas_call(
        body, grid=(T // tm, N // tn),
        in_specs=[pl.BlockSpec((tm, K), lambda i, j: (i, 0)), pl.BlockSpec((K, tn), lambda i, j: (0, j))],
        out_specs=out_specs, out_shape=out_shape,
        compiler_params=_cparams(("parallel", "arbitrary")), name=name)(a, w)
    return res if also_bf16 else res[0]


def mm_nt(g, w, name, add=None, add_scale=1.0):
    T, N = g.shape
    K = w.shape[0]
    tm = min(T, 512)

    def body(*refs):
        if add is None:
            g_ref, w_ref, o_ref = refs
        else:
            g_ref, w_ref, add_ref, o_ref = refs
        r = _dot_nt(g_ref[...].astype(bf16), w_ref[...].astype(bf16))
        if add is not None:
            r = r + add_scale * add_ref[...]
        o_ref[...] = r

    in_specs = [pl.BlockSpec((tm, N), lambda i: (i, 0)), pl.BlockSpec((K, N), lambda i: (0, 0))]
    args = [g, w]
    if add is not None:
        in_specs.append(pl.BlockSpec((tm, K), lambda i: (i, 0)))
        args.append(add)
    return pl.pallas_call(
        body, grid=(T // tm,), in_specs=in_specs,
        out_specs=pl.BlockSpec((tm, K), lambda i: (i, 0)), out_shape=_sds((T, K)),
        compiler_params=_cparams(("parallel",)), name=name)(*args)


def mm_tn(a, g, name):
    T, K = a.shape
    N = g.shape[1]
    tt = min(T, 512)
    tk = K if K * N * 4 <= 14 * 1024 * 1024 else _pick(K, (512, 1408))

    def body(a_ref, g_ref, o_ref):
        @pl.when(pl.program_id(1) == 0)
        def _():
            o_ref[...] = jnp.zeros_like(o_ref)
        o_ref[...] += _dot_tn(a_ref[...].astype(bf16), g_ref[...].astype(bf16))

    return pl.pallas_call(
        body, grid=(K // tk, T // tt),
        in_specs=[pl.BlockSpec((tt, tk), lambda j, t: (t, j)), pl.BlockSpec((tt, N), lambda j, t: (t, 0))],
        out_specs=pl.BlockSpec((tk, N), lambda j, t: (j, 0)), out_shape=_sds((K, N)),
        compiler_params=_cparams(("parallel", "arbitrary")), name=name)(a, g)


FFN_TF = 1408


def ffn_fwd(x, wg, wu, wd, g, b, name, rider=None):
    T = x.shape[0]
    tm = min(T, 512)
    nf = F // FFN_TF

    def body(x_ref, wg_ref, wu_ref, wd_ref, g_ref, b_ref, z_ref, y_ref, acc):
        j = pl.program_id(1)

        @pl.when(j == 0)
        def _():
            acc[...] = jnp.zeros_like(acc)

        xb = x_ref[...].astype(bf16)
        h = _dot(xb, wg_ref[...])
        u = _dot(xb, wu_ref[...])
        a = (h * jax.nn.sigmoid(h) * u).astype(bf16)
        acc[...] += _dot(a, wd_ref[...])

        @pl.when(j == nf - 1)
        def _():
            z = ALPHA * x_ref[...] + 0.5 * acc[...]
            z_ref[...] = z
            y_ref[...] = _layer_norm(z, g_ref[...], b_ref[...])

    row = pl.BlockSpec((tm, D), lambda i, j: (i, 0))
    vec = pl.BlockSpec((1, D), lambda i, j: (0, 0))
    body, r_in, r_args, r_out, r_shape, r_scratch = _ride(body, 6, 2, rider, (T // tm, nf), 1)
    res = pl.pallas_call(
        body, grid=(T // tm, nf),
        in_specs=[row, pl.BlockSpec((D, FFN_TF), lambda i, j: (0, j)), pl.BlockSpec((D, FFN_TF), lambda i, j: (0, j)),
                  pl.BlockSpec((FFN_TF, D), lambda i, j: (j, 0)), vec, vec] + r_in,
        out_specs=[row, row] + r_out, out_shape=[_sds((T, D)), _sds((T, D))] + r_shape,
        scratch_shapes=[pltpu.VMEM((tm, D), f32)] + r_scratch,
        compiler_params=_cparams(("arbitrary", "arbitrary")), name=name)(x, wg, wu, wd, g, b, *r_args)
    return res[0], res[1], res[2:]


def ffn_bwd(dz, x, wg, wu, wd, name, rider=None):
    T = x.shape[0]
    tm = min(T, 512)
    nf = F // FFN_TF

    def body(dz_ref, x_ref, wg_ref, wu_ref, wd_ref, dx_ref, a_ref, dh_ref, du_ref, dzh_ref, acc):
        j = pl.program_id(1)

        @pl.when(j == 0)
        def _():
            acc[...] = jnp.zeros_like(acc)

        dzh = (0.5 * dz_ref[...]).astype(bf16)
        xb = x_ref[...].astype(bf16)
        h = _dot(xb, wg_ref[...])
        u = _dot(xb, wu_ref[...])
        s = jax.nn.sigmoid(h)
        hs = h * s
        da = _dot_nt(dzh, wd_ref[...])
        du = (da * hs).astype(bf16)
        dh = (da * u * (s + hs * (1.0 - s))).astype(bf16)
        a_ref[...] = (hs * u).astype(bf16)
        dh_ref[...] = dh
        du_ref[...] = du
        acc[...] += _dot_nt(dh, wg_ref[...]) + _dot_nt(du, wu_ref[...])

        @pl.when(j == nf - 1)
        def _():
            dx_ref[...] = ALPHA * dz_ref[...] + acc[...]
            dzh_ref[...] = dzh

    row = pl.BlockSpec((tm, D), lambda i, j: (i, 0))
    wide = pl.BlockSpec((tm, FFN_TF), lambda i, j: (i, j))
    body, r_in, r_args, r_out, r_shape, r_scratch = _ride(body, 5, 5, rider, (T // tm, nf), 1)
    res = pl.pallas_call(
        body, grid=(T // tm, nf),
        in_specs=[row, row, pl.BlockSpec((D, FFN_TF), lambda i, j: (0, j)), pl.BlockSpec((D, FFN_TF), lambda i, j: (0, j)),
                  pl.BlockSpec((FFN_TF, D), lambda i, j: (j, 0))] + r_in,
        out_specs=[row, wide, wide, wide, row] + r_out,
        out_shape=[_sds((T, D)), _sds((T, F), bf16), _sds((T, F), bf16), _sds((T, F), bf16), _sds((T, D), bf16)] + r_shape,
        scratch_shapes=[pltpu.VMEM((tm, D), f32)] + r_scratch,
        compiler_params=_cparams(("arbitrary", "arbitrary")), name=name)(dz, x, wg, wu, wd, *r_args)
    return res[0], res[1], res[2], res[3], res[4], res[5:]


def lin_res_ln(a, w, res, g, b, name):
    T, K = a.shape
    tm = min(T, 512)

    def body(a_ref, w_ref, res_ref, g_ref, b_ref, z_ref, y_ref):
        z = ALPHA * res_ref[...] + _dot(a_ref[...].astype(bf16), w_ref[...])
        z_ref[...] = z
        y_ref[...] = _layer_norm(z, g_ref[...], b_ref[...])

    row = pl.BlockSpec((tm, D), lambda i: (i, 0))
    vec = pl.BlockSpec((1, D), lambda i: (0, 0))
    return pl.pallas_call(
        body, grid=(T // tm,),
        in_specs=[pl.BlockSpec((tm, K), lambda i: (i, 0)), pl.BlockSpec((K, D), lambda i: (0, 0)), row, vec, vec],
        out_specs=[row, row], out_shape=[_sds((T, D)), _sds((T, D))],
        compiler_params=_cparams(("parallel",)), name=name)(a, w, res, g, b)


def ln_bwd(dy, z, g, name):
    T = z.shape[0]
    tm = min(T, 512)

    def body(dy_ref, z_ref, g_ref, dz_ref, dg_ref, db_ref):
        @pl.when(pl.program_id(0) == 0)
        def _():
            dg_ref[...] = jnp.zeros_like(dg_ref)
            db_ref[...] = jnp.zeros_like(db_ref)

        zv = z_ref[...]
        dy = dy_ref[...]
        mu = jnp.mean(zv, axis=-1, keepdims=True)
        zc = zv - mu
        rstd = lax.rsqrt(jnp.mean(zc * zc, axis=-1, keepdims=True) + LN_EPS)
        xh = zc * rstd
        dxh = dy * g_ref[...]
        m1 = jnp.mean(dxh, axis=-1, keepdims=True)
        m2 = jnp.mean(dxh * xh, axis=-1, keepdims=True)
        dz_ref[...] = rstd * (dxh - m1 - xh * m2)
        dg_ref[...] += jnp.sum(dy * xh, axis=0, keepdims=True)
        db_ref[...] += jnp.sum(dy, axis=0, keepdims=True)

    row = pl.BlockSpec((tm, D), lambda i: (i, 0))
    vec = pl.BlockSpec((1, D), lambda i: (0, 0))
    return pl.pallas_call(
        body, grid=(T // tm,), in_specs=[row, row, vec], out_specs=[row, vec, vec],
        out_shape=[_sds((T, D)), _sds((1, D)), _sds((1, D))],
        compiler_params=_cparams(("arbitrary",)), name=name)(dy, z, g)


def loss_and_grad(y, target, name):
    T = y.shape[0]
    tm = min(T, 512)

    def body(y_ref, t_ref, l_ref, dy_ref):
        @pl.when(pl.program_id(0) == 0)
        def _():
            l_ref[...] = jnp.zeros_like(l_ref)
        d = y_ref[...] - t_ref[...]
        dy_ref[...] = d * (1.0 / D)
        l_ref[...] += (0.5 / D) * jnp.sum(jnp.sum(d * d, axis=1, keepdims=True), axis=0, keepdims=True)

    row = pl.BlockSpec((tm, D), lambda i: (i, 0))
    return pl.pallas_call(
        body, grid=(T // tm,), in_specs=[row, row],
        out_specs=[pl.BlockSpec((1, 128), lambda i: (0, 0)), row],
        out_shape=[_sds((1, 128)), _sds((T, D))],
        compiler_params=_cparams(("arbitrary",)), name=name)(y, target)


def _shifted(ext, s):
    return ext if s == 0 else pltpu.roll(ext, s, 0)


def _halo_maps(tm, P, nblk):
    per = tm // P
    prev = lambda i, c: (jnp.maximum(i * per - 1, 0), c)
    nxt = lambda i, c: (jnp.minimum((i + 1) * per, nblk * per - 1), c)
    return prev, nxt


GDN_P = 8
CONF_P = 32


def gdn_conv_fwd(proj, w8, name):
    T = proj.shape[0]
    tm = min(T, 512)
    nblk = T // tm
    C = 3 * GW
    prev, _ = _halo_maps(tm, GDN_P, nblk)

    def body(xc_ref, xp_ref, w_ref, o_ref):
        i = pl.program_id(0)
        xp = jnp.where(i > 0, xp_ref[...], 0.0)
        ext = jnp.concatenate([xp, xc_ref[...]], axis=0)
        acc = jnp.zeros((tm, C), f32)
        for k in range(GDN_K):
            acc = acc + w_ref[k:k + 1, :] * _shifted(ext, GDN_K - 1 - k)[GDN_P:, :]
        o_ref[...] = acc * jax.nn.sigmoid(acc)

    return pl.pallas_call(
        body, grid=(nblk,),
        in_specs=[pl.BlockSpec((tm, C), lambda i: (i, 0)), pl.BlockSpec((GDN_P, C), lambda i: prev(i, 0)),
                  pl.BlockSpec((8, C), lambda i: (0, 0))],
        out_specs=pl.BlockSpec((tm, C), lambda i: (i, 0)), out_shape=_sds((T, C)),
        compiler_params=_cparams(("parallel",)), name=name)(proj, proj, w8)


def gdn_conv_bwd(dy, proj, w8, name):
    T = proj.shape[0]
    tm = min(T, 512)
    nblk = T // tm
    C = 3 * GW
    P = GDN_P
    prev, nxt = _halo_maps(tm, P, nblk)

    def body(dyc_ref, dyn_ref, xc_ref, xp_ref, xn_ref, w_ref, dx_ref, dw_ref):
        i = pl.program_id(0)

        @pl.when(i == 0)
        def _():
            dw_ref[...] = jnp.zeros_like(dw_ref)

        xp = jnp.where(i > 0, xp_ref[...], 0.0)
        last = i == nblk - 1
        xn = jnp.where(last, 0.0, xn_ref[...])
        dyn = jnp.where(last, 0.0, dyn_ref[...])
        ext = jnp.concatenate([xp, xc_ref[...], xn], axis=0)
        sh = [_shifted(ext, GDN_K - 1 - k)[P:, :] for k in range(GDN_K)]
        s = jnp.zeros((tm + P, C), f32)
        for k in range(GDN_K):
            s = s + w_ref[k:k + 1, :] * sh[k]
        ds = jnp.concatenate([dyc_ref[...], dyn], axis=0) * _dsilu(s)
        dx = jnp.zeros((tm, C), f32)
        for k in range(GDN_K):
            d = GDN_K - 1 - k
            moved = ds if d == 0 else pltpu.roll(ds, tm + P - d, 0)
            dx = dx + w_ref[k:k + 1, :] * moved[:tm, :]
            dw_ref[k:k + 1, :] += jnp.sum(ds[:tm, :] * sh[k][:tm, :], axis=0, keepdims=True)
        dx_ref[...] = dx

    col = lambda i: (i, 0)
    return pl.pallas_call(
        body, grid=(nblk,),
        in_specs=[pl.BlockSpec((tm, C), col), pl.BlockSpec((P, C), lambda i: nxt(i, 0)),
                  pl.BlockSpec((tm, C), col), pl.BlockSpec((P, C), lambda i: prev(i, 0)),
                  pl.BlockSpec((P, C), lambda i: nxt(i, 0)), pl.BlockSpec((8, C), lambda i: (0, 0))],
        out_specs=[pl.BlockSpec((tm, C), col), pl.BlockSpec((8, C), lambda i: (0, 0))],
        out_shape=[_sds((T, C)), _sds((8, C))],
        compiler_params=_cparams(("arbitrary",)), name=name)(dy, dy, proj, proj, proj, w8)


def _group_ones():
    r = lax.broadcasted_iota(jnp.int32, (GW, GW), 0) // HD
    c = lax.broadcasted_iota(jnp.int32, (GW, GW), 1) // HD
    return (r == c).astype(bf16)


def _group_mean(x, ones):
    hi, lo = _split_hi_lo(x)
    return (_dot(hi, ones) + _dot(lo, ones)) * (1.0 / HD)


def _conf_norm(c, g, b, ones):
    mu = _group_mean(c, ones)
    cc = c - mu
    rstd = lax.rsqrt(_group_mean(cc * cc, ones) + LN_EPS)
    hn = cc * rstd
    return hn, rstd, hn * g + b


CONF_VAL_BLK = C_CONF // GW
CONF_GATE_BLK = C_CONF // GW + 1


def conf_fwd(proj, w32, bias, ng, nb, name):
    T = proj.shape[0]
    tm = min(T, 512)
    nblk = T // tm
    P = CONF_P
    prev, _ = _halo_maps(tm, P, nblk)

    def body(vc_ref, gc_ref, vp_ref, gp_ref, w_ref, bias_ref, ng_ref, nb_ref, y_ref, c_ref):
        i = pl.program_id(0)
        pc = vc_ref[...] * jax.nn.sigmoid(gc_ref[...])
        pp = jnp.where(i > 0, vp_ref[...] * jax.nn.sigmoid(gp_ref[...]), 0.0)
        ext = jnp.concatenate([pp, pc], axis=0)
        acc = jnp.zeros((tm, GW), f32)
        for k in range(CONF_K):
            acc = acc + w_ref[k:k + 1, :] * _shifted(ext, CONF_K - 1 - k)[P:, :]
        c = acc + bias_ref[...]
        c_ref[...] = c
        _, _, yn = _conf_norm(c, ng_ref[...], nb_ref[...], _group_ones())
        y_ref[...] = yn * jax.nn.sigmoid(yn)

    vec = pl.BlockSpec((1, GW), lambda i: (0, 0))
    return pl.pallas_call(
        body, grid=(nblk,),
        in_specs=[pl.BlockSpec((tm, GW), lambda i: (i, CONF_VAL_BLK)), pl.BlockSpec((tm, GW), lambda i: (i, CONF_GATE_BLK)),
                  pl.BlockSpec((P, GW), lambda i: prev(i, CONF_VAL_BLK)), pl.BlockSpec((P, GW), lambda i: prev(i, CONF_GATE_BLK)),
                  pl.BlockSpec((32, GW), lambda i: (0, 0)), vec, vec, vec],
        out_specs=[pl.BlockSpec((tm, GW), lambda i: (i, 0))] * 2, out_shape=[_sds((T, GW))] * 2,
        compiler_params=_cparams(("parallel",)), name=name)(proj, proj, proj, proj, w32, bias, ng, nb)


def conf_bwd(dy, c, proj, w32, ng, nb, name):
    T = proj.shape[0]
    tm = min(T, 512)
    nblk = T // tm
    P = CONF_P
    prev, nxt = _halo_maps(tm, P, nblk)

    def body(dyc_ref, dyn_ref, cc_ref, cn_ref, vc_ref, gc_ref, vp_ref, gp_ref, w_ref, ng_ref, nb_ref,
             dglu_ref, dw_ref, dbias_ref, dng_ref, dnb_ref):
        i = pl.program_id(0)

        @pl.when(i == 0)
        def _():
            dw_ref[...] = jnp.zeros_like(dw_ref)
            dbias_ref[...] = jnp.zeros_like(dbias_ref)
            dng_ref[...] = jnp.zeros_like(dng_ref)
            dnb_ref[...] = jnp.zeros_like(dnb_ref)

        ones = _group_ones()
        g = ng_ref[...]

        def dc_of(dyv, cv):
            hn, rstd, yn = _conf_norm(cv, g, nb_ref[...], ones)
            dyn_ = dyv * _dsilu(yn)
            dhn = dyn_ * g
            dc = rstd * (dhn - _group_mean(dhn, ones) - hn * _group_mean(dhn * hn, ones))
            return dc, dyn_, hn

        dc_c, dyn_c, hn_c = dc_of(dyc_ref[...], cc_ref[...])
        dc_n, _, _ = dc_of(dyn_ref[...], cn_ref[...])
        dc_n = jnp.where(i == nblk - 1, 0.0, dc_n)
        dng_ref[...] += jnp.sum(dyn_c * hn_c, axis=0, keepdims=True)
        dnb_ref[...] += jnp.sum(dyn_c, axis=0, keepdims=True)
        dbias_ref[...] += jnp.sum(dc_c, axis=0, keepdims=True)

        sig_c = jax.nn.sigmoid(gc_ref[...])
        val_c = vc_ref[...]
        pc = val_c * sig_c
        pp = jnp.where(i > 0, vp_ref[...] * jax.nn.sigmoid(gp_ref[...]), 0.0)
        ext = jnp.concatenate([pp, pc], axis=0)
        dext = jnp.concatenate([dc_c, dc_n], axis=0)
        dp = jnp.zeros((tm, GW), f32)
        for k in range(CONF_K):
            d = CONF_K - 1 - k
            moved = dext if d == 0 else pltpu.roll(dext, tm + P - d, 0)
            dp = dp + w_ref[k:k + 1, :] * moved[:tm, :]
            dw_ref[k:k + 1, :] += jnp.sum(dc_c * _shifted(ext, d)[P:, :], axis=0, keepdims=True)
        dglu_ref[:, 0:GW] = dp * sig_c
        dglu_ref[:, GW:2 * GW] = dp * val_c * sig_c * (1.0 - sig_c)

    vec = pl.BlockSpec((1, GW), lambda i: (0, 0))
    blk = pl.BlockSpec((tm, GW), lambda i: (i, 0))
    return pl.pallas_call(
        body, grid=(nblk,),
        in_specs=[blk, pl.BlockSpec((P, GW), lambda i: nxt(i, 0)), blk, pl.BlockSpec((P, GW), lambda i: nxt(i, 0)),
                  pl.BlockSpec((tm, GW), lambda i: (i, CONF_VAL_BLK)), pl.BlockSpec((tm, GW), lambda i: (i, CONF_GATE_BLK)),
                  pl.BlockSpec((P, GW), lambda i: prev(i, CONF_VAL_BLK)), pl.BlockSpec((P, GW), lambda i: prev(i, CONF_GATE_BLK)),
                  pl.BlockSpec((32, GW), lambda i: (0, 0)), vec, vec],
        out_specs=[pl.BlockSpec((tm, 2 * GW), lambda i: (i, 0)), pl.BlockSpec((32, GW), lambda i: (0, 0)), vec, vec, vec],
        out_shape=[_sds((T, 2 * GW)), _sds((32, GW)), _sds((1, GW)), _sds((1, GW)), _sds((1, GW))],
        compiler_params=_cparams(("arbitrary",)), name=name)(dy, dy, c, c, proj, proj, proj, proj, w32, ng, nb)


def _mm_raw(a, b, ta, tb):
    ca = a.ndim - 2 if ta else a.ndim - 1
    cb = b.ndim - 1 if tb else b.ndim - 2
    batch = ((0,), (0,)) if a.ndim == 3 else ((), ())
    return lax.dot_general(a, b, (((ca,), (cb,)), batch), preferred_element_type=f32)


def _mm_prec(a, b, ta, tb, prec):
    if prec == 1:
        return _mm_raw(a.astype(bf16), b.astype(bf16), ta, tb)
    bh, bl = _split_hi_lo(b)
    if prec == 2:
        ab = a.astype(bf16)
        return _mm_raw(ab, bh, ta, tb) + _mm_raw(ab, bl, ta, tb)
    ah, al = _split_hi_lo(a)
    return _mm_raw(ah, bh, ta, tb) + (_mm_raw(ah, bl, ta, tb) + _mm_raw(al, bh, ta, tb))


@functools.partial(jax.custom_vjp, nondiff_argnums=(2, 3, 4))
def mm(a, b, ta=False, tb=False, prec=1):
    return _mm_prec(a, b, ta, tb, prec)


def _mm_fwd(a, b, ta, tb, prec):
    return _mm_prec(a, b, ta, tb, prec), (a, b)


def _mm_bwd(ta, tb, prec, res, ct):
    a, b = res
    da = _mm_prec(b, ct, tb, True, 1) if ta else _mm_prec(ct, b, False, not tb, 1)
    db = _mm_prec(ct, a, True, ta, 1) if tb else _mm_prec(a, ct, not ta, False, 2 if prec == 2 else 1)
    return da, db


mm.defvjp(_mm_fwd, _mm_bwd)


def _tri_inv_raw(l):
    n = -l
    rr = lax.broadcasted_iota(jnp.int32, l.shape, 1)
    cc = lax.broadcasted_iota(jnp.int32, l.shape, 2)
    p = jnp.where(rr == cc, 1.0, 0.0) + n
    for _ in range(5):
        n = _mm_prec(n, n, False, False, 1)
        p = p + _mm_prec(p, n, False, False, 1)
    return p


@jax.custom_vjp
def tri_inv(l):
    return _tri_inv_raw(l)


def _tri_inv_fwd(l):
    t = _tri_inv_raw(l)
    return t, t


def _tri_inv_bwd(t, ct):
    return (-_mm_prec(_mm_prec(t, ct, True, False, 1), t, False, True, 1),)


tri_inv.defvjp(_tri_inv_fwd, _tri_inv_bwd)


def _gdn_block(S, qs, ks, vs, a, b, z, alog, dtb, ng):
    shp = (NH, CHUNK, CHUNK)
    ii = lax.broadcasted_iota(jnp.int32, shp, 1)
    jj = lax.broadcasted_iota(jnp.int32, shp, 2)
    l_incl = jnp.where(ii >= jj, 1.0, 0.0)
    ys = []
    for c in range(len(qs)):
        q = qs[c] * lax.rsqrt(jnp.sum(qs[c] * qs[c], axis=-1, keepdims=True) + L2_EPS) * (HD ** -0.5)
        k = ks[c] * lax.rsqrt(jnp.sum(ks[c] * ks[c], axis=-1, keepdims=True) + L2_EPS)
        v = vs[c]
        beta = jax.nn.sigmoid(b[c])
        g = -jnp.exp(alog) * _softplus(a[c] + dtb)
        gcb = mm(l_incl, jnp.broadcast_to(g, shp), False, False, 2)
        gcr = jnp.swapaxes(gcb, 1, 2)
        decay = jnp.exp(jnp.where(ii >= jj, gcb - gcr, NEG))
        g_last = jnp.sum(jnp.where(ii == CHUNK - 1, gcb, 0.0), axis=1, keepdims=True)
        eg = jnp.exp(gcb)
        kb = k * beta
        lkk = jnp.where(ii > jj, mm(kb, k, False, True) * decay, 0.0)
        t_inv = tri_inv(lkk)
        u = mm(t_inv, v * beta)
        w = mm(t_inv, kb * eg)
        a_qk = jnp.where(ii >= jj, mm(q, k, False, True) * decay, 0.0)
        q_dec = q * eg
        k_dec = k * jnp.exp(g_last - gcb)
        v_new = u - mm(w, S)
        o = mm(q_dec, S) + mm(a_qk, v_new)
        S = S * jnp.exp(g_last) + mm(k_dec, v_new, True, False)
        y = o * lax.rsqrt(jnp.mean(o * o, axis=-1, keepdims=True) + RMS_EPS) * ng
        ys.append(y * (z[c] * jax.nn.sigmoid(z[c])))
    return S, ys


GDN_CB = 256


def _gdn_load(refs, nc):
    q_ref, k_ref, v_ref, z_ref, a_ref, b_ref = refs
    sl = lambda r, c: r[:, c * CHUNK:(c + 1) * CHUNK, :]
    return tuple([sl(r, c) for c in range(nc)] for r in (q_ref, k_ref, v_ref, z_ref, a_ref, b_ref))


def gdn_fwd(qkv_hm, z_hm, ab_hm, alog, dtb, ng, name):
    T = z_hm.shape[1]
    cb = min(T, GDN_CB)
    nc = cb // CHUNK
    nb = T // cb

    def body(q_ref, k_ref, v_ref, z_ref, a_ref, b_ref, alog_ref, dtb_ref, ng_ref, y_ref, s_ref, S):
        @pl.when(pl.program_id(0) == 0)
        def _():
            S[...] = jnp.zeros_like(S)
        s_ref[...] = S[...]
        qs, ks, vs, zs, as_, bs = _gdn_load((q_ref, k_ref, v_ref, z_ref, a_ref, b_ref), nc)
        s_out, ys = _gdn_block(S[...], qs, ks, vs, as_, bs, zs, alog_ref[...], dtb_ref[...], ng_ref[...])
        S[...] = s_out
        for c in range(nc):
            y_ref[:, c * CHUNK:(c + 1) * CHUNK, :] = ys[c]

    hm = lambda h0: pl.BlockSpec((NH, cb, HD), lambda i: (h0, i, 0))
    col = lambda h0: pl.BlockSpec((NH, cb, 1), lambda i: (h0, i, 0))
    par = pl.BlockSpec((NH, 1, 1), lambda i: (0, 0, 0))
    return pl.pallas_call(
        body, grid=(nb,),
        in_specs=[hm(0), hm(1), hm(2), hm(0), col(0), col(1), par, par, pl.BlockSpec((1, 1, HD), lambda i: (0, 0, 0))],
        out_specs=[hm(0), pl.BlockSpec((None, NH, HD, HD), lambda i: (i, 0, 0, 0))],
        out_shape=[_sds((NH, T, HD)), _sds((nb, NH, HD, HD))],
        scratch_shapes=[pltpu.VMEM((NH, HD, HD), f32)],
        compiler_params=_cparams(("arbitrary",)), name=name)(qkv_hm, qkv_hm, qkv_hm, z_hm, ab_hm, ab_hm, alog, dtb, ng)


def gdn_bwd(dy_hm, states, qkv_hm, z_hm, ab_hm, alog, dtb, ng, name):
    T = z_hm.shape[1]
    cb = min(T, GDN_CB)
    nc = cb // CHUNK
    nb = T // cb

    def body(dy_ref, s_ref, q_ref, k_ref, v_ref, z_ref, a_ref, b_ref, alog_ref, dtb_ref, ng_ref,
             dq_ref, dk_ref, dv_ref, dz_ref, da_ref, db_ref, dalog_ref, ddtb_ref, dng_ref, dS):
        @pl.when(pl.program_id(0) == 0)
        def _():
            dS[...] = jnp.zeros_like(dS)
            dalog_ref[...] = jnp.zeros_like(dalog_ref)
            ddtb_ref[...] = jnp.zeros_like(ddtb_ref)
            dng_ref[...] = jnp.zeros_like(dng_ref)

        qs, ks, vs, zs, as_, bs = _gdn_load((q_ref, k_ref, v_ref, z_ref, a_ref, b_ref), nc)
        _, vjp = jax.vjp(_gdn_block, s_ref[...], qs, ks, vs, as_, bs, zs, alog_ref[...], dtb_ref[...], ng_ref[...])
        dys = [dy_ref[:, c * CHUNK:(c + 1) * CHUNK, :] for c in range(nc)]
        d_s, dqs, dks, dvs, das, dbs, dzs, d_alog, d_dtb, d_ng = vjp((dS[...], dys))
        dS[...] = d_s
        dalog_ref[...] += d_alog
        ddtb_ref[...] += d_dtb
        dng_ref[...] += d_ng
        for c in range(nc):
            sl = slice(c * CHUNK, (c + 1) * CHUNK)
            dq_ref[:, sl, :] = dqs[c]
            dk_ref[:, sl, :] = dks[c]
            dv_ref[:, sl, :] = dvs[c]
            dz_ref[:, sl, :] = dzs[c]
            da_ref[:, sl, :] = das[c]
            db_ref[:, sl, :] = dbs[c]

    rev = lambda i: nb - 1 - i
    hm = lambda h0: pl.BlockSpec((NH, cb, HD), lambda i: (h0, rev(i), 0))
    col = lambda h0: pl.BlockSpec((NH, cb, 1), lambda i: (h0, rev(i), 0))
    par = pl.BlockSpec((NH, 1, 1), lambda i: (0, 0, 0))
    ngs = pl.BlockSpec((1, 1, HD), lambda i: (0, 0, 0))
    res = pl.pallas_call(
        body, grid=(nb,),
        in_specs=[hm(0), pl.BlockSpec((None, NH, HD, HD), lambda i: (rev(i), 0, 0, 0)),
                  hm(0), hm(1), hm(2), hm(0), col(0), col(1), par, par, ngs],
        out_specs=[hm(0), hm(0), hm(0), hm(0), col(0), col(0), par, par, ngs],
        out_shape=[_sds((NH, T, HD))] * 4 + [_sds((NH, T, 1))] * 2 + [_sds((NH, 1, 1))] * 2 + [_sds((1, 1, HD))],
        scratch_shapes=[pltpu.VMEM((NH, HD, HD), f32)],
        compiler_params=_cparams(("arbitrary",)), name=name)(dy_hm, states, qkv_hm, qkv_hm, qkv_hm, z_hm, ab_hm, ab_hm, alog, dtb, ng)
    dq, dk, dv, dz, da, db, dalog, ddtb, dng = res
    return jnp.concatenate([dq, dk, dv], axis=0), dz, jnp.concatenate([da, db], axis=0), dalog, ddtb, dng


F_LANE = 8
SCAN_TB = 256


def fox_gate_fwd(proj, bfv, name):
    T = proj.shape[0]
    tb = min(T, SCAN_TB)

    def body(x_ref, b_ref, o_ref, carry):
        @pl.when(pl.program_id(0) == 0)
        def _():
            carry[...] = jnp.zeros_like(carry)
        logf = -_softplus(-(x_ref[...] + b_ref[...]))
        r = lax.broadcasted_iota(jnp.int32, (tb, tb), 0)
        c = lax.broadcasted_iota(jnp.int32, (tb, tb), 1)
        tri = (r >= c).astype(bf16)
        hi, lo = _split_hi_lo(logf)
        cum = _dot(tri, hi) + _dot(tri, lo) + carry[0:1, :]
        o_ref[...] = cum
        carry[0:1, :] = cum[tb - 1:tb, :]

    return pl.pallas_call(
        body, grid=(T // tb,),
        in_specs=[pl.BlockSpec((tb, 128), lambda i: (i, C_SMALL // 128)), pl.BlockSpec((1, 128), lambda i: (0, 0))],
        out_specs=pl.BlockSpec((tb, 128), lambda i: (i, 0)), out_shape=_sds((T, 128)),
        scratch_shapes=[pltpu.VMEM((8, 128), f32)],
        compiler_params=_cparams(("arbitrary",)), name=name)(proj, bfv)


def fox_gate_bwd(dcum, proj, bfv, name):
    T = proj.shape[0]
    tb = min(T, SCAN_TB)
    nb = T // tb

    def body(d_ref, x_ref, b_ref, o_ref, db_ref, carry):
        @pl.when(pl.program_id(0) == 0)
        def _():
            carry[...] = jnp.zeros_like(carry)
            db_ref[...] = jnp.zeros_like(db_ref)
        r = lax.broadcasted_iota(jnp.int32, (tb, tb), 0)
        c = lax.broadcasted_iota(jnp.int32, (tb, tb), 1)
        tri = (c >= r).astype(bf16)
        hi, lo = _split_hi_lo(d_ref[...])
        dlogf = _dot(tri, hi) + _dot(tri, lo) + carry[0:1, :]
        carry[0:1, :] = dlogf[0:1, :]
        lane = lax.broadcasted_iota(jnp.int32, (tb, 128), 1)
        keep = (lane >= F_LANE) & (lane < F_LANE + NH)
        dx = jnp.where(keep, dlogf * jax.nn.sigmoid(-(x_ref[...] + b_ref[...])), 0.0)
        o_ref[...] = dx
        db_ref[...] += jnp.sum(dx, axis=0, keepdims=True)

    rev = lambda i: nb - 1 - i
    return pl.pallas_call(
        body, grid=(nb,),
        in_specs=[pl.BlockSpec((tb, 128), lambda i: (rev(i), 0)), pl.BlockSpec((tb, 128), lambda i: (rev(i), C_SMALL // 128)),
                  pl.BlockSpec((1, 128), lambda i: (0, 0))],
        out_specs=[pl.BlockSpec((tb, 128), lambda i: (rev(i), 0)), pl.BlockSpec((1, 128), lambda i: (0, 0))],
        out_shape=[_sds((T, 128)), _sds((1, 128))],
        scratch_shapes=[pltpu.VMEM((8, 128), f32)],
        compiler_params=_cparams(("arbitrary",)), name=name)(dcum, proj, bfv)


ATT_TQ = 512
ATT_TK = 256


def _lane_col(tile, lane):
    li = lax.broadcasted_iota(jnp.int32, tile.shape, 1)
    return jnp.sum(jnp.where(li == lane, tile, 0.0), axis=1, keepdims=True)


def _pack_cols(cols):
    rows = cols[0].shape[0]
    li = lax.broadcasted_iota(jnp.int32, (rows, 128), 1)
    out = jnp.zeros((rows, 128), f32)
    for h, cv in enumerate(cols):
        out = jnp.where(li == h, cv, out)
    return out


def _head_masks():
    li = lax.broadcasted_iota(jnp.int32, (1, 128), 1)
    return [li < HD, li >= HD]


def _qkv_specs(T, tq, base_blk):
    q = pl.BlockSpec((tq, 128), lambda p, i: (i, base_blk + p))
    k = pl.BlockSpec((T, 128), lambda p, i: (0, base_blk + 2 + p))
    v = pl.BlockSpec((T, 128), lambda p, i: (0, base_blk + 4 + p))
    return q, k, v


def _stack_heads(x, masks):
    return jnp.concatenate([jnp.where(m, x, jnp.zeros_like(x)) for m in masks], axis=0)


def _side_by_side(x, tq):
    return jnp.concatenate([x[:tq], x[tq:]], axis=1)


def _stacked_mask(tq, tk, d, strict):
    r = lax.broadcasted_iota(jnp.int32, (2 * tq, tk), 0)
    r = jnp.where(r >= tq, r - tq, r)
    c = lax.broadcasted_iota(jnp.int32, (2 * tq, tk), 1) + d * tk
    return c < r if strict else c <= r


def _sub_head_rows(s, rows2, tq):
    return jnp.concatenate([s[:tq] - rows2[0:1, :], s[tq:] - rows2[1:2, :]], axis=0)


def _lane_cols2(tile):
    return jnp.concatenate([_lane_col(tile, 0), _lane_col(tile, 1)], axis=0)


def _pack_cols2(col, tq):
    return _pack_cols([col[:tq], col[tq:]])


def _dot_hilo2(x, tri):
    n = x.shape[0]
    hi, lo = _split_hi_lo(x)
    r = _dot(jnp.concatenate([hi, lo], axis=0), tri)
    return r[:n] + r[n:]


def fox_fwd(pbf, cumrow, name):
    T = pbf.shape[0]
    tq, tk = min(T, ATT_TQ), min(T, ATT_TK)
    nq, nk, per = T // tq, T // tk, tq // tk
    scale = HD ** -0.5

    def body(q_ref, k_ref, v_ref, cr_ref, o_ref, lse_ref):
        i = pl.program_id(1)
        masks = _head_masks()
        qs2 = _stack_heads(q_ref[...] * scale, masks)

        def tile(kb, carry, d=None):
            m, l, acc = carry
            off = pl.multiple_of(kb * tk, tk)
            v2 = _stack_heads(v_ref[pl.ds(off, tk), :], masks)
            s = _sub_head_rows(_dot_nt(qs2, k_ref[pl.ds(off, tk), :]), cr_ref[kb], tq)
            if d is not None:
                s = jnp.where(_stacked_mask(tq, tk, d, False), s, NEG)
            m_new = jnp.maximum(m, jnp.max(s, axis=1, keepdims=True))
            corr = jnp.exp(m - m_new)
            p = jnp.exp(s - m_new)
            l = l * corr + jnp.sum(p, axis=1, keepdims=True)
            acc = acc * jnp.where(masks[0], corr[:tq], corr[tq:]) + _dot(_side_by_side(p.astype(bf16), tq), v2)
            return m_new, l, acc

        init = (jnp.full((2 * tq, 1), NEG, f32), jnp.zeros((2 * tq, 1), f32), jnp.zeros((tq, 128), f32))
        carry = lax.fori_loop(0, i * per, tile, init)
        for d in range(per):
            carry = tile(i * per + d, carry, d)
        m, l, acc = carry
        o_ref[...] = acc * jnp.where(masks[0], 1.0 / l[:tq], 1.0 / l[tq:])
        lse_ref[...] = _pack_cols2(m + jnp.log(l), tq)

    qs, ks, vs = _qkv_specs(T, tq, C_FOX // 128)
    return pl.pallas_call(
        body, grid=(2, nq),
        in_specs=[qs, ks, vs, pl.BlockSpec((None, nk, 8, tk), lambda p, i: (p, 0, 0, 0))],
        out_specs=[pl.BlockSpec((tq, 128), lambda p, i: (i, p)), pl.BlockSpec((None, tq, 128), lambda p, i: (p, i, 0))],
        out_shape=[_sds((T, GW)), _sds((2, T, 128))],
        compiler_params=_cparams(("parallel", "parallel")), name=name)(pbf, pbf, pbf, cumrow)


def fox_bwd(do, o, lse, pbf, cumrow, name, rider=None):
    T = pbf.shape[0]
    tq, tk = min(T, ATT_TQ), min(T, ATT_TK)
    nq, nk, per = T // tq, T // tk, tq // tk
    scale = HD ** -0.5

    def body(do_ref, o_ref, lse_ref, q_ref, k_ref, v_ref, cr_ref, dq_ref, dk_ref, dv_ref, dc_ref, dcq_ref):
        i = pl.program_id(1)

        @pl.when(i == 0)
        def _():
            dk_ref[...] = jnp.zeros_like(dk_ref)
            dv_ref[...] = jnp.zeros_like(dv_ref)
            dc_ref[...] = jnp.zeros_like(dc_ref)

        masks = _head_masks()
        dov = do_ref[...]
        qs2 = _stack_heads(q_ref[...] * scale, masks)
        do2 = _stack_heads(dov.astype(bf16), masks)
        prod = dov * o_ref[...]
        delta = jnp.concatenate([jnp.sum(jnp.where(m, prod, 0.0), axis=1, keepdims=True) for m in masks], axis=0)
        lse2 = _lane_cols2(lse_ref[...])

        def tile(kb, carry, d=None):
            dq, rsum = carry
            off = pl.multiple_of(kb * tk, tk)
            kblk = k_ref[pl.ds(off, tk), :]
            p = jnp.exp(_sub_head_rows(_dot_nt(qs2, kblk), cr_ref[kb], tq) - lse2)
            if d is not None:
                p = jnp.where(_stacked_mask(tq, tk, d, False), p, 0.0)
            dp = _dot_nt(do2, v_ref[pl.ds(off, tk), :])
            ds = p * (dp - delta)
            dsb = ds.astype(bf16)
            dq = dq + _dot(_side_by_side(dsb, tq), _stack_heads(kblk, masks))
            dk_ref[pl.ds(off, tk), :] += _dot_tn(dsb, qs2)
            dv_ref[pl.ds(off, tk), :] += _dot_tn(p.astype(bf16), do2)
            dc_ref[kb, 0:1, :] += -jnp.sum(ds[:tq], axis=0, keepdims=True)
            dc_ref[kb, 1:2, :] += -jnp.sum(ds[tq:], axis=0, keepdims=True)
            return dq, rsum + jnp.sum(ds, axis=1, keepdims=True)

        carry = lax.fori_loop(0, i * per, tile, (jnp.zeros((tq, 128), f32), jnp.zeros((2 * tq, 1), f32)))
        for d in range(per):
            carry = tile(i * per + d, carry, d)
        dq, rsum = carry
        dq_ref[...] = dq * scale
        dcq_ref[...] = _pack_cols2(rsum, tq)

    qs, ks, vs = _qkv_specs(T, tq, C_FOX // 128)
    tile_spec = pl.BlockSpec((tq, 128), lambda p, i: (i, p))
    pair = pl.BlockSpec((None, tq, 128), lambda p, i: (p, i, 0))
    rowsp = pl.BlockSpec((None, nk, 8, tk), lambda p, i: (p, 0, 0, 0))
    full = pl.BlockSpec((T, 128), lambda p, i: (0, p))
    body, r_in, r_args, r_out, r_shape, r_scratch = _ride(body, 7, 5, rider, (2, nq))
    res = pl.pallas_call(
        body, grid=(2, nq),
        in_specs=[tile_spec, tile_spec, pair, qs, ks, vs, rowsp] + r_in,
        out_specs=[tile_spec, full, full, rowsp, pair] + r_out,
        out_shape=[_sds((T, GW)), _sds((T, GW)), _sds((T, GW)), _sds((2, nk, 8, tk)), _sds((2, T, 128))] + r_shape,
        scratch_shapes=r_scratch,
        compiler_params=_cparams(("arbitrary", "arbitrary")), name=name)(do, o, lse, pbf, pbf, pbf, cumrow, *r_args)
    return res[0], res[1], res[2], res[3], res[4], res[5:]


def _tri(tq, pred):
    r = lax.broadcasted_iota(jnp.int32, (tq, tq), 0)
    c = lax.broadcasted_iota(jnp.int32, (tq, tq), 1)
    return pred(r, c).astype(bf16)


def _ride(body, n_in, n_out, rider, grid, n_scratch=0):
    if rider is None:
        return body, [], [], [], [], []
    nr = rider.n

    def wrapped(*refs):
        ins, rin = refs[:n_in], refs[n_in:n_in + nr]
        outs = refs[n_in + nr:n_in + nr + n_out]
        rout = refs[n_in + nr + n_out:n_in + 2 * nr + n_out]
        own = refs[n_in + 2 * nr + n_out:n_in + 2 * nr + n_out + n_scratch]
        sems = refs[n_in + 2 * nr + n_out + n_scratch:]
        ids = [pl.program_id(a) for a in range(len(grid))]
        first = functools.reduce(jnp.logical_and, [i == 0 for i in ids])
        last = functools.reduce(jnp.logical_and, [i == g - 1 for i, g in zip(ids, grid)])

        @pl.when(first)
        def _():
            rider.start(rin, rout, sems)

        body(*ins, *outs, *own)

        @pl.when(last)
        def _():
            rider.finish(rin, rout, sems)

    anyspec = pl.BlockSpec(memory_space=pl.ANY)
    return wrapped, [anyspec] * nr, list(rider.arrays), [anyspec] * nr, list(rider.out_shape), list(rider.scratch)


def sb_fwd(pbf, name, rider=None):
    T = pbf.shape[0]
    tq, tk = min(T, ATT_TQ), min(T, ATT_TK)
    nq, per = T // tq, tq // tk
    scale = HD ** -0.5

    def body(q_ref, k_ref, v_ref, o_ref, tot_ref):
        i = pl.program_id(1)
        masks = _head_masks()
        qs2 = _stack_heads(q_ref[...] * scale, masks)
        after = _tri(tk, lambda r, c: r > c)

        def tile(kb, carry, d=None):
            rs, acc = carry
            off = pl.multiple_of(kb * tk, tk)
            z = _dot_nt(qs2, k_ref[pl.ds(off, tk), :])
            lk = _neg_softplus(z)
            if d is not None:
                lk = jnp.where(_stacked_mask(tq, tk, d, True), lk, 0.0)
            w = jnp.exp(z + lk + (_dot_hilo2(lk, after) + rs))
            if d is not None:
                w = jnp.where(_stacked_mask(tq, tk, d, True), w, 0.0)
            acc = acc + _dot(_side_by_side(w.astype(bf16), tq), _stack_heads(v_ref[pl.ds(off, tk), :], masks))
            return rs + jnp.sum(lk, axis=1, keepdims=True), acc

        carry = (jnp.zeros((2 * tq, 1), f32), jnp.zeros((tq, 128), f32))
        for d in reversed(range(per)):
            carry = tile(i * per + d, carry, d)
        rs, acc = lax.fori_loop(0, i * per, lambda n, c: tile(i * per - 1 - n, c), carry)
        o_ref[...] = acc
        tot_ref[...] = _pack_cols2(rs, tq)

    qs, ks, vs = _qkv_specs(T, tq, C_SB // 128)
    body, r_in, r_args, r_out, r_shape, r_scratch = _ride(body, 3, 2, rider, (2, nq))
    res = pl.pallas_call(
        body, grid=(2, nq), in_specs=[qs, ks, vs] + r_in,
        out_specs=[pl.BlockSpec((tq, 128), lambda p, i: (i, p)), pl.BlockSpec((None, tq, 128), lambda p, i: (p, i, 0))] + r_out,
        out_shape=[_sds((T, GW)), _sds((2, T, 128))] + r_shape, scratch_shapes=r_scratch,
        compiler_params=_cparams(("arbitrary", "arbitrary")), name=name)(pbf, pbf, pbf, *r_args)
    return res[0], res[1], res[2:]


def sb_bwd(do, tot, pbf, name, rider=None):
    T = pbf.shape[0]
    tq, tk = min(T, ATT_TQ), min(T, ATT_TK)
    nq, per = T // tq, tq // tk
    scale = HD ** -0.5

    def body(do_ref, tot_ref, q_ref, k_ref, v_ref, dq_ref, dk_ref, dv_ref):
        i = pl.program_id(1)

        @pl.when(i == 0)
        def _():
            dk_ref[...] = jnp.zeros_like(dk_ref)
            dv_ref[...] = jnp.zeros_like(dv_ref)

        masks = _head_masks()
        qs2 = _stack_heads(q_ref[...] * scale, masks)
        do2 = _stack_heads(do_ref[...].astype(bf16), masks)
        tot2 = _lane_cols2(tot_ref[...])
        upto = _tri(tk, lambda r, c: r <= c)
        before = _tri(tk, lambda r, c: r < c)

        def tile(kb, carry, d=None):
            pre, cg, dq = carry
            off = pl.multiple_of(kb * tk, tk)
            kblk = k_ref[pl.ds(off, tk), :]
            z = _dot_nt(qs2, kblk)
            lk = _neg_softplus(z)
            keep = jnp.exp(lk)
            if d is not None:
                lk = jnp.where(_stacked_mask(tq, tk, d, True), lk, 0.0)
            w = jnp.exp(z + lk + (tot2 - (pre + _dot_hilo2(lk, upto))))
            if d is not None:
                w = jnp.where(_stacked_mask(tq, tk, d, True), w, 0.0)
            gmat = w * _dot_nt(do2, v_ref[pl.ds(off, tk), :])
            cmat = cg + _dot(gmat.astype(bf16), before)
            dz = gmat * keep - cmat * (1.0 - keep)
            if d is not None:
                dz = jnp.where(_stacked_mask(tq, tk, d, True), dz, 0.0)
            dzb = dz.astype(bf16)
            dq = dq + _dot(_side_by_side(dzb, tq), _stack_heads(kblk, masks))
            dk_ref[pl.ds(off, tk), :] += _dot_tn(dzb, qs2)
            dv_ref[pl.ds(off, tk), :] += _dot_tn(w.astype(bf16), do2)
            return pre + jnp.sum(lk, axis=1, keepdims=True), cg + jnp.sum(gmat, axis=1, keepdims=True), dq

        zc = jnp.zeros((2 * tq, 1), f32)
        carry = lax.fori_loop(0, i * per, tile, (zc, zc, jnp.zeros((tq, 128), f32)))
        for d in range(per):
            carry = tile(i * per + d, carry, d)
        dq_ref[...] = carry[2] * scale

    qs, ks, vs = _qkv_specs(T, tq, C_SB // 128)
    tile_spec = pl.BlockSpec((tq, 128), lambda p, i: (i, p))
    pair = pl.BlockSpec((None, tq, 128), lambda p, i: (p, i, 0))
    full = pl.BlockSpec((T, 128), lambda p, i: (0, p))
    body, r_in, r_args, r_out, r_shape, r_scratch = _ride(body, 5, 3, rider, (2, nq))
    res = pl.pallas_call(
        body, grid=(2, nq), in_specs=[tile_spec, pair, qs, ks, vs] + r_in,
        out_specs=[tile_spec, full, full] + r_out, out_shape=[_sds((T, GW))] * 3 + r_shape, scratch_shapes=r_scratch,
        compiler_params=_cparams(("arbitrary", "arbitrary")), name=name)(do, tot, pbf, pbf, pbf, *r_args)
    return res[0], res[1], res[2], res[3:]


MEM_HD = D // 4


def mem_fwd(q, kv, name):
    T = q.shape[0]
    M = kv.shape[0]
    tm = min(T, 512)
    scale = MEM_HD ** -0.5

    def body(q_ref, kv_ref, o_ref):
        for h in range(4):
            sl = slice(h * MEM_HD, (h + 1) * MEM_HD)
            kh = kv_ref[:, sl].astype(bf16)
            vh = kv_ref[:, D + h * MEM_HD:D + (h + 1) * MEM_HD].astype(bf16)
            s = _dot_nt(q_ref[:, sl], kh) * scale
            e = jnp.exp(s - jnp.max(s, axis=1, keepdims=True))
            p = e / jnp.sum(e, axis=1, keepdims=True)
            o_ref[:, sl] = _dot(p.astype(bf16), vh).astype(bf16)

    return pl.pallas_call(
        body, grid=(T // tm,),
        in_specs=[pl.BlockSpec((tm, D), lambda i: (i, 0)), pl.BlockSpec((M, 2 * D), lambda i: (0, 0))],
        out_specs=pl.BlockSpec((tm, D), lambda i: (i, 0)), out_shape=_sds((T, D), bf16),
        compiler_params=_cparams(("parallel",)), name=name)(q, kv)


def mem_bwd(do, q, kv, name):
    T = q.shape[0]
    M = kv.shape[0]
    tm = min(T, 512)
    scale = MEM_HD ** -0.5

    def body(do_ref, q_ref, kv_ref, dq_ref, dkv_ref):
        @pl.when(pl.program_id(0) == 0)
        def _():
            dkv_ref[...] = jnp.zeros_like(dkv_ref)
        for h in range(4):
            sl = slice(h * MEM_HD, (h + 1) * MEM_HD)
            vsl = slice(D + h * MEM_HD, D + (h + 1) * MEM_HD)
            qh = q_ref[:, sl]
            kh = kv_ref[:, sl].astype(bf16)
            vh = kv_ref[:, vsl].astype(bf16)
            doh = do_ref[:, sl].astype(bf16)
            s = _dot_nt(qh, kh) * scale
            e = jnp.exp(s - jnp.max(s, axis=1, keepdims=True))
            p = e / jnp.sum(e, axis=1, keepdims=True)
            dp = _dot_nt(doh, vh)
            ds = p * (dp - jnp.sum(dp * p, axis=1, keepdims=True))
            dsb = ds.astype(bf16)
            dq_ref[:, sl] = _dot(dsb, kh) * scale
            dkv_ref[:, sl] += _dot_tn(dsb, qh) * scale
            dkv_ref[:, vsl] += _dot_tn(p.astype(bf16), doh)

    row = pl.BlockSpec((tm, D), lambda i: (i, 0))
    whole = pl.BlockSpec((M, 2 * D), lambda i: (0, 0))
    return pl.pallas_call(
        body, grid=(T // tm,), in_specs=[row, row, whole], out_specs=[row, whole],
        out_shape=[_sds((T, D)), _sds((M, 2 * D))],
        compiler_params=_cparams(("arbitrary",)), name=name)(do, q, kv)


def _chip_peers():
    x, y, c = lax.axis_index("x"), lax.axis_index("y"), lax.axis_index("c")
    me = 2 * x + y
    peers = [((1 - x, y, c), 2 * (1 - x) + y), ((x, 1 - y, c), 2 * x + (1 - y)), ((1 - x, 1 - y, c), 2 * (1 - x) + (1 - y))]
    return me, peers


class Rider:
    def __init__(self, arrays, out_shape, scratch, start, finish):
        self.arrays, self.out_shape, self.scratch, self.start, self.finish = arrays, out_shape, scratch, start, finish
        self.n = len(arrays)

    def split(self, refs):
        return refs[:self.n], refs[self.n:2 * self.n], refs[2 * self.n:]


def run_rider(rider, name):
    def body(*refs):
        parts = rider.split(refs)
        rider.start(*parts)
        rider.finish(*parts)

    anyspec = pl.BlockSpec(memory_space=pl.ANY)
    return pl.pallas_call(
        body, in_specs=[anyspec] * rider.n, out_specs=[anyspec] * rider.n, out_shape=rider.out_shape,
        scratch_shapes=rider.scratch, name=name)(*rider.arrays)


def gather_rider(arrs, split):
    n = len(arrs)

    def copies(ins, outs, sems):
        send_sems, recv_sems, pass_send, pass_recv, loc_sems = sems
        x, y, c = lax.axis_index("x"), lax.axis_index("y"), lax.axis_index("c")
        me, peers = _chip_peers()

        def mine(ref, i):
            if not split[i]:
                return ref
            r = arrs[i].shape[1] // 2
            return ref.at[:, pl.ds(c * r, r), :]

        local, fetch, passed = [], [], []
        for i in range(n):
            local.append(pltpu.make_async_copy(ins[i], outs[i].at[me], loc_sems.at[i]))
            for k, (dev, pj) in enumerate(peers):
                fetch.append(pltpu.make_async_remote_copy(src_ref=mine(ins[i], i), dst_ref=mine(outs[i].at[me], i),
                                                          send_sem=send_sems.at[i, k], recv_sem=recv_sems.at[i, k],
                                                          device_id=dev, device_id_type=MESH))
                rows = mine(outs[i].at[pj], i)
                passed.append(pltpu.make_async_remote_copy(src_ref=rows, dst_ref=rows, send_sem=pass_send.at[i, k],
                                                           recv_sem=pass_recv.at[i, k], device_id=(x, y, 1 - c),
                                                           device_id_type=MESH) if split[i] else None)
        return local, fetch, passed

    def start(ins, outs, sems):
        local, fetch, _ = copies(ins, outs, sems)
        for cp in local + fetch:
            cp.start()

    def finish(ins, outs, sems):
        local, fetch, passed = copies(ins, outs, sems)
        for cp, fw in zip(fetch, passed):
            cp.wait_recv()
            if fw is not None:
                fw.start()
        for cp in fetch:
            cp.wait_send()
        for cp in [fw for fw in passed if fw is not None] + local:
            cp.wait()

    sem = pltpu.SemaphoreType.DMA((n, 3))
    return Rider(list(arrs), [_sds((4,) + a.shape, a.dtype) for a in arrs],
                 [sem, sem, sem, sem, pltpu.SemaphoreType.DMA((n,))], start, finish)


def exchange_rider(stacks):
    n = len(stacks)

    def copies(ins, outs, sems):
        send_sems, recv_sems, loc_sems = sems
        me, peers = _chip_peers()
        out = []
        for i in range(n):
            out.append(pltpu.make_async_copy(ins[i].at[me], outs[i].at[me], loc_sems.at[i]))
            for k, (dev, pj) in enumerate(peers):
                out.append(pltpu.make_async_remote_copy(src_ref=ins[i].at[pj], dst_ref=outs[i].at[me], send_sem=send_sems.at[i, k],
                                                        recv_sem=recv_sems.at[i, k], device_id=dev, device_id_type=MESH))
        return out

    def start(ins, outs, sems):
        for cp in copies(ins, outs, sems):
            cp.start()

    def finish(ins, outs, sems):
        for cp in copies(ins, outs, sems):
            cp.wait()

    return Rider(list(stacks), [_sds(a.shape, a.dtype) for a in stacks],
                 [pltpu.SemaphoreType.DMA((n, 3)), pltpu.SemaphoreType.DMA((n, 3)), pltpu.SemaphoreType.DMA((n,))], start, finish)


def swap_sibling(arrs, name):
    n = len(arrs)

    def body(*refs):
        ins, outs = refs[:n], refs[n:2 * n]
        send_sems, recv_sems = refs[2 * n:]
        x, y, c = lax.axis_index("x"), lax.axis_index("y"), lax.axis_index("c")
        started = []
        for i in range(n):
            cp = pltpu.make_async_remote_copy(src_ref=ins[i], dst_ref=outs[i], send_sem=send_sems.at[i],
                                              recv_sem=recv_sems.at[i], device_id=(x, y, 1 - c), device_id_type=MESH)
            cp.start()
            started.append(cp)
        for cp in started:
            cp.wait()

    anyspec = pl.BlockSpec(memory_space=pl.ANY)
    return pl.pallas_call(
        body, in_specs=[anyspec] * n, out_specs=[anyspec] * n,
        out_shape=[_sds(a.shape, a.dtype) for a in arrs],
        scratch_shapes=[pltpu.SemaphoreType.DMA((n,)), pltpu.SemaphoreType.DMA((n,))],
        name=name)(*arrs)


def gather_all(a, name):
    def body(a_ref, o_ref, send_sems, recv_sems, loc_sem):
        x, y, c = lax.axis_index("x"), lax.axis_index("y"), lax.axis_index("c")
        me = 4 * x + 2 * y + c
        loc = pltpu.make_async_copy(a_ref, o_ref.at[me], loc_sem)
        loc.start()
        started = [loc]
        for k in range(1, 8):
            dev = (x ^ (k >> 2), y ^ ((k >> 1) & 1), c ^ (k & 1))
            cp = pltpu.make_async_remote_copy(src_ref=a_ref, dst_ref=o_ref.at[me], send_sem=send_sems.at[k - 1],
                                              recv_sem=recv_sems.at[k - 1], device_id=dev, device_id_type=MESH)
            cp.start()
            started.append(cp)
        for cp in started:
            cp.wait()

    anyspec = pl.BlockSpec(memory_space=pl.ANY)
    return pl.pallas_call(
        body, in_specs=[anyspec], out_specs=anyspec, out_shape=_sds((8,) + a.shape, a.dtype),
        scratch_shapes=[pltpu.SemaphoreType.DMA((7,)), pltpu.SemaphoreType.DMA((7,)), pltpu.SemaphoreType.DMA(())],
        name=name)(a)


def sum_slots(stack, name):
    n, R, C = stack.shape
    tr = R if R <= 512 else _pick(R, (512, 352, 256))

    def body(s_ref, o_ref):
        acc = s_ref[0].astype(f32)
        for j in range(1, n):
            acc = acc + s_ref[j].astype(f32)
        o_ref[...] = acc

    return pl.pallas_call(
        body, grid=(R // tr,), in_specs=[pl.BlockSpec((n, tr, C), lambda i: (0, i, 0))],
        out_specs=pl.BlockSpec((tr, C), lambda i: (i, 0)), out_shape=_sds((R, C)),
        compiler_params=_cparams(("parallel",)), name=name)(stack)


def adamw(w, g1, g2, m, v, name):
    R, C = w.shape
    tr = R if R <= 512 else _pick(R, (512, 352, 256))
    c1 = 1.0 - ADAM_B1 ** ADAM_STEP
    c2 = 1.0 - ADAM_B2 ** ADAM_STEP

    def body(*refs):
        if g2 is None:
            w_ref, g1_ref, m_ref, v_ref, g_out, d_out, m_out, v_out = refs
            g = g1_ref[...]
        else:
            w_ref, g1_ref, g2_ref, m_ref, v_ref, g_out, d_out, m_out, v_out = refs
            g = g1_ref[...] + g2_ref[...]
        mn = ADAM_B1 * m_ref[...] + (1.0 - ADAM_B1) * g
        vn = ADAM_B2 * v_ref[...] + (1.0 - ADAM_B2) * (g * g)
        g_out[...] = g
        m_out[...] = mn
        v_out[...] = vn
        d_out[...] = -ADAM_LR * ((mn / c1) / (jnp.sqrt(vn / c2) + ADAM_EPS) + ADAM_WD * w_ref[...])

    blk = pl.BlockSpec((tr, C), lambda i: (i, 0))
    args = [w, g1] + ([] if g2 is None else [g2]) + [m, v]
    return pl.pallas_call(
        body, grid=(R // tr,), in_specs=[blk] * len(args), out_specs=[blk] * 4, out_shape=[_sds((R, C))] * 4,
        compiler_params=_cparams(("parallel",)), name=name)(*args)


IN_SPLITS = (768, 256, 4, 4, 768, 4, 512, 768)
IN_OFF = (0, 768, 1024, 1028, 1032, 1800, 1804, 2316, 3084)


def regroup_w_in(w):
    seg = lambda i: w[:, IN_OFF[i]:IN_OFF[i + 1]]
    pad = jnp.zeros((w.shape[0], PW - C_SMALL - 12), w.dtype)
    return jnp.concatenate([seg(0), seg(1), seg(4), seg(6), seg(7), seg(2), seg(3), seg(5), pad], axis=1)


def ungroup_w_in(g):
    s = C_SMALL
    return jnp.concatenate([g[:, 0:1024], g[:, s:s + 8], g[:, 1024:1792], g[:, s + 8:s + 12], g[:, 1792:3072]], axis=1)


def to_hm(t, nh):
    T = t.shape[0]
    return t.reshape(T, nh, HD).transpose(1, 0, 2)


def from_hm(t):
    nh, T, _ = t.shape
    return t.transpose(1, 0, 2).reshape(T, nh * HD)


def col_shards(w):
    c = w.shape[-1] // 4
    return jnp.moveaxis(w.reshape(w.shape[:-1] + (4, c)), -2, 0)


def row_shards(w):
    L, r4, c = w.shape
    return w.reshape(L, 4, r4 // 4, c).transpose(1, 0, 2, 3)


def join_cols(g):
    return jnp.moveaxis(g, 0, -2).reshape(g.shape[1:-1] + (4 * g.shape[-1],))


def join_rows(g):
    _, L, r, c = g.shape
    return g.transpose(1, 0, 2, 3).reshape(L, 4 * r, c)


COL_SHARDED = ("ffn1_w_gate", "ffn1_w_up", "w_in", "gdn_conv_w", "conf_dw_w", "mem_w_kv", "ffn2_w_gate", "ffn2_w_up")
CONV_WEIGHTS = ("gdn_conv_w", "conf_dw_w")
ROW_SHARDED = ("ffn1_w_down", "w_out", "mem_w_q", "mem_w_o", "ffn2_w_down")
REPLICATED = ("ln_ffn1_g", "ln_ffn1_b", "gdn_a_log", "gdn_dt_bias", "gdn_norm_g", "fox_b_f", "conf_dw_b", "conf_norm_g",
              "conf_norm_b", "ln_mix_g", "ln_mix_b", "ln_mem_g", "ln_mem_b", "ln_ffn2_g", "ln_ffn2_b")
WEIGHTS = ("ffn1_w_gate", "ffn1_w_up", "ffn1_w_down", "ln_ffn1_g", "ln_ffn1_b", "w_in", "gdn_conv_w", "gdn_a_log",
           "gdn_dt_bias", "gdn_norm_g", "fox_b_f", "conf_dw_w", "conf_dw_b", "conf_norm_g", "conf_norm_b", "w_out",
           "ln_mix_g", "ln_mix_b", "mem_w_q", "mem_w_kv", "mem_w_o", "ln_mem_g", "ln_mem_b", "ffn2_w_gate",
           "ffn2_w_up", "ffn2_w_down", "ln_ffn2_g", "ln_ffn2_b")


def pack_small(d):
    flat = jnp.concatenate([d[n].reshape(-1) for n in REPLICATED])
    rows = -(-flat.shape[0] // 1024) * 8
    return jnp.pad(flat, (0, rows * 128 - flat.shape[0])).reshape(rows, 128)


def unpack_small(p, like):
    flat = p.reshape(-1)
    out, o = {}, 0
    for n in REPLICATED:
        sz = like[n].size
        out[n] = flat[o:o + sz].reshape(like[n].shape)
        o += sz
    return out


def _vec(v):
    return v.reshape(1, -1)


def _pad_rows(w, rows):
    return jnp.pad(w, ((0, rows - w.shape[0]), (0, 0)))


def _small_lane_vec(v4, lane0):
    return jnp.pad(v4.reshape(1, -1), ((0, 0), (lane0, 128 - lane0 - v4.shape[0])))


def layer_fwd(x0, mem, W, li, rider=None, ffn1_rider=None, complete=None):
    T = x0.shape[0]
    tk = min(T, ATT_TK)
    n = lambda s: f"l{li}_{s}"
    R = {"x0": x0}
    R["z1"], x1, got = ffn_fwd(x0, W["ffn1_w_gate"], W["ffn1_w_up"], W["ffn1_w_down"], _vec(W["ln_ffn1_g"]), _vec(W["ln_ffn1_b"]),
                               n("ffn1_fwd"), ffn1_rider)
    if ffn1_rider is not None:
        W = complete(W, got)
    R["x1"] = x1
    proj, pbf = mm_nn(x1, W["w_in_r"], n("proj"), also_bf16=True)
    R["proj"], R["pbf"] = proj, pbf

    w8 = _pad_rows(W["gdn_conv_w"], 8)
    qkv_s = gdn_conv_fwd(proj, w8, n("gdn_conv_fwd"))
    qkv_hm = to_hm(qkv_s, 12)
    z_hm = to_hm(proj[:, C_GZ:C_GZ + GW], 4)
    ab_hm = proj[:, C_SMALL:C_SMALL + 8].T.reshape(8, T, 1)
    alog = W["gdn_a_log"].reshape(NH, 1, 1)
    dtb = W["gdn_dt_bias"].reshape(NH, 1, 1)
    ng = W["gdn_norm_g"].reshape(1, 1, HD)
    ya_hm, states = gdn_fwd(qkv_hm, z_hm, ab_hm, alog, dtb, ng, n("gdn_fwd"))
    R.update(qkv_hm=qkv_hm, z_hm=z_hm, ab_hm=ab_hm, states=states)

    bfv = _small_lane_vec(W["fox_b_f"], F_LANE)
    cum = fox_gate_fwd(proj, bfv, n("fox_gate_fwd"))
    cum4 = cum[:, F_LANE:F_LANE + NH]
    cumrow = jnp.pad(cum4.T.reshape(2, 2, T // tk, tk).transpose(0, 2, 1, 3), ((0, 0), (0, 0), (0, 6), (0, 0)))
    yb, lse = fox_fwd(pbf, cumrow, n("fox_fwd"))
    R.update(cumrow=cumrow, yb=yb, lse=lse)

    w32 = _pad_rows(W["conf_dw_w"], 32)
    yc, cc = conf_fwd(proj, w32, _vec(W["conf_dw_b"]), _vec(W["conf_norm_g"]), _vec(W["conf_norm_b"]), n("conf_fwd"))
    R["cc"] = cc

    yd, tot, rider_out = sb_fwd(pbf, n("sb_fwd"), rider)
    R["tot"] = tot

    ycat = jnp.concatenate([from_hm(ya_hm), yb, yc, yd], axis=1).astype(bf16)
    R["ycat"] = ycat
    R["z2"], x2 = lin_res_ln(ycat, W["w_out"], x1, _vec(W["ln_mix_g"]), _vec(W["ln_mix_b"]), n("mix_out"))
    R["x2"] = x2

    qm = mm_nn(x2, W["mem_w_q"], n("mem_q"), out_dtype=bf16)
    kv = mm_nn(mem, W["mem_w_kv"], n("mem_kv"))
    om = mem_fwd(qm, kv, n("mem_fwd"))
    R.update(qm=qm, kv=kv, om=om)
    R["z3"], x3 = lin_res_ln(om, W["mem_w_o"], x2, _vec(W["ln_mem_g"]), _vec(W["ln_mem_b"]), n("mem_out"))
    R["x3"] = x3
    R["z4"], x4, _ = ffn_fwd(x3, W["ffn2_w_gate"], W["ffn2_w_up"], W["ffn2_w_down"], _vec(W["ln_ffn2_g"]), _vec(W["ln_ffn2_b"]), n("ffn2_fwd"))
    return x4, R, rider_out, W


EARLY_GRADS = ("ffn2_w_gate", "ffn2_w_up", "ffn2_w_down", "mem_w_q", "mem_w_kv", "mem_w_o", "w_out")
LATE_GRADS = ("w_in", "gdn_conv_w", "conf_dw_w")
LAST_GRADS = ("ffn1_w_gate", "ffn1_w_up", "ffn1_w_down")


def grad_slots(G, names):
    out = []
    for k in names:
        s = col_shards(G[k][None]) if k in COL_SHARDED else row_shards(G[k][None])
        out.append(s if k in CONV_WEIGHTS else s.astype(bf16))
    return out


def layer_bwd(dx4, mem, W, R, li, rider=None, exchange=True):
    T = dx4.shape[0]
    n = lambda s: f"l{li}_{s}"
    G = {}

    def ffn_back(dy, z, x, pre, tag, ride=None):
        dz, dg, db = ln_bwd(dy, z, _vec(W[f"ln_{pre}_g"]), n(f"{tag}_ln_bwd"))
        dx, a, dh, du, dzh, got = ffn_bwd(dz, x, W[f"{pre}_w_gate"], W[f"{pre}_w_up"], W[f"{pre}_w_down"], n(f"{tag}_bwd"), ride)
        G[f"{pre}_w_gate"] = mm_tn(x, dh, n(f"{tag}_dwg"))
        G[f"{pre}_w_up"] = mm_tn(x, du, n(f"{tag}_dwu"))
        G[f"{pre}_w_down"] = mm_tn(a, dzh, n(f"{tag}_dwd"))
        G[f"ln_{pre}_g"], G[f"ln_{pre}_b"] = dg.reshape(-1), db.reshape(-1)
        return dx, got

    dx3, _ = ffn_back(dx4, R["z4"], R["x3"], "ffn2", "ffn2")

    dz3, dg, db = ln_bwd(dx3, R["z3"], _vec(W["ln_mem_g"]), n("mem_ln_bwd"))
    G["ln_mem_g"], G["ln_mem_b"] = dg.reshape(-1), db.reshape(-1)
    dom = mm_nt(dz3, W["mem_w_o"], n("mem_dom"))
    G["mem_w_o"] = mm_tn(R["om"], dz3, n("mem_dwo"))
    dqm, dkv = mem_bwd(dom, R["qm"], R["kv"], n("mem_bwd"))
    G["mem_w_q"] = mm_tn(R["x2"], dqm, n("mem_dwq"))
    G["mem_w_kv"] = mm_tn(mem, dkv, n("mem_dwkv"))
    dx2 = mm_nt(dqm, W["mem_w_q"], n("mem_dx"), add=dz3, add_scale=ALPHA)

    dz2, dg, db = ln_bwd(dx2, R["z2"], _vec(W["ln_mix_g"]), n("mix_ln_bwd"))
    G["ln_mix_g"], G["ln_mix_b"] = dg.reshape(-1), db.reshape(-1)
    dycat = mm_nt(dz2, W["w_out"], n("mix_dycat"))
    G["w_out"] = mm_tn(R["ycat"], dz2, n("mix_dwout"))
    dya, dyb, dyc, dyd = (dycat[:, i * GW:(i + 1) * GW] for i in range(4))
    proj, pbf = R["proj"], R["pbf"]

    alog = W["gdn_a_log"].reshape(NH, 1, 1)
    dtb = W["gdn_dt_bias"].reshape(NH, 1, 1)
    ng = W["gdn_norm_g"].reshape(1, 1, HD)
    dqkv_hm, dz_hm, dab_hm, dalog, ddtb, dng = gdn_bwd(to_hm(dya, 4), R["states"], R["qkv_hm"], R["z_hm"], R["ab_hm"],
                                                      alog, dtb, ng, n("gdn_bwd"))
    G["gdn_a_log"], G["gdn_dt_bias"], G["gdn_norm_g"] = dalog.reshape(-1), ddtb.reshape(-1), dng.reshape(-1)
    w8 = _pad_rows(W["gdn_conv_w"], 8)
    dgqkv, dw8 = gdn_conv_bwd(from_hm(dqkv_hm), proj, w8, n("gdn_conv_bwd"))
    G["gdn_conv_w"] = dw8[:GDN_K]

    dfq, dfk, dfv, dcumrow, dcumq, got_early = fox_bwd(dyb, R["yb"], R["lse"], pbf, R["cumrow"], n("fox_bwd"),
                                                      exchange_rider(grad_slots(G, EARLY_GRADS)) if exchange else None)
    dcum4 = dcumrow[:, :, 0:2, :].transpose(0, 2, 1, 3).reshape(4, T).T
    dcum4 = dcum4 + dcumq[:, :, 0:2].transpose(1, 0, 2).reshape(T, 4)
    dcum = jnp.pad(dcum4, ((0, 0), (F_LANE, 128 - F_LANE - NH)))
    bfv = _small_lane_vec(W["fox_b_f"], F_LANE)
    dsmall_f, dbf = fox_gate_bwd(dcum, proj, bfv, n("fox_gate_bwd"))
    G["fox_b_f"] = dbf[0, F_LANE:F_LANE + NH]

    w32 = _pad_rows(W["conf_dw_w"], 32)
    dglu, dw32, dcb, dcg, dcbeta = conf_bwd(dyc, R["cc"], proj, w32, _vec(W["conf_norm_g"]), _vec(W["conf_norm_b"]), n("conf_bwd"))
    G["conf_dw_w"], G["conf_dw_b"] = dw32[:CONF_K], dcb.reshape(-1)
    G["conf_norm_g"], G["conf_norm_b"] = dcg.reshape(-1), dcbeta.reshape(-1)

    dsq, dsk, dsv, got_carried = sb_bwd(dyd, R["tot"], pbf, n("sb_bwd"), rider)

    dsmall = jnp.concatenate([dab_hm.reshape(8, T).T, dsmall_f[:, F_LANE:F_LANE + NH], jnp.zeros((T, PW - C_SMALL - 12), f32)], axis=1)
    dproj = jnp.concatenate([dgqkv, from_hm(dz_hm), dfq, dfk, dfv, dglu, dsq, dsk, dsv, dsmall], axis=1).astype(bf16)
    G["w_in"] = ungroup_w_in(mm_tn(R["x1"], dproj, n("proj_dw")))
    dx1 = mm_nt(dproj, W["w_in_r"], n("proj_dx"), add=dz2, add_scale=ALPHA)

    dx0, got_late = ffn_back(dx1, R["z1"], R["x0"], "ffn1", "ffn1", exchange_rider(grad_slots(G, LATE_GRADS)) if exchange else None)
    return dx0, G, {"carried": got_carried, "early": got_early, "late": got_late}


def _step(P, M, V, x, mem, loss_target):
    xs, mems, tgt = x[0], mem[0], loss_target[0]

    big = COL_SHARDED + ROW_SHARDED

    def weight_gather(li, names):
        return gather_rider([P[k][li:li + 1] if k in CONV_WEIGHTS else P[k][li:li + 1].astype(bf16) for k in names],
                            [k not in CONV_WEIGHTS for k in names])

    def layer_weights(li, names, gathered, W=None):
        W = dict(W) if W else {k: P[k][li] for k in REPLICATED}
        W.update({k: (join_cols(g) if k in COL_SHARDED else join_rows(g))[0] for k, g in zip(names, gathered)})
        if "w_in" in names:
            W["w_in_r"] = regroup_w_in(W["w_in"])
        return W

    rest = tuple(k for k in big if k not in LAST_GRADS)
    W = layer_weights(0, LAST_GRADS, run_rider(weight_gather(0, LAST_GRADS), "gather_weights_first"))
    layers, saved = [], []
    h = xs
    for li in range(DEPTH):
        nxt = weight_gather(li + 1, big) if li + 1 < DEPTH else None
        if li == 0:
            h, R, gathered, W = layer_fwd(h, mems, W, li, nxt, weight_gather(0, rest),
                                          lambda W0, got: layer_weights(0, rest, got, W0))
        else:
            h, R, gathered, W = layer_fwd(h, mems, W, li, nxt)
        layers.append(W)
        saved.append(R)
        if li + 1 < DEPTH:
            W = layer_weights(li + 1, big, gathered)
    loss_row, dy = loss_and_grad(h, tgt, "loss")
    loss = lax.psum(loss_row[0, 0], ("x", "y", "c"))

    grads, received_at = [None] * DEPTH, [dict() for _ in range(DEPTH)]
    rider = None
    for li in reversed(range(DEPTH)):
        dy, G, got = layer_bwd(dy, mems, layers[li], saved[li], li, rider)
        if rider is not None:
            received_at[li + 1].update(zip(LAST_GRADS, got["carried"]))
        received_at[li].update(zip(EARLY_GRADS, got["early"]))
        received_at[li].update(zip(LATE_GRADS, got["late"]))
        grads[li] = G
        rider = exchange_rider(grad_slots(G, LAST_GRADS))
    received_at[0].update(zip(LAST_GRADS, run_rider(rider, "exchange_grads_last")))
    grad_x = dy[None]

    stacked = {k: jnp.stack([grads[li][k] for li in range(DEPTH)]) for k in WEIGHTS}
    received = [jnp.concatenate([received_at[li][k] for li in range(DEPTH)], axis=1) for k in big]
    partial_sums = []
    for k, r in zip(big, received):
        shp = r.shape
        partial_sums.append(sum_slots(r.reshape(4, -1, shp[-1]), f"sum_{k}"))
    from_sibling = swap_sibling(partial_sums, "swap_partials")

    out_g, out_d, out_m, out_v = {}, {}, {}, {}
    for k, mine, theirs in zip(big, partial_sums, from_sibling):
        shp = P[k].shape
        flat = lambda t: t.reshape(-1, shp[-1])
        g, d, mn, vn = adamw(flat(P[k]), mine, theirs, flat(M[k]), flat(V[k]), f"adamw_{k}")
        out_g[k], out_d[k], out_m[k], out_v[k] = (t.reshape(shp) for t in (g, d, mn, vn))

    gsmall = sum_slots(gather_all(pack_small(stacked), "gather_small"), "sum_small")
    g, d, mn, vn = adamw(pack_small(P), gsmall, None, pack_small(M), pack_small(V), "adamw_small")
    for dst, packed in ((out_g, g), (out_d, d), (out_m, mn), (out_v, vn)):
        dst.update(unpack_small(packed, P))

    return (loss, grad_x, *[out_g[k] for k in WEIGHTS], *[out_d[k] for k in WEIGHTS],
            *[out_m[k] for k in WEIGHTS], *[out_v[k] for k in WEIGHTS])


def kernel(x, mem, ffn1_w_gate, ffn1_w_up, ffn1_w_down, ln_ffn1_g, ln_ffn1_b, w_in, gdn_conv_w, gdn_a_log, gdn_dt_bias, gdn_norm_g, fox_b_f, conf_dw_w, conf_dw_b, conf_norm_g, conf_norm_b, w_out, ln_mix_g, ln_mix_b, mem_w_q, mem_w_kv, mem_w_o, ln_mem_g, ln_mem_b, ffn2_w_gate, ffn2_w_up, ffn2_w_down, ln_ffn2_g, ln_ffn2_b, loss_target, m_ffn1_w_gate, m_ffn1_w_up, m_ffn1_w_down, m_ln_ffn1_g, m_ln_ffn1_b, m_w_in, m_gdn_conv_w, m_gdn_a_log, m_gdn_dt_bias, m_gdn_norm_g, m_fox_b_f, m_conf_dw_w, m_conf_dw_b, m_conf_norm_g, m_conf_norm_b, m_w_out, m_ln_mix_g, m_ln_mix_b, m_mem_w_q, m_mem_w_kv, m_mem_w_o, m_ln_mem_g, m_ln_mem_b, m_ffn2_w_gate, m_ffn2_w_up, m_ffn2_w_down, m_ln_ffn2_g, m_ln_ffn2_b, v_ffn1_w_gate, v_ffn1_w_up, v_ffn1_w_down, v_ln_ffn1_g, v_ln_ffn1_b, v_w_in, v_gdn_conv_w, v_gdn_a_log, v_gdn_dt_bias, v_gdn_norm_g, v_fox_b_f, v_conf_dw_w, v_conf_dw_b, v_conf_norm_g, v_conf_norm_b, v_w_out, v_ln_mix_g, v_ln_mix_b, v_mem_w_q, v_mem_w_kv, v_mem_w_o, v_ln_mem_g, v_ln_mem_b, v_ffn2_w_gate, v_ffn2_w_up, v_ffn2_w_down, v_ln_ffn2_g, v_ln_ffn2_b):
    a = locals()
    P = {k: a[k] for k in WEIGHTS}
    M = {k: a["m_" + k] for k in WEIGHTS}
    V = {k: a["v_" + k] for k in WEIGHTS}
    return _step(P, M, V, x, mem, loss_target)
```

```python
import functools

import jax
import jax.numpy as jnp
from jax import lax
from jax.experimental import pallas as pl
from jax.experimental.pallas import tpu as pltpu

f32 = jnp.float32
bf16 = jnp.bfloat16

D = 1024
F = 2816
GW = 256
HD = 64
NH = 4
CHUNK = 64
CONF_K = 31
GDN_K = 4
DEPTH = 2
ALPHA = float((2 * DEPTH) ** 0.25)
LN_EPS = 1e-5
RMS_EPS = 1e-6
L2_EPS = 1e-6
NEG = -1e30
PW = 3200
C_GQKV, C_GZ, C_FOX, C_CONF, C_SB, C_SMALL = 0, 768, 1024, 1792, 2304, 3072
ADAM_LR, ADAM_B1, ADAM_B2, ADAM_EPS, ADAM_WD, ADAM_STEP = 0.001, 0.9, 0.999, 1e-08, 0.01, 10
VMEM_LIMIT = 56 * 1024 * 1024
MESH = pl.DeviceIdType.MESH


def _cparams(sem):
    return pltpu.CompilerParams(dimension_semantics=sem, vmem_limit_bytes=VMEM_LIMIT)


def _pick(n, cands):
    for c in cands:
        if n % c == 0:
            return c
    return n


def _sds(shape, dtype=f32):
    return jax.ShapeDtypeStruct(shape, dtype)


def _layer_norm(z, g, b):
    mu = jnp.mean(z, axis=-1, keepdims=True)
    zc = z - mu
    var = jnp.mean(zc * zc, axis=-1, keepdims=True)
    return zc * lax.rsqrt(var + LN_EPS) * g + b


def _softplus(x):
    return jnp.maximum(x, 0.0) + jnp.log(1.0 + jnp.exp(-jnp.abs(x)))


def _neg_softplus(z):
    nz = -z
    return jnp.minimum(nz, 0.0) - jnp.log(1.0 + jnp.exp(jnp.minimum(z, nz)))


def _dsilu(x):
    s = jax.nn.sigmoid(x)
    return s * (1.0 + x * (1.0 - s))


def _split_hi_lo(x):
    hi = x.astype(bf16)
    lo = (x - hi.astype(f32)).astype(bf16)
    return hi, lo


def _dot(a, b):
    return jnp.dot(a, b, preferred_element_type=f32)


def _dot_nt(a, b):
    return lax.dot_general(a, b, (((1,), (1,)), ((), ())), preferred_element_type=f32)


def _dot_tn(a, b):
    return lax.dot_general(a, b, (((0,), (0,)), ((), ())), preferred_element_type=f32)


def mm_nn(a, w, name, out_dtype=f32, also_bf16=False):
    T, K = a.shape
    N = w.shape[1]
    tm = min(T, 512)
    tn = N if N <= 1024 else _pick(N, (640, 512))

    def body(a_ref, w_ref, *o_refs):
        r = _dot(a_ref[...].astype(bf16), w_ref[...].astype(bf16))
        o_refs[0][...] = r.astype(o_refs[0].dtype)
        if also_bf16:
            o_refs[1][...] = r.astype(bf16)

    out_shape = [_sds((T, N), out_dtype)]
    out_specs = [pl.BlockSpec((tm, tn), lambda i, j: (i, j))]
    if also_bf16:
        out_shape.append(_sds((T, N), bf16))
        out_specs.append(pl.BlockSpec((tm, tn), lambda i, j: (i, j)))
    res = pl.pallas_call(
        body, grid=(T // tm, N // tn),
        in_specs=[pl.BlockSpec((tm, K), lambda i, j: (i, 0)), pl.BlockSpec((K, tn), lambda i, j: (0, j))],
        out_specs=out_specs, out_shape=out_shape,
        compiler_params=_cparams(("parallel", "arbitrary")), name=name)(a, w)
    return res if also_bf16 else res[0]


def _pieces(g):
    gs = list(g) if isinstance(g, (list, tuple)) else [g]
    offs = [sum(p.shape[1] for p in gs[:i]) for i in range(len(gs))]
    return gs, offs, offs[-1] + gs[-1].shape[1]


def mm_nt(g, w, name, add=None, add_scale=1.0):
    gs, offs, N = _pieces(g)
    T = gs[0].shape[0]
    K = w.shape[0]
    tm = min(T, 512)
    ng = len(gs)

    def body(*refs):
        w_ref, o_ref = refs[ng], refs[-1]
        r = None
        for g_ref, off in zip(refs[:ng], offs):
            part = _dot_nt(g_ref[...].astype(bf16), w_ref[:, off:off + g_ref.shape[1]].astype(bf16))
            r = part if r is None else r + part
        if add is not None:
            r = r + add_scale * refs[ng + 1][...]
        o_ref[...] = r

    in_specs = [pl.BlockSpec((tm, p.shape[1]), lambda i: (i, 0)) for p in gs] + [pl.BlockSpec((K, N), lambda i: (0, 0))]
    args = gs + [w]
    if add is not None:
        in_specs.append(pl.BlockSpec((tm, K), lambda i: (i, 0)))
        args.append(add)
    return pl.pallas_call(
        body, grid=(T // tm,), in_specs=in_specs,
        out_specs=pl.BlockSpec((tm, K), lambda i: (i, 0)), out_shape=_sds((T, K)),
        compiler_params=_cparams(("parallel",)), name=name)(*args)


def mm_tn(a, g, name):
    gs, offs, N = _pieces(g)
    T, K = a.shape
    tt = min(T, 512)
    tk = K if K * N * 4 <= 14 * 1024 * 1024 else _pick(K, (512, 1408))

    def body(a_ref, *refs):
        o_ref = refs[-1]

        @pl.when(pl.program_id(1) == 0)
        def _():
            o_ref[...] = jnp.zeros_like(o_ref)
        if len(gs) == 1:
            o_ref[...] += _dot_tn(a_ref[...].astype(bf16), refs[0][...].astype(bf16))
        else:
            at = a_ref[...].astype(bf16).T
            for g_ref, off in zip(refs[:-1], offs):
                o_ref[:, off:off + g_ref.shape[1]] += _dot(at, g_ref[...].astype(bf16))

    return pl.pallas_call(
        body, grid=(K // tk, T // tt),
        in_specs=[pl.BlockSpec((tt, tk), lambda j, t: (t, j))] + [pl.BlockSpec((tt, p.shape[1]), lambda j, t: (t, 0)) for p in gs],
        out_specs=pl.BlockSpec((tk, N), lambda j, t: (j, 0)), out_shape=_sds((K, N)),
        compiler_params=_cparams(("parallel", "arbitrary")), name=name)(a, *gs)


FFN_TF = 1408


def ffn_fwd(x, wg, wu, wd, g, b, name, rider=None):
    T = x.shape[0]
    tm = min(T, 512)
    nf = F // FFN_TF

    def body(x_ref, wg_ref, wu_ref, wd_ref, g_ref, b_ref, z_ref, y_ref, acc):
        j = pl.program_id(1)

        @pl.when(j == 0)
        def _():
            acc[...] = jnp.zeros_like(acc)

        xb = x_ref[...].astype(bf16)
        h = _dot(xb, wg_ref[...])
        u = _dot(xb, wu_ref[...])
        a = (h * jax.nn.sigmoid(h) * u).astype(bf16)
        acc[...] += _dot(a, wd_ref[...])

        @pl.when(j == nf - 1)
        def _():
            z = ALPHA * x_ref[...] + 0.5 * acc[...]
            z_ref[...] = z
            y_ref[...] = _layer_norm(z, g_ref[...], b_ref[...])

    row = pl.BlockSpec((tm, D), lambda i, j: (i, 0))
    vec = pl.BlockSpec((1, D), lambda i, j: (0, 0))
    body, r_in, r_args, r_out, r_shape, r_scratch = _ride(body, 6, 2, rider, (T // tm, nf), 1)
    res = pl.pallas_call(
        body, grid=(T // tm, nf),
        in_specs=[row, pl.BlockSpec((D, FFN_TF), lambda i, j: (0, j)), pl.BlockSpec((D, FFN_TF), lambda i, j: (0, j)),
                  pl.BlockSpec((FFN_TF, D), lambda i, j: (j, 0)), vec, vec] + r_in,
        out_specs=[row, row] + r_out, out_shape=[_sds((T, D)), _sds((T, D))] + r_shape,
        scratch_shapes=[pltpu.VMEM((tm, D), f32)] + r_scratch,
        compiler_params=_cparams(("arbitrary", "arbitrary")), name=name)(x, wg, wu, wd, g, b, *r_args)
    return res[0], res[1], res[2:]


def ffn_bwd(dz, x, wg, wu, wd, name, rider=None):
    T = x.shape[0]
    tm = min(T, 512)
    nf = F // FFN_TF

    def body(dz_ref, x_ref, wg_ref, wu_ref, wd_ref, dx_ref, a_ref, dh_ref, du_ref, dzh_ref, acc):
        j = pl.program_id(1)

        @pl.when(j == 0)
        def _():
            acc[...] = jnp.zeros_like(acc)

        dzh = (0.5 * dz_ref[...]).astype(bf16)
        xb = x_ref[...].astype(bf16)
        h = _dot(xb, wg_ref[...])
        u = _dot(xb, wu_ref[...])
        s = jax.nn.sigmoid(h)
        hs = h * s
        da = _dot_nt(dzh, wd_ref[...])
        du = (da * hs).astype(bf16)
        dh = (da * u * (s + hs * (1.0 - s))).astype(bf16)
        a_ref[...] = (hs * u).astype(bf16)
        dh_ref[...] = dh
        du_ref[...] = du
        acc[...] += _dot_nt(dh, wg_ref[...]) + _dot_nt(du, wu_ref[...])

        @pl.when(j == nf - 1)
        def _():
            dx_ref[...] = ALPHA * dz_ref[...] + acc[...]
            dzh_ref[...] = dzh

    row = pl.BlockSpec((tm, D), lambda i, j: (i, 0))
    wide = pl.BlockSpec((tm, FFN_TF), lambda i, j: (i, j))
    body, r_in, r_args, r_out, r_shape, r_scratch = _ride(body, 5, 5, rider, (T // tm, nf), 1)
    res = pl.pallas_call(
        body, grid=(T // tm, nf),
        in_specs=[row, row, pl.BlockSpec((D, FFN_TF), lambda i, j: (0, j)), pl.BlockSpec((D, FFN_TF), lambda i, j: (0, j)),
                  pl.BlockSpec((FFN_TF, D), lambda i, j: (j, 0))] + r_in,
        out_specs=[row, wide, wide, wide, row] + r_out,
        out_shape=[_sds((T, D)), _sds((T, F), bf16), _sds((T, F), bf16), _sds((T, F), bf16), _sds((T, D), bf16)] + r_shape,
        scratch_shapes=[pltpu.VMEM((tm, D), f32)] + r_scratch,
        compiler_params=_cparams(("arbitrary", "arbitrary")), name=name)(dz, x, wg, wu, wd, *r_args)
    return res[0], res[1], res[2], res[3], res[4], res[5:]


def lin_res_ln(a, w, res, g, b, name):
    T, K = a.shape
    tm = min(T, 512)

    def body(a_ref, w_ref, res_ref, g_ref, b_ref, z_ref, y_ref):
        z = ALPHA * res_ref[...] + _dot(a_ref[...].astype(bf16), w_ref[...])
        z_ref[...] = z
        y_ref[...] = _layer_norm(z, g_ref[...], b_ref[...])

    row = pl.BlockSpec((tm, D), lambda i: (i, 0))
    vec = pl.BlockSpec((1, D), lambda i: (0, 0))
    return pl.pallas_call(
        body, grid=(T // tm,),
        in_specs=[pl.BlockSpec((tm, K), lambda i: (i, 0)), pl.BlockSpec((K, D), lambda i: (0, 0)), row, vec, vec],
        out_specs=[row, row], out_shape=[_sds((T, D)), _sds((T, D))],
        compiler_params=_cparams(("parallel",)), name=name)(a, w, res, g, b)


def ln_bwd(dy, z, g, name):
    T = z.shape[0]
    tm = min(T, 512)

    def body(dy_ref, z_ref, g_ref, dz_ref, dg_ref, db_ref):
        @pl.when(pl.program_id(0) == 0)
        def _():
            dg_ref[...] = jnp.zeros_like(dg_ref)
            db_ref[...] = jnp.zeros_like(db_ref)

        zv = z_ref[...]
        dy = dy_ref[...]
        mu = jnp.mean(zv, axis=-1, keepdims=True)
        zc = zv - mu
        rstd = lax.rsqrt(jnp.mean(zc * zc, axis=-1, keepdims=True) + LN_EPS)
        xh = zc * rstd
        dxh = dy * g_ref[...]
        m1 = jnp.mean(dxh, axis=-1, keepdims=True)
        m2 = jnp.mean(dxh * xh, axis=-1, keepdims=True)
        dz_ref[...] = rstd * (dxh - m1 - xh * m2)
        dg_ref[...] += jnp.sum(dy * xh, axis=0, keepdims=True)
        db_ref[...] += jnp.sum(dy, axis=0, keepdims=True)

    row = pl.BlockSpec((tm, D), lambda i: (i, 0))
    vec = pl.BlockSpec((1, D), lambda i: (0, 0))
    return pl.pallas_call(
        body, grid=(T // tm,), in_specs=[row, row, vec], out_specs=[row, vec, vec],
        out_shape=[_sds((T, D)), _sds((1, D)), _sds((1, D))],
        compiler_params=_cparams(("arbitrary",)), name=name)(dy, z, g)


def loss_and_grad(y, target, name):
    T = y.shape[0]
    tm = min(T, 512)

    def body(y_ref, t_ref, l_ref, dy_ref):
        @pl.when(pl.program_id(0) == 0)
        def _():
            l_ref[...] = jnp.zeros_like(l_ref)
        d = y_ref[...] - t_ref[...]
        dy_ref[...] = d * (1.0 / D)
        l_ref[...] += (0.5 / D) * jnp.sum(jnp.sum(d * d, axis=1, keepdims=True), axis=0, keepdims=True)

    row = pl.BlockSpec((tm, D), lambda i: (i, 0))
    return pl.pallas_call(
        body, grid=(T // tm,), in_specs=[row, row],
        out_specs=[pl.BlockSpec((1, 128), lambda i: (0, 0)), row],
        out_shape=[_sds((1, 128)), _sds((T, D))],
        compiler_params=_cparams(("arbitrary",)), name=name)(y, target)


def _shifted(ext, s):
    return ext if s == 0 else pltpu.roll(ext, s, 0)


def _halo_maps(tm, P, nblk):
    per = tm // P
    prev = lambda i, c: (jnp.maximum(i * per - 1, 0), c)
    nxt = lambda i, c: (jnp.minimum((i + 1) * per, nblk * per - 1), c)
    return prev, nxt


GDN_P = 8
CONF_P = 32


def gdn_conv_fwd(proj, w8, name):
    T = proj.shape[0]
    tm = min(T, 512)
    nblk = T // tm
    C = 3 * GW
    prev, _ = _halo_maps(tm, GDN_P, nblk)

    def body(xc_ref, xp_ref, w_ref, o_ref):
        i = pl.program_id(0)
        xp = jnp.where(i > 0, xp_ref[...], 0.0)
        ext = jnp.concatenate([xp, xc_ref[...]], axis=0)
        acc = jnp.zeros((tm, C), f32)
        for k in range(GDN_K):
            acc = acc + w_ref[k:k + 1, :] * _shifted(ext, GDN_K - 1 - k)[GDN_P:, :]
        y = acc * jax.nn.sigmoid(acc)
        for h in range(3 * NH):
            o_ref[h] = y[:, h * HD:(h + 1) * HD]

    return pl.pallas_call(
        body, grid=(nblk,),
        in_specs=[pl.BlockSpec((tm, C), lambda i: (i, 0)), pl.BlockSpec((GDN_P, C), lambda i: prev(i, 0)),
                  pl.BlockSpec((8, C), lambda i: (0, 0))],
        out_specs=pl.BlockSpec((3 * NH, tm, HD), lambda i: (0, i, 0)), out_shape=_sds((3 * NH, T, HD)),
        compiler_params=_cparams(("parallel",)), name=name)(proj, proj, w8)


def gdn_conv_bwd(dy, proj, w8, name):
    T = proj.shape[0]
    tm = min(T, 512)
    nblk = T // tm
    C = 3 * GW
    P = GDN_P
    prev, nxt = _halo_maps(tm, P, nblk)

    def body(dyc_ref, dyn_ref, xc_ref, xp_ref, xn_ref, w_ref, dx_ref, dw_ref):
        i = pl.program_id(0)

        @pl.when(i == 0)
        def _():
            dw_ref[...] = jnp.zeros_like(dw_ref)

        xp = jnp.where(i > 0, xp_ref[...], 0.0)
        last = i == nblk - 1
        xn = jnp.where(last, 0.0, xn_ref[...])
        dyn = jnp.where(last, 0.0, jnp.concatenate([dyn_ref[h] for h in range(3 * NH)], axis=1))
        ext = jnp.concatenate([xp, xc_ref[...], xn], axis=0)
        sh = [_shifted(ext, GDN_K - 1 - k)[P:, :] for k in range(GDN_K)]
        s = jnp.zeros((tm + P, C), f32)
        for k in range(GDN_K):
            s = s + w_ref[k:k + 1, :] * sh[k]
        dyc = jnp.concatenate([dyc_ref[h] for h in range(3 * NH)], axis=1)
        ds = jnp.concatenate([dyc, dyn], axis=0) * _dsilu(s)
        dx = jnp.zeros((tm, C), f32)
        for k in range(GDN_K):
            d = GDN_K - 1 - k
            moved = ds if d == 0 else pltpu.roll(ds, tm + P - d, 0)
            dx = dx + w_ref[k:k + 1, :] * moved[:tm, :]
            dw_ref[k:k + 1, :] += jnp.sum(ds[:tm, :] * sh[k][:tm, :], axis=0, keepdims=True)
        dx_ref[...] = dx

    col = lambda i: (i, 0)
    return pl.pallas_call(
        body, grid=(nblk,),
        in_specs=[pl.BlockSpec((3 * NH, tm, HD), lambda i: (0, i, 0)), pl.BlockSpec((3 * NH, P, HD), lambda i: (0, nxt(i, 0)[0], 0)),
                  pl.BlockSpec((tm, C), col), pl.BlockSpec((P, C), lambda i: prev(i, 0)),
                  pl.BlockSpec((P, C), lambda i: nxt(i, 0)), pl.BlockSpec((8, C), lambda i: (0, 0))],
        out_specs=[pl.BlockSpec((tm, C), col), pl.BlockSpec((8, C), lambda i: (0, 0))],
        out_shape=[_sds((T, C)), _sds((8, C))],
        compiler_params=_cparams(("arbitrary",)), name=name)(dy, dy, proj, proj, proj, w8)


def _group_ones():
    r = lax.broadcasted_iota(jnp.int32, (GW, GW), 0) // HD
    c = lax.broadcasted_iota(jnp.int32, (GW, GW), 1) // HD
    return (r == c).astype(bf16)


def _group_mean(x, ones):
    hi, lo = _split_hi_lo(x)
    return (_dot(hi, ones) + _dot(lo, ones)) * (1.0 / HD)


def _conf_norm(c, g, b, ones):
    mu = _group_mean(c, ones)
    cc = c - mu
    rstd = lax.rsqrt(_group_mean(cc * cc, ones) + LN_EPS)
    hn = cc * rstd
    return hn, rstd, hn * g + b


CONF_VAL_BLK = C_CONF // GW
CONF_GATE_BLK = C_CONF // GW + 1


def conf_fwd(proj, w32, bias, ng, nb, name):
    T = proj.shape[0]
    tm = min(T, 512)
    nblk = T // tm
    P = CONF_P
    prev, _ = _halo_maps(tm, P, nblk)

    def body(vc_ref, gc_ref, vp_ref, gp_ref, w_ref, bias_ref, ng_ref, nb_ref, y_ref, c_ref):
        i = pl.program_id(0)
        pc = vc_ref[...] * jax.nn.sigmoid(gc_ref[...])
        pp = jnp.where(i > 0, vp_ref[...] * jax.nn.sigmoid(gp_ref[...]), 0.0)
        ext = jnp.concatenate([pp, pc], axis=0)
        acc = jnp.zeros((tm, GW), f32)
        for k in range(CONF_K):
            acc = acc + w_ref[k:k + 1, :] * _shifted(ext, CONF_K - 1 - k)[P:, :]
        c = acc + bias_ref[...]
        c_ref[...] = c
        _, _, yn = _conf_norm(c, ng_ref[...], nb_ref[...], _group_ones())
        y_ref[...] = yn * jax.nn.sigmoid(yn)

    vec = pl.BlockSpec((1, GW), lambda i: (0, 0))
    return pl.pallas_call(
        body, grid=(nblk,),
        in_specs=[pl.BlockSpec((tm, GW), lambda i: (i, CONF_VAL_BLK)), pl.BlockSpec((tm, GW), lambda i: (i, CONF_GATE_BLK)),
                  pl.BlockSpec((P, GW), lambda i: prev(i, CONF_VAL_BLK)), pl.BlockSpec((P, GW), lambda i: prev(i, CONF_GATE_BLK)),
                  pl.BlockSpec((32, GW), lambda i: (0, 0)), vec, vec, vec],
        out_specs=[pl.BlockSpec((tm, GW), lambda i: (i, 0))] * 2, out_shape=[_sds((T, GW))] * 2,
        compiler_params=_cparams(("parallel",)), name=name)(proj, proj, proj, proj, w32, bias, ng, nb)


def conf_bwd(dy, c, proj, w32, ng, nb, name):
    T = proj.shape[0]
    tm = min(T, 512)
    nblk = T // tm
    P = CONF_P
    prev, nxt = _halo_maps(tm, P, nblk)

    def body(dyc_ref, dyn_ref, cc_ref, cn_ref, vc_ref, gc_ref, vp_ref, gp_ref, w_ref, ng_ref, nb_ref,
             dglu_ref, dw_ref, dbias_ref, dng_ref, dnb_ref):
        i = pl.program_id(0)

        @pl.when(i == 0)
        def _():
            dw_ref[...] = jnp.zeros_like(dw_ref)
            dbias_ref[...] = jnp.zeros_like(dbias_ref)
            dng_ref[...] = jnp.zeros_like(dng_ref)
            dnb_ref[...] = jnp.zeros_like(dnb_ref)

        ones = _group_ones()
        g = ng_ref[...]

        def dc_of(dyv, cv):
            hn, rstd, yn = _conf_norm(cv, g, nb_ref[...], ones)
            dyn_ = dyv * _dsilu(yn)
            dhn = dyn_ * g
            dc = rstd * (dhn - _group_mean(dhn, ones) - hn * _group_mean(dhn * hn, ones))
            return dc, dyn_, hn

        dc_c, dyn_c, hn_c = dc_of(dyc_ref[...], cc_ref[...])
        dc_n, _, _ = dc_of(dyn_ref[...], cn_ref[...])
        dc_n = jnp.where(i == nblk - 1, 0.0, dc_n)
        dng_ref[...] += jnp.sum(dyn_c * hn_c, axis=0, keepdims=True)
        dnb_ref[...] += jnp.sum(dyn_c, axis=0, keepdims=True)
        dbias_ref[...] += jnp.sum(dc_c, axis=0, keepdims=True)

        sig_c = jax.nn.sigmoid(gc_ref[...])
        val_c = vc_ref[...]
        pc = val_c * sig_c
        pp = jnp.where(i > 0, vp_ref[...] * jax.nn.sigmoid(gp_ref[...]), 0.0)
        ext = jnp.concatenate([pp, pc], axis=0)
        dext = jnp.concatenate([dc_c, dc_n], axis=0)
        dp = jnp.zeros((tm, GW), f32)
        for k in range(CONF_K):
            d = CONF_K - 1 - k
            moved = dext if d == 0 else pltpu.roll(dext, tm + P - d, 0)
            dp = dp + w_ref[k:k + 1, :] * moved[:tm, :]
            dw_ref[k:k + 1, :] += jnp.sum(dc_c * _shifted(ext, d)[P:, :], axis=0, keepdims=True)
        dglu_ref[:, 0:GW] = dp * sig_c
        dglu_ref[:, GW:2 * GW] = dp * val_c * sig_c * (1.0 - sig_c)

    vec = pl.BlockSpec((1, GW), lambda i: (0, 0))
    blk = pl.BlockSpec((tm, GW), lambda i: (i, 0))
    return pl.pallas_call(
        body, grid=(nblk,),
        in_specs=[blk, pl.BlockSpec((P, GW), lambda i: nxt(i, 0)), blk, pl.BlockSpec((P, GW), lambda i: nxt(i, 0)),
                  pl.BlockSpec((tm, GW), lambda i: (i, CONF_VAL_BLK)), pl.BlockSpec((tm, GW), lambda i: (i, CONF_GATE_BLK)),
                  pl.BlockSpec((P, GW), lambda i: prev(i, CONF_VAL_BLK)), pl.BlockSpec((P, GW), lambda i: prev(i, CONF_GATE_BLK)),
                  pl.BlockSpec((32, GW), lambda i: (0, 0)), vec, vec],
        out_specs=[pl.BlockSpec((tm, 2 * GW), lambda i: (i, 0)), pl.BlockSpec((32, GW), lambda i: (0, 0)), vec, vec, vec],
        out_shape=[_sds((T, 2 * GW)), _sds((32, GW)), _sds((1, GW)), _sds((1, GW)), _sds((1, GW))],
        compiler_params=_cparams(("arbitrary",)), name=name)(dy, dy, c, c, proj, proj, proj, proj, w32, ng, nb)


def _mm_raw(a, b, ta, tb):
    ca = a.ndim - 2 if ta else a.ndim - 1
    cb = b.ndim - 1 if tb else b.ndim - 2
    batch = ((0,), (0,)) if a.ndim == 3 else ((), ())
    return lax.dot_general(a, b, (((ca,), (cb,)), batch), preferred_element_type=f32)


def _mm_prec(a, b, ta, tb, prec):
    if prec == 1:
        return _mm_raw(a.astype(bf16), b.astype(bf16), ta, tb)
    bh, bl = _split_hi_lo(b)
    if prec == 2:
        ab = a.astype(bf16)
        return _mm_raw(ab, bh, ta, tb) + _mm_raw(ab, bl, ta, tb)
    ah, al = _split_hi_lo(a)
    return _mm_raw(ah, bh, ta, tb) + (_mm_raw(ah, bl, ta, tb) + _mm_raw(al, bh, ta, tb))


@functools.partial(jax.custom_vjp, nondiff_argnums=(2, 3, 4))
def mm(a, b, ta=False, tb=False, prec=1):
    return _mm_prec(a, b, ta, tb, prec)


def _mm_fwd(a, b, ta, tb, prec):
    return _mm_prec(a, b, ta, tb, prec), (a, b)


def _mm_bwd(ta, tb, prec, res, ct):
    a, b = res
    da = _mm_prec(b, ct, tb, True, 1) if ta else _mm_prec(ct, b, False, not tb, 1)
    db = _mm_prec(ct, a, True, ta, 1) if tb else _mm_prec(a, ct, not ta, False, 2 if prec == 2 else 1)
    return da, db


mm.defvjp(_mm_fwd, _mm_bwd)


def _tri_inv_raw(l):
    n = -l
    rr = lax.broadcasted_iota(jnp.int32, l.shape, 1)
    cc = lax.broadcasted_iota(jnp.int32, l.shape, 2)
    p = jnp.where(rr == cc, 1.0, 0.0) + n
    for _ in range(5):
        n = _mm_prec(n, n, False, False, 1)
        p = p + _mm_prec(p, n, False, False, 1)
    return p


@jax.custom_vjp
def tri_inv(l):
    return _tri_inv_raw(l)


def _tri_inv_fwd(l):
    t = _tri_inv_raw(l)
    return t, t


def _tri_inv_bwd(t, ct):
    return (-_mm_prec(_mm_prec(t, ct, True, False, 1), t, False, True, 1),)


tri_inv.defvjp(_tri_inv_fwd, _tri_inv_bwd)


def _gdn_block(S, qs, ks, vs, a, b, z, alog, dtb, ng):
    shp = (NH, CHUNK, CHUNK)
    ii = lax.broadcasted_iota(jnp.int32, shp, 1)
    jj = lax.broadcasted_iota(jnp.int32, shp, 2)
    l_incl = jnp.where(ii >= jj, 1.0, 0.0)
    ys = []
    for c in range(len(qs)):
        q = qs[c] * lax.rsqrt(jnp.sum(qs[c] * qs[c], axis=-1, keepdims=True) + L2_EPS) * (HD ** -0.5)
        k = ks[c] * lax.rsqrt(jnp.sum(ks[c] * ks[c], axis=-1, keepdims=True) + L2_EPS)
        v = vs[c]
        beta = jax.nn.sigmoid(b[c])
        g = -jnp.exp(alog) * _softplus(a[c] + dtb)
        gcb = mm(l_incl, jnp.broadcast_to(g, shp), False, False, 2)
        gcr = jnp.swapaxes(gcb, 1, 2)
        decay = jnp.exp(jnp.where(ii >= jj, gcb - gcr, NEG))
        g_last = jnp.sum(jnp.where(ii == CHUNK - 1, gcb, 0.0), axis=1, keepdims=True)
        eg = jnp.exp(gcb)
        kb = k * beta
        lkk = jnp.where(ii > jj, mm(kb, k, False, True) * decay, 0.0)
        t_inv = tri_inv(lkk)
        u = mm(t_inv, v * beta)
        w = mm(t_inv, kb * eg)
        a_qk = jnp.where(ii >= jj, mm(q, k, False, True) * decay, 0.0)
        q_dec = q * eg
        k_dec = k * jnp.exp(g_last - gcb)
        v_new = u - mm(w, S)
        o = mm(q_dec, S) + mm(a_qk, v_new)
        S = S * jnp.exp(g_last) + mm(k_dec, v_new, True, False)
        y = o * lax.rsqrt(jnp.mean(o * o, axis=-1, keepdims=True) + RMS_EPS) * ng
        ys.append(y * (z[c] * jax.nn.sigmoid(z[c])))
    return S, ys


GDN_CB = 256


def _heads(ref, rows, width, lane0=0):
    return jnp.stack([ref[rows, lane0 + h * width:lane0 + (h + 1) * width] for h in range(NH)])


def _chunk_rows(c):
    return slice(c * CHUNK, (c + 1) * CHUNK)


def _gdn_load(refs, nc):
    q_ref, k_ref, v_ref, z_ref, sm_ref = refs
    hm = lambda r: [r[:, _chunk_rows(c), :] for c in range(nc)]
    return (hm(q_ref), hm(k_ref), hm(v_ref), [_heads(z_ref, _chunk_rows(c), HD) for c in range(nc)],
            [_heads(sm_ref, _chunk_rows(c), 1) for c in range(nc)], [_heads(sm_ref, _chunk_rows(c), 1, NH) for c in range(nc)])


def gdn_fwd(qkv_hm, proj, alog, dtb, ng, name):
    T = proj.shape[0]
    cb = min(T, GDN_CB)
    nc = cb // CHUNK
    nb = T // cb

    def body(q_ref, k_ref, v_ref, z_ref, sm_ref, alog_ref, dtb_ref, ng_ref, y_ref, s_ref, S):
        @pl.when(pl.program_id(0) == 0)
        def _():
            S[...] = jnp.zeros_like(S)
        s_ref[...] = S[...]
        qs, ks, vs, zs, as_, bs = _gdn_load((q_ref, k_ref, v_ref, z_ref, sm_ref), nc)
        s_out, ys = _gdn_block(S[...], qs, ks, vs, as_, bs, zs, alog_ref[...], dtb_ref[...], ng_ref[...])
        S[...] = s_out
        for c in range(nc):
            for h in range(NH):
                y_ref[_chunk_rows(c), h * HD:(h + 1) * HD] = ys[c][h]

    hm = lambda h0: pl.BlockSpec((NH, cb, HD), lambda i: (h0, i, 0))
    par = pl.BlockSpec((NH, 1, 1), lambda i: (0, 0, 0))
    return pl.pallas_call(
        body, grid=(nb,),
        in_specs=[hm(0), hm(1), hm(2), pl.BlockSpec((cb, GW), lambda i: (i, C_GZ // GW)),
                  pl.BlockSpec((cb, 128), lambda i: (i, C_SMALL // 128)), par, par, pl.BlockSpec((1, 1, HD), lambda i: (0, 0, 0))],
        out_specs=[pl.BlockSpec((cb, GW), lambda i: (i, 0)), pl.BlockSpec((None, NH, HD, HD), lambda i: (i, 0, 0, 0))],
        out_shape=[_sds((T, GW)), _sds((nb, NH, HD, HD))],
        scratch_shapes=[pltpu.VMEM((NH, HD, HD), f32)],
        compiler_params=_cparams(("arbitrary",)), name=name)(qkv_hm, qkv_hm, qkv_hm, proj, proj, alog, dtb, ng)


def gdn_bwd(dycat, states, qkv_hm, proj, alog, dtb, ng, name):
    T = proj.shape[0]
    cb = min(T, GDN_CB)
    nc = cb // CHUNK
    nb = T // cb

    def body(dy_ref, s_ref, q_ref, k_ref, v_ref, z_ref, sm_ref, alog_ref, dtb_ref, ng_ref,
             dqkv_ref, dz_ref, dsm_ref, dalog_ref, ddtb_ref, dng_ref, dS):
        @pl.when(pl.program_id(0) == 0)
        def _():
            dS[...] = jnp.zeros_like(dS)
            dalog_ref[...] = jnp.zeros_like(dalog_ref)
            ddtb_ref[...] = jnp.zeros_like(ddtb_ref)
            dng_ref[...] = jnp.zeros_like(dng_ref)

        qs, ks, vs, zs, as_, bs = _gdn_load((q_ref, k_ref, v_ref, z_ref, sm_ref), nc)
        _, vjp = jax.vjp(_gdn_block, s_ref[...], qs, ks, vs, as_, bs, zs, alog_ref[...], dtb_ref[...], ng_ref[...])
        dys = [_heads(dy_ref, _chunk_rows(c), HD) for c in range(nc)]
        d_s, dqs, dks, dvs, das, dbs, dzs, d_alog, d_dtb, d_ng = vjp((dS[...], dys))
        dS[...] = d_s
        dalog_ref[...] += d_alog
        ddtb_ref[...] += d_dtb
        dng_ref[...] += d_ng
        for c in range(nc):
            sl = _chunk_rows(c)
            dqkv_ref[0:NH, sl, :] = dqs[c]
            dqkv_ref[NH:2 * NH, sl, :] = dks[c]
            dqkv_ref[2 * NH:3 * NH, sl, :] = dvs[c]
            for h in range(NH):
                dz_ref[sl, h * HD:(h + 1) * HD] = dzs[c][h]
            dsm_ref[sl, :] = _pack_cols([das[c][h] for h in range(NH)] + [dbs[c][h] for h in range(NH)])

    rev = lambda i: nb - 1 - i
    hm = lambda h0: pl.BlockSpec((NH, cb, HD), lambda i: (h0, rev(i), 0))
    tok = lambda w, cblk: pl.BlockSpec((cb, w), lambda i: (rev(i), cblk))
    par = pl.BlockSpec((NH, 1, 1), lambda i: (0, 0, 0))
    ngs = pl.BlockSpec((1, 1, HD), lambda i: (0, 0, 0))
    res = pl.pallas_call(
        body, grid=(nb,),
        in_specs=[tok(GW, 0), pl.BlockSpec((None, NH, HD, HD), lambda i: (rev(i), 0, 0, 0)),
                  hm(0), hm(1), hm(2), tok(GW, C_GZ // GW), tok(128, C_SMALL // 128), par, par, ngs],
        out_specs=[pl.BlockSpec((3 * NH, cb, HD), lambda i: (0, rev(i), 0)), tok(GW, 0), tok(128, 0), par, par, ngs],
        out_shape=[_sds((3 * NH, T, HD)), _sds((T, GW)), _sds((T, 128))] + [_sds((NH, 1, 1))] * 2 + [_sds((1, 1, HD))],
        scratch_shapes=[pltpu.VMEM((NH, HD, HD), f32)],
        compiler_params=_cparams(("arbitrary",)), name=name)(dycat, states, qkv_hm, qkv_hm, qkv_hm, proj, proj, alog, dtb, ng)
    return res


F_LANE = 8
SCAN_TB = 256


def fox_gate_fwd(proj, bfv, name):
    T = proj.shape[0]
    tb = min(T, SCAN_TB)

    def body(x_ref, b_ref, o_ref, carry):
        @pl.when(pl.program_id(0) == 0)
        def _():
            carry[...] = jnp.zeros_like(carry)
        logf = -_softplus(-(x_ref[...] + b_ref[...]))
        r = lax.broadcasted_iota(jnp.int32, (tb, tb), 0)
        c = lax.broadcasted_iota(jnp.int32, (tb, tb), 1)
        tri = (r >= c).astype(bf16)
        hi, lo = _split_hi_lo(logf)
        cum = _dot(tri, hi) + _dot(tri, lo) + carry[0:1, :]
        o_ref[...] = cum
        carry[0:1, :] = cum[tb - 1:tb, :]

    return pl.pallas_call(
        body, grid=(T // tb,),
        in_specs=[pl.BlockSpec((tb, 128), lambda i: (i, C_SMALL // 128)), pl.BlockSpec((1, 128), lambda i: (0, 0))],
        out_specs=pl.BlockSpec((tb, 128), lambda i: (i, 0)), out_shape=_sds((T, 128)),
        scratch_shapes=[pltpu.VMEM((8, 128), f32)],
        compiler_params=_cparams(("arbitrary",)), name=name)(proj, bfv)


def fox_gate_bwd(dcum, proj, bfv, name):
    T = proj.shape[0]
    tb = min(T, SCAN_TB)
    nb = T // tb

    def body(d_ref, x_ref, b_ref, o_ref, db_ref, carry):
        @pl.when(pl.program_id(0) == 0)
        def _():
            carry[...] = jnp.zeros_like(carry)
            db_ref[...] = jnp.zeros_like(db_ref)
        r = lax.broadcasted_iota(jnp.int32, (tb, tb), 0)
        c = lax.broadcasted_iota(jnp.int32, (tb, tb), 1)
        tri = (c >= r).astype(bf16)
        hi, lo = _split_hi_lo(d_ref[...])
        dlogf = _dot(tri, hi) + _dot(tri, lo) + carry[0:1, :]
        carry[0:1, :] = dlogf[0:1, :]
        lane = lax.broadcasted_iota(jnp.int32, (tb, 128), 1)
        keep = (lane >= F_LANE) & (lane < F_LANE + NH)
        dx = jnp.where(keep, dlogf * jax.nn.sigmoid(-(x_ref[...] + b_ref[...])), 0.0)
        o_ref[...] = dx
        db_ref[...] += jnp.sum(dx, axis=0, keepdims=True)

    rev = lambda i: nb - 1 - i
    return pl.pallas_call(
        body, grid=(nb,),
        in_specs=[pl.BlockSpec((tb, 128), lambda i: (rev(i), 0)), pl.BlockSpec((tb, 128), lambda i: (rev(i), C_SMALL // 128)),
                  pl.BlockSpec((1, 128), lambda i: (0, 0))],
        out_specs=[pl.BlockSpec((tb, 128), lambda i: (rev(i), 0)), pl.BlockSpec((1, 128), lambda i: (0, 0))],
        out_shape=[_sds((T, 128)), _sds((1, 128))],
        scratch_shapes=[pltpu.VMEM((8, 128), f32)],
        compiler_params=_cparams(("arbitrary",)), name=name)(dcum, proj, bfv)


ATT_TQ = 512
ATT_TK = 256


def _lane_col(tile, lane):
    li = lax.broadcasted_iota(jnp.int32, tile.shape, 1)
    return jnp.sum(jnp.where(li == lane, tile, 0.0), axis=1, keepdims=True)


def _pack_cols(cols):
    rows = cols[0].shape[0]
    li = lax.broadcasted_iota(jnp.int32, (rows, 128), 1)
    out = jnp.zeros((rows, 128), f32)
    for h, cv in enumerate(cols):
        out = jnp.where(li == h, cv, out)
    return out


def _head_masks():
    li = lax.broadcasted_iota(jnp.int32, (1, 128), 1)
    return [li < HD, li >= HD]


def _qkv_specs(T, tq, base_blk):
    q = pl.BlockSpec((tq, 128), lambda p, i: (i, base_blk + p))
    k = pl.BlockSpec((T, 128), lambda p, i: (0, base_blk + 2 + p))
    v = pl.BlockSpec((T, 128), lambda p, i: (0, base_blk + 4 + p))
    return q, k, v


def _stack_heads(x, masks):
    return jnp.concatenate([jnp.where(m, x, jnp.zeros_like(x)) for m in masks], axis=0)


def _side_by_side(x, tq):
    return jnp.concatenate([x[:tq], x[tq:]], axis=1)


def _stacked_mask(tq, tk, d, strict):
    r = lax.broadcasted_iota(jnp.int32, (2 * tq, tk), 0)
    r = jnp.where(r >= tq, r - tq, r)
    c = lax.broadcasted_iota(jnp.int32, (2 * tq, tk), 1) + d * tk
    return c < r if strict else c <= r


def _sub_head_rows(s, rows2, tq):
    return jnp.concatenate([s[:tq] - rows2[0:1, :], s[tq:] - rows2[1:2, :]], axis=0)


def _lane_cols2(tile):
    return jnp.concatenate([_lane_col(tile, 0), _lane_col(tile, 1)], axis=0)


def _pack_cols2(col, tq):
    return _pack_cols([col[:tq], col[tq:]])


def _dot_hilo2(x, tri):
    n = x.shape[0]
    hi, lo = _split_hi_lo(x)
    r = _dot(jnp.concatenate([hi, lo], axis=0), tri)
    return r[:n] + r[n:]


def fox_fwd(pbf, cumrow, name):
    T = pbf.shape[0]
    tq, tk = min(T, ATT_TQ), min(T, ATT_TK)
    nq, nk, per = T // tq, T // tk, tq // tk
    scale = HD ** -0.5

    def body(q_ref, k_ref, v_ref, cr_ref, o_ref, lse_ref):
        i = pl.program_id(1)
        masks = _head_masks()
        qs2 = _stack_heads(q_ref[...] * scale, masks)

        def tile(kb, carry, d=None):
            m, l, acc = carry
            off = pl.multiple_of(kb * tk, tk)
            v2 = _stack_heads(v_ref[pl.ds(off, tk), :], masks)
            s = _sub_head_rows(_dot_nt(qs2, k_ref[pl.ds(off, tk), :]), cr_ref[kb], tq)
            if d is not None:
                s = jnp.where(_stacked_mask(tq, tk, d, False), s, NEG)
            m_new = jnp.maximum(m, jnp.max(s, axis=1, keepdims=True))
            corr = jnp.exp(m - m_new)
            p = jnp.exp(s - m_new)
            l = l * corr + jnp.sum(p, axis=1, keepdims=True)
            acc = acc * jnp.where(masks[0], corr[:tq], corr[tq:]) + _dot(_side_by_side(p.astype(bf16), tq), v2)
            return m_new, l, acc

        init = (jnp.full((2 * tq, 1), NEG, f32), jnp.zeros((2 * tq, 1), f32), jnp.zeros((tq, 128), f32))
        carry = lax.fori_loop(0, i * per, tile, init)
        for d in range(per):
            carry = tile(i * per + d, carry, d)
        m, l, acc = carry
        o_ref[...] = acc * jnp.where(masks[0], 1.0 / l[:tq], 1.0 / l[tq:])
        lse_ref[...] = _pack_cols2(m + jnp.log(l), tq)

    qs, ks, vs = _qkv_specs(T, tq, C_FOX // 128)
    return pl.pallas_call(
        body, grid=(2, nq),
        in_specs=[qs, ks, vs, pl.BlockSpec((None, nk, 8, tk), lambda p, i: (p, 0, 0, 0))],
        out_specs=[pl.BlockSpec((tq, 128), lambda p, i: (i, p)), pl.BlockSpec((None, tq, 128), lambda p, i: (p, i, 0))],
        out_shape=[_sds((T, GW)), _sds((2, T, 128))],
        compiler_params=_cparams(("parallel", "parallel")), name=name)(pbf, pbf, pbf, cumrow)


def fox_bwd(do, o, lse, pbf, cumrow, name, rider=None):
    T = pbf.shape[0]
    tq, tk = min(T, ATT_TQ), min(T, ATT_TK)
    nq, nk, per = T // tq, T // tk, tq // tk
    scale = HD ** -0.5

    def body(do_ref, o_ref, lse_ref, q_ref, k_ref, v_ref, cr_ref, dq_ref, dk_ref, dv_ref, dc_ref, dcq_ref):
        i = pl.program_id(1)

        @pl.when(i == 0)
        def _():
            dk_ref[...] = jnp.zeros_like(dk_ref)
            dv_ref[...] = jnp.zeros_like(dv_ref)
            dc_ref[...] = jnp.zeros_like(dc_ref)

        masks = _head_masks()
        dov = do_ref[...]
        qs2 = _stack_heads(q_ref[...] * scale, masks)
        do2 = _stack_heads(dov.astype(bf16), masks)
        prod = dov * o_ref[...]
        delta = jnp.concatenate([jnp.sum(jnp.where(m, prod, 0.0), axis=1, keepdims=True) for m in masks], axis=0)
        lse2 = _lane_cols2(lse_ref[...])

        def tile(kb, carry, d=None):
            dq, rsum = carry
            off = pl.multiple_of(kb * tk, tk)
            kblk = k_ref[pl.ds(off, tk), :]
            p = jnp.exp(_sub_head_rows(_dot_nt(qs2, kblk), cr_ref[kb], tq) - lse2)
            if d is not None:
                p = jnp.where(_stacked_mask(tq, tk, d, False), p, 0.0)
            dp = _dot_nt(do2, v_ref[pl.ds(off, tk), :])
            ds = p * (dp - delta)
            dsb = ds.astype(bf16)
            dq = dq + _dot(_side_by_side(dsb, tq), _stack_heads(kblk, masks))
            dk_ref[pl.ds(off, tk), :] += _dot_tn(dsb, qs2)
            dv_ref[pl.ds(off, tk), :] += _dot_tn(p.astype(bf16), do2)
            dc_ref[kb, 0:1, :] += -jnp.sum(ds[:tq], axis=0, keepdims=True)
            dc_ref[kb, 1:2, :] += -jnp.sum(ds[tq:], axis=0, keepdims=True)
            return dq, rsum + jnp.sum(ds, axis=1, keepdims=True)

        carry = lax.fori_loop(0, i * per, tile, (jnp.zeros((tq, 128), f32), jnp.zeros((2 * tq, 1), f32)))
        for d in range(per):
            carry = tile(i * per + d, carry, d)
        dq, rsum = carry
        dq_ref[...] = dq * scale
        dcq_ref[...] = _pack_cols2(rsum, tq)

    qs, ks, vs = _qkv_specs(T, tq, C_FOX // 128)
    tile_spec = pl.BlockSpec((tq, 128), lambda p, i: (i, p))
    pair = pl.BlockSpec((None, tq, 128), lambda p, i: (p, i, 0))
    rowsp = pl.BlockSpec((None, nk, 8, tk), lambda p, i: (p, 0, 0, 0))
    full = pl.BlockSpec((T, 128), lambda p, i: (0, p))
    body, r_in, r_args, r_out, r_shape, r_scratch = _ride(body, 7, 5, rider, (2, nq))
    res = pl.pallas_call(
        body, grid=(2, nq),
        in_specs=[tile_spec, tile_spec, pair, qs, ks, vs, rowsp] + r_in,
        out_specs=[tile_spec, full, full, rowsp, pair] + r_out,
        out_shape=[_sds((T, GW)), _sds((T, GW)), _sds((T, GW)), _sds((2, nk, 8, tk)), _sds((2, T, 128))] + r_shape,
        scratch_shapes=r_scratch,
        compiler_params=_cparams(("arbitrary", "arbitrary")), name=name)(do, o, lse, pbf, pbf, pbf, cumrow, *r_args)
    return res[0], res[1], res[2], res[3], res[4], res[5:]


def _tri(tq, pred):
    r = lax.broadcasted_iota(jnp.int32, (tq, tq), 0)
    c = lax.broadcasted_iota(jnp.int32, (tq, tq), 1)
    return pred(r, c).astype(bf16)


def _ride(body, n_in, n_out, rider, grid, n_scratch=0):
    if rider is None:
        return body, [], [], [], [], []
    nr = rider.n

    def wrapped(*refs):
        ins, rin = refs[:n_in], refs[n_in:n_in + nr]
        outs = refs[n_in + nr:n_in + nr + n_out]
        rout = refs[n_in + nr + n_out:n_in + 2 * nr + n_out]
        own = refs[n_in + 2 * nr + n_out:n_in + 2 * nr + n_out + n_scratch]
        sems = refs[n_in + 2 * nr + n_out + n_scratch:]
        ids = [pl.program_id(a) for a in range(len(grid))]
        first = functools.reduce(jnp.logical_and, [i == 0 for i in ids])
        last = functools.reduce(jnp.logical_and, [i == g - 1 for i, g in zip(ids, grid)])

        @pl.when(first)
        def _():
            rider.start(rin, rout, sems)

        body(*ins, *outs, *own)

        @pl.when(last)
        def _():
            rider.finish(rin, rout, sems)

    anyspec = pl.BlockSpec(memory_space=pl.ANY)
    return wrapped, [anyspec] * nr, list(rider.arrays), [anyspec] * nr, list(rider.out_shape), list(rider.scratch)


def sb_fwd(pbf, name, rider=None):
    T = pbf.shape[0]
    tq, tk = min(T, ATT_TQ), min(T, ATT_TK)
    nq, per = T // tq, tq // tk
    scale = HD ** -0.5

    def body(q_ref, k_ref, v_ref, o_ref, tot_ref):
        i = pl.program_id(1)
        masks = _head_masks()
        qs2 = _stack_heads(q_ref[...] * scale, masks)
        after = _tri(tk, lambda r, c: r > c)

        def tile(kb, carry, d=None):
            rs, acc = carry
            off = pl.multiple_of(kb * tk, tk)
            z = _dot_nt(qs2, k_ref[pl.ds(off, tk), :])
            lk = _neg_softplus(z)
            if d is not None:
                lk = jnp.where(_stacked_mask(tq, tk, d, True), lk, 0.0)
            w = jnp.exp(z + lk + (_dot_hilo2(lk, after) + rs))
            if d is not None:
                w = jnp.where(_stacked_mask(tq, tk, d, True), w, 0.0)
            acc = acc + _dot(_side_by_side(w.astype(bf16), tq), _stack_heads(v_ref[pl.ds(off, tk), :], masks))
            return rs + jnp.sum(lk, axis=1, keepdims=True), acc

        carry = (jnp.zeros((2 * tq, 1), f32), jnp.zeros((tq, 128), f32))
        for d in reversed(range(per)):
            carry = tile(i * per + d, carry, d)
        rs, acc = lax.fori_loop(0, i * per, lambda n, c: tile(i * per - 1 - n, c), carry)
        o_ref[...] = acc
        tot_ref[...] = _pack_cols2(rs, tq)

    qs, ks, vs = _qkv_specs(T, tq, C_SB // 128)
    body, r_in, r_args, r_out, r_shape, r_scratch = _ride(body, 3, 2, rider, (2, nq))
    res = pl.pallas_call(
        body, grid=(2, nq), in_specs=[qs, ks, vs] + r_in,
        out_specs=[pl.BlockSpec((tq, 128), lambda p, i: (i, p)), pl.BlockSpec((None, tq, 128), lambda p, i: (p, i, 0))] + r_out,
        out_shape=[_sds((T, GW)), _sds((2, T, 128))] + r_shape, scratch_shapes=r_scratch,
        compiler_params=_cparams(("arbitrary", "arbitrary")), name=name)(pbf, pbf, pbf, *r_args)
    return res[0], res[1], res[2:]


def sb_bwd(do, tot, pbf, name, rider=None):
    T = pbf.shape[0]
    tq, tk = min(T, ATT_TQ), min(T, ATT_TK)
    nq, per = T // tq, tq // tk
    scale = HD ** -0.5

    def body(do_ref, tot_ref, q_ref, k_ref, v_ref, dq_ref, dk_ref, dv_ref):
        i = pl.program_id(1)

        @pl.when(i == 0)
        def _():
            dk_ref[...] = jnp.zeros_like(dk_ref)
            dv_ref[...] = jnp.zeros_like(dv_ref)

        masks = _head_masks()
        qs2 = _stack_heads(q_ref[...] * scale, masks)
        do2 = _stack_heads(do_ref[...].astype(bf16), masks)
        tot2 = _lane_cols2(tot_ref[...])
        upto = _tri(tk, lambda r, c: r <= c)
        before = _tri(tk, lambda r, c: r < c)

        def tile(kb, carry, d=None):
            pre, cg, dq = carry
            off = pl.multiple_of(kb * tk, tk)
            kblk = k_ref[pl.ds(off, tk), :]
            z = _dot_nt(qs2, kblk)
            lk = _neg_softplus(z)
            keep = jnp.exp(lk)
            if d is not None:
                lk = jnp.where(_stacked_mask(tq, tk, d, True), lk, 0.0)
            w = jnp.exp(z + lk + (tot2 - (pre + _dot_hilo2(lk, upto))))
            if d is not None:
                w = jnp.where(_stacked_mask(tq, tk, d, True), w, 0.0)
            gmat = w * _dot_nt(do2, v_ref[pl.ds(off, tk), :])
            cmat = cg + _dot(gmat.astype(bf16), before)
            dz = gmat * keep - cmat * (1.0 - keep)
            if d is not None:
                dz = jnp.where(_stacked_mask(tq, tk, d, True), dz, 0.0)
            dzb = dz.astype(bf16)
            dq = dq + _dot(_side_by_side(dzb, tq), _stack_heads(kblk, masks))
            dk_ref[pl.ds(off, tk), :] += _dot_tn(dzb, qs2)
            dv_ref[pl.ds(off, tk), :] += _dot_tn(w.astype(bf16), do2)
            return pre + jnp.sum(lk, axis=1, keepdims=True), cg + jnp.sum(gmat, axis=1, keepdims=True), dq

        zc = jnp.zeros((2 * tq, 1), f32)
        carry = lax.fori_loop(0, i * per, tile, (zc, zc, jnp.zeros((tq, 128), f32)))
        for d in range(per):
            carry = tile(i * per + d, carry, d)
        dq_ref[...] = carry[2] * scale

    qs, ks, vs = _qkv_specs(T, tq, C_SB // 128)
    tile_spec = pl.BlockSpec((tq, 128), lambda p, i: (i, p))
    pair = pl.BlockSpec((None, tq, 128), lambda p, i: (p, i, 0))
    full = pl.BlockSpec((T, 128), lambda p, i: (0, p))
    body, r_in, r_args, r_out, r_shape, r_scratch = _ride(body, 5, 3, rider, (2, nq))
    res = pl.pallas_call(
        body, grid=(2, nq), in_specs=[tile_spec, pair, qs, ks, vs] + r_in,
        out_specs=[tile_spec, full, full] + r_out, out_shape=[_sds((T, GW))] * 3 + r_shape, scratch_shapes=r_scratch,
        compiler_params=_cparams(("arbitrary", "arbitrary")), name=name)(do, tot, pbf, pbf, pbf, *r_args)
    return res[0], res[1], res[2], res[3:]


MEM_HD = D // 4


def mem_fwd(q, kv, name):
    T = q.shape[0]
    M = kv.shape[0]
    tm = min(T, 512)
    scale = MEM_HD ** -0.5

    def body(q_ref, kv_ref, o_ref):
        for h in range(4):
            sl = slice(h * MEM_HD, (h + 1) * MEM_HD)
            kh = kv_ref[:, sl].astype(bf16)
            vh = kv_ref[:, D + h * MEM_HD:D + (h + 1) * MEM_HD].astype(bf16)
            s = _dot_nt(q_ref[:, sl], kh) * scale
            e = jnp.exp(s - jnp.max(s, axis=1, keepdims=True))
            p = e / jnp.sum(e, axis=1, keepdims=True)
            o_ref[:, sl] = _dot(p.astype(bf16), vh).astype(bf16)

    return pl.pallas_call(
        body, grid=(T // tm,),
        in_specs=[pl.BlockSpec((tm, D), lambda i: (i, 0)), pl.BlockSpec((M, 2 * D), lambda i: (0, 0))],
        out_specs=pl.BlockSpec((tm, D), lambda i: (i, 0)), out_shape=_sds((T, D), bf16),
        compiler_params=_cparams(("parallel",)), name=name)(q, kv)


def mem_bwd(do, q, kv, name):
    T = q.shape[0]
    M = kv.shape[0]
    tm = min(T, 512)
    scale = MEM_HD ** -0.5

    def body(do_ref, q_ref, kv_ref, dq_ref, dkv_ref):
        @pl.when(pl.program_id(0) == 0)
        def _():
            dkv_ref[...] = jnp.zeros_like(dkv_ref)
        for h in range(4):
            sl = slice(h * MEM_HD, (h + 1) * MEM_HD)
            vsl = slice(D + h * MEM_HD, D + (h + 1) * MEM_HD)
            qh = q_ref[:, sl]
            kh = kv_ref[:, sl].astype(bf16)
            vh = kv_ref[:, vsl].astype(bf16)
            doh = do_ref[:, sl].astype(bf16)
            s = _dot_nt(qh, kh) * scale
            e = jnp.exp(s - jnp.max(s, axis=1, keepdims=True))
            p = e / jnp.sum(e, axis=1, keepdims=True)
            dp = _dot_nt(doh, vh)
            ds = p * (dp - jnp.sum(dp * p, axis=1, keepdims=True))
            dsb = ds.astype(bf16)
            dq_ref[:, sl] = _dot(dsb, kh) * scale
            dkv_ref[:, sl] += _dot_tn(dsb, qh) * scale
            dkv_ref[:, vsl] += _dot_tn(p.astype(bf16), doh)

    row = pl.BlockSpec((tm, D), lambda i: (i, 0))
    whole = pl.BlockSpec((M, 2 * D), lambda i: (0, 0))
    return pl.pallas_call(
        body, grid=(T // tm,), in_specs=[row, row, whole], out_specs=[row, whole],
        out_shape=[_sds((T, D)), _sds((M, 2 * D))],
        compiler_params=_cparams(("arbitrary",)), name=name)(do, q, kv)


def _chip_peers():
    x, y, c = lax.axis_index("x"), lax.axis_index("y"), lax.axis_index("c")
    me = 2 * x + y
    peers = [((1 - x, y, c), 2 * (1 - x) + y), ((x, 1 - y, c), 2 * x + (1 - y)), ((1 - x, 1 - y, c), 2 * (1 - x) + (1 - y))]
    return me, peers


class Rider:
    def __init__(self, arrays, out_shape, scratch, start, finish):
        self.arrays, self.out_shape, self.scratch, self.start, self.finish = arrays, out_shape, scratch, start, finish
        self.n = len(arrays)

    def split(self, refs):
        return refs[:self.n], refs[self.n:2 * self.n], refs[2 * self.n:]


def run_rider(rider, name):
    def body(*refs):
        parts = rider.split(refs)
        rider.start(*parts)
        rider.finish(*parts)

    anyspec = pl.BlockSpec(memory_space=pl.ANY)
    return pl.pallas_call(
        body, in_specs=[anyspec] * rider.n, out_specs=[anyspec] * rider.n, out_shape=rider.out_shape,
        scratch_shapes=rider.scratch, name=name)(*rider.arrays)


def gather_rider(arrs, split):
    n = len(arrs)

    def copies(ins, outs, sems):
        send_sems, recv_sems, pass_send, pass_recv, loc_sems = sems
        x, y, c = lax.axis_index("x"), lax.axis_index("y"), lax.axis_index("c")
        me, peers = _chip_peers()

        def mine(ref, i):
            if not split[i]:
                return ref
            r = arrs[i].shape[1] // 2
            return ref.at[:, pl.ds(c * r, r), :]

        local, fetch, passed = [], [], []
        for i in range(n):
            local.append(pltpu.make_async_copy(ins[i], outs[i].at[me], loc_sems.at[i]))
            for k, (dev, pj) in enumerate(peers):
                fetch.append(pltpu.make_async_remote_copy(src_ref=mine(ins[i], i), dst_ref=mine(outs[i].at[me], i),
                                                          send_sem=send_sems.at[i, k], recv_sem=recv_sems.at[i, k],
                                                          device_id=dev, device_id_type=MESH))
                rows = mine(outs[i].at[pj], i)
                passed.append(pltpu.make_async_remote_copy(src_ref=rows, dst_ref=rows, send_sem=pass_send.at[i, k],
                                                           recv_sem=pass_recv.at[i, k], device_id=(x, y, 1 - c),
                                                           device_id_type=MESH) if split[i] else None)
        return local, fetch, passed

    def start(ins, outs, sems):
        local, fetch, _ = copies(ins, outs, sems)
        for cp in local + fetch:
            cp.start()

    def finish(ins, outs, sems):
        local, fetch, passed = copies(ins, outs, sems)
        for cp, fw in zip(fetch, passed):
            cp.wait_recv()
            if fw is not None:
                fw.start()
        for cp in fetch:
            cp.wait_send()
        for cp in [fw for fw in passed if fw is not None] + local:
            cp.wait()

    sem = pltpu.SemaphoreType.DMA((n, 3))
    return Rider(list(arrs), [_sds((4,) + a.shape, a.dtype) for a in arrs],
                 [sem, sem, sem, sem, pltpu.SemaphoreType.DMA((n,))], start, finish)


def exchange_rider(stacks):
    n = len(stacks)

    def copies(ins, outs, sems):
        send_sems, recv_sems, loc_sems = sems
        me, peers = _chip_peers()
        out = []
        for i in range(n):
            out.append(pltpu.make_async_copy(ins[i].at[me], outs[i].at[me], loc_sems.at[i]))
            for k, (dev, pj) in enumerate(peers):
                out.append(pltpu.make_async_remote_copy(src_ref=ins[i].at[pj], dst_ref=outs[i].at[me], send_sem=send_sems.at[i, k],
                                                        recv_sem=recv_sems.at[i, k], device_id=dev, device_id_type=MESH))
        return out

    def start(ins, outs, sems):
        for cp in copies(ins, outs, sems):
            cp.start()

    def finish(ins, outs, sems):
        for cp in copies(ins, outs, sems):
            cp.wait()

    return Rider(list(stacks), [_sds(a.shape, a.dtype) for a in stacks],
                 [pltpu.SemaphoreType.DMA((n, 3)), pltpu.SemaphoreType.DMA((n, 3)), pltpu.SemaphoreType.DMA((n,))], start, finish)


def swap_sibling(arrs, name):
    n = len(arrs)

    def body(*refs):
        ins, outs = refs[:n], refs[n:2 * n]
        send_sems, recv_sems = refs[2 * n:]
        x, y, c = lax.axis_index("x"), lax.axis_index("y"), lax.axis_index("c")
        started = []
        for i in range(n):
            cp = pltpu.make_async_remote_copy(src_ref=ins[i], dst_ref=outs[i], send_sem=send_sems.at[i],
                                              recv_sem=recv_sems.at[i], device_id=(x, y, 1 - c), device_id_type=MESH)
            cp.start()
            started.append(cp)
        for cp in started:
            cp.wait()

    anyspec = pl.BlockSpec(memory_space=pl.ANY)
    return pl.pallas_call(
        body, in_specs=[anyspec] * n, out_specs=[anyspec] * n,
        out_shape=[_sds(a.shape, a.dtype) for a in arrs],
        scratch_shapes=[pltpu.SemaphoreType.DMA((n,)), pltpu.SemaphoreType.DMA((n,))],
        name=name)(*arrs)


def gather_all(a, name):
    def body(a_ref, o_ref, send_sems, recv_sems, loc_sem):
        x, y, c = lax.axis_index("x"), lax.axis_index("y"), lax.axis_index("c")
        me = 4 * x + 2 * y + c
        loc = pltpu.make_async_copy(a_ref, o_ref.at[me], loc_sem)
        loc.start()
        started = [loc]
        for k in range(1, 8):
            dev = (x ^ (k >> 2), y ^ ((k >> 1) & 1), c ^ (k & 1))
            cp = pltpu.make_async_remote_copy(src_ref=a_ref, dst_ref=o_ref.at[me], send_sem=send_sems.at[k - 1],
                                              recv_sem=recv_sems.at[k - 1], device_id=dev, device_id_type=MESH)
            cp.start()
            started.append(cp)
        for cp in started:
            cp.wait()

    anyspec = pl.BlockSpec(memory_space=pl.ANY)
    return pl.pallas_call(
        body, in_specs=[anyspec], out_specs=anyspec, out_shape=_sds((8,) + a.shape, a.dtype),
        scratch_shapes=[pltpu.SemaphoreType.DMA((7,)), pltpu.SemaphoreType.DMA((7,)), pltpu.SemaphoreType.DMA(())],
        name=name)(a)


def sum_slots(stack, name):
    n, R, C = stack.shape
    tr = R if R <= 512 else _pick(R, (512, 352, 256))

    def body(s_ref, o_ref):
        acc = s_ref[0].astype(f32)
        for j in range(1, n):
            acc = acc + s_ref[j].astype(f32)
        o_ref[...] = acc

    return pl.pallas_call(
        body, grid=(R // tr,), in_specs=[pl.BlockSpec((n, tr, C), lambda i: (0, i, 0))],
        out_specs=pl.BlockSpec((tr, C), lambda i: (i, 0)), out_shape=_sds((R, C)),
        compiler_params=_cparams(("parallel",)), name=name)(stack)


def adamw(w, g1, g2, m, v, name):
    R, C = w.shape
    tr = R if R <= 512 else _pick(R, (512, 352, 256))
    c1 = 1.0 - ADAM_B1 ** ADAM_STEP
    c2 = 1.0 - ADAM_B2 ** ADAM_STEP

    def body(*refs):
        if g2 is None:
            w_ref, g1_ref, m_ref, v_ref, g_out, d_out, m_out, v_out = refs
            g = g1_ref[...]
        else:
            w_ref, g1_ref, g2_ref, m_ref, v_ref, g_out, d_out, m_out, v_out = refs
            g = g1_ref[...] + g2_ref[...]
        mn = ADAM_B1 * m_ref[...] + (1.0 - ADAM_B1) * g
        vn = ADAM_B2 * v_ref[...] + (1.0 - ADAM_B2) * (g * g)
        g_out[...] = g
        m_out[...] = mn
        v_out[...] = vn
        d_out[...] = -ADAM_LR * ((mn / c1) / (jnp.sqrt(vn / c2) + ADAM_EPS) + ADAM_WD * w_ref[...])

    blk = pl.BlockSpec((tr, C), lambda i: (i, 0))
    args = [w, g1] + ([] if g2 is None else [g2]) + [m, v]
    return pl.pallas_call(
        body, grid=(R // tr,), in_specs=[blk] * len(args), out_specs=[blk] * 4, out_shape=[_sds((R, C))] * 4,
        compiler_params=_cparams(("parallel",)), name=name)(*args)


IN_SPLITS = (768, 256, 4, 4, 768, 4, 512, 768)
IN_OFF = (0, 768, 1024, 1028, 1032, 1800, 1804, 2316, 3084)


def regroup_w_in(w):
    seg = lambda i: w[:, IN_OFF[i]:IN_OFF[i + 1]]
    pad = jnp.zeros((w.shape[0], PW - C_SMALL - 12), w.dtype)
    return jnp.concatenate([seg(0), seg(1), seg(4), seg(6), seg(7), seg(2), seg(3), seg(5), pad], axis=1)


def ungroup_w_in(g):
    s = C_SMALL
    return jnp.concatenate([g[:, 0:1024], g[:, s:s + 8], g[:, 1024:1792], g[:, s + 8:s + 12], g[:, 1792:3072]], axis=1)


def to_hm(t, nh):
    T = t.shape[0]
    return t.reshape(T, nh, HD).transpose(1, 0, 2)


def from_hm(t):
    nh, T, _ = t.shape
    return t.transpose(1, 0, 2).reshape(T, nh * HD)


def col_shards(w):
    c = w.shape[-1] // 4
    return jnp.moveaxis(w.reshape(w.shape[:-1] + (4, c)), -2, 0)


def row_shards(w):
    L, r4, c = w.shape
    return w.reshape(L, 4, r4 // 4, c).transpose(1, 0, 2, 3)


def join_cols(g):
    return jnp.moveaxis(g, 0, -2).reshape(g.shape[1:-1] + (4 * g.shape[-1],))


def join_rows(g):
    _, L, r, c = g.shape
    return g.transpose(1, 0, 2, 3).reshape(L, 4 * r, c)


COL_SHARDED = ("ffn1_w_gate", "ffn1_w_up", "w_in", "gdn_conv_w", "conf_dw_w", "mem_w_kv", "ffn2_w_gate", "ffn2_w_up")
CONV_WEIGHTS = ("gdn_conv_w", "conf_dw_w")
ROW_SHARDED = ("ffn1_w_down", "w_out", "mem_w_q", "mem_w_o", "ffn2_w_down")
REPLICATED = ("ln_ffn1_g", "ln_ffn1_b", "gdn_a_log", "gdn_dt_bias", "gdn_norm_g", "fox_b_f", "conf_dw_b", "conf_norm_g",
              "conf_norm_b", "ln_mix_g", "ln_mix_b", "ln_mem_g", "ln_mem_b", "ln_ffn2_g", "ln_ffn2_b")
WEIGHTS = ("ffn1_w_gate", "ffn1_w_up", "ffn1_w_down", "ln_ffn1_g", "ln_ffn1_b", "w_in", "gdn_conv_w", "gdn_a_log",
           "gdn_dt_bias", "gdn_norm_g", "fox_b_f", "conf_dw_w", "conf_dw_b", "conf_norm_g", "conf_norm_b", "w_out",
           "ln_mix_g", "ln_mix_b", "mem_w_q", "mem_w_kv", "mem_w_o", "ln_mem_g", "ln_mem_b", "ffn2_w_gate",
           "ffn2_w_up", "ffn2_w_down", "ln_ffn2_g", "ln_ffn2_b")


def pack_small(d):
    flat = jnp.concatenate([d[n].reshape(-1) for n in REPLICATED])
    rows = -(-flat.shape[0] // 1024) * 8
    return jnp.pad(flat, (0, rows * 128 - flat.shape[0])).reshape(rows, 128)


def unpack_small(p, like):
    flat = p.reshape(-1)
    out, o = {}, 0
    for n in REPLICATED:
        sz = like[n].size
        out[n] = flat[o:o + sz].reshape(like[n].shape)
        o += sz
    return out


def _vec(v):
    return v.reshape(1, -1)


def _pad_rows(w, rows):
    return jnp.pad(w, ((0, rows - w.shape[0]), (0, 0)))


def _small_lane_vec(v4, lane0):
    return jnp.pad(v4.reshape(1, -1), ((0, 0), (lane0, 128 - lane0 - v4.shape[0])))


def layer_fwd(x0, mem, W, li, rider=None, ffn1_rider=None, complete=None):
    T = x0.shape[0]
    tk = min(T, ATT_TK)
    n = lambda s: f"l{li}_{s}"
    R = {"x0": x0}
    R["z1"], x1, got = ffn_fwd(x0, W["ffn1_w_gate"], W["ffn1_w_up"], W["ffn1_w_down"], _vec(W["ln_ffn1_g"]), _vec(W["ln_ffn1_b"]),
                               n("ffn1_fwd"), ffn1_rider)
    if ffn1_rider is not None:
        W = complete(W, got)
    R["x1"] = x1
    proj, pbf = mm_nn(x1, W["w_in_r"], n("proj"), also_bf16=True)
    R["proj"], R["pbf"] = proj, pbf

    w8 = _pad_rows(W["gdn_conv_w"], 8)
    qkv_hm = gdn_conv_fwd(proj, w8, n("gdn_conv_fwd"))
    alog = W["gdn_a_log"].reshape(NH, 1, 1)
    dtb = W["gdn_dt_bias"].reshape(NH, 1, 1)
    ng = W["gdn_norm_g"].reshape(1, 1, HD)
    ya, states = gdn_fwd(qkv_hm, proj, alog, dtb, ng, n("gdn_fwd"))
    R.update(qkv_hm=qkv_hm, states=states)

    bfv = _small_lane_vec(W["fox_b_f"], F_LANE)
    cum = fox_gate_fwd(proj, bfv, n("fox_gate_fwd"))
    cum4 = cum[:, F_LANE:F_LANE + NH]
    cumrow = jnp.pad(cum4.T.reshape(2, 2, T // tk, tk).transpose(0, 2, 1, 3), ((0, 0), (0, 0), (0, 6), (0, 0)))
    yb, lse = fox_fwd(pbf, cumrow, n("fox_fwd"))
    R.update(cumrow=cumrow, yb=yb, lse=lse)

    w32 = _pad_rows(W["conf_dw_w"], 32)
    yc, cc = conf_fwd(proj, w32, _vec(W["conf_dw_b"]), _vec(W["conf_norm_g"]), _vec(W["conf_norm_b"]), n("conf_fwd"))
    R["cc"] = cc

    yd, tot, rider_out = sb_fwd(pbf, n("sb_fwd"), rider)
    R["tot"] = tot

    ycat = jnp.concatenate([ya, yb, yc, yd], axis=1).astype(bf16)
    R["ycat"] = ycat
    R["z2"], x2 = lin_res_ln(ycat, W["w_out"], x1, _vec(W["ln_mix_g"]), _vec(W["ln_mix_b"]), n("mix_out"))
    R["x2"] = x2

    qm = mm_nn(x2, W["mem_w_q"], n("mem_q"), out_dtype=bf16)
    kv = mm_nn(mem, W["mem_w_kv"], n("mem_kv"))
    om = mem_fwd(qm, kv, n("mem_fwd"))
    R.update(qm=qm, kv=kv, om=om)
    R["z3"], x3 = lin_res_ln(om, W["mem_w_o"], x2, _vec(W["ln_mem_g"]), _vec(W["ln_mem_b"]), n("mem_out"))
    R["x3"] = x3
    R["z4"], x4, _ = ffn_fwd(x3, W["ffn2_w_gate"], W["ffn2_w_up"], W["ffn2_w_down"], _vec(W["ln_ffn2_g"]), _vec(W["ln_ffn2_b"]), n("ffn2_fwd"))
    return x4, R, rider_out, W


EARLY_GRADS = ("ffn2_w_gate", "ffn2_w_up", "ffn2_w_down", "mem_w_q", "mem_w_kv", "mem_w_o", "w_out")
LATE_GRADS = ("w_in", "gdn_conv_w", "conf_dw_w")
LAST_GRADS = ("ffn1_w_gate", "ffn1_w_up", "ffn1_w_down")


def grad_slots(G, names):
    out = []
    for k in names:
        s = col_shards(G[k][None]) if k in COL_SHARDED else row_shards(G[k][None])
        out.append(s if k in CONV_WEIGHTS else s.astype(bf16))
    return out


def layer_bwd(dx4, mem, W, R, li, rider=None, exchange=True):
    T = dx4.shape[0]
    n = lambda s: f"l{li}_{s}"
    G = {}

    def ffn_back(dy, z, x, pre, tag, ride=None):
        dz, dg, db = ln_bwd(dy, z, _vec(W[f"ln_{pre}_g"]), n(f"{tag}_ln_bwd"))
        dx, a, dh, du, dzh, got = ffn_bwd(dz, x, W[f"{pre}_w_gate"], W[f"{pre}_w_up"], W[f"{pre}_w_down"], n(f"{tag}_bwd"), ride)
        G[f"{pre}_w_gate"] = mm_tn(x, dh, n(f"{tag}_dwg"))
        G[f"{pre}_w_up"] = mm_tn(x, du, n(f"{tag}_dwu"))
        G[f"{pre}_w_down"] = mm_tn(a, dzh, n(f"{tag}_dwd"))
        G[f"ln_{pre}_g"], G[f"ln_{pre}_b"] = dg.reshape(-1), db.reshape(-1)
        return dx, got

    dx3, _ = ffn_back(dx4, R["z4"], R["x3"], "ffn2", "ffn2")

    dz3, dg, db = ln_bwd(dx3, R["z3"], _vec(W["ln_mem_g"]), n("mem_ln_bwd"))
    G["ln_mem_g"], G["ln_mem_b"] = dg.reshape(-1), db.reshape(-1)
    dom = mm_nt(dz3, W["mem_w_o"], n("mem_dom"))
    G["mem_w_o"] = mm_tn(R["om"], dz3, n("mem_dwo"))
    dqm, dkv = mem_bwd(dom, R["qm"], R["kv"], n("mem_bwd"))
    G["mem_w_q"] = mm_tn(R["x2"], dqm, n("mem_dwq"))
    G["mem_w_kv"] = mm_tn(mem, dkv, n("mem_dwkv"))
    dx2 = mm_nt(dqm, W["mem_w_q"], n("mem_dx"), add=dz3, add_scale=ALPHA)

    dz2, dg, db = ln_bwd(dx2, R["z2"], _vec(W["ln_mix_g"]), n("mix_ln_bwd"))
    G["ln_mix_g"], G["ln_mix_b"] = dg.reshape(-1), db.reshape(-1)
    dycat = mm_nt(dz2, W["w_out"], n("mix_dycat"))
    G["w_out"] = mm_tn(R["ycat"], dz2, n("mix_dwout"))
    dya, dyb, dyc, dyd = (dycat[:, i * GW:(i + 1) * GW] for i in range(4))
    proj, pbf = R["proj"], R["pbf"]

    alog = W["gdn_a_log"].reshape(NH, 1, 1)
    dtb = W["gdn_dt_bias"].reshape(NH, 1, 1)
    ng = W["gdn_norm_g"].reshape(1, 1, HD)
    dqkv_hm, dgz, dsmall_ab, dalog, ddtb, dng = gdn_bwd(dycat, R["states"], R["qkv_hm"], proj, alog, dtb, ng, n("gdn_bwd"))
    G["gdn_a_log"], G["gdn_dt_bias"], G["gdn_norm_g"] = dalog.reshape(-1), ddtb.reshape(-1), dng.reshape(-1)
    w8 = _pad_rows(W["gdn_conv_w"], 8)
    dgqkv, dw8 = gdn_conv_bwd(dqkv_hm, proj, w8, n("gdn_conv_bwd"))
    G["gdn_conv_w"] = dw8[:GDN_K]

    dfq, dfk, dfv, dcumrow, dcumq, got_early = fox_bwd(dyb, R["yb"], R["lse"], pbf, R["cumrow"], n("fox_bwd"),
                                                      exchange_rider(grad_slots(G, EARLY_GRADS)) if exchange else None)
    dcum4 = dcumrow[:, :, 0:2, :].transpose(0, 2, 1, 3).reshape(4, T).T
    dcum4 = dcum4 + dcumq[:, :, 0:2].transpose(1, 0, 2).reshape(T, 4)
    dcum = jnp.pad(dcum4, ((0, 0), (F_LANE, 128 - F_LANE - NH)))
    bfv = _small_lane_vec(W["fox_b_f"], F_LANE)
    dsmall_f, dbf = fox_gate_bwd(dcum, proj, bfv, n("fox_gate_bwd"))
    G["fox_b_f"] = dbf[0, F_LANE:F_LANE + NH]

    w32 = _pad_rows(W["conf_dw_w"], 32)
    dglu, dw32, dcb, dcg, dcbeta = conf_bwd(dyc, R["cc"], proj, w32, _vec(W["conf_norm_g"]), _vec(W["conf_norm_b"]), n("conf_bwd"))
    G["conf_dw_w"], G["conf_dw_b"] = dw32[:CONF_K], dcb.reshape(-1)
    G["conf_norm_g"], G["conf_norm_b"] = dcg.reshape(-1), dcbeta.reshape(-1)

    dsq, dsk, dsv, got_carried = sb_bwd(dyd, R["tot"], pbf, n("sb_bwd"), rider)

    dsmall = dsmall_ab + dsmall_f
    dproj = [dgqkv, dgz, dfq, dfk, dfv, dglu, dsq, dsk, dsv, dsmall]
    G["w_in"] = ungroup_w_in(mm_tn(R["x1"], dproj, n("proj_dw")))
    dx1 = mm_nt(dproj, W["w_in_r"], n("proj_dx"), add=dz2, add_scale=ALPHA)

    dx0, got_late = ffn_back(dx1, R["z1"], R["x0"], "ffn1", "ffn1", exchange_rider(grad_slots(G, LATE_GRADS)) if exchange else None)
    return dx0, G, {"carried": got_carried, "early": got_early, "late": got_late}


def _step(P, M, V, x, mem, loss_target):
    xs, mems, tgt = x[0], mem[0], loss_target[0]

    big = COL_SHARDED + ROW_SHARDED

    def weight_gather(li, names):
        return gather_rider([P[k][li:li + 1] if k in CONV_WEIGHTS else P[k][li:li + 1].astype(bf16) for k in names],
                            [k not in CONV_WEIGHTS for k in names])

    def layer_weights(li, names, gathered, W=None):
        W = dict(W) if W else {k: P[k][li] for k in REPLICATED}
        W.update({k: (join_cols(g) if k in COL_SHARDED else join_rows(g))[0] for k, g in zip(names, gathered)})
        if "w_in" in names:
            W["w_in_r"] = regroup_w_in(W["w_in"])
        return W

    rest = tuple(k for k in big if k not in LAST_GRADS)
    W = layer_weights(0, LAST_GRADS, run_rider(weight_gather(0, LAST_GRADS), "gather_weights_first"))
    layers, saved = [], []
    h = xs
    for li in range(DEPTH):
        nxt = weight_gather(li + 1, big) if li + 1 < DEPTH else None
        if li == 0:
            h, R, gathered, W = layer_fwd(h, mems, W, li, nxt, weight_gather(0, rest),
                                          lambda W0, got: layer_weights(0, rest, got, W0))
        else:
            h, R, gathered, W = layer_fwd(h, mems, W, li, nxt)
        layers.append(W)
        saved.append(R)
        if li + 1 < DEPTH:
            W = layer_weights(li + 1, big, gathered)
    loss_row, dy = loss_and_grad(h, tgt, "loss")
    loss = lax.psum(loss_row[0, 0], ("x", "y", "c"))

    grads, received_at = [None] * DEPTH, [dict() for _ in range(DEPTH)]
    rider = None
    for li in reversed(range(DEPTH)):
        dy, G, got = layer_bwd(dy, mems, layers[li], saved[li], li, rider)
        if rider is not None:
            received_at[li + 1].update(zip(LAST_GRADS, got["carried"]))
        received_at[li].update(zip(EARLY_GRADS, got["early"]))
        received_at[li].update(zip(LATE_GRADS, got["late"]))
        grads[li] = G
        rider = exchange_rider(grad_slots(G, LAST_GRADS))
    received_at[0].update(zip(LAST_GRADS, run_rider(rider, "exchange_grads_last")))
    grad_x = dy[None]

    stacked = {k: jnp.stack([grads[li][k] for li in range(DEPTH)]) for k in WEIGHTS}
    received = [jnp.concatenate([received_at[li][k] for li in range(DEPTH)], axis=1) for k in big]
    partial_sums = []
    for k, r in zip(big, received):
        shp = r.shape
        partial_sums.append(sum_slots(r.reshape(4, -1, shp[-1]), f"sum_{k}"))
    from_sibling = swap_sibling(partial_sums, "swap_partials")

    out_g, out_d, out_m, out_v = {}, {}, {}, {}
    for k, mine, theirs in zip(big, partial_sums, from_sibling):
        shp = P[k].shape
        flat = lambda t: t.reshape(-1, shp[-1])
        g, d, mn, vn = adamw(flat(P[k]), mine, theirs, flat(M[k]), flat(V[k]), f"adamw_{k}")
        out_g[k], out_d[k], out_m[k], out_v[k] = (t.reshape(shp) for t in (g, d, mn, vn))

    gsmall = sum_slots(gather_all(pack_small(stacked), "gather_small"), "sum_small")
    g, d, mn, vn = adamw(pack_small(P), gsmall, None, pack_small(M), pack_small(V), "adamw_small")
    for dst, packed in ((out_g, g), (out_d, d), (out_m, mn), (out_v, vn)):
        dst.update(unpack_small(packed, P))

    return (loss, grad_x, *[out_g[k] for k in WEIGHTS], *[out_d[k] for k in WEIGHTS],
            *[out_m[k] for k in WEIGHTS], *[out_v[k] for k in WEIGHTS])


def kernel(x, mem, ffn1_w_gate, ffn1_w_up, ffn1_w_down, ln_ffn1_g, ln_ffn1_b, w_in, gdn_conv_w, gdn_a_log, gdn_dt_bias, gdn_norm_g, fox_b_f, conf_dw_w, conf_dw_b, conf_norm_g, conf_norm_b, w_out, ln_mix_g, ln_mix_b, mem_w_q, mem_w_kv, mem_w_o, ln_mem_g, ln_mem_b, ffn2_w_gate, ffn2_w_up, ffn2_w_down, ln_ffn2_g, ln_ffn2_b, loss_target, m_ffn1_w_gate, m_ffn1_w_up, m_ffn1_w_down, m_ln_ffn1_g, m_ln_ffn1_b, m_w_in, m_gdn_conv_w, m_gdn_a_log, m_gdn_dt_bias, m_gdn_norm_g, m_fox_b_f, m_conf_dw_w, m_conf_dw_b, m_conf_norm_g, m_conf_norm_b, m_w_out, m_ln_mix_g, m_ln_mix_b, m_mem_w_q, m_mem_w_kv, m_mem_w_o, m_ln_mem_g, m_ln_mem_b, m_ffn2_w_gate, m_ffn2_w_up, m_ffn2_w_down, m_ln_ffn2_g, m_ln_ffn2_b, v_ffn1_w_gate, v_ffn1_w_up, v_ffn1_w_down, v_ln_ffn1_g, v_ln_ffn1_b, v_w_in, v_gdn_conv_w, v_gdn_a_log, v_gdn_dt_bias, v_gdn_norm_g, v_fox_b_f, v_conf_dw_w, v_conf_dw_b, v_conf_norm_g, v_conf_norm_b, v_w_out, v_ln_mix_g, v_ln_mix_b, v_mem_w_q, v_mem_w_kv, v_mem_w_o, v_ln_mem_g, v_ln_mem_b, v_ffn2_w_gate, v_ffn2_w_up, v_ffn2_w_down, v_ln_ffn2_g, v_ln_ffn2_b):
    a = locals()
    P = {k: a[k] for k in WEIGHTS}
    M = {k: a["m_" + k] for k in WEIGHTS}
    V = {k: a["v_" + k] for k in WEIGHTS}
    return _step(P, M, V, x, mem, loss_target)
```

```python
import functools

import jax
import jax.numpy as jnp
from jax import lax
from jax.experimental import pallas as pl
from jax.experimental.pallas import tpu as pltpu

f32 = jnp.float32
bf16 = jnp.bfloat16

D = 1024
F = 2816
GW = 256
HD = 64
NH = 4
CHUNK = 64
CONF_K = 31
GDN_K = 4
DEPTH = 2
ALPHA = float((2 * DEPTH) ** 0.25)
LN_EPS = 1e-5
RMS_EPS = 1e-6
L2_EPS = 1e-6
NEG = -1e30
PW = 3200
C_GQKV, C_GZ, C_FOX, C_CONF, C_SB, C_SMALL = 0, 768, 1024, 1792, 2304, 3072
ADAM_LR, ADAM_B1, ADAM_B2, ADAM_EPS, ADAM_WD, ADAM_STEP = 0.001, 0.9, 0.999, 1e-08, 0.01, 10
VMEM_LIMIT = 56 * 1024 * 1024
MESH = pl.DeviceIdType.MESH


def _cparams(sem):
    return pltpu.CompilerParams(dimension_semantics=sem, vmem_limit_bytes=VMEM_LIMIT)


def _pick(n, cands):
    for c in cands:
        if n % c == 0:
            return c
    return n


def _sds(shape, dtype=f32):
    return jax.ShapeDtypeStruct(shape, dtype)


def _layer_norm(z, g, b):
    mu = jnp.mean(z, axis=-1, keepdims=True)
    zc = z - mu
    var = jnp.mean(zc * zc, axis=-1, keepdims=True)
    return zc * lax.rsqrt(var + LN_EPS) * g + b


def _softplus(x):
    return jnp.maximum(x, 0.0) + jnp.log(1.0 + jnp.exp(-jnp.abs(x)))


def _neg_softplus(z):
    nz = -z
    return jnp.minimum(nz, 0.0) - jnp.log(1.0 + jnp.exp(jnp.minimum(z, nz)))


def _dsilu(x):
    s = jax.nn.sigmoid(x)
    return s * (1.0 + x * (1.0 - s))


def _split_hi_lo(x):
    hi = x.astype(bf16)
    lo = (x - hi.astype(f32)).astype(bf16)
    return hi, lo


def _dot(a, b):
    return jnp.dot(a, b, preferred_element_type=f32)


def _dot_nt(a, b):
    return lax.dot_general(a, b, (((1,), (1,)), ((), ())), preferred_element_type=f32)


def _dot_tn(a, b):
    return lax.dot_general(a, b, (((0,), (0,)), ((), ())), preferred_element_type=f32)


def mm_nn(a, w, name, out_dtype=f32, also_bf16=False):
    T, K = a.shape
    N = w.shape[1]
    tm = min(T, 512)
    tn = N if N <= 1024 else _pick(N, (640, 512))

    def body(a_ref, w_ref, *o_refs):
        r = _dot(a_ref[...].astype(bf16), w_ref[...].astype(bf16))
        o_refs[0][...] = r.astype(o_refs[0].dtype)
        if also_bf16:
            o_refs[1][...] = r.astype(bf16)

    out_shape = [_sds((T, N), out_dtype)]
    out_specs = [pl.BlockSpec((tm, tn), lambda i, j: (i, j))]
    if also_bf16:
        out_shape.append(_sds((T, N), bf16))
        out_specs.append(pl.BlockSpec((tm, tn), lambda i, j: (i, j)))
    res = pl.pallas_call(
        body, grid=(T // tm, N // tn),
        in_specs=[pl.BlockSpec((tm, K), lambda i, j: (i, 0)), pl.BlockSpec((K, tn), lambda i, j: (0, j))],
        out_specs=out_specs, out_shape=out_shape,
        compiler_params=_cparams(("parallel", "arbitrary")), name=name)(a, w)
    return res if also_bf16 else res[0]


def _pieces(g):
    gs = list(g) if isinstance(g, (list, tuple)) else [g]
    offs = [sum(p.shape[1] for p in gs[:i]) for i in range(len(gs))]
    return gs, offs, offs[-1] + gs[-1].shape[1]


def mm_nt(g, w, name, add=None, add_scale=1.0):
    gs, offs, N = _pieces(g)
    T = gs[0].shape[0]
    K = w.shape[0]
    tm = min(T, 512)
    ng = len(gs)

    def body(*refs):
        w_ref, o_ref = refs[ng], refs[-1]
        r = None
        for g_ref, off in zip(refs[:ng], offs):
            part = _dot_nt(g_ref[...].astype(bf16), w_ref[:, off:off + g_ref.shape[1]].astype(bf16))
            r = part if r is None else r + part
        if add is not None:
            r = r + add_scale * refs[ng + 1][...]
        o_ref[...] = r

    in_specs = [pl.BlockSpec((tm, p.shape[1]), lambda i: (i, 0)) for p in gs] + [pl.BlockSpec((K, N), lambda i: (0, 0))]
    args = gs + [w]
    if add is not None:
        in_specs.append(pl.BlockSpec((tm, K), lambda i: (i, 0)))
        args.append(add)
    return pl.pallas_call(
        body, grid=(T // tm,), in_specs=in_specs,
        out_specs=pl.BlockSpec((tm, K), lambda i: (i, 0)), out_shape=_sds((T, K)),
        compiler_params=_cparams(("parallel",)), name=name)(*args)


def mm_tn(a, g, name):
    gs, offs, N = _pieces(g)
    T, K = a.shape
    tk = K if K * N * 4 <= 14 * 1024 * 1024 else _pick(K, (512, 1408))
    row_bytes = tk * a.dtype.itemsize + sum(p.shape[1] * p.dtype.itemsize for p in gs)
    tt = min(T, 1024 if 2 * (tk * N * 4 + 1024 * row_bytes) <= 44 * 1024 * 1024 else 512)

    def body(a_ref, *refs):
        o_ref = refs[-1]

        @pl.when(pl.program_id(1) == 0)
        def _():
            o_ref[...] = jnp.zeros_like(o_ref)
        if len(gs) == 1:
            o_ref[...] += _dot_tn(a_ref[...].astype(bf16), refs[0][...].astype(bf16))
        else:
            at = a_ref[...].astype(bf16).T
            for g_ref, off in zip(refs[:-1], offs):
                o_ref[:, off:off + g_ref.shape[1]] += _dot(at, g_ref[...].astype(bf16))

    return pl.pallas_call(
        body, grid=(K // tk, T // tt),
        in_specs=[pl.BlockSpec((tt, tk), lambda j, t: (t, j))] + [pl.BlockSpec((tt, p.shape[1]), lambda j, t: (t, 0)) for p in gs],
        out_specs=pl.BlockSpec((tk, N), lambda j, t: (j, 0)), out_shape=_sds((K, N)),
        compiler_params=_cparams(("parallel", "arbitrary")), name=name)(a, *gs)


FFN_TF = 1408


def ffn_fwd(x, wg, wu, wd, g, b, name, rider=None):
    T = x.shape[0]
    tm = min(T, 512)
    nf = F // FFN_TF

    def body(x_ref, wg_ref, wu_ref, wd_ref, g_ref, b_ref, z_ref, y_ref, acc):
        j = pl.program_id(1)

        @pl.when(j == 0)
        def _():
            acc[...] = jnp.zeros_like(acc)

        xb = x_ref[...].astype(bf16)
        h = _dot(xb, wg_ref[...])
        u = _dot(xb, wu_ref[...])
        a = (h * jax.nn.sigmoid(h) * u).astype(bf16)
        acc[...] += _dot(a, wd_ref[...])

        @pl.when(j == nf - 1)
        def _():
            z = ALPHA * x_ref[...] + 0.5 * acc[...]
            z_ref[...] = z
            y_ref[...] = _layer_norm(z, g_ref[...], b_ref[...])

    row = pl.BlockSpec((tm, D), lambda i, j: (i, 0))
    vec = pl.BlockSpec((1, D), lambda i, j: (0, 0))
    body, r_in, r_args, r_out, r_shape, r_scratch = _ride(body, 6, 2, rider, (T // tm, nf), 1)
    res = pl.pallas_call(
        body, grid=(T // tm, nf),
        in_specs=[row, pl.BlockSpec((D, FFN_TF), lambda i, j: (0, j)), pl.BlockSpec((D, FFN_TF), lambda i, j: (0, j)),
                  pl.BlockSpec((FFN_TF, D), lambda i, j: (j, 0)), vec, vec] + r_in,
        out_specs=[row, row] + r_out, out_shape=[_sds((T, D)), _sds((T, D))] + r_shape,
        scratch_shapes=[pltpu.VMEM((tm, D), f32)] + r_scratch,
        compiler_params=_cparams(("arbitrary", "arbitrary")), name=name)(x, wg, wu, wd, g, b, *r_args)
    return res[0], res[1], res[2:]


def _ln_bwd_rows(dy, zv, g):
    mu = jnp.mean(zv, axis=-1, keepdims=True)
    zc = zv - mu
    rstd = lax.rsqrt(jnp.mean(zc * zc, axis=-1, keepdims=True) + LN_EPS)
    xh = zc * rstd
    dxh = dy * g
    m1 = jnp.mean(dxh, axis=-1, keepdims=True)
    m2 = jnp.mean(dxh * xh, axis=-1, keepdims=True)
    return rstd * (dxh - m1 - xh * m2), jnp.sum(dy * xh, axis=0, keepdims=True), jnp.sum(dy, axis=0, keepdims=True)


def ffn_bwd(dz, x, wg, wu, wd, name, rider=None):
    T = x.shape[0]
    tm = min(T, 512)
    nf = F // FFN_TF

    def body(dz_ref, x_ref, wg_ref, wu_ref, wd_ref, dx_ref, a_ref, dh_ref, du_ref, dzh_ref, acc):
        j = pl.program_id(1)

        @pl.when(j == 0)
        def _():
            acc[...] = jnp.zeros_like(acc)

        dzh = (0.5 * dz_ref[...]).astype(bf16)
        xb = x_ref[...].astype(bf16)
        h = _dot(xb, wg_ref[...])
        u = _dot(xb, wu_ref[...])
        s = jax.nn.sigmoid(h)
        hs = h * s
        da = _dot_nt(dzh, wd_ref[...])
        du = (da * hs).astype(bf16)
        dh = (da * u * (s + hs * (1.0 - s))).astype(bf16)
        a_ref[...] = (hs * u).astype(bf16)
        dh_ref[...] = dh
        du_ref[...] = du
        acc[...] += _dot_nt(dh, wg_ref[...]) + _dot_nt(du, wu_ref[...])

        @pl.when(j == nf - 1)
        def _():
            dx_ref[...] = ALPHA * dz_ref[...] + acc[...]
            dzh_ref[...] = dzh

    row = pl.BlockSpec((tm, D), lambda i, j: (i, 0))
    wide = pl.BlockSpec((tm, FFN_TF), lambda i, j: (i, j))
    body, r_in, r_args, r_out, r_shape, r_scratch = _ride(body, 5, 5, rider, (T // tm, nf), 1)
    res = pl.pallas_call(
        body, grid=(T // tm, nf),
        in_specs=[row, row, pl.BlockSpec((D, FFN_TF), lambda i, j: (0, j)), pl.BlockSpec((D, FFN_TF), lambda i, j: (0, j)),
                  pl.BlockSpec((FFN_TF, D), lambda i, j: (j, 0))] + r_in,
        out_specs=[row, wide, wide, wide, row] + r_out,
        out_shape=[_sds((T, D)), _sds((T, F), bf16), _sds((T, F), bf16), _sds((T, F), bf16), _sds((T, D), bf16)] + r_shape,
        scratch_shapes=[pltpu.VMEM((tm, D), f32)] + r_scratch,
        compiler_params=_cparams(("arbitrary", "arbitrary")), name=name)(dz, x, wg, wu, wd, *r_args)
    return res[:5], res[5:]


def lin_res_ln(a, w, res, g, b, name):
    T, K = a.shape
    tm = min(T, 512)

    def body(a_ref, w_ref, res_ref, g_ref, b_ref, z_ref, y_ref):
        z = ALPHA * res_ref[...] + _dot(a_ref[...].astype(bf16), w_ref[...])
        z_ref[...] = z
        y_ref[...] = _layer_norm(z, g_ref[...], b_ref[...])

    row = pl.BlockSpec((tm, D), lambda i: (i, 0))
    vec = pl.BlockSpec((1, D), lambda i: (0, 0))
    return pl.pallas_call(
        body, grid=(T // tm,),
        in_specs=[pl.BlockSpec((tm, K), lambda i: (i, 0)), pl.BlockSpec((K, D), lambda i: (0, 0)), row, vec, vec],
        out_specs=[row, row], out_shape=[_sds((T, D)), _sds((T, D))],
        compiler_params=_cparams(("parallel",)), name=name)(a, w, res, g, b)


def ln_bwd(dy, z, g, name):
    T = z.shape[0]
    tm = min(T, 512)

    def body(dy_ref, z_ref, g_ref, dz_ref, dg_ref, db_ref):
        @pl.when(pl.program_id(0) == 0)
        def _():
            dg_ref[...] = jnp.zeros_like(dg_ref)
            db_ref[...] = jnp.zeros_like(db_ref)

        dz, dg, db = _ln_bwd_rows(dy_ref[...], z_ref[...], g_ref[...])
        dz_ref[...] = dz
        dg_ref[...] += dg
        db_ref[...] += db

    row = pl.BlockSpec((tm, D), lambda i: (i, 0))
    vec = pl.BlockSpec((1, D), lambda i: (0, 0))
    return pl.pallas_call(
        body, grid=(T // tm,), in_specs=[row, row, vec], out_specs=[row, vec, vec],
        out_shape=[_sds((T, D)), _sds((1, D)), _sds((1, D))],
        compiler_params=_cparams(("arbitrary",)), name=name)(dy, z, g)


def loss_and_grad(y, target, name):
    T = y.shape[0]
    tm = min(T, 512)

    def body(y_ref, t_ref, l_ref, dy_ref):
        @pl.when(pl.program_id(0) == 0)
        def _():
            l_ref[...] = jnp.zeros_like(l_ref)
        d = y_ref[...] - t_ref[...]
        dy_ref[...] = d * (1.0 / D)
        l_ref[...] += (0.5 / D) * jnp.sum(jnp.sum(d * d, axis=1, keepdims=True), axis=0, keepdims=True)

    row = pl.BlockSpec((tm, D), lambda i: (i, 0))
    return pl.pallas_call(
        body, grid=(T // tm,), in_specs=[row, row],
        out_specs=[pl.BlockSpec((1, 128), lambda i: (0, 0)), row],
        out_shape=[_sds((1, 128)), _sds((T, D))],
        compiler_params=_cparams(("arbitrary",)), name=name)(y, target)


def _shifted(ext, s):
    return ext if s == 0 else pltpu.roll(ext, s, 0)


def _halo_maps(tm, P, nblk):
    per = tm // P
    prev = lambda i, c: (jnp.maximum(i * per - 1, 0), c)
    nxt = lambda i, c: (jnp.minimum((i + 1) * per, nblk * per - 1), c)
    return prev, nxt


GDN_P = 8
CONF_P = 32


def gdn_conv_fwd(proj, w8, name):
    T = proj.shape[0]
    tm = min(T, 512)
    nblk = T // tm
    C = 3 * GW
    prev, _ = _halo_maps(tm, GDN_P, nblk)

    def body(xc_ref, xp_ref, w_ref, o_ref):
        i = pl.program_id(0)
        xp = jnp.where(i > 0, xp_ref[...], 0.0)
        ext = jnp.concatenate([xp, xc_ref[...]], axis=0)
        acc = jnp.zeros((tm, C), f32)
        for k in range(GDN_K):
            acc = acc + w_ref[k:k + 1, :] * _shifted(ext, GDN_K - 1 - k)[GDN_P:, :]
        y = acc * jax.nn.sigmoid(acc)
        for h in range(3 * NH):
            o_ref[h] = y[:, h * HD:(h + 1) * HD]

    return pl.pallas_call(
        body, grid=(nblk,),
        in_specs=[pl.BlockSpec((tm, C), lambda i: (i, 0)), pl.BlockSpec((GDN_P, C), lambda i: prev(i, 0)),
                  pl.BlockSpec((8, C), lambda i: (0, 0))],
        out_specs=pl.BlockSpec((3 * NH, tm, HD), lambda i: (0, i, 0)), out_shape=_sds((3 * NH, T, HD)),
        compiler_params=_cparams(("parallel",)), name=name)(proj, proj, w8)


def gdn_conv_bwd(dy, proj, w8, name):
    T = proj.shape[0]
    tm = min(T, 512)
    nblk = T // tm
    C = 3 * GW
    P = GDN_P
    prev, nxt = _halo_maps(tm, P, nblk)

    def body(dyc_ref, dyn_ref, xc_ref, xp_ref, xn_ref, w_ref, dx_ref, dw_ref):
        i = pl.program_id(0)

        @pl.when(i == 0)
        def _():
            dw_ref[...] = jnp.zeros_like(dw_ref)

        xp = jnp.where(i > 0, xp_ref[...], 0.0)
        last = i == nblk - 1
        xn = jnp.where(last, 0.0, xn_ref[...])
        dyn = jnp.where(last, 0.0, jnp.concatenate([dyn_ref[h] for h in range(3 * NH)], axis=1))
        ext = jnp.concatenate([xp, xc_ref[...], xn], axis=0)
        sh = [_shifted(ext, GDN_K - 1 - k)[P:, :] for k in range(GDN_K)]
        s = jnp.zeros((tm + P, C), f32)
        for k in range(GDN_K):
            s = s + w_ref[k:k + 1, :] * sh[k]
        dyc = jnp.concatenate([dyc_ref[h] for h in range(3 * NH)], axis=1)
        ds = jnp.concatenate([dyc, dyn], axis=0) * _dsilu(s)
        dx = jnp.zeros((tm, C), f32)
        for k in range(GDN_K):
            d = GDN_K - 1 - k
            moved = ds if d == 0 else pltpu.roll(ds, tm + P - d, 0)
            dx = dx + w_ref[k:k + 1, :] * moved[:tm, :]
            dw_ref[k:k + 1, :] += jnp.sum(ds[:tm, :] * sh[k][:tm, :], axis=0, keepdims=True)
        dx_ref[...] = dx

    col = lambda i: (i, 0)
    return pl.pallas_call(
        body, grid=(nblk,),
        in_specs=[pl.BlockSpec((3 * NH, tm, HD), lambda i: (0, i, 0)), pl.BlockSpec((3 * NH, P, HD), lambda i: (0, nxt(i, 0)[0], 0)),
                  pl.BlockSpec((tm, C), col), pl.BlockSpec((P, C), lambda i: prev(i, 0)),
                  pl.BlockSpec((P, C), lambda i: nxt(i, 0)), pl.BlockSpec((8, C), lambda i: (0, 0))],
        out_specs=[pl.BlockSpec((tm, C), col), pl.BlockSpec((8, C), lambda i: (0, 0))],
        out_shape=[_sds((T, C)), _sds((8, C))],
        compiler_params=_cparams(("arbitrary",)), name=name)(dy, dy, proj, proj, proj, w8)


def _group_ones():
    r = lax.broadcasted_iota(jnp.int32, (GW, GW), 0) // HD
    c = lax.broadcasted_iota(jnp.int32, (GW, GW), 1) // HD
    return (r == c).astype(bf16)


def _group_mean(x, ones):
    hi, lo = _split_hi_lo(x)
    return (_dot(hi, ones) + _dot(lo, ones)) * (1.0 / HD)


def _conf_norm(c, g, b, ones):
    mu = _group_mean(c, ones)
    cc = c - mu
    rstd = lax.rsqrt(_group_mean(cc * cc, ones) + LN_EPS)
    hn = cc * rstd
    return hn, rstd, hn * g + b


CONF_VAL_BLK = C_CONF // GW
CONF_GATE_BLK = C_CONF // GW + 1


def conf_fwd(proj, w32, bias, ng, nb, name):
    T = proj.shape[0]
    tm = min(T, 512)
    nblk = T // tm
    P = CONF_P
    prev, _ = _halo_maps(tm, P, nblk)

    def body(vc_ref, gc_ref, vp_ref, gp_ref, w_ref, bias_ref, ng_ref, nb_ref, y_ref, c_ref):
        i = pl.program_id(0)
        pc = vc_ref[...] * jax.nn.sigmoid(gc_ref[...])
        pp = jnp.where(i > 0, vp_ref[...] * jax.nn.sigmoid(gp_ref[...]), 0.0)
        ext = jnp.concatenate([pp, pc], axis=0)
        acc = jnp.zeros((tm, GW), f32)
        for k in range(CONF_K):
            acc = acc + w_ref[k:k + 1, :] * _shifted(ext, CONF_K - 1 - k)[P:, :]
        c = acc + bias_ref[...]
        c_ref[...] = c
        _, _, yn = _conf_norm(c, ng_ref[...], nb_ref[...], _group_ones())
        y_ref[...] = yn * jax.nn.sigmoid(yn)

    vec = pl.BlockSpec((1, GW), lambda i: (0, 0))
    return pl.pallas_call(
        body, grid=(nblk,),
        in_specs=[pl.BlockSpec((tm, GW), lambda i: (i, CONF_VAL_BLK)), pl.BlockSpec((tm, GW), lambda i: (i, CONF_GATE_BLK)),
                  pl.BlockSpec((P, GW), lambda i: prev(i, CONF_VAL_BLK)), pl.BlockSpec((P, GW), lambda i: prev(i, CONF_GATE_BLK)),
                  pl.BlockSpec((32, GW), lambda i: (0, 0)), vec, vec, vec],
        out_specs=[pl.BlockSpec((tm, GW), lambda i: (i, 0))] * 2, out_shape=[_sds((T, GW))] * 2,
        compiler_params=_cparams(("parallel",)), name=name)(proj, proj, proj, proj, w32, bias, ng, nb)


def conf_bwd(dy, c, proj, w32, ng, nb, name):
    T = proj.shape[0]
    tm = min(T, 512)
    nblk = T // tm
    P = CONF_P
    prev, nxt = _halo_maps(tm, P, nblk)

    def body(dyc_ref, dyn_ref, cc_ref, cn_ref, vc_ref, gc_ref, vp_ref, gp_ref, w_ref, ng_ref, nb_ref,
             dglu_ref, dw_ref, dbias_ref, dng_ref, dnb_ref):
        i = pl.program_id(0)

        @pl.when(i == 0)
        def _():
            dw_ref[...] = jnp.zeros_like(dw_ref)
            dbias_ref[...] = jnp.zeros_like(dbias_ref)
            dng_ref[...] = jnp.zeros_like(dng_ref)
            dnb_ref[...] = jnp.zeros_like(dnb_ref)

        ones = _group_ones()
        g = ng_ref[...]

        def dc_of(dyv, cv):
            hn, rstd, yn = _conf_norm(cv, g, nb_ref[...], ones)
            dyn_ = dyv * _dsilu(yn)
            dhn = dyn_ * g
            dc = rstd * (dhn - _group_mean(dhn, ones) - hn * _group_mean(dhn * hn, ones))
            return dc, dyn_, hn

        dc_c, dyn_c, hn_c = dc_of(dyc_ref[...], cc_ref[...])
        dc_n, _, _ = dc_of(dyn_ref[...], cn_ref[...])
        dc_n = jnp.where(i == nblk - 1, 0.0, dc_n)
        dng_ref[...] += jnp.sum(dyn_c * hn_c, axis=0, keepdims=True)
        dnb_ref[...] += jnp.sum(dyn_c, axis=0, keepdims=True)
        dbias_ref[...] += jnp.sum(dc_c, axis=0, keepdims=True)

        sig_c = jax.nn.sigmoid(gc_ref[...])
        val_c = vc_ref[...]
        pc = val_c * sig_c
        pp = jnp.where(i > 0, vp_ref[...] * jax.nn.sigmoid(gp_ref[...]), 0.0)
        ext = jnp.concatenate([pp, pc], axis=0)
        dext = jnp.concatenate([dc_c, dc_n], axis=0)
        dp = jnp.zeros((tm, GW), f32)
        for k in range(CONF_K):
            d = CONF_K - 1 - k
            moved = dext if d == 0 else pltpu.roll(dext, tm + P - d, 0)
            dp = dp + w_ref[k:k + 1, :] * moved[:tm, :]
            dw_ref[k:k + 1, :] += jnp.sum(dc_c * _shifted(ext, d)[P:, :], axis=0, keepdims=True)
        dglu_ref[:, 0:GW] = dp * sig_c
        dglu_ref[:, GW:2 * GW] = dp * val_c * sig_c * (1.0 - sig_c)

    vec = pl.BlockSpec((1, GW), lambda i: (0, 0))
    blk = pl.BlockSpec((tm, GW), lambda i: (i, 0))
    return pl.pallas_call(
        body, grid=(nblk,),
        in_specs=[blk, pl.BlockSpec((P, GW), lambda i: nxt(i, 0)), blk, pl.BlockSpec((P, GW), lambda i: nxt(i, 0)),
                  pl.BlockSpec((tm, GW), lambda i: (i, CONF_VAL_BLK)), pl.BlockSpec((tm, GW), lambda i: (i, CONF_GATE_BLK)),
                  pl.BlockSpec((P, GW), lambda i: prev(i, CONF_VAL_BLK)), pl.BlockSpec((P, GW), lambda i: prev(i, CONF_GATE_BLK)),
                  pl.BlockSpec((32, GW), lambda i: (0, 0)), vec, vec],
        out_specs=[pl.BlockSpec((tm, 2 * GW), lambda i: (i, 0)), pl.BlockSpec((32, GW), lambda i: (0, 0)), vec, vec, vec],
        out_shape=[_sds((T, 2 * GW)), _sds((32, GW)), _sds((1, GW)), _sds((1, GW)), _sds((1, GW))],
        compiler_params=_cparams(("arbitrary",)), name=name)(dy, dy, c, c, proj, proj, proj, proj, w32, ng, nb)


def _mm_raw(a, b, ta, tb):
    ca = a.ndim - 2 if ta else a.ndim - 1
    cb = b.ndim - 1 if tb else b.ndim - 2
    batch = ((0,), (0,)) if a.ndim == 3 else ((), ())
    return lax.dot_general(a, b, (((ca,), (cb,)), batch), preferred_element_type=f32)


def _mm_prec(a, b, ta, tb, prec):
    if prec == 1:
        return _mm_raw(a.astype(bf16), b.astype(bf16), ta, tb)
    bh, bl = _split_hi_lo(b)
    if prec == 2:
        ab = a.astype(bf16)
        return _mm_raw(ab, bh, ta, tb) + _mm_raw(ab, bl, ta, tb)
    ah, al = _split_hi_lo(a)
    return _mm_raw(ah, bh, ta, tb) + (_mm_raw(ah, bl, ta, tb) + _mm_raw(al, bh, ta, tb))


@functools.partial(jax.custom_vjp, nondiff_argnums=(2, 3, 4))
def mm(a, b, ta=False, tb=False, prec=1):
    return _mm_prec(a, b, ta, tb, prec)


def _mm_fwd(a, b, ta, tb, prec):
    return _mm_prec(a, b, ta, tb, prec), (a, b)


def _mm_bwd(ta, tb, prec, res, ct):
    a, b = res
    da = _mm_prec(b, ct, tb, True, 1) if ta else _mm_prec(ct, b, False, not tb, 1)
    db = _mm_prec(ct, a, True, ta, 1) if tb else _mm_prec(a, ct, not ta, False, 2 if prec == 2 else 1)
    return da, db


mm.defvjp(_mm_fwd, _mm_bwd)


def _tri_inv_raw(l):
    n = -l
    rr = lax.broadcasted_iota(jnp.int32, l.shape, 1)
    cc = lax.broadcasted_iota(jnp.int32, l.shape, 2)
    p = jnp.where(rr == cc, 1.0, 0.0) + n
    for _ in range(5):
        n = _mm_prec(n, n, False, False, 1)
        p = p + _mm_prec(p, n, False, False, 1)
    return p


@jax.custom_vjp
def tri_inv(l):
    return _tri_inv_raw(l)


def _tri_inv_fwd(l):
    t = _tri_inv_raw(l)
    return t, t


def _tri_inv_bwd(t, ct):
    return (-_mm_prec(_mm_prec(t, ct, True, False, 1), t, False, True, 1),)


tri_inv.defvjp(_tri_inv_fwd, _tri_inv_bwd)


def _gdn_block(S, qs, ks, vs, a, b, z, alog, dtb, ng):
    shp = (NH, CHUNK, CHUNK)
    ii = lax.broadcasted_iota(jnp.int32, shp, 1)
    jj = lax.broadcasted_iota(jnp.int32, shp, 2)
    l_incl = jnp.where(ii >= jj, 1.0, 0.0)
    ys = []
    for c in range(len(qs)):
        q = qs[c] * lax.rsqrt(jnp.sum(qs[c] * qs[c], axis=-1, keepdims=True) + L2_EPS) * (HD ** -0.5)
        k = ks[c] * lax.rsqrt(jnp.sum(ks[c] * ks[c], axis=-1, keepdims=True) + L2_EPS)
        v = vs[c]
        beta = jax.nn.sigmoid(b[c])
        g = -jnp.exp(alog) * _softplus(a[c] + dtb)
        gcb = mm(l_incl, jnp.broadcast_to(g, shp), False, False, 2)
        gcr = jnp.swapaxes(gcb, 1, 2)
        decay = jnp.exp(jnp.where(ii >= jj, gcb - gcr, NEG))
        g_last = jnp.sum(jnp.where(ii == CHUNK - 1, gcb, 0.0), axis=1, keepdims=True)
        eg = jnp.exp(gcb)
        kb = k * beta
        lkk = jnp.where(ii > jj, mm(kb, k, False, True) * decay, 0.0)
        t_inv = tri_inv(lkk)
        u = mm(t_inv, v * beta)
        w = mm(t_inv, kb * eg)
        a_qk = jnp.where(ii >= jj, mm(q, k, False, True) * decay, 0.0)
        q_dec = q * eg
        k_dec = k * jnp.exp(g_last - gcb)
        v_new = u - mm(w, S)
        o = mm(q_dec, S) + mm(a_qk, v_new)
        S = S * jnp.exp(g_last) + mm(k_dec, v_new, True, False)
        y = o * lax.rsqrt(jnp.mean(o * o, axis=-1, keepdims=True) + RMS_EPS) * ng
        ys.append(y * (z[c] * jax.nn.sigmoid(z[c])))
    return S, ys


GDN_CB = 256


def _heads(ref, rows, width, lane0=0):
    return jnp.stack([ref[rows, lane0 + h * width:lane0 + (h + 1) * width] for h in range(NH)])


def _chunk_rows(c):
    return slice(c * CHUNK, (c + 1) * CHUNK)


def _gdn_load(refs, nc):
    q_ref, k_ref, v_ref, z_ref, sm_ref = refs
    hm = lambda r: [r[:, _chunk_rows(c), :] for c in range(nc)]
    return (hm(q_ref), hm(k_ref), hm(v_ref), [_heads(z_ref, _chunk_rows(c), HD) for c in range(nc)],
            [_heads(sm_ref, _chunk_rows(c), 1) for c in range(nc)], [_heads(sm_ref, _chunk_rows(c), 1, NH) for c in range(nc)])


def gdn_fwd(qkv_hm, proj, alog, dtb, ng, name):
    T = proj.shape[0]
    cb = min(T, GDN_CB)
    nc = cb // CHUNK
    nb = T // cb

    def body(q_ref, k_ref, v_ref, z_ref, sm_ref, alog_ref, dtb_ref, ng_ref, y_ref, s_ref, S):
        @pl.when(pl.program_id(0) == 0)
        def _():
            S[...] = jnp.zeros_like(S)
        s_ref[...] = S[...]
        qs, ks, vs, zs, as_, bs = _gdn_load((q_ref, k_ref, v_ref, z_ref, sm_ref), nc)
        s_out, ys = _gdn_block(S[...], qs, ks, vs, as_, bs, zs, alog_ref[...], dtb_ref[...], ng_ref[...])
        S[...] = s_out
        for c in range(nc):
            for h in range(NH):
                y_ref[_chunk_rows(c), h * HD:(h + 1) * HD] = ys[c][h]

    hm = lambda h0: pl.BlockSpec((NH, cb, HD), lambda i: (h0, i, 0))
    par = pl.BlockSpec((NH, 1, 1), lambda i: (0, 0, 0))
    return pl.pallas_call(
        body, grid=(nb,),
        in_specs=[hm(0), hm(1), hm(2), pl.BlockSpec((cb, GW), lambda i: (i, C_GZ // GW)),
                  pl.BlockSpec((cb, 128), lambda i: (i, C_SMALL // 128)), par, par, pl.BlockSpec((1, 1, HD), lambda i: (0, 0, 0))],
        out_specs=[pl.BlockSpec((cb, GW), lambda i: (i, 0)), pl.BlockSpec((None, NH, HD, HD), lambda i: (i, 0, 0, 0))],
        out_shape=[_sds((T, GW)), _sds((nb, NH, HD, HD))],
        scratch_shapes=[pltpu.VMEM((NH, HD, HD), f32)],
        compiler_params=_cparams(("arbitrary",)), name=name)(qkv_hm, qkv_hm, qkv_hm, proj, proj, alog, dtb, ng)


def gdn_bwd(dycat, states, qkv_hm, proj, alog, dtb, ng, name):
    T = proj.shape[0]
    cb = min(T, GDN_CB)
    nc = cb // CHUNK
    nb = T // cb

    def body(dy_ref, s_ref, q_ref, k_ref, v_ref, z_ref, sm_ref, alog_ref, dtb_ref, ng_ref,
             dqkv_ref, dz_ref, dsm_ref, dalog_ref, ddtb_ref, dng_ref, dS):
        @pl.when(pl.program_id(0) == 0)
        def _():
            dS[...] = jnp.zeros_like(dS)
            dalog_ref[...] = jnp.zeros_like(dalog_ref)
            ddtb_ref[...] = jnp.zeros_like(ddtb_ref)
            dng_ref[...] = jnp.zeros_like(dng_ref)

        qs, ks, vs, zs, as_, bs = _gdn_load((q_ref, k_ref, v_ref, z_ref, sm_ref), nc)
        _, vjp = jax.vjp(_gdn_block, s_ref[...], qs, ks, vs, as_, bs, zs, alog_ref[...], dtb_ref[...], ng_ref[...])
        dys = [_heads(dy_ref, _chunk_rows(c), HD) for c in range(nc)]
        d_s, dqs, dks, dvs, das, dbs, dzs, d_alog, d_dtb, d_ng = vjp((dS[...], dys))
        dS[...] = d_s
        dalog_ref[...] += d_alog
        ddtb_ref[...] += d_dtb
        dng_ref[...] += d_ng
        for c in range(nc):
            sl = _chunk_rows(c)
            dqkv_ref[0:NH, sl, :] = dqs[c]
            dqkv_ref[NH:2 * NH, sl, :] = dks[c]
            dqkv_ref[2 * NH:3 * NH, sl, :] = dvs[c]
            for h in range(NH):
                dz_ref[sl, h * HD:(h + 1) * HD] = dzs[c][h]
            dsm_ref[sl, :] = _pack_cols([das[c][h] for h in range(NH)] + [dbs[c][h] for h in range(NH)])

    rev = lambda i: nb - 1 - i
    hm = lambda h0: pl.BlockSpec((NH, cb, HD), lambda i: (h0, rev(i), 0))
    tok = lambda w, cblk: pl.BlockSpec((cb, w), lambda i: (rev(i), cblk))
    par = pl.BlockSpec((NH, 1, 1), lambda i: (0, 0, 0))
    ngs = pl.BlockSpec((1, 1, HD), lambda i: (0, 0, 0))
    res = pl.pallas_call(
        body, grid=(nb,),
        in_specs=[tok(GW, 0), pl.BlockSpec((None, NH, HD, HD), lambda i: (rev(i), 0, 0, 0)),
                  hm(0), hm(1), hm(2), tok(GW, C_GZ // GW), tok(128, C_SMALL // 128), par, par, ngs],
        out_specs=[pl.BlockSpec((3 * NH, cb, HD), lambda i: (0, rev(i), 0)), tok(GW, 0), tok(128, 0), par, par, ngs],
        out_shape=[_sds((3 * NH, T, HD)), _sds((T, GW)), _sds((T, 128))] + [_sds((NH, 1, 1))] * 2 + [_sds((1, 1, HD))],
        scratch_shapes=[pltpu.VMEM((NH, HD, HD), f32)],
        compiler_params=_cparams(("arbitrary",)), name=name)(dycat, states, qkv_hm, qkv_hm, qkv_hm, proj, proj, alog, dtb, ng)
    return res


F_LANE = 8
SCAN_TB = 256


def fox_gate_fwd(proj, bfv, name):
    T = proj.shape[0]
    tb = min(T, SCAN_TB)

    def body(x_ref, b_ref, o_ref, carry):
        @pl.when(pl.program_id(0) == 0)
        def _():
            carry[...] = jnp.zeros_like(carry)
        logf = -_softplus(-(x_ref[...] + b_ref[...]))
        r = lax.broadcasted_iota(jnp.int32, (tb, tb), 0)
        c = lax.broadcasted_iota(jnp.int32, (tb, tb), 1)
        tri = (r >= c).astype(bf16)
        hi, lo = _split_hi_lo(logf)
        cum = _dot(tri, hi) + _dot(tri, lo) + carry[0:1, :]
        o_ref[...] = cum
        carry[0:1, :] = cum[tb - 1:tb, :]

    return pl.pallas_call(
        body, grid=(T // tb,),
        in_specs=[pl.BlockSpec((tb, 128), lambda i: (i, C_SMALL // 128)), pl.BlockSpec((1, 128), lambda i: (0, 0))],
        out_specs=pl.BlockSpec((tb, 128), lambda i: (i, 0)), out_shape=_sds((T, 128)),
        scratch_shapes=[pltpu.VMEM((8, 128), f32)],
        compiler_params=_cparams(("arbitrary",)), name=name)(proj, bfv)


def fox_gate_bwd(dcum, proj, bfv, name):
    T = proj.shape[0]
    tb = min(T, SCAN_TB)
    nb = T // tb

    def body(d_ref, x_ref, b_ref, o_ref, db_ref, carry):
        @pl.when(pl.program_id(0) == 0)
        def _():
            carry[...] = jnp.zeros_like(carry)
            db_ref[...] = jnp.zeros_like(db_ref)
        r = lax.broadcasted_iota(jnp.int32, (tb, tb), 0)
        c = lax.broadcasted_iota(jnp.int32, (tb, tb), 1)
        tri = (c >= r).astype(bf16)
        hi, lo = _split_hi_lo(d_ref[...])
        dlogf = _dot(tri, hi) + _dot(tri, lo) + carry[0:1, :]
        carry[0:1, :] = dlogf[0:1, :]
        lane = lax.broadcasted_iota(jnp.int32, (tb, 128), 1)
        keep = (lane >= F_LANE) & (lane < F_LANE + NH)
        dx = jnp.where(keep, dlogf * jax.nn.sigmoid(-(x_ref[...] + b_ref[...])), 0.0)
        o_ref[...] = dx
        db_ref[...] += jnp.sum(dx, axis=0, keepdims=True)

    rev = lambda i: nb - 1 - i
    return pl.pallas_call(
        body, grid=(nb,),
        in_specs=[pl.BlockSpec((tb, 128), lambda i: (rev(i), 0)), pl.BlockSpec((tb, 128), lambda i: (rev(i), C_SMALL // 128)),
                  pl.BlockSpec((1, 128), lambda i: (0, 0))],
        out_specs=[pl.BlockSpec((tb, 128), lambda i: (rev(i), 0)), pl.BlockSpec((1, 128), lambda i: (0, 0))],
        out_shape=[_sds((T, 128)), _sds((1, 128))],
        scratch_shapes=[pltpu.VMEM((8, 128), f32)],
        compiler_params=_cparams(("arbitrary",)), name=name)(dcum, proj, bfv)


ATT_TQ = 1024
ATT_TQ_BWD = 512
ATT_TK = 256


def _lane_col(tile, lane):
    li = lax.broadcasted_iota(jnp.int32, tile.shape, 1)
    return jnp.sum(jnp.where(li == lane, tile, 0.0), axis=1, keepdims=True)


def _pack_cols(cols):
    rows = cols[0].shape[0]
    li = lax.broadcasted_iota(jnp.int32, (rows, 128), 1)
    out = jnp.zeros((rows, 128), f32)
    for h, cv in enumerate(cols):
        out = jnp.where(li == h, cv, out)
    return out


def _head_masks():
    li = lax.broadcasted_iota(jnp.int32, (1, 128), 1)
    return [li < HD, li >= HD]


def _qkv_specs(T, tq, base_blk):
    q = pl.BlockSpec((tq, 128), lambda p, i: (i, base_blk + p))
    k = pl.BlockSpec((T, 128), lambda p, i: (0, base_blk + 2 + p))
    v = pl.BlockSpec((T, 128), lambda p, i: (0, base_blk + 4 + p))
    return q, k, v


def _stack_heads(x, masks):
    return jnp.concatenate([jnp.where(m, x, jnp.zeros_like(x)) for m in masks], axis=0)


def _side_by_side(x, tq):
    return jnp.concatenate([x[:tq], x[tq:]], axis=1)


def _stacked_mask(tq, tk, d, strict):
    r = lax.broadcasted_iota(jnp.int32, (2 * tq, tk), 0)
    r = jnp.where(r >= tq, r - tq, r)
    c = lax.broadcasted_iota(jnp.int32, (2 * tq, tk), 1) + d * tk
    return c < r if strict else c <= r


def _sub_head_rows(s, rows2, tq):
    return jnp.concatenate([s[:tq] - rows2[0:1, :], s[tq:] - rows2[1:2, :]], axis=0)


def _lane_cols2(tile):
    return jnp.concatenate([_lane_col(tile, 0), _lane_col(tile, 1)], axis=0)


def _pack_cols2(col, tq):
    return _pack_cols([col[:tq], col[tq:]])


def _dot_hilo2(x, tri):
    n = x.shape[0]
    hi, lo = _split_hi_lo(x)
    r = _dot(jnp.concatenate([hi, lo], axis=0), tri)
    return r[:n] + r[n:]


def fox_fwd(pbf, cumrow, name):
    T = pbf.shape[0]
    tq, tk = min(T, ATT_TQ), min(T, ATT_TK)
    nq, nk, per = T // tq, T // tk, tq // tk
    scale = HD ** -0.5

    def body(q_ref, k_ref, v_ref, cr_ref, o_ref, lse_ref):
        i = pl.program_id(1)
        masks = _head_masks()
        qs2 = _stack_heads(q_ref[...] * scale, masks)

        def tile(kb, carry, d=None):
            m, l, acc = carry
            off = pl.multiple_of(kb * tk, tk)
            v2 = _stack_heads(v_ref[pl.ds(off, tk), :], masks)
            s = _sub_head_rows(_dot_nt(qs2, k_ref[pl.ds(off, tk), :]), cr_ref[kb], tq)
            if d is not None:
                s = jnp.where(_stacked_mask(tq, tk, d, False), s, NEG)
            m_new = jnp.maximum(m, jnp.max(s, axis=1, keepdims=True))
            corr = jnp.exp(m - m_new)
            p = jnp.exp(s - m_new)
            l = l * corr + jnp.sum(p, axis=1, keepdims=True)
            acc = acc * jnp.where(masks[0], corr[:tq], corr[tq:]) + _dot(_side_by_side(p.astype(bf16), tq), v2)
            return m_new, l, acc

        init = (jnp.full((2 * tq, 1), NEG, f32), jnp.zeros((2 * tq, 1), f32), jnp.zeros((tq, 128), f32))
        carry = lax.fori_loop(0, i * per, tile, init)
        for d in range(per):
            carry = tile(i * per + d, carry, d)
        m, l, acc = carry
        o_ref[...] = acc * jnp.where(masks[0], 1.0 / l[:tq], 1.0 / l[tq:])
        lse_ref[...] = _pack_cols2(m + jnp.log(l), tq)

    qs, ks, vs = _qkv_specs(T, tq, C_FOX // 128)
    return pl.pallas_call(
        body, grid=(2, nq),
        in_specs=[qs, ks, vs, pl.BlockSpec((None, nk, 8, tk), lambda p, i: (p, 0, 0, 0))],
        out_specs=[pl.BlockSpec((tq, 128), lambda p, i: (i, p)), pl.BlockSpec((None, tq, 128), lambda p, i: (p, i, 0))],
        out_shape=[_sds((T, GW)), _sds((2, T, 128))],
        compiler_params=_cparams(("parallel", "parallel")), name=name)(pbf, pbf, pbf, cumrow)


def fox_bwd(do, o, lse, pbf, cumrow, name, rider=None):
    T = pbf.shape[0]
    tq, tk = min(T, ATT_TQ_BWD), min(T, ATT_TK)
    nq, nk, per = T // tq, T // tk, tq // tk
    scale = HD ** -0.5

    def body(do_ref, o_ref, lse_ref, q_ref, k_ref, v_ref, cr_ref, dq_ref, dk_ref, dv_ref, dc_ref, dcq_ref):
        i = pl.program_id(1)

        @pl.when(i == 0)
        def _():
            dk_ref[...] = jnp.zeros_like(dk_ref)
            dv_ref[...] = jnp.zeros_like(dv_ref)
            dc_ref[...] = jnp.zeros_like(dc_ref)

        masks = _head_masks()
        dov = do_ref[...]
        qs2 = _stack_heads(q_ref[...] * scale, masks)
        do2 = _stack_heads(dov.astype(bf16), masks)
        prod = dov * o_ref[...]
        delta = jnp.concatenate([jnp.sum(jnp.where(m, prod, 0.0), axis=1, keepdims=True) for m in masks], axis=0)
        lse2 = _lane_cols2(lse_ref[...])

        def tile(kb, carry, d=None):
            dq, rsum = carry
            off = pl.multiple_of(kb * tk, tk)
            kblk = k_ref[pl.ds(off, tk), :]
            p = jnp.exp(_sub_head_rows(_dot_nt(qs2, kblk), cr_ref[kb], tq) - lse2)
            if d is not None:
                p = jnp.where(_stacked_mask(tq, tk, d, False), p, 0.0)
            dp = _dot_nt(do2, v_ref[pl.ds(off, tk), :])
            ds = p * (dp - delta)
            dsb = ds.astype(bf16)
            dq = dq + _dot(_side_by_side(dsb, tq), _stack_heads(kblk, masks))
            dk_ref[pl.ds(off, tk), :] += _dot_tn(dsb, qs2)
            dv_ref[pl.ds(off, tk), :] += _dot_tn(p.astype(bf16), do2)
            dc_ref[kb, 0:1, :] += -jnp.sum(ds[:tq], axis=0, keepdims=True)
            dc_ref[kb, 1:2, :] += -jnp.sum(ds[tq:], axis=0, keepdims=True)
            return dq, rsum + jnp.sum(ds, axis=1, keepdims=True)

        carry = lax.fori_loop(0, i * per, tile, (jnp.zeros((tq, 128), f32), jnp.zeros((2 * tq, 1), f32)))
        for d in range(per):
            carry = tile(i * per + d, carry, d)
        dq, rsum = carry
        dq_ref[...] = dq * scale
        dcq_ref[...] = _pack_cols2(rsum, tq)

    qs, ks, vs = _qkv_specs(T, tq, C_FOX // 128)
    tile_spec = pl.BlockSpec((tq, 128), lambda p, i: (i, p))
    pair = pl.BlockSpec((None, tq, 128), lambda p, i: (p, i, 0))
    rowsp = pl.BlockSpec((None, nk, 8, tk), lambda p, i: (p, 0, 0, 0))
    full = pl.BlockSpec((T, 128), lambda p, i: (0, p))
    body, r_in, r_args, r_out, r_shape, r_scratch = _ride(body, 7, 5, rider, (2, nq))
    res = pl.pallas_call(
        body, grid=(2, nq),
        in_specs=[tile_spec, tile_spec, pair, qs, ks, vs, rowsp] + r_in,
        out_specs=[tile_spec, full, full, rowsp, pair] + r_out,
        out_shape=[_sds((T, GW)), _sds((T, GW)), _sds((T, GW)), _sds((2, nk, 8, tk)), _sds((2, T, 128))] + r_shape,
        scratch_shapes=r_scratch,
        compiler_params=_cparams(("arbitrary", "arbitrary")), name=name)(do, o, lse, pbf, pbf, pbf, cumrow, *r_args)
    return res[0], res[1], res[2], res[3], res[4], res[5:]


def _tri(tq, pred):
    r = lax.broadcasted_iota(jnp.int32, (tq, tq), 0)
    c = lax.broadcasted_iota(jnp.int32, (tq, tq), 1)
    return pred(r, c).astype(bf16)


def _ride(body, n_in, n_out, rider, grid, n_scratch=0):
    if rider is None:
        return body, [], [], [], [], []
    nr = rider.n

    def wrapped(*refs):
        ins, rin = refs[:n_in], refs[n_in:n_in + nr]
        outs = refs[n_in + nr:n_in + nr + n_out]
        rout = refs[n_in + nr + n_out:n_in + 2 * nr + n_out]
        own = refs[n_in + 2 * nr + n_out:n_in + 2 * nr + n_out + n_scratch]
        sems = refs[n_in + 2 * nr + n_out + n_scratch:]
        ids = [pl.program_id(a) for a in range(len(grid))]
        first = functools.reduce(jnp.logical_and, [i == 0 for i in ids])
        last = functools.reduce(jnp.logical_and, [i == g - 1 for i, g in zip(ids, grid)])

        @pl.when(first)
        def _():
            rider.start(rin, rout, sems)

        body(*ins, *outs, *own)

        @pl.when(last)
        def _():
            rider.finish(rin, rout, sems)

    anyspec = pl.BlockSpec(memory_space=pl.ANY)
    return wrapped, [anyspec] * nr, list(rider.arrays), [anyspec] * nr, list(rider.out_shape), list(rider.scratch)


def sb_fwd(pbf, name, rider=None):
    T = pbf.shape[0]
    tq, tk = min(T, ATT_TQ), min(T, ATT_TK)
    nq, per = T // tq, tq // tk
    scale = HD ** -0.5

    def body(q_ref, k_ref, v_ref, o_ref, tot_ref):
        i = pl.program_id(1)
        masks = _head_masks()
        qs2 = _stack_heads(q_ref[...] * scale, masks)
        after = _tri(tk, lambda r, c: r > c)

        def tile(kb, carry, d=None):
            rs, acc = carry
            off = pl.multiple_of(kb * tk, tk)
            z = _dot_nt(qs2, k_ref[pl.ds(off, tk), :])
            lk = _neg_softplus(z)
            if d is not None:
                lk = jnp.where(_stacked_mask(tq, tk, d, True), lk, 0.0)
            w = jnp.exp(z + lk + (_dot_hilo2(lk, after) + rs))
            if d is not None:
                w = jnp.where(_stacked_mask(tq, tk, d, True), w, 0.0)
            acc = acc + _dot(_side_by_side(w.astype(bf16), tq), _stack_heads(v_ref[pl.ds(off, tk), :], masks))
            return rs + jnp.sum(lk, axis=1, keepdims=True), acc

        carry = (jnp.zeros((2 * tq, 1), f32), jnp.zeros((tq, 128), f32))
        for d in reversed(range(per)):
            carry = tile(i * per + d, carry, d)
        rs, acc = lax.fori_loop(0, i * per, lambda n, c: tile(i * per - 1 - n, c), carry)
        o_ref[...] = acc
        tot_ref[...] = _pack_cols2(rs, tq)

    qs, ks, vs = _qkv_specs(T, tq, C_SB // 128)
    body, r_in, r_args, r_out, r_shape, r_scratch = _ride(body, 3, 2, rider, (2, nq))
    res = pl.pallas_call(
        body, grid=(2, nq), in_specs=[qs, ks, vs] + r_in,
        out_specs=[pl.BlockSpec((tq, 128), lambda p, i: (i, p)), pl.BlockSpec((None, tq, 128), lambda p, i: (p, i, 0))] + r_out,
        out_shape=[_sds((T, GW)), _sds((2, T, 128))] + r_shape, scratch_shapes=r_scratch,
        compiler_params=_cparams(("arbitrary", "arbitrary")), name=name)(pbf, pbf, pbf, *r_args)
    return res[0], res[1], res[2:]


def sb_bwd(do, tot, pbf, name, rider=None):
    T = pbf.shape[0]
    tq, tk = min(T, ATT_TQ_BWD), min(T, ATT_TK)
    nq, per = T // tq, tq // tk
    scale = HD ** -0.5

    def body(do_ref, tot_ref, q_ref, k_ref, v_ref, dq_ref, dk_ref, dv_ref):
        i = pl.program_id(1)

        @pl.when(i == 0)
        def _():
            dk_ref[...] = jnp.zeros_like(dk_ref)
            dv_ref[...] = jnp.zeros_like(dv_ref)

        masks = _head_masks()
        qs2 = _stack_heads(q_ref[...] * scale, masks)
        do2 = _stack_heads(do_ref[...].astype(bf16), masks)
        tot2 = _lane_cols2(tot_ref[...])
        upto = _tri(tk, lambda r, c: r <= c)
        before = _tri(tk, lambda r, c: r < c)

        def tile(kb, carry, d=None):
            pre, cg, dq = carry
            off = pl.multiple_of(kb * tk, tk)
            kblk = k_ref[pl.ds(off, tk), :]
            z = _dot_nt(qs2, kblk)
            lk = _neg_softplus(z)
            keep = jnp.exp(lk)
            if d is not None:
                lk = jnp.where(_stacked_mask(tq, tk, d, True), lk, 0.0)
            w = jnp.exp(z + lk + (tot2 - (pre + _dot_hilo2(lk, upto))))
            if d is not None:
                w = jnp.where(_stacked_mask(tq, tk, d, True), w, 0.0)
            gmat = w * _dot_nt(do2, v_ref[pl.ds(off, tk), :])
            cmat = cg + _dot(gmat.astype(bf16), before)
            dz = gmat * keep - cmat * (1.0 - keep)
            if d is not None:
                dz = jnp.where(_stacked_mask(tq, tk, d, True), dz, 0.0)
            dzb = dz.astype(bf16)
            dq = dq + _dot(_side_by_side(dzb, tq), _stack_heads(kblk, masks))
            dk_ref[pl.ds(off, tk), :] += _dot_tn(dzb, qs2)
            dv_ref[pl.ds(off, tk), :] += _dot_tn(w.astype(bf16), do2)
            return pre + jnp.sum(lk, axis=1, keepdims=True), cg + jnp.sum(gmat, axis=1, keepdims=True), dq

        zc = jnp.zeros((2 * tq, 1), f32)
        carry = lax.fori_loop(0, i * per, tile, (zc, zc, jnp.zeros((tq, 128), f32)))
        for d in range(per):
            carry = tile(i * per + d, carry, d)
        dq_ref[...] = carry[2] * scale

    qs, ks, vs = _qkv_specs(T, tq, C_SB // 128)
    tile_spec = pl.BlockSpec((tq, 128), lambda p, i: (i, p))
    pair = pl.BlockSpec((None, tq, 128), lambda p, i: (p, i, 0))
    full = pl.BlockSpec((T, 128), lambda p, i: (0, p))
    body, r_in, r_args, r_out, r_shape, r_scratch = _ride(body, 5, 3, rider, (2, nq))
    res = pl.pallas_call(
        body, grid=(2, nq), in_specs=[tile_spec, pair, qs, ks, vs] + r_in,
        out_specs=[tile_spec, full, full] + r_out, out_shape=[_sds((T, GW))] * 3 + r_shape, scratch_shapes=r_scratch,
        compiler_params=_cparams(("arbitrary", "arbitrary")), name=name)(do, tot, pbf, pbf, pbf, *r_args)
    return res[0], res[1], res[2], res[3:]


MEM_HD = D // 4


def mem_fwd(q, kv, name):
    T = q.shape[0]
    M = kv.shape[0]
    tm = min(T, 512)
    scale = MEM_HD ** -0.5

    def body(q_ref, kv_ref, o_ref):
        for h in range(4):
            sl = slice(h * MEM_HD, (h + 1) * MEM_HD)
            kh = kv_ref[:, sl].astype(bf16)
            vh = kv_ref[:, D + h * MEM_HD:D + (h + 1) * MEM_HD].astype(bf16)
            s = _dot_nt(q_ref[:, sl], kh) * scale
            e = jnp.exp(s - jnp.max(s, axis=1, keepdims=True))
            p = e / jnp.sum(e, axis=1, keepdims=True)
            o_ref[:, sl] = _dot(p.astype(bf16), vh).astype(bf16)

    return pl.pallas_call(
        body, grid=(T // tm,),
        in_specs=[pl.BlockSpec((tm, D), lambda i: (i, 0)), pl.BlockSpec((M, 2 * D), lambda i: (0, 0))],
        out_specs=pl.BlockSpec((tm, D), lambda i: (i, 0)), out_shape=_sds((T, D), bf16),
        compiler_params=_cparams(("parallel",)), name=name)(q, kv)


def mem_bwd(do, q, kv, name):
    T = q.shape[0]
    M = kv.shape[0]
    tm = min(T, 512)
    scale = MEM_HD ** -0.5

    def body(do_ref, q_ref, kv_ref, dq_ref, dkv_ref):
        @pl.when(pl.program_id(0) == 0)
        def _():
            dkv_ref[...] = jnp.zeros_like(dkv_ref)
        for h in range(4):
            sl = slice(h * MEM_HD, (h + 1) * MEM_HD)
            vsl = slice(D + h * MEM_HD, D + (h + 1) * MEM_HD)
            qh = q_ref[:, sl]
            kh = kv_ref[:, sl].astype(bf16)
            vh = kv_ref[:, vsl].astype(bf16)
            doh = do_ref[:, sl].astype(bf16)
            s = _dot_nt(qh, kh) * scale
            e = jnp.exp(s - jnp.max(s, axis=1, keepdims=True))
            p = e / jnp.sum(e, axis=1, keepdims=True)
            dp = _dot_nt(doh, vh)
            ds = p * (dp - jnp.sum(dp * p, axis=1, keepdims=True))
            dsb = ds.astype(bf16)
            dq_ref[:, sl] = _dot(dsb, kh) * scale
            dkv_ref[:, sl] += _dot_tn(dsb, qh) * scale
            dkv_ref[:, vsl] += _dot_tn(p.astype(bf16), doh)

    row = pl.BlockSpec((tm, D), lambda i: (i, 0))
    whole = pl.BlockSpec((M, 2 * D), lambda i: (0, 0))
    return pl.pallas_call(
        body, grid=(T // tm,), in_specs=[row, row, whole], out_specs=[row, whole],
        out_shape=[_sds((T, D)), _sds((M, 2 * D))],
        compiler_params=_cparams(("arbitrary",)), name=name)(do, q, kv)


def _chip_peers():
    x, y, c = lax.axis_index("x"), lax.axis_index("y"), lax.axis_index("c")
    me = 2 * x + y
    peers = [((1 - x, y, c), 2 * (1 - x) + y), ((x, 1 - y, c), 2 * x + (1 - y)), ((1 - x, 1 - y, c), 2 * (1 - x) + (1 - y))]
    return me, peers


class Rider:
    def __init__(self, arrays, out_shape, scratch, start, finish):
        self.arrays, self.out_shape, self.scratch, self.start, self.finish = arrays, out_shape, scratch, start, finish
        self.n = len(arrays)

    def split(self, refs):
        return refs[:self.n], refs[self.n:2 * self.n], refs[2 * self.n:]


def run_rider(rider, name):
    def body(*refs):
        parts = rider.split(refs)
        rider.start(*parts)
        rider.finish(*parts)

    anyspec = pl.BlockSpec(memory_space=pl.ANY)
    return pl.pallas_call(
        body, in_specs=[anyspec] * rider.n, out_specs=[anyspec] * rider.n, out_shape=rider.out_shape,
        scratch_shapes=rider.scratch, name=name)(*rider.arrays)


def gather_rider(arrs, split):
    n = len(arrs)

    def copies(ins, outs, sems):
        send_sems, recv_sems, pass_send, pass_recv, loc_sems = sems
        x, y, c = lax.axis_index("x"), lax.axis_index("y"), lax.axis_index("c")
        me, peers = _chip_peers()

        def mine(ref, i):
            if not split[i]:
                return ref
            r = arrs[i].shape[1] // 2
            return ref.at[:, pl.ds(c * r, r), :]

        local, fetch, passed = [], [], []
        for i in range(n):
            local.append(pltpu.make_async_copy(ins[i], outs[i].at[me], loc_sems.at[i]))
            for k, (dev, pj) in enumerate(peers):
                fetch.append(pltpu.make_async_remote_copy(src_ref=mine(ins[i], i), dst_ref=mine(outs[i].at[me], i),
                                                          send_sem=send_sems.at[i, k], recv_sem=recv_sems.at[i, k],
                                                          device_id=dev, device_id_type=MESH))
                rows = mine(outs[i].at[pj], i)
                passed.append(pltpu.make_async_remote_copy(src_ref=rows, dst_ref=rows, send_sem=pass_send.at[i, k],
                                                           recv_sem=pass_recv.at[i, k], device_id=(x, y, 1 - c),
                                                           device_id_type=MESH) if split[i] else None)
        return local, fetch, passed

    def start(ins, outs, sems):
        local, fetch, _ = copies(ins, outs, sems)
        for cp in local + fetch:
            cp.start()

    def finish(ins, outs, sems):
        local, fetch, passed = copies(ins, outs, sems)
        for cp, fw in zip(fetch, passed):
            cp.wait_recv()
            if fw is not None:
                fw.start()
        for cp in fetch:
            cp.wait_send()
        for cp in [fw for fw in passed if fw is not None] + local:
            cp.wait()

    sem = pltpu.SemaphoreType.DMA((n, 3))
    return Rider(list(arrs), [_sds((4,) + a.shape, a.dtype) for a in arrs],
                 [sem, sem, sem, sem, pltpu.SemaphoreType.DMA((n,))], start, finish)


def exchange_rider(stacks):
    n = len(stacks)

    def copies(ins, outs, sems):
        send_sems, recv_sems, loc_sems = sems
        me, peers = _chip_peers()
        out = []
        for i in range(n):
            out.append(pltpu.make_async_copy(ins[i].at[me], outs[i].at[me], loc_sems.at[i]))
            for k, (dev, pj) in enumerate(peers):
                out.append(pltpu.make_async_remote_copy(src_ref=ins[i].at[pj], dst_ref=outs[i].at[me], send_sem=send_sems.at[i, k],
                                                        recv_sem=recv_sems.at[i, k], device_id=dev, device_id_type=MESH))
        return out

    def start(ins, outs, sems):
        for cp in copies(ins, outs, sems):
            cp.start()

    def finish(ins, outs, sems):
        for cp in copies(ins, outs, sems):
            cp.wait()

    return Rider(list(stacks), [_sds(a.shape, a.dtype) for a in stacks],
                 [pltpu.SemaphoreType.DMA((n, 3)), pltpu.SemaphoreType.DMA((n, 3)), pltpu.SemaphoreType.DMA((n,))], start, finish)


def swap_sibling(arrs, name):
    n = len(arrs)

    def body(*refs):
        ins, outs = refs[:n], refs[n:2 * n]
        send_sems, recv_sems = refs[2 * n:]
        x, y, c = lax.axis_index("x"), lax.axis_index("y"), lax.axis_index("c")
        started = []
        for i in range(n):
            cp = pltpu.make_async_remote_copy(src_ref=ins[i], dst_ref=outs[i], send_sem=send_sems.at[i],
                                              recv_sem=recv_sems.at[i], device_id=(x, y, 1 - c), device_id_type=MESH)
            cp.start()
            started.append(cp)
        for cp in started:
            cp.wait()

    anyspec = pl.BlockSpec(memory_space=pl.ANY)
    return pl.pallas_call(
        body, in_specs=[anyspec] * n, out_specs=[anyspec] * n,
        out_shape=[_sds(a.shape, a.dtype) for a in arrs],
        scratch_shapes=[pltpu.SemaphoreType.DMA((n,)), pltpu.SemaphoreType.DMA((n,))],
        name=name)(*arrs)


def gather_all(a, name):
    def body(a_ref, o_ref, send_sems, recv_sems, loc_sem):
        x, y, c = lax.axis_index("x"), lax.axis_index("y"), lax.axis_index("c")
        me = 4 * x + 2 * y + c
        loc = pltpu.make_async_copy(a_ref, o_ref.at[me], loc_sem)
        loc.start()
        started = [loc]
        for k in range(1, 8):
            dev = (x ^ (k >> 2), y ^ ((k >> 1) & 1), c ^ (k & 1))
            cp = pltpu.make_async_remote_copy(src_ref=a_ref, dst_ref=o_ref.at[me], send_sem=send_sems.at[k - 1],
                                              recv_sem=recv_sems.at[k - 1], device_id=dev, device_id_type=MESH)
            cp.start()
            started.append(cp)
        for cp in started:
            cp.wait()

    anyspec = pl.BlockSpec(memory_space=pl.ANY)
    return pl.pallas_call(
        body, in_specs=[anyspec], out_specs=anyspec, out_shape=_sds((8,) + a.shape, a.dtype),
        scratch_shapes=[pltpu.SemaphoreType.DMA((7,)), pltpu.SemaphoreType.DMA((7,)), pltpu.SemaphoreType.DMA(())],
        name=name)(a)


def sum_slots(stack, name):
    n, R, C = stack.shape
    tr = R if R <= 512 else _pick(R, (512, 352, 256))

    def body(s_ref, o_ref):
        acc = s_ref[0].astype(f32)
        for j in range(1, n):
            acc = acc + s_ref[j].astype(f32)
        o_ref[...] = acc

    return pl.pallas_call(
        body, grid=(R // tr,), in_specs=[pl.BlockSpec((n, tr, C), lambda i: (0, i, 0))],
        out_specs=pl.BlockSpec((tr, C), lambda i: (i, 0)), out_shape=_sds((R, C)),
        compiler_params=_cparams(("parallel",)), name=name)(stack)


def adamw(w, g1, g2, m, v, name):
    R, C = w.shape
    tr = R if R <= 512 else _pick(R, (512, 352, 256))
    c1 = 1.0 - ADAM_B1 ** ADAM_STEP
    c2 = 1.0 - ADAM_B2 ** ADAM_STEP

    def body(*refs):
        if g2 is None:
            w_ref, g1_ref, m_ref, v_ref, g_out, d_out, m_out, v_out = refs
            g = g1_ref[...]
        else:
            w_ref, g1_ref, g2_ref, m_ref, v_ref, g_out, d_out, m_out, v_out = refs
            g = g1_ref[...] + g2_ref[...]
        mn = ADAM_B1 * m_ref[...] + (1.0 - ADAM_B1) * g
        vn = ADAM_B2 * v_ref[...] + (1.0 - ADAM_B2) * (g * g)
        g_out[...] = g
        m_out[...] = mn
        v_out[...] = vn
        d_out[...] = -ADAM_LR * ((mn / c1) / (jnp.sqrt(vn / c2) + ADAM_EPS) + ADAM_WD * w_ref[...])

    blk = pl.BlockSpec((tr, C), lambda i: (i, 0))
    args = [w, g1] + ([] if g2 is None else [g2]) + [m, v]
    return pl.pallas_call(
        body, grid=(R // tr,), in_specs=[blk] * len(args), out_specs=[blk] * 4, out_shape=[_sds((R, C))] * 4,
        compiler_params=_cparams(("parallel",)), name=name)(*args)


IN_SPLITS = (768, 256, 4, 4, 768, 4, 512, 768)
IN_OFF = (0, 768, 1024, 1028, 1032, 1800, 1804, 2316, 3084)


def regroup_w_in(w):
    seg = lambda i: w[:, IN_OFF[i]:IN_OFF[i + 1]]
    pad = jnp.zeros((w.shape[0], PW - C_SMALL - 12), w.dtype)
    return jnp.concatenate([seg(0), seg(1), seg(4), seg(6), seg(7), seg(2), seg(3), seg(5), pad], axis=1)


def ungroup_w_in(g):
    s = C_SMALL
    return jnp.concatenate([g[:, 0:1024], g[:, s:s + 8], g[:, 1024:1792], g[:, s + 8:s + 12], g[:, 1792:3072]], axis=1)


def to_hm(t, nh):
    T = t.shape[0]
    return t.reshape(T, nh, HD).transpose(1, 0, 2)


def from_hm(t):
    nh, T, _ = t.shape
    return t.transpose(1, 0, 2).reshape(T, nh * HD)


def col_shards(w):
    c = w.shape[-1] // 4
    return jnp.moveaxis(w.reshape(w.shape[:-1] + (4, c)), -2, 0)


def row_shards(w):
    L, r4, c = w.shape
    return w.reshape(L, 4, r4 // 4, c).transpose(1, 0, 2, 3)


def join_cols(g):
    return jnp.moveaxis(g, 0, -2).reshape(g.shape[1:-1] + (4 * g.shape[-1],))


def join_rows(g):
    _, L, r, c = g.shape
    return g.transpose(1, 0, 2, 3).reshape(L, 4 * r, c)


COL_SHARDED = ("ffn1_w_gate", "ffn1_w_up", "w_in", "gdn_conv_w", "conf_dw_w", "mem_w_kv", "ffn2_w_gate", "ffn2_w_up")
CONV_WEIGHTS = ("gdn_conv_w", "conf_dw_w")
ROW_SHARDED = ("ffn1_w_down", "w_out", "mem_w_q", "mem_w_o", "ffn2_w_down")
REPLICATED = ("ln_ffn1_g", "ln_ffn1_b", "gdn_a_log", "gdn_dt_bias", "gdn_norm_g", "fox_b_f", "conf_dw_b", "conf_norm_g",
              "conf_norm_b", "ln_mix_g", "ln_mix_b", "ln_mem_g", "ln_mem_b", "ln_ffn2_g", "ln_ffn2_b")
WEIGHTS = ("ffn1_w_gate", "ffn1_w_up", "ffn1_w_down", "ln_ffn1_g", "ln_ffn1_b", "w_in", "gdn_conv_w", "gdn_a_log",
           "gdn_dt_bias", "gdn_norm_g", "fox_b_f", "conf_dw_w", "conf_dw_b", "conf_norm_g", "conf_norm_b", "w_out",
           "ln_mix_g", "ln_mix_b", "mem_w_q", "mem_w_kv", "mem_w_o", "ln_mem_g", "ln_mem_b", "ffn2_w_gate",
           "ffn2_w_up", "ffn2_w_down", "ln_ffn2_g", "ln_ffn2_b")


def pack_small(d):
    flat = jnp.concatenate([d[n].reshape(-1) for n in REPLICATED])
    rows = -(-flat.shape[0] // 1024) * 8
    return jnp.pad(flat, (0, rows * 128 - flat.shape[0])).reshape(rows, 128)


def unpack_small(p, like):
    flat = p.reshape(-1)
    out, o = {}, 0
    for n in REPLICATED:
        sz = like[n].size
        out[n] = flat[o:o + sz].reshape(like[n].shape)
        o += sz
    return out


def _vec(v):
    return v.reshape(1, -1)


def _pad_rows(w, rows):
    return jnp.pad(w, ((0, rows - w.shape[0]), (0, 0)))


def _small_lane_vec(v4, lane0):
    return jnp.pad(v4.reshape(1, -1), ((0, 0), (lane0, 128 - lane0 - v4.shape[0])))


def layer_fwd(x0, mem, W, li, rider=None, ffn1_rider=None, complete=None):
    T = x0.shape[0]
    tk = min(T, ATT_TK)
    n = lambda s: f"l{li}_{s}"
    R = {"x0": x0}
    R["z1"], x1, got = ffn_fwd(x0, W["ffn1_w_gate"], W["ffn1_w_up"], W["ffn1_w_down"], _vec(W["ln_ffn1_g"]), _vec(W["ln_ffn1_b"]),
                               n("ffn1_fwd"), ffn1_rider)
    if ffn1_rider is not None:
        W = complete(W, got)
    R["x1"] = x1
    proj, pbf = mm_nn(x1, W["w_in_r"], n("proj"), also_bf16=True)
    R["proj"], R["pbf"] = proj, pbf

    w8 = _pad_rows(W["gdn_conv_w"], 8)
    qkv_hm = gdn_conv_fwd(proj, w8, n("gdn_conv_fwd"))
    alog = W["gdn_a_log"].reshape(NH, 1, 1)
    dtb = W["gdn_dt_bias"].reshape(NH, 1, 1)
    ng = W["gdn_norm_g"].reshape(1, 1, HD)
    ya, states = gdn_fwd(qkv_hm, proj, alog, dtb, ng, n("gdn_fwd"))
    R.update(qkv_hm=qkv_hm, states=states)

    bfv = _small_lane_vec(W["fox_b_f"], F_LANE)
    cum = fox_gate_fwd(proj, bfv, n("fox_gate_fwd"))
    cum4 = cum[:, F_LANE:F_LANE + NH]
    cumrow = jnp.pad(cum4.T.reshape(2, 2, T // tk, tk).transpose(0, 2, 1, 3), ((0, 0), (0, 0), (0, 6), (0, 0)))
    yb, lse = fox_fwd(pbf, cumrow, n("fox_fwd"))
    R.update(cumrow=cumrow, yb=yb, lse=lse)

    w32 = _pad_rows(W["conf_dw_w"], 32)
    yc, cc = conf_fwd(proj, w32, _vec(W["conf_dw_b"]), _vec(W["conf_norm_g"]), _vec(W["conf_norm_b"]), n("conf_fwd"))
    R["cc"] = cc

    yd, tot, rider_out = sb_fwd(pbf, n("sb_fwd"), rider)
    R["tot"] = tot

    ycat = jnp.concatenate([ya, yb, yc, yd], axis=1).astype(bf16)
    R["ycat"] = ycat
    R["z2"], x2 = lin_res_ln(ycat, W["w_out"], x1, _vec(W["ln_mix_g"]), _vec(W["ln_mix_b"]), n("mix_out"))
    R["x2"] = x2

    qm = mm_nn(x2, W["mem_w_q"], n("mem_q"), out_dtype=bf16)
    kv = mm_nn(mem, W["mem_w_kv"], n("mem_kv"))
    om = mem_fwd(qm, kv, n("mem_fwd"))
    R.update(qm=qm, kv=kv, om=om)
    R["z3"], x3 = lin_res_ln(om, W["mem_w_o"], x2, _vec(W["ln_mem_g"]), _vec(W["ln_mem_b"]), n("mem_out"))
    R["x3"] = x3
    R["z4"], x4, _ = ffn_fwd(x3, W["ffn2_w_gate"], W["ffn2_w_up"], W["ffn2_w_down"], _vec(W["ln_ffn2_g"]), _vec(W["ln_ffn2_b"]), n("ffn2_fwd"))
    return x4, R, rider_out, W


EARLY_GRADS = ("ffn2_w_gate", "ffn2_w_up", "ffn2_w_down", "mem_w_q", "mem_w_kv", "mem_w_o", "w_out")
LATE_GRADS = ("w_in", "gdn_conv_w", "conf_dw_w")
LAST_GRADS = ("ffn1_w_gate", "ffn1_w_up", "ffn1_w_down")


def grad_slots(G, names):
    out = []
    for k in names:
        s = col_shards(G[k][None]) if k in COL_SHARDED else row_shards(G[k][None])
        out.append(s if k in CONV_WEIGHTS else s.astype(bf16))
    return out


def layer_bwd(dx4, mem, W, R, li, rider=None, exchange=True):
    T = dx4.shape[0]
    n = lambda s: f"l{li}_{s}"
    G = {}

    def ffn_back(dy, z, x, pre, tag, ride=None):
        dz, dg, db = ln_bwd(dy, z, _vec(W[f"ln_{pre}_g"]), n(f"{tag}_ln_bwd"))
        (dx, a, dh, du, dzh), got = ffn_bwd(dz, x, W[f"{pre}_w_gate"], W[f"{pre}_w_up"], W[f"{pre}_w_down"], n(f"{tag}_bwd"), ride)
        G[f"{pre}_w_gate"] = mm_tn(x, dh, n(f"{tag}_dwg"))
        G[f"{pre}_w_up"] = mm_tn(x, du, n(f"{tag}_dwu"))
        G[f"{pre}_w_down"] = mm_tn(a, dzh, n(f"{tag}_dwd"))
        G[f"ln_{pre}_g"], G[f"ln_{pre}_b"] = dg.reshape(-1), db.reshape(-1)
        return dx, got

    dx3, _ = ffn_back(dx4, R["z4"], R["x3"], "ffn2", "ffn2")

    dz3, dg, db = ln_bwd(dx3, R["z3"], _vec(W["ln_mem_g"]), n("mem_ln_bwd"))
    G["ln_mem_g"], G["ln_mem_b"] = dg.reshape(-1), db.reshape(-1)
    dom = mm_nt(dz3, W["mem_w_o"], n("mem_dom"))
    G["mem_w_o"] = mm_tn(R["om"], dz3, n("mem_dwo"))
    dqm, dkv = mem_bwd(dom, R["qm"], R["kv"], n("mem_bwd"))
    G["mem_w_q"] = mm_tn(R["x2"], dqm, n("mem_dwq"))
    G["mem_w_kv"] = mm_tn(mem, dkv, n("mem_dwkv"))
    dx2 = mm_nt(dqm, W["mem_w_q"], n("mem_dx"), add=dz3, add_scale=ALPHA)

    dz2, dg, db = ln_bwd(dx2, R["z2"], _vec(W["ln_mix_g"]), n("mix_ln_bwd"))
    G["ln_mix_g"], G["ln_mix_b"] = dg.reshape(-1), db.reshape(-1)
    dycat = mm_nt(dz2, W["w_out"], n("mix_dycat"))
    G["w_out"] = mm_tn(R["ycat"], dz2, n("mix_dwout"))
    dya, dyb, dyc, dyd = (dycat[:, i * GW:(i + 1) * GW] for i in range(4))
    proj, pbf = R["proj"], R["pbf"]

    alog = W["gdn_a_log"].reshape(NH, 1, 1)
    dtb = W["gdn_dt_bias"].reshape(NH, 1, 1)
    ng = W["gdn_norm_g"].reshape(1, 1, HD)
    dqkv_hm, dgz, dsmall_ab, dalog, ddtb, dng = gdn_bwd(dycat, R["states"], R["qkv_hm"], proj, alog, dtb, ng, n("gdn_bwd"))
    G["gdn_a_log"], G["gdn_dt_bias"], G["gdn_norm_g"] = dalog.reshape(-1), ddtb.reshape(-1), dng.reshape(-1)
    w8 = _pad_rows(W["gdn_conv_w"], 8)
    dgqkv, dw8 = gdn_conv_bwd(dqkv_hm, proj, w8, n("gdn_conv_bwd"))
    G["gdn_conv_w"] = dw8[:GDN_K]

    dfq, dfk, dfv, dcumrow, dcumq, got_early = fox_bwd(dyb, R["yb"], R["lse"], pbf, R["cumrow"], n("fox_bwd"),
                                                      exchange_rider(grad_slots(G, EARLY_GRADS)) if exchange else None)
    dcum4 = dcumrow[:, :, 0:2, :].transpose(0, 2, 1, 3).reshape(4, T).T
    dcum4 = dcum4 + dcumq[:, :, 0:2].transpose(1, 0, 2).reshape(T, 4)
    dcum = jnp.pad(dcum4, ((0, 0), (F_LANE, 128 - F_LANE - NH)))
    bfv = _small_lane_vec(W["fox_b_f"], F_LANE)
    dsmall_f, dbf = fox_gate_bwd(dcum, proj, bfv, n("fox_gate_bwd"))
    G["fox_b_f"] = dbf[0, F_LANE:F_LANE + NH]

    w32 = _pad_rows(W["conf_dw_w"], 32)
    dglu, dw32, dcb, dcg, dcbeta = conf_bwd(dyc, R["cc"], proj, w32, _vec(W["conf_norm_g"]), _vec(W["conf_norm_b"]), n("conf_bwd"))
    G["conf_dw_w"], G["conf_dw_b"] = dw32[:CONF_K], dcb.reshape(-1)
    G["conf_norm_g"], G["conf_norm_b"] = dcg.reshape(-1), dcbeta.reshape(-1)

    dsq, dsk, dsv, got_carried = sb_bwd(dyd, R["tot"], pbf, n("sb_bwd"), rider)

    dsmall = dsmall_ab + dsmall_f
    dproj = [dgqkv, dgz, dfq, dfk, dfv, dglu, dsq, dsk, dsv, dsmall]
    G["w_in"] = ungroup_w_in(mm_tn(R["x1"], dproj, n("proj_dw")))
    dx1 = mm_nt(dproj, W["w_in_r"], n("proj_dx"), add=dz2, add_scale=ALPHA)

    dx0, got_late = ffn_back(dx1, R["z1"], R["x0"], "ffn1", "ffn1", exchange_rider(grad_slots(G, LATE_GRADS)) if exchange else None)
    return dx0, G, {"carried": got_carried, "early": got_early, "late": got_late}


def _step(P, M, V, x, mem, loss_target):
    xs, mems, tgt = x[0], mem[0], loss_target[0]

    big = COL_SHARDED + ROW_SHARDED

    def weight_gather(li, names):
        return gather_rider([P[k][li:li + 1] if k in CONV_WEIGHTS else P[k][li:li + 1].astype(bf16) for k in names],
                            [k not in CONV_WEIGHTS for k in names])

    def layer_weights(li, names, gathered, W=None):
        W = dict(W) if W else {k: P[k][li] for k in REPLICATED}
        W.update({k: (join_cols(g) if k in COL_SHARDED else join_rows(g))[0] for k, g in zip(names, gathered)})
        if "w_in" in names:
            W["w_in_r"] = regroup_w_in(W["w_in"])
        return W

    rest = tuple(k for k in big if k not in LAST_GRADS)
    W = layer_weights(0, LAST_GRADS, run_rider(weight_gather(0, LAST_GRADS), "gather_weights_first"))
    layers, saved = [], []
    h = xs
    for li in range(DEPTH):
        nxt = weight_gather(li + 1, big) if li + 1 < DEPTH else None
        if li == 0:
            h, R, gathered, W = layer_fwd(h, mems, W, li, nxt, weight_gather(0, rest),
                                          lambda W0, got: layer_weights(0, rest, got, W0))
        else:
            h, R, gathered, W = layer_fwd(h, mems, W, li, nxt)
        layers.append(W)
        saved.append(R)
        if li + 1 < DEPTH:
            W = layer_weights(li + 1, big, gathered)
    loss_row, dy = loss_and_grad(h, tgt, "loss")
    loss = lax.psum(loss_row[0, 0], ("x", "y", "c"))

    grads, received_at = [None] * DEPTH, [dict() for _ in range(DEPTH)]
    rider = None
    for li in reversed(range(DEPTH)):
        dy, G, got = layer_bwd(dy, mems, layers[li], saved[li], li, rider)
        if rider is not None:
            received_at[li + 1].update(zip(LAST_GRADS, got["carried"]))
        received_at[li].update(zip(EARLY_GRADS, got["early"]))
        received_at[li].update(zip(LATE_GRADS, got["late"]))
        grads[li] = G
        rider = exchange_rider(grad_slots(G, LAST_GRADS))
    received_at[0].update(zip(LAST_GRADS, run_rider(rider, "exchange_grads_last")))
    grad_x = dy[None]

    stacked = {k: jnp.stack([grads[li][k] for li in range(DEPTH)]) for k in WEIGHTS}
    received = [jnp.concatenate([received_at[li][k] for li in range(DEPTH)], axis=1) for k in big]
    partial_sums = []
    for k, r in zip(big, received):
        shp = r.shape
        partial_sums.append(sum_slots(r.reshape(4, -1, shp[-1]), f"sum_{k}"))
    from_sibling = swap_sibling(partial_sums, "swap_partials")

    out_g, out_d, out_m, out_v = {}, {}, {}, {}
    for k, mine, theirs in zip(big, partial_sums, from_sibling):
        shp = P[k].shape
        flat = lambda t: t.reshape(-1, shp[-1])
        g, d, mn, vn = adamw(flat(P[k]), mine, theirs, flat(M[k]), flat(V[k]), f"adamw_{k}")
        out_g[k], out_d[k], out_m[k], out_v[k] = (t.reshape(shp) for t in (g, d, mn, vn))

    gsmall = sum_slots(gather_all(pack_small(stacked), "gather_small"), "sum_small")
    g, d, mn, vn = adamw(pack_small(P), gsmall, None, pack_small(M), pack_small(V), "adamw_small")
    for dst, packed in ((out_g, g), (out_d, d), (out_m, mn), (out_v, vn)):
        dst.update(unpack_small(packed, P))

    return (loss, grad_x, *[out_g[k] for k in WEIGHTS], *[out_d[k] for k in WEIGHTS],
            *[out_m[k] for k in WEIGHTS], *[out_v[k] for k in WEIGHTS])


def kernel(x, mem, ffn1_w_gate, ffn1_w_up, ffn1_w_down, ln_ffn1_g, ln_ffn1_b, w_in, gdn_conv_w, gdn_a_log, gdn_dt_bias, gdn_norm_g, fox_b_f, conf_dw_w, conf_dw_b, conf_norm_g, conf_norm_b, w_out, ln_mix_g, ln_mix_b, mem_w_q, mem_w_kv, mem_w_o, ln_mem_g, ln_mem_b, ffn2_w_gate, ffn2_w_up, ffn2_w_down, ln_ffn2_g, ln_ffn2_b, loss_target, m_ffn1_w_gate, m_ffn1_w_up, m_ffn1_w_down, m_ln_ffn1_g, m_ln_ffn1_b, m_w_in, m_gdn_conv_w, m_gdn_a_log, m_gdn_dt_bias, m_gdn_norm_g, m_fox_b_f, m_conf_dw_w, m_conf_dw_b, m_conf_norm_g, m_conf_norm_b, m_w_out, m_ln_mix_g, m_ln_mix_b, m_mem_w_q, m_mem_w_kv, m_mem_w_o, m_ln_mem_g, m_ln_mem_b, m_ffn2_w_gate, m_ffn2_w_up, m_ffn2_w_down, m_ln_ffn2_g, m_ln_ffn2_b, v_ffn1_w_gate, v_ffn1_w_up, v_ffn1_w_down, v_ln_ffn1_g, v_ln_ffn1_b, v_w_in, v_gdn_conv_w, v_gdn_a_log, v_gdn_dt_bias, v_gdn_norm_g, v_fox_b_f, v_conf_dw_w, v_conf_dw_b, v_conf_norm_g, v_conf_norm_b, v_w_out, v_ln_mix_g, v_ln_mix_b, v_mem_w_q, v_mem_w_kv, v_mem_w_o, v_ln_mem_g, v_ln_mem_b, v_ffn2_w_gate, v_ffn2_w_up, v_ffn2_w_down, v_ln_ffn2_g, v_ln_ffn2_b):
    a = locals()
    P = {k: a[k] for k in WEIGHTS}
    M = {k: a["m_" + k] for k in WEIGHTS}
    V = {k: a["v_" + k] for k in WEIGHTS}
    return _step(P, M, V, x, mem, loss_target)
```

```python
import functools

import jax
import jax.numpy as jnp
from jax import lax
from jax.experimental import pallas as pl
from jax.experimental.pallas import tpu as pltpu

f32 = jnp.float32
bf16 = jnp.bfloat16

D = 1024
F = 2816
GW = 256
HD = 64
NH = 4
CHUNK = 64
CONF_K = 31
GDN_K = 4
DEPTH = 2
ALPHA = float((2 * DEPTH) ** 0.25)
LN_EPS = 1e-5
RMS_EPS = 1e-6
L2_EPS = 1e-6
NEG = -1e30
PW = 3200
C_GQKV, C_GZ, C_FOX, C_CONF, C_SB, C_SMALL = 0, 768, 1024, 1792, 2304, 3072
ADAM_LR, ADAM_B1, ADAM_B2, ADAM_EPS, ADAM_WD, ADAM_STEP = 0.001, 0.9, 0.999, 1e-08, 0.01, 10
VMEM_LIMIT = 56 * 1024 * 1024
MESH = pl.DeviceIdType.MESH


def _cparams(sem):
    return pltpu.CompilerParams(dimension_semantics=sem, vmem_limit_bytes=VMEM_LIMIT)


def _pick(n, cands):
    for c in cands:
        if n % c == 0:
            return c
    return n


def _sds(shape, dtype=f32):
    return jax.ShapeDtypeStruct(shape, dtype)


def _layer_norm(z, g, b):
    mu = jnp.mean(z, axis=-1, keepdims=True)
    zc = z - mu
    var = jnp.mean(zc * zc, axis=-1, keepdims=True)
    return zc * lax.rsqrt(var + LN_EPS) * g + b


def _softplus(x):
    return jnp.maximum(x, 0.0) + jnp.log(1.0 + jnp.exp(-jnp.abs(x)))


def _neg_softplus(z):
    nz = -z
    return jnp.minimum(nz, 0.0) - jnp.log(1.0 + jnp.exp(jnp.minimum(z, nz)))


def _dsilu(x):
    s = jax.nn.sigmoid(x)
    return s * (1.0 + x * (1.0 - s))


def _split_hi_lo(x):
    hi = x.astype(bf16)
    lo = (x - hi.astype(f32)).astype(bf16)
    return hi, lo


def _dot(a, b):
    return jnp.dot(a, b, preferred_element_type=f32)


def _dot_nt(a, b):
    return lax.dot_general(a, b, (((1,), (1,)), ((), ())), preferred_element_type=f32)


def _dot_tn(a, b):
    return lax.dot_general(a, b, (((0,), (0,)), ((), ())), preferred_element_type=f32)


def mm_nn(a, w, name, out_dtype=f32, also_bf16=False):
    T, K = a.shape
    N = w.shape[1]
    tm = min(T, 512)
    tn = N if N <= 1024 else _pick(N, (640, 512))

    def body(a_ref, w_ref, *o_refs):
        r = _dot(a_ref[...].astype(bf16), w_ref[...].astype(bf16))
        o_refs[0][...] = r.astype(o_refs[0].dtype)
        if also_bf16:
            o_refs[1][...] = r.astype(bf16)

    out_shape = [_sds((T, N), out_dtype)]
    out_specs = [pl.BlockSpec((tm, tn), lambda i, j: (i, j))]
    if also_bf16:
        out_shape.append(_sds((T, N), bf16))
        out_specs.append(pl.BlockSpec((tm, tn), lambda i, j: (i, j)))
    res = pl.pallas_call(
        body, grid=(T // tm, N // tn),
        in_specs=[pl.BlockSpec((tm, K), lambda i, j: (i, 0)), pl.BlockSpec((K, tn), lambda i, j: (0, j))],
        out_specs=out_specs, out_shape=out_shape,
        compiler_params=_cparams(("parallel", "arbitrary")), name=name)(a, w)
    return res if also_bf16 else res[0]


def _pieces(g):
    gs = list(g) if isinstance(g, (list, tuple)) else [g]
    offs = [sum(p.shape[1] for p in gs[:i]) for i in range(len(gs))]
    return gs, offs, offs[-1] + gs[-1].shape[1]


def mm_nt(g, w, name, add=None, add_scale=1.0):
    gs, offs, N = _pieces(g)
    T = gs[0].shape[0]
    K = w.shape[0]
    tm = min(T, 512)
    ng = len(gs)

    def body(*refs):
        w_ref, o_ref = refs[ng], refs[-1]
        r = None
        for g_ref, off in zip(refs[:ng], offs):
            part = _dot_nt(g_ref[...].astype(bf16), w_ref[:, off:off + g_ref.shape[1]].astype(bf16))
            r = part if r is None else r + part
        if add is not None:
            r = r + add_scale * refs[ng + 1][...]
        o_ref[...] = r

    in_specs = [pl.BlockSpec((tm, p.shape[1]), lambda i: (i, 0)) for p in gs] + [pl.BlockSpec((K, N), lambda i: (0, 0))]
    args = gs + [w]
    if add is not None:
        in_specs.append(pl.BlockSpec((tm, K), lambda i: (i, 0)))
        args.append(add)
    return pl.pallas_call(
        body, grid=(T // tm,), in_specs=in_specs,
        out_specs=pl.BlockSpec((tm, K), lambda i: (i, 0)), out_shape=_sds((T, K)),
        compiler_params=_cparams(("parallel",)), name=name)(*args)


def mm_tn(a, g, name):
    gs, offs, N = _pieces(g)
    T, K = a.shape
    tk = K if K * N * 4 <= 14 * 1024 * 1024 else _pick(K, (512, 1408))
    row_bytes = tk * a.dtype.itemsize + sum(p.shape[1] * p.dtype.itemsize for p in gs)
    tt = min(T, 1024 if 2 * (tk * N * 4 + 1024 * row_bytes) <= 44 * 1024 * 1024 else 512)

    def body(a_ref, *refs):
        o_ref = refs[-1]

        @pl.when(pl.program_id(1) == 0)
        def _():
            o_ref[...] = jnp.zeros_like(o_ref)
        if len(gs) == 1:
            o_ref[...] += _dot_tn(a_ref[...].astype(bf16), refs[0][...].astype(bf16))
        else:
            at = a_ref[...].astype(bf16).T
            for g_ref, off in zip(refs[:-1], offs):
                o_ref[:, off:off + g_ref.shape[1]] += _dot(at, g_ref[...].astype(bf16))

    return pl.pallas_call(
        body, grid=(K // tk, T // tt),
        in_specs=[pl.BlockSpec((tt, tk), lambda j, t: (t, j))] + [pl.BlockSpec((tt, p.shape[1]), lambda j, t: (t, 0)) for p in gs],
        out_specs=pl.BlockSpec((tk, N), lambda j, t: (j, 0)), out_shape=_sds((K, N)),
        compiler_params=_cparams(("parallel", "arbitrary")), name=name)(a, *gs)


FFN_TF = 1408


def ffn_fwd(x, wg, wu, wd, g, b, name, rider=None):
    T = x.shape[0]
    tm = min(T, 512)
    nf = F // FFN_TF

    def body(x_ref, wg_ref, wu_ref, wd_ref, g_ref, b_ref, z_ref, y_ref, acc):
        j = pl.program_id(1)

        @pl.when(j == 0)
        def _():
            acc[...] = jnp.zeros_like(acc)

        xb = x_ref[...].astype(bf16)
        h = _dot(xb, wg_ref[...])
        u = _dot(xb, wu_ref[...])
        a = (h * jax.nn.sigmoid(h) * u).astype(bf16)
        acc[...] += _dot(a, wd_ref[...])

        @pl.when(j == nf - 1)
        def _():
            z = ALPHA * x_ref[...] + 0.5 * acc[...]
            z_ref[...] = z
            y_ref[...] = _layer_norm(z, g_ref[...], b_ref[...])

    row = pl.BlockSpec((tm, D), lambda i, j: (i, 0))
    vec = pl.BlockSpec((1, D), lambda i, j: (0, 0))
    body, r_in, r_args, r_out, r_shape, r_scratch = _ride(body, 6, 2, rider, (T // tm, nf), 1)
    res = pl.pallas_call(
        body, grid=(T // tm, nf),
        in_specs=[row, pl.BlockSpec((D, FFN_TF), lambda i, j: (0, j)), pl.BlockSpec((D, FFN_TF), lambda i, j: (0, j)),
                  pl.BlockSpec((FFN_TF, D), lambda i, j: (j, 0)), vec, vec] + r_in,
        out_specs=[row, row] + r_out, out_shape=[_sds((T, D)), _sds((T, D))] + r_shape,
        scratch_shapes=[pltpu.VMEM((tm, D), f32)] + r_scratch,
        compiler_params=_cparams(("arbitrary", "arbitrary")), name=name)(x, wg, wu, wd, g, b, *r_args)
    return res[0], res[1], res[2:]


def _ln_bwd_rows(dy, zv, g):
    mu = jnp.mean(zv, axis=-1, keepdims=True)
    zc = zv - mu
    rstd = lax.rsqrt(jnp.mean(zc * zc, axis=-1, keepdims=True) + LN_EPS)
    xh = zc * rstd
    dxh = dy * g
    m1 = jnp.mean(dxh, axis=-1, keepdims=True)
    m2 = jnp.mean(dxh * xh, axis=-1, keepdims=True)
    return rstd * (dxh - m1 - xh * m2), jnp.sum(dy * xh, axis=0, keepdims=True), jnp.sum(dy, axis=0, keepdims=True)


def ffn_bwd(dz, x, wg, wu, wd, name, rider=None):
    T = x.shape[0]
    tm = min(T, 512)
    nf = F // FFN_TF

    def body(dz_ref, x_ref, wg_ref, wu_ref, wd_ref, dx_ref, a_ref, dh_ref, du_ref, dzh_ref, acc):
        j = pl.program_id(1)

        @pl.when(j == 0)
        def _():
            acc[...] = jnp.zeros_like(acc)

        dzh = (0.5 * dz_ref[...]).astype(bf16)
        xb = x_ref[...].astype(bf16)
        h = _dot(xb, wg_ref[...])
        u = _dot(xb, wu_ref[...])
        s = jax.nn.sigmoid(h)
        hs = h * s
        da = _dot_nt(dzh, wd_ref[...])
        du = (da * hs).astype(bf16)
        dh = (da * u * (s + hs * (1.0 - s))).astype(bf16)
        a_ref[...] = (hs * u).astype(bf16)
        dh_ref[...] = dh
        du_ref[...] = du
        acc[...] += _dot_nt(dh, wg_ref[...]) + _dot_nt(du, wu_ref[...])

        @pl.when(j == nf - 1)
        def _():
            dx_ref[...] = ALPHA * dz_ref[...] + acc[...]
            dzh_ref[...] = dzh

    row = pl.BlockSpec((tm, D), lambda i, j: (i, 0))
    wide = pl.BlockSpec((tm, FFN_TF), lambda i, j: (i, j))
    body, r_in, r_args, r_out, r_shape, r_scratch = _ride(body, 5, 5, rider, (T // tm, nf), 1)
    res = pl.pallas_call(
        body, grid=(T // tm, nf),
        in_specs=[row, row, pl.BlockSpec((D, FFN_TF), lambda i, j: (0, j)), pl.BlockSpec((D, FFN_TF), lambda i, j: (0, j)),
                  pl.BlockSpec((FFN_TF, D), lambda i, j: (j, 0))] + r_in,
        out_specs=[row, wide, wide, wide, row] + r_out,
        out_shape=[_sds((T, D)), _sds((T, F), bf16), _sds((T, F), bf16), _sds((T, F), bf16), _sds((T, D), bf16)] + r_shape,
        scratch_shapes=[pltpu.VMEM((tm, D), f32)] + r_scratch,
        compiler_params=_cparams(("arbitrary", "arbitrary")), name=name)(dz, x, wg, wu, wd, *r_args)
    return res[:5], res[5:]


def lin_res_ln(a, w, res, g, b, name):
    T, K = a.shape
    tm = min(T, 512)

    def body(a_ref, w_ref, res_ref, g_ref, b_ref, z_ref, y_ref):
        z = ALPHA * res_ref[...] + _dot(a_ref[...].astype(bf16), w_ref[...])
        z_ref[...] = z
        y_ref[...] = _layer_norm(z, g_ref[...], b_ref[...])

    row = pl.BlockSpec((tm, D), lambda i: (i, 0))
    vec = pl.BlockSpec((1, D), lambda i: (0, 0))
    return pl.pallas_call(
        body, grid=(T // tm,),
        in_specs=[pl.BlockSpec((tm, K), lambda i: (i, 0)), pl.BlockSpec((K, D), lambda i: (0, 0)), row, vec, vec],
        out_specs=[row, row], out_shape=[_sds((T, D)), _sds((T, D))],
        compiler_params=_cparams(("parallel",)), name=name)(a, w, res, g, b)


def ln_bwd(dy, z, g, name):
    T = z.shape[0]
    tm = min(T, 512)

    def body(dy_ref, z_ref, g_ref, dz_ref, dg_ref, db_ref):
        @pl.when(pl.program_id(0) == 0)
        def _():
            dg_ref[...] = jnp.zeros_like(dg_ref)
            db_ref[...] = jnp.zeros_like(db_ref)

        dz, dg, db = _ln_bwd_rows(dy_ref[...], z_ref[...], g_ref[...])
        dz_ref[...] = dz
        dg_ref[...] += dg
        db_ref[...] += db

    row = pl.BlockSpec((tm, D), lambda i: (i, 0))
    vec = pl.BlockSpec((1, D), lambda i: (0, 0))
    return pl.pallas_call(
        body, grid=(T // tm,), in_specs=[row, row, vec], out_specs=[row, vec, vec],
        out_shape=[_sds((T, D)), _sds((1, D)), _sds((1, D))],
        compiler_params=_cparams(("arbitrary",)), name=name)(dy, z, g)


def loss_and_grad(y, target, name):
    T = y.shape[0]
    tm = min(T, 512)

    def body(y_ref, t_ref, l_ref, dy_ref):
        @pl.when(pl.program_id(0) == 0)
        def _():
            l_ref[...] = jnp.zeros_like(l_ref)
        d = y_ref[...] - t_ref[...]
        dy_ref[...] = d * (1.0 / D)
        l_ref[...] += (0.5 / D) * jnp.sum(jnp.sum(d * d, axis=1, keepdims=True), axis=0, keepdims=True)

    row = pl.BlockSpec((tm, D), lambda i: (i, 0))
    return pl.pallas_call(
        body, grid=(T // tm,), in_specs=[row, row],
        out_specs=[pl.BlockSpec((1, 128), lambda i: (0, 0)), row],
        out_shape=[_sds((1, 128)), _sds((T, D))],
        compiler_params=_cparams(("arbitrary",)), name=name)(y, target)


def _shifted(ext, s):
    return ext if s == 0 else pltpu.roll(ext, s, 0)


def _halo_maps(tm, P, nblk):
    per = tm // P
    prev = lambda i, c: (jnp.maximum(i * per - 1, 0), c)
    nxt = lambda i, c: (jnp.minimum((i + 1) * per, nblk * per - 1), c)
    return prev, nxt


GDN_P = 8
CONF_P = 32


def gdn_conv_fwd(proj, w8, name):
    T = proj.shape[0]
    tm = min(T, 512)
    nblk = T // tm
    C = 3 * GW
    prev, _ = _halo_maps(tm, GDN_P, nblk)

    def body(xc_ref, xp_ref, w_ref, o_ref):
        i = pl.program_id(0)
        xp = jnp.where(i > 0, xp_ref[...], 0.0)
        ext = jnp.concatenate([xp, xc_ref[...]], axis=0)
        acc = jnp.zeros((tm, C), f32)
        for k in range(GDN_K):
            acc = acc + w_ref[k:k + 1, :] * _shifted(ext, GDN_K - 1 - k)[GDN_P:, :]
        y = acc * jax.nn.sigmoid(acc)
        for h in range(3 * NH):
            o_ref[h] = y[:, h * HD:(h + 1) * HD]

    return pl.pallas_call(
        body, grid=(nblk,),
        in_specs=[pl.BlockSpec((tm, C), lambda i: (i, 0)), pl.BlockSpec((GDN_P, C), lambda i: prev(i, 0)),
                  pl.BlockSpec((8, C), lambda i: (0, 0))],
        out_specs=pl.BlockSpec((3 * NH, tm, HD), lambda i: (0, i, 0)), out_shape=_sds((3 * NH, T, HD)),
        compiler_params=_cparams(("parallel",)), name=name)(proj, proj, w8)


def gdn_conv_bwd(dy, proj, w8, name):
    T = proj.shape[0]
    tm = min(T, 512)
    nblk = T // tm
    C = 3 * GW
    P = GDN_P
    prev, nxt = _halo_maps(tm, P, nblk)

    def body(dyc_ref, dyn_ref, xc_ref, xp_ref, xn_ref, w_ref, dx_ref, dw_ref):
        i = pl.program_id(0)

        @pl.when(i == 0)
        def _():
            dw_ref[...] = jnp.zeros_like(dw_ref)

        xp = jnp.where(i > 0, xp_ref[...], 0.0)
        last = i == nblk - 1
        xn = jnp.where(last, 0.0, xn_ref[...])
        dyn = jnp.where(last, 0.0, jnp.concatenate([dyn_ref[h] for h in range(3 * NH)], axis=1))
        ext = jnp.concatenate([xp, xc_ref[...], xn], axis=0)
        sh = [_shifted(ext, GDN_K - 1 - k)[P:, :] for k in range(GDN_K)]
        s = jnp.zeros((tm + P, C), f32)
        for k in range(GDN_K):
            s = s + w_ref[k:k + 1, :] * sh[k]
        dyc = jnp.concatenate([dyc_ref[h] for h in range(3 * NH)], axis=1)
        ds = jnp.concatenate([dyc, dyn], axis=0) * _dsilu(s)
        dx = jnp.zeros((tm, C), f32)
        for k in range(GDN_K):
            d = GDN_K - 1 - k
            moved = ds if d == 0 else pltpu.roll(ds, tm + P - d, 0)
            dx = dx + w_ref[k:k + 1, :] * moved[:tm, :]
            dw_ref[k:k + 1, :] += jnp.sum(ds[:tm, :] * sh[k][:tm, :], axis=0, keepdims=True)
        dx_ref[...] = dx

    col = lambda i: (i, 0)
    return pl.pallas_call(
        body, grid=(nblk,),
        in_specs=[pl.BlockSpec((3 * NH, tm, HD), lambda i: (0, i, 0)), pl.BlockSpec((3 * NH, P, HD), lambda i: (0, nxt(i, 0)[0], 0)),
                  pl.BlockSpec((tm, C), col), pl.BlockSpec((P, C), lambda i: prev(i, 0)),
                  pl.BlockSpec((P, C), lambda i: nxt(i, 0)), pl.BlockSpec((8, C), lambda i: (0, 0))],
        out_specs=[pl.BlockSpec((tm, C), col), pl.BlockSpec((8, C), lambda i: (0, 0))],
        out_shape=[_sds((T, C)), _sds((8, C))],
        compiler_params=_cparams(("arbitrary",)), name=name)(dy, dy, proj, proj, proj, w8)


def _group_ones():
    r = lax.broadcasted_iota(jnp.int32, (GW, GW), 0) // HD
    c = lax.broadcasted_iota(jnp.int32, (GW, GW), 1) // HD
    return (r == c).astype(bf16)


def _group_mean(x, ones):
    hi, lo = _split_hi_lo(x)
    return (_dot(hi, ones) + _dot(lo, ones)) * (1.0 / HD)


def _conf_norm(c, g, b, ones):
    mu = _group_mean(c, ones)
    cc = c - mu
    rstd = lax.rsqrt(_group_mean(cc * cc, ones) + LN_EPS)
    hn = cc * rstd
    return hn, rstd, hn * g + b


CONF_VAL_BLK = C_CONF // GW
CONF_GATE_BLK = C_CONF // GW + 1


def conf_fwd(proj, w32, bias, ng, nb, name):
    T = proj.shape[0]
    tm = min(T, 512)
    nblk = T // tm
    P = CONF_P
    prev, _ = _halo_maps(tm, P, nblk)

    def body(vc_ref, gc_ref, vp_ref, gp_ref, w_ref, bias_ref, ng_ref, nb_ref, y_ref, c_ref):
        i = pl.program_id(0)
        pc = vc_ref[...] * jax.nn.sigmoid(gc_ref[...])
        pp = jnp.where(i > 0, vp_ref[...] * jax.nn.sigmoid(gp_ref[...]), 0.0)
        ext = jnp.concatenate([pp, pc], axis=0)
        acc = jnp.zeros((tm, GW), f32)
        for k in range(CONF_K):
            acc = acc + w_ref[k:k + 1, :] * _shifted(ext, CONF_K - 1 - k)[P:, :]
        c = acc + bias_ref[...]
        c_ref[...] = c
        _, _, yn = _conf_norm(c, ng_ref[...], nb_ref[...], _group_ones())
        y_ref[...] = yn * jax.nn.sigmoid(yn)

    vec = pl.BlockSpec((1, GW), lambda i: (0, 0))
    return pl.pallas_call(
        body, grid=(nblk,),
        in_specs=[pl.BlockSpec((tm, GW), lambda i: (i, CONF_VAL_BLK)), pl.BlockSpec((tm, GW), lambda i: (i, CONF_GATE_BLK)),
                  pl.BlockSpec((P, GW), lambda i: prev(i, CONF_VAL_BLK)), pl.BlockSpec((P, GW), lambda i: prev(i, CONF_GATE_BLK)),
                  pl.BlockSpec((32, GW), lambda i: (0, 0)), vec, vec, vec],
        out_specs=[pl.BlockSpec((tm, GW), lambda i: (i, 0))] * 2, out_shape=[_sds((T, GW))] * 2,
        compiler_params=_cparams(("parallel",)), name=name)(proj, proj, proj, proj, w32, bias, ng, nb)


def conf_bwd(dy, c, proj, w32, ng, nb, name):
    T = proj.shape[0]
    tm = min(T, 512)
    nblk = T // tm
    P = CONF_P
    prev, nxt = _halo_maps(tm, P, nblk)

    def body(dyc_ref, dyn_ref, cc_ref, cn_ref, vc_ref, gc_ref, vp_ref, gp_ref, w_ref, ng_ref, nb_ref,
             dglu_ref, dw_ref, dbias_ref, dng_ref, dnb_ref):
        i = pl.program_id(0)

        @pl.when(i == 0)
        def _():
            dw_ref[...] = jnp.zeros_like(dw_ref)
            dbias_ref[...] = jnp.zeros_like(dbias_ref)
            dng_ref[...] = jnp.zeros_like(dng_ref)
            dnb_ref[...] = jnp.zeros_like(dnb_ref)

        ones = _group_ones()
        g = ng_ref[...]

        def dc_of(dyv, cv):
            hn, rstd, yn = _conf_norm(cv, g, nb_ref[...], ones)
            dyn_ = dyv * _dsilu(yn)
            dhn = dyn_ * g
            dc = rstd * (dhn - _group_mean(dhn, ones) - hn * _group_mean(dhn * hn, ones))
            return dc, dyn_, hn

        dc_c, dyn_c, hn_c = dc_of(dyc_ref[...], cc_ref[...])
        dc_n, _, _ = dc_of(dyn_ref[...], cn_ref[...])
        dc_n = jnp.where(i == nblk - 1, 0.0, dc_n)
        dng_ref[...] += jnp.sum(dyn_c * hn_c, axis=0, keepdims=True)
        dnb_ref[...] += jnp.sum(dyn_c, axis=0, keepdims=True)
        dbias_ref[...] += jnp.sum(dc_c, axis=0, keepdims=True)

        sig_c = jax.nn.sigmoid(gc_ref[...])
        val_c = vc_ref[...]
        pc = val_c * sig_c
        pp = jnp.where(i > 0, vp_ref[...] * jax.nn.sigmoid(gp_ref[...]), 0.0)
        ext = jnp.concatenate([pp, pc], axis=0)
        dext = jnp.concatenate([dc_c, dc_n], axis=0)
        dp = jnp.zeros((tm, GW), f32)
        for k in range(CONF_K):
            d = CONF_K - 1 - k
            moved = dext if d == 0 else pltpu.roll(dext, tm + P - d, 0)
            dp = dp + w_ref[k:k + 1, :] * moved[:tm, :]
            dw_ref[k:k + 1, :] += jnp.sum(dc_c * _shifted(ext, d)[P:, :], axis=0, keepdims=True)
        dglu_ref[:, 0:GW] = dp * sig_c
        dglu_ref[:, GW:2 * GW] = dp * val_c * sig_c * (1.0 - sig_c)

    vec = pl.BlockSpec((1, GW), lambda i: (0, 0))
    blk = pl.BlockSpec((tm, GW), lambda i: (i, 0))
    return pl.pallas_call(
        body, grid=(nblk,),
        in_specs=[blk, pl.BlockSpec((P, GW), lambda i: nxt(i, 0)), blk, pl.BlockSpec((P, GW), lambda i: nxt(i, 0)),
                  pl.BlockSpec((tm, GW), lambda i: (i, CONF_VAL_BLK)), pl.BlockSpec((tm, GW), lambda i: (i, CONF_GATE_BLK)),
                  pl.BlockSpec((P, GW), lambda i: prev(i, CONF_VAL_BLK)), pl.BlockSpec((P, GW), lambda i: prev(i, CONF_GATE_BLK)),
                  pl.BlockSpec((32, GW), lambda i: (0, 0)), vec, vec],
        out_specs=[pl.BlockSpec((tm, 2 * GW), lambda i: (i, 0)), pl.BlockSpec((32, GW), lambda i: (0, 0)), vec, vec, vec],
        out_shape=[_sds((T, 2 * GW)), _sds((32, GW)), _sds((1, GW)), _sds((1, GW)), _sds((1, GW))],
        compiler_params=_cparams(("arbitrary",)), name=name)(dy, dy, c, c, proj, proj, proj, proj, w32, ng, nb)


def _mm_raw(a, b, ta, tb):
    ca = a.ndim - 2 if ta else a.ndim - 1
    cb = b.ndim - 1 if tb else b.ndim - 2
    batch = ((0,), (0,)) if a.ndim == 3 else ((), ())
    return lax.dot_general(a, b, (((ca,), (cb,)), batch), preferred_element_type=f32)


def _mm_prec(a, b, ta, tb, prec):
    if prec == 1:
        return _mm_raw(a.astype(bf16), b.astype(bf16), ta, tb)
    bh, bl = _split_hi_lo(b)
    if prec == 2:
        ab = a.astype(bf16)
        return _mm_raw(ab, bh, ta, tb) + _mm_raw(ab, bl, ta, tb)
    ah, al = _split_hi_lo(a)
    return _mm_raw(ah, bh, ta, tb) + (_mm_raw(ah, bl, ta, tb) + _mm_raw(al, bh, ta, tb))


@functools.partial(jax.custom_vjp, nondiff_argnums=(2, 3, 4))
def mm(a, b, ta=False, tb=False, prec=1):
    return _mm_prec(a, b, ta, tb, prec)


def _mm_fwd(a, b, ta, tb, prec):
    return _mm_prec(a, b, ta, tb, prec), (a, b)


def _mm_bwd(ta, tb, prec, res, ct):
    a, b = res
    da = _mm_prec(b, ct, tb, True, 1) if ta else _mm_prec(ct, b, False, not tb, 1)
    db = _mm_prec(ct, a, True, ta, 1) if tb else _mm_prec(a, ct, not ta, False, 2 if prec == 2 else 1)
    return da, db


mm.defvjp(_mm_fwd, _mm_bwd)


def _tri_inv_raw(l):
    n = -l
    rr = lax.broadcasted_iota(jnp.int32, l.shape, 1)
    cc = lax.broadcasted_iota(jnp.int32, l.shape, 2)
    p = jnp.where(rr == cc, 1.0, 0.0) + n
    for _ in range(5):
        n = _mm_prec(n, n, False, False, 1)
        p = p + _mm_prec(p, n, False, False, 1)
    return p


@jax.custom_vjp
def tri_inv(l):
    return _tri_inv_raw(l)


def _tri_inv_fwd(l):
    t = _tri_inv_raw(l)
    return t, t


def _tri_inv_bwd(t, ct):
    return (-_mm_prec(_mm_prec(t, ct, True, False, 1), t, False, True, 1),)


tri_inv.defvjp(_tri_inv_fwd, _tri_inv_bwd)


def _gdn_block(S, qs, ks, vs, a, b, z, alog, dtb, ng):
    shp = (NH, CHUNK, CHUNK)
    ii = lax.broadcasted_iota(jnp.int32, shp, 1)
    jj = lax.broadcasted_iota(jnp.int32, shp, 2)
    l_incl = jnp.where(ii >= jj, 1.0, 0.0)
    ys = []
    for c in range(len(qs)):
        q = qs[c] * lax.rsqrt(jnp.sum(qs[c] * qs[c], axis=-1, keepdims=True) + L2_EPS) * (HD ** -0.5)
        k = ks[c] * lax.rsqrt(jnp.sum(ks[c] * ks[c], axis=-1, keepdims=True) + L2_EPS)
        v = vs[c]
        beta = jax.nn.sigmoid(b[c])
        g = -jnp.exp(alog) * _softplus(a[c] + dtb)
        gcb = mm(l_incl, jnp.broadcast_to(g, shp), False, False, 2)
        gcr = jnp.swapaxes(gcb, 1, 2)
        decay = jnp.exp(jnp.where(ii >= jj, gcb - gcr, NEG))
        g_last = jnp.sum(jnp.where(ii == CHUNK - 1, gcb, 0.0), axis=1, keepdims=True)
        eg = jnp.exp(gcb)
        kb = k * beta
        lkk = jnp.where(ii > jj, mm(kb, k, False, True) * decay, 0.0)
        t_inv = tri_inv(lkk)
        u = mm(t_inv, v * beta)
        w = mm(t_inv, kb * eg)
        a_qk = jnp.where(ii >= jj, mm(q, k, False, True) * decay, 0.0)
        q_dec = q * eg
        k_dec = k * jnp.exp(g_last - gcb)
        v_new = u - mm(w, S)
        o = mm(q_dec, S) + mm(a_qk, v_new)
        S = S * jnp.exp(g_last) + mm(k_dec, v_new, True, False)
        y = o * lax.rsqrt(jnp.mean(o * o, axis=-1, keepdims=True) + RMS_EPS) * ng
        ys.append(y * (z[c] * jax.nn.sigmoid(z[c])))
    return S, ys


GDN_CB = 256


def _heads(ref, rows, width, lane0=0):
    return jnp.stack([ref[rows, lane0 + h * width:lane0 + (h + 1) * width] for h in range(NH)])


def _chunk_rows(c):
    return slice(c * CHUNK, (c + 1) * CHUNK)


def _gdn_load(refs, nc):
    q_ref, k_ref, v_ref, z_ref, sm_ref = refs
    hm = lambda r: [r[:, _chunk_rows(c), :] for c in range(nc)]
    return (hm(q_ref), hm(k_ref), hm(v_ref), [_heads(z_ref, _chunk_rows(c), HD) for c in range(nc)],
            [_heads(sm_ref, _chunk_rows(c), 1) for c in range(nc)], [_heads(sm_ref, _chunk_rows(c), 1, NH) for c in range(nc)])


def gdn_fwd(qkv_hm, proj, alog, dtb, ng, name):
    T = proj.shape[0]
    cb = min(T, GDN_CB)
    nc = cb // CHUNK
    nb = T // cb

    def body(q_ref, k_ref, v_ref, z_ref, sm_ref, alog_ref, dtb_ref, ng_ref, y_ref, s_ref, S):
        @pl.when(pl.program_id(0) == 0)
        def _():
            S[...] = jnp.zeros_like(S)
        s_ref[...] = S[...]
        qs, ks, vs, zs, as_, bs = _gdn_load((q_ref, k_ref, v_ref, z_ref, sm_ref), nc)
        s_out, ys = _gdn_block(S[...], qs, ks, vs, as_, bs, zs, alog_ref[...], dtb_ref[...], ng_ref[...])
        S[...] = s_out
        for c in range(nc):
            for h in range(NH):
                y_ref[_chunk_rows(c), h * HD:(h + 1) * HD] = ys[c][h]

    hm = lambda h0: pl.BlockSpec((NH, cb, HD), lambda i: (h0, i, 0))
    par = pl.BlockSpec((NH, 1, 1), lambda i: (0, 0, 0))
    return pl.pallas_call(
        body, grid=(nb,),
        in_specs=[hm(0), hm(1), hm(2), pl.BlockSpec((cb, GW), lambda i: (i, C_GZ // GW)),
                  pl.BlockSpec((cb, 128), lambda i: (i, C_SMALL // 128)), par, par, pl.BlockSpec((1, 1, HD), lambda i: (0, 0, 0))],
        out_specs=[pl.BlockSpec((cb, GW), lambda i: (i, 0)), pl.BlockSpec((None, NH, HD, HD), lambda i: (i, 0, 0, 0))],
        out_shape=[_sds((T, GW)), _sds((nb, NH, HD, HD))],
        scratch_shapes=[pltpu.VMEM((NH, HD, HD), f32)],
        compiler_params=_cparams(("arbitrary",)), name=name)(qkv_hm, qkv_hm, qkv_hm, proj, proj, alog, dtb, ng)


def gdn_bwd(dycat, states, qkv_hm, proj, alog, dtb, ng, name):
    T = proj.shape[0]
    cb = min(T, GDN_CB)
    nc = cb // CHUNK
    nb = T // cb

    def body(dy_ref, s_ref, q_ref, k_ref, v_ref, z_ref, sm_ref, alog_ref, dtb_ref, ng_ref,
             dqkv_ref, dz_ref, dsm_ref, dalog_ref, ddtb_ref, dng_ref, dS):
        @pl.when(pl.program_id(0) == 0)
        def _():
            dS[...] = jnp.zeros_like(dS)
            dalog_ref[...] = jnp.zeros_like(dalog_ref)
            ddtb_ref[...] = jnp.zeros_like(ddtb_ref)
            dng_ref[...] = jnp.zeros_like(dng_ref)

        qs, ks, vs, zs, as_, bs = _gdn_load((q_ref, k_ref, v_ref, z_ref, sm_ref), nc)
        _, vjp = jax.vjp(_gdn_block, s_ref[...], qs, ks, vs, as_, bs, zs, alog_ref[...], dtb_ref[...], ng_ref[...])
        dys = [_heads(dy_ref, _chunk_rows(c), HD) for c in range(nc)]
        d_s, dqs, dks, dvs, das, dbs, dzs, d_alog, d_dtb, d_ng = vjp((dS[...], dys))
        dS[...] = d_s
        dalog_ref[...] += d_alog
        ddtb_ref[...] += d_dtb
        dng_ref[...] += d_ng
        for c in range(nc):
            sl = _chunk_rows(c)
            dqkv_ref[0:NH, sl, :] = dqs[c]
            dqkv_ref[NH:2 * NH, sl, :] = dks[c]
            dqkv_ref[2 * NH:3 * NH, sl, :] = dvs[c]
            for h in range(NH):
                dz_ref[sl, h * HD:(h + 1) * HD] = dzs[c][h]
            dsm_ref[sl, :] = _pack_cols([das[c][h] for h in range(NH)] + [dbs[c][h] for h in range(NH)])

    rev = lambda i: nb - 1 - i
    hm = lambda h0: pl.BlockSpec((NH, cb, HD), lambda i: (h0, rev(i), 0))
    tok = lambda w, cblk: pl.BlockSpec((cb, w), lambda i: (rev(i), cblk))
    par = pl.BlockSpec((NH, 1, 1), lambda i: (0, 0, 0))
    ngs = pl.BlockSpec((1, 1, HD), lambda i: (0, 0, 0))
    res = pl.pallas_call(
        body, grid=(nb,),
        in_specs=[tok(GW, 0), pl.BlockSpec((None, NH, HD, HD), lambda i: (rev(i), 0, 0, 0)),
                  hm(0), hm(1), hm(2), tok(GW, C_GZ // GW), tok(128, C_SMALL // 128), par, par, ngs],
        out_specs=[pl.BlockSpec((3 * NH, cb, HD), lambda i: (0, rev(i), 0)), tok(GW, 0), tok(128, 0), par, par, ngs],
        out_shape=[_sds((3 * NH, T, HD)), _sds((T, GW)), _sds((T, 128))] + [_sds((NH, 1, 1))] * 2 + [_sds((1, 1, HD))],
        scratch_shapes=[pltpu.VMEM((NH, HD, HD), f32)],
        compiler_params=_cparams(("arbitrary",)), name=name)(dycat, states, qkv_hm, qkv_hm, qkv_hm, proj, proj, alog, dtb, ng)
    return res


F_LANE = 8
SCAN_TB = 256


def fox_gate_fwd(proj, bfv, name):
    T = proj.shape[0]
    tb = min(T, SCAN_TB)

    def body(x_ref, b_ref, o_ref, carry):
        @pl.when(pl.program_id(0) == 0)
        def _():
            carry[...] = jnp.zeros_like(carry)
        logf = -_softplus(-(x_ref[...] + b_ref[...]))
        r = lax.broadcasted_iota(jnp.int32, (tb, tb), 0)
        c = lax.broadcasted_iota(jnp.int32, (tb, tb), 1)
        tri = (r >= c).astype(bf16)
        hi, lo = _split_hi_lo(logf)
        cum = _dot(tri, hi) + _dot(tri, lo) + carry[0:1, :]
        o_ref[...] = cum
        carry[0:1, :] = cum[tb - 1:tb, :]

    return pl.pallas_call(
        body, grid=(T // tb,),
        in_specs=[pl.BlockSpec((tb, 128), lambda i: (i, C_SMALL // 128)), pl.BlockSpec((1, 128), lambda i: (0, 0))],
        out_specs=pl.BlockSpec((tb, 128), lambda i: (i, 0)), out_shape=_sds((T, 128)),
        scratch_shapes=[pltpu.VMEM((8, 128), f32)],
        compiler_params=_cparams(("arbitrary",)), name=name)(proj, bfv)


def fox_gate_bwd(dcum, proj, bfv, name):
    T = proj.shape[0]
    tb = min(T, SCAN_TB)
    nb = T // tb

    def body(d_ref, x_ref, b_ref, o_ref, db_ref, carry):
        @pl.when(pl.program_id(0) == 0)
        def _():
            carry[...] = jnp.zeros_like(carry)
            db_ref[...] = jnp.zeros_like(db_ref)
        r = lax.broadcasted_iota(jnp.int32, (tb, tb), 0)
        c = lax.broadcasted_iota(jnp.int32, (tb, tb), 1)
        tri = (c >= r).astype(bf16)
        hi, lo = _split_hi_lo(d_ref[...])
        dlogf = _dot(tri, hi) + _dot(tri, lo) + carry[0:1, :]
        carry[0:1, :] = dlogf[0:1, :]
        lane = lax.broadcasted_iota(jnp.int32, (tb, 128), 1)
        keep = (lane >= F_LANE) & (lane < F_LANE + NH)
        dx = jnp.where(keep, dlogf * jax.nn.sigmoid(-(x_ref[...] + b_ref[...])), 0.0)
        o_ref[...] = dx
        db_ref[...] += jnp.sum(dx, axis=0, keepdims=True)

    rev = lambda i: nb - 1 - i
    return pl.pallas_call(
        body, grid=(nb,),
        in_specs=[pl.BlockSpec((tb, 128), lambda i: (rev(i), 0)), pl.BlockSpec((tb, 128), lambda i: (rev(i), C_SMALL // 128)),
                  pl.BlockSpec((1, 128), lambda i: (0, 0))],
        out_specs=[pl.BlockSpec((tb, 128), lambda i: (rev(i), 0)), pl.BlockSpec((1, 128), lambda i: (0, 0))],
        out_shape=[_sds((T, 128)), _sds((1, 128))],
        scratch_shapes=[pltpu.VMEM((8, 128), f32)],
        compiler_params=_cparams(("arbitrary",)), name=name)(dcum, proj, bfv)


ATT_TQ = 1024
ATT_TQ_BWD = 512
ATT_TK = 256


def _lane_col(tile, lane):
    li = lax.broadcasted_iota(jnp.int32, tile.shape, 1)
    return jnp.sum(jnp.where(li == lane, tile, 0.0), axis=1, keepdims=True)


def _pack_cols(cols):
    rows = cols[0].shape[0]
    li = lax.broadcasted_iota(jnp.int32, (rows, 128), 1)
    out = jnp.zeros((rows, 128), f32)
    for h, cv in enumerate(cols):
        out = jnp.where(li == h, cv, out)
    return out


def _head_masks():
    li = lax.broadcasted_iota(jnp.int32, (1, 128), 1)
    return [li < HD, li >= HD]


def _qkv_specs(T, tq, base_blk):
    q = pl.BlockSpec((tq, 128), lambda p, i: (i, base_blk + p))
    k = pl.BlockSpec((T, 128), lambda p, i: (0, base_blk + 2 + p))
    v = pl.BlockSpec((T, 128), lambda p, i: (0, base_blk + 4 + p))
    return q, k, v


def _stack_heads(x, masks):
    return jnp.concatenate([jnp.where(m, x, jnp.zeros_like(x)) for m in masks], axis=0)


def _side_by_side(x, tq):
    return jnp.concatenate([x[:tq], x[tq:]], axis=1)


def _stacked_mask(tq, tk, d, strict):
    r = lax.broadcasted_iota(jnp.int32, (2 * tq, tk), 0)
    r = jnp.where(r >= tq, r - tq, r)
    c = lax.broadcasted_iota(jnp.int32, (2 * tq, tk), 1) + d * tk
    return c < r if strict else c <= r


def _sub_head_rows(s, rows2, tq):
    return jnp.concatenate([s[:tq] - rows2[0:1, :], s[tq:] - rows2[1:2, :]], axis=0)


def _lane_cols2(tile):
    return jnp.concatenate([_lane_col(tile, 0), _lane_col(tile, 1)], axis=0)


def _pack_cols2(col, tq):
    return _pack_cols([col[:tq], col[tq:]])


def _dot_hilo2(x, tri):
    n = x.shape[0]
    hi, lo = _split_hi_lo(x)
    r = _dot(jnp.concatenate([hi, lo], axis=0), tri)
    return r[:n] + r[n:]


def fox_fwd(pbf, cumrow, name, rider=None):
    T = pbf.shape[0]
    tq, tk = min(T, ATT_TQ), min(T, ATT_TK)
    nq, nk, per = T // tq, T // tk, tq // tk
    scale = HD ** -0.5

    def body(q_ref, k_ref, v_ref, cr_ref, o_ref, lse_ref):
        i = pl.program_id(1)
        masks = _head_masks()
        qs2 = _stack_heads(q_ref[...] * scale, masks)

        def tile(kb, carry, d=None):
            m, l, acc = carry
            off = pl.multiple_of(kb * tk, tk)
            v2 = _stack_heads(v_ref[pl.ds(off, tk), :], masks)
            s = _sub_head_rows(_dot_nt(qs2, k_ref[pl.ds(off, tk), :]), cr_ref[kb], tq)
            if d is not None:
                s = jnp.where(_stacked_mask(tq, tk, d, False), s, NEG)
            m_new = jnp.maximum(m, jnp.max(s, axis=1, keepdims=True))
            corr = jnp.exp(m - m_new)
            p = jnp.exp(s - m_new)
            l = l * corr + jnp.sum(p, axis=1, keepdims=True)
            acc = acc * jnp.where(masks[0], corr[:tq], corr[tq:]) + _dot(_side_by_side(p.astype(bf16), tq), v2)
            return m_new, l, acc

        init = (jnp.full((2 * tq, 1), NEG, f32), jnp.zeros((2 * tq, 1), f32), jnp.zeros((tq, 128), f32))
        carry = lax.fori_loop(0, i * per, tile, init)
        for d in range(per):
            carry = tile(i * per + d, carry, d)
        m, l, acc = carry
        o_ref[...] = acc * jnp.where(masks[0], 1.0 / l[:tq], 1.0 / l[tq:])
        lse_ref[...] = _pack_cols2(m + jnp.log(l), tq)

    qs, ks, vs = _qkv_specs(T, tq, C_FOX // 128)
    body, r_in, r_args, r_out, r_shape, r_scratch = _ride(body, 4, 2, rider, (2, nq))
    res = pl.pallas_call(
        body, grid=(2, nq),
        in_specs=[qs, ks, vs, pl.BlockSpec((None, nk, 8, tk), lambda p, i: (p, 0, 0, 0))] + r_in,
        out_specs=[pl.BlockSpec((tq, 128), lambda p, i: (i, p)), pl.BlockSpec((None, tq, 128), lambda p, i: (p, i, 0))] + r_out,
        out_shape=[_sds((T, GW)), _sds((2, T, 128))] + r_shape, scratch_shapes=r_scratch,
        compiler_params=_cparams(("arbitrary", "arbitrary")), name=name)(pbf, pbf, pbf, cumrow, *r_args)
    return res[0], res[1], res[2:]


def fox_bwd(do, o, lse, pbf, cumrow, name, rider=None):
    T = pbf.shape[0]
    tq, tk = min(T, ATT_TQ_BWD), min(T, ATT_TK)
    nq, nk, per = T // tq, T // tk, tq // tk
    scale = HD ** -0.5

    def body(do_ref, o_ref, lse_ref, q_ref, k_ref, v_ref, cr_ref, dq_ref, dk_ref, dv_ref, dc_ref, dcq_ref):
        i = pl.program_id(1)

        @pl.when(i == 0)
        def _():
            dk_ref[...] = jnp.zeros_like(dk_ref)
            dv_ref[...] = jnp.zeros_like(dv_ref)
            dc_ref[...] = jnp.zeros_like(dc_ref)

        masks = _head_masks()
        dov = do_ref[...]
        qs2 = _stack_heads(q_ref[...] * scale, masks)
        do2 = _stack_heads(dov.astype(bf16), masks)
        prod = dov * o_ref[...]
        delta = jnp.concatenate([jnp.sum(jnp.where(m, prod, 0.0), axis=1, keepdims=True) for m in masks], axis=0)
        lse2 = _lane_cols2(lse_ref[...])

        def tile(kb, carry, d=None):
            dq, rsum = carry
            off = pl.multiple_of(kb * tk, tk)
            kblk = k_ref[pl.ds(off, tk), :]
            p = jnp.exp(_sub_head_rows(_dot_nt(qs2, kblk), cr_ref[kb], tq) - lse2)
            if d is not None:
                p = jnp.where(_stacked_mask(tq, tk, d, False), p, 0.0)
            dp = _dot_nt(do2, v_ref[pl.ds(off, tk), :])
            ds = p * (dp - delta)
            dsb = ds.astype(bf16)
            dq = dq + _dot(_side_by_side(dsb, tq), _stack_heads(kblk, masks))
            dk_ref[pl.ds(off, tk), :] += _dot_tn(dsb, qs2)
            dv_ref[pl.ds(off, tk), :] += _dot_tn(p.astype(bf16), do2)
            dc_ref[kb, 0:1, :] += -jnp.sum(ds[:tq], axis=0, keepdims=True)
            dc_ref[kb, 1:2, :] += -jnp.sum(ds[tq:], axis=0, keepdims=True)
            return dq, rsum + jnp.sum(ds, axis=1, keepdims=True)

        carry = lax.fori_loop(0, i * per, tile, (jnp.zeros((tq, 128), f32), jnp.zeros((2 * tq, 1), f32)))
        for d in range(per):
            carry = tile(i * per + d, carry, d)
        dq, rsum = carry
        dq_ref[...] = dq * scale
        dcq_ref[...] = _pack_cols2(rsum, tq)

    qs, ks, vs = _qkv_specs(T, tq, C_FOX // 128)
    tile_spec = pl.BlockSpec((tq, 128), lambda p, i: (i, p))
    pair = pl.BlockSpec((None, tq, 128), lambda p, i: (p, i, 0))
    rowsp = pl.BlockSpec((None, nk, 8, tk), lambda p, i: (p, 0, 0, 0))
    full = pl.BlockSpec((T, 128), lambda p, i: (0, p))
    body, r_in, r_args, r_out, r_shape, r_scratch = _ride(body, 7, 5, rider, (2, nq))
    res = pl.pallas_call(
        body, grid=(2, nq),
        in_specs=[tile_spec, tile_spec, pair, qs, ks, vs, rowsp] + r_in,
        out_specs=[tile_spec, full, full, rowsp, pair] + r_out,
        out_shape=[_sds((T, GW)), _sds((T, GW)), _sds((T, GW)), _sds((2, nk, 8, tk)), _sds((2, T, 128))] + r_shape,
        scratch_shapes=r_scratch,
        compiler_params=_cparams(("arbitrary", "arbitrary")), name=name)(do, o, lse, pbf, pbf, pbf, cumrow, *r_args)
    return res[0], res[1], res[2], res[3], res[4], res[5:]


def _tri(tq, pred):
    r = lax.broadcasted_iota(jnp.int32, (tq, tq), 0)
    c = lax.broadcasted_iota(jnp.int32, (tq, tq), 1)
    return pred(r, c).astype(bf16)


def _ride(body, n_in, n_out, rider, grid, n_scratch=0):
    if rider is None:
        return body, [], [], [], [], []
    nr = rider.n

    def wrapped(*refs):
        ins, rin = refs[:n_in], refs[n_in:n_in + nr]
        outs = refs[n_in + nr:n_in + nr + n_out]
        rout = refs[n_in + nr + n_out:n_in + 2 * nr + n_out]
        own = refs[n_in + 2 * nr + n_out:n_in + 2 * nr + n_out + n_scratch]
        sems = refs[n_in + 2 * nr + n_out + n_scratch:]
        ids = [pl.program_id(a) for a in range(len(grid))]
        first = functools.reduce(jnp.logical_and, [i == 0 for i in ids])
        last = functools.reduce(jnp.logical_and, [i == g - 1 for i, g in zip(ids, grid)])

        @pl.when(first)
        def _():
            rider.start(rin, rout, sems)

        body(*ins, *outs, *own)

        @pl.when(last)
        def _():
            rider.finish(rin, rout, sems)

    anyspec = pl.BlockSpec(memory_space=pl.ANY)
    return wrapped, [anyspec] * nr, list(rider.arrays), [anyspec] * nr, list(rider.out_shape), list(rider.scratch)


def sb_fwd(pbf, name, rider=None):
    T = pbf.shape[0]
    tq, tk = min(T, ATT_TQ), min(T, ATT_TK)
    nq, per = T // tq, tq // tk
    scale = HD ** -0.5

    def body(q_ref, k_ref, v_ref, o_ref, tot_ref):
        i = pl.program_id(1)
        masks = _head_masks()
        qs2 = _stack_heads(q_ref[...] * scale, masks)
        after = _tri(tk, lambda r, c: r > c)

        def tile(kb, carry, d=None):
            rs, acc = carry
            off = pl.multiple_of(kb * tk, tk)
            z = _dot_nt(qs2, k_ref[pl.ds(off, tk), :])
            lk = _neg_softplus(z)
            if d is not None:
                lk = jnp.where(_stacked_mask(tq, tk, d, True), lk, 0.0)
            w = jnp.exp(z + lk + (_dot_hilo2(lk, after) + rs))
            if d is not None:
                w = jnp.where(_stacked_mask(tq, tk, d, True), w, 0.0)
            acc = acc + _dot(_side_by_side(w.astype(bf16), tq), _stack_heads(v_ref[pl.ds(off, tk), :], masks))
            return rs + jnp.sum(lk, axis=1, keepdims=True), acc

        carry = (jnp.zeros((2 * tq, 1), f32), jnp.zeros((tq, 128), f32))
        for d in reversed(range(per)):
            carry = tile(i * per + d, carry, d)
        rs, acc = lax.fori_loop(0, i * per, lambda n, c: tile(i * per - 1 - n, c), carry)
        o_ref[...] = acc
        tot_ref[...] = _pack_cols2(rs, tq)

    qs, ks, vs = _qkv_specs(T, tq, C_SB // 128)
    body, r_in, r_args, r_out, r_shape, r_scratch = _ride(body, 3, 2, rider, (2, nq))
    res = pl.pallas_call(
        body, grid=(2, nq), in_specs=[qs, ks, vs] + r_in,
        out_specs=[pl.BlockSpec((tq, 128), lambda p, i: (i, p)), pl.BlockSpec((None, tq, 128), lambda p, i: (p, i, 0))] + r_out,
        out_shape=[_sds((T, GW)), _sds((2, T, 128))] + r_shape, scratch_shapes=r_scratch,
        compiler_params=_cparams(("arbitrary", "arbitrary")), name=name)(pbf, pbf, pbf, *r_args)
    return res[0], res[1], res[2:]


def sb_bwd(do, tot, pbf, name, rider=None):
    T = pbf.shape[0]
    tq, tk = min(T, ATT_TQ_BWD), min(T, ATT_TK)
    nq, per = T // tq, tq // tk
    scale = HD ** -0.5

    def body(do_ref, tot_ref, q_ref, k_ref, v_ref, dq_ref, dk_ref, dv_ref):
        i = pl.program_id(1)

        @pl.when(i == 0)
        def _():
            dk_ref[...] = jnp.zeros_like(dk_ref)
            dv_ref[...] = jnp.zeros_like(dv_ref)

        masks = _head_masks()
        qs2 = _stack_heads(q_ref[...] * scale, masks)
        do2 = _stack_heads(do_ref[...].astype(bf16), masks)
        tot2 = _lane_cols2(tot_ref[...])
        upto = _tri(tk, lambda r, c: r <= c)
        before = _tri(tk, lambda r, c: r < c)

        def tile(kb, carry, d=None):
            pre, cg, dq = carry
            off = pl.multiple_of(kb * tk, tk)
            kblk = k_ref[pl.ds(off, tk), :]
            z = _dot_nt(qs2, kblk)
            lk = _neg_softplus(z)
            keep = jnp.exp(lk)
            if d is not None:
                lk = jnp.where(_stacked_mask(tq, tk, d, True), lk, 0.0)
            w = jnp.exp(z + lk + (tot2 - (pre + _dot_hilo2(lk, upto))))
            if d is not None:
                w = jnp.where(_stacked_mask(tq, tk, d, True), w, 0.0)
            gmat = w * _dot_nt(do2, v_ref[pl.ds(off, tk), :])
            cmat = cg + _dot(gmat.astype(bf16), before)
            dz = keep * (gmat + cmat) - cmat
            if d is not None:
                dz = jnp.where(_stacked_mask(tq, tk, d, True), dz, 0.0)
            dzb = dz.astype(bf16)
            dq = dq + _dot(_side_by_side(dzb, tq), _stack_heads(kblk, masks))
            dk_ref[pl.ds(off, tk), :] += _dot_tn(dzb, qs2)
            dv_ref[pl.ds(off, tk), :] += _dot_tn(w.astype(bf16), do2)
            return pre + jnp.sum(lk, axis=1, keepdims=True), cg + jnp.sum(gmat, axis=1, keepdims=True), dq

        zc = jnp.zeros((2 * tq, 1), f32)
        carry = lax.fori_loop(0, i * per, tile, (zc, zc, jnp.zeros((tq, 128), f32)))
        for d in range(per):
            carry = tile(i * per + d, carry, d)
        dq_ref[...] = carry[2] * scale

    qs, ks, vs = _qkv_specs(T, tq, C_SB // 128)
    tile_spec = pl.BlockSpec((tq, 128), lambda p, i: (i, p))
    pair = pl.BlockSpec((None, tq, 128), lambda p, i: (p, i, 0))
    full = pl.BlockSpec((T, 128), lambda p, i: (0, p))
    body, r_in, r_args, r_out, r_shape, r_scratch = _ride(body, 5, 3, rider, (2, nq))
    res = pl.pallas_call(
        body, grid=(2, nq), in_specs=[tile_spec, pair, qs, ks, vs] + r_in,
        out_specs=[tile_spec, full, full] + r_out, out_shape=[_sds((T, GW))] * 3 + r_shape, scratch_shapes=r_scratch,
        compiler_params=_cparams(("arbitrary", "arbitrary")), name=name)(do, tot, pbf, pbf, pbf, *r_args)
    return res[0], res[1], res[2], res[3:]


MEM_HD = D // 4


def mem_fwd(q, kv, name):
    T = q.shape[0]
    M = kv.shape[0]
    tm = min(T, 512)
    scale = MEM_HD ** -0.5

    def body(q_ref, kv_ref, o_ref):
        for h in range(4):
            sl = slice(h * MEM_HD, (h + 1) * MEM_HD)
            kh = kv_ref[:, sl].astype(bf16)
            vh = kv_ref[:, D + h * MEM_HD:D + (h + 1) * MEM_HD].astype(bf16)
            s = _dot_nt(q_ref[:, sl], kh) * scale
            e = jnp.exp(s - jnp.max(s, axis=1, keepdims=True))
            p = e / jnp.sum(e, axis=1, keepdims=True)
            o_ref[:, sl] = _dot(p.astype(bf16), vh).astype(bf16)

    return pl.pallas_call(
        body, grid=(T // tm,),
        in_specs=[pl.BlockSpec((tm, D), lambda i: (i, 0)), pl.BlockSpec((M, 2 * D), lambda i: (0, 0))],
        out_specs=pl.BlockSpec((tm, D), lambda i: (i, 0)), out_shape=_sds((T, D), bf16),
        compiler_params=_cparams(("parallel",)), name=name)(q, kv)


def mem_bwd(do, q, kv, name):
    T = q.shape[0]
    M = kv.shape[0]
    tm = min(T, 512)
    scale = MEM_HD ** -0.5

    def body(do_ref, q_ref, kv_ref, dq_ref, dkv_ref):
        @pl.when(pl.program_id(0) == 0)
        def _():
            dkv_ref[...] = jnp.zeros_like(dkv_ref)
        for h in range(4):
            sl = slice(h * MEM_HD, (h + 1) * MEM_HD)
            vsl = slice(D + h * MEM_HD, D + (h + 1) * MEM_HD)
            qh = q_ref[:, sl]
            kh = kv_ref[:, sl].astype(bf16)
            vh = kv_ref[:, vsl].astype(bf16)
            doh = do_ref[:, sl].astype(bf16)
            s = _dot_nt(qh, kh) * scale
            e = jnp.exp(s - jnp.max(s, axis=1, keepdims=True))
            p = e / jnp.sum(e, axis=1, keepdims=True)
            dp = _dot_nt(doh, vh)
            ds = p * (dp - jnp.sum(dp * p, axis=1, keepdims=True))
            dsb = ds.astype(bf16)
            dq_ref[:, sl] = _dot(dsb, kh) * scale
            dkv_ref[:, sl] += _dot_tn(dsb, qh) * scale
            dkv_ref[:, vsl] += _dot_tn(p.astype(bf16), doh)

    row = pl.BlockSpec((tm, D), lambda i: (i, 0))
    whole = pl.BlockSpec((M, 2 * D), lambda i: (0, 0))
    return pl.pallas_call(
        body, grid=(T // tm,), in_specs=[row, row, whole], out_specs=[row, whole],
        out_shape=[_sds((T, D)), _sds((M, 2 * D))],
        compiler_params=_cparams(("arbitrary",)), name=name)(do, q, kv)


def _chip_peers():
    x, y, c = lax.axis_index("x"), lax.axis_index("y"), lax.axis_index("c")
    me = 2 * x + y
    peers = [((1 - x, y, c), 2 * (1 - x) + y), ((x, 1 - y, c), 2 * x + (1 - y)), ((1 - x, 1 - y, c), 2 * (1 - x) + (1 - y))]
    return me, peers


class Rider:
    def __init__(self, arrays, out_shape, scratch, start, finish):
        self.arrays, self.out_shape, self.scratch, self.start, self.finish = arrays, out_shape, scratch, start, finish
        self.n = len(arrays)

    def split(self, refs):
        return refs[:self.n], refs[self.n:2 * self.n], refs[2 * self.n:]


def run_rider(rider, name):
    def body(*refs):
        parts = rider.split(refs)
        rider.start(*parts)
        rider.finish(*parts)

    anyspec = pl.BlockSpec(memory_space=pl.ANY)
    return pl.pallas_call(
        body, in_specs=[anyspec] * rider.n, out_specs=[anyspec] * rider.n, out_shape=rider.out_shape,
        scratch_shapes=rider.scratch, name=name)(*rider.arrays)


def gather_rider(arrs, split):
    n = len(arrs)

    def copies(ins, outs, sems):
        send_sems, recv_sems, pass_send, pass_recv, loc_sems = sems
        x, y, c = lax.axis_index("x"), lax.axis_index("y"), lax.axis_index("c")
        me, peers = _chip_peers()

        def mine(ref, i):
            if not split[i]:
                return ref
            r = arrs[i].shape[1] // 2
            return ref.at[:, pl.ds(c * r, r), :]

        local, fetch, passed = [], [], []
        for i in range(n):
            local.append(pltpu.make_async_copy(ins[i], outs[i].at[me], loc_sems.at[i]))
            for k, (dev, pj) in enumerate(peers):
                fetch.append(pltpu.make_async_remote_copy(src_ref=mine(ins[i], i), dst_ref=mine(outs[i].at[me], i),
                                                          send_sem=send_sems.at[i, k], recv_sem=recv_sems.at[i, k],
                                                          device_id=dev, device_id_type=MESH))
                rows = mine(outs[i].at[pj], i)
                passed.append(pltpu.make_async_remote_copy(src_ref=rows, dst_ref=rows, send_sem=pass_send.at[i, k],
                                                           recv_sem=pass_recv.at[i, k], device_id=(x, y, 1 - c),
                                                           device_id_type=MESH) if split[i] else None)
        return local, fetch, passed

    def start(ins, outs, sems):
        local, fetch, _ = copies(ins, outs, sems)
        for cp in local + fetch:
            cp.start()

    def finish(ins, outs, sems):
        local, fetch, passed = copies(ins, outs, sems)
        for cp, fw in zip(fetch, passed):
            cp.wait_recv()
            if fw is not None:
                fw.start()
        for cp in fetch:
            cp.wait_send()
        for cp in [fw for fw in passed if fw is not None] + local:
            cp.wait()

    sem = pltpu.SemaphoreType.DMA((n, 3))
    return Rider(list(arrs), [_sds((4,) + a.shape, a.dtype) for a in arrs],
                 [sem, sem, sem, sem, pltpu.SemaphoreType.DMA((n,))], start, finish)


def exchange_rider(stacks):
    n = len(stacks)

    def copies(ins, outs, sems):
        send_sems, recv_sems, loc_sems = sems
        me, peers = _chip_peers()
        out = []
        for i in range(n):
            out.append(pltpu.make_async_copy(ins[i].at[me], outs[i].at[me], loc_sems.at[i]))
            for k, (dev, pj) in enumerate(peers):
                out.append(pltpu.make_async_remote_copy(src_ref=ins[i].at[pj], dst_ref=outs[i].at[me], send_sem=send_sems.at[i, k],
                                                        recv_sem=recv_sems.at[i, k], device_id=dev, device_id_type=MESH))
        return out

    def start(ins, outs, sems):
        for cp in copies(ins, outs, sems):
            cp.start()

    def finish(ins, outs, sems):
        for cp in copies(ins, outs, sems):
            cp.wait()

    return Rider(list(stacks), [_sds(a.shape, a.dtype) for a in stacks],
                 [pltpu.SemaphoreType.DMA((n, 3)), pltpu.SemaphoreType.DMA((n, 3)), pltpu.SemaphoreType.DMA((n,))], start, finish)


def swap_sibling(arrs, name):
    n = len(arrs)

    def body(*refs):
        ins, outs = refs[:n], refs[n:2 * n]
        send_sems, recv_sems = refs[2 * n:]
        x, y, c = lax.axis_index("x"), lax.axis_index("y"), lax.axis_index("c")
        started = []
        for i in range(n):
            cp = pltpu.make_async_remote_copy(src_ref=ins[i], dst_ref=outs[i], send_sem=send_sems.at[i],
                                              recv_sem=recv_sems.at[i], device_id=(x, y, 1 - c), device_id_type=MESH)
            cp.start()
            started.append(cp)
        for cp in started:
            cp.wait()

    anyspec = pl.BlockSpec(memory_space=pl.ANY)
    return pl.pallas_call(
        body, in_specs=[anyspec] * n, out_specs=[anyspec] * n,
        out_shape=[_sds(a.shape, a.dtype) for a in arrs],
        scratch_shapes=[pltpu.SemaphoreType.DMA((n,)), pltpu.SemaphoreType.DMA((n,))],
        name=name)(*arrs)


def gather_all(a, name):
    def body(a_ref, o_ref, send_sems, recv_sems, loc_sem):
        x, y, c = lax.axis_index("x"), lax.axis_index("y"), lax.axis_index("c")
        me = 4 * x + 2 * y + c
        loc = pltpu.make_async_copy(a_ref, o_ref.at[me], loc_sem)
        loc.start()
        started = [loc]
        for k in range(1, 8):
            dev = (x ^ (k >> 2), y ^ ((k >> 1) & 1), c ^ (k & 1))
            cp = pltpu.make_async_remote_copy(src_ref=a_ref, dst_ref=o_ref.at[me], send_sem=send_sems.at[k - 1],
                                              recv_sem=recv_sems.at[k - 1], device_id=dev, device_id_type=MESH)
            cp.start()
            started.append(cp)
        for cp in started:
            cp.wait()

    anyspec = pl.BlockSpec(memory_space=pl.ANY)
    return pl.pallas_call(
        body, in_specs=[anyspec], out_specs=anyspec, out_shape=_sds((8,) + a.shape, a.dtype),
        scratch_shapes=[pltpu.SemaphoreType.DMA((7,)), pltpu.SemaphoreType.DMA((7,)), pltpu.SemaphoreType.DMA(())],
        name=name)(a)


def sum_slots(stack, name):
    n, R, C = stack.shape
    tr = R if R <= 512 else _pick(R, (512, 352, 256))

    def body(s_ref, o_ref):
        acc = s_ref[0].astype(f32)
        for j in range(1, n):
            acc = acc + s_ref[j].astype(f32)
        o_ref[...] = acc

    return pl.pallas_call(
        body, grid=(R // tr,), in_specs=[pl.BlockSpec((n, tr, C), lambda i: (0, i, 0))],
        out_specs=pl.BlockSpec((tr, C), lambda i: (i, 0)), out_shape=_sds((R, C)),
        compiler_params=_cparams(("parallel",)), name=name)(stack)


def adamw(w, g1, g2, m, v, name):
    R, C = w.shape
    tr = R if R <= 512 else _pick(R, (512, 352, 256))
    c1 = 1.0 - ADAM_B1 ** ADAM_STEP
    c2 = 1.0 - ADAM_B2 ** ADAM_STEP

    def body(*refs):
        if g2 is None:
            w_ref, g1_ref, m_ref, v_ref, g_out, d_out, m_out, v_out = refs
            g = g1_ref[...]
        else:
            w_ref, g1_ref, g2_ref, m_ref, v_ref, g_out, d_out, m_out, v_out = refs
            g = g1_ref[...] + g2_ref[...]
        mn = ADAM_B1 * m_ref[...] + (1.0 - ADAM_B1) * g
        vn = ADAM_B2 * v_ref[...] + (1.0 - ADAM_B2) * (g * g)
        g_out[...] = g
        m_out[...] = mn
        v_out[...] = vn
        d_out[...] = -ADAM_LR * ((mn / c1) / (jnp.sqrt(vn / c2) + ADAM_EPS) + ADAM_WD * w_ref[...])

    blk = pl.BlockSpec((tr, C), lambda i: (i, 0))
    args = [w, g1] + ([] if g2 is None else [g2]) + [m, v]
    return pl.pallas_call(
        body, grid=(R // tr,), in_specs=[blk] * len(args), out_specs=[blk] * 4, out_shape=[_sds((R, C))] * 4,
        compiler_params=_cparams(("parallel",)), name=name)(*args)


IN_OFF = (0, 768, 1024, 1028, 1032, 1800, 1804, 2316, 3084)


def regroup_w_in(w):
    seg = lambda i: w[:, IN_OFF[i]:IN_OFF[i + 1]]
    pad = jnp.zeros((w.shape[0], PW - C_SMALL - 12), w.dtype)
    return jnp.concatenate([seg(0), seg(1), seg(4), seg(6), seg(7), seg(2), seg(3), seg(5), pad], axis=1)


def ungroup_w_in(g):
    s = C_SMALL
    return jnp.concatenate([g[:, 0:1024], g[:, s:s + 8], g[:, 1024:1792], g[:, s + 8:s + 12], g[:, 1792:3072]], axis=1)


def col_shards(w):
    c = w.shape[-1] // 4
    return jnp.moveaxis(w.reshape(w.shape[:-1] + (4, c)), -2, 0)


def row_shards(w):
    L, r4, c = w.shape
    return w.reshape(L, 4, r4 // 4, c).transpose(1, 0, 2, 3)


def join_cols(g):
    return jnp.moveaxis(g, 0, -2).reshape(g.shape[1:-1] + (4 * g.shape[-1],))


def join_rows(g):
    _, L, r, c = g.shape
    return g.transpose(1, 0, 2, 3).reshape(L, 4 * r, c)


COL_SHARDED = ("ffn1_w_gate", "ffn1_w_up", "w_in", "gdn_conv_w", "conf_dw_w", "mem_w_kv", "ffn2_w_gate", "ffn2_w_up")
CONV_WEIGHTS = ("gdn_conv_w", "conf_dw_w")
ROW_SHARDED = ("ffn1_w_down", "w_out", "mem_w_q", "mem_w_o", "ffn2_w_down")
REPLICATED = ("ln_ffn1_g", "ln_ffn1_b", "gdn_a_log", "gdn_dt_bias", "gdn_norm_g", "fox_b_f", "conf_dw_b", "conf_norm_g",
              "conf_norm_b", "ln_mix_g", "ln_mix_b", "ln_mem_g", "ln_mem_b", "ln_ffn2_g", "ln_ffn2_b")
WEIGHTS = ("ffn1_w_gate", "ffn1_w_up", "ffn1_w_down", "ln_ffn1_g", "ln_ffn1_b", "w_in", "gdn_conv_w", "gdn_a_log",
           "gdn_dt_bias", "gdn_norm_g", "fox_b_f", "conf_dw_w", "conf_dw_b", "conf_norm_g", "conf_norm_b", "w_out",
           "ln_mix_g", "ln_mix_b", "mem_w_q", "mem_w_kv", "mem_w_o", "ln_mem_g", "ln_mem_b", "ffn2_w_gate",
           "ffn2_w_up", "ffn2_w_down", "ln_ffn2_g", "ln_ffn2_b")


def pack_small(d):
    flat = jnp.concatenate([d[n].reshape(-1) for n in REPLICATED])
    rows = -(-flat.shape[0] // 1024) * 8
    return jnp.pad(flat, (0, rows * 128 - flat.shape[0])).reshape(rows, 128)


def unpack_small(p, like):
    flat = p.reshape(-1)
    out, o = {}, 0
    for n in REPLICATED:
        sz = like[n].size
        out[n] = flat[o:o + sz].reshape(like[n].shape)
        o += sz
    return out


def _vec(v):
    return v.reshape(1, -1)


def _pad_rows(w, rows):
    return jnp.pad(w, ((0, rows - w.shape[0]), (0, 0)))


def _small_lane_vec(v4, lane0):
    return jnp.pad(v4.reshape(1, -1), ((0, 0), (lane0, 128 - lane0 - v4.shape[0])))


def layer_fwd(x0, mem, W, li, rider=None, staged=None):
    T = x0.shape[0]
    tk = min(T, ATT_TK)
    n = lambda s: f"l{li}_{s}"
    R = {"x0": x0}
    (ffn1_rider, complete1), (fox_rider, complete2) = staged if staged else ((None, None), (None, None))
    R["z1"], x1, got = ffn_fwd(x0, W["ffn1_w_gate"], W["ffn1_w_up"], W["ffn1_w_down"], _vec(W["ln_ffn1_g"]), _vec(W["ln_ffn1_b"]),
                               n("ffn1_fwd"), ffn1_rider)
    if ffn1_rider is not None:
        W = complete1(W, got)
    R["x1"] = x1
    proj, pbf = mm_nn(x1, W["w_in_r"], n("proj"), also_bf16=True)
    R["proj"], R["pbf"] = proj, pbf

    w8 = _pad_rows(W["gdn_conv_w"], 8)
    qkv_hm = gdn_conv_fwd(proj, w8, n("gdn_conv_fwd"))
    alog = W["gdn_a_log"].reshape(NH, 1, 1)
    dtb = W["gdn_dt_bias"].reshape(NH, 1, 1)
    ng = W["gdn_norm_g"].reshape(1, 1, HD)
    ya, states = gdn_fwd(qkv_hm, proj, alog, dtb, ng, n("gdn_fwd"))
    R.update(qkv_hm=qkv_hm, states=states)

    bfv = _small_lane_vec(W["fox_b_f"], F_LANE)
    cum = fox_gate_fwd(proj, bfv, n("fox_gate_fwd"))
    cum4 = cum[:, F_LANE:F_LANE + NH]
    cumrow = jnp.pad(cum4.T.reshape(2, 2, T // tk, tk).transpose(0, 2, 1, 3), ((0, 0), (0, 0), (0, 6), (0, 0)))
    yb, lse, got = fox_fwd(pbf, cumrow, n("fox_fwd"), fox_rider)
    if fox_rider is not None:
        W = complete2(W, got)
    R.update(cumrow=cumrow, yb=yb, lse=lse)

    w32 = _pad_rows(W["conf_dw_w"], 32)
    yc, cc = conf_fwd(proj, w32, _vec(W["conf_dw_b"]), _vec(W["conf_norm_g"]), _vec(W["conf_norm_b"]), n("conf_fwd"))
    R["cc"] = cc

    yd, tot, rider_out = sb_fwd(pbf, n("sb_fwd"), rider)
    R["tot"] = tot

    ycat = jnp.concatenate([ya, yb, yc, yd], axis=1).astype(bf16)
    R["ycat"] = ycat
    R["z2"], x2 = lin_res_ln(ycat, W["w_out"], x1, _vec(W["ln_mix_g"]), _vec(W["ln_mix_b"]), n("mix_out"))
    R["x2"] = x2

    qm = mm_nn(x2, W["mem_w_q"], n("mem_q"), out_dtype=bf16)
    kv = mm_nn(mem, W["mem_w_kv"], n("mem_kv"))
    om = mem_fwd(qm, kv, n("mem_fwd"))
    R.update(qm=qm, kv=kv, om=om)
    R["z3"], x3 = lin_res_ln(om, W["mem_w_o"], x2, _vec(W["ln_mem_g"]), _vec(W["ln_mem_b"]), n("mem_out"))
    R["x3"] = x3
    R["z4"], x4, _ = ffn_fwd(x3, W["ffn2_w_gate"], W["ffn2_w_up"], W["ffn2_w_down"], _vec(W["ln_ffn2_g"]), _vec(W["ln_ffn2_b"]), n("ffn2_fwd"))
    return x4, R, rider_out, W


EARLY_GRADS = ("ffn2_w_gate", "ffn2_w_up", "ffn2_w_down", "mem_w_q", "mem_w_kv", "mem_w_o", "w_out")
LATE_GRADS = ("w_in", "gdn_conv_w", "conf_dw_w")
LAST_GRADS = ("ffn1_w_gate", "ffn1_w_up", "ffn1_w_down")


def grad_slots(G, names):
    out = []
    for k in names:
        s = col_shards(G[k][None]) if k in COL_SHARDED else row_shards(G[k][None])
        out.append(s if k in CONV_WEIGHTS else s.astype(bf16))
    return out


def layer_bwd(dx4, mem, W, R, li, rider=None, exchange=True):
    T = dx4.shape[0]
    n = lambda s: f"l{li}_{s}"
    G = {}

    def ffn_back(dy, z, x, pre, tag, ride=None):
        dz, dg, db = ln_bwd(dy, z, _vec(W[f"ln_{pre}_g"]), n(f"{tag}_ln_bwd"))
        (dx, a, dh, du, dzh), got = ffn_bwd(dz, x, W[f"{pre}_w_gate"], W[f"{pre}_w_up"], W[f"{pre}_w_down"], n(f"{tag}_bwd"), ride)
        G[f"{pre}_w_gate"] = mm_tn(x, dh, n(f"{tag}_dwg"))
        G[f"{pre}_w_up"] = mm_tn(x, du, n(f"{tag}_dwu"))
        G[f"{pre}_w_down"] = mm_tn(a, dzh, n(f"{tag}_dwd"))
        G[f"ln_{pre}_g"], G[f"ln_{pre}_b"] = dg.reshape(-1), db.reshape(-1)
        return dx, got

    dx3, _ = ffn_back(dx4, R["z4"], R["x3"], "ffn2", "ffn2")

    dz3, dg, db = ln_bwd(dx3, R["z3"], _vec(W["ln_mem_g"]), n("mem_ln_bwd"))
    G["ln_mem_g"], G["ln_mem_b"] = dg.reshape(-1), db.reshape(-1)
    dom = mm_nt(dz3, W["mem_w_o"], n("mem_dom"))
    G["mem_w_o"] = mm_tn(R["om"], dz3, n("mem_dwo"))
    dqm, dkv = mem_bwd(dom, R["qm"], R["kv"], n("mem_bwd"))
    G["mem_w_q"] = mm_tn(R["x2"], dqm, n("mem_dwq"))
    G["mem_w_kv"] = mm_tn(mem, dkv, n("mem_dwkv"))
    dx2 = mm_nt(dqm, W["mem_w_q"], n("mem_dx"), add=dz3, add_scale=ALPHA)

    dz2, dg, db = ln_bwd(dx2, R["z2"], _vec(W["ln_mix_g"]), n("mix_ln_bwd"))
    G["ln_mix_g"], G["ln_mix_b"] = dg.reshape(-1), db.reshape(-1)
    dycat = mm_nt(dz2, W["w_out"], n("mix_dycat"))
    G["w_out"] = mm_tn(R["ycat"], dz2, n("mix_dwout"))
    dyb, dyc, dyd = (dycat[:, i * GW:(i + 1) * GW] for i in range(1, 4))
    proj, pbf = R["proj"], R["pbf"]

    alog = W["gdn_a_log"].reshape(NH, 1, 1)
    dtb = W["gdn_dt_bias"].reshape(NH, 1, 1)
    ng = W["gdn_norm_g"].reshape(1, 1, HD)
    dqkv_hm, dgz, dsmall_ab, dalog, ddtb, dng = gdn_bwd(dycat, R["states"], R["qkv_hm"], proj, alog, dtb, ng, n("gdn_bwd"))
    G["gdn_a_log"], G["gdn_dt_bias"], G["gdn_norm_g"] = dalog.reshape(-1), ddtb.reshape(-1), dng.reshape(-1)
    w8 = _pad_rows(W["gdn_conv_w"], 8)
    dgqkv, dw8 = gdn_conv_bwd(dqkv_hm, proj, w8, n("gdn_conv_bwd"))
    G["gdn_conv_w"] = dw8[:GDN_K]

    dfq, dfk, dfv, dcumrow, dcumq, got_early = fox_bwd(dyb, R["yb"], R["lse"], pbf, R["cumrow"], n("fox_bwd"),
                                                      exchange_rider(grad_slots(G, EARLY_GRADS)) if exchange else None)
    dcum4 = dcumrow[:, :, 0:2, :].transpose(0, 2, 1, 3).reshape(4, T).T
    dcum4 = dcum4 + dcumq[:, :, 0:2].transpose(1, 0, 2).reshape(T, 4)
    dcum = jnp.pad(dcum4, ((0, 0), (F_LANE, 128 - F_LANE - NH)))
    bfv = _small_lane_vec(W["fox_b_f"], F_LANE)
    dsmall_f, dbf = fox_gate_bwd(dcum, proj, bfv, n("fox_gate_bwd"))
    G["fox_b_f"] = dbf[0, F_LANE:F_LANE + NH]

    w32 = _pad_rows(W["conf_dw_w"], 32)
    dglu, dw32, dcb, dcg, dcbeta = conf_bwd(dyc, R["cc"], proj, w32, _vec(W["conf_norm_g"]), _vec(W["conf_norm_b"]), n("conf_bwd"))
    G["conf_dw_w"], G["conf_dw_b"] = dw32[:CONF_K], dcb.reshape(-1)
    G["conf_norm_g"], G["conf_norm_b"] = dcg.reshape(-1), dcbeta.reshape(-1)

    dsq, dsk, dsv, got_carried = sb_bwd(dyd, R["tot"], pbf, n("sb_bwd"), rider)

    dsmall = dsmall_ab + dsmall_f
    dproj = [dgqkv, dgz, dfq, dfk, dfv, dglu, dsq, dsk, dsv, dsmall]
    G["w_in"] = ungroup_w_in(mm_tn(R["x1"], dproj, n("proj_dw")))
    dx1 = mm_nt(dproj, W["w_in_r"], n("proj_dx"), add=dz2, add_scale=ALPHA)

    dx0, got_late = ffn_back(dx1, R["z1"], R["x0"], "ffn1", "ffn1", exchange_rider(grad_slots(G, LATE_GRADS)) if exchange else None)
    return dx0, G, {"carried": got_carried, "early": got_early, "late": got_late}


def _step(P, M, V, x, mem, loss_target):
    xs, mems, tgt = x[0], mem[0], loss_target[0]

    big = COL_SHARDED + ROW_SHARDED

    def weight_gather(li, names):
        return gather_rider([P[k][li:li + 1] if k in CONV_WEIGHTS else P[k][li:li + 1].astype(bf16) for k in names],
                            [k not in CONV_WEIGHTS for k in names])

    def layer_weights(li, names, gathered, W=None):
        W = dict(W) if W else {k: P[k][li] for k in REPLICATED}
        W.update({k: (join_cols(g) if k in COL_SHARDED else join_rows(g))[0] for k, g in zip(names, gathered)})
        if "w_in" in names:
            W["w_in_r"] = regroup_w_in(W["w_in"])
        return W

    W = layer_weights(0, LAST_GRADS, run_rider(weight_gather(0, LAST_GRADS), "gather_weights_first"))
    staged = tuple((weight_gather(0, names), functools.partial(lambda names, W0, got: layer_weights(0, names, got, W0), names))
                   for names in (LATE_GRADS, EARLY_GRADS))
    layers, saved = [], []
    h = xs
    for li in range(DEPTH):
        nxt = weight_gather(li + 1, big) if li + 1 < DEPTH else None
        h, R, gathered, W = layer_fwd(h, mems, W, li, nxt, staged if li == 0 else None)
        layers.append(W)
        saved.append(R)
        if li + 1 < DEPTH:
            W = layer_weights(li + 1, big, gathered)
    loss_row, dy = loss_and_grad(h, tgt, "loss")
    loss = lax.psum(loss_row[0, 0], ("x", "y", "c"))

    grads, received_at = [None] * DEPTH, [dict() for _ in range(DEPTH)]
    rider = None
    for li in reversed(range(DEPTH)):
        dy, G, got = layer_bwd(dy, mems, layers[li], saved[li], li, rider)
        if rider is not None:
            received_at[li + 1].update(zip(LAST_GRADS, got["carried"]))
        received_at[li].update(zip(EARLY_GRADS, got["early"]))
        received_at[li].update(zip(LATE_GRADS, got["late"]))
        grads[li] = G
        rider = exchange_rider(grad_slots(G, LAST_GRADS))
    received_at[0].update(zip(LAST_GRADS, run_rider(rider, "exchange_grads_last")))
    grad_x = dy[None]

    stacked = {k: jnp.stack([grads[li][k] for li in range(DEPTH)]) for k in WEIGHTS}
    received = [jnp.concatenate([received_at[li][k] for li in range(DEPTH)], axis=1) for k in big]
    partial_sums = []
    for k, r in zip(big, received):
        shp = r.shape
        partial_sums.append(sum_slots(r.reshape(4, -1, shp[-1]), f"sum_{k}"))
    from_sibling = swap_sibling(partial_sums, "swap_partials")

    out_g, out_d, out_m, out_v = {}, {}, {}, {}
    for k, mine, theirs in zip(big, partial_sums, from_sibling):
        shp = P[k].shape
        flat = lambda t: t.reshape(-1, shp[-1])
        g, d, mn, vn = adamw(flat(P[k]), mine, theirs, flat(M[k]), flat(V[k]), f"adamw_{k}")
        out_g[k], out_d[k], out_m[k], out_v[k] = (t.reshape(shp) for t in (g, d, mn, vn))

    gsmall = sum_slots(gather_all(pack_small(stacked), "gather_small"), "sum_small")
    g, d, mn, vn = adamw(pack_small(P), gsmall, None, pack_small(M), pack_small(V), "adamw_small")
    for dst, packed in ((out_g, g), (out_d, d), (out_m, mn), (out_v, vn)):
        dst.update(unpack_small(packed, P))

    return (loss, grad_x, *[out_g[k] for k in WEIGHTS], *[out_d[k] for k in WEIGHTS],
            *[out_m[k] for k in WEIGHTS], *[out_v[k] for k in WEIGHTS])


def kernel(x, mem, ffn1_w_gate, ffn1_w_up, ffn1_w_down, ln_ffn1_g, ln_ffn1_b, w_in, gdn_conv_w, gdn_a_log, gdn_dt_bias, gdn_norm_g, fox_b_f, conf_dw_w, conf_dw_b, conf_norm_g, conf_norm_b, w_out, ln_mix_g, ln_mix_b, mem_w_q, mem_w_kv, mem_w_o, ln_mem_g, ln_mem_b, ffn2_w_gate, ffn2_w_up, ffn2_w_down, ln_ffn2_g, ln_ffn2_b, loss_target, m_ffn1_w_gate, m_ffn1_w_up, m_ffn1_w_down, m_ln_ffn1_g, m_ln_ffn1_b, m_w_in, m_gdn_conv_w, m_gdn_a_log, m_gdn_dt_bias, m_gdn_norm_g, m_fox_b_f, m_conf_dw_w, m_conf_dw_b, m_conf_norm_g, m_conf_norm_b, m_w_out, m_ln_mix_g, m_ln_mix_b, m_mem_w_q, m_mem_w_kv, m_mem_w_o, m_ln_mem_g, m_ln_mem_b, m_ffn2_w_gate, m_ffn2_w_up, m_ffn2_w_down, m_ln_ffn2_g, m_ln_ffn2_b, v_ffn1_w_gate, v_ffn1_w_up, v_ffn1_w_down, v_ln_ffn1_g, v_ln_ffn1_b, v_w_in, v_gdn_conv_w, v_gdn_a_log, v_gdn_dt_bias, v_gdn_norm_g, v_fox_b_f, v_conf_dw_w, v_conf_dw_b, v_conf_norm_g, v_conf_norm_b, v_w_out, v_ln_mix_g, v_ln_mix_b, v_mem_w_q, v_mem_w_kv, v_mem_w_o, v_ln_mem_g, v_ln_mem_b, v_ffn2_w_gate, v_ffn2_w_up, v_ffn2_w_down, v_ln_ffn2_g, v_ln_ffn2_b):
    a = locals()
    P = {k: a[k] for k in WEIGHTS}
    M = {k: a["m_" + k] for k in WEIGHTS}
    V = {k: a["v_" + k] for k in WEIGHTS}
    return _step(P, M, V, x, mem, loss_target)
```

```python
import functools

import jax
import jax.numpy as jnp
from jax import lax
from jax.experimental import pallas as pl
from jax.experimental.pallas import tpu as pltpu

f32 = jnp.float32
bf16 = jnp.bfloat16

D = 1024
F = 2816
GW = 256
HD = 64
NH = 4
CHUNK = 64
CONF_K = 31
GDN_K = 4
DEPTH = 2
ALPHA = float((2 * DEPTH) ** 0.25)
LN_EPS = 1e-5
RMS_EPS = 1e-6
L2_EPS = 1e-6
NEG = -1e30
PW = 3200
C_GQKV, C_GZ, C_FOX, C_CONF, C_SB, C_SMALL = 0, 768, 1024, 1792, 2304, 3072
ADAM_LR, ADAM_B1, ADAM_B2, ADAM_EPS, ADAM_WD, ADAM_STEP = 0.001, 0.9, 0.999, 1e-08, 0.01, 10
VMEM_LIMIT = 56 * 1024 * 1024
MESH = pl.DeviceIdType.MESH


def _cparams(sem):
    return pltpu.CompilerParams(dimension_semantics=sem, vmem_limit_bytes=VMEM_LIMIT)


def _pick(n, cands):
    for c in cands:
        if n % c == 0:
            return c
    return n


def _sds(shape, dtype=f32):
    return jax.ShapeDtypeStruct(shape, dtype)


def _layer_norm(z, g, b):
    mu = jnp.mean(z, axis=-1, keepdims=True)
    zc = z - mu
    var = jnp.mean(zc * zc, axis=-1, keepdims=True)
    return zc * lax.rsqrt(var + LN_EPS) * g + b


def _softplus(x):
    return jnp.maximum(x, 0.0) + jnp.log(1.0 + jnp.exp(-jnp.abs(x)))


def _neg_softplus(z):
    nz = -z
    return jnp.minimum(nz, 0.0) - jnp.log(1.0 + jnp.exp(jnp.minimum(z, nz)))


def _dsilu(x):
    s = jax.nn.sigmoid(x)
    return s * (1.0 + x * (1.0 - s))


def _split_hi_lo(x):
    hi = x.astype(bf16)
    lo = (x - hi.astype(f32)).astype(bf16)
    return hi, lo


def _dot(a, b):
    return jnp.dot(a, b, preferred_element_type=f32)


def _dot_nt(a, b):
    return lax.dot_general(a, b, (((1,), (1,)), ((), ())), preferred_element_type=f32)


def _dot_tn(a, b):
    return lax.dot_general(a, b, (((0,), (0,)), ((), ())), preferred_element_type=f32)


def mm_nn(a, w, name, out_dtype=f32, also_bf16=False):
    T, K = a.shape
    N = w.shape[1]
    tm = min(T, 512)
    tn = N if N <= 1024 else _pick(N, (640, 512))

    def body(a_ref, w_ref, *o_refs):
        r = _dot(a_ref[...].astype(bf16), w_ref[...].astype(bf16))
        o_refs[0][...] = r.astype(o_refs[0].dtype)
        if also_bf16:
            o_refs[1][...] = r.astype(bf16)

    out_shape = [_sds((T, N), out_dtype)]
    out_specs = [pl.BlockSpec((tm, tn), lambda i, j: (i, j))]
    if also_bf16:
        out_shape.append(_sds((T, N), bf16))
        out_specs.append(pl.BlockSpec((tm, tn), lambda i, j: (i, j)))
    res = pl.pallas_call(
        body, grid=(T // tm, N // tn),
        in_specs=[pl.BlockSpec((tm, K), lambda i, j: (i, 0)), pl.BlockSpec((K, tn), lambda i, j: (0, j))],
        out_specs=out_specs, out_shape=out_shape,
        compiler_params=_cparams(("parallel", "arbitrary")), name=name)(a, w)
    return res if also_bf16 else res[0]


def _pieces(g):
    gs = list(g) if isinstance(g, (list, tuple)) else [g]
    offs = [sum(p.shape[1] for p in gs[:i]) for i in range(len(gs))]
    return gs, offs, offs[-1] + gs[-1].shape[1]


def mm_nt(g, w, name, add=None, add_scale=1.0):
    gs, offs, N = _pieces(g)
    T = gs[0].shape[0]
    K = w.shape[0]
    tm = min(T, 512)
    ng = len(gs)

    def body(*refs):
        w_ref, o_ref = refs[ng], refs[-1]
        r = None
        for g_ref, off in zip(refs[:ng], offs):
            part = _dot_nt(g_ref[...].astype(bf16), w_ref[:, off:off + g_ref.shape[1]].astype(bf16))
            r = part if r is None else r + part
        if add is not None:
            r = r + add_scale * refs[ng + 1][...]
        o_ref[...] = r

    in_specs = [pl.BlockSpec((tm, p.shape[1]), lambda i: (i, 0)) for p in gs] + [pl.BlockSpec((K, N), lambda i: (0, 0))]
    args = gs + [w]
    if add is not None:
        in_specs.append(pl.BlockSpec((tm, K), lambda i: (i, 0)))
        args.append(add)
    return pl.pallas_call(
        body, grid=(T // tm,), in_specs=in_specs,
        out_specs=pl.BlockSpec((tm, K), lambda i: (i, 0)), out_shape=_sds((T, K)),
        compiler_params=_cparams(("parallel",)), name=name)(*args)


def mm_tn(a, g, name, rider=None):
    gs, offs, N = _pieces(g)
    T, K = a.shape
    tk = K if K * N * 4 <= 14 * 1024 * 1024 else _pick(K, (512, 1408))
    row_bytes = tk * a.dtype.itemsize + sum(p.shape[1] * p.dtype.itemsize for p in gs)
    tt = min(T, 1024 if 2 * (tk * N * 4 + 1024 * row_bytes) <= 44 * 1024 * 1024 else 512)

    def body(a_ref, *refs):
        o_ref = refs[-1]

        @pl.when(pl.program_id(1) == 0)
        def _():
            o_ref[...] = jnp.zeros_like(o_ref)
        if len(gs) == 1:
            o_ref[...] += _dot_tn(a_ref[...].astype(bf16), refs[0][...].astype(bf16))
        else:
            at = a_ref[...].astype(bf16).T
            for g_ref, off in zip(refs[:-1], offs):
                o_ref[:, off:off + g_ref.shape[1]] += _dot(at, g_ref[...].astype(bf16))

    body, r_in, r_args, r_out, r_shape, r_scratch = _ride(body, 1 + len(gs), 1, rider, (K // tk, T // tt))
    res = pl.pallas_call(
        body, grid=(K // tk, T // tt),
        in_specs=[pl.BlockSpec((tt, tk), lambda j, t: (t, j))] + [pl.BlockSpec((tt, p.shape[1]), lambda j, t: (t, 0)) for p in gs] + r_in,
        out_specs=[pl.BlockSpec((tk, N), lambda j, t: (j, 0))] + r_out, out_shape=[_sds((K, N))] + r_shape,
        scratch_shapes=r_scratch,
        compiler_params=_cparams(("arbitrary", "arbitrary")), name=name)(a, *gs, *r_args)
    return res[0] if rider is None else (res[0], res[1:])


FFN_TF = 1408


def ffn_fwd(x, wg, wu, wd, g, b, name, rider=None):
    T = x.shape[0]
    tm = min(T, 512)
    nf = F // FFN_TF

    def body(x_ref, wg_ref, wu_ref, wd_ref, g_ref, b_ref, z_ref, y_ref, acc):
        j = pl.program_id(1)

        @pl.when(j == 0)
        def _():
            acc[...] = jnp.zeros_like(acc)

        xb = x_ref[...].astype(bf16)
        h = _dot(xb, wg_ref[...])
        u = _dot(xb, wu_ref[...])
        a = (h * jax.nn.sigmoid(h) * u).astype(bf16)
        acc[...] += _dot(a, wd_ref[...])

        @pl.when(j == nf - 1)
        def _():
            z = ALPHA * x_ref[...] + 0.5 * acc[...]
            z_ref[...] = z
            y_ref[...] = _layer_norm(z, g_ref[...], b_ref[...])

    row = pl.BlockSpec((tm, D), lambda i, j: (i, 0))
    vec = pl.BlockSpec((1, D), lambda i, j: (0, 0))
    body, r_in, r_args, r_out, r_shape, r_scratch = _ride(body, 6, 2, rider, (T // tm, nf), 1)
    res = pl.pallas_call(
        body, grid=(T // tm, nf),
        in_specs=[row, pl.BlockSpec((D, FFN_TF), lambda i, j: (0, j)), pl.BlockSpec((D, FFN_TF), lambda i, j: (0, j)),
                  pl.BlockSpec((FFN_TF, D), lambda i, j: (j, 0)), vec, vec] + r_in,
        out_specs=[row, row] + r_out, out_shape=[_sds((T, D)), _sds((T, D))] + r_shape,
        scratch_shapes=[pltpu.VMEM((tm, D), f32)] + r_scratch,
        compiler_params=_cparams(("arbitrary", "arbitrary")), name=name)(x, wg, wu, wd, g, b, *r_args)
    return res[0], res[1], res[2:]


def _ln_bwd_rows(dy, zv, g):
    mu = jnp.mean(zv, axis=-1, keepdims=True)
    zc = zv - mu
    rstd = lax.rsqrt(jnp.mean(zc * zc, axis=-1, keepdims=True) + LN_EPS)
    xh = zc * rstd
    dxh = dy * g
    m1 = jnp.mean(dxh, axis=-1, keepdims=True)
    m2 = jnp.mean(dxh * xh, axis=-1, keepdims=True)
    return rstd * (dxh - m1 - xh * m2), jnp.sum(dy * xh, axis=0, keepdims=True), jnp.sum(dy, axis=0, keepdims=True)


def ffn_bwd(dz, x, wg, wu, wd, name, rider=None):
    T = x.shape[0]
    tm = min(T, 512)
    nf = F // FFN_TF

    def body(dz_ref, x_ref, wg_ref, wu_ref, wd_ref, dx_ref, a_ref, dh_ref, du_ref, dzh_ref, acc):
        j = pl.program_id(1)

        @pl.when(j == 0)
        def _():
            acc[...] = jnp.zeros_like(acc)

        dzh = (0.5 * dz_ref[...]).astype(bf16)
        xb = x_ref[...].astype(bf16)
        h = _dot(xb, wg_ref[...])
        u = _dot(xb, wu_ref[...])
        s = jax.nn.sigmoid(h)
        hs = h * s
        da = _dot_nt(dzh, wd_ref[...])
        du = (da * hs).astype(bf16)
        dh = (da * u * (s + hs * (1.0 - s))).astype(bf16)
        a_ref[...] = (hs * u).astype(bf16)
        dh_ref[...] = dh
        du_ref[...] = du
        acc[...] += _dot_nt(dh, wg_ref[...]) + _dot_nt(du, wu_ref[...])

        @pl.when(j == nf - 1)
        def _():
            dx_ref[...] = ALPHA * dz_ref[...] + acc[...]
            dzh_ref[...] = dzh

    row = pl.BlockSpec((tm, D), lambda i, j: (i, 0))
    wide = pl.BlockSpec((tm, FFN_TF), lambda i, j: (i, j))
    body, r_in, r_args, r_out, r_shape, r_scratch = _ride(body, 5, 5, rider, (T // tm, nf), 1)
    res = pl.pallas_call(
        body, grid=(T // tm, nf),
        in_specs=[row, row, pl.BlockSpec((D, FFN_TF), lambda i, j: (0, j)), pl.BlockSpec((D, FFN_TF), lambda i, j: (0, j)),
                  pl.BlockSpec((FFN_TF, D), lambda i, j: (j, 0))] + r_in,
        out_specs=[row, wide, wide, wide, row] + r_out,
        out_shape=[_sds((T, D)), _sds((T, F), bf16), _sds((T, F), bf16), _sds((T, F), bf16), _sds((T, D), bf16)] + r_shape,
        scratch_shapes=[pltpu.VMEM((tm, D), f32)] + r_scratch,
        compiler_params=_cparams(("arbitrary", "arbitrary")), name=name)(dz, x, wg, wu, wd, *r_args)
    return res[:5], res[5:]


def lin_res_ln(a, w, res, g, b, name):
    T, K = a.shape
    tm = min(T, 512)

    def body(a_ref, w_ref, res_ref, g_ref, b_ref, z_ref, y_ref):
        z = ALPHA * res_ref[...] + _dot(a_ref[...].astype(bf16), w_ref[...])
        z_ref[...] = z
        y_ref[...] = _layer_norm(z, g_ref[...], b_ref[...])

    row = pl.BlockSpec((tm, D), lambda i: (i, 0))
    vec = pl.BlockSpec((1, D), lambda i: (0, 0))
    return pl.pallas_call(
        body, grid=(T // tm,),
        in_specs=[pl.BlockSpec((tm, K), lambda i: (i, 0)), pl.BlockSpec((K, D), lambda i: (0, 0)), row, vec, vec],
        out_specs=[row, row], out_shape=[_sds((T, D)), _sds((T, D))],
        compiler_params=_cparams(("parallel",)), name=name)(a, w, res, g, b)


def ln_bwd(dy, z, g, name):
    T = z.shape[0]
    tm = min(T, 512)

    def body(dy_ref, z_ref, g_ref, dz_ref, dg_ref, db_ref):
        @pl.when(pl.program_id(0) == 0)
        def _():
            dg_ref[...] = jnp.zeros_like(dg_ref)
            db_ref[...] = jnp.zeros_like(db_ref)

        dz, dg, db = _ln_bwd_rows(dy_ref[...], z_ref[...], g_ref[...])
        dz_ref[...] = dz
        dg_ref[...] += dg
        db_ref[...] += db

    row = pl.BlockSpec((tm, D), lambda i: (i, 0))
    vec = pl.BlockSpec((1, D), lambda i: (0, 0))
    return pl.pallas_call(
        body, grid=(T // tm,), in_specs=[row, row, vec], out_specs=[row, vec, vec],
        out_shape=[_sds((T, D)), _sds((1, D)), _sds((1, D))],
        compiler_params=_cparams(("arbitrary",)), name=name)(dy, z, g)


def loss_and_grad(y, target, name):
    T = y.shape[0]
    tm = min(T, 512)

    def body(y_ref, t_ref, l_ref, dy_ref):
        @pl.when(pl.program_id(0) == 0)
        def _():
            l_ref[...] = jnp.zeros_like(l_ref)
        d = y_ref[...] - t_ref[...]
        dy_ref[...] = d * (1.0 / D)
        l_ref[...] += (0.5 / D) * jnp.sum(jnp.sum(d * d, axis=1, keepdims=True), axis=0, keepdims=True)

    row = pl.BlockSpec((tm, D), lambda i: (i, 0))
    return pl.pallas_call(
        body, grid=(T // tm,), in_specs=[row, row],
        out_specs=[pl.BlockSpec((1, 128), lambda i: (0, 0)), row],
        out_shape=[_sds((1, 128)), _sds((T, D))],
        compiler_params=_cparams(("arbitrary",)), name=name)(y, target)


def _shifted(ext, s):
    return ext if s == 0 else pltpu.roll(ext, s, 0)


def _halo_maps(tm, P, nblk):
    per = tm // P
    prev = lambda i, c: (jnp.maximum(i * per - 1, 0), c)
    nxt = lambda i, c: (jnp.minimum((i + 1) * per, nblk * per - 1), c)
    return prev, nxt


GDN_P = 8
CONF_P = 32


def gdn_conv_fwd(proj, w8, name):
    T = proj.shape[0]
    tm = min(T, 512)
    nblk = T // tm
    C = 3 * GW
    prev, _ = _halo_maps(tm, GDN_P, nblk)

    def body(xc_ref, xp_ref, w_ref, o_ref):
        i = pl.program_id(0)
        xp = jnp.where(i > 0, xp_ref[...], 0.0)
        ext = jnp.concatenate([xp, xc_ref[...]], axis=0)
        acc = jnp.zeros((tm, C), f32)
        for k in range(GDN_K):
            acc = acc + w_ref[k:k + 1, :] * _shifted(ext, GDN_K - 1 - k)[GDN_P:, :]
        y = acc * jax.nn.sigmoid(acc)
        for h in range(3 * NH):
            o_ref[h] = y[:, h * HD:(h + 1) * HD]

    return pl.pallas_call(
        body, grid=(nblk,),
        in_specs=[pl.BlockSpec((tm, C), lambda i: (i, 0)), pl.BlockSpec((GDN_P, C), lambda i: prev(i, 0)),
                  pl.BlockSpec((8, C), lambda i: (0, 0))],
        out_specs=pl.BlockSpec((3 * NH, tm, HD), lambda i: (0, i, 0)), out_shape=_sds((3 * NH, T, HD)),
        compiler_params=_cparams(("parallel",)), name=name)(proj, proj, w8)


def gdn_conv_bwd(dy, proj, w8, name):
    T = proj.shape[0]
    tm = min(T, 512)
    nblk = T // tm
    C = 3 * GW
    P = GDN_P
    prev, nxt = _halo_maps(tm, P, nblk)

    def body(dyc_ref, dyn_ref, xc_ref, xp_ref, xn_ref, w_ref, dx_ref, dw_ref):
        i = pl.program_id(0)

        @pl.when(i == 0)
        def _():
            dw_ref[...] = jnp.zeros_like(dw_ref)

        xp = jnp.where(i > 0, xp_ref[...], 0.0)
        last = i == nblk - 1
        xn = jnp.where(last, 0.0, xn_ref[...])
        dyn = jnp.where(last, 0.0, jnp.concatenate([dyn_ref[h] for h in range(3 * NH)], axis=1))
        ext = jnp.concatenate([xp, xc_ref[...], xn], axis=0)
        sh = [_shifted(ext, GDN_K - 1 - k)[P:, :] for k in range(GDN_K)]
        s = jnp.zeros((tm + P, C), f32)
        for k in range(GDN_K):
            s = s + w_ref[k:k + 1, :] * sh[k]
        dyc = jnp.concatenate([dyc_ref[h] for h in range(3 * NH)], axis=1)
        ds = jnp.concatenate([dyc, dyn], axis=0) * _dsilu(s)
        dx = jnp.zeros((tm, C), f32)
        for k in range(GDN_K):
            d = GDN_K - 1 - k
            moved = ds if d == 0 else pltpu.roll(ds, tm + P - d, 0)
            dx = dx + w_ref[k:k + 1, :] * moved[:tm, :]
            dw_ref[k:k + 1, :] += jnp.sum(ds[:tm, :] * sh[k][:tm, :], axis=0, keepdims=True)
        dx_ref[...] = dx

    col = lambda i: (i, 0)
    return pl.pallas_call(
        body, grid=(nblk,),
        in_specs=[pl.BlockSpec((3 * NH, tm, HD), lambda i: (0, i, 0)), pl.BlockSpec((3 * NH, P, HD), lambda i: (0, nxt(i, 0)[0], 0)),
                  pl.BlockSpec((tm, C), col), pl.BlockSpec((P, C), lambda i: prev(i, 0)),
                  pl.BlockSpec((P, C), lambda i: nxt(i, 0)), pl.BlockSpec((8, C), lambda i: (0, 0))],
        out_specs=[pl.BlockSpec((tm, C), col), pl.BlockSpec((8, C), lambda i: (0, 0))],
        out_shape=[_sds((T, C)), _sds((8, C))],
        compiler_params=_cparams(("arbitrary",)), name=name)(dy, dy, proj, proj, proj, w8)


def _group_ones():
    r = lax.broadcasted_iota(jnp.int32, (GW, GW), 0) // HD
    c = lax.broadcasted_iota(jnp.int32, (GW, GW), 1) // HD
    return (r == c).astype(bf16)


def _group_mean(x, ones):
    hi, lo = _split_hi_lo(x)
    return (_dot(hi, ones) + _dot(lo, ones)) * (1.0 / HD)


def _conf_norm(c, g, b, ones):
    mu = _group_mean(c, ones)
    cc = c - mu
    rstd = lax.rsqrt(_group_mean(cc * cc, ones) + LN_EPS)
    hn = cc * rstd
    return hn, rstd, hn * g + b


CONF_VAL_BLK = C_CONF // GW
CONF_GATE_BLK = C_CONF // GW + 1


def conf_fwd(proj, w32, bias, ng, nb, name):
    T = proj.shape[0]
    tm = min(T, 512)
    nblk = T // tm
    P = CONF_P
    prev, _ = _halo_maps(tm, P, nblk)

    def body(vc_ref, gc_ref, vp_ref, gp_ref, w_ref, bias_ref, ng_ref, nb_ref, y_ref, c_ref):
        i = pl.program_id(0)
        pc = vc_ref[...] * jax.nn.sigmoid(gc_ref[...])
        pp = jnp.where(i > 0, vp_ref[...] * jax.nn.sigmoid(gp_ref[...]), 0.0)
        ext = jnp.concatenate([pp, pc], axis=0)
        acc = jnp.zeros((tm, GW), f32)
        for k in range(CONF_K):
            acc = acc + w_ref[k:k + 1, :] * _shifted(ext, CONF_K - 1 - k)[P:, :]
        c = acc + bias_ref[...]
        c_ref[...] = c
        _, _, yn = _conf_norm(c, ng_ref[...], nb_ref[...], _group_ones())
        y_ref[...] = yn * jax.nn.sigmoid(yn)

    vec = pl.BlockSpec((1, GW), lambda i: (0, 0))
    return pl.pallas_call(
        body, grid=(nblk,),
        in_specs=[pl.BlockSpec((tm, GW), lambda i: (i, CONF_VAL_BLK)), pl.BlockSpec((tm, GW), lambda i: (i, CONF_GATE_BLK)),
                  pl.BlockSpec((P, GW), lambda i: prev(i, CONF_VAL_BLK)), pl.BlockSpec((P, GW), lambda i: prev(i, CONF_GATE_BLK)),
                  pl.BlockSpec((32, GW), lambda i: (0, 0)), vec, vec, vec],
        out_specs=[pl.BlockSpec((tm, GW), lambda i: (i, 0))] * 2, out_shape=[_sds((T, GW))] * 2,
        compiler_params=_cparams(("parallel",)), name=name)(proj, proj, proj, proj, w32, bias, ng, nb)


def conf_bwd(dy, c, proj, w32, ng, nb, name):
    T = proj.shape[0]
    tm = min(T, 512)
    nblk = T // tm
    P = CONF_P
    prev, nxt = _halo_maps(tm, P, nblk)

    def body(dyc_ref, dyn_ref, cc_ref, cn_ref, vc_ref, gc_ref, vp_ref, gp_ref, w_ref, ng_ref, nb_ref,
             dglu_ref, dw_ref, dbias_ref, dng_ref, dnb_ref):
        i = pl.program_id(0)

        @pl.when(i == 0)
        def _():
            dw_ref[...] = jnp.zeros_like(dw_ref)
            dbias_ref[...] = jnp.zeros_like(dbias_ref)
            dng_ref[...] = jnp.zeros_like(dng_ref)
            dnb_ref[...] = jnp.zeros_like(dnb_ref)

        ones = _group_ones()
        g = ng_ref[...]

        def dc_of(dyv, cv):
            hn, rstd, yn = _conf_norm(cv, g, nb_ref[...], ones)
            dyn_ = dyv * _dsilu(yn)
            dhn = dyn_ * g
            dc = rstd * (dhn - _group_mean(dhn, ones) - hn * _group_mean(dhn * hn, ones))
            return dc, dyn_, hn

        dc_c, dyn_c, hn_c = dc_of(dyc_ref[...], cc_ref[...])
        dc_n, _, _ = dc_of(dyn_ref[...], cn_ref[...])
        dc_n = jnp.where(i == nblk - 1, 0.0, dc_n)
        dng_ref[...] += jnp.sum(dyn_c * hn_c, axis=0, keepdims=True)
        dnb_ref[...] += jnp.sum(dyn_c, axis=0, keepdims=True)
        dbias_ref[...] += jnp.sum(dc_c, axis=0, keepdims=True)

        sig_c = jax.nn.sigmoid(gc_ref[...])
        val_c = vc_ref[...]
        pc = val_c * sig_c
        pp = jnp.where(i > 0, vp_ref[...] * jax.nn.sigmoid(gp_ref[...]), 0.0)
        ext = jnp.concatenate([pp, pc], axis=0)
        dext = jnp.concatenate([dc_c, dc_n], axis=0)
        dp = jnp.zeros((tm, GW), f32)
        for k in range(CONF_K):
            d = CONF_K - 1 - k
            moved = dext if d == 0 else pltpu.roll(dext, tm + P - d, 0)
            dp = dp + w_ref[k:k + 1, :] * moved[:tm, :]
            dw_ref[k:k + 1, :] += jnp.sum(dc_c * _shifted(ext, d)[P:, :], axis=0, keepdims=True)
        dglu_ref[:, 0:GW] = dp * sig_c
        dglu_ref[:, GW:2 * GW] = dp * val_c * sig_c * (1.0 - sig_c)

    vec = pl.BlockSpec((1, GW), lambda i: (0, 0))
    blk = pl.BlockSpec((tm, GW), lambda i: (i, 0))
    return pl.pallas_call(
        body, grid=(nblk,),
        in_specs=[blk, pl.BlockSpec((P, GW), lambda i: nxt(i, 0)), blk, pl.BlockSpec((P, GW), lambda i: nxt(i, 0)),
                  pl.BlockSpec((tm, GW), lambda i: (i, CONF_VAL_BLK)), pl.BlockSpec((tm, GW), lambda i: (i, CONF_GATE_BLK)),
                  pl.BlockSpec((P, GW), lambda i: prev(i, CONF_VAL_BLK)), pl.BlockSpec((P, GW), lambda i: prev(i, CONF_GATE_BLK)),
                  pl.BlockSpec((32, GW), lambda i: (0, 0)), vec, vec],
        out_specs=[pl.BlockSpec((tm, 2 * GW), lambda i: (i, 0)), pl.BlockSpec((32, GW), lambda i: (0, 0)), vec, vec, vec],
        out_shape=[_sds((T, 2 * GW)), _sds((32, GW)), _sds((1, GW)), _sds((1, GW)), _sds((1, GW))],
        compiler_params=_cparams(("arbitrary",)), name=name)(dy, dy, c, c, proj, proj, proj, proj, w32, ng, nb)


def _mm_raw(a, b, ta, tb):
    ca = a.ndim - 2 if ta else a.ndim - 1
    cb = b.ndim - 1 if tb else b.ndim - 2
    batch = ((0,), (0,)) if a.ndim == 3 else ((), ())
    return lax.dot_general(a, b, (((ca,), (cb,)), batch), preferred_element_type=f32)


def _mm_prec(a, b, ta, tb, prec):
    if prec == 1:
        return _mm_raw(a.astype(bf16), b.astype(bf16), ta, tb)
    bh, bl = _split_hi_lo(b)
    if prec == 2:
        ab = a.astype(bf16)
        return _mm_raw(ab, bh, ta, tb) + _mm_raw(ab, bl, ta, tb)
    ah, al = _split_hi_lo(a)
    return _mm_raw(ah, bh, ta, tb) + (_mm_raw(ah, bl, ta, tb) + _mm_raw(al, bh, ta, tb))


@functools.partial(jax.custom_vjp, nondiff_argnums=(2, 3, 4))
def mm(a, b, ta=False, tb=False, prec=1):
    return _mm_prec(a, b, ta, tb, prec)


def _mm_fwd(a, b, ta, tb, prec):
    return _mm_prec(a, b, ta, tb, prec), (a, b)


def _mm_bwd(ta, tb, prec, res, ct):
    a, b = res
    da = _mm_prec(b, ct, tb, True, 1) if ta else _mm_prec(ct, b, False, not tb, 1)
    db = _mm_prec(ct, a, True, ta, 1) if tb else _mm_prec(a, ct, not ta, False, 2 if prec == 2 else 1)
    return da, db


mm.defvjp(_mm_fwd, _mm_bwd)


def _tri_inv_raw(l):
    n = -l
    rr = lax.broadcasted_iota(jnp.int32, l.shape, 1)
    cc = lax.broadcasted_iota(jnp.int32, l.shape, 2)
    p = jnp.where(rr == cc, 1.0, 0.0) + n
    for _ in range(5):
        n = _mm_prec(n, n, False, False, 1)
        p = p + _mm_prec(p, n, False, False, 1)
    return p


@jax.custom_vjp
def tri_inv(l):
    return _tri_inv_raw(l)


def _tri_inv_fwd(l):
    t = _tri_inv_raw(l)
    return t, t


def _tri_inv_bwd(t, ct):
    return (-_mm_prec(_mm_prec(t, ct, True, False, 1), t, False, True, 1),)


tri_inv.defvjp(_tri_inv_fwd, _tri_inv_bwd)


def _gdn_block(S, qs, ks, vs, a, b, z, alog, dtb, ng):
    shp = (NH, CHUNK, CHUNK)
    ii = lax.broadcasted_iota(jnp.int32, shp, 1)
    jj = lax.broadcasted_iota(jnp.int32, shp, 2)
    l_incl = jnp.where(ii >= jj, 1.0, 0.0)
    ys = []
    for c in range(len(qs)):
        q = qs[c] * lax.rsqrt(jnp.sum(qs[c] * qs[c], axis=-1, keepdims=True) + L2_EPS) * (HD ** -0.5)
        k = ks[c] * lax.rsqrt(jnp.sum(ks[c] * ks[c], axis=-1, keepdims=True) + L2_EPS)
        v = vs[c]
        beta = jax.nn.sigmoid(b[c])
        g = -jnp.exp(alog) * _softplus(a[c] + dtb)
        gcb = mm(l_incl, jnp.broadcast_to(g, shp), False, False, 2)
        gcr = jnp.swapaxes(gcb, 1, 2)
        decay = jnp.exp(jnp.where(ii >= jj, gcb - gcr, NEG))
        g_last = jnp.sum(jnp.where(ii == CHUNK - 1, gcb, 0.0), axis=1, keepdims=True)
        eg = jnp.exp(gcb)
        kb = k * beta
        lkk = jnp.where(ii > jj, mm(kb, k, False, True) * decay, 0.0)
        t_inv = tri_inv(lkk)
        u = mm(t_inv, v * beta)
        w = mm(t_inv, kb * eg)
        a_qk = jnp.where(ii >= jj, mm(q, k, False, True) * decay, 0.0)
        q_dec = q * eg
        k_dec = k * jnp.exp(g_last - gcb)
        v_new = u - mm(w, S)
        o = mm(q_dec, S) + mm(a_qk, v_new)
        S = S * jnp.exp(g_last) + mm(k_dec, v_new, True, False)
        y = o * lax.rsqrt(jnp.mean(o * o, axis=-1, keepdims=True) + RMS_EPS) * ng
        ys.append(y * (z[c] * jax.nn.sigmoid(z[c])))
    return S, ys


GDN_CB = 256


def _heads(ref, rows, width, lane0=0):
    return jnp.stack([ref[rows, lane0 + h * width:lane0 + (h + 1) * width] for h in range(NH)])


def _chunk_rows(c):
    return slice(c * CHUNK, (c + 1) * CHUNK)


def _gdn_load(refs, nc):
    q_ref, k_ref, v_ref, z_ref, sm_ref = refs
    hm = lambda r: [r[:, _chunk_rows(c), :] for c in range(nc)]
    return (hm(q_ref), hm(k_ref), hm(v_ref), [_heads(z_ref, _chunk_rows(c), HD) for c in range(nc)],
            [_heads(sm_ref, _chunk_rows(c), 1) for c in range(nc)], [_heads(sm_ref, _chunk_rows(c), 1, NH) for c in range(nc)])


def gdn_fwd(qkv_hm, proj, alog, dtb, ng, name):
    T = proj.shape[0]
    cb = min(T, GDN_CB)
    nc = cb // CHUNK
    nb = T // cb

    def body(q_ref, k_ref, v_ref, z_ref, sm_ref, alog_ref, dtb_ref, ng_ref, y_ref, s_ref, S):
        @pl.when(pl.program_id(0) == 0)
        def _():
            S[...] = jnp.zeros_like(S)
        s_ref[...] = S[...]
        qs, ks, vs, zs, as_, bs = _gdn_load((q_ref, k_ref, v_ref, z_ref, sm_ref), nc)
        s_out, ys = _gdn_block(S[...], qs, ks, vs, as_, bs, zs, alog_ref[...], dtb_ref[...], ng_ref[...])
        S[...] = s_out
        for c in range(nc):
            for h in range(NH):
                y_ref[_chunk_rows(c), h * HD:(h + 1) * HD] = ys[c][h]

    hm = lambda h0: pl.BlockSpec((NH, cb, HD), lambda i: (h0, i, 0))
    par = pl.BlockSpec((NH, 1, 1), lambda i: (0, 0, 0))
    return pl.pallas_call(
        body, grid=(nb,),
        in_specs=[hm(0), hm(1), hm(2), pl.BlockSpec((cb, GW), lambda i: (i, C_GZ // GW)),
                  pl.BlockSpec((cb, 128), lambda i: (i, C_SMALL // 128)), par, par, pl.BlockSpec((1, 1, HD), lambda i: (0, 0, 0))],
        out_specs=[pl.BlockSpec((cb, GW), lambda i: (i, 0)), pl.BlockSpec((None, NH, HD, HD), lambda i: (i, 0, 0, 0))],
        out_shape=[_sds((T, GW)), _sds((nb, NH, HD, HD))],
        scratch_shapes=[pltpu.VMEM((NH, HD, HD), f32)],
        compiler_params=_cparams(("arbitrary",)), name=name)(qkv_hm, qkv_hm, qkv_hm, proj, proj, alog, dtb, ng)


def gdn_bwd(dycat, states, qkv_hm, proj, alog, dtb, ng, name):
    T = proj.shape[0]
    cb = min(T, GDN_CB)
    nc = cb // CHUNK
    nb = T // cb

    def body(dy_ref, s_ref, q_ref, k_ref, v_ref, z_ref, sm_ref, alog_ref, dtb_ref, ng_ref,
             dqkv_ref, dz_ref, dsm_ref, dalog_ref, ddtb_ref, dng_ref, dS):
        @pl.when(pl.program_id(0) == 0)
        def _():
            dS[...] = jnp.zeros_like(dS)
            dalog_ref[...] = jnp.zeros_like(dalog_ref)
            ddtb_ref[...] = jnp.zeros_like(ddtb_ref)
            dng_ref[...] = jnp.zeros_like(dng_ref)

        qs, ks, vs, zs, as_, bs = _gdn_load((q_ref, k_ref, v_ref, z_ref, sm_ref), nc)
        _, vjp = jax.vjp(_gdn_block, s_ref[...], qs, ks, vs, as_, bs, zs, alog_ref[...], dtb_ref[...], ng_ref[...])
        dys = [_heads(dy_ref, _chunk_rows(c), HD) for c in range(nc)]
        d_s, dqs, dks, dvs, das, dbs, dzs, d_alog, d_dtb, d_ng = vjp((dS[...], dys))
        dS[...] = d_s
        dalog_ref[...] += d_alog
        ddtb_ref[...] += d_dtb
        dng_ref[...] += d_ng
        for c in range(nc):
            sl = _chunk_rows(c)
            dqkv_ref[0:NH, sl, :] = dqs[c]
            dqkv_ref[NH:2 * NH, sl, :] = dks[c]
            dqkv_ref[2 * NH:3 * NH, sl, :] = dvs[c]
            for h in range(NH):
                dz_ref[sl, h * HD:(h + 1) * HD] = dzs[c][h]
            dsm_ref[sl, :] = _pack_cols([das[c][h] for h in range(NH)] + [dbs[c][h] for h in range(NH)])

    rev = lambda i: nb - 1 - i
    hm = lambda h0: pl.BlockSpec((NH, cb, HD), lambda i: (h0, rev(i), 0))
    tok = lambda w, cblk: pl.BlockSpec((cb, w), lambda i: (rev(i), cblk))
    par = pl.BlockSpec((NH, 1, 1), lambda i: (0, 0, 0))
    ngs = pl.BlockSpec((1, 1, HD), lambda i: (0, 0, 0))
    res = pl.pallas_call(
        body, grid=(nb,),
        in_specs=[tok(GW, 0), pl.BlockSpec((None, NH, HD, HD), lambda i: (rev(i), 0, 0, 0)),
                  hm(0), hm(1), hm(2), tok(GW, C_GZ // GW), tok(128, C_SMALL // 128), par, par, ngs],
        out_specs=[pl.BlockSpec((3 * NH, cb, HD), lambda i: (0, rev(i), 0)), tok(GW, 0), tok(128, 0), par, par, ngs],
        out_shape=[_sds((3 * NH, T, HD)), _sds((T, GW)), _sds((T, 128))] + [_sds((NH, 1, 1))] * 2 + [_sds((1, 1, HD))],
        scratch_shapes=[pltpu.VMEM((NH, HD, HD), f32)],
        compiler_params=_cparams(("arbitrary",)), name=name)(dycat, states, qkv_hm, qkv_hm, qkv_hm, proj, proj, alog, dtb, ng)
    return res


F_LANE = 8
SCAN_TB = 256


def fox_gate_fwd(proj, bfv, name):
    T = proj.shape[0]
    tb = min(T, SCAN_TB)

    def body(x_ref, b_ref, o_ref, carry):
        @pl.when(pl.program_id(0) == 0)
        def _():
            carry[...] = jnp.zeros_like(carry)
        logf = -_softplus(-(x_ref[...] + b_ref[...]))
        r = lax.broadcasted_iota(jnp.int32, (tb, tb), 0)
        c = lax.broadcasted_iota(jnp.int32, (tb, tb), 1)
        tri = (r >= c).astype(bf16)
        hi, lo = _split_hi_lo(logf)
        cum = _dot(tri, hi) + _dot(tri, lo) + carry[0:1, :]
        o_ref[...] = cum
        carry[0:1, :] = cum[tb - 1:tb, :]

    return pl.pallas_call(
        body, grid=(T // tb,),
        in_specs=[pl.BlockSpec((tb, 128), lambda i: (i, C_SMALL // 128)), pl.BlockSpec((1, 128), lambda i: (0, 0))],
        out_specs=pl.BlockSpec((tb, 128), lambda i: (i, 0)), out_shape=_sds((T, 128)),
        scratch_shapes=[pltpu.VMEM((8, 128), f32)],
        compiler_params=_cparams(("arbitrary",)), name=name)(proj, bfv)


def fox_gate_bwd(dcum, proj, bfv, name):
    T = proj.shape[0]
    tb = min(T, SCAN_TB)
    nb = T // tb

    def body(d_ref, x_ref, b_ref, o_ref, db_ref, carry):
        @pl.when(pl.program_id(0) == 0)
        def _():
            carry[...] = jnp.zeros_like(carry)
            db_ref[...] = jnp.zeros_like(db_ref)
        r = lax.broadcasted_iota(jnp.int32, (tb, tb), 0)
        c = lax.broadcasted_iota(jnp.int32, (tb, tb), 1)
        tri = (c >= r).astype(bf16)
        hi, lo = _split_hi_lo(d_ref[...])
        dlogf = _dot(tri, hi) + _dot(tri, lo) + carry[0:1, :]
        carry[0:1, :] = dlogf[0:1, :]
        lane = lax.broadcasted_iota(jnp.int32, (tb, 128), 1)
        keep = (lane >= F_LANE) & (lane < F_LANE + NH)
        dx = jnp.where(keep, dlogf * jax.nn.sigmoid(-(x_ref[...] + b_ref[...])), 0.0)
        o_ref[...] = dx
        db_ref[...] += jnp.sum(dx, axis=0, keepdims=True)

    rev = lambda i: nb - 1 - i
    return pl.pallas_call(
        body, grid=(nb,),
        in_specs=[pl.BlockSpec((tb, 128), lambda i: (rev(i), 0)), pl.BlockSpec((tb, 128), lambda i: (rev(i), C_SMALL // 128)),
                  pl.BlockSpec((1, 128), lambda i: (0, 0))],
        out_specs=[pl.BlockSpec((tb, 128), lambda i: (rev(i), 0)), pl.BlockSpec((1, 128), lambda i: (0, 0))],
        out_shape=[_sds((T, 128)), _sds((1, 128))],
        scratch_shapes=[pltpu.VMEM((8, 128), f32)],
        compiler_params=_cparams(("arbitrary",)), name=name)(dcum, proj, bfv)


ATT_TQ = 1024
ATT_TQ_BWD = 512
ATT_TK = 256


def _lane_col(tile, lane):
    li = lax.broadcasted_iota(jnp.int32, tile.shape, 1)
    return jnp.sum(jnp.where(li == lane, tile, 0.0), axis=1, keepdims=True)


def _pack_cols(cols):
    rows = cols[0].shape[0]
    li = lax.broadcasted_iota(jnp.int32, (rows, 128), 1)
    out = jnp.zeros((rows, 128), f32)
    for h, cv in enumerate(cols):
        out = jnp.where(li == h, cv, out)
    return out


def _head_masks():
    li = lax.broadcasted_iota(jnp.int32, (1, 128), 1)
    return [li < HD, li >= HD]


def _qkv_specs(T, tq, base_blk):
    q = pl.BlockSpec((tq, 128), lambda p, i: (i, base_blk + p))
    k = pl.BlockSpec((T, 128), lambda p, i: (0, base_blk + 2 + p))
    v = pl.BlockSpec((T, 128), lambda p, i: (0, base_blk + 4 + p))
    return q, k, v


def _stack_heads(x, masks):
    return jnp.concatenate([jnp.where(m, x, jnp.zeros_like(x)) for m in masks], axis=0)


def _side_by_side(x, tq):
    return jnp.concatenate([x[:tq], x[tq:]], axis=1)


def _stacked_mask(tq, tk, d, strict):
    r = lax.broadcasted_iota(jnp.int32, (2 * tq, tk), 0)
    r = jnp.where(r >= tq, r - tq, r)
    c = lax.broadcasted_iota(jnp.int32, (2 * tq, tk), 1) + d * tk
    return c < r if strict else c <= r


def _sub_head_rows(s, rows2, tq):
    return jnp.concatenate([s[:tq] - rows2[0:1, :], s[tq:] - rows2[1:2, :]], axis=0)


def _lane_cols2(tile):
    return jnp.concatenate([_lane_col(tile, 0), _lane_col(tile, 1)], axis=0)


def _pack_cols2(col, tq):
    return _pack_cols([col[:tq], col[tq:]])


def _dot_hilo2(x, tri):
    n = x.shape[0]
    hi, lo = _split_hi_lo(x)
    r = _dot(jnp.concatenate([hi, lo], axis=0), tri)
    return r[:n] + r[n:]


def fox_fwd(pbf, cumrow, name, rider=None):
    T = pbf.shape[0]
    tq, tk = min(T, ATT_TQ), min(T, ATT_TK)
    nq, nk, per = T // tq, T // tk, tq // tk
    scale = HD ** -0.5

    def body(q_ref, k_ref, v_ref, cr_ref, o_ref, lse_ref):
        i = pl.program_id(1)
        masks = _head_masks()
        qs2 = _stack_heads(q_ref[...] * scale, masks)

        def tile(kb, carry, d=None):
            m, l, acc = carry
            off = pl.multiple_of(kb * tk, tk)
            v2 = _stack_heads(v_ref[pl.ds(off, tk), :], masks)
            s = _sub_head_rows(_dot_nt(qs2, k_ref[pl.ds(off, tk), :]), cr_ref[kb], tq)
            if d is not None:
                s = jnp.where(_stacked_mask(tq, tk, d, False), s, NEG)
            m_new = jnp.maximum(m, jnp.max(s, axis=1, keepdims=True))
            corr = jnp.exp(m - m_new)
            p = jnp.exp(s - m_new)
            l = l * corr + jnp.sum(p, axis=1, keepdims=True)
            acc = acc * jnp.where(masks[0], corr[:tq], corr[tq:]) + _dot(_side_by_side(p.astype(bf16), tq), v2)
            return m_new, l, acc

        init = (jnp.full((2 * tq, 1), NEG, f32), jnp.zeros((2 * tq, 1), f32), jnp.zeros((tq, 128), f32))
        carry = lax.fori_loop(0, i * per, tile, init)
        for d in range(per):
            carry = tile(i * per + d, carry, d)
        m, l, acc = carry
        o_ref[...] = acc * jnp.where(masks[0], 1.0 / l[:tq], 1.0 / l[tq:])
        lse_ref[...] = _pack_cols2(m + jnp.log(l), tq)

    qs, ks, vs = _qkv_specs(T, tq, C_FOX // 128)
    body, r_in, r_args, r_out, r_shape, r_scratch = _ride(body, 4, 2, rider, (2, nq))
    res = pl.pallas_call(
        body, grid=(2, nq),
        in_specs=[qs, ks, vs, pl.BlockSpec((None, nk, 8, tk), lambda p, i: (p, 0, 0, 0))] + r_in,
        out_specs=[pl.BlockSpec((tq, 128), lambda p, i: (i, p)), pl.BlockSpec((None, tq, 128), lambda p, i: (p, i, 0))] + r_out,
        out_shape=[_sds((T, GW)), _sds((2, T, 128))] + r_shape, scratch_shapes=r_scratch,
        compiler_params=_cparams(("arbitrary", "arbitrary")), name=name)(pbf, pbf, pbf, cumrow, *r_args)
    return res[0], res[1], res[2:]


def fox_bwd(do, o, lse, pbf, cumrow, name, rider=None):
    T = pbf.shape[0]
    tq, tk = min(T, ATT_TQ_BWD), min(T, ATT_TK)
    nq, nk, per = T // tq, T // tk, tq // tk
    scale = HD ** -0.5

    def body(do_ref, o_ref, lse_ref, q_ref, k_ref, v_ref, cr_ref, dq_ref, dk_ref, dv_ref, dc_ref, dcq_ref):
        i = pl.program_id(1)

        @pl.when(i == 0)
        def _():
            dk_ref[...] = jnp.zeros_like(dk_ref)
            dv_ref[...] = jnp.zeros_like(dv_ref)
            dc_ref[...] = jnp.zeros_like(dc_ref)

        masks = _head_masks()
        dov = do_ref[...]
        qs2 = _stack_heads(q_ref[...] * scale, masks)
        do2 = _stack_heads(dov.astype(bf16), masks)
        prod = dov * o_ref[...]
        delta = jnp.concatenate([jnp.sum(jnp.where(m, prod, 0.0), axis=1, keepdims=True) for m in masks], axis=0)
        lse2 = _lane_cols2(lse_ref[...])

        def tile(kb, carry, d=None):
            dq, rsum = carry
            off = pl.multiple_of(kb * tk, tk)
            kblk = k_ref[pl.ds(off, tk), :]
            p = jnp.exp(_sub_head_rows(_dot_nt(qs2, kblk), cr_ref[kb], tq) - lse2)
            if d is not None:
                p = jnp.where(_stacked_mask(tq, tk, d, False), p, 0.0)
            dp = _dot_nt(do2, v_ref[pl.ds(off, tk), :])
            ds = p * (dp - delta)
            dsb = ds.astype(bf16)
            dq = dq + _dot(_side_by_side(dsb, tq), _stack_heads(kblk, masks))
            dk_ref[pl.ds(off, tk), :] += _dot_tn(dsb, qs2)
            dv_ref[pl.ds(off, tk), :] += _dot_tn(p.astype(bf16), do2)
            dc_ref[kb, 0:1, :] += -jnp.sum(ds[:tq], axis=0, keepdims=True)
            dc_ref[kb, 1:2, :] += -jnp.sum(ds[tq:], axis=0, keepdims=True)
            return dq, rsum + jnp.sum(ds, axis=1, keepdims=True)

        carry = lax.fori_loop(0, i * per, tile, (jnp.zeros((tq, 128), f32), jnp.zeros((2 * tq, 1), f32)))
        for d in range(per):
            carry = tile(i * per + d, carry, d)
        dq, rsum = carry
        dq_ref[...] = dq * scale
        dcq_ref[...] = _pack_cols2(rsum, tq)

    qs, ks, vs = _qkv_specs(T, tq, C_FOX // 128)
    tile_spec = pl.BlockSpec((tq, 128), lambda p, i: (i, p))
    pair = pl.BlockSpec((None, tq, 128), lambda p, i: (p, i, 0))
    rowsp = pl.BlockSpec((None, nk, 8, tk), lambda p, i: (p, 0, 0, 0))
    full = pl.BlockSpec((T, 128), lambda p, i: (0, p))
    body, r_in, r_args, r_out, r_shape, r_scratch = _ride(body, 7, 5, rider, (2, nq))
    res = pl.pallas_call(
        body, grid=(2, nq),
        in_specs=[tile_spec, tile_spec, pair, qs, ks, vs, rowsp] + r_in,
        out_specs=[tile_spec, full, full, rowsp, pair] + r_out,
        out_shape=[_sds((T, GW)), _sds((T, GW)), _sds((T, GW)), _sds((2, nk, 8, tk)), _sds((2, T, 128))] + r_shape,
        scratch_shapes=r_scratch,
        compiler_params=_cparams(("arbitrary", "arbitrary")), name=name)(do, o, lse, pbf, pbf, pbf, cumrow, *r_args)
    return res[0], res[1], res[2], res[3], res[4], res[5:]


def _tri(tq, pred):
    r = lax.broadcasted_iota(jnp.int32, (tq, tq), 0)
    c = lax.broadcasted_iota(jnp.int32, (tq, tq), 1)
    return pred(r, c).astype(bf16)


def _ride(body, n_in, n_out, rider, grid, n_scratch=0):
    if rider is None:
        return body, [], [], [], [], []
    nr = rider.n

    def wrapped(*refs):
        ins, rin = refs[:n_in], refs[n_in:n_in + nr]
        outs = refs[n_in + nr:n_in + nr + n_out]
        rout = refs[n_in + nr + n_out:n_in + 2 * nr + n_out]
        own = refs[n_in + 2 * nr + n_out:n_in + 2 * nr + n_out + n_scratch]
        sems = refs[n_in + 2 * nr + n_out + n_scratch:]
        ids = [pl.program_id(a) for a in range(len(grid))]
        first = functools.reduce(jnp.logical_and, [i == 0 for i in ids])
        last = functools.reduce(jnp.logical_and, [i == g - 1 for i, g in zip(ids, grid)])

        @pl.when(first)
        def _():
            rider.start(rin, rout, sems)

        body(*ins, *outs, *own)

        @pl.when(last)
        def _():
            rider.finish(rin, rout, sems)

    anyspec = pl.BlockSpec(memory_space=pl.ANY)
    return wrapped, [anyspec] * nr, list(rider.arrays), [anyspec] * nr, list(rider.out_shape), list(rider.scratch)


def sb_fwd(pbf, name, rider=None):
    T = pbf.shape[0]
    tq, tk = min(T, ATT_TQ), min(T, ATT_TK)
    nq, per = T // tq, tq // tk
    scale = HD ** -0.5

    def body(q_ref, k_ref, v_ref, o_ref, tot_ref):
        i = pl.program_id(1)
        masks = _head_masks()
        qs2 = _stack_heads(q_ref[...] * scale, masks)
        after = _tri(tk, lambda r, c: r > c)

        def tile(kb, carry, d=None):
            rs, acc = carry
            off = pl.multiple_of(kb * tk, tk)
            z = _dot_nt(qs2, k_ref[pl.ds(off, tk), :])
            lk = _neg_softplus(z)
            if d is not None:
                lk = jnp.where(_stacked_mask(tq, tk, d, True), lk, 0.0)
            w = jnp.exp(z + lk + (_dot_hilo2(lk, after) + rs))
            if d is not None:
                w = jnp.where(_stacked_mask(tq, tk, d, True), w, 0.0)
            acc = acc + _dot(_side_by_side(w.astype(bf16), tq), _stack_heads(v_ref[pl.ds(off, tk), :], masks))
            return rs + jnp.sum(lk, axis=1, keepdims=True), acc

        carry = (jnp.zeros((2 * tq, 1), f32), jnp.zeros((tq, 128), f32))
        for d in reversed(range(per)):
            carry = tile(i * per + d, carry, d)
        rs, acc = lax.fori_loop(0, i * per, lambda n, c: tile(i * per - 1 - n, c), carry)
        o_ref[...] = acc
        tot_ref[...] = _pack_cols2(rs, tq)

    qs, ks, vs = _qkv_specs(T, tq, C_SB // 128)
    body, r_in, r_args, r_out, r_shape, r_scratch = _ride(body, 3, 2, rider, (2, nq))
    res = pl.pallas_call(
        body, grid=(2, nq), in_specs=[qs, ks, vs] + r_in,
        out_specs=[pl.BlockSpec((tq, 128), lambda p, i: (i, p)), pl.BlockSpec((None, tq, 128), lambda p, i: (p, i, 0))] + r_out,
        out_shape=[_sds((T, GW)), _sds((2, T, 128))] + r_shape, scratch_shapes=r_scratch,
        compiler_params=_cparams(("arbitrary", "arbitrary")), name=name)(pbf, pbf, pbf, *r_args)
    return res[0], res[1], res[2:]


def sb_bwd(do, tot, pbf, name, rider=None):
    T = pbf.shape[0]
    tq, tk = min(T, ATT_TQ_BWD), min(T, ATT_TK)
    nq, per = T // tq, tq // tk
    scale = HD ** -0.5

    def body(do_ref, tot_ref, q_ref, k_ref, v_ref, dq_ref, dk_ref, dv_ref):
        i = pl.program_id(1)

        @pl.when(i == 0)
        def _():
            dk_ref[...] = jnp.zeros_like(dk_ref)
            dv_ref[...] = jnp.zeros_like(dv_ref)

        masks = _head_masks()
        qs2 = _stack_heads(q_ref[...] * scale, masks)
        do2 = _stack_heads(do_ref[...].astype(bf16), masks)
        tot2 = _lane_cols2(tot_ref[...])
        upto = _tri(tk, lambda r, c: r <= c)
        before = _tri(tk, lambda r, c: r < c)

        def tile(kb, carry, d=None):
            pre, cg, dq = carry
            off = pl.multiple_of(kb * tk, tk)
            kblk = k_ref[pl.ds(off, tk), :]
            z = _dot_nt(qs2, kblk)
            lk = _neg_softplus(z)
            keep = jnp.exp(lk)
            if d is not None:
                lk = jnp.where(_stacked_mask(tq, tk, d, True), lk, 0.0)
            w = jnp.exp(z + lk + (tot2 - (pre + _dot_hilo2(lk, upto))))
            if d is not None:
                w = jnp.where(_stacked_mask(tq, tk, d, True), w, 0.0)
            gmat = w * _dot_nt(do2, v_ref[pl.ds(off, tk), :])
            cmat = cg + _dot(gmat.astype(bf16), before)
            dz = keep * (gmat + cmat) - cmat
            if d is not None:
                dz = jnp.where(_stacked_mask(tq, tk, d, True), dz, 0.0)
            dzb = dz.astype(bf16)
            dq = dq + _dot(_side_by_side(dzb, tq), _stack_heads(kblk, masks))
            dk_ref[pl.ds(off, tk), :] += _dot_tn(dzb, qs2)
            dv_ref[pl.ds(off, tk), :] += _dot_tn(w.astype(bf16), do2)
            return pre + jnp.sum(lk, axis=1, keepdims=True), cg + jnp.sum(gmat, axis=1, keepdims=True), dq

        zc = jnp.zeros((2 * tq, 1), f32)
        carry = lax.fori_loop(0, i * per, tile, (zc, zc, jnp.zeros((tq, 128), f32)))
        for d in range(per):
            carry = tile(i * per + d, carry, d)
        dq_ref[...] = carry[2] * scale

    qs, ks, vs = _qkv_specs(T, tq, C_SB // 128)
    tile_spec = pl.BlockSpec((tq, 128), lambda p, i: (i, p))
    pair = pl.BlockSpec((None, tq, 128), lambda p, i: (p, i, 0))
    full = pl.BlockSpec((T, 128), lambda p, i: (0, p))
    body, r_in, r_args, r_out, r_shape, r_scratch = _ride(body, 5, 3, rider, (2, nq))
    res = pl.pallas_call(
        body, grid=(2, nq), in_specs=[tile_spec, pair, qs, ks, vs] + r_in,
        out_specs=[tile_spec, full, full] + r_out, out_shape=[_sds((T, GW))] * 3 + r_shape, scratch_shapes=r_scratch,
        compiler_params=_cparams(("arbitrary", "arbitrary")), name=name)(do, tot, pbf, pbf, pbf, *r_args)
    return res[0], res[1], res[2], res[3:]


MEM_HD = D // 4


def mem_fwd(q, kv, name):
    T = q.shape[0]
    M = kv.shape[0]
    tm = min(T, 512)
    scale = MEM_HD ** -0.5

    def body(q_ref, kv_ref, o_ref):
        for h in range(4):
            sl = slice(h * MEM_HD, (h + 1) * MEM_HD)
            kh = kv_ref[:, sl].astype(bf16)
            vh = kv_ref[:, D + h * MEM_HD:D + (h + 1) * MEM_HD].astype(bf16)
            s = _dot_nt(q_ref[:, sl], kh) * scale
            e = jnp.exp(s - jnp.max(s, axis=1, keepdims=True))
            p = e / jnp.sum(e, axis=1, keepdims=True)
            o_ref[:, sl] = _dot(p.astype(bf16), vh).astype(bf16)

    return pl.pallas_call(
        body, grid=(T // tm,),
        in_specs=[pl.BlockSpec((tm, D), lambda i: (i, 0)), pl.BlockSpec((M, 2 * D), lambda i: (0, 0))],
        out_specs=pl.BlockSpec((tm, D), lambda i: (i, 0)), out_shape=_sds((T, D), bf16),
        compiler_params=_cparams(("parallel",)), name=name)(q, kv)


def mem_bwd(do, q, kv, name):
    T = q.shape[0]
    M = kv.shape[0]
    tm = min(T, 512)
    scale = MEM_HD ** -0.5

    def body(do_ref, q_ref, kv_ref, dq_ref, dkv_ref):
        @pl.when(pl.program_id(0) == 0)
        def _():
            dkv_ref[...] = jnp.zeros_like(dkv_ref)
        for h in range(4):
            sl = slice(h * MEM_HD, (h + 1) * MEM_HD)
            vsl = slice(D + h * MEM_HD, D + (h + 1) * MEM_HD)
            qh = q_ref[:, sl]
            kh = kv_ref[:, sl].astype(bf16)
            vh = kv_ref[:, vsl].astype(bf16)
            doh = do_ref[:, sl].astype(bf16)
            s = _dot_nt(qh, kh) * scale
            e = jnp.exp(s - jnp.max(s, axis=1, keepdims=True))
            p = e / jnp.sum(e, axis=1, keepdims=True)
            dp = _dot_nt(doh, vh)
            ds = p * (dp - jnp.sum(dp * p, axis=1, keepdims=True))
            dsb = ds.astype(bf16)
            dq_ref[:, sl] = _dot(dsb, kh) * scale
            dkv_ref[:, sl] += _dot_tn(dsb, qh) * scale
            dkv_ref[:, vsl] += _dot_tn(p.astype(bf16), doh)

    row = pl.BlockSpec((tm, D), lambda i: (i, 0))
    whole = pl.BlockSpec((M, 2 * D), lambda i: (0, 0))
    return pl.pallas_call(
        body, grid=(T // tm,), in_specs=[row, row, whole], out_specs=[row, whole],
        out_shape=[_sds((T, D)), _sds((M, 2 * D))],
        compiler_params=_cparams(("arbitrary",)), name=name)(do, q, kv)


def _chip_peers():
    x, y, c = lax.axis_index("x"), lax.axis_index("y"), lax.axis_index("c")
    me = 2 * x + y
    peers = [((1 - x, y, c), 2 * (1 - x) + y), ((x, 1 - y, c), 2 * x + (1 - y)), ((1 - x, 1 - y, c), 2 * (1 - x) + (1 - y))]
    return me, peers


class Rider:
    def __init__(self, arrays, out_shape, scratch, start, finish):
        self.arrays, self.out_shape, self.scratch, self.start, self.finish = arrays, out_shape, scratch, start, finish
        self.n = len(arrays)

    def split(self, refs):
        return refs[:self.n], refs[self.n:2 * self.n], refs[2 * self.n:]


def run_rider(rider, name):
    def body(*refs):
        parts = rider.split(refs)
        rider.start(*parts)
        rider.finish(*parts)

    anyspec = pl.BlockSpec(memory_space=pl.ANY)
    return pl.pallas_call(
        body, in_specs=[anyspec] * rider.n, out_specs=[anyspec] * rider.n, out_shape=rider.out_shape,
        scratch_shapes=rider.scratch, name=name)(*rider.arrays)


def gather_rider(arrs, split):
    n = len(arrs)

    def copies(ins, outs, sems):
        send_sems, recv_sems, pass_send, pass_recv, loc_sems = sems
        x, y, c = lax.axis_index("x"), lax.axis_index("y"), lax.axis_index("c")
        me, peers = _chip_peers()

        def mine(ref, i):
            if not split[i]:
                return ref
            r = arrs[i].shape[1] // 2
            return ref.at[:, pl.ds(c * r, r), :]

        local, fetch, passed = [], [], []
        for i in range(n):
            local.append(pltpu.make_async_copy(ins[i], outs[i].at[me], loc_sems.at[i]))
            for k, (dev, pj) in enumerate(peers):
                fetch.append(pltpu.make_async_remote_copy(src_ref=mine(ins[i], i), dst_ref=mine(outs[i].at[me], i),
                                                          send_sem=send_sems.at[i, k], recv_sem=recv_sems.at[i, k],
                                                          device_id=dev, device_id_type=MESH))
                rows = mine(outs[i].at[pj], i)
                passed.append(pltpu.make_async_remote_copy(src_ref=rows, dst_ref=rows, send_sem=pass_send.at[i, k],
                                                           recv_sem=pass_recv.at[i, k], device_id=(x, y, 1 - c),
                                                           device_id_type=MESH) if split[i] else None)
        return local, fetch, passed

    def start(ins, outs, sems):
        local, fetch, _ = copies(ins, outs, sems)
        for cp in local + fetch:
            cp.start()

    def finish(ins, outs, sems):
        local, fetch, passed = copies(ins, outs, sems)
        for cp, fw in zip(fetch, passed):
            cp.wait_recv()
            if fw is not None:
                fw.start()
        for cp in fetch:
            cp.wait_send()
        for cp in [fw for fw in passed if fw is not None] + local:
            cp.wait()

    sem = pltpu.SemaphoreType.DMA((n, 3))
    return Rider(list(arrs), [_sds((4,) + a.shape, a.dtype) for a in arrs],
                 [sem, sem, sem, sem, pltpu.SemaphoreType.DMA((n,))], start, finish)


def exchange_rider(stacks):
    n = len(stacks)

    def copies(ins, outs, sems):
        send_sems, recv_sems, loc_sems = sems
        me, peers = _chip_peers()
        out = []
        for i in range(n):
            out.append(pltpu.make_async_copy(ins[i].at[me], outs[i].at[me], loc_sems.at[i]))
            for k, (dev, pj) in enumerate(peers):
                out.append(pltpu.make_async_remote_copy(src_ref=ins[i].at[pj], dst_ref=outs[i].at[me], send_sem=send_sems.at[i, k],
                                                        recv_sem=recv_sems.at[i, k], device_id=dev, device_id_type=MESH))
        return out

    def start(ins, outs, sems):
        for cp in copies(ins, outs, sems):
            cp.start()

    def finish(ins, outs, sems):
        for cp in copies(ins, outs, sems):
            cp.wait()

    return Rider(list(stacks), [_sds(a.shape, a.dtype) for a in stacks],
                 [pltpu.SemaphoreType.DMA((n, 3)), pltpu.SemaphoreType.DMA((n, 3)), pltpu.SemaphoreType.DMA((n,))], start, finish)


def swap_sibling(arrs, name):
    n = len(arrs)

    def body(*refs):
        ins, outs = refs[:n], refs[n:2 * n]
        send_sems, recv_sems = refs[2 * n:]
        x, y, c = lax.axis_index("x"), lax.axis_index("y"), lax.axis_index("c")
        started = []
        for i in range(n):
            cp = pltpu.make_async_remote_copy(src_ref=ins[i], dst_ref=outs[i], send_sem=send_sems.at[i],
                                              recv_sem=recv_sems.at[i], device_id=(x, y, 1 - c), device_id_type=MESH)
            cp.start()
            started.append(cp)
        for cp in started:
            cp.wait()

    anyspec = pl.BlockSpec(memory_space=pl.ANY)
    return pl.pallas_call(
        body, in_specs=[anyspec] * n, out_specs=[anyspec] * n,
        out_shape=[_sds(a.shape, a.dtype) for a in arrs],
        scratch_shapes=[pltpu.SemaphoreType.DMA((n,)), pltpu.SemaphoreType.DMA((n,))],
        name=name)(*arrs)


def gather_all(a, name):
    def body(a_ref, o_ref, send_sems, recv_sems, loc_sem):
        x, y, c = lax.axis_index("x"), lax.axis_index("y"), lax.axis_index("c")
        me = 4 * x + 2 * y + c
        loc = pltpu.make_async_copy(a_ref, o_ref.at[me], loc_sem)
        loc.start()
        started = [loc]
        for k in range(1, 8):
            dev = (x ^ (k >> 2), y ^ ((k >> 1) & 1), c ^ (k & 1))
            cp = pltpu.make_async_remote_copy(src_ref=a_ref, dst_ref=o_ref.at[me], send_sem=send_sems.at[k - 1],
                                              recv_sem=recv_sems.at[k - 1], device_id=dev, device_id_type=MESH)
            cp.start()
            started.append(cp)
        for cp in started:
            cp.wait()

    anyspec = pl.BlockSpec(memory_space=pl.ANY)
    return pl.pallas_call(
        body, in_specs=[anyspec], out_specs=anyspec, out_shape=_sds((8,) + a.shape, a.dtype),
        scratch_shapes=[pltpu.SemaphoreType.DMA((7,)), pltpu.SemaphoreType.DMA((7,)), pltpu.SemaphoreType.DMA(())],
        name=name)(a)


def sum_slots(stack, name):
    n, R, C = stack.shape
    tr = R if R <= 512 else _pick(R, (512, 352, 256))

    def body(s_ref, o_ref):
        acc = s_ref[0].astype(f32)
        for j in range(1, n):
            acc = acc + s_ref[j].astype(f32)
        o_ref[...] = acc

    return pl.pallas_call(
        body, grid=(R // tr,), in_specs=[pl.BlockSpec((n, tr, C), lambda i: (0, i, 0))],
        out_specs=pl.BlockSpec((tr, C), lambda i: (i, 0)), out_shape=_sds((R, C)),
        compiler_params=_cparams(("parallel",)), name=name)(stack)


def adamw(w, g1, g2, m, v, name):
    R, C = w.shape
    tr = R if R <= 512 else _pick(R, (512, 352, 256))
    c1 = 1.0 - ADAM_B1 ** ADAM_STEP
    c2 = 1.0 - ADAM_B2 ** ADAM_STEP

    def body(*refs):
        if g2 is None:
            w_ref, g1_ref, m_ref, v_ref, g_out, d_out, m_out, v_out = refs
            g = g1_ref[...]
        else:
            w_ref, g1_ref, g2_ref, m_ref, v_ref, g_out, d_out, m_out, v_out = refs
            g = g1_ref[...] + g2_ref[...]
        mn = ADAM_B1 * m_ref[...] + (1.0 - ADAM_B1) * g
        vn = ADAM_B2 * v_ref[...] + (1.0 - ADAM_B2) * (g * g)
        g_out[...] = g
        m_out[...] = mn
        v_out[...] = vn
        d_out[...] = -ADAM_LR * ((mn / c1) / (jnp.sqrt(vn / c2) + ADAM_EPS) + ADAM_WD * w_ref[...])

    blk = pl.BlockSpec((tr, C), lambda i: (i, 0))
    args = [w, g1] + ([] if g2 is None else [g2]) + [m, v]
    return pl.pallas_call(
        body, grid=(R // tr,), in_specs=[blk] * len(args), out_specs=[blk] * 4, out_shape=[_sds((R, C))] * 4,
        compiler_params=_cparams(("parallel",)), name=name)(*args)


IN_OFF = (0, 768, 1024, 1028, 1032, 1800, 1804, 2316, 3084)


def regroup_w_in(w):
    seg = lambda i: w[:, IN_OFF[i]:IN_OFF[i + 1]]
    pad = jnp.zeros((w.shape[0], PW - C_SMALL - 12), w.dtype)
    return jnp.concatenate([seg(0), seg(1), seg(4), seg(6), seg(7), seg(2), seg(3), seg(5), pad], axis=1)


def ungroup_w_in(g):
    s = C_SMALL
    return jnp.concatenate([g[:, 0:1024], g[:, s:s + 8], g[:, 1024:1792], g[:, s + 8:s + 12], g[:, 1792:3072]], axis=1)


def col_shards(w):
    c = w.shape[-1] // 4
    return jnp.moveaxis(w.reshape(w.shape[:-1] + (4, c)), -2, 0)


def row_shards(w):
    L, r4, c = w.shape
    return w.reshape(L, 4, r4 // 4, c).transpose(1, 0, 2, 3)


def join_cols(g):
    return jnp.moveaxis(g, 0, -2).reshape(g.shape[1:-1] + (4 * g.shape[-1],))


def join_rows(g):
    _, L, r, c = g.shape
    return g.transpose(1, 0, 2, 3).reshape(L, 4 * r, c)


COL_SHARDED = ("ffn1_w_gate", "ffn1_w_up", "w_in", "gdn_conv_w", "conf_dw_w", "mem_w_kv", "ffn2_w_gate", "ffn2_w_up")
CONV_WEIGHTS = ("gdn_conv_w", "conf_dw_w")
ROW_SHARDED = ("ffn1_w_down", "w_out", "mem_w_q", "mem_w_o", "ffn2_w_down")
REPLICATED = ("ln_ffn1_g", "ln_ffn1_b", "gdn_a_log", "gdn_dt_bias", "gdn_norm_g", "fox_b_f", "conf_dw_b", "conf_norm_g",
              "conf_norm_b", "ln_mix_g", "ln_mix_b", "ln_mem_g", "ln_mem_b", "ln_ffn2_g", "ln_ffn2_b")
WEIGHTS = ("ffn1_w_gate", "ffn1_w_up", "ffn1_w_down", "ln_ffn1_g", "ln_ffn1_b", "w_in", "gdn_conv_w", "gdn_a_log",
           "gdn_dt_bias", "gdn_norm_g", "fox_b_f", "conf_dw_w", "conf_dw_b", "conf_norm_g", "conf_norm_b", "w_out",
           "ln_mix_g", "ln_mix_b", "mem_w_q", "mem_w_kv", "mem_w_o", "ln_mem_g", "ln_mem_b", "ffn2_w_gate",
           "ffn2_w_up", "ffn2_w_down", "ln_ffn2_g", "ln_ffn2_b")


def pack_small(d):
    flat = jnp.concatenate([d[n].reshape(-1) for n in REPLICATED])
    rows = -(-flat.shape[0] // 1024) * 8
    return jnp.pad(flat, (0, rows * 128 - flat.shape[0])).reshape(rows, 128)


def unpack_small(p, like):
    flat = p.reshape(-1)
    out, o = {}, 0
    for n in REPLICATED:
        sz = like[n].size
        out[n] = flat[o:o + sz].reshape(like[n].shape)
        o += sz
    return out


def _vec(v):
    return v.reshape(1, -1)


def _pad_rows(w, rows):
    return jnp.pad(w, ((0, rows - w.shape[0]), (0, 0)))


def _small_lane_vec(v4, lane0):
    return jnp.pad(v4.reshape(1, -1), ((0, 0), (lane0, 128 - lane0 - v4.shape[0])))


def layer_fwd(x0, mem, W, li, rider=None, staged=None):
    T = x0.shape[0]
    tk = min(T, ATT_TK)
    n = lambda s: f"l{li}_{s}"
    R = {"x0": x0}
    (ffn1_rider, complete1), (fox_rider, complete2) = staged if staged else ((None, None), (None, None))
    R["z1"], x1, got = ffn_fwd(x0, W["ffn1_w_gate"], W["ffn1_w_up"], W["ffn1_w_down"], _vec(W["ln_ffn1_g"]), _vec(W["ln_ffn1_b"]),
                               n("ffn1_fwd"), ffn1_rider)
    if ffn1_rider is not None:
        W = complete1(W, got)
    R["x1"] = x1
    proj, pbf = mm_nn(x1, W["w_in_r"], n("proj"), also_bf16=True)
    R["proj"], R["pbf"] = proj, pbf

    w8 = _pad_rows(W["gdn_conv_w"], 8)
    qkv_hm = gdn_conv_fwd(proj, w8, n("gdn_conv_fwd"))
    alog = W["gdn_a_log"].reshape(NH, 1, 1)
    dtb = W["gdn_dt_bias"].reshape(NH, 1, 1)
    ng = W["gdn_norm_g"].reshape(1, 1, HD)
    ya, states = gdn_fwd(qkv_hm, proj, alog, dtb, ng, n("gdn_fwd"))
    R.update(qkv_hm=qkv_hm, states=states)

    bfv = _small_lane_vec(W["fox_b_f"], F_LANE)
    cum = fox_gate_fwd(proj, bfv, n("fox_gate_fwd"))
    cum4 = cum[:, F_LANE:F_LANE + NH]
    cumrow = jnp.pad(cum4.T.reshape(2, 2, T // tk, tk).transpose(0, 2, 1, 3), ((0, 0), (0, 0), (0, 6), (0, 0)))
    yb, lse, got = fox_fwd(pbf, cumrow, n("fox_fwd"), fox_rider)
    if fox_rider is not None:
        W = complete2(W, got)
    R.update(cumrow=cumrow, yb=yb, lse=lse)

    w32 = _pad_rows(W["conf_dw_w"], 32)
    yc, cc = conf_fwd(proj, w32, _vec(W["conf_dw_b"]), _vec(W["conf_norm_g"]), _vec(W["conf_norm_b"]), n("conf_fwd"))
    R["cc"] = cc

    yd, tot, rider_out = sb_fwd(pbf, n("sb_fwd"), rider)
    R["tot"] = tot

    ycat = jnp.concatenate([ya, yb, yc, yd], axis=1).astype(bf16)
    R["ycat"] = ycat
    R["z2"], x2 = lin_res_ln(ycat, W["w_out"], x1, _vec(W["ln_mix_g"]), _vec(W["ln_mix_b"]), n("mix_out"))
    R["x2"] = x2

    qm = mm_nn(x2, W["mem_w_q"], n("mem_q"), out_dtype=bf16)
    kv = mm_nn(mem, W["mem_w_kv"], n("mem_kv"))
    om = mem_fwd(qm, kv, n("mem_fwd"))
    R.update(qm=qm, kv=kv, om=om)
    R["z3"], x3 = lin_res_ln(om, W["mem_w_o"], x2, _vec(W["ln_mem_g"]), _vec(W["ln_mem_b"]), n("mem_out"))
    R["x3"] = x3
    R["z4"], x4, _ = ffn_fwd(x3, W["ffn2_w_gate"], W["ffn2_w_up"], W["ffn2_w_down"], _vec(W["ln_ffn2_g"]), _vec(W["ln_ffn2_b"]), n("ffn2_fwd"))
    return x4, R, rider_out, W


EARLY_GRADS = ("ffn2_w_gate", "ffn2_w_up", "ffn2_w_down", "mem_w_q", "mem_w_kv", "mem_w_o", "w_out")
LATE_GRADS = ("w_in", "gdn_conv_w", "conf_dw_w")
LAST_GRADS = ("ffn1_w_gate", "ffn1_w_up", "ffn1_w_down")


def grad_slots(G, names):
    out = []
    for k in names:
        s = col_shards(G[k][None]) if k in COL_SHARDED else row_shards(G[k][None])
        out.append(s if k in CONV_WEIGHTS else s.astype(bf16))
    return out


def layer_bwd(dx4, mem, W, R, li, rider=None, exchange=True, last=False):
    T = dx4.shape[0]
    n = lambda s: f"l{li}_{s}"
    G = {}

    def ffn_back(dy, z, x, pre, tag, ride=None, stagger=False):
        dz, dg, db = ln_bwd(dy, z, _vec(W[f"ln_{pre}_g"]), n(f"{tag}_ln_bwd"))
        (dx, a, dh, du, dzh), got = ffn_bwd(dz, x, W[f"{pre}_w_gate"], W[f"{pre}_w_up"], W[f"{pre}_w_down"], n(f"{tag}_bwd"), ride)
        send = lambda k: exchange_rider(grad_slots(G, (k,))) if stagger else None
        G[f"{pre}_w_gate"] = mm_tn(x, dh, n(f"{tag}_dwg"))
        up = mm_tn(x, du, n(f"{tag}_dwu"), send(f"{pre}_w_gate"))
        G[f"{pre}_w_up"], got_gate = up if stagger else (up, None)
        down = mm_tn(a, dzh, n(f"{tag}_dwd"), send(f"{pre}_w_up"))
        G[f"{pre}_w_down"], got_up = down if stagger else (down, None)
        G[f"ln_{pre}_g"], G[f"ln_{pre}_b"] = dg.reshape(-1), db.reshape(-1)
        return dx, got, got_gate, got_up

    dx3 = ffn_back(dx4, R["z4"], R["x3"], "ffn2", "ffn2")[0]

    dz3, dg, db = ln_bwd(dx3, R["z3"], _vec(W["ln_mem_g"]), n("mem_ln_bwd"))
    G["ln_mem_g"], G["ln_mem_b"] = dg.reshape(-1), db.reshape(-1)
    dom = mm_nt(dz3, W["mem_w_o"], n("mem_dom"))
    G["mem_w_o"] = mm_tn(R["om"], dz3, n("mem_dwo"))
    dqm, dkv = mem_bwd(dom, R["qm"], R["kv"], n("mem_bwd"))
    G["mem_w_q"] = mm_tn(R["x2"], dqm, n("mem_dwq"))
    G["mem_w_kv"] = mm_tn(mem, dkv, n("mem_dwkv"))
    dx2 = mm_nt(dqm, W["mem_w_q"], n("mem_dx"), add=dz3, add_scale=ALPHA)

    dz2, dg, db = ln_bwd(dx2, R["z2"], _vec(W["ln_mix_g"]), n("mix_ln_bwd"))
    G["ln_mix_g"], G["ln_mix_b"] = dg.reshape(-1), db.reshape(-1)
    dycat = mm_nt(dz2, W["w_out"], n("mix_dycat"))
    G["w_out"] = mm_tn(R["ycat"], dz2, n("mix_dwout"))
    dyb, dyc, dyd = (dycat[:, i * GW:(i + 1) * GW] for i in range(1, 4))
    proj, pbf = R["proj"], R["pbf"]

    alog = W["gdn_a_log"].reshape(NH, 1, 1)
    dtb = W["gdn_dt_bias"].reshape(NH, 1, 1)
    ng = W["gdn_norm_g"].reshape(1, 1, HD)
    dqkv_hm, dgz, dsmall_ab, dalog, ddtb, dng = gdn_bwd(dycat, R["states"], R["qkv_hm"], proj, alog, dtb, ng, n("gdn_bwd"))
    G["gdn_a_log"], G["gdn_dt_bias"], G["gdn_norm_g"] = dalog.reshape(-1), ddtb.reshape(-1), dng.reshape(-1)
    w8 = _pad_rows(W["gdn_conv_w"], 8)
    dgqkv, dw8 = gdn_conv_bwd(dqkv_hm, proj, w8, n("gdn_conv_bwd"))
    G["gdn_conv_w"] = dw8[:GDN_K]

    dfq, dfk, dfv, dcumrow, dcumq, got_early = fox_bwd(dyb, R["yb"], R["lse"], pbf, R["cumrow"], n("fox_bwd"),
                                                      exchange_rider(grad_slots(G, EARLY_GRADS)) if exchange else None)
    dcum4 = dcumrow[:, :, 0:2, :].transpose(0, 2, 1, 3).reshape(4, T).T
    dcum4 = dcum4 + dcumq[:, :, 0:2].transpose(1, 0, 2).reshape(T, 4)
    dcum = jnp.pad(dcum4, ((0, 0), (F_LANE, 128 - F_LANE - NH)))
    bfv = _small_lane_vec(W["fox_b_f"], F_LANE)
    dsmall_f, dbf = fox_gate_bwd(dcum, proj, bfv, n("fox_gate_bwd"))
    G["fox_b_f"] = dbf[0, F_LANE:F_LANE + NH]

    w32 = _pad_rows(W["conf_dw_w"], 32)
    dglu, dw32, dcb, dcg, dcbeta = conf_bwd(dyc, R["cc"], proj, w32, _vec(W["conf_norm_g"]), _vec(W["conf_norm_b"]), n("conf_bwd"))
    G["conf_dw_w"], G["conf_dw_b"] = dw32[:CONF_K], dcb.reshape(-1)
    G["conf_norm_g"], G["conf_norm_b"] = dcg.reshape(-1), dcbeta.reshape(-1)

    dsq, dsk, dsv, got_carried = sb_bwd(dyd, R["tot"], pbf, n("sb_bwd"), rider)

    dsmall = dsmall_ab + dsmall_f
    dproj = [dgqkv, dgz, dfq, dfk, dfv, dglu, dsq, dsk, dsv, dsmall]
    G["w_in"] = ungroup_w_in(mm_tn(R["x1"], dproj, n("proj_dw")))
    dx1 = mm_nt(dproj, W["w_in_r"], n("proj_dx"), add=dz2, add_scale=ALPHA)

    dx0, got_late, got_gate, got_up = ffn_back(dx1, R["z1"], R["x0"], "ffn1", "ffn1",
                                                exchange_rider(grad_slots(G, LATE_GRADS)) if exchange else None, stagger=last)
    return dx0, G, {"carried": got_carried, "early": got_early, "late": got_late, "gate": got_gate, "up": got_up}


def _step(P, M, V, x, mem, loss_target):
    xs, mems, tgt = x[0], mem[0], loss_target[0]

    big = COL_SHARDED + ROW_SHARDED

    def weight_gather(li, names):
        return gather_rider([P[k][li:li + 1] if k in CONV_WEIGHTS else P[k][li:li + 1].astype(bf16) for k in names],
                            [k not in CONV_WEIGHTS for k in names])

    def layer_weights(li, names, gathered, W=None):
        W = dict(W) if W else {k: P[k][li] for k in REPLICATED}
        W.update({k: (join_cols(g) if k in COL_SHARDED else join_rows(g))[0] for k, g in zip(names, gathered)})
        if "w_in" in names:
            W["w_in_r"] = regroup_w_in(W["w_in"])
        return W

    W = layer_weights(0, LAST_GRADS, run_rider(weight_gather(0, LAST_GRADS), "gather_weights_first"))
    staged = tuple((weight_gather(0, names), functools.partial(lambda names, W0, got: layer_weights(0, names, got, W0), names))
                   for names in (LATE_GRADS, EARLY_GRADS))
    layers, saved = [], []
    h = xs
    for li in range(DEPTH):
        nxt = weight_gather(li + 1, big) if li + 1 < DEPTH else None
        h, R, gathered, W = layer_fwd(h, mems, W, li, nxt, staged if li == 0 else None)
        layers.append(W)
        saved.append(R)
        if li + 1 < DEPTH:
            W = layer_weights(li + 1, big, gathered)
    loss_row, dy = loss_and_grad(h, tgt, "loss")
    loss = lax.psum(loss_row[0, 0], ("x", "y", "c"))

    grads, received_at = [None] * DEPTH, [dict() for _ in range(DEPTH)]
    rider = None
    for li in reversed(range(DEPTH)):
        dy, G, got = layer_bwd(dy, mems, layers[li], saved[li], li, rider, last=(li == 0))
        if rider is not None:
            received_at[li + 1].update(zip(LAST_GRADS, got["carried"]))
        received_at[li].update(zip(EARLY_GRADS, got["early"]))
        received_at[li].update(zip(LATE_GRADS, got["late"]))
        grads[li] = G
        rider = exchange_rider(grad_slots(G, LAST_GRADS)) if li > 0 else None
    received_at[0].update(ffn1_w_gate=got["gate"][0], ffn1_w_up=got["up"][0])
    received_at[0]["ffn1_w_down"] = run_rider(exchange_rider(grad_slots(grads[0], ("ffn1_w_down",))), "exchange_grads_last")[0]
    grad_x = dy[None]

    stacked = {k: jnp.stack([grads[li][k] for li in range(DEPTH)]) for k in WEIGHTS}
    received = [jnp.concatenate([received_at[li][k] for li in range(DEPTH)], axis=1) for k in big]
    partial_sums = []
    for k, r in zip(big, received):
        shp = r.shape
        partial_sums.append(sum_slots(r.reshape(4, -1, shp[-1]), f"sum_{k}"))
    from_sibling = swap_sibling(partial_sums, "swap_partials")

    out_g, out_d, out_m, out_v = {}, {}, {}, {}
    for k, mine, theirs in zip(big, partial_sums, from_sibling):
        shp = P[k].shape
        flat = lambda t: t.reshape(-1, shp[-1])
        g, d, mn, vn = adamw(flat(P[k]), mine, theirs, flat(M[k]), flat(V[k]), f"adamw_{k}")
        out_g[k], out_d[k], out_m[k], out_v[k] = (t.reshape(shp) for t in (g, d, mn, vn))

    gsmall = sum_slots(gather_all(pack_small(stacked), "gather_small"), "sum_small")
    g, d, mn, vn = adamw(pack_small(P), gsmall, None, pack_small(M), pack_small(V), "adamw_small")
    for dst, packed in ((out_g, g), (out_d, d), (out_m, mn), (out_v, vn)):
        dst.update(unpack_small(packed, P))

    return (loss, grad_x, *[out_g[k] for k in WEIGHTS], *[out_d[k] for k in WEIGHTS],
            *[out_m[k] for k in WEIGHTS], *[out_v[k] for k in WEIGHTS])


def kernel(x, mem, ffn1_w_gate, ffn1_w_up, ffn1_w_down, ln_ffn1_g, ln_ffn1_b, w_in, gdn_conv_w, gdn_a_log, gdn_dt_bias, gdn_norm_g, fox_b_f, conf_dw_w, conf_dw_b, conf_norm_g, conf_norm_b, w_out, ln_mix_g, ln_mix_b, mem_w_q, mem_w_kv, mem_w_o, ln_mem_g, ln_mem_b, ffn2_w_gate, ffn2_w_up, ffn2_w_down, ln_ffn2_g, ln_ffn2_b, loss_target, m_ffn1_w_gate, m_ffn1_w_up, m_ffn1_w_down, m_ln_ffn1_g, m_ln_ffn1_b, m_w_in, m_gdn_conv_w, m_gdn_a_log, m_gdn_dt_bias, m_gdn_norm_g, m_fox_b_f, m_conf_dw_w, m_conf_dw_b, m_conf_norm_g, m_conf_norm_b, m_w_out, m_ln_mix_g, m_ln_mix_b, m_mem_w_q, m_mem_w_kv, m_mem_w_o, m_ln_mem_g, m_ln_mem_b, m_ffn2_w_gate, m_ffn2_w_up, m_ffn2_w_down, m_ln_ffn2_g, m_ln_ffn2_b, v_ffn1_w_gate, v_ffn1_w_up, v_ffn1_w_down, v_ln_ffn1_g, v_ln_ffn1_b, v_w_in, v_gdn_conv_w, v_gdn_a_log, v_gdn_dt_bias, v_gdn_norm_g, v_fox_b_f, v_conf_dw_w, v_conf_dw_b, v_conf_norm_g, v_conf_norm_b, v_w_out, v_ln_mix_g, v_ln_mix_b, v_mem_w_q, v_mem_w_kv, v_mem_w_o, v_ln_mem_g, v_ln_mem_b, v_ffn2_w_gate, v_ffn2_w_up, v_ffn2_w_down, v_ln_ffn2_g, v_ln_ffn2_b):
    a = locals()
    P = {k: a[k] for k in WEIGHTS}
    M = {k: a["m_" + k] for k in WEIGHTS}
    V = {k: a["v_" + k] for k in WEIGHTS}
    return _step(P, M, V, x, mem, loss_target)
```

```python
import functools

import jax
import jax.numpy as jnp
from jax import lax
from jax.experimental import pallas as pl
from jax.experimental.pallas import tpu as pltpu

f32 = jnp.float32
bf16 = jnp.bfloat16

D = 1024
F = 2816
GW = 256
HD = 64
NH = 4
CHUNK = 64
CONF_K = 31
GDN_K = 4
DEPTH = 2
ALPHA = float((2 * DEPTH) ** 0.25)
LN_EPS = 1e-5
RMS_EPS = 1e-6
L2_EPS = 1e-6
NEG = -1e30
PW = 3200
C_GQKV, C_GZ, C_FOX, C_CONF, C_SB, C_SMALL = 0, 768, 1024, 1792, 2304, 3072
ADAM_LR, ADAM_B1, ADAM_B2, ADAM_EPS, ADAM_WD, ADAM_STEP = 0.001, 0.9, 0.999, 1e-08, 0.01, 10
VMEM_LIMIT = 56 * 1024 * 1024
MESH = pl.DeviceIdType.MESH


def _cparams(sem):
    return pltpu.CompilerParams(dimension_semantics=sem, vmem_limit_bytes=VMEM_LIMIT)


def _pick(n, cands):
    for c in cands:
        if n % c == 0:
            return c
    return n


def _sds(shape, dtype=f32):
    return jax.ShapeDtypeStruct(shape, dtype)


def _layer_norm(z, g, b):
    mu = jnp.mean(z, axis=-1, keepdims=True)
    zc = z - mu
    var = jnp.mean(zc * zc, axis=-1, keepdims=True)
    return zc * lax.rsqrt(var + LN_EPS) * g + b


def _softplus(x):
    return jnp.maximum(x, 0.0) + jnp.log(1.0 + jnp.exp(-jnp.abs(x)))


def _neg_softplus(z):
    nz = -z
    return jnp.minimum(nz, 0.0) - jnp.log(1.0 + jnp.exp(jnp.minimum(z, nz)))


def _dsilu(x):
    s = jax.nn.sigmoid(x)
    return s * (1.0 + x * (1.0 - s))


def _split_hi_lo(x):
    hi = x.astype(bf16)
    lo = (x - hi.astype(f32)).astype(bf16)
    return hi, lo


def _dot(a, b):
    return jnp.dot(a, b, preferred_element_type=f32)


def _dot_nt(a, b):
    return lax.dot_general(a, b, (((1,), (1,)), ((), ())), preferred_element_type=f32)


def _dot_tn(a, b):
    return lax.dot_general(a, b, (((0,), (0,)), ((), ())), preferred_element_type=f32)


def mm_nn(a, w, name, out_dtype=f32, also_bf16=False):
    T, K = a.shape
    N = w.shape[1]
    tm = min(T, 512)
    tn = N if N <= 1024 else _pick(N, (640, 512))

    def body(a_ref, w_ref, *o_refs):
        r = _dot(a_ref[...].astype(bf16), w_ref[...].astype(bf16))
        o_refs[0][...] = r.astype(o_refs[0].dtype)
        if also_bf16:
            o_refs[1][...] = r.astype(bf16)

    out_shape = [_sds((T, N), out_dtype)]
    out_specs = [pl.BlockSpec((tm, tn), lambda i, j: (i, j))]
    if also_bf16:
        out_shape.append(_sds((T, N), bf16))
        out_specs.append(pl.BlockSpec((tm, tn), lambda i, j: (i, j)))
    res = pl.pallas_call(
        body, grid=(T // tm, N // tn),
        in_specs=[pl.BlockSpec((tm, K), lambda i, j: (i, 0)), pl.BlockSpec((K, tn), lambda i, j: (0, j))],
        out_specs=out_specs, out_shape=out_shape,
        compiler_params=_cparams(("parallel", "arbitrary")), name=name)(a, w)
    return res if also_bf16 else res[0]


def _pieces(g):
    gs = list(g) if isinstance(g, (list, tuple)) else [g]
    offs = [sum(p.shape[1] for p in gs[:i]) for i in range(len(gs))]
    return gs, offs, offs[-1] + gs[-1].shape[1]


def mm_nt(g, w, name, add=None, add_scale=1.0):
    gs, offs, N = _pieces(g)
    T = gs[0].shape[0]
    K = w.shape[0]
    tm = min(T, 512)
    ng = len(gs)

    def body(*refs):
        w_ref, o_ref = refs[ng], refs[-1]
        r = None
        for g_ref, off in zip(refs[:ng], offs):
            part = _dot_nt(g_ref[...].astype(bf16), w_ref[:, off:off + g_ref.shape[1]].astype(bf16))
            r = part if r is None else r + part
        if add is not None:
            r = r + add_scale * refs[ng + 1][...]
        o_ref[...] = r

    in_specs = [pl.BlockSpec((tm, p.shape[1]), lambda i: (i, 0)) for p in gs] + [pl.BlockSpec((K, N), lambda i: (0, 0))]
    args = gs + [w]
    if add is not None:
        in_specs.append(pl.BlockSpec((tm, K), lambda i: (i, 0)))
        args.append(add)
    return pl.pallas_call(
        body, grid=(T // tm,), in_specs=in_specs,
        out_specs=pl.BlockSpec((tm, K), lambda i: (i, 0)), out_shape=_sds((T, K)),
        compiler_params=_cparams(("parallel",)), name=name)(*args)


MM_TN_BUFFERS = 3


def mm_tn(a, g, name, rider=None):
    gs, offs, N = _pieces(g)
    T, K = a.shape
    tk = K if K * N * 4 <= 14 * 1024 * 1024 else _pick(K, (512, 1408))
    row_bytes = tk * a.dtype.itemsize + sum(p.shape[1] * p.dtype.itemsize for p in gs)
    tt = min(T, 1024 if 2 * tk * N * 4 + MM_TN_BUFFERS * 1024 * row_bytes <= 44 * 1024 * 1024 else 512)

    def body(a_ref, *refs):
        o_ref = refs[-1]

        @pl.when(pl.program_id(1) == 0)
        def _():
            o_ref[...] = jnp.zeros_like(o_ref)
        if len(gs) == 1:
            o_ref[...] += _dot_tn(a_ref[...].astype(bf16), refs[0][...].astype(bf16))
        else:
            at = a_ref[...].astype(bf16).T
            for g_ref, off in zip(refs[:-1], offs):
                o_ref[:, off:off + g_ref.shape[1]] += _dot(at, g_ref[...].astype(bf16))

    nk, nt, nb = K // tk, T // tt, MM_TN_BUFFERS
    manual = len(gs) == 1
    own = []
    if manual:
        g0 = gs[0]

        def body(a_hbm, g_hbm, o_ref, abuf, gbuf, sems):
            t = pl.program_id(1)
            s = pl.program_id(0) * nt + t

            def fetch(step, slot):
                rows = pl.ds(pl.multiple_of((step % nt) * tt, tt), tt)
                cols = pl.ds(pl.multiple_of((step // nt) * tk, tk), tk)
                return (pltpu.make_async_copy(a_hbm.at[rows, cols], abuf.at[slot], sems.at[0, slot]),
                        pltpu.make_async_copy(g_hbm.at[rows, :], gbuf.at[slot], sems.at[1, slot]))

            @pl.when(s == 0)
            def _():
                for p in range(min(nb - 1, nk * nt)):
                    for cp in fetch(p, p):
                        cp.start()

            @pl.when(s + nb - 1 < nk * nt)
            def _():
                for cp in fetch(s + nb - 1, (s + nb - 1) % nb):
                    cp.start()

            slot = s % nb
            for cp in fetch(s, slot):
                cp.wait()

            @pl.when(t == 0)
            def _():
                o_ref[...] = jnp.zeros_like(o_ref)
            o_ref[...] += _dot_tn(abuf[slot].astype(bf16), gbuf[slot].astype(bf16))

        own = [pltpu.VMEM((nb, tt, tk), a.dtype), pltpu.VMEM((nb, tt, N), g0.dtype), pltpu.SemaphoreType.DMA((2, nb))]
        anyspec = pl.BlockSpec(memory_space=pl.ANY)
        in_specs = [anyspec, anyspec]
    else:
        in_specs = [pl.BlockSpec((tt, tk), lambda j, t: (t, j))] + [pl.BlockSpec((tt, p.shape[1]), lambda j, t: (t, 0)) for p in gs]
    body, r_in, r_args, r_out, r_shape, r_scratch = _ride(body, 1 + len(gs), 1, rider, (nk, nt), len(own))
    res = pl.pallas_call(
        body, grid=(nk, nt), in_specs=in_specs + r_in,
        out_specs=[pl.BlockSpec((tk, N), lambda j, t: (j, 0))] + r_out, out_shape=[_sds((K, N))] + r_shape,
        scratch_shapes=own + r_scratch,
        compiler_params=_cparams(("arbitrary", "arbitrary")), name=name)(a, *gs, *r_args)
    return res[0] if rider is None else (res[0], res[1:])


FFN_TF = 1408


def ffn_fwd(x, wg, wu, wd, g, b, name, rider=None):
    T = x.shape[0]
    tm = min(T, 512)
    nf = F // FFN_TF

    def body(x_ref, wg_ref, wu_ref, wd_ref, g_ref, b_ref, z_ref, y_ref, acc):
        j = pl.program_id(1)

        @pl.when(j == 0)
        def _():
            acc[...] = jnp.zeros_like(acc)

        xb = x_ref[...].astype(bf16)
        h = _dot(xb, wg_ref[...])
        u = _dot(xb, wu_ref[...])
        a = (h * jax.nn.sigmoid(h) * u).astype(bf16)
        acc[...] += _dot(a, wd_ref[...])

        @pl.when(j == nf - 1)
        def _():
            z = ALPHA * x_ref[...] + 0.5 * acc[...]
            z_ref[...] = z
            y_ref[...] = _layer_norm(z, g_ref[...], b_ref[...])

    row = pl.BlockSpec((tm, D), lambda i, j: (i, 0))
    vec = pl.BlockSpec((1, D), lambda i, j: (0, 0))
    body, r_in, r_args, r_out, r_shape, r_scratch = _ride(body, 6, 2, rider, (T // tm, nf), 1)
    res = pl.pallas_call(
        body, grid=(T // tm, nf),
        in_specs=[row, pl.BlockSpec((D, FFN_TF), lambda i, j: (0, j)), pl.BlockSpec((D, FFN_TF), lambda i, j: (0, j)),
                  pl.BlockSpec((FFN_TF, D), lambda i, j: (j, 0)), vec, vec] + r_in,
        out_specs=[row, row] + r_out, out_shape=[_sds((T, D)), _sds((T, D))] + r_shape,
        scratch_shapes=[pltpu.VMEM((tm, D), f32)] + r_scratch,
        compiler_params=_cparams(("arbitrary", "arbitrary")), name=name)(x, wg, wu, wd, g, b, *r_args)
    return res[0], res[1], res[2:]


def _ln_bwd_rows(dy, zv, g):
    mu = jnp.mean(zv, axis=-1, keepdims=True)
    zc = zv - mu
    rstd = lax.rsqrt(jnp.mean(zc * zc, axis=-1, keepdims=True) + LN_EPS)
    xh = zc * rstd
    dxh = dy * g
    m1 = jnp.mean(dxh, axis=-1, keepdims=True)
    m2 = jnp.mean(dxh * xh, axis=-1, keepdims=True)
    return rstd * (dxh - m1 - xh * m2), jnp.sum(dy * xh, axis=0, keepdims=True), jnp.sum(dy, axis=0, keepdims=True)


def ffn_bwd(dz, x, wg, wu, wd, name, rider=None):
    T = x.shape[0]
    tm = min(T, 512)
    nf = F // FFN_TF

    def body(dz_ref, x_ref, wg_ref, wu_ref, wd_ref, dx_ref, a_ref, dh_ref, du_ref, dzh_ref, acc):
        j = pl.program_id(1)

        @pl.when(j == 0)
        def _():
            acc[...] = jnp.zeros_like(acc)

        dzh = (0.5 * dz_ref[...]).astype(bf16)
        xb = x_ref[...].astype(bf16)
        h = _dot(xb, wg_ref[...])
        u = _dot(xb, wu_ref[...])
        s = jax.nn.sigmoid(h)
        hs = h * s
        da = _dot_nt(dzh, wd_ref[...])
        du = (da * hs).astype(bf16)
        dh = (da * u * (s + hs * (1.0 - s))).astype(bf16)
        a_ref[...] = (hs * u).astype(bf16)
        dh_ref[...] = dh
        du_ref[...] = du
        acc[...] += _dot_nt(dh, wg_ref[...]) + _dot_nt(du, wu_ref[...])

        @pl.when(j == nf - 1)
        def _():
            dx_ref[...] = ALPHA * dz_ref[...] + acc[...]
            dzh_ref[...] = dzh

    row = pl.BlockSpec((tm, D), lambda i, j: (i, 0))
    wide = pl.BlockSpec((tm, FFN_TF), lambda i, j: (i, j))
    body, r_in, r_args, r_out, r_shape, r_scratch = _ride(body, 5, 5, rider, (T // tm, nf), 1)
    res = pl.pallas_call(
        body, grid=(T // tm, nf),
        in_specs=[row, row, pl.BlockSpec((D, FFN_TF), lambda i, j: (0, j)), pl.BlockSpec((D, FFN_TF), lambda i, j: (0, j)),
                  pl.BlockSpec((FFN_TF, D), lambda i, j: (j, 0))] + r_in,
        out_specs=[row, wide, wide, wide, row] + r_out,
        out_shape=[_sds((T, D)), _sds((T, F), bf16), _sds((T, F), bf16), _sds((T, F), bf16), _sds((T, D), bf16)] + r_shape,
        scratch_shapes=[pltpu.VMEM((tm, D), f32)] + r_scratch,
        compiler_params=_cparams(("arbitrary", "arbitrary")), name=name)(dz, x, wg, wu, wd, *r_args)
    return res[:5], res[5:]


def lin_res_ln(a, w, res, g, b, name):
    T, K = a.shape
    tm = min(T, 512)

    def body(a_ref, w_ref, res_ref, g_ref, b_ref, z_ref, y_ref):
        z = ALPHA * res_ref[...] + _dot(a_ref[...].astype(bf16), w_ref[...])
        z_ref[...] = z
        y_ref[...] = _layer_norm(z, g_ref[...], b_ref[...])

    row = pl.BlockSpec((tm, D), lambda i: (i, 0))
    vec = pl.BlockSpec((1, D), lambda i: (0, 0))
    return pl.pallas_call(
        body, grid=(T // tm,),
        in_specs=[pl.BlockSpec((tm, K), lambda i: (i, 0)), pl.BlockSpec((K, D), lambda i: (0, 0)), row, vec, vec],
        out_specs=[row, row], out_shape=[_sds((T, D)), _sds((T, D))],
        compiler_params=_cparams(("parallel",)), name=name)(a, w, res, g, b)


def ln_bwd(dy, z, g, name):
    T = z.shape[0]
    tm = min(T, 512)

    def body(dy_ref, z_ref, g_ref, dz_ref, dg_ref, db_ref):
        @pl.when(pl.program_id(0) == 0)
        def _():
            dg_ref[...] = jnp.zeros_like(dg_ref)
            db_ref[...] = jnp.zeros_like(db_ref)

        dz, dg, db = _ln_bwd_rows(dy_ref[...], z_ref[...], g_ref[...])
        dz_ref[...] = dz
        dg_ref[...] += dg
        db_ref[...] += db

    row = pl.BlockSpec((tm, D), lambda i: (i, 0))
    vec = pl.BlockSpec((1, D), lambda i: (0, 0))
    return pl.pallas_call(
        body, grid=(T // tm,), in_specs=[row, row, vec], out_specs=[row, vec, vec],
        out_shape=[_sds((T, D)), _sds((1, D)), _sds((1, D))],
        compiler_params=_cparams(("arbitrary",)), name=name)(dy, z, g)


def loss_and_grad(y, target, name):
    T = y.shape[0]
    tm = min(T, 512)

    def body(y_ref, t_ref, l_ref, dy_ref):
        @pl.when(pl.program_id(0) == 0)
        def _():
            l_ref[...] = jnp.zeros_like(l_ref)
        d = y_ref[...] - t_ref[...]
        dy_ref[...] = d * (1.0 / D)
        l_ref[...] += (0.5 / D) * jnp.sum(jnp.sum(d * d, axis=1, keepdims=True), axis=0, keepdims=True)

    row = pl.BlockSpec((tm, D), lambda i: (i, 0))
    return pl.pallas_call(
        body, grid=(T // tm,), in_specs=[row, row],
        out_specs=[pl.BlockSpec((1, 128), lambda i: (0, 0)), row],
        out_shape=[_sds((1, 128)), _sds((T, D))],
        compiler_params=_cparams(("arbitrary",)), name=name)(y, target)


def _shifted(ext, s):
    return ext if s == 0 else pltpu.roll(ext, s, 0)


def _halo_maps(tm, P, nblk):
    per = tm // P
    prev = lambda i, c: (jnp.maximum(i * per - 1, 0), c)
    nxt = lambda i, c: (jnp.minimum((i + 1) * per, nblk * per - 1), c)
    return prev, nxt


GDN_P = 8
CONF_P = 32


def gdn_conv_fwd(proj, w8, name):
    T = proj.shape[0]
    tm = min(T, 512)
    nblk = T // tm
    C = 3 * GW
    prev, _ = _halo_maps(tm, GDN_P, nblk)

    def body(xc_ref, xp_ref, w_ref, o_ref):
        i = pl.program_id(0)
        xp = jnp.where(i > 0, xp_ref[...], 0.0)
        ext = jnp.concatenate([xp, xc_ref[...]], axis=0)
        acc = jnp.zeros((tm, C), f32)
        for k in range(GDN_K):
            acc = acc + w_ref[k:k + 1, :] * _shifted(ext, GDN_K - 1 - k)[GDN_P:, :]
        y = acc * jax.nn.sigmoid(acc)
        for h in range(3 * NH):
            o_ref[h] = y[:, h * HD:(h + 1) * HD]

    return pl.pallas_call(
        body, grid=(nblk,),
        in_specs=[pl.BlockSpec((tm, C), lambda i: (i, 0)), pl.BlockSpec((GDN_P, C), lambda i: prev(i, 0)),
                  pl.BlockSpec((8, C), lambda i: (0, 0))],
        out_specs=pl.BlockSpec((3 * NH, tm, HD), lambda i: (0, i, 0)), out_shape=_sds((3 * NH, T, HD)),
        compiler_params=_cparams(("parallel",)), name=name)(proj, proj, w8)


def gdn_conv_bwd(dy, proj, w8, name):
    T = proj.shape[0]
    tm = min(T, 512)
    nblk = T // tm
    C = 3 * GW
    P = GDN_P
    prev, nxt = _halo_maps(tm, P, nblk)

    def body(dyc_ref, dyn_ref, xc_ref, xp_ref, xn_ref, w_ref, dx_ref, dw_ref):
        i = pl.program_id(0)

        @pl.when(i == 0)
        def _():
            dw_ref[...] = jnp.zeros_like(dw_ref)

        xp = jnp.where(i > 0, xp_ref[...], 0.0)
        last = i == nblk - 1
        xn = jnp.where(last, 0.0, xn_ref[...])
        dyn = jnp.where(last, 0.0, jnp.concatenate([dyn_ref[h] for h in range(3 * NH)], axis=1))
        ext = jnp.concatenate([xp, xc_ref[...], xn], axis=0)
        sh = [_shifted(ext, GDN_K - 1 - k)[P:, :] for k in range(GDN_K)]
        s = jnp.zeros((tm + P, C), f32)
        for k in range(GDN_K):
            s = s + w_ref[k:k + 1, :] * sh[k]
        dyc = jnp.concatenate([dyc_ref[h] for h in range(3 * NH)], axis=1)
        ds = jnp.concatenate([dyc, dyn], axis=0) * _dsilu(s)
        dx = jnp.zeros((tm, C), f32)
        for k in range(GDN_K):
            d = GDN_K - 1 - k
            moved = ds if d == 0 else pltpu.roll(ds, tm + P - d, 0)
            dx = dx + w_ref[k:k + 1, :] * moved[:tm, :]
            dw_ref[k:k + 1, :] += jnp.sum(ds[:tm, :] * sh[k][:tm, :], axis=0, keepdims=True)
        dx_ref[...] = dx

    col = lambda i: (i, 0)
    return pl.pallas_call(
        body, grid=(nblk,),
        in_specs=[pl.BlockSpec((3 * NH, tm, HD), lambda i: (0, i, 0)), pl.BlockSpec((3 * NH, P, HD), lambda i: (0, nxt(i, 0)[0], 0)),
                  pl.BlockSpec((tm, C), col), pl.BlockSpec((P, C), lambda i: prev(i, 0)),
                  pl.BlockSpec((P, C), lambda i: nxt(i, 0)), pl.BlockSpec((8, C), lambda i: (0, 0))],
        out_specs=[pl.BlockSpec((tm, C), col), pl.BlockSpec((8, C), lambda i: (0, 0))],
        out_shape=[_sds((T, C)), _sds((8, C))],
        compiler_params=_cparams(("arbitrary",)), name=name)(dy, dy, proj, proj, proj, w8)


def _group_ones():
    r = lax.broadcasted_iota(jnp.int32, (GW, GW), 0) // HD
    c = lax.broadcasted_iota(jnp.int32, (GW, GW), 1) // HD
    return (r == c).astype(bf16)


def _group_mean(x, ones):
    hi, lo = _split_hi_lo(x)
    return (_dot(hi, ones) + _dot(lo, ones)) * (1.0 / HD)


def _conf_norm(c, g, b, ones):
    mu = _group_mean(c, ones)
    cc = c - mu
    rstd = lax.rsqrt(_group_mean(cc * cc, ones) + LN_EPS)
    hn = cc * rstd
    return hn, rstd, hn * g + b


CONF_VAL_BLK = C_CONF // GW
CONF_GATE_BLK = C_CONF // GW + 1


def conf_fwd(proj, w32, bias, ng, nb, name):
    T = proj.shape[0]
    tm = min(T, 512)
    nblk = T // tm
    P = CONF_P
    prev, _ = _halo_maps(tm, P, nblk)

    def body(vc_ref, gc_ref, vp_ref, gp_ref, w_ref, bias_ref, ng_ref, nb_ref, y_ref, c_ref):
        i = pl.program_id(0)
        pc = vc_ref[...] * jax.nn.sigmoid(gc_ref[...])
        pp = jnp.where(i > 0, vp_ref[...] * jax.nn.sigmoid(gp_ref[...]), 0.0)
        ext = jnp.concatenate([pp, pc], axis=0)
        acc = jnp.zeros((tm, GW), f32)
        for k in range(CONF_K):
            acc = acc + w_ref[k:k + 1, :] * _shifted(ext, CONF_K - 1 - k)[P:, :]
        c = acc + bias_ref[...]
        c_ref[...] = c
        _, _, yn = _conf_norm(c, ng_ref[...], nb_ref[...], _group_ones())
        y_ref[...] = yn * jax.nn.sigmoid(yn)

    vec = pl.BlockSpec((1, GW), lambda i: (0, 0))
    return pl.pallas_call(
        body, grid=(nblk,),
        in_specs=[pl.BlockSpec((tm, GW), lambda i: (i, CONF_VAL_BLK)), pl.BlockSpec((tm, GW), lambda i: (i, CONF_GATE_BLK)),
                  pl.BlockSpec((P, GW), lambda i: prev(i, CONF_VAL_BLK)), pl.BlockSpec((P, GW), lambda i: prev(i, CONF_GATE_BLK)),
                  pl.BlockSpec((32, GW), lambda i: (0, 0)), vec, vec, vec],
        out_specs=[pl.BlockSpec((tm, GW), lambda i: (i, 0))] * 2, out_shape=[_sds((T, GW))] * 2,
        compiler_params=_cparams(("parallel",)), name=name)(proj, proj, proj, proj, w32, bias, ng, nb)


def conf_bwd(dy, c, proj, w32, ng, nb, name):
    T = proj.shape[0]
    tm = min(T, 512)
    nblk = T // tm
    P = CONF_P
    prev, nxt = _halo_maps(tm, P, nblk)

    def body(dyc_ref, dyn_ref, cc_ref, cn_ref, vc_ref, gc_ref, vp_ref, gp_ref, w_ref, ng_ref, nb_ref,
             dglu_ref, dw_ref, dbias_ref, dng_ref, dnb_ref):
        i = pl.program_id(0)

        @pl.when(i == 0)
        def _():
            dw_ref[...] = jnp.zeros_like(dw_ref)
            dbias_ref[...] = jnp.zeros_like(dbias_ref)
            dng_ref[...] = jnp.zeros_like(dng_ref)
            dnb_ref[...] = jnp.zeros_like(dnb_ref)

        ones = _group_ones()
        g = ng_ref[...]

        def dc_of(dyv, cv):
            hn, rstd, yn = _conf_norm(cv, g, nb_ref[...], ones)
            dyn_ = dyv * _dsilu(yn)
            dhn = dyn_ * g
            dc = rstd * (dhn - _group_mean(dhn, ones) - hn * _group_mean(dhn * hn, ones))
            return dc, dyn_, hn

        dc_c, dyn_c, hn_c = dc_of(dyc_ref[...], cc_ref[...])
        dc_n, _, _ = dc_of(dyn_ref[...], cn_ref[...])
        dc_n = jnp.where(i == nblk - 1, 0.0, dc_n)
        dng_ref[...] += jnp.sum(dyn_c * hn_c, axis=0, keepdims=True)
        dnb_ref[...] += jnp.sum(dyn_c, axis=0, keepdims=True)
        dbias_ref[...] += jnp.sum(dc_c, axis=0, keepdims=True)

        sig_c = jax.nn.sigmoid(gc_ref[...])
        val_c = vc_ref[...]
        pc = val_c * sig_c
        pp = jnp.where(i > 0, vp_ref[...] * jax.nn.sigmoid(gp_ref[...]), 0.0)
        ext = jnp.concatenate([pp, pc], axis=0)
        dext = jnp.concatenate([dc_c, dc_n], axis=0)
        dp = jnp.zeros((tm, GW), f32)
        for k in range(CONF_K):
            d = CONF_K - 1 - k
            moved = dext if d == 0 else pltpu.roll(dext, tm + P - d, 0)
            dp = dp + w_ref[k:k + 1, :] * moved[:tm, :]
            dw_ref[k:k + 1, :] += jnp.sum(dc_c * _shifted(ext, d)[P:, :], axis=0, keepdims=True)
        dglu_ref[:, 0:GW] = dp * sig_c
        dglu_ref[:, GW:2 * GW] = dp * val_c * sig_c * (1.0 - sig_c)

    vec = pl.BlockSpec((1, GW), lambda i: (0, 0))
    blk = pl.BlockSpec((tm, GW), lambda i: (i, 0))
    return pl.pallas_call(
        body, grid=(nblk,),
        in_specs=[blk, pl.BlockSpec((P, GW), lambda i: nxt(i, 0)), blk, pl.BlockSpec((P, GW), lambda i: nxt(i, 0)),
                  pl.BlockSpec((tm, GW), lambda i: (i, CONF_VAL_BLK)), pl.BlockSpec((tm, GW), lambda i: (i, CONF_GATE_BLK)),
                  pl.BlockSpec((P, GW), lambda i: prev(i, CONF_VAL_BLK)), pl.BlockSpec((P, GW), lambda i: prev(i, CONF_GATE_BLK)),
                  pl.BlockSpec((32, GW), lambda i: (0, 0)), vec, vec],
        out_specs=[pl.BlockSpec((tm, 2 * GW), lambda i: (i, 0)), pl.BlockSpec((32, GW), lambda i: (0, 0)), vec, vec, vec],
        out_shape=[_sds((T, 2 * GW)), _sds((32, GW)), _sds((1, GW)), _sds((1, GW)), _sds((1, GW))],
        compiler_params=_cparams(("arbitrary",)), name=name)(dy, dy, c, c, proj, proj, proj, proj, w32, ng, nb)


def _mm_raw(a, b, ta, tb):
    ca = a.ndim - 2 if ta else a.ndim - 1
    cb = b.ndim - 1 if tb else b.ndim - 2
    batch = ((0,), (0,)) if a.ndim == 3 else ((), ())
    return lax.dot_general(a, b, (((ca,), (cb,)), batch), preferred_element_type=f32)


def _mm_prec(a, b, ta, tb, prec):
    if prec == 1:
        return _mm_raw(a.astype(bf16), b.astype(bf16), ta, tb)
    bh, bl = _split_hi_lo(b)
    if prec == 2:
        ab = a.astype(bf16)
        return _mm_raw(ab, bh, ta, tb) + _mm_raw(ab, bl, ta, tb)
    ah, al = _split_hi_lo(a)
    return _mm_raw(ah, bh, ta, tb) + (_mm_raw(ah, bl, ta, tb) + _mm_raw(al, bh, ta, tb))


@functools.partial(jax.custom_vjp, nondiff_argnums=(2, 3, 4))
def mm(a, b, ta=False, tb=False, prec=1):
    return _mm_prec(a, b, ta, tb, prec)


def _mm_fwd(a, b, ta, tb, prec):
    return _mm_prec(a, b, ta, tb, prec), (a, b)


def _mm_bwd(ta, tb, prec, res, ct):
    a, b = res
    da = _mm_prec(b, ct, tb, True, 1) if ta else _mm_prec(ct, b, False, not tb, 1)
    db = _mm_prec(ct, a, True, ta, 1) if tb else _mm_prec(a, ct, not ta, False, 2 if prec == 2 else 1)
    return da, db


mm.defvjp(_mm_fwd, _mm_bwd)


def _tri_inv_raw(l):
    n = -l
    rr = lax.broadcasted_iota(jnp.int32, l.shape, 1)
    cc = lax.broadcasted_iota(jnp.int32, l.shape, 2)
    p = jnp.where(rr == cc, 1.0, 0.0) + n
    for _ in range(5):
        n = _mm_prec(n, n, False, False, 1)
        p = p + _mm_prec(p, n, False, False, 1)
    return p


@jax.custom_vjp
def tri_inv(l):
    return _tri_inv_raw(l)


def _tri_inv_fwd(l):
    t = _tri_inv_raw(l)
    return t, t


def _tri_inv_bwd(t, ct):
    return (-_mm_prec(_mm_prec(t, ct, True, False, 1), t, False, True, 1),)


tri_inv.defvjp(_tri_inv_fwd, _tri_inv_bwd)


def _gdn_block(S, qs, ks, vs, a, b, z, alog, dtb, ng):
    shp = (NH, CHUNK, CHUNK)
    ii = lax.broadcasted_iota(jnp.int32, shp, 1)
    jj = lax.broadcasted_iota(jnp.int32, shp, 2)
    l_incl = jnp.where(ii >= jj, 1.0, 0.0)
    ys = []
    for c in range(len(qs)):
        q = qs[c] * lax.rsqrt(jnp.sum(qs[c] * qs[c], axis=-1, keepdims=True) + L2_EPS) * (HD ** -0.5)
        k = ks[c] * lax.rsqrt(jnp.sum(ks[c] * ks[c], axis=-1, keepdims=True) + L2_EPS)
        v = vs[c]
        beta = jax.nn.sigmoid(b[c])
        g = -jnp.exp(alog) * _softplus(a[c] + dtb)
        gcb = mm(l_incl, jnp.broadcast_to(g, shp), False, False, 2)
        gcr = jnp.swapaxes(gcb, 1, 2)
        decay = jnp.exp(jnp.where(ii >= jj, gcb - gcr, NEG))
        g_last = jnp.sum(jnp.where(ii == CHUNK - 1, gcb, 0.0), axis=1, keepdims=True)
        eg = jnp.exp(gcb)
        kb = k * beta
        lkk = jnp.where(ii > jj, mm(kb, k, False, True) * decay, 0.0)
        t_inv = tri_inv(lkk)
        u = mm(t_inv, v * beta)
        w = mm(t_inv, kb * eg)
        a_qk = jnp.where(ii >= jj, mm(q, k, False, True) * decay, 0.0)
        q_dec = q * eg
        k_dec = k * jnp.exp(g_last - gcb)
        v_new = u - mm(w, S)
        o = mm(q_dec, S) + mm(a_qk, v_new)
        S = S * jnp.exp(g_last) + mm(k_dec, v_new, True, False)
        y = o * lax.rsqrt(jnp.mean(o * o, axis=-1, keepdims=True) + RMS_EPS) * ng
        ys.append(y * (z[c] * jax.nn.sigmoid(z[c])))
    return S, ys


GDN_CB = 256


def _heads(ref, rows, width, lane0=0):
    return jnp.stack([ref[rows, lane0 + h * width:lane0 + (h + 1) * width] for h in range(NH)])


def _chunk_rows(c):
    return slice(c * CHUNK, (c + 1) * CHUNK)


def _gdn_load(refs, nc):
    q_ref, k_ref, v_ref, z_ref, sm_ref = refs
    hm = lambda r: [r[:, _chunk_rows(c), :] for c in range(nc)]
    return (hm(q_ref), hm(k_ref), hm(v_ref), [_heads(z_ref, _chunk_rows(c), HD) for c in range(nc)],
            [_heads(sm_ref, _chunk_rows(c), 1) for c in range(nc)], [_heads(sm_ref, _chunk_rows(c), 1, NH) for c in range(nc)])


def gdn_fwd(qkv_hm, proj, alog, dtb, ng, name):
    T = proj.shape[0]
    cb = min(T, GDN_CB)
    nc = cb // CHUNK
    nb = T // cb

    def body(q_ref, k_ref, v_ref, z_ref, sm_ref, alog_ref, dtb_ref, ng_ref, y_ref, s_ref, S):
        @pl.when(pl.program_id(0) == 0)
        def _():
            S[...] = jnp.zeros_like(S)
        s_ref[...] = S[...]
        qs, ks, vs, zs, as_, bs = _gdn_load((q_ref, k_ref, v_ref, z_ref, sm_ref), nc)
        s_out, ys = _gdn_block(S[...], qs, ks, vs, as_, bs, zs, alog_ref[...], dtb_ref[...], ng_ref[...])
        S[...] = s_out
        for c in range(nc):
            for h in range(NH):
                y_ref[_chunk_rows(c), h * HD:(h + 1) * HD] = ys[c][h]

    hm = lambda h0: pl.BlockSpec((NH, cb, HD), lambda i: (h0, i, 0))
    par = pl.BlockSpec((NH, 1, 1), lambda i: (0, 0, 0))
    return pl.pallas_call(
        body, grid=(nb,),
        in_specs=[hm(0), hm(1), hm(2), pl.BlockSpec((cb, GW), lambda i: (i, C_GZ // GW)),
                  pl.BlockSpec((cb, 128), lambda i: (i, C_SMALL // 128)), par, par, pl.BlockSpec((1, 1, HD), lambda i: (0, 0, 0))],
        out_specs=[pl.BlockSpec((cb, GW), lambda i: (i, 0)), pl.BlockSpec((None, NH, HD, HD), lambda i: (i, 0, 0, 0))],
        out_shape=[_sds((T, GW)), _sds((nb, NH, HD, HD))],
        scratch_shapes=[pltpu.VMEM((NH, HD, HD), f32)],
        compiler_params=_cparams(("arbitrary",)), name=name)(qkv_hm, qkv_hm, qkv_hm, proj, proj, alog, dtb, ng)


def gdn_bwd(dycat, states, qkv_hm, proj, alog, dtb, ng, name):
    T = proj.shape[0]
    cb = min(T, GDN_CB)
    nc = cb // CHUNK
    nb = T // cb

    def body(dy_ref, s_ref, q_ref, k_ref, v_ref, z_ref, sm_ref, alog_ref, dtb_ref, ng_ref,
             dqkv_ref, dz_ref, dsm_ref, dalog_ref, ddtb_ref, dng_ref, dS):
        @pl.when(pl.program_id(0) == 0)
        def _():
            dS[...] = jnp.zeros_like(dS)
            dalog_ref[...] = jnp.zeros_like(dalog_ref)
            ddtb_ref[...] = jnp.zeros_like(ddtb_ref)
            dng_ref[...] = jnp.zeros_like(dng_ref)

        qs, ks, vs, zs, as_, bs = _gdn_load((q_ref, k_ref, v_ref, z_ref, sm_ref), nc)
        _, vjp = jax.vjp(_gdn_block, s_ref[...], qs, ks, vs, as_, bs, zs, alog_ref[...], dtb_ref[...], ng_ref[...])
        dys = [_heads(dy_ref, _chunk_rows(c), HD) for c in range(nc)]
        d_s, dqs, dks, dvs, das, dbs, dzs, d_alog, d_dtb, d_ng = vjp((dS[...], dys))
        dS[...] = d_s
        dalog_ref[...] += d_alog
        ddtb_ref[...] += d_dtb
        dng_ref[...] += d_ng
        for c in range(nc):
            sl = _chunk_rows(c)
            dqkv_ref[0:NH, sl, :] = dqs[c]
            dqkv_ref[NH:2 * NH, sl, :] = dks[c]
            dqkv_ref[2 * NH:3 * NH, sl, :] = dvs[c]
            for h in range(NH):
                dz_ref[sl, h * HD:(h + 1) * HD] = dzs[c][h]
            dsm_ref[sl, :] = _pack_cols([das[c][h] for h in range(NH)] + [dbs[c][h] for h in range(NH)])

    rev = lambda i: nb - 1 - i
    hm = lambda h0: pl.BlockSpec((NH, cb, HD), lambda i: (h0, rev(i), 0))
    tok = lambda w, cblk: pl.BlockSpec((cb, w), lambda i: (rev(i), cblk))
    par = pl.BlockSpec((NH, 1, 1), lambda i: (0, 0, 0))
    ngs = pl.BlockSpec((1, 1, HD), lambda i: (0, 0, 0))
    res = pl.pallas_call(
        body, grid=(nb,),
        in_specs=[tok(GW, 0), pl.BlockSpec((None, NH, HD, HD), lambda i: (rev(i), 0, 0, 0)),
                  hm(0), hm(1), hm(2), tok(GW, C_GZ // GW), tok(128, C_SMALL // 128), par, par, ngs],
        out_specs=[pl.BlockSpec((3 * NH, cb, HD), lambda i: (0, rev(i), 0)), tok(GW, 0), tok(128, 0), par, par, ngs],
        out_shape=[_sds((3 * NH, T, HD)), _sds((T, GW)), _sds((T, 128))] + [_sds((NH, 1, 1))] * 2 + [_sds((1, 1, HD))],
        scratch_shapes=[pltpu.VMEM((NH, HD, HD), f32)],
        compiler_params=_cparams(("arbitrary",)), name=name)(dycat, states, qkv_hm, qkv_hm, qkv_hm, proj, proj, alog, dtb, ng)
    return res


F_LANE = 8
SCAN_TB = 256


def fox_gate_fwd(proj, bfv, name):
    T = proj.shape[0]
    tb = min(T, SCAN_TB)

    def body(x_ref, b_ref, o_ref, carry):
        @pl.when(pl.program_id(0) == 0)
        def _():
            carry[...] = jnp.zeros_like(carry)
        logf = -_softplus(-(x_ref[...] + b_ref[...]))
        r = lax.broadcasted_iota(jnp.int32, (tb, tb), 0)
        c = lax.broadcasted_iota(jnp.int32, (tb, tb), 1)
        tri = (r >= c).astype(bf16)
        hi, lo = _split_hi_lo(logf)
        cum = _dot(tri, hi) + _dot(tri, lo) + carry[0:1, :]
        o_ref[...] = cum
        carry[0:1, :] = cum[tb - 1:tb, :]

    return pl.pallas_call(
        body, grid=(T // tb,),
        in_specs=[pl.BlockSpec((tb, 128), lambda i: (i, C_SMALL // 128)), pl.BlockSpec((1, 128), lambda i: (0, 0))],
        out_specs=pl.BlockSpec((tb, 128), lambda i: (i, 0)), out_shape=_sds((T, 128)),
        scratch_shapes=[pltpu.VMEM((8, 128), f32)],
        compiler_params=_cparams(("arbitrary",)), name=name)(proj, bfv)


def fox_gate_bwd(dcum, proj, bfv, name):
    T = proj.shape[0]
    tb = min(T, SCAN_TB)
    nb = T // tb

    def body(d_ref, x_ref, b_ref, o_ref, db_ref, carry):
        @pl.when(pl.program_id(0) == 0)
        def _():
            carry[...] = jnp.zeros_like(carry)
            db_ref[...] = jnp.zeros_like(db_ref)
        r = lax.broadcasted_iota(jnp.int32, (tb, tb), 0)
        c = lax.broadcasted_iota(jnp.int32, (tb, tb), 1)
        tri = (c >= r).astype(bf16)
        hi, lo = _split_hi_lo(d_ref[...])
        dlogf = _dot(tri, hi) + _dot(tri, lo) + carry[0:1, :]
        carry[0:1, :] = dlogf[0:1, :]
        lane = lax.broadcasted_iota(jnp.int32, (tb, 128), 1)
        keep = (lane >= F_LANE) & (lane < F_LANE + NH)
        dx = jnp.where(keep, dlogf * jax.nn.sigmoid(-(x_ref[...] + b_ref[...])), 0.0)
        o_ref[...] = dx
        db_ref[...] += jnp.sum(dx, axis=0, keepdims=True)

    rev = lambda i: nb - 1 - i
    return pl.pallas_call(
        body, grid=(nb,),
        in_specs=[pl.BlockSpec((tb, 128), lambda i: (rev(i), 0)), pl.BlockSpec((tb, 128), lambda i: (rev(i), C_SMALL // 128)),
                  pl.BlockSpec((1, 128), lambda i: (0, 0))],
        out_specs=[pl.BlockSpec((tb, 128), lambda i: (rev(i), 0)), pl.BlockSpec((1, 128), lambda i: (0, 0))],
        out_shape=[_sds((T, 128)), _sds((1, 128))],
        scratch_shapes=[pltpu.VMEM((8, 128), f32)],
        compiler_params=_cparams(("arbitrary",)), name=name)(dcum, proj, bfv)


ATT_TQ = 1024
ATT_TQ_BWD = 512
ATT_TK = 256


def _lane_col(tile, lane):
    li = lax.broadcasted_iota(jnp.int32, tile.shape, 1)
    return jnp.sum(jnp.where(li == lane, tile, 0.0), axis=1, keepdims=True)


def _pack_cols(cols):
    rows = cols[0].shape[0]
    li = lax.broadcasted_iota(jnp.int32, (rows, 128), 1)
    out = jnp.zeros((rows, 128), f32)
    for h, cv in enumerate(cols):
        out = jnp.where(li == h, cv, out)
    return out


def _head_masks():
    li = lax.broadcasted_iota(jnp.int32, (1, 128), 1)
    return [li < HD, li >= HD]


def _qkv_specs(T, tq, base_blk):
    q = pl.BlockSpec((tq, 128), lambda p, i: (i, base_blk + p))
    k = pl.BlockSpec((T, 128), lambda p, i: (0, base_blk + 2 + p))
    v = pl.BlockSpec((T, 128), lambda p, i: (0, base_blk + 4 + p))
    return q, k, v


def _stack_heads(x, masks):
    return jnp.concatenate([jnp.where(m, x, jnp.zeros_like(x)) for m in masks], axis=0)


def _side_by_side(x, tq):
    return jnp.concatenate([x[:tq], x[tq:]], axis=1)


def _stacked_mask(tq, tk, d, strict):
    r = lax.broadcasted_iota(jnp.int32, (2 * tq, tk), 0)
    r = jnp.where(r >= tq, r - tq, r)
    c = lax.broadcasted_iota(jnp.int32, (2 * tq, tk), 1) + d * tk
    return c < r if strict else c <= r


def _sub_head_rows(s, rows2, tq):
    return jnp.concatenate([s[:tq] - rows2[0:1, :], s[tq:] - rows2[1:2, :]], axis=0)


def _lane_cols2(tile):
    return jnp.concatenate([_lane_col(tile, 0), _lane_col(tile, 1)], axis=0)


def _pack_cols2(col, tq):
    return _pack_cols([col[:tq], col[tq:]])


def _dot_hilo2(x, tri):
    n = x.shape[0]
    hi, lo = _split_hi_lo(x)
    r = _dot(jnp.concatenate([hi, lo], axis=0), tri)
    return r[:n] + r[n:]


def fox_fwd(pbf, cumrow, name, rider=None):
    T = pbf.shape[0]
    tq, tk = min(T, ATT_TQ), min(T, ATT_TK)
    nq, nk, per = T // tq, T // tk, tq // tk
    scale = HD ** -0.5

    def body(q_ref, k_ref, v_ref, cr_ref, o_ref, lse_ref):
        i = pl.program_id(1)
        masks = _head_masks()
        qs2 = _stack_heads(q_ref[...] * scale, masks)

        def tile(kb, carry, d=None):
            m, l, acc = carry
            off = pl.multiple_of(kb * tk, tk)
            v2 = _stack_heads(v_ref[pl.ds(off, tk), :], masks)
            s = _sub_head_rows(_dot_nt(qs2, k_ref[pl.ds(off, tk), :]), cr_ref[kb], tq)
            if d is not None:
                s = jnp.where(_stacked_mask(tq, tk, d, False), s, NEG)
            m_new = jnp.maximum(m, jnp.max(s, axis=1, keepdims=True))
            corr = jnp.exp(m - m_new)
            p = jnp.exp(s - m_new)
            l = l * corr + jnp.sum(p, axis=1, keepdims=True)
            acc = acc * jnp.where(masks[0], corr[:tq], corr[tq:]) + _dot(_side_by_side(p.astype(bf16), tq), v2)
            return m_new, l, acc

        init = (jnp.full((2 * tq, 1), NEG, f32), jnp.zeros((2 * tq, 1), f32), jnp.zeros((tq, 128), f32))
        carry = lax.fori_loop(0, i * per, tile, init)
        for d in range(per):
            carry = tile(i * per + d, carry, d)
        m, l, acc = carry
        o_ref[...] = acc * jnp.where(masks[0], 1.0 / l[:tq], 1.0 / l[tq:])
        lse_ref[...] = _pack_cols2(m + jnp.log(l), tq)

    qs, ks, vs = _qkv_specs(T, tq, C_FOX // 128)
    body, r_in, r_args, r_out, r_shape, r_scratch = _ride(body, 4, 2, rider, (2, nq))
    res = pl.pallas_call(
        body, grid=(2, nq),
        in_specs=[qs, ks, vs, pl.BlockSpec((None, nk, 8, tk), lambda p, i: (p, 0, 0, 0))] + r_in,
        out_specs=[pl.BlockSpec((tq, 128), lambda p, i: (i, p)), pl.BlockSpec((None, tq, 128), lambda p, i: (p, i, 0))] + r_out,
        out_shape=[_sds((T, GW)), _sds((2, T, 128))] + r_shape, scratch_shapes=r_scratch,
        compiler_params=_cparams(("arbitrary", "arbitrary")), name=name)(pbf, pbf, pbf, cumrow, *r_args)
    return res[0], res[1], res[2:]


def fox_bwd(do, o, lse, pbf, cumrow, name, rider=None):
    T = pbf.shape[0]
    tq, tk = min(T, ATT_TQ_BWD), min(T, ATT_TK)
    nq, nk, per = T // tq, T // tk, tq // tk
    scale = HD ** -0.5

    def body(do_ref, o_ref, lse_ref, q_ref, k_ref, v_ref, cr_ref, dq_ref, dk_ref, dv_ref, dc_ref, dcq_ref):
        i = pl.program_id(1)

        @pl.when(i == 0)
        def _():
            dk_ref[...] = jnp.zeros_like(dk_ref)
            dv_ref[...] = jnp.zeros_like(dv_ref)
            dc_ref[...] = jnp.zeros_like(dc_ref)

        masks = _head_masks()
        dov = do_ref[...]
        qs2 = _stack_heads(q_ref[...] * scale, masks)
        do2 = _stack_heads(dov.astype(bf16), masks)
        prod = dov * o_ref[...]
        delta = jnp.concatenate([jnp.sum(jnp.where(m, prod, 0.0), axis=1, keepdims=True) for m in masks], axis=0)
        lse2 = _lane_cols2(lse_ref[...])

        def tile(kb, carry, d=None):
            dq, rsum = carry
            off = pl.multiple_of(kb * tk, tk)
            kblk = k_ref[pl.ds(off, tk), :]
            p = jnp.exp(_sub_head_rows(_dot_nt(qs2, kblk), cr_ref[kb], tq) - lse2)
            if d is not None:
                p = jnp.where(_stacked_mask(tq, tk, d, False), p, 0.0)
            dp = _dot_nt(do2, v_ref[pl.ds(off, tk), :])
            ds = p * (dp - delta)
            dsb = ds.astype(bf16)
            dq = dq + _dot(_side_by_side(dsb, tq), _stack_heads(kblk, masks))
            dk_ref[pl.ds(off, tk), :] += _dot_tn(dsb, qs2)
            dv_ref[pl.ds(off, tk), :] += _dot_tn(p.astype(bf16), do2)
            dc_ref[kb, 0:1, :] += -jnp.sum(ds[:tq], axis=0, keepdims=True)
            dc_ref[kb, 1:2, :] += -jnp.sum(ds[tq:], axis=0, keepdims=True)
            return dq, rsum + jnp.sum(ds, axis=1, keepdims=True)

        carry = lax.fori_loop(0, i * per, tile, (jnp.zeros((tq, 128), f32), jnp.zeros((2 * tq, 1), f32)))
        for d in range(per):
            carry = tile(i * per + d, carry, d)
        dq, rsum = carry
        dq_ref[...] = dq * scale
        dcq_ref[...] = _pack_cols2(rsum, tq)

    qs, ks, vs = _qkv_specs(T, tq, C_FOX // 128)
    tile_spec = pl.BlockSpec((tq, 128), lambda p, i: (i, p))
    pair = pl.BlockSpec((None, tq, 128), lambda p, i: (p, i, 0))
    rowsp = pl.BlockSpec((None, nk, 8, tk), lambda p, i: (p, 0, 0, 0))
    full = pl.BlockSpec((T, 128), lambda p, i: (0, p))
    body, r_in, r_args, r_out, r_shape, r_scratch = _ride(body, 7, 5, rider, (2, nq))
    res = pl.pallas_call(
        body, grid=(2, nq),
        in_specs=[tile_spec, tile_spec, pair, qs, ks, vs, rowsp] + r_in,
        out_specs=[tile_spec, full, full, rowsp, pair] + r_out,
        out_shape=[_sds((T, GW)), _sds((T, GW)), _sds((T, GW)), _sds((2, nk, 8, tk)), _sds((2, T, 128))] + r_shape,
        scratch_shapes=r_scratch,
        compiler_params=_cparams(("arbitrary", "arbitrary")), name=name)(do, o, lse, pbf, pbf, pbf, cumrow, *r_args)
    return res[0], res[1], res[2], res[3], res[4], res[5:]


def _tri(tq, pred):
    r = lax.broadcasted_iota(jnp.int32, (tq, tq), 0)
    c = lax.broadcasted_iota(jnp.int32, (tq, tq), 1)
    return pred(r, c).astype(bf16)


def _ride(body, n_in, n_out, rider, grid, n_scratch=0):
    if rider is None:
        return body, [], [], [], [], []
    nr = rider.n

    def wrapped(*refs):
        ins, rin = refs[:n_in], refs[n_in:n_in + nr]
        outs = refs[n_in + nr:n_in + nr + n_out]
        rout = refs[n_in + nr + n_out:n_in + 2 * nr + n_out]
        own = refs[n_in + 2 * nr + n_out:n_in + 2 * nr + n_out + n_scratch]
        sems = refs[n_in + 2 * nr + n_out + n_scratch:]
        ids = [pl.program_id(a) for a in range(len(grid))]
        first = functools.reduce(jnp.logical_and, [i == 0 for i in ids])
        last = functools.reduce(jnp.logical_and, [i == g - 1 for i, g in zip(ids, grid)])

        @pl.when(first)
        def _():
            rider.start(rin, rout, sems)

        body(*ins, *outs, *own)

        @pl.when(last)
        def _():
            rider.finish(rin, rout, sems)

    anyspec = pl.BlockSpec(memory_space=pl.ANY)
    return wrapped, [anyspec] * nr, list(rider.arrays), [anyspec] * nr, list(rider.out_shape), list(rider.scratch)


def sb_fwd(pbf, name, rider=None):
    T = pbf.shape[0]
    tq, tk = min(T, ATT_TQ), min(T, ATT_TK)
    nq, per = T // tq, tq // tk
    scale = HD ** -0.5

    def body(q_ref, k_ref, v_ref, o_ref, tot_ref):
        i = pl.program_id(1)
        masks = _head_masks()
        qs2 = _stack_heads(q_ref[...] * scale, masks)
        after = _tri(tk, lambda r, c: r > c)

        def tile(kb, carry, d=None):
            rs, acc = carry
            off = pl.multiple_of(kb * tk, tk)
            z = _dot_nt(qs2, k_ref[pl.ds(off, tk), :])
            lk = _neg_softplus(z)
            if d is not None:
                lk = jnp.where(_stacked_mask(tq, tk, d, True), lk, 0.0)
            w = jnp.exp(z + lk + (_dot_hilo2(lk, after) + rs))
            if d is not None:
                w = jnp.where(_stacked_mask(tq, tk, d, True), w, 0.0)
            acc = acc + _dot(_side_by_side(w.astype(bf16), tq), _stack_heads(v_ref[pl.ds(off, tk), :], masks))
            return rs + jnp.sum(lk, axis=1, keepdims=True), acc

        carry = (jnp.zeros((2 * tq, 1), f32), jnp.zeros((tq, 128), f32))
        for d in reversed(range(per)):
            carry = tile(i * per + d, carry, d)
        rs, acc = lax.fori_loop(0, i * per, lambda n, c: tile(i * per - 1 - n, c), carry)
        o_ref[...] = acc
        tot_ref[...] = _pack_cols2(rs, tq)

    qs, ks, vs = _qkv_specs(T, tq, C_SB // 128)
    body, r_in, r_args, r_out, r_shape, r_scratch = _ride(body, 3, 2, rider, (2, nq))
    res = pl.pallas_call(
        body, grid=(2, nq), in_specs=[qs, ks, vs] + r_in,
        out_specs=[pl.BlockSpec((tq, 128), lambda p, i: (i, p)), pl.BlockSpec((None, tq, 128), lambda p, i: (p, i, 0))] + r_out,
        out_shape=[_sds((T, GW)), _sds((2, T, 128))] + r_shape, scratch_shapes=r_scratch,
        compiler_params=_cparams(("arbitrary", "arbitrary")), name=name)(pbf, pbf, pbf, *r_args)
    return res[0], res[1], res[2:]


def sb_bwd(do, tot, pbf, name, rider=None):
    T = pbf.shape[0]
    tq, tk = min(T, ATT_TQ_BWD), min(T, ATT_TK)
    nq, per = T // tq, tq // tk
    scale = HD ** -0.5

    def body(do_ref, tot_ref, q_ref, k_ref, v_ref, dq_ref, dk_ref, dv_ref):
        i = pl.program_id(1)

        @pl.when(i == 0)
        def _():
            dk_ref[...] = jnp.zeros_like(dk_ref)
            dv_ref[...] = jnp.zeros_like(dv_ref)

        masks = _head_masks()
        qs2 = _stack_heads(q_ref[...] * scale, masks)
        do2 = _stack_heads(do_ref[...].astype(bf16), masks)
        tot2 = _lane_cols2(tot_ref[...])
        upto = _tri(tk, lambda r, c: r <= c)
        before = _tri(tk, lambda r, c: r < c)

        def tile(kb, carry, d=None):
            pre, cg, dq = carry
            off = pl.multiple_of(kb * tk, tk)
            kblk = k_ref[pl.ds(off, tk), :]
            z = _dot_nt(qs2, kblk)
            lk = _neg_softplus(z)
            keep = jnp.exp(lk)
            if d is not None:
                lk = jnp.where(_stacked_mask(tq, tk, d, True), lk, 0.0)
            w = jnp.exp(z + lk + (tot2 - (pre + _dot_hilo2(lk, upto))))
            if d is not None:
                w = jnp.where(_stacked_mask(tq, tk, d, True), w, 0.0)
            gmat = w * _dot_nt(do2, v_ref[pl.ds(off, tk), :])
            cmat = cg + _dot(gmat.astype(bf16), before)
            dz = keep * (gmat + cmat) - cmat
            if d is not None:
                dz = jnp.where(_stacked_mask(tq, tk, d, True), dz, 0.0)
            dzb = dz.astype(bf16)
            dq = dq + _dot(_side_by_side(dzb, tq), _stack_heads(kblk, masks))
            dk_ref[pl.ds(off, tk), :] += _dot_tn(dzb, qs2)
            dv_ref[pl.ds(off, tk), :] += _dot_tn(w.astype(bf16), do2)
            return pre + jnp.sum(lk, axis=1, keepdims=True), cg + jnp.sum(gmat, axis=1, keepdims=True), dq

        zc = jnp.zeros((2 * tq, 1), f32)
        carry = lax.fori_loop(0, i * per, tile, (zc, zc, jnp.zeros((tq, 128), f32)))
        for d in range(per):
            carry = tile(i * per + d, carry, d)
        dq_ref[...] = carry[2] * scale

    qs, ks, vs = _qkv_specs(T, tq, C_SB // 128)
    tile_spec = pl.BlockSpec((tq, 128), lambda p, i: (i, p))
    pair = pl.BlockSpec((None, tq, 128), lambda p, i: (p, i, 0))
    full = pl.BlockSpec((T, 128), lambda p, i: (0, p))
    body, r_in, r_args, r_out, r_shape, r_scratch = _ride(body, 5, 3, rider, (2, nq))
    res = pl.pallas_call(
        body, grid=(2, nq), in_specs=[tile_spec, pair, qs, ks, vs] + r_in,
        out_specs=[tile_spec, full, full] + r_out, out_shape=[_sds((T, GW))] * 3 + r_shape, scratch_shapes=r_scratch,
        compiler_params=_cparams(("arbitrary", "arbitrary")), name=name)(do, tot, pbf, pbf, pbf, *r_args)
    return res[0], res[1], res[2], res[3:]


MEM_HD = D // 4


def mem_fwd(q, kv, name):
    T = q.shape[0]
    M = kv.shape[0]
    tm = min(T, 512)
    scale = MEM_HD ** -0.5

    def body(q_ref, kv_ref, o_ref):
        for h in range(4):
            sl = slice(h * MEM_HD, (h + 1) * MEM_HD)
            kh = kv_ref[:, sl].astype(bf16)
            vh = kv_ref[:, D + h * MEM_HD:D + (h + 1) * MEM_HD].astype(bf16)
            s = _dot_nt(q_ref[:, sl], kh) * scale
            e = jnp.exp(s - jnp.max(s, axis=1, keepdims=True))
            p = e / jnp.sum(e, axis=1, keepdims=True)
            o_ref[:, sl] = _dot(p.astype(bf16), vh).astype(bf16)

    return pl.pallas_call(
        body, grid=(T // tm,),
        in_specs=[pl.BlockSpec((tm, D), lambda i: (i, 0)), pl.BlockSpec((M, 2 * D), lambda i: (0, 0))],
        out_specs=pl.BlockSpec((tm, D), lambda i: (i, 0)), out_shape=_sds((T, D), bf16),
        compiler_params=_cparams(("parallel",)), name=name)(q, kv)


def mem_bwd(do, q, kv, name):
    T = q.shape[0]
    M = kv.shape[0]
    tm = min(T, 512)
    scale = MEM_HD ** -0.5

    def body(do_ref, q_ref, kv_ref, dq_ref, dkv_ref):
        @pl.when(pl.program_id(0) == 0)
        def _():
            dkv_ref[...] = jnp.zeros_like(dkv_ref)
        for h in range(4):
            sl = slice(h * MEM_HD, (h + 1) * MEM_HD)
            vsl = slice(D + h * MEM_HD, D + (h + 1) * MEM_HD)
            qh = q_ref[:, sl]
            kh = kv_ref[:, sl].astype(bf16)
            vh = kv_ref[:, vsl].astype(bf16)
            doh = do_ref[:, sl].astype(bf16)
            s = _dot_nt(qh, kh) * scale
            e = jnp.exp(s - jnp.max(s, axis=1, keepdims=True))
            p = e / jnp.sum(e, axis=1, keepdims=True)
            dp = _dot_nt(doh, vh)
            ds = p * (dp - jnp.sum(dp * p, axis=1, keepdims=True))
            dsb = ds.astype(bf16)
            dq_ref[:, sl] = _dot(dsb, kh) * scale
            dkv_ref[:, sl] += _dot_tn(dsb, qh) * scale
            dkv_ref[:, vsl] += _dot_tn(p.astype(bf16), doh)

    row = pl.BlockSpec((tm, D), lambda i: (i, 0))
    whole = pl.BlockSpec((M, 2 * D), lambda i: (0, 0))
    return pl.pallas_call(
        body, grid=(T // tm,), in_specs=[row, row, whole], out_specs=[row, whole],
        out_shape=[_sds((T, D)), _sds((M, 2 * D))],
        compiler_params=_cparams(("arbitrary",)), name=name)(do, q, kv)


def _chip_peers():
    x, y, c = lax.axis_index("x"), lax.axis_index("y"), lax.axis_index("c")
    me = 2 * x + y
    peers = [((1 - x, y, c), 2 * (1 - x) + y), ((x, 1 - y, c), 2 * x + (1 - y)), ((1 - x, 1 - y, c), 2 * (1 - x) + (1 - y))]
    return me, peers


class Rider:
    def __init__(self, arrays, out_shape, scratch, start, finish):
        self.arrays, self.out_shape, self.scratch, self.start, self.finish = arrays, out_shape, scratch, start, finish
        self.n = len(arrays)

    def split(self, refs):
        return refs[:self.n], refs[self.n:2 * self.n], refs[2 * self.n:]


def run_rider(rider, name):
    def body(*refs):
        parts = rider.split(refs)
        rider.start(*parts)
        rider.finish(*parts)

    anyspec = pl.BlockSpec(memory_space=pl.ANY)
    return pl.pallas_call(
        body, in_specs=[anyspec] * rider.n, out_specs=[anyspec] * rider.n, out_shape=rider.out_shape,
        scratch_shapes=rider.scratch, name=name)(*rider.arrays)


def gather_rider(arrs, split):
    n = len(arrs)

    def copies(ins, outs, sems):
        send_sems, recv_sems, pass_send, pass_recv, loc_sems = sems
        x, y, c = lax.axis_index("x"), lax.axis_index("y"), lax.axis_index("c")
        me, peers = _chip_peers()

        def mine(ref, i):
            if not split[i]:
                return ref
            r = arrs[i].shape[1] // 2
            return ref.at[:, pl.ds(c * r, r), :]

        local, fetch, passed = [], [], []
        for i in range(n):
            local.append(pltpu.make_async_copy(ins[i], outs[i].at[me], loc_sems.at[i]))
            for k, (dev, pj) in enumerate(peers):
                fetch.append(pltpu.make_async_remote_copy(src_ref=mine(ins[i], i), dst_ref=mine(outs[i].at[me], i),
                                                          send_sem=send_sems.at[i, k], recv_sem=recv_sems.at[i, k],
                                                          device_id=dev, device_id_type=MESH))
                rows = mine(outs[i].at[pj], i)
                passed.append(pltpu.make_async_remote_copy(src_ref=rows, dst_ref=rows, send_sem=pass_send.at[i, k],
                                                           recv_sem=pass_recv.at[i, k], device_id=(x, y, 1 - c),
                                                           device_id_type=MESH) if split[i] else None)
        return local, fetch, passed

    def start(ins, outs, sems):
        local, fetch, _ = copies(ins, outs, sems)
        for cp in local + fetch:
            cp.start()

    def finish(ins, outs, sems):
        local, fetch, passed = copies(ins, outs, sems)
        for cp, fw in zip(fetch, passed):
            cp.wait_recv()
            if fw is not None:
                fw.start()
        for cp in fetch:
            cp.wait_send()
        for cp in [fw for fw in passed if fw is not None] + local:
            cp.wait()

    sem = pltpu.SemaphoreType.DMA((n, 3))
    return Rider(list(arrs), [_sds((4,) + a.shape, a.dtype) for a in arrs],
                 [sem, sem, sem, sem, pltpu.SemaphoreType.DMA((n,))], start, finish)


def exchange_rider(stacks):
    n = len(stacks)

    def copies(ins, outs, sems):
        send_sems, recv_sems, loc_sems = sems
        me, peers = _chip_peers()
        out = []
        for i in range(n):
            out.append(pltpu.make_async_copy(ins[i].at[me], outs[i].at[me], loc_sems.at[i]))
            for k, (dev, pj) in enumerate(peers):
                out.append(pltpu.make_async_remote_copy(src_ref=ins[i].at[pj], dst_ref=outs[i].at[me], send_sem=send_sems.at[i, k],
                                                        recv_sem=recv_sems.at[i, k], device_id=dev, device_id_type=MESH))
        return out

    def start(ins, outs, sems):
        for cp in copies(ins, outs, sems):
            cp.start()

    def finish(ins, outs, sems):
        for cp in copies(ins, outs, sems):
            cp.wait()

    return Rider(list(stacks), [_sds(a.shape, a.dtype) for a in stacks],
                 [pltpu.SemaphoreType.DMA((n, 3)), pltpu.SemaphoreType.DMA((n, 3)), pltpu.SemaphoreType.DMA((n,))], start, finish)


def swap_sibling(arrs, name):
    n = len(arrs)

    def body(*refs):
        ins, outs = refs[:n], refs[n:2 * n]
        send_sems, recv_sems = refs[2 * n:]
        x, y, c = lax.axis_index("x"), lax.axis_index("y"), lax.axis_index("c")
        started = []
        for i in range(n):
            cp = pltpu.make_async_remote_copy(src_ref=ins[i], dst_ref=outs[i], send_sem=send_sems.at[i],
                                              recv_sem=recv_sems.at[i], device_id=(x, y, 1 - c), device_id_type=MESH)
            cp.start()
            started.append(cp)
        for cp in started:
            cp.wait()

    anyspec = pl.BlockSpec(memory_space=pl.ANY)
    return pl.pallas_call(
        body, in_specs=[anyspec] * n, out_specs=[anyspec] * n,
        out_shape=[_sds(a.shape, a.dtype) for a in arrs],
        scratch_shapes=[pltpu.SemaphoreType.DMA((n,)), pltpu.SemaphoreType.DMA((n,))],
        name=name)(*arrs)


def gather_all(a, name):
    def body(a_ref, o_ref, send_sems, recv_sems, loc_sem):
        x, y, c = lax.axis_index("x"), lax.axis_index("y"), lax.axis_index("c")
        me = 4 * x + 2 * y + c
        loc = pltpu.make_async_copy(a_ref, o_ref.at[me], loc_sem)
        loc.start()
        started = [loc]
        for k in range(1, 8):
            dev = (x ^ (k >> 2), y ^ ((k >> 1) & 1), c ^ (k & 1))
            cp = pltpu.make_async_remote_copy(src_ref=a_ref, dst_ref=o_ref.at[me], send_sem=send_sems.at[k - 1],
                                              recv_sem=recv_sems.at[k - 1], device_id=dev, device_id_type=MESH)
            cp.start()
            started.append(cp)
        for cp in started:
            cp.wait()

    anyspec = pl.BlockSpec(memory_space=pl.ANY)
    return pl.pallas_call(
        body, in_specs=[anyspec], out_specs=anyspec, out_shape=_sds((8,) + a.shape, a.dtype),
        scratch_shapes=[pltpu.SemaphoreType.DMA((7,)), pltpu.SemaphoreType.DMA((7,)), pltpu.SemaphoreType.DMA(())],
        name=name)(a)


def sum_slots(stack, name):
    n, R, C = stack.shape
    tr = R if R <= 512 else _pick(R, (512, 352, 256))

    def body(s_ref, o_ref):
        acc = s_ref[0].astype(f32)
        for j in range(1, n):
            acc = acc + s_ref[j].astype(f32)
        o_ref[...] = acc

    return pl.pallas_call(
        body, grid=(R // tr,), in_specs=[pl.BlockSpec((n, tr, C), lambda i: (0, i, 0))],
        out_specs=pl.BlockSpec((tr, C), lambda i: (i, 0)), out_shape=_sds((R, C)),
        compiler_params=_cparams(("parallel",)), name=name)(stack)


def adamw(w, g1, g2, m, v, name):
    R, C = w.shape
    tr = R if R <= 512 else _pick(R, (512, 352, 256))
    c1 = 1.0 - ADAM_B1 ** ADAM_STEP
    c2 = 1.0 - ADAM_B2 ** ADAM_STEP

    def body(*refs):
        if g2 is None:
            w_ref, g1_ref, m_ref, v_ref, g_out, d_out, m_out, v_out = refs
            g = g1_ref[...]
        else:
            w_ref, g1_ref, g2_ref, m_ref, v_ref, g_out, d_out, m_out, v_out = refs
            g = g1_ref[...] + g2_ref[...]
        mn = ADAM_B1 * m_ref[...] + (1.0 - ADAM_B1) * g
        vn = ADAM_B2 * v_ref[...] + (1.0 - ADAM_B2) * (g * g)
        g_out[...] = g
        m_out[...] = mn
        v_out[...] = vn
        d_out[...] = -ADAM_LR * ((mn / c1) / (jnp.sqrt(vn / c2) + ADAM_EPS) + ADAM_WD * w_ref[...])

    blk = pl.BlockSpec((tr, C), lambda i: (i, 0))
    args = [w, g1] + ([] if g2 is None else [g2]) + [m, v]
    return pl.pallas_call(
        body, grid=(R // tr,), in_specs=[blk] * len(args), out_specs=[blk] * 4, out_shape=[_sds((R, C))] * 4,
        compiler_params=_cparams(("parallel",)), name=name)(*args)


IN_OFF = (0, 768, 1024, 1028, 1032, 1800, 1804, 2316, 3084)


def regroup_w_in(w):
    seg = lambda i: w[:, IN_OFF[i]:IN_OFF[i + 1]]
    pad = jnp.zeros((w.shape[0], PW - C_SMALL - 12), w.dtype)
    return jnp.concatenate([seg(0), seg(1), seg(4), seg(6), seg(7), seg(2), seg(3), seg(5), pad], axis=1)


def ungroup_w_in(g):
    s = C_SMALL
    return jnp.concatenate([g[:, 0:1024], g[:, s:s + 8], g[:, 1024:1792], g[:, s + 8:s + 12], g[:, 1792:3072]], axis=1)


def col_shards(w):
    c = w.shape[-1] // 4
    return jnp.moveaxis(w.reshape(w.shape[:-1] + (4, c)), -2, 0)


def row_shards(w):
    L, r4, c = w.shape
    return w.reshape(L, 4, r4 // 4, c).transpose(1, 0, 2, 3)


def join_cols(g):
    return jnp.moveaxis(g, 0, -2).reshape(g.shape[1:-1] + (4 * g.shape[-1],))


def join_rows(g):
    _, L, r, c = g.shape
    return g.transpose(1, 0, 2, 3).reshape(L, 4 * r, c)


COL_SHARDED = ("ffn1_w_gate", "ffn1_w_up", "w_in", "gdn_conv_w", "conf_dw_w", "mem_w_kv", "ffn2_w_gate", "ffn2_w_up")
CONV_WEIGHTS = ("gdn_conv_w", "conf_dw_w")
ROW_SHARDED = ("ffn1_w_down", "w_out", "mem_w_q", "mem_w_o", "ffn2_w_down")
REPLICATED = ("ln_ffn1_g", "ln_ffn1_b", "gdn_a_log", "gdn_dt_bias", "gdn_norm_g", "fox_b_f", "conf_dw_b", "conf_norm_g",
              "conf_norm_b", "ln_mix_g", "ln_mix_b", "ln_mem_g", "ln_mem_b", "ln_ffn2_g", "ln_ffn2_b")
WEIGHTS = ("ffn1_w_gate", "ffn1_w_up", "ffn1_w_down", "ln_ffn1_g", "ln_ffn1_b", "w_in", "gdn_conv_w", "gdn_a_log",
           "gdn_dt_bias", "gdn_norm_g", "fox_b_f", "conf_dw_w", "conf_dw_b", "conf_norm_g", "conf_norm_b", "w_out",
           "ln_mix_g", "ln_mix_b", "mem_w_q", "mem_w_kv", "mem_w_o", "ln_mem_g", "ln_mem_b", "ffn2_w_gate",
           "ffn2_w_up", "ffn2_w_down", "ln_ffn2_g", "ln_ffn2_b")


def pack_small(d):
    flat = jnp.concatenate([d[n].reshape(-1) for n in REPLICATED])
    rows = -(-flat.shape[0] // 1024) * 8
    return jnp.pad(flat, (0, rows * 128 - flat.shape[0])).reshape(rows, 128)


def unpack_small(p, like):
    flat = p.reshape(-1)
    out, o = {}, 0
    for n in REPLICATED:
        sz = like[n].size
        out[n] = flat[o:o + sz].reshape(like[n].shape)
        o += sz
    return out


def _vec(v):
    return v.reshape(1, -1)


def _pad_rows(w, rows):
    return jnp.pad(w, ((0, rows - w.shape[0]), (0, 0)))


def _small_lane_vec(v4, lane0):
    return jnp.pad(v4.reshape(1, -1), ((0, 0), (lane0, 128 - lane0 - v4.shape[0])))


def layer_fwd(x0, mem, W, li, rider=None, staged=None):
    T = x0.shape[0]
    tk = min(T, ATT_TK)
    n = lambda s: f"l{li}_{s}"
    R = {"x0": x0}
    (ffn1_rider, complete1), (fox_rider, complete2) = staged if staged else ((None, None), (None, None))
    R["z1"], x1, got = ffn_fwd(x0, W["ffn1_w_gate"], W["ffn1_w_up"], W["ffn1_w_down"], _vec(W["ln_ffn1_g"]), _vec(W["ln_ffn1_b"]),
                               n("ffn1_fwd"), ffn1_rider)
    if ffn1_rider is not None:
        W = complete1(W, got)
    R["x1"] = x1
    proj, pbf = mm_nn(x1, W["w_in_r"], n("proj"), also_bf16=True)
    R["proj"], R["pbf"] = proj, pbf

    w8 = _pad_rows(W["gdn_conv_w"], 8)
    qkv_hm = gdn_conv_fwd(proj, w8, n("gdn_conv_fwd"))
    alog = W["gdn_a_log"].reshape(NH, 1, 1)
    dtb = W["gdn_dt_bias"].reshape(NH, 1, 1)
    ng = W["gdn_norm_g"].reshape(1, 1, HD)
    ya, states = gdn_fwd(qkv_hm, proj, alog, dtb, ng, n("gdn_fwd"))
    R.update(qkv_hm=qkv_hm, states=states)

    bfv = _small_lane_vec(W["fox_b_f"], F_LANE)
    cum = fox_gate_fwd(proj, bfv, n("fox_gate_fwd"))
    cum4 = cum[:, F_LANE:F_LANE + NH]
    cumrow = jnp.pad(cum4.T.reshape(2, 2, T // tk, tk).transpose(0, 2, 1, 3), ((0, 0), (0, 0), (0, 6), (0, 0)))
    yb, lse, got = fox_fwd(pbf, cumrow, n("fox_fwd"), fox_rider)
    if fox_rider is not None:
        W = complete2(W, got)
    R.update(cumrow=cumrow, yb=yb, lse=lse)

    w32 = _pad_rows(W["conf_dw_w"], 32)
    yc, cc = conf_fwd(proj, w32, _vec(W["conf_dw_b"]), _vec(W["conf_norm_g"]), _vec(W["conf_norm_b"]), n("conf_fwd"))
    R["cc"] = cc

    yd, tot, rider_out = sb_fwd(pbf, n("sb_fwd"), rider)
    R["tot"] = tot

    ycat = jnp.concatenate([ya, yb, yc, yd], axis=1).astype(bf16)
    R["ycat"] = ycat
    R["z2"], x2 = lin_res_ln(ycat, W["w_out"], x1, _vec(W["ln_mix_g"]), _vec(W["ln_mix_b"]), n("mix_out"))
    R["x2"] = x2

    qm = mm_nn(x2, W["mem_w_q"], n("mem_q"), out_dtype=bf16)
    kv = mm_nn(mem, W["mem_w_kv"], n("mem_kv"))
    om = mem_fwd(qm, kv, n("mem_fwd"))
    R.update(qm=qm, kv=kv, om=om)
    R["z3"], x3 = lin_res_ln(om, W["mem_w_o"], x2, _vec(W["ln_mem_g"]), _vec(W["ln_mem_b"]), n("mem_out"))
    R["x3"] = x3
    R["z4"], x4, _ = ffn_fwd(x3, W["ffn2_w_gate"], W["ffn2_w_up"], W["ffn2_w_down"], _vec(W["ln_ffn2_g"]), _vec(W["ln_ffn2_b"]), n("ffn2_fwd"))
    return x4, R, rider_out, W


EARLY_GRADS = ("ffn2_w_gate", "ffn2_w_up", "ffn2_w_down", "mem_w_q", "mem_w_kv", "mem_w_o", "w_out")
LATE_GRADS = ("w_in", "gdn_conv_w", "conf_dw_w")
LAST_GRADS = ("ffn1_w_gate", "ffn1_w_up", "ffn1_w_down")


def grad_slots(G, names):
    out = []
    for k in names:
        s = col_shards(G[k][None]) if k in COL_SHARDED else row_shards(G[k][None])
        out.append(s if k in CONV_WEIGHTS else s.astype(bf16))
    return out


def layer_bwd(dx4, mem, W, R, li, rider=None, exchange=True, last=False):
    T = dx4.shape[0]
    n = lambda s: f"l{li}_{s}"
    G = {}

    def ffn_back(dy, z, x, pre, tag, ride=None, stagger=False):
        dz, dg, db = ln_bwd(dy, z, _vec(W[f"ln_{pre}_g"]), n(f"{tag}_ln_bwd"))
        (dx, a, dh, du, dzh), got = ffn_bwd(dz, x, W[f"{pre}_w_gate"], W[f"{pre}_w_up"], W[f"{pre}_w_down"], n(f"{tag}_bwd"), ride)
        send = lambda k: exchange_rider(grad_slots(G, (k,))) if stagger else None
        G[f"{pre}_w_gate"] = mm_tn(x, dh, n(f"{tag}_dwg"))
        up = mm_tn(x, du, n(f"{tag}_dwu"), send(f"{pre}_w_gate"))
        G[f"{pre}_w_up"], got_gate = up if stagger else (up, None)
        down = mm_tn(a, dzh, n(f"{tag}_dwd"), send(f"{pre}_w_up"))
        G[f"{pre}_w_down"], got_up = down if stagger else (down, None)
        G[f"ln_{pre}_g"], G[f"ln_{pre}_b"] = dg.reshape(-1), db.reshape(-1)
        return dx, got, got_gate, got_up

    dx3 = ffn_back(dx4, R["z4"], R["x3"], "ffn2", "ffn2")[0]

    dz3, dg, db = ln_bwd(dx3, R["z3"], _vec(W["ln_mem_g"]), n("mem_ln_bwd"))
    G["ln_mem_g"], G["ln_mem_b"] = dg.reshape(-1), db.reshape(-1)
    dom = mm_nt(dz3, W["mem_w_o"], n("mem_dom"))
    G["mem_w_o"] = mm_tn(R["om"], dz3, n("mem_dwo"))
    dqm, dkv = mem_bwd(dom, R["qm"], R["kv"], n("mem_bwd"))
    G["mem_w_q"] = mm_tn(R["x2"], dqm, n("mem_dwq"))
    G["mem_w_kv"] = mm_tn(mem, dkv, n("mem_dwkv"))
    dx2 = mm_nt(dqm, W["mem_w_q"], n("mem_dx"), add=dz3, add_scale=ALPHA)

    dz2, dg, db = ln_bwd(dx2, R["z2"], _vec(W["ln_mix_g"]), n("mix_ln_bwd"))
    G["ln_mix_g"], G["ln_mix_b"] = dg.reshape(-1), db.reshape(-1)
    dycat = mm_nt(dz2, W["w_out"], n("mix_dycat"))
    G["w_out"] = mm_tn(R["ycat"], dz2, n("mix_dwout"))
    dyb, dyc, dyd = (dycat[:, i * GW:(i + 1) * GW] for i in range(1, 4))
    proj, pbf = R["proj"], R["pbf"]

    alog = W["gdn_a_log"].reshape(NH, 1, 1)
    dtb = W["gdn_dt_bias"].reshape(NH, 1, 1)
    ng = W["gdn_norm_g"].reshape(1, 1, HD)
    dqkv_hm, dgz, dsmall_ab, dalog, ddtb, dng = gdn_bwd(dycat, R["states"], R["qkv_hm"], proj, alog, dtb, ng, n("gdn_bwd"))
    G["gdn_a_log"], G["gdn_dt_bias"], G["gdn_norm_g"] = dalog.reshape(-1), ddtb.reshape(-1), dng.reshape(-1)
    w8 = _pad_rows(W["gdn_conv_w"], 8)
    dgqkv, dw8 = gdn_conv_bwd(dqkv_hm, proj, w8, n("gdn_conv_bwd"))
    G["gdn_conv_w"] = dw8[:GDN_K]

    dfq, dfk, dfv, dcumrow, dcumq, got_early = fox_bwd(dyb, R["yb"], R["lse"], pbf, R["cumrow"], n("fox_bwd"),
                                                      exchange_rider(grad_slots(G, EARLY_GRADS)) if exchange else None)
    dcum4 = dcumrow[:, :, 0:2, :].transpose(0, 2, 1, 3).reshape(4, T).T
    dcum4 = dcum4 + dcumq[:, :, 0:2].transpose(1, 0, 2).reshape(T, 4)
    dcum = jnp.pad(dcum4, ((0, 0), (F_LANE, 128 - F_LANE - NH)))
    bfv = _small_lane_vec(W["fox_b_f"], F_LANE)
    dsmall_f, dbf = fox_gate_bwd(dcum, proj, bfv, n("fox_gate_bwd"))
    G["fox_b_f"] = dbf[0, F_LANE:F_LANE + NH]

    w32 = _pad_rows(W["conf_dw_w"], 32)
    dglu, dw32, dcb, dcg, dcbeta = conf_bwd(dyc, R["cc"], proj, w32, _vec(W["conf_norm_g"]), _vec(W["conf_norm_b"]), n("conf_bwd"))
    G["conf_dw_w"], G["conf_dw_b"] = dw32[:CONF_K], dcb.reshape(-1)
    G["conf_norm_g"], G["conf_norm_b"] = dcg.reshape(-1), dcbeta.reshape(-1)

    dsq, dsk, dsv, got_carried = sb_bwd(dyd, R["tot"], pbf, n("sb_bwd"), rider)

    dsmall = dsmall_ab + dsmall_f
    dproj = [dgqkv, dgz, dfq, dfk, dfv, dglu, dsq, dsk, dsv, dsmall]
    G["w_in"] = ungroup_w_in(mm_tn(R["x1"], dproj, n("proj_dw")))
    dx1 = mm_nt(dproj, W["w_in_r"], n("proj_dx"), add=dz2, add_scale=ALPHA)

    dx0, got_late, got_gate, got_up = ffn_back(dx1, R["z1"], R["x0"], "ffn1", "ffn1",
                                                exchange_rider(grad_slots(G, LATE_GRADS)) if exchange else None, stagger=last)
    return dx0, G, {"carried": got_carried, "early": got_early, "late": got_late, "gate": got_gate, "up": got_up}


def _step(P, M, V, x, mem, loss_target):
    xs, mems, tgt = x[0], mem[0], loss_target[0]

    big = COL_SHARDED + ROW_SHARDED

    def weight_gather(li, names):
        return gather_rider([P[k][li:li + 1] if k in CONV_WEIGHTS else P[k][li:li + 1].astype(bf16) for k in names],
                            [k not in CONV_WEIGHTS for k in names])

    def layer_weights(li, names, gathered, W=None):
        W = dict(W) if W else {k: P[k][li] for k in REPLICATED}
        W.update({k: (join_cols(g) if k in COL_SHARDED else join_rows(g))[0] for k, g in zip(names, gathered)})
        if "w_in" in names:
            W["w_in_r"] = regroup_w_in(W["w_in"])
        return W

    W = layer_weights(0, LAST_GRADS, run_rider(weight_gather(0, LAST_GRADS), "gather_weights_first"))
    staged = tuple((weight_gather(0, names), functools.partial(lambda names, W0, got: layer_weights(0, names, got, W0), names))
                   for names in (LATE_GRADS, EARLY_GRADS))
    layers, saved = [], []
    h = xs
    for li in range(DEPTH):
        nxt = weight_gather(li + 1, big) if li + 1 < DEPTH else None
        h, R, gathered, W = layer_fwd(h, mems, W, li, nxt, staged if li == 0 else None)
        layers.append(W)
        saved.append(R)
        if li + 1 < DEPTH:
            W = layer_weights(li + 1, big, gathered)
    loss_row, dy = loss_and_grad(h, tgt, "loss")
    loss = lax.psum(loss_row[0, 0], ("x", "y", "c"))

    grads, received_at = [None] * DEPTH, [dict() for _ in range(DEPTH)]
    rider = None
    for li in reversed(range(DEPTH)):
        dy, G, got = layer_bwd(dy, mems, layers[li], saved[li], li, rider, last=(li == 0))
        if rider is not None:
            received_at[li + 1].update(zip(LAST_GRADS, got["carried"]))
        received_at[li].update(zip(EARLY_GRADS, got["early"]))
        received_at[li].update(zip(LATE_GRADS, got["late"]))
        grads[li] = G
        rider = exchange_rider(grad_slots(G, LAST_GRADS)) if li > 0 else None
    received_at[0].update(ffn1_w_gate=got["gate"][0], ffn1_w_up=got["up"][0])
    received_at[0]["ffn1_w_down"] = run_rider(exchange_rider(grad_slots(grads[0], ("ffn1_w_down",))), "exchange_grads_last")[0]
    grad_x = dy[None]

    stacked = {k: jnp.stack([grads[li][k] for li in range(DEPTH)]) for k in WEIGHTS}
    received = [jnp.concatenate([received_at[li][k] for li in range(DEPTH)], axis=1) for k in big]
    partial_sums = []
    for k, r in zip(big, received):
        shp = r.shape
        partial_sums.append(sum_slots(r.reshape(4, -1, shp[-1]), f"sum_{k}"))
    from_sibling = swap_sibling(partial_sums, "swap_partials")

    out_g, out_d, out_m, out_v = {}, {}, {}, {}
    for k, mine, theirs in zip(big, partial_sums, from_sibling):
        shp = P[k].shape
        flat = lambda t: t.reshape(-1, shp[-1])
        g, d, mn, vn = adamw(flat(P[k]), mine, theirs, flat(M[k]), flat(V[k]), f"adamw_{k}")
        out_g[k], out_d[k], out_m[k], out_v[k] = (t.reshape(shp) for t in (g, d, mn, vn))

    gsmall = sum_slots(gather_all(pack_small(stacked), "gather_small"), "sum_small")
    g, d, mn, vn = adamw(pack_small(P), gsmall, None, pack_small(M), pack_small(V), "adamw_small")
    for dst, packed in ((out_g, g), (out_d, d), (out_m, mn), (out_v, vn)):
        dst.update(unpack_small(packed, P))

    return (loss, grad_x, *[out_g[k] for k in WEIGHTS], *[out_d[k] for k in WEIGHTS],
            *[out_m[k] for k in WEIGHTS], *[out_v[k] for k in WEIGHTS])


def kernel(x, mem, ffn1_w_gate, ffn1_w_up, ffn1_w_down, ln_ffn1_g, ln_ffn1_b, w_in, gdn_conv_w, gdn_a_log, gdn_dt_bias, gdn_norm_g, fox_b_f, conf_dw_w, conf_dw_b, conf_norm_g, conf_norm_b, w_out, ln_mix_g, ln_mix_b, mem_w_q, mem_w_kv, mem_w_o, ln_mem_g, ln_mem_b, ffn2_w_gate, ffn2_w_up, ffn2_w_down, ln_ffn2_g, ln_ffn2_b, loss_target, m_ffn1_w_gate, m_ffn1_w_up, m_ffn1_w_down, m_ln_ffn1_g, m_ln_ffn1_b, m_w_in, m_gdn_conv_w, m_gdn_a_log, m_gdn_dt_bias, m_gdn_norm_g, m_fox_b_f, m_conf_dw_w, m_conf_dw_b, m_conf_norm_g, m_conf_norm_b, m_w_out, m_ln_mix_g, m_ln_mix_b, m_mem_w_q, m_mem_w_kv, m_mem_w_o, m_ln_mem_g, m_ln_mem_b, m_ffn2_w_gate, m_ffn2_w_up, m_ffn2_w_down, m_ln_ffn2_g, m_ln_ffn2_b, v_ffn1_w_gate, v_ffn1_w_up, v_ffn1_w_down, v_ln_ffn1_g, v_ln_ffn1_b, v_w_in, v_gdn_conv_w, v_gdn_a_log, v_gdn_dt_bias, v_gdn_norm_g, v_fox_b_f, v_conf_dw_w, v_conf_dw_b, v_conf_norm_g, v_conf_norm_b, v_w_out, v_ln_mix_g, v_ln_mix_b, v_mem_w_q, v_mem_w_kv, v_mem_w_o, v_ln_mem_g, v_ln_mem_b, v_ffn2_w_gate, v_ffn2_w_up, v_ffn2_w_down, v_ln_ffn2_g, v_ln_ffn2_b):
    a = locals()
    P = {k: a[k] for k in WEIGHTS}
    M = {k: a["m_" + k] for k in WEIGHTS}
    V = {k: a["v_" + k] for k in WEIGHTS}
    return _step(P, M, V, x, mem, loss_target)
```
